```python
import jax, jax.numpy as jnp
from jax import lax
import numpy as np

D_MODEL = 1024
BATCH = 32
SEQ = 2048
DEPTH = 2

N_Q_HEADS = 8
N_KV_HEADS = 2
HEAD_DIM = 64
WINDOW = 128
ATTN_BLOCK = 128
ATTN_WIDTH = N_Q_HEADS * HEAD_DIM
KV_WIDTH = N_KV_HEADS * HEAD_DIM
CONV_WIDTH = D_MODEL - ATTN_WIDTH
CONV_KERNEL = 31
IN0_WIDTH = ATTN_WIDTH + 2 * KV_WIDTH + 2 * CONV_WIDTH
POOL_WINDOWS = (2, 4, 8, 16)
POOL_WIDTH = D_MODEL // 2
POOL_GROUP = POOL_WIDTH // len(POOL_WINDOWS)
SGU_WIDTH = D_MODEL - POOL_WIDTH
SGU_HEADS = 4
SGU_HEAD_DIM = SGU_WIDTH // SGU_HEADS
SGU_CHUNK = 128
IN1_WIDTH = POOL_WIDTH + 2 * SGU_WIDTH
D_FF = -(-(8 * D_MODEL) // (3 * 256)) * 256
N_EVEN = (DEPTH + 1) // 2
N_ODD = DEPTH // 2
EPS = 1e-5

kernel_name = "hybrid_swa_conformer_pool_sgu"


def rms_norm(x, g):
    xf = x.astype(jnp.float32)
    y = xf * lax.rsqrt(jnp.mean(xf * xf, axis=-1, keepdims=True) + EPS)
    return (y * g.astype(jnp.float32)).astype(x.dtype)


def layer_norm(x, g, b):
    xf = x.astype(jnp.float32)
    mu = jnp.mean(xf, axis=-1, keepdims=True)
    var = jnp.mean(jnp.square(xf - mu), axis=-1, keepdims=True)
    y = (xf - mu) * lax.rsqrt(var + EPS)
    return (y * g.astype(jnp.float32) + b.astype(jnp.float32)).astype(x.dtype)


def sliding_window_attention(q, k, v, sinks):
    B, S = q.shape[0], q.shape[1]
    nb = S // ATTN_BLOCK
    G = N_Q_HEADS // N_KV_HEADS
    qb = q.reshape(B, nb, ATTN_BLOCK, N_KV_HEADS, G, HEAD_DIM)
    kb = k.reshape(B, nb, ATTN_BLOCK, N_KV_HEADS, HEAD_DIM)
    vb = v.reshape(B, nb, ATTN_BLOCK, N_KV_HEADS, HEAD_DIM)

    def with_prev(t):
        prev = jnp.pad(t, ((0, 0), (1, 0), (0, 0), (0, 0), (0, 0)))[:, :-1]
        return jnp.concatenate([prev, t], axis=2)

    kw, vw = with_prev(kb), with_prev(vb)
    logits = jnp.einsum('bnqkgd,bnskd->bnkgqs', qb, kw).astype(jnp.float32) * (HEAD_DIM ** -0.5)
    qi = jnp.arange(ATTN_BLOCK)[:, None]
    r = jnp.arange(2 * ATTN_BLOCK)[None, :]
    dist = qi + ATTN_BLOCK - r
    band = (dist >= 0) & (dist < WINDOW)
    key_pos = jnp.arange(nb)[:, None, None] * ATTN_BLOCK + r[None] - ATTN_BLOCK
    mask = band[None] & (key_pos >= 0)
    logits = jnp.where(mask[None, :, None, None], logits, -jnp.inf)
    sink = sinks.astype(jnp.float32).reshape(1, 1, N_KV_HEADS, G, 1, 1)
    m = jnp.maximum(jnp.max(logits, axis=-1, keepdims=True), sink)
    p = jnp.exp(logits - m)
    probs = p / (jnp.sum(p, axis=-1, keepdims=True) + jnp.exp(sink - m))
    out = jnp.einsum('bnkgqs,bnskd->bnqkgd', probs.astype(v.dtype), vw)
    return out.reshape(B, S, ATTN_WIDTH)


def conformer_conv(c, conv_w, conv_b, ln_g, ln_b):
    a, gate = jnp.split(c, 2, axis=-1)
    h = a * jax.nn.sigmoid(gate)
    h = lax.conv_general_dilated(
        h, conv_w[:, None, :].astype(h.dtype), window_strides=(1,),
        padding=[(CONV_KERNEL - 1, 0)],
        dimension_numbers=('NWC', 'WIO', 'NWC'),
        feature_group_count=CONV_WIDTH) + conv_b
    h = layer_norm(h, ln_g, ln_b)
    return jax.nn.silu(h)


def multiscale_pool(z, w_pool, scale):
    S = z.shape[1]
    zf = z.astype(jnp.float32)
    cs = jnp.cumsum(zf, axis=1)
    t = jnp.arange(S)
    outs = []
    for g, w in enumerate(POOL_WINDOWS):
        lo, hi = g * POOL_GROUP, (g + 1) * POOL_GROUP
        c = cs[..., lo:hi]
        prev = jnp.pad(c, ((0, 0), (w, 0), (0, 0)))[:, :S]
        cnt = jnp.minimum(t + 1, w).astype(jnp.float32)[:, None]
        pooled = (c - prev) / cnt - zf[..., lo:hi]
        outs.append(jnp.einsum('bsc,cd->bsd', pooled.astype(z.dtype), w_pool[g]))
    return jnp.concatenate(outs, axis=-1) * scale


def chunked_spatial_gating(z, ln_g, ln_b, w_s, b_s):
    B, S = z.shape[0], z.shape[1]
    u, v = jnp.split(jax.nn.gelu(z), 2, axis=-1)
    v = layer_norm(v, ln_g, ln_b)
    nc = S // SGU_CHUNK
    vc = v.reshape(B, nc, SGU_CHUNK, SGU_HEADS, SGU_HEAD_DIM)
    causal = jnp.tril(jnp.ones((SGU_CHUNK, SGU_CHUNK), dtype=bool))
    w = jnp.where(causal[None], w_s, jnp.zeros_like(w_s))
    mixed = jnp.einsum('gts,bcsgh->bctgh', w, vc) + b_s.T[None, None, :, :, None]
    return u * mixed.reshape(B, S, SGU_WIDTH)


def swiglu(h, w_gate, w_up, w_down):
    return (jax.nn.silu(h @ w_gate) * (h @ w_up)) @ w_down


def _fwd_setup_inputs(seed: int = 0) -> dict:
    key = jax.random.key(seed)
    ks = jax.random.split(key, 24)
    f32 = jnp.float32

    def nrm(k, shape, s):
        return jax.random.normal(k, shape, f32) * s

    def gain(k, shape):
        return 1.0 + 0.02 * jax.random.normal(k, shape, f32)

    return {
        'x': jax.random.normal(ks[0], (BATCH, SEQ, D_MODEL), f32),
        'mix_norm': gain(ks[1], (DEPTH, D_MODEL)),
        'a_w_in': nrm(ks[2], (N_EVEN, D_MODEL, IN0_WIDTH), D_MODEL ** -0.5),
        'a_b_in': nrm(ks[3], (N_EVEN, IN0_WIDTH), 0.02),
        'a_sinks': nrm(ks[4], (N_EVEN, N_Q_HEADS), 1.0),
        'a_conv_w': nrm(ks[5], (N_EVEN, CONV_KERNEL, CONV_WIDTH), CONV_KERNEL ** -0.5),
        'a_conv_b': nrm(ks[6], (N_EVEN, CONV_WIDTH), 0.02),
        'a_cln_g': gain(ks[7], (N_EVEN, CONV_WIDTH)),
        'a_cln_b': nrm(ks[8], (N_EVEN, CONV_WIDTH), 0.02),
        'a_w_out': nrm(ks[9], (N_EVEN, D_MODEL, D_MODEL), D_MODEL ** -0.5),
        'c_w_in': nrm(ks[10], (N_ODD, D_MODEL, IN1_WIDTH), D_MODEL ** -0.5),
        'c_w_pool': nrm(ks[11], (N_ODD, len(POOL_WINDOWS), POOL_GROUP, POOL_GROUP), POOL_GROUP ** -0.5),
        'c_pool_scale': gain(ks[12], (N_ODD, POOL_WIDTH)),
        'c_sln_g': gain(ks[13], (N_ODD, SGU_WIDTH)),
        'c_sln_b': nrm(ks[14], (N_ODD, SGU_WIDTH), 0.02),
        'c_w_s': nrm(ks[15], (N_ODD, SGU_HEADS, SGU_CHUNK, SGU_CHUNK), SGU_CHUNK ** -0.5),
        'c_b_s': gain(ks[16], (N_ODD, SGU_HEADS, SGU_CHUNK)),
        'c_w_out': nrm(ks[17], (N_ODD, D_MODEL, D_MODEL), D_MODEL ** -0.5),
        'ffn_norm': gain(ks[18], (DEPTH, D_MODEL)),
        'ffn_w_gate': nrm(ks[19], (DEPTH, D_MODEL, D_FF), D_MODEL ** -0.5),
        'ffn_w_up': nrm(ks[20], (DEPTH, D_MODEL, D_FF), D_MODEL ** -0.5),
        'ffn_w_down': nrm(ks[21], (DEPTH, D_FF, D_MODEL), D_FF ** -0.5),
        'final_norm': gain(ks[22], (D_MODEL,)),
    }


def _fwd_reference(x, mix_norm, a_w_in, a_b_in, a_sinks, a_conv_w, a_conv_b, a_cln_g, a_cln_b, a_w_out,
              c_w_in, c_w_pool, c_pool_scale, c_sln_g, c_sln_b, c_w_s, c_b_s, c_w_out,
              ffn_norm, ffn_w_gate, ffn_w_up, ffn_w_down, final_norm):
    B, S = x.shape[0], x.shape[1]
    h = x
    for i in range(DEPTH):
        j = i // 2
        hn = rms_norm(h, mix_norm[i])
        if i % 2 == 0:
            z = hn @ a_w_in[j] + a_b_in[j]
            q = z[..., :ATTN_WIDTH].reshape(B, S, N_Q_HEADS, HEAD_DIM)
            k = z[..., ATTN_WIDTH:ATTN_WIDTH + KV_WIDTH].reshape(B, S, N_KV_HEADS, HEAD_DIM)
            v = z[..., ATTN_WIDTH + KV_WIDTH:ATTN_WIDTH + 2 * KV_WIDTH].reshape(B, S, N_KV_HEADS, HEAD_DIM)
            c = z[..., ATTN_WIDTH + 2 * KV_WIDTH:]
            attn = sliding_window_attention(q, k, v, a_sinks[j])
            conv = conformer_conv(c, a_conv_w[j], a_conv_b[j], a_cln_g[j], a_cln_b[j])
            h = h + jnp.concatenate([attn, conv], axis=-1) @ a_w_out[j]
        else:
            z = hn @ c_w_in[j]
            pool = multiscale_pool(z[..., :POOL_WIDTH], c_w_pool[j], c_pool_scale[j])
            sgu = chunked_spatial_gating(z[..., POOL_WIDTH:], c_sln_g[j], c_sln_b[j], c_w_s[j], c_b_s[j])
            h = h + jnp.concatenate([pool, sgu], axis=-1) @ c_w_out[j]
        h = h + swiglu(rms_norm(h, ffn_norm[i]), ffn_w_gate[i], ffn_w_up[i], ffn_w_down[i])
    return rms_norm(h, final_norm)


import jax as _jax
import jax.numpy as _jnp

TWIN_FORMAT = 'train_step'
FWD_PARAMS = ['x', 'mix_norm', 'a_w_in', 'a_b_in', 'a_sinks', 'a_conv_w', 'a_conv_b', 'a_cln_g', 'a_cln_b', 'a_w_out', 'c_w_in', 'c_w_pool', 'c_pool_scale', 'c_sln_g', 'c_sln_b', 'c_w_s', 'c_b_s', 'c_w_out', 'ffn_norm', 'ffn_w_gate', 'ffn_w_up', 'ffn_w_down', 'final_norm']
TWIN_WEIGHTS = ['mix_norm', 'a_w_in', 'a_b_in', 'a_sinks', 'a_conv_w', 'a_conv_b', 'a_cln_g', 'a_cln_b', 'a_w_out', 'c_w_in', 'c_w_pool', 'c_pool_scale', 'c_sln_g', 'c_sln_b', 'c_w_s', 'c_b_s', 'c_w_out', 'ffn_norm', 'ffn_w_gate', 'ffn_w_up', 'ffn_w_down', 'final_norm']
TWIN_DIFF_INPUT = 'x'
TWIN_INPUTS = ['x', 'mix_norm', 'a_w_in', 'a_b_in', 'a_sinks', 'a_conv_w', 'a_conv_b', 'a_cln_g', 'a_cln_b', 'a_w_out', 'c_w_in', 'c_w_pool', 'c_pool_scale', 'c_sln_g', 'c_sln_b', 'c_w_s', 'c_b_s', 'c_w_out', 'ffn_norm', 'ffn_w_gate', 'ffn_w_up', 'ffn_w_down', 'final_norm', 'loss_target', 'm_mix_norm', 'm_a_w_in', 'm_a_b_in', 'm_a_sinks', 'm_a_conv_w', 'm_a_conv_b', 'm_a_cln_g', 'm_a_cln_b', 'm_a_w_out', 'm_c_w_in', 'm_c_w_pool', 'm_c_pool_scale', 'm_c_sln_g', 'm_c_sln_b', 'm_c_w_s', 'm_c_b_s', 'm_c_w_out', 'm_ffn_norm', 'm_ffn_w_gate', 'm_ffn_w_up', 'm_ffn_w_down', 'm_final_norm', 'v_mix_norm', 'v_a_w_in', 'v_a_b_in', 'v_a_sinks', 'v_a_conv_w', 'v_a_conv_b', 'v_a_cln_g', 'v_a_cln_b', 'v_a_w_out', 'v_c_w_in', 'v_c_w_pool', 'v_c_pool_scale', 'v_c_sln_g', 'v_c_sln_b', 'v_c_w_s', 'v_c_b_s', 'v_c_w_out', 'v_ffn_norm', 'v_ffn_w_gate', 'v_ffn_w_up', 'v_ffn_w_down', 'v_final_norm']
TWIN_OUTPUTS = ['loss', 'grad_x', 'grad_mix_norm', 'grad_a_w_in', 'grad_a_b_in', 'grad_a_sinks', 'grad_a_conv_w', 'grad_a_conv_b', 'grad_a_cln_g', 'grad_a_cln_b', 'grad_a_w_out', 'grad_c_w_in', 'grad_c_w_pool', 'grad_c_pool_scale', 'grad_c_sln_g', 'grad_c_sln_b', 'grad_c_w_s', 'grad_c_b_s', 'grad_c_w_out', 'grad_ffn_norm', 'grad_ffn_w_gate', 'grad_ffn_w_up', 'grad_ffn_w_down', 'grad_final_norm', 'delta_mix_norm', 'delta_a_w_in', 'delta_a_b_in', 'delta_a_sinks', 'delta_a_conv_w', 'delta_a_conv_b', 'delta_a_cln_g', 'delta_a_cln_b', 'delta_a_w_out', 'delta_c_w_in', 'delta_c_w_pool', 'delta_c_pool_scale', 'delta_c_sln_g', 'delta_c_sln_b', 'delta_c_w_s', 'delta_c_b_s', 'delta_c_w_out', 'delta_ffn_norm', 'delta_ffn_w_gate', 'delta_ffn_w_up', 'delta_ffn_w_down', 'delta_final_norm', 'new_m_mix_norm', 'new_m_a_w_in', 'new_m_a_b_in', 'new_m_a_sinks', 'new_m_a_conv_w', 'new_m_a_conv_b', 'new_m_a_cln_g', 'new_m_a_cln_b', 'new_m_a_w_out', 'new_m_c_w_in', 'new_m_c_w_pool', 'new_m_c_pool_scale', 'new_m_c_sln_g', 'new_m_c_sln_b', 'new_m_c_w_s', 'new_m_c_b_s', 'new_m_c_w_out', 'new_m_ffn_norm', 'new_m_ffn_w_gate', 'new_m_ffn_w_up', 'new_m_ffn_w_down', 'new_m_final_norm', 'new_v_mix_norm', 'new_v_a_w_in', 'new_v_a_b_in', 'new_v_a_sinks', 'new_v_a_conv_w', 'new_v_a_conv_b', 'new_v_a_cln_g', 'new_v_a_cln_b', 'new_v_a_w_out', 'new_v_c_w_in', 'new_v_c_w_pool', 'new_v_c_pool_scale', 'new_v_c_sln_g', 'new_v_c_sln_b', 'new_v_c_w_s', 'new_v_c_b_s', 'new_v_c_w_out', 'new_v_ffn_norm', 'new_v_ffn_w_gate', 'new_v_ffn_w_up', 'new_v_ffn_w_down', 'new_v_final_norm']
TWIN_LEAF_KINDS = {'loss': 'loss', 'grad_x': 'grad_x', 'grad_mix_norm': 'grad_w', 'grad_a_w_in': 'grad_w', 'grad_a_b_in': 'grad_w', 'grad_a_sinks': 'grad_w', 'grad_a_conv_w': 'grad_w', 'grad_a_conv_b': 'grad_w', 'grad_a_cln_g': 'grad_w', 'grad_a_cln_b': 'grad_w', 'grad_a_w_out': 'grad_w', 'grad_c_w_in': 'grad_w', 'grad_c_w_pool': 'grad_w', 'grad_c_pool_scale': 'grad_w', 'grad_c_sln_g': 'grad_w', 'grad_c_sln_b': 'grad_w', 'grad_c_w_s': 'grad_w', 'grad_c_b_s': 'grad_w', 'grad_c_w_out': 'grad_w', 'grad_ffn_norm': 'grad_w', 'grad_ffn_w_gate': 'grad_w', 'grad_ffn_w_up': 'grad_w', 'grad_ffn_w_down': 'grad_w', 'grad_final_norm': 'grad_w', 'delta_mix_norm': 'delta_w', 'delta_a_w_in': 'delta_w', 'delta_a_b_in': 'delta_w', 'delta_a_sinks': 'delta_w', 'delta_a_conv_w': 'delta_w', 'delta_a_conv_b': 'delta_w', 'delta_a_cln_g': 'delta_w', 'delta_a_cln_b': 'delta_w', 'delta_a_w_out': 'delta_w', 'delta_c_w_in': 'delta_w', 'delta_c_w_pool': 'delta_w', 'delta_c_pool_scale': 'delta_w', 'delta_c_sln_g': 'delta_w', 'delta_c_sln_b': 'delta_w', 'delta_c_w_s': 'delta_w', 'delta_c_b_s': 'delta_w', 'delta_c_w_out': 'delta_w', 'delta_ffn_norm': 'delta_w', 'delta_ffn_w_gate': 'delta_w', 'delta_ffn_w_up': 'delta_w', 'delta_ffn_w_down': 'delta_w', 'delta_final_norm': 'delta_w', 'new_m_mix_norm': 'new_m', 'new_m_a_w_in': 'new_m', 'new_m_a_b_in': 'new_m', 'new_m_a_sinks': 'new_m', 'new_m_a_conv_w': 'new_m', 'new_m_a_conv_b': 'new_m', 'new_m_a_cln_g': 'new_m', 'new_m_a_cln_b': 'new_m', 'new_m_a_w_out': 'new_m', 'new_m_c_w_in': 'new_m', 'new_m_c_w_pool': 'new_m', 'new_m_c_pool_scale': 'new_m', 'new_m_c_sln_g': 'new_m', 'new_m_c_sln_b': 'new_m', 'new_m_c_w_s': 'new_m', 'new_m_c_b_s': 'new_m', 'new_m_c_w_out': 'new_m', 'new_m_ffn_norm': 'new_m', 'new_m_ffn_w_gate': 'new_m', 'new_m_ffn_w_up': 'new_m', 'new_m_ffn_w_down': 'new_m', 'new_m_final_norm': 'new_m', 'new_v_mix_norm': 'new_v', 'new_v_a_w_in': 'new_v', 'new_v_a_b_in': 'new_v', 'new_v_a_sinks': 'new_v', 'new_v_a_conv_w': 'new_v', 'new_v_a_conv_b': 'new_v', 'new_v_a_cln_g': 'new_v', 'new_v_a_cln_b': 'new_v', 'new_v_a_w_out': 'new_v', 'new_v_c_w_in': 'new_v', 'new_v_c_w_pool': 'new_v', 'new_v_c_pool_scale': 'new_v', 'new_v_c_sln_g': 'new_v', 'new_v_c_sln_b': 'new_v', 'new_v_c_w_s': 'new_v', 'new_v_c_b_s': 'new_v', 'new_v_c_w_out': 'new_v', 'new_v_ffn_norm': 'new_v', 'new_v_ffn_w_gate': 'new_v', 'new_v_ffn_w_up': 'new_v', 'new_v_ffn_w_down': 'new_v', 'new_v_final_norm': 'new_v'}


def _forward(args):
    return _fwd_reference(*[args[k] for k in FWD_PARAMS])


def _output_shape():
    out = _jax.eval_shape(lambda: _forward(_fwd_setup_inputs(0)))
    return out.shape, out.dtype

N_MICROBATCH = 1
ADAM_LR = 0.001
ADAM_B1 = 0.9
ADAM_B2 = 0.999
ADAM_EPS = 1e-08
ADAM_WD = 0.01
ADAM_STEP = 10
PER_EXAMPLE_BATCH_AXIS = {'x': 0, 'loss_target': 0}
SHARED_INPUTS = []
_WEIGHT_DTYPES = {'mix_norm': _jnp.float32, 'a_w_in': _jnp.float32, 'a_b_in': _jnp.float32, 'a_sinks': _jnp.float32, 'a_conv_w': _jnp.float32, 'a_conv_b': _jnp.float32, 'a_cln_g': _jnp.float32, 'a_cln_b': _jnp.float32, 'a_w_out': _jnp.float32, 'c_w_in': _jnp.float32, 'c_w_pool': _jnp.float32, 'c_pool_scale': _jnp.float32, 'c_sln_g': _jnp.float32, 'c_sln_b': _jnp.float32, 'c_w_s': _jnp.float32, 'c_b_s': _jnp.float32, 'c_w_out': _jnp.float32, 'ffn_norm': _jnp.float32, 'ffn_w_gate': _jnp.float32, 'ffn_w_up': _jnp.float32, 'ffn_w_down': _jnp.float32, 'final_norm': _jnp.float32}
MOMENT_SCALE = {'mix_norm': 1.544062e-01, 'a_w_in': 1.122750e-01, 'a_b_in': 2.412249e-01, 'a_sinks': 4.877466e-02, 'a_conv_w': 1.787235e-01, 'a_conv_b': 3.670412e-01, 'a_cln_g': 2.319417e-01, 'a_cln_b': 2.183773e-01, 'a_w_out': 1.266714e-01, 'c_w_in': 1.354557e-01, 'c_w_pool': 1.559776e-01, 'c_pool_scale': 1.586128e-01, 'c_sln_g': 9.048025e-02, 'c_sln_b': 7.976698e-02, 'c_w_s': 8.212031e-02, 'c_b_s': 1.165150e-01, 'c_w_out': 1.521159e-01, 'ffn_norm': 1.709047e-01, 'ffn_w_gate': 7.163569e-02, 'ffn_w_up': 6.959382e-02, 'ffn_w_down': 1.151749e-01, 'final_norm': 6.395599e+01}


def _to_microbatches(a, axis):
    t = _jnp.moveaxis(a, axis, 0)
    t = t.reshape((N_MICROBATCH, t.shape[0] // N_MICROBATCH) + t.shape[1:])
    return _jnp.moveaxis(t, 1, axis + 1)


def setup_inputs(seed: int = 0) -> dict:
    inp = _fwd_setup_inputs(seed)
    key = _jax.random.fold_in(_jax.random.key(seed), 7919)
    shape, _ = _output_shape()
    out = dict(inp)
    out["loss_target"] = _jax.random.normal(_jax.random.fold_in(key, 0), shape, _jnp.float32)
    for i, name in enumerate(TWIN_WEIGHTS):
        w = inp[name].astype(_jnp.float32)
        if MOMENT_SCALE is None:
            s = _jnp.sqrt(_jnp.mean(_jnp.square(w)) + 1e-30)
        else:
            s = MOMENT_SCALE[name]
        km, kv = _jax.random.split(_jax.random.fold_in(key, i + 1))
        out[name] = w
        out["m_" + name] = s * _jax.random.normal(km, w.shape, _jnp.float32)
        out["v_" + name] = (s * s) * _jax.random.uniform(kv, w.shape, _jnp.float32, 0.5, 1.5)
    if N_MICROBATCH > 1:
        for name, axis in PER_EXAMPLE_BATCH_AXIS.items():
            out[name] = _to_microbatches(out[name], axis)
    return {'x': out['x'], 'mix_norm': out['mix_norm'], 'a_w_in': out['a_w_in'], 'a_b_in': out['a_b_in'], 'a_sinks': out['a_sinks'], 'a_conv_w': out['a_conv_w'], 'a_conv_b': out['a_conv_b'], 'a_cln_g': out['a_cln_g'], 'a_cln_b': out['a_cln_b'], 'a_w_out': out['a_w_out'], 'c_w_in': out['c_w_in'], 'c_w_pool': out['c_w_pool'], 'c_pool_scale': out['c_pool_scale'], 'c_sln_g': out['c_sln_g'], 'c_sln_b': out['c_sln_b'], 'c_w_s': out['c_w_s'], 'c_b_s': out['c_b_s'], 'c_w_out': out['c_w_out'], 'ffn_norm': out['ffn_norm'], 'ffn_w_gate': out['ffn_w_gate'], 'ffn_w_up': out['ffn_w_up'], 'ffn_w_down': out['ffn_w_down'], 'final_norm': out['final_norm'], 'loss_target': out['loss_target'], 'm_mix_norm': out['m_mix_norm'], 'm_a_w_in': out['m_a_w_in'], 'm_a_b_in': out['m_a_b_in'], 'm_a_sinks': out['m_a_sinks'], 'm_a_conv_w': out['m_a_conv_w'], 'm_a_conv_b': out['m_a_conv_b'], 'm_a_cln_g': out['m_a_cln_g'], 'm_a_cln_b': out['m_a_cln_b'], 'm_a_w_out': out['m_a_w_out'], 'm_c_w_in': out['m_c_w_in'], 'm_c_w_pool': out['m_c_w_pool'], 'm_c_pool_scale': out['m_c_pool_scale'], 'm_c_sln_g': out['m_c_sln_g'], 'm_c_sln_b': out['m_c_sln_b'], 'm_c_w_s': out['m_c_w_s'], 'm_c_b_s': out['m_c_b_s'], 'm_c_w_out': out['m_c_w_out'], 'm_ffn_norm': out['m_ffn_norm'], 'm_ffn_w_gate': out['m_ffn_w_gate'], 'm_ffn_w_up': out['m_ffn_w_up'], 'm_ffn_w_down': out['m_ffn_w_down'], 'm_final_norm': out['m_final_norm'], 'v_mix_norm': out['v_mix_norm'], 'v_a_w_in': out['v_a_w_in'], 'v_a_b_in': out['v_a_b_in'], 'v_a_sinks': out['v_a_sinks'], 'v_a_conv_w': out['v_a_conv_w'], 'v_a_conv_b': out['v_a_conv_b'], 'v_a_cln_g': out['v_a_cln_g'], 'v_a_cln_b': out['v_a_cln_b'], 'v_a_w_out': out['v_a_w_out'], 'v_c_w_in': out['v_c_w_in'], 'v_c_w_pool': out['v_c_w_pool'], 'v_c_pool_scale': out['v_c_pool_scale'], 'v_c_sln_g': out['v_c_sln_g'], 'v_c_sln_b': out['v_c_sln_b'], 'v_c_w_s': out['v_c_w_s'], 'v_c_b_s': out['v_c_b_s'], 'v_c_w_out': out['v_c_w_out'], 'v_ffn_norm': out['v_ffn_norm'], 'v_ffn_w_gate': out['v_ffn_w_gate'], 'v_ffn_w_up': out['v_ffn_w_up'], 'v_ffn_w_down': out['v_ffn_w_down'], 'v_final_norm': out['v_final_norm']}


def _loss(weights, diff, rest, loss_target):
    with _jax.named_scope("forward"):
        args = {**rest, TWIN_DIFF_INPUT: diff, **{k: w.astype(_WEIGHT_DTYPES[k]) for k, w in weights.items()}}
        y = _forward(args)
    with _jax.named_scope("loss_head"):
        err = _jnp.square(y.astype(_jnp.float32) - loss_target)
        return 0.5 * _jnp.sum(_jnp.mean(err, axis=-1)) if err.ndim else 0.5 * err


def _adamw(w, g, m, v):
    m = ADAM_B1 * m + (1.0 - ADAM_B1) * g
    v = ADAM_B2 * v + (1.0 - ADAM_B2) * _jnp.square(g)
    m_hat = m / (1.0 - ADAM_B1 ** ADAM_STEP)
    v_hat = v / (1.0 - ADAM_B2 ** ADAM_STEP)
    delta = -ADAM_LR * (m_hat / (_jnp.sqrt(v_hat) + ADAM_EPS) + ADAM_WD * w)
    return delta, m, v


def reference(x, mix_norm, a_w_in, a_b_in, a_sinks, a_conv_w, a_conv_b, a_cln_g, a_cln_b, a_w_out, c_w_in, c_w_pool, c_pool_scale, c_sln_g, c_sln_b, c_w_s, c_b_s, c_w_out, ffn_norm, ffn_w_gate, ffn_w_up, ffn_w_down, final_norm, loss_target, m_mix_norm, m_a_w_in, m_a_b_in, m_a_sinks, m_a_conv_w, m_a_conv_b, m_a_cln_g, m_a_cln_b, m_a_w_out, m_c_w_in, m_c_w_pool, m_c_pool_scale, m_c_sln_g, m_c_sln_b, m_c_w_s, m_c_b_s, m_c_w_out, m_ffn_norm, m_ffn_w_gate, m_ffn_w_up, m_ffn_w_down, m_final_norm, v_mix_norm, v_a_w_in, v_a_b_in, v_a_sinks, v_a_conv_w, v_a_conv_b, v_a_cln_g, v_a_cln_b, v_a_w_out, v_c_w_in, v_c_w_pool, v_c_pool_scale, v_c_sln_g, v_c_sln_b, v_c_w_s, v_c_b_s, v_c_w_out, v_ffn_norm, v_ffn_w_gate, v_ffn_w_up, v_ffn_w_down, v_final_norm):
    given = dict(x=x, mix_norm=mix_norm, a_w_in=a_w_in, a_b_in=a_b_in, a_sinks=a_sinks, a_conv_w=a_conv_w, a_conv_b=a_conv_b, a_cln_g=a_cln_g, a_cln_b=a_cln_b, a_w_out=a_w_out, c_w_in=c_w_in, c_w_pool=c_w_pool, c_pool_scale=c_pool_scale, c_sln_g=c_sln_g, c_sln_b=c_sln_b, c_w_s=c_w_s, c_b_s=c_b_s, c_w_out=c_w_out, ffn_norm=ffn_norm, ffn_w_gate=ffn_w_gate, ffn_w_up=ffn_w_up, ffn_w_down=ffn_w_down, final_norm=final_norm, loss_target=loss_target, m_mix_norm=m_mix_norm, m_a_w_in=m_a_w_in, m_a_b_in=m_a_b_in, m_a_sinks=m_a_sinks, m_a_conv_w=m_a_conv_w, m_a_conv_b=m_a_conv_b, m_a_cln_g=m_a_cln_g, m_a_cln_b=m_a_cln_b, m_a_w_out=m_a_w_out, m_c_w_in=m_c_w_in, m_c_w_pool=m_c_w_pool, m_c_pool_scale=m_c_pool_scale, m_c_sln_g=m_c_sln_g, m_c_sln_b=m_c_sln_b, m_c_w_s=m_c_w_s, m_c_b_s=m_c_b_s, m_c_w_out=m_c_w_out, m_ffn_norm=m_ffn_norm, m_ffn_w_gate=m_ffn_w_gate, m_ffn_w_up=m_ffn_w_up, m_ffn_w_down=m_ffn_w_down, m_final_norm=m_final_norm, v_mix_norm=v_mix_norm, v_a_w_in=v_a_w_in, v_a_b_in=v_a_b_in, v_a_sinks=v_a_sinks, v_a_conv_w=v_a_conv_w, v_a_conv_b=v_a_conv_b, v_a_cln_g=v_a_cln_g, v_a_cln_b=v_a_cln_b, v_a_w_out=v_a_w_out, v_c_w_in=v_c_w_in, v_c_w_pool=v_c_w_pool, v_c_pool_scale=v_c_pool_scale, v_c_sln_g=v_c_sln_g, v_c_sln_b=v_c_sln_b, v_c_w_s=v_c_w_s, v_c_b_s=v_c_b_s, v_c_w_out=v_c_w_out, v_ffn_norm=v_ffn_norm, v_ffn_w_gate=v_ffn_w_gate, v_ffn_w_up=v_ffn_w_up, v_ffn_w_down=v_ffn_w_down, v_final_norm=v_final_norm)
    weights = {n: given[n] for n in TWIN_WEIGHTS}
    shared = {n: given[n] for n in SHARED_INPUTS}
    per_example = {n: given[n] for n in ['x']}
    grad_fn = _jax.value_and_grad(_loss, argnums=(0, 1))

    def one_microbatch(ex, loss_target):
        ex = dict(ex)
        diff = ex.pop(TWIN_DIFF_INPUT)
        return grad_fn(weights, diff, {**shared, **ex}, loss_target)

    if N_MICROBATCH == 1:
        loss, (grad_w, grad_x) = one_microbatch(per_example, given["loss_target"])
    else:
        def body(carry, xs):
            loss_sum, grad_sum = carry
            l_k, (gw_k, gx_k) = one_microbatch(xs[0], xs[1])
            with _jax.named_scope("update"):
                return (loss_sum + l_k, _jax.tree.map(_jnp.add, grad_sum, gw_k)), gx_k

        init = (_jnp.zeros((), _jnp.float32), _jax.tree.map(_jnp.zeros_like, weights))
        (loss, grad_w), grad_x = _jax.lax.scan(body, init, (per_example, given["loss_target"]))
    with _jax.named_scope("update"):
        delta_w, new_m, new_v = {}, {}, {}
        for n in TWIN_WEIGHTS:
            delta_w[n], new_m[n], new_v[n] = _adamw(weights[n], grad_w[n], given["m_" + n], given["v_" + n])
    return (loss, grad_x, *[grad_w[n] for n in TWIN_WEIGHTS], *[delta_w[n] for n in TWIN_WEIGHTS],
            *[new_m[n] for n in TWIN_WEIGHTS], *[new_v[n] for n in TWIN_WEIGHTS])
```

```python
import functools

import jax
import jax.numpy as jnp
import numpy as np
from jax import lax
from jax.experimental import pallas as pl
from jax.experimental.pallas import tpu as pltpu

F32 = jnp.float32
BF16 = jnp.bfloat16
MESH = pl.DeviceIdType.MESH

D = 1024
N_DEV = 8
EPS = 1e-5
HEAD_PAIRS = 4
ATT_BLK = 128
CONV_K = 31
HALO = 32
D_FF = 2816
FF_TILE = 256
IN0 = 1792
IN1 = 1536
POOL_WINDOWS = (2, 4, 8, 16)
SGU_CHUNK = 128
GELU_C = 0.7978845608028654
GELU_A = 0.044715
ADAM_LR, ADAM_B1, ADAM_B2, ADAM_EPS, ADAM_WD, ADAM_STEP = 0.001, 0.9, 0.999, 1e-08, 0.01, 10
VMEM_LIMIT = 56 << 20

W_ROWS = (("a_in", 224), ("a_out", 128), ("c_in", 192), ("c_out", 128),
          ("g0", 352), ("u0", 352), ("d0", 352), ("g1", 352), ("u1", 352), ("d1", 352))
BIG_ROWS = sum(r for _, r in W_ROWS)
SMALL_SHARD = 31 * 64 + 3 * 64
W_MISC_ROWS = 16
G_MISC_ROWS = 32
G_SMALL_ROWS = 8
REP_ROWS = G_MISC_ROWS - G_SMALL_ROWS
G_ROWS = BIG_ROWS + G_MISC_ROWS
REP_NAMES = ("mix_norm", "a_b_in", "a_sinks", "a_conv_b", "a_cln_g", "a_cln_b", "c_w_pool", "c_w_s", "c_b_s",
             "ffn_norm", "final_norm")
Q_PERM = (0, 4, 1, 5, 2, 6, 3, 7)


def _params(*sem):
    return pltpu.CompilerParams(dimension_semantics=sem, vmem_limit_bytes=VMEM_LIMIT)


def _nn(a, b):
    return jnp.dot(a, b, preferred_element_type=F32)


def _nt(a, b):
    return lax.dot_general(a, b, (((1,), (1,)), ((), ())), preferred_element_type=F32)


def _tn(a, b):
    return lax.dot_general(a, b, (((0,), (0,)), ((), ())), preferred_element_type=F32)


def _tile(n, want=512):
    t = min(want, n)
    assert n % t == 0, (n, t)
    return t


def _seq_tile(s):
    return 512 if s >= 1024 else s // 2


def _rms(x, g):
    r = lax.rsqrt(jnp.mean(x * x, axis=-1, keepdims=True) + EPS)
    return x * r * g, r


def _rms_bwd(x, g, d_y):
    r = lax.rsqrt(jnp.mean(x * x, axis=-1, keepdims=True) + EPS)
    xr = x * r
    u = d_y * g
    d_x = r * (u - xr * jnp.mean(u * xr, axis=-1, keepdims=True))
    return d_x, jnp.sum(d_y * xr, axis=0, keepdims=True)


def _ln(y, g, b):
    mu = jnp.mean(y, axis=-1, keepdims=True)
    yc = y - mu
    rstd = lax.rsqrt(jnp.mean(yc * yc, axis=-1, keepdims=True) + EPS)
    xhat = yc * rstd
    return xhat * g + b, xhat, rstd


def _ln_bwd(d_o, xhat, rstd, g):
    dxh = d_o * g
    return rstd * (dxh - jnp.mean(dxh, axis=-1, keepdims=True) - xhat * jnp.mean(dxh * xhat, axis=-1, keepdims=True))


def _gelu(x):
    th = jnp.tanh(GELU_C * (x + GELU_A * x * x * x))
    return 0.5 * x * (1.0 + th), th


def _gelu_grad(x, th):
    return 0.5 * (1.0 + th) + 0.5 * x * (1.0 - th * th) * GELU_C * (1.0 + 3.0 * GELU_A * x * x)


def _row(c):
    return pl.BlockSpec((1, c), lambda *_: (0, 0))


def _full(shape):
    return pl.BlockSpec(shape, lambda *_: (0,) * len(shape))


def _norm_proj(h, g, wt, bias, name):
    t, n = h.shape[0], wt.shape[0]
    tm = _tile(t)
    has_bias = bias is not None

    def body(*refs):
        h_ref, g_ref, wt_ref = refs[:3]
        z_ref, hn_ref = refs[-2:]
        hn = _rms(h_ref[...], g_ref[...])[0].astype(BF16)
        hn_ref[...] = hn
        z = _nt(hn, wt_ref[...])
        if has_bias:
            z = z + refs[3][...]
        z_ref[...] = z.astype(BF16)

    in_specs = [pl.BlockSpec((tm, D), lambda i: (i, 0)), _row(D), _full((n, D))]
    args = [h, g, wt]
    if has_bias:
        in_specs.append(_row(n))
        args.append(bias)
    return pl.pallas_call(
        body, name=name, grid=(t // tm,), in_specs=in_specs,
        out_specs=[pl.BlockSpec((tm, n), lambda i: (i, 0)), pl.BlockSpec((tm, D), lambda i: (i, 0))],
        out_shape=[jax.ShapeDtypeStruct((t, n), BF16), jax.ShapeDtypeStruct((t, D), BF16)],
        compiler_params=_params("parallel"))(*args)


def _out_proj(a, b, w, h, name):
    t = h.shape[0]
    tm = _tile(t)

    def body(a_ref, b_ref, wa_ref, wb_ref, h_ref, o_ref):
        o_ref[...] = h_ref[...] + _nn(a_ref[...], wa_ref[...]) + _nn(b_ref[...], wb_ref[...])

    half = pl.BlockSpec((tm, 512), lambda i: (i, 0))
    return pl.pallas_call(
        body, name=name, grid=(t // tm,),
        in_specs=[half, half, pl.BlockSpec((512, D), lambda i: (0, 0)), pl.BlockSpec((512, D), lambda i: (1, 0)),
                  pl.BlockSpec((tm, D), lambda i: (i, 0))],
        out_specs=pl.BlockSpec((tm, D), lambda i: (i, 0)), out_shape=jax.ShapeDtypeStruct((t, D), F32),
        compiler_params=_params("parallel"))(a, b, w, w, h)


def _dmix(dh, w, name):
    t = dh.shape[0]
    tm = _tile(t)

    def body(dh_ref, w_ref, o_ref):
        o_ref[...] = _nt(dh_ref[...].astype(BF16), w_ref[...]).astype(BF16)

    return pl.pallas_call(
        body, name=name, grid=(t // tm,), in_specs=[pl.BlockSpec((tm, D), lambda i: (i, 0)), _full((D, D))],
        out_specs=pl.BlockSpec((tm, D), lambda i: (i, 0)), out_shape=jax.ShapeDtypeStruct((t, D), BF16),
        compiler_params=_params("parallel"))(dh, w)


def _ffn_fwd(h, g, wtg, wtu, wd, name):
    t = h.shape[0]
    tm, tf = _tile(t), FF_TILE
    nf = D_FF // tf

    def body(h_ref, g_ref, wtg_ref, wtu_ref, wd_ref, o_ref, hn_ref, gate_ref, up_ref):
        @pl.when(pl.program_id(1) == 0)
        def _():
            x = h_ref[...]
            hn_ref[...] = _rms(x, g_ref[...])[0].astype(BF16)
            o_ref[...] = x

        hn = hn_ref[...]
        gate = _nt(hn, wtg_ref[...])
        up = _nt(hn, wtu_ref[...])
        gate_ref[...] = gate.astype(BF16)
        up_ref[...] = up.astype(BF16)
        act = (gate * jax.nn.sigmoid(gate) * up).astype(BF16)
        o_ref[...] += _nn(act, wd_ref[...])

    tok = pl.BlockSpec((tm, D), lambda i, f: (i, 0))
    wsp = pl.BlockSpec((tf, D), lambda i, f: (f, 0))
    mid = pl.BlockSpec((tm, tf), lambda i, f: (i, f))
    return pl.pallas_call(
        body, name=name, grid=(t // tm, nf), in_specs=[tok, _row(D), wsp, wsp, wsp],
        out_specs=[tok, tok, mid, mid],
        out_shape=[jax.ShapeDtypeStruct((t, D), F32), jax.ShapeDtypeStruct((t, D), BF16),
                   jax.ShapeDtypeStruct((t, D_FF), BF16), jax.ShapeDtypeStruct((t, D_FF), BF16)],
        compiler_params=_params("parallel", "arbitrary"))(h, g, wtg, wtu, wd)


def _ffn_bwd(dh, h, g, gate, up, wtg, wtu, wd, name):
    t = h.shape[0]
    tm, tf = _tile(t), FF_TILE
    nf = D_FF // tf

    def body(dh_ref, h_ref, g_ref, gate_ref, up_ref, wtg_ref, wtu_ref, wd_ref,
             dhin_ref, dgate_ref, dup_ref, act_ref, dg_ref, dhb):
        i, f = pl.program_id(0), pl.program_id(1)

        @pl.when(f == 0)
        def _():
            dhb[...] = dh_ref[...].astype(BF16)
            dhin_ref[...] = jnp.zeros_like(dhin_ref)

        @pl.when((i == 0) & (f == 0))
        def _():
            dg_ref[...] = jnp.zeros_like(dg_ref)

        dact = _nt(dhb[...], wd_ref[...])
        gt = gate_ref[...].astype(F32)
        u = up_ref[...].astype(F32)
        sg = jax.nn.sigmoid(gt)
        sil = gt * sg
        act_ref[...] = (sil * u).astype(BF16)
        dup = (dact * sil).astype(BF16)
        dgate = (dact * u * sg * (1.0 + gt * (1.0 - sg))).astype(BF16)
        dup_ref[...] = dup
        dgate_ref[...] = dgate
        dhin_ref[...] += _nn(dgate, wtg_ref[...]) + _nn(dup, wtu_ref[...])

        @pl.when(f == nf - 1)
        def _():
            d_x, d_g = _rms_bwd(h_ref[...], g_ref[...], dhin_ref[...])
            dhin_ref[...] = dh_ref[...] + d_x
            dg_ref[...] += d_g

    tok = pl.BlockSpec((tm, D), lambda i, f: (i, 0))
    wsp = pl.BlockSpec((tf, D), lambda i, f: (f, 0))
    mid = pl.BlockSpec((tm, tf), lambda i, f: (i, f))
    mid_shape = jax.ShapeDtypeStruct((t, D_FF), BF16)
    return pl.pallas_call(
        body, name=name, grid=(t // tm, nf), in_specs=[tok, tok, _row(D), mid, mid, wsp, wsp, wsp],
        out_specs=[tok, mid, mid, mid, _row(D)],
        out_shape=[jax.ShapeDtypeStruct((t, D), F32), mid_shape, mid_shape, mid_shape, jax.ShapeDtypeStruct((1, D), F32)],
        scratch_shapes=[pltpu.VMEM((tm, D), BF16)],
        compiler_params=_params("arbitrary", "arbitrary"))(dh, h, g, gate, up, wtg, wtu, wd)


def _proj_bwd_norm(pieces, wt, h, dh, g, name):
    t = h.shape[0]
    tm = _tile(t)
    n_p = len(pieces)

    def body(*refs):
        p_refs, w_refs = refs[:n_p], refs[n_p:2 * n_p]
        h_ref, dh_ref, g_ref, o_ref, dg_ref = refs[2 * n_p:]

        @pl.when(pl.program_id(0) == 0)
        def _():
            dg_ref[...] = jnp.zeros_like(dg_ref)

        d_hn = _nn(p_refs[0][...], w_refs[0][...])
        for p_ref, w_ref in zip(p_refs[1:], w_refs[1:]):
            d_hn = d_hn + _nn(p_ref[...], w_ref[...])
        d_x, d_g = _rms_bwd(h_ref[...], g_ref[...], d_hn)
        o_ref[...] = dh_ref[...] + d_x
        dg_ref[...] += d_g

    tok = pl.BlockSpec((tm, D), lambda i: (i, 0))
    in_specs = [pl.BlockSpec((tm, a.shape[1]), lambda i: (i, 0)) for a, _ in pieces]
    for a, off in pieces:
        w = a.shape[1]
        assert off % w == 0
        in_specs.append(pl.BlockSpec((w, D), functools.partial(lambda i, blk: (blk, 0), blk=off // w)))
    in_specs += [tok, tok, _row(D)]
    return pl.pallas_call(
        body, name=name, grid=(t // tm,), in_specs=in_specs, out_specs=[tok, _row(D)],
        out_shape=[jax.ShapeDtypeStruct((t, D), F32), jax.ShapeDtypeStruct((1, D), F32)],
        compiler_params=_params("arbitrary"))(*[a for a, _ in pieces], *([wt] * n_p), h, dh, g)


def _mm_tn(a, b, name):
    t, n = a.shape
    k = b.shape[1]
    tn = n if n <= 1024 else n // 2
    tt = _tile(t)

    def body(a_ref, b_ref, o_ref):
        @pl.when(pl.program_id(1) == 0)
        def _():
            o_ref[...] = jnp.zeros_like(o_ref)

        o_ref[...] += _tn(a_ref[...], b_ref[...].astype(BF16))

    return pl.pallas_call(
        body, name=name, grid=(n // tn, t // tt),
        in_specs=[pl.BlockSpec((tt, tn), lambda j, s: (s, j)), pl.BlockSpec((tt, k), lambda j, s: (s, 0))],
        out_specs=pl.BlockSpec((tn, k), lambda j, s: (j, 0)), out_shape=jax.ShapeDtypeStruct((n, k), F32),
        compiler_params=_params("parallel", "arbitrary"))(a, b)


def _loss_head(h, g, target, name):
    t = h.shape[0]
    tm = _tile(t)

    def body(h_ref, g_ref, t_ref, dh_ref, dg_ref, loss_ref):
        @pl.when(pl.program_id(0) == 0)
        def _():
            dg_ref[...] = jnp.zeros_like(dg_ref)
            loss_ref[...] = jnp.zeros_like(loss_ref)

        x = h_ref[...]
        gv = g_ref[...]
        err = _rms(x, gv)[0] - t_ref[...]
        per_tok = jnp.mean(err * err, axis=-1, keepdims=True)
        loss_ref[...] += 0.5 * jnp.sum(per_tok, axis=0, keepdims=True)
        d_x, d_g = _rms_bwd(x, gv, err * (1.0 / D))
        dh_ref[...] = d_x
        dg_ref[...] += d_g

    tok = pl.BlockSpec((tm, D), lambda i: (i, 0))
    return pl.pallas_call(
        body, name=name, grid=(t // tm,), in_specs=[tok, _row(D), tok],
        out_specs=[tok, _row(D), _row(1)],
        out_shape=[jax.ShapeDtypeStruct((t, D), F32), jax.ShapeDtypeStruct((1, D), F32), jax.ShapeDtypeStruct((1, 1), F32)],
        compiler_params=_params("arbitrary"))(h, g, target)


def _attn_valid(first):
    qi = lax.broadcasted_iota(jnp.int32, (ATT_BLK, 2 * ATT_BLK), 0)
    r = lax.broadcasted_iota(jnp.int32, (ATT_BLK, 2 * ATT_BLK), 1)
    dist = qi + ATT_BLK - r
    return (dist >= 0) & (dist < ATT_BLK) & ((r >= ATT_BLK) | jnp.logical_not(first))


def _attn_probs(qm, kpair, sink, valid):
    s = jnp.where(valid, _nt(qm, kpair), -1e30)
    m = jnp.maximum(jnp.max(s, axis=-1, keepdims=True), sink)
    p = jnp.exp(s - m)
    es = jnp.exp(sink - m)
    inv = 1.0 / (jnp.sum(p, axis=-1, keepdims=True) + es)
    return p * inv, es * inv


def _attn_specs(nb, order):
    q = pl.BlockSpec((ATT_BLK, 512), lambda b, j: (b * nb + order(j), 0))
    kvc = pl.BlockSpec((ATT_BLK, 256), lambda b, j: (b * nb + order(j), 6))
    kvp = pl.BlockSpec((ATT_BLK, 256), lambda b, j: (jnp.maximum(b * nb + order(j) - 1, 0), 6))
    return q, kvc, kvp


def _window_kv(kvc_ref, kvp_ref):
    kvc, kvp = kvc_ref[...], kvp_ref[...]
    kpair = jnp.concatenate([kvp[:, :128], kvc[:, :128]], axis=0)
    vpair = jnp.concatenate([kvp[:, 128:], kvc[:, 128:]], axis=0)
    return kpair, vpair


def _attn_fwd(z0, sinks, bsz, name):
    t = z0.shape[0]
    nb = t // bsz // ATT_BLK

    def body(s_ref, q_ref, kvc_ref, kvp_ref, o_ref):
        valid = _attn_valid(pl.program_id(1) == 0)
        kpair, vpair = _window_kv(kvc_ref, kvp_ref)
        lo = lax.broadcasted_iota(jnp.int32, (ATT_BLK, 128), 1) < 64
        for g in range(HEAD_PAIRS):
            qs = q_ref[:, g * 128:(g + 1) * 128] * 0.125
            outs = []
            for kh in range(2):
                qm = jnp.where(lo if kh == 0 else ~lo, qs, 0.0).astype(BF16)
                p, _ = _attn_probs(qm, kpair, s_ref[0, kh * 4 + g], valid)
                outs.append(_nn(p.astype(BF16), vpair))
            o_ref[:, g * 128:(g + 1) * 128] = jnp.where(lo, outs[0], outs[1]).astype(BF16)

    q, kvc, kvp = _attn_specs(nb, lambda j: j)
    return pl.pallas_call(
        body, name=name, grid=(bsz, nb),
        in_specs=[pl.BlockSpec(memory_space=pltpu.SMEM), q, kvc, kvp],
        out_specs=pl.BlockSpec((ATT_BLK, 512), lambda b, j: (b * nb + j, 0)),
        out_shape=jax.ShapeDtypeStruct((t, 512), BF16),
        compiler_params=_params("parallel", "parallel"))(sinks, z0, z0, z0)


def _attn_bwd(z0, dmix, sinks, bsz, name):
    t = z0.shape[0]
    nb = t // bsz // ATT_BLK

    def body(s_ref, q_ref, kvc_ref, kvp_ref, do_ref, dq_ref, dkv_ref, dsink_ref, dbq_ref, dbkv_ref, carry):
        b, j = pl.program_id(0), pl.program_id(1)

        @pl.when(j == 0)
        def _():
            carry[...] = jnp.zeros_like(carry)

        @pl.when((b == 0) & (j == 0))
        def _():
            dsink_ref[...] = jnp.zeros_like(dsink_ref)
            dbq_ref[...] = jnp.zeros_like(dbq_ref)
            dbkv_ref[...] = jnp.zeros_like(dbkv_ref)

        valid = _attn_valid(j == nb - 1)
        kpair, vpair = _window_kv(kvc_ref, kvp_ref)
        lo = lax.broadcasted_iota(jnp.int32, (ATT_BLK, 128), 1) < 64
        lane = lax.broadcasted_iota(jnp.int32, (1, 128), 1)
        dk = jnp.zeros((2 * ATT_BLK, 128), F32)
        dv = jnp.zeros((2 * ATT_BLK, 128), F32)
        dsink = jnp.zeros((1, 128), F32)
        for g in range(HEAD_PAIRS):
            qs = q_ref[:, g * 128:(g + 1) * 128] * 0.125
            do = do_ref[:, g * 128:(g + 1) * 128]
            dqs = []
            for kh in range(2):
                msk = lo if kh == 0 else ~lo
                qm = jnp.where(msk, qs, 0.0).astype(BF16)
                dom = jnp.where(msk, do, 0.0).astype(BF16)
                p, ps = _attn_probs(qm, kpair, s_ref[0, kh * 4 + g], valid)
                dp = _nt(dom, vpair)
                delta = jnp.sum(p * dp, axis=-1, keepdims=True)
                ds = (p * (dp - delta)).astype(BF16)
                dqs.append(_nn(ds, kpair))
                dk = dk + _tn(ds, qm)
                dv = dv + _tn(p.astype(BF16), dom)
                dsink = dsink + jnp.where(lane == kh * 4 + g, jnp.sum(-ps * delta, axis=0, keepdims=True), 0.0)
            dq = jnp.where(lo, dqs[0], dqs[1]) * 0.125
            dq_ref[:, g * 128:(g + 1) * 128] = dq.astype(BF16)
            dbq_ref[:, g * 128:(g + 1) * 128] += jnp.sum(dq, axis=0, keepdims=True)
        dkv = jnp.concatenate([dk[ATT_BLK:], dv[ATT_BLK:]], axis=1) + carry[...]
        dkv_ref[...] = dkv.astype(BF16)
        dbkv_ref[...] += jnp.sum(dkv, axis=0, keepdims=True)
        carry[...] = jnp.concatenate([dk[:ATT_BLK], dv[:ATT_BLK]], axis=1)
        dsink_ref[...] += dsink

    q, kvc, kvp = _attn_specs(nb, lambda j: nb - 1 - j)
    return pl.pallas_call(
        body, name=name, grid=(bsz, nb),
        in_specs=[pl.BlockSpec(memory_space=pltpu.SMEM), q, kvc, kvp,
                  pl.BlockSpec((ATT_BLK, 512), lambda b, j: (b * nb + nb - 1 - j, 0))],
        out_specs=[pl.BlockSpec((ATT_BLK, 512), lambda b, j: (b * nb + nb - 1 - j, 0)),
                   pl.BlockSpec((ATT_BLK, 256), lambda b, j: (b * nb + nb - 1 - j, 0)), _row(128), _row(512), _row(256)],
        out_shape=[jax.ShapeDtypeStruct((t, 512), BF16), jax.ShapeDtypeStruct((t, 256), BF16),
                   jax.ShapeDtypeStruct((1, 128), F32), jax.ShapeDtypeStruct((1, 512), F32),
                   jax.ShapeDtypeStruct((1, 256), F32)],
        scratch_shapes=[pltpu.VMEM((ATT_BLK, 256), F32)],
        compiler_params=_params("arbitrary", "arbitrary"))(sinks, z0, z0, z0, dmix)


def _seq_specs(ts, nt, t, width, col):
    per = ts // HALO
    cur = pl.BlockSpec((ts, width), lambda b, i: (b * nt + i, col))
    prev = pl.BlockSpec((HALO, width), lambda b, i: (jnp.maximum((b * nt + i) * per - 1, 0), col))
    nxt = pl.BlockSpec((HALO, width), lambda b, i: (jnp.minimum((b * nt + i + 1) * per, t // HALO - 1), col))
    return prev, cur, nxt


def _conv_taps(buf, w, first, rows):
    acc = buf[pl.ds(first, rows), :] * w[0:1, :]
    for k in range(1, CONV_K):
        acc = acc + buf[pl.ds(first + k, rows), :] * w[k:k + 1, :]
    return acc


def _conv_fwd(z0, conv_w, conv_b, ln_g, ln_b, bsz, name):
    t = z0.shape[0]
    s = t // bsz
    ts = _seq_tile(s)
    nt = s // ts

    def body(ap_ref, ac_ref, gp_ref, gc_ref, w_ref, cb_ref, lg_ref, lb_ref, o_ref, hbuf):
        hp = ap_ref[...].astype(F32) * jax.nn.sigmoid(gp_ref[...].astype(F32))
        hbuf[0:HALO, :] = jnp.where(pl.program_id(1) > 0, hp, 0.0)
        hbuf[HALO:HALO + ts, :] = ac_ref[...].astype(F32) * jax.nn.sigmoid(gc_ref[...].astype(F32))
        y = _conv_taps(hbuf, w_ref[...], HALO - (CONV_K - 1), ts) + cb_ref[...]
        o = _ln(y, lg_ref[...], lb_ref[...])[0]
        o_ref[...] = (o * jax.nn.sigmoid(o)).astype(BF16)

    ap, ac, _ = _seq_specs(ts, nt, t, 512, 1)
    gp, gc, _ = _seq_specs(ts, nt, t, 512, 2)
    return pl.pallas_call(
        body, name=name, grid=(bsz, nt),
        in_specs=[ap, ac, gp, gc, _full((HALO, 512)), _row(512), _row(512), _row(512)],
        out_specs=pl.BlockSpec((ts, 512), lambda b, i: (b * nt + i, 0)),
        out_shape=jax.ShapeDtypeStruct((t, 512), BF16),
        scratch_shapes=[pltpu.VMEM((HALO + ts, 512), F32)],
        compiler_params=_params("parallel", "parallel"))(z0, z0, z0, z0, conv_w, conv_b, ln_g, ln_b)


def _conv_bwd(z0, dmix, conv_w, conv_b, ln_g, ln_b, bsz, name):
    t = z0.shape[0]
    s = t // bsz
    ts = _seq_tile(s)
    nt = s // ts
    rr = ts + HALO

    def body(ap_ref, ac_ref, an_ref, gp_ref, gc_ref, gn_ref, dc_ref, dn_ref, w_ref, cb_ref, lg_ref, lb_ref,
             da_ref, dg_ref, dw_ref, dcb_ref, dlg_ref, dlb_ref, dba_ref, dbg_ref, hbuf, dybuf):
        b, i = pl.program_id(0), pl.program_id(1)

        @pl.when((b == 0) & (i == 0))
        def _():
            for ref in (dw_ref, dcb_ref, dlg_ref, dlb_ref, dba_ref, dbg_ref):
                ref[...] = jnp.zeros_like(ref)

        a_c = ac_ref[...].astype(F32)
        sg_c = jax.nn.sigmoid(gc_ref[...].astype(F32))
        hp = ap_ref[...].astype(F32) * jax.nn.sigmoid(gp_ref[...].astype(F32))
        hbuf[0:HALO, :] = jnp.where(i > 0, hp, 0.0)
        hbuf[HALO:HALO + ts, :] = a_c * sg_c
        hbuf[HALO + ts:2 * HALO + ts, :] = an_ref[...].astype(F32) * jax.nn.sigmoid(gn_ref[...].astype(F32))
        w = w_ref[...]
        y = _conv_taps(hbuf, w, HALO - (CONV_K - 1), rr) + cb_ref[...]
        lg = lg_ref[...]
        o, xhat, rstd = _ln(y, lg, lb_ref[...])
        dn = jnp.where(i < nt - 1, dn_ref[...].astype(F32), 0.0)
        dout = jnp.concatenate([dc_ref[...].astype(F32), dn], axis=0)
        sg_o = jax.nn.sigmoid(o)
        d_o = dout * sg_o * (1.0 + o * (1.0 - sg_o))
        dlg_ref[...] += jnp.sum(d_o[:ts] * xhat[:ts], axis=0, keepdims=True)
        dlb_ref[...] += jnp.sum(d_o[:ts], axis=0, keepdims=True)
        dy = _ln_bwd(d_o, xhat, rstd, lg)
        dybuf[...] = dy
        dy_c = dy[:ts]
        dcb_ref[...] += jnp.sum(dy_c, axis=0, keepdims=True)
        for k in range(CONV_K):
            dw_ref[pl.ds(k, 1), :] += jnp.sum(dy_c * hbuf[pl.ds(HALO - (CONV_K - 1) + k, ts), :], axis=0, keepdims=True)
        dh = dybuf[pl.ds(CONV_K - 1, ts), :] * w[0:1, :]
        for k in range(1, CONV_K):
            dh = dh + dybuf[pl.ds(CONV_K - 1 - k, ts), :] * w[k:k + 1, :]
        d_a = dh * sg_c
        d_g = dh * a_c * sg_c * (1.0 - sg_c)
        da_ref[...] = d_a.astype(BF16)
        dg_ref[...] = d_g.astype(BF16)
        dba_ref[...] += jnp.sum(d_a, axis=0, keepdims=True)
        dbg_ref[...] += jnp.sum(d_g, axis=0, keepdims=True)

    ap, ac, an = _seq_specs(ts, nt, t, 512, 1)
    gp, gc, gn = _seq_specs(ts, nt, t, 512, 2)
    _, dc, dn = _seq_specs(ts, nt, t, 512, 1)
    tile = pl.BlockSpec((ts, 512), lambda b, i: (b * nt + i, 0))
    vec = jax.ShapeDtypeStruct((1, 512), F32)
    return pl.pallas_call(
        body, name=name, grid=(bsz, nt),
        in_specs=[ap, ac, an, gp, gc, gn, dc, dn, _full((HALO, 512)), _row(512), _row(512), _row(512)],
        out_specs=[tile, tile, _full((HALO, 512)), _row(512), _row(512), _row(512), _row(512), _row(512)],
        out_shape=[jax.ShapeDtypeStruct((t, 512), BF16), jax.ShapeDtypeStruct((t, 512), BF16),
                   jax.ShapeDtypeStruct((HALO, 512), F32), vec, vec, vec, vec, vec],
        scratch_shapes=[pltpu.VMEM((2 * HALO + ts, 512), F32), pltpu.VMEM((rr, 512), F32)],
        compiler_params=_params("arbitrary", "arbitrary"))(z0, z0, z0, z0, z0, z0, dmix, dmix, conv_w, conv_b, ln_g, ln_b)


def _pooled(pbuf, g, ts, tok):
    w = 2 << g
    cols = slice(128 * g, 128 * (g + 1))
    sm = pbuf[pl.ds(HALO, ts), cols]
    for d in range(1, w):
        sm = sm + pbuf[pl.ds(HALO - d, ts), cols]
    cnt = jnp.minimum(tok + 1, w).astype(F32)
    return sm / cnt - pbuf[pl.ds(HALO, ts), cols]


def _pool_fwd(z1, w_pool, scale, bsz, name):
    t = z1.shape[0]
    s = t // bsz
    ts = _seq_tile(s)
    nt = s // ts

    def body(zp_ref, zc_ref, wp_ref, sc_ref, o_ref, pbuf):
        i = pl.program_id(1)
        pbuf[0:HALO, :] = jnp.where(i > 0, zp_ref[...].astype(F32), 0.0)
        pbuf[HALO:HALO + ts, :] = zc_ref[...].astype(F32)
        tok = i * ts + lax.broadcasted_iota(jnp.int32, (ts, 1), 0)
        for g in range(4):
            cols = slice(128 * g, 128 * (g + 1))
            pooled = _pooled(pbuf, g, ts, tok).astype(BF16)
            o_ref[:, cols] = (_nn(pooled, wp_ref[g].astype(BF16)) * sc_ref[:, cols]).astype(BF16)

    zp, zc, _ = _seq_specs(ts, nt, t, 512, 0)
    return pl.pallas_call(
        body, name=name, grid=(bsz, nt), in_specs=[zp, zc, _full((4, 128, 128)), _row(512)],
        out_specs=pl.BlockSpec((ts, 512), lambda b, i: (b * nt + i, 0)),
        out_shape=jax.ShapeDtypeStruct((t, 512), BF16),
        scratch_shapes=[pltpu.VMEM((HALO + ts, 512), F32)],
        compiler_params=_params("parallel", "parallel"))(z1, z1, w_pool, scale)


def _pool_bwd(z1, dmix, w_pool, scale, bsz, name):
    t = z1.shape[0]
    s = t // bsz
    ts = _seq_tile(s)
    nt = s // ts
    rr = ts + HALO

    def body(zp_ref, zc_ref, dc_ref, dn_ref, wp_ref, sc_ref, dz_ref, dwp_ref, dsc_ref, pbuf, ebuf):
        b, i = pl.program_id(0), pl.program_id(1)

        @pl.when((b == 0) & (i == 0))
        def _():
            dwp_ref[...] = jnp.zeros_like(dwp_ref)
            dsc_ref[...] = jnp.zeros_like(dsc_ref)

        pbuf[0:HALO, :] = jnp.where(i > 0, zp_ref[...].astype(F32), 0.0)
        pbuf[HALO:HALO + ts, :] = zc_ref[...].astype(F32)
        dn = jnp.where(i < nt - 1, dn_ref[...].astype(F32), 0.0)
        dout = jnp.concatenate([dc_ref[...].astype(F32), dn], axis=0)
        tok = i * ts + lax.broadcasted_iota(jnp.int32, (ts, 1), 0)
        tok_r = i * ts + lax.broadcasted_iota(jnp.int32, (rr, 1), 0)
        for g in range(4):
            w = 2 << g
            cols = slice(128 * g, 128 * (g + 1))
            wg = wp_ref[g].astype(BF16)
            pooled = _pooled(pbuf, g, ts, tok).astype(BF16)
            dsc_ref[:, cols] += jnp.sum(dout[:ts, cols] * _nn(pooled, wg), axis=0, keepdims=True)
            dy = (dout[:, cols] * sc_ref[:, cols]).astype(BF16)
            dwp_ref[g] += _tn(pooled, dy[:ts])
            dpl = _nt(dy, wg)
            ebuf[...] = dpl / jnp.minimum(tok_r + 1, w).astype(F32)
            dz = ebuf[pl.ds(0, ts), :] - dpl[:ts]
            for d in range(1, w):
                dz = dz + ebuf[pl.ds(d, ts), :]
            dz_ref[:, cols] = dz.astype(BF16)

    zp, zc, _ = _seq_specs(ts, nt, t, 512, 0)
    _, dc, dn = _seq_specs(ts, nt, t, 512, 0)
    return pl.pallas_call(
        body, name=name, grid=(bsz, nt), in_specs=[zp, zc, dc, dn, _full((4, 128, 128)), _row(512)],
        out_specs=[pl.BlockSpec((ts, 512), lambda b, i: (b * nt + i, 0)), _full((4, 128, 128)), _row(512)],
        out_shape=[jax.ShapeDtypeStruct((t, 512), BF16), jax.ShapeDtypeStruct((4, 128, 128), F32),
                   jax.ShapeDtypeStruct((1, 512), F32)],
        scratch_shapes=[pltpu.VMEM((HALO + ts, 512), F32), pltpu.VMEM((rr, 128), F32)],
        compiler_params=_params("arbitrary", "arbitrary"))(z1, z1, dmix, dmix, w_pool, scale)


def _tril():
    r = lax.broadcasted_iota(jnp.int32, (SGU_CHUNK, SGU_CHUNK), 0)
    c = lax.broadcasted_iota(jnp.int32, (SGU_CHUNK, SGU_CHUNK), 1)
    return r >= c


def _sgu_fwd(z1, ln_g, ln_b, w_s, b_rows, name):
    t = z1.shape[0]
    ts = _tile(t)

    def body(zu_ref, zv_ref, lg_ref, lb_ref, ws_ref, bs_ref, o_ref):
        v = _gelu(zv_ref[...].astype(F32))[0]
        vb = _ln(v, lg_ref[...], lb_ref[...])[0].astype(BF16)
        tril = _tril()
        for g in range(4):
            cols = slice(128 * g, 128 * (g + 1))
            wg = jnp.where(tril, ws_ref[g], 0.0).astype(BF16)
            for c in range(ts // SGU_CHUNK):
                rows = slice(SGU_CHUNK * c, SGU_CHUNK * (c + 1))
                mixed = _nn(wg, vb[rows, cols]) + bs_ref[g]
                o_ref[rows, cols] = (_gelu(zu_ref[rows, cols].astype(F32))[0] * mixed).astype(BF16)

    return pl.pallas_call(
        body, name=name, grid=(t // ts,),
        in_specs=[pl.BlockSpec((ts, 512), lambda i: (i, 1)), pl.BlockSpec((ts, 512), lambda i: (i, 2)),
                  _row(512), _row(512), _full((4, 128, 128)), _full((4, 128, 128))],
        out_specs=pl.BlockSpec((ts, 512), lambda i: (i, 0)), out_shape=jax.ShapeDtypeStruct((t, 512), BF16),
        compiler_params=_params("parallel"))(z1, z1, ln_g, ln_b, w_s, b_rows)


def _sgu_bwd(z1, dmix, ln_g, ln_b, w_s, b_rows, name):
    t = z1.shape[0]
    ts = _tile(t)

    def body(zu_ref, zv_ref, d_ref, lg_ref, lb_ref, ws_ref, bs_ref,
             dzu_ref, dzv_ref, dws_ref, dbs_ref, dlg_ref, dlb_ref, dvbuf):
        @pl.when(pl.program_id(0) == 0)
        def _():
            for ref in (dws_ref, dbs_ref, dlg_ref, dlb_ref):
                ref[...] = jnp.zeros_like(ref)

        zv = zv_ref[...].astype(F32)
        v, thv = _gelu(zv)
        lg = lg_ref[...]
        vln, xhat, rstd = _ln(v, lg, lb_ref[...])
        vb = vln.astype(BF16)
        tril = _tril()
        for g in range(4):
            cols = slice(128 * g, 128 * (g + 1))
            wg = jnp.where(tril, ws_ref[g], 0.0).astype(BF16)
            dws = jnp.zeros((SGU_CHUNK, SGU_CHUNK), F32)
            dbs = jnp.zeros((1, SGU_CHUNK), F32)
            for c in range(ts // SGU_CHUNK):
                rows = slice(SGU_CHUNK * c, SGU_CHUNK * (c + 1))
                vbc = vb[rows, cols]
                mixed = _nn(wg, vbc) + bs_ref[g]
                zu = zu_ref[rows, cols].astype(F32)
                u, thu = _gelu(zu)
                dout = d_ref[rows, cols].astype(F32)
                dzu_ref[rows, cols] = (dout * mixed * _gelu_grad(zu, thu)).astype(BF16)
                dm = dout * u
                dmb = dm.astype(BF16)
                dws = dws + _nt(dmb, vbc)
                dbs = dbs + jnp.sum(dm.T, axis=0, keepdims=True)
                dvbuf[rows, cols] = _tn(wg, dmb)
            dws_ref[g] += jnp.where(tril, dws, 0.0)
            dbs_ref[pl.ds(g, 1), :] += dbs
        dvln = dvbuf[...]
        dlg_ref[...] += jnp.sum(dvln * xhat, axis=0, keepdims=True)
        dlb_ref[...] += jnp.sum(dvln, axis=0, keepdims=True)
        dzv_ref[...] = (_ln_bwd(dvln, xhat, rstd, lg) * _gelu_grad(zv, thv)).astype(BF16)

    tile = pl.BlockSpec((ts, 512), lambda i: (i, 0))
    vec = jax.ShapeDtypeStruct((1, 512), F32)
    return pl.pallas_call(
        body, name=name, grid=(t // ts,),
        in_specs=[pl.BlockSpec((ts, 512), lambda i: (i, 1)), pl.BlockSpec((ts, 512), lambda i: (i, 2)),
                  pl.BlockSpec((ts, 512), lambda i: (i, 1)), _row(512), _row(512), _full((4, 128, 128)),
                  _full((4, 128, 128))],
        out_specs=[tile, tile, _full((4, 128, 128)), _full((4, 128)), _row(512), _row(512)],
        out_shape=[jax.ShapeDtypeStruct((t, 512), BF16), jax.ShapeDtypeStruct((t, 512), BF16),
                   jax.ShapeDtypeStruct((4, 128, 128), F32), jax.ShapeDtypeStruct((4, 128), F32), vec, vec],
        scratch_shapes=[pltpu.VMEM((ts, 512), F32)],
        compiler_params=_params("arbitrary"))(z1, z1, dmix, ln_g, ln_b, w_s, b_rows)


def _row_tile(r):
    for cand in (512, 352, 256, 192, 128, 64, 32, 16, 8):
        if r % cand == 0:
            return cand
    return r


def _sum_rows(parts, dtype, name):
    r, c = parts[0].shape
    tr = _row_tile(r)

    def body(*refs):
        acc = refs[0][...].astype(F32)
        for ref in refs[1:-1]:
            acc = acc + ref[...].astype(F32)
        refs[-1][...] = acc.astype(dtype)

    spec = pl.BlockSpec((tr, c), lambda i: (i, 0))
    return pl.pallas_call(
        body, name=name, grid=(r // tr,), in_specs=[spec] * len(parts), out_specs=spec,
        out_shape=jax.ShapeDtypeStruct((r, c), dtype), compiler_params=_params("parallel"))(*parts)


def _sum_slabs(a, name):
    k, r, c = a.shape
    tr = _row_tile(r)

    def body(*refs):
        acc = refs[0][...].astype(F32)
        for ref in refs[1:-1]:
            acc = acc + ref[...].astype(F32)
        refs[-1][...] = acc

    in_specs = [pl.BlockSpec((None, tr, c), functools.partial(lambda i, s: (s, i, 0), s=s)) for s in range(k)]
    return pl.pallas_call(
        body, name=name, grid=(r // tr,), in_specs=in_specs, out_specs=pl.BlockSpec((tr, c), lambda i: (i, 0)),
        out_shape=jax.ShapeDtypeStruct((r, c), F32), compiler_params=_params("parallel"))(*([a] * k))


def _adamw(w, g, m, v, name):
    r, c = w.shape
    tr = _row_tile(r)

    def body(w_ref, g_ref, m_ref, v_ref, d_ref, mo_ref, vo_ref):
        gv = g_ref[...]
        mn = ADAM_B1 * m_ref[...] + (1.0 - ADAM_B1) * gv
        vn = ADAM_B2 * v_ref[...] + (1.0 - ADAM_B2) * (gv * gv)
        m_hat = mn / (1.0 - ADAM_B1 ** ADAM_STEP)
        v_hat = vn / (1.0 - ADAM_B2 ** ADAM_STEP)
        d_ref[...] = -ADAM_LR * (m_hat / (jnp.sqrt(v_hat) + ADAM_EPS) + ADAM_WD * w_ref[...])
        mo_ref[...] = mn
        vo_ref[...] = vn

    spec = pl.BlockSpec((tr, c), lambda i: (i, 0))
    shape = jax.ShapeDtypeStruct((r, c), F32)
    return pl.pallas_call(
        body, name=name, grid=(r // tr,), in_specs=[spec] * 4, out_specs=[spec] * 3, out_shape=[shape] * 3,
        compiler_params=_params("parallel"))(w, g, m, v)


ANY = pl.BlockSpec(memory_space=pl.ANY)


def _all_gather(block, name):
    r, c_dim = block.shape

    def body(x_ref, out_ref, send_sems, recv_sems, local_sem):
        x, y, c = lax.axis_index("x"), lax.axis_index("y"), lax.axis_index("c")
        me, sibling = (x, y, c), (x, y, 1 - c)
        chips = [(1 - x, y), (x, 1 - y), (1 - x, 1 - y)]

        def rows(px, py, pc):
            return out_ref.at[4 * px + 2 * py + pc]

        def copy(k, blk, to, src=None):
            return pltpu.make_async_remote_copy(
                src_ref=rows(*blk) if src is None else src, dst_ref=rows(*blk), send_sem=send_sems.at[k],
                recv_sem=recv_sems.at[k], device_id=to, device_id_type=MESH)

        mine = pltpu.make_async_copy(x_ref, rows(*me), local_sem)
        mine.start()
        first = [copy(0, me, sibling, src=x_ref)]
        first += [copy(1 + j, me, (*chip, c), src=x_ref) for j, chip in enumerate(chips)]
        for cp in first:
            cp.start()
        passed = [copy(4 + j, (*chip, c), sibling) for j, chip in enumerate(chips)]
        for j, chip in enumerate(chips):
            copy(1 + j, (*chip, c), me).wait_recv()
            passed[j].start()
        copy(0, sibling, me).wait_recv()
        for j, chip in enumerate(chips):
            copy(4 + j, (*chip, 1 - c), me).wait_recv()
        for cp in first + passed:
            cp.wait_send()
        mine.wait()

    return pl.pallas_call(
        body, name=name, in_specs=[ANY], out_specs=ANY,
        out_shape=jax.ShapeDtypeStruct((N_DEV, r, c_dim), block.dtype),
        scratch_shapes=[pltpu.SemaphoreType.DMA((7,)), pltpu.SemaphoreType.DMA((7,)), pltpu.SemaphoreType.DMA],
    )(block)


def _pair_exchange(grads, name):
    n = len(grads)

    def body(*refs):
        g_refs, land_refs, keep_refs = refs[:n], refs[n:2 * n], refs[2 * n:3 * n]
        send_sems, recv_sems, local_sems = refs[3 * n:]
        x, y, c = lax.axis_index("x"), lax.axis_index("y"), lax.axis_index("c")
        sends, locals_ = [], []
        for a in range(n):
            for k in range(4):
                s = 4 * a + k
                sends.append(pltpu.make_async_remote_copy(
                    src_ref=g_refs[a].at[2 * k + 1 - c], dst_ref=land_refs[a].at[k], send_sem=send_sems.at[s],
                    recv_sem=recv_sems.at[s], device_id=(x, y, 1 - c), device_id_type=MESH))
                locals_.append(pltpu.make_async_copy(g_refs[a].at[2 * k + c], keep_refs[a].at[k], local_sems.at[s]))
        for cp in sends + locals_:
            cp.start()
        for cp in sends:
            cp.wait()
        for cp in locals_:
            cp.wait()

    shapes = [jax.ShapeDtypeStruct((4,) + g.shape[1:], g.dtype) for g in grads]
    res = pl.pallas_call(
        body, name=name, in_specs=[ANY] * n, out_specs=[ANY] * (2 * n), out_shape=shapes + shapes,
        scratch_shapes=[pltpu.SemaphoreType.DMA((4 * n,))] * 3)(*grads)
    return res[:n], res[n:]


def _chip_exchange(parts, name):
    n = len(parts)

    def body(*refs):
        p_refs, l_refs = refs[:n], refs[n:2 * n]
        send_sems, recv_sems, local_sems = refs[2 * n:]
        x, y, c = lax.axis_index("x"), lax.axis_index("y"), lax.axis_index("c")
        me = 2 * x + y
        peers = [(x, 1 - y), (1 - x, y), (1 - x, 1 - y)]
        owns, sends, arrivals = [], [], []
        for a in range(n):
            owns.append(pltpu.make_async_copy(p_refs[a].at[me], l_refs[a].at[me], local_sems.at[a]))
            for j, (px, py) in enumerate(peers):
                peer, s = 2 * px + py, 3 * a + j
                sends.append(pltpu.make_async_remote_copy(
                    src_ref=p_refs[a].at[peer], dst_ref=l_refs[a].at[me], send_sem=send_sems.at[s],
                    recv_sem=recv_sems.at[s], device_id=(px, py, c), device_id_type=MESH))
                arrivals.append(pltpu.make_async_remote_copy(
                    src_ref=p_refs[a].at[peer], dst_ref=l_refs[a].at[peer], send_sem=send_sems.at[s],
                    recv_sem=recv_sems.at[s], device_id=(x, y, c), device_id_type=MESH))
        for cp in owns + sends:
            cp.start()
        for cp in arrivals:
            cp.wait_recv()
        for cp in sends:
            cp.wait_send()
        for cp in owns:
            cp.wait()

    return pl.pallas_call(
        body, name=name, in_specs=[ANY] * n, out_specs=[ANY] * n,
        out_shape=[jax.ShapeDtypeStruct(p.shape, p.dtype) for p in parts],
        scratch_shapes=[pltpu.SemaphoreType.DMA((3 * n,)), pltpu.SemaphoreType.DMA((3 * n,)),
                        pltpu.SemaphoreType.DMA((n,))])(*parts)


def _perm_heads(a, perm, axis):
    idx = [slice(None)] * a.ndim
    parts = []
    for h in perm:
        idx[axis] = slice(64 * h, 64 * (h + 1))
        parts.append(a[tuple(idx)])
    idx[axis] = slice(512, None)
    if a.shape[axis] > 512:
        parts.append(a[tuple(idx)])
    return jnp.concatenate(parts, axis=axis)


Q_INV = tuple(int(i) for i in np.argsort(Q_PERM))


def _in0_to_kernel(a, axis):
    a = _perm_heads(a, Q_PERM, axis)
    idx = [slice(None)] * a.ndim

    def cut(lo, hi):
        idx[axis] = slice(lo, hi)
        return a[tuple(idx)]

    return jnp.concatenate([cut(0, 512), cut(768, 1792), cut(512, 768)], axis=axis)


def _in0_from_kernel(a, axis):
    idx = [slice(None)] * a.ndim

    def cut(lo, hi):
        idx[axis] = slice(lo, hi)
        return a[tuple(idx)]

    a = jnp.concatenate([cut(0, 512), cut(1536, 1792), cut(512, 1536)], axis=axis)
    return _perm_heads(a, Q_INV, axis)


def _f32_as_u16_rows(v, rows):
    bits = lax.bitcast_convert_type(v, jnp.uint16).reshape(-1)
    return jnp.pad(bits, (0, rows * D - bits.shape[0])).reshape(rows, D)


def _pad_rows(v, rows):
    v = v.reshape(-1)
    return jnp.pad(v, (0, rows * D - v.shape[0])).reshape(rows, D)


def kernel(x, mix_norm, a_w_in, a_b_in, a_sinks, a_conv_w, a_conv_b, a_cln_g, a_cln_b, a_w_out, c_w_in, c_w_pool, c_pool_scale, c_sln_g, c_sln_b, c_w_s, c_b_s, c_w_out, ffn_norm, ffn_w_gate, ffn_w_up, ffn_w_down, final_norm, loss_target, m_mix_norm, m_a_w_in, m_a_b_in, m_a_sinks, m_a_conv_w, m_a_conv_b, m_a_cln_g, m_a_cln_b, m_a_w_out, m_c_w_in, m_c_w_pool, m_c_pool_scale, m_c_sln_g, m_c_sln_b, m_c_w_s, m_c_b_s, m_c_w_out, m_ffn_norm, m_ffn_w_gate, m_ffn_w_up, m_ffn_w_down, m_final_norm, v_mix_norm, v_a_w_in, v_a_b_in, v_a_sinks, v_a_conv_w, v_a_conv_b, v_a_cln_g, v_a_cln_b, v_a_w_out, v_c_w_in, v_c_w_pool, v_c_pool_scale, v_c_sln_g, v_c_sln_b, v_c_w_s, v_c_b_s, v_c_w_out, v_ffn_norm, v_ffn_w_gate, v_ffn_w_up, v_ffn_w_down, v_final_norm):
    bsz, seq, _ = x.shape
    t = bsz * seq
    w_in = dict(mix_norm=mix_norm, a_w_in=a_w_in, a_b_in=a_b_in, a_sinks=a_sinks, a_conv_w=a_conv_w, a_conv_b=a_conv_b,
                a_cln_g=a_cln_g, a_cln_b=a_cln_b, a_w_out=a_w_out, c_w_in=c_w_in, c_w_pool=c_w_pool,
                c_pool_scale=c_pool_scale, c_sln_g=c_sln_g, c_sln_b=c_sln_b, c_w_s=c_w_s, c_b_s=c_b_s, c_w_out=c_w_out,
                ffn_norm=ffn_norm, ffn_w_gate=ffn_w_gate, ffn_w_up=ffn_w_up, ffn_w_down=ffn_w_down, final_norm=final_norm)
    m_in = dict(mix_norm=m_mix_norm, a_w_in=m_a_w_in, a_b_in=m_a_b_in, a_sinks=m_a_sinks, a_conv_w=m_a_conv_w,
                a_conv_b=m_a_conv_b, a_cln_g=m_a_cln_g, a_cln_b=m_a_cln_b, a_w_out=m_a_w_out, c_w_in=m_c_w_in,
                c_w_pool=m_c_w_pool, c_pool_scale=m_c_pool_scale, c_sln_g=m_c_sln_g, c_sln_b=m_c_sln_b, c_w_s=m_c_w_s,
                c_b_s=m_c_b_s, c_w_out=m_c_w_out, ffn_norm=m_ffn_norm, ffn_w_gate=m_ffn_w_gate, ffn_w_up=m_ffn_w_up,
                ffn_w_down=m_ffn_w_down, final_norm=m_final_norm)
    v_in = dict(mix_norm=v_mix_norm, a_w_in=v_a_w_in, a_b_in=v_a_b_in, a_sinks=v_a_sinks, a_conv_w=v_a_conv_w,
                a_conv_b=v_a_conv_b, a_cln_g=v_a_cln_g, a_cln_b=v_a_cln_b, a_w_out=v_a_w_out, c_w_in=v_c_w_in,
                c_w_pool=v_c_w_pool, c_pool_scale=v_c_pool_scale, c_sln_g=v_c_sln_g, c_sln_b=v_c_sln_b, c_w_s=v_c_w_s,
                c_b_s=v_c_b_s, c_w_out=v_c_w_out, ffn_norm=v_ffn_norm, ffn_w_gate=v_ffn_w_gate, ffn_w_up=v_ffn_w_up,
                ffn_w_down=v_ffn_w_down, final_norm=v_final_norm)

    big = [a_w_in[0].T, a_w_out[0], c_w_in[0].T, c_w_out[0]]
    for l in range(2):
        big += [ffn_w_gate[l].T, ffn_w_up[l].T, ffn_w_down[l]]
    small = jnp.concatenate([a_conv_w[0].reshape(-1), c_pool_scale[0], c_sln_g[0], c_sln_b[0]])
    big_bits = lax.bitcast_convert_type(jnp.concatenate(big, axis=0).astype(BF16), jnp.uint16)
    gathered = _all_gather(jnp.concatenate([big_bits, _f32_as_u16_rows(small, W_MISC_ROWS)], axis=0), "gather_weights")
    full, off = {}, 0
    for nm, rows in W_ROWS:
        full[nm] = lax.bitcast_convert_type(gathered[:, off:off + rows].reshape(N_DEV * rows, D), BF16)
        off += rows
    small_all = lax.bitcast_convert_type(
        gathered[:, BIG_ROWS:].reshape(N_DEV, -1)[:, :2 * SMALL_SHARD].reshape(N_DEV, SMALL_SHARD, 2), F32)
    conv_w = small_all[:, :31 * 64].reshape(N_DEV, 31, 64).transpose(1, 0, 2).reshape(31, 512)
    conv_w = jnp.pad(conv_w, ((0, HALO - CONV_K), (0, 0)))
    pool_scale = small_all[:, 31 * 64:31 * 64 + 64].reshape(1, 512)
    sln_g = small_all[:, 31 * 64 + 64:31 * 64 + 128].reshape(1, 512)
    sln_b = small_all[:, 31 * 64 + 128:].reshape(1, 512)

    wt_in0 = _in0_to_kernel(full["a_in"], 0)
    b_in0 = _in0_to_kernel(a_b_in, 1)
    w_out0 = _perm_heads(full["a_out"], Q_PERM, 0)
    wt_in1, w_out1 = full["c_in"], full["c_out"]
    b_rows = jnp.broadcast_to(c_b_s[0][:, :, None], (4, 128, 128))
    conv_b, cln_g, cln_b = a_conv_b, a_cln_g, a_cln_b

    h0 = x.reshape(t, D)
    target = loss_target.reshape(t, D)
    z0, hn0 = _norm_proj(h0, mix_norm[0:1], wt_in0, b_in0, "in_proj0")
    attn = _attn_fwd(z0, a_sinks, bsz, "attn_fwd")
    conv = _conv_fwd(z0, conv_w, conv_b, cln_g, cln_b, bsz, "conv_fwd")
    h1 = _out_proj(attn, conv, w_out0, h0, "out_proj0")
    h2, hnf0, gate0, up0 = _ffn_fwd(h1, ffn_norm[0:1], full["g0"], full["u0"], full["d0"], "ffn_fwd0")
    z1, hn1 = _norm_proj(h2, mix_norm[1:2], wt_in1, None, "in_proj1")
    pool = _pool_fwd(z1, c_w_pool[0], pool_scale, bsz, "pool_fwd")
    sgu = _sgu_fwd(z1, sln_g, sln_b, c_w_s[0], b_rows, "sgu_fwd")
    h3 = _out_proj(pool, sgu, w_out1, h2, "out_proj1")
    h4, hnf1, gate1, up1 = _ffn_fwd(h3, ffn_norm[1:2], full["g1"], full["u1"], full["d1"], "ffn_fwd1")

    dh4, d_final_norm, loss_part = _loss_head(h4, final_norm.reshape(1, D), target, "loss_head")
    dh3, dgate1, dup1, act1, d_fn1 = _ffn_bwd(dh4, h3, ffn_norm[1:2], gate1, up1, full["g1"], full["u1"], full["d1"], "ffn_bwd1")
    gw = {"d1": _mm_tn(act1, dh4, "dw_down1"), "g1": _mm_tn(dgate1, hnf1, "dw_gate1"), "u1": _mm_tn(dup1, hnf1, "dw_up1")}
    dmix1 = _dmix(dh3, w_out1, "dmix1")
    gw["c_out"] = jnp.concatenate([_mm_tn(pool, dh3, "dw_out1_pool"), _mm_tn(sgu, dh3, "dw_out1_sgu")], axis=0)
    dzp, d_w_pool, d_pool_scale = _pool_bwd(z1, dmix1, c_w_pool[0], pool_scale, bsz, "pool_bwd")
    dzu, dzv, d_w_s, d_b_s, d_sln_g, d_sln_b = _sgu_bwd(z1, dmix1, sln_g, sln_b, c_w_s[0], b_rows, "sgu_bwd")
    dh2, d_mn1 = _proj_bwd_norm([(dzp, 0), (dzu, 512), (dzv, 1024)], wt_in1, h2, dh3, mix_norm[1:2], "in_proj1_bwd")
    gw["c_in"] = jnp.concatenate([_mm_tn(dzp, hn1, "dw_in1_pool"), _mm_tn(dzu, hn1, "dw_in1_u"),
                                  _mm_tn(dzv, hn1, "dw_in1_v")], axis=0)
    dh1, dgate0, dup0, act0, d_fn0 = _ffn_bwd(dh2, h1, ffn_norm[0:1], gate0, up0, full["g0"], full["u0"], full["d0"], "ffn_bwd0")
    gw["d0"] = _mm_tn(act0, dh2, "dw_down0")
    gw["g0"] = _mm_tn(dgate0, hnf0, "dw_gate0")
    gw["u0"] = _mm_tn(dup0, hnf0, "dw_up0")
    dmix0 = _dmix(dh1, w_out0, "dmix0")
    gw["a_out"] = _perm_heads(jnp.concatenate([_mm_tn(attn, dh1, "dw_out0_attn"), _mm_tn(conv, dh1, "dw_out0_conv")],
                                              axis=0), Q_INV, 0)
    dq, dkv, d_sink_row, d_bq, d_bkv = _attn_bwd(z0, dmix0, a_sinks, bsz, "attn_bwd")
    dca, dcg, d_conv_w, d_conv_b, d_cln_g, d_cln_b, d_ba, d_bg = _conv_bwd(z0, dmix0, conv_w, conv_b, cln_g, cln_b, bsz, "conv_bwd")
    dx, d_mn0 = _proj_bwd_norm([(dq, 0), (dca, 512), (dcg, 1024), (dkv, 1536)], wt_in0, h0, dh1, mix_norm[0:1], "in_proj0_bwd")
    gw["a_in"] = _in0_from_kernel(jnp.concatenate(
        [_mm_tn(dq, hn0, "dw_in0_q"), _mm_tn(dca, hn0, "dw_in0_a"), _mm_tn(dcg, hn0, "dw_in0_g"),
         _mm_tn(dkv, hn0, "dw_in0_kv")], axis=0), 0)
    d_b_in = _in0_from_kernel(jnp.concatenate([d_bq, d_ba, d_bg, d_bkv], axis=1), 1)

    rep = dict(mix_norm=jnp.concatenate([d_mn0, d_mn1], axis=0), a_b_in=d_b_in, a_sinks=d_sink_row[:, :8],
               a_conv_b=d_conv_b, a_cln_g=d_cln_g, a_cln_b=d_cln_b, c_w_pool=d_w_pool[None], c_w_s=d_w_s[None],
               c_b_s=d_b_s[None], ffn_norm=jnp.concatenate([d_fn0, d_fn1], axis=0), final_norm=d_final_norm.reshape(D))
    rep_flat = jnp.concatenate([rep[nm].reshape(-1) for nm in REP_NAMES])
    rep_flat = jnp.pad(rep_flat, (0, N_DEV * REP_ROWS * D - rep_flat.shape[0])).reshape(N_DEV, REP_ROWS, D)
    small_g = jnp.concatenate([
        d_conv_w[:CONV_K].reshape(31, N_DEV, 64).transpose(1, 0, 2).reshape(N_DEV, 31 * 64),
        d_pool_scale.reshape(N_DEV, 64), d_sln_g.reshape(N_DEV, 64), d_sln_b.reshape(N_DEV, 64)], axis=1)
    small_g = jnp.pad(small_g, ((0, 0), (0, G_SMALL_ROWS * D - SMALL_SHARD))).reshape(N_DEV, G_SMALL_ROWS, D)
    g_big = jnp.concatenate([gw[nm].reshape(N_DEV, rows, D) for nm, rows in W_ROWS], axis=1)
    g_misc = jnp.concatenate([small_g, rep_flat], axis=1)
    landed, kept = _pair_exchange([g_big, g_misc], "grad_pair_exchange")
    part_big = _sum_rows([kept[0].reshape(4 * BIG_ROWS, D), landed[0].reshape(4 * BIG_ROWS, D)], BF16, "grad_pair_sum")
    part_misc = _sum_rows([kept[1].reshape(4 * G_MISC_ROWS, D), landed[1].reshape(4 * G_MISC_ROWS, D)], F32,
                          "grad_pair_sum_tail")
    arrived = _chip_exchange([part_big.reshape(4, BIG_ROWS, D), part_misc.reshape(4, G_MISC_ROWS, D)], "grad_chip_exchange")
    g_shard = _sum_slabs(arrived[0], "grad_chip_sum")
    g_tail = _sum_slabs(arrived[1], "grad_chip_sum_tail")
    rep_all = _all_gather(g_tail[G_SMALL_ROWS:], "gather_replicated_grads").reshape(-1)

    grad, off = {}, 0
    for nm, rows in W_ROWS:
        grad[nm] = g_shard[off:off + rows]
        off += rows
    small_r = g_tail[:G_SMALL_ROWS].reshape(-1)[:SMALL_SHARD]
    g_out = dict(
        a_w_in=grad["a_in"].T[None], a_w_out=grad["a_out"][None], c_w_in=grad["c_in"].T[None], c_w_out=grad["c_out"][None],
        ffn_w_gate=jnp.stack([grad["g0"].T, grad["g1"].T]), ffn_w_up=jnp.stack([grad["u0"].T, grad["u1"].T]),
        ffn_w_down=jnp.stack([grad["d0"], grad["d1"]]), a_conv_w=small_r[:31 * 64].reshape(1, 31, 64),
        c_pool_scale=small_r[31 * 64:31 * 64 + 64].reshape(1, 64), c_sln_g=small_r[31 * 64 + 64:31 * 64 + 128].reshape(1, 64),
        c_sln_b=small_r[31 * 64 + 128:].reshape(1, 64))
    off = 0
    for nm in REP_NAMES:
        n = int(np.prod(w_in[nm].shape))
        g_out[nm] = rep_all[off:off + n].reshape(w_in[nm].shape)
        off += n

    names = list(w_in)
    delta, new_m, new_v = {}, {}, {}
    for nm in ("a_w_in", "a_w_out", "c_w_in", "c_w_out", "ffn_w_gate", "ffn_w_up", "ffn_w_down"):
        shp = w_in[nm].shape
        two_d = (shp[0] * shp[1], shp[2])
        res = _adamw(w_in[nm].reshape(two_d), g_out[nm].reshape(two_d), m_in[nm].reshape(two_d), v_in[nm].reshape(two_d),
                     "adamw_" + nm)
        delta[nm], new_m[nm], new_v[nm] = (r.reshape(shp) for r in res)
    for group, rows, label in ((("a_conv_w", "c_pool_scale", "c_sln_g", "c_sln_b"), G_SMALL_ROWS, "adamw_small_sharded"),
                               (REP_NAMES, N_DEV * REP_ROWS, "adamw_replicated")):
        flat = [_pad_rows(jnp.concatenate([d[nm].reshape(-1) for nm in group]), rows) for d in (w_in, g_out, m_in, v_in)]
        res = [r.reshape(-1) for r in _adamw(*flat, label)]
        off = 0
        for nm in group:
            n = int(np.prod(w_in[nm].shape))
            delta[nm], new_m[nm], new_v[nm] = (r[off:off + n].reshape(w_in[nm].shape) for r in res)
            off += n

    loss = lax.psum(loss_part[0, 0], ("x", "y", "c"))
    grad_x = dx.reshape(bsz, seq, D)
    return (loss, grad_x, *[g_out[nm] for nm in names], *[delta[nm] for nm in names],
            *[new_m[nm] for nm in names], *[new_v[nm] for nm in names])
```

```python
import functools

import jax
import jax.numpy as jnp
import numpy as np
from jax import lax
from jax.experimental import pallas as pl
from jax.experimental.pallas import tpu as pltpu

F32 = jnp.float32
BF16 = jnp.bfloat16
MESH = pl.DeviceIdType.MESH

D = 1024
N_DEV = 8
EPS = 1e-5
HEAD_PAIRS = 4
ATT_BLK = 128
CONV_K = 31
HALO = 32
D_FF = 2816
FF_TILE = 256
IN0 = 1792
IN1 = 1536
POOL_WINDOWS = (2, 4, 8, 16)
SGU_CHUNK = 128
GELU_C = 0.7978845608028654
GELU_A = 0.044715
ADAM_LR, ADAM_B1, ADAM_B2, ADAM_EPS, ADAM_WD, ADAM_STEP = 0.001, 0.9, 0.999, 1e-08, 0.01, 10
VMEM_LIMIT = 56 << 20

SMALL_SHARD = 31 * 64 + 3 * 64
W_MISC_ROWS = 16
G_MISC_ROWS = 32
G_SMALL_ROWS = 8
REP_ROWS = G_MISC_ROWS - G_SMALL_ROWS
REP_NAMES = ("mix_norm", "a_b_in", "a_sinks", "a_conv_b", "a_cln_g", "a_cln_b", "c_w_pool", "c_w_s", "c_b_s",
             "ffn_norm", "final_norm")
Q_PERM = (0, 4, 1, 5, 2, 6, 3, 7)


def _params(*sem):
    return pltpu.CompilerParams(dimension_semantics=sem, vmem_limit_bytes=VMEM_LIMIT)


def _nn(a, b):
    return jnp.dot(a, b, preferred_element_type=F32)


def _nt(a, b):
    return lax.dot_general(a, b, (((1,), (1,)), ((), ())), preferred_element_type=F32)


def _tn(a, b):
    return lax.dot_general(a, b, (((0,), (0,)), ((), ())), preferred_element_type=F32)


def _tile(n, want=512):
    t = min(want, n)
    assert n % t == 0, (n, t)
    return t


def _seq_tile(s):
    return 512 if s >= 1024 else s // 2


def _rms(x, g):
    r = lax.rsqrt(jnp.mean(x * x, axis=-1, keepdims=True) + EPS)
    return x * r * g, r


def _rms_bwd(x, g, d_y):
    r = lax.rsqrt(jnp.mean(x * x, axis=-1, keepdims=True) + EPS)
    xr = x * r
    u = d_y * g
    d_x = r * (u - xr * jnp.mean(u * xr, axis=-1, keepdims=True))
    return d_x, jnp.sum(d_y * xr, axis=0, keepdims=True)


def _ln(y, g, b):
    mu = jnp.mean(y, axis=-1, keepdims=True)
    yc = y - mu
    rstd = lax.rsqrt(jnp.mean(yc * yc, axis=-1, keepdims=True) + EPS)
    xhat = yc * rstd
    return xhat * g + b, xhat, rstd


def _ln_bwd(d_o, xhat, rstd, g):
    dxh = d_o * g
    return rstd * (dxh - jnp.mean(dxh, axis=-1, keepdims=True) - xhat * jnp.mean(dxh * xhat, axis=-1, keepdims=True))


def _gelu(x):
    th = jnp.tanh(GELU_C * (x + GELU_A * x * x * x))
    return 0.5 * x * (1.0 + th), th


def _gelu_grad(x, th):
    return 0.5 * (1.0 + th) + 0.5 * x * (1.0 - th * th) * GELU_C * (1.0 + 3.0 * GELU_A * x * x)


def _row(c):
    return pl.BlockSpec((1, c), lambda *_: (0, 0))


def _full(shape):
    return pl.BlockSpec(shape, lambda *_: (0,) * len(shape))


def _norm_proj(h, g, wt, bias, name):
    t, n = h.shape[0], wt.shape[0]
    tm = _tile(t)
    has_bias = bias is not None

    def body(*refs):
        h_ref, g_ref, wt_ref = refs[:3]
        z_ref, hn_ref = refs[-2:]
        hn = _rms(h_ref[...], g_ref[...])[0].astype(BF16)
        hn_ref[...] = hn
        z = _nt(hn, wt_ref[...])
        if has_bias:
            z = z + refs[3][...]
        z_ref[...] = z.astype(BF16)

    in_specs = [pl.BlockSpec((tm, D), lambda i: (i, 0)), _row(D), _full((n, D))]
    args = [h, g, wt]
    if has_bias:
        in_specs.append(_row(n))
        args.append(bias)
    return pl.pallas_call(
        body, name=name, grid=(t // tm,), in_specs=in_specs,
        out_specs=[pl.BlockSpec((tm, n), lambda i: (i, 0)), pl.BlockSpec((tm, D), lambda i: (i, 0))],
        out_shape=[jax.ShapeDtypeStruct((t, n), BF16), jax.ShapeDtypeStruct((t, D), BF16)],
        compiler_params=_params("parallel"))(*args)


def _out_proj(a, b, w, h, name):
    t = h.shape[0]
    tm = _tile(t)

    def body(a_ref, b_ref, wa_ref, wb_ref, h_ref, o_ref):
        o_ref[...] = h_ref[...] + _nn(a_ref[...], wa_ref[...]) + _nn(b_ref[...], wb_ref[...])

    half = pl.BlockSpec((tm, 512), lambda i: (i, 0))
    return pl.pallas_call(
        body, name=name, grid=(t // tm,),
        in_specs=[half, half, pl.BlockSpec((512, D), lambda i: (0, 0)), pl.BlockSpec((512, D), lambda i: (1, 0)),
                  pl.BlockSpec((tm, D), lambda i: (i, 0))],
        out_specs=pl.BlockSpec((tm, D), lambda i: (i, 0)), out_shape=jax.ShapeDtypeStruct((t, D), F32),
        compiler_params=_params("parallel"))(a, b, w, w, h)


def _dmix(dh, w, name):
    t = dh.shape[0]
    tm = _tile(t)

    def body(dh_ref, w_ref, o_ref):
        o_ref[...] = _nt(dh_ref[...].astype(BF16), w_ref[...]).astype(BF16)

    return pl.pallas_call(
        body, name=name, grid=(t // tm,), in_specs=[pl.BlockSpec((tm, D), lambda i: (i, 0)), _full((D, D))],
        out_specs=pl.BlockSpec((tm, D), lambda i: (i, 0)), out_shape=jax.ShapeDtypeStruct((t, D), BF16),
        compiler_params=_params("parallel"))(dh, w)


def _ffn_fwd(h, g, wtg, wtu, wd, name):
    t = h.shape[0]
    tm, tf = _tile(t), FF_TILE
    nf = D_FF // tf

    def body(h_ref, g_ref, wtg_ref, wtu_ref, wd_ref, o_ref, hn_ref, gate_ref, up_ref):
        @pl.when(pl.program_id(1) == 0)
        def _():
            x = h_ref[...]
            hn_ref[...] = _rms(x, g_ref[...])[0].astype(BF16)
            o_ref[...] = x

        hn = hn_ref[...]
        gate = _nt(hn, wtg_ref[...])
        up = _nt(hn, wtu_ref[...])
        gate_ref[...] = gate.astype(BF16)
        up_ref[...] = up.astype(BF16)
        act = (gate * jax.nn.sigmoid(gate) * up).astype(BF16)
        o_ref[...] += _nn(act, wd_ref[...])

    tok = pl.BlockSpec((tm, D), lambda i, f: (i, 0))
    wsp = pl.BlockSpec((tf, D), lambda i, f: (f, 0))
    mid = pl.BlockSpec((tm, tf), lambda i, f: (i, f))
    return pl.pallas_call(
        body, name=name, grid=(t // tm, nf), in_specs=[tok, _row(D), wsp, wsp, wsp],
        out_specs=[tok, tok, mid, mid],
        out_shape=[jax.ShapeDtypeStruct((t, D), F32), jax.ShapeDtypeStruct((t, D), BF16),
                   jax.ShapeDtypeStruct((t, D_FF), BF16), jax.ShapeDtypeStruct((t, D_FF), BF16)],
        compiler_params=_params("parallel", "arbitrary"))(h, g, wtg, wtu, wd)


def _ffn_bwd(dh, h, g, gate, up, wtg, wtu, wd, name):
    t = h.shape[0]
    tm, tf = _tile(t), FF_TILE
    nf = D_FF // tf

    def body(dh_ref, h_ref, g_ref, gate_ref, up_ref, wtg_ref, wtu_ref, wd_ref,
             dhin_ref, dgate_ref, dup_ref, act_ref, dg_ref, dhb):
        i, f = pl.program_id(0), pl.program_id(1)

        @pl.when(f == 0)
        def _():
            dhb[...] = dh_ref[...].astype(BF16)
            dhin_ref[...] = jnp.zeros_like(dhin_ref)

        @pl.when((i == 0) & (f == 0))
        def _():
            dg_ref[...] = jnp.zeros_like(dg_ref)

        dact = _nt(dhb[...], wd_ref[...])
        gt = gate_ref[...].astype(F32)
        u = up_ref[...].astype(F32)
        sg = jax.nn.sigmoid(gt)
        sil = gt * sg
        act_ref[...] = (sil * u).astype(BF16)
        dup = (dact * sil).astype(BF16)
        dgate = (dact * u * sg * (1.0 + gt * (1.0 - sg))).astype(BF16)
        dup_ref[...] = dup
        dgate_ref[...] = dgate
        dhin_ref[...] += _nn(dgate, wtg_ref[...]) + _nn(dup, wtu_ref[...])

        @pl.when(f == nf - 1)
        def _():
            d_x, d_g = _rms_bwd(h_ref[...], g_ref[...], dhin_ref[...])
            dhin_ref[...] = dh_ref[...] + d_x
            dg_ref[...] += d_g

    tok = pl.BlockSpec((tm, D), lambda i, f: (i, 0))
    wsp = pl.BlockSpec((tf, D), lambda i, f: (f, 0))
    mid = pl.BlockSpec((tm, tf), lambda i, f: (i, f))
    mid_shape = jax.ShapeDtypeStruct((t, D_FF), BF16)
    return pl.pallas_call(
        body, name=name, grid=(t // tm, nf), in_specs=[tok, tok, _row(D), mid, mid, wsp, wsp, wsp],
        out_specs=[tok, mid, mid, mid, _row(D)],
        out_shape=[jax.ShapeDtypeStruct((t, D), F32), mid_shape, mid_shape, mid_shape, jax.ShapeDtypeStruct((1, D), F32)],
        scratch_shapes=[pltpu.VMEM((tm, D), BF16)],
        compiler_params=_params("arbitrary", "arbitrary"))(dh, h, g, gate, up, wtg, wtu, wd)


def _proj_bwd_norm(pieces, wt, h, dh, g, name):
    t = h.shape[0]
    tm = _tile(t)
    n_p = len(pieces)

    def body(*refs):
        p_refs, w_refs = refs[:n_p], refs[n_p:2 * n_p]
        h_ref, dh_ref, g_ref, o_ref, dg_ref = refs[2 * n_p:]

        @pl.when(pl.program_id(0) == 0)
        def _():
            dg_ref[...] = jnp.zeros_like(dg_ref)

        d_hn = _nn(p_refs[0][...], w_refs[0][...])
        for p_ref, w_ref in zip(p_refs[1:], w_refs[1:]):
            d_hn = d_hn + _nn(p_ref[...], w_ref[...])
        d_x, d_g = _rms_bwd(h_ref[...], g_ref[...], d_hn)
        o_ref[...] = dh_ref[...] + d_x
        dg_ref[...] += d_g

    tok = pl.BlockSpec((tm, D), lambda i: (i, 0))
    in_specs = [pl.BlockSpec((tm, a.shape[1]), lambda i: (i, 0)) for a, _ in pieces]
    for a, off in pieces:
        w = a.shape[1]
        assert off % w == 0
        in_specs.append(pl.BlockSpec((w, D), functools.partial(lambda i, blk: (blk, 0), blk=off // w)))
    in_specs += [tok, tok, _row(D)]
    return pl.pallas_call(
        body, name=name, grid=(t // tm,), in_specs=in_specs, out_specs=[tok, _row(D)],
        out_shape=[jax.ShapeDtypeStruct((t, D), F32), jax.ShapeDtypeStruct((1, D), F32)],
        compiler_params=_params("arbitrary"))(*[a for a, _ in pieces], *([wt] * n_p), h, dh, g)


def _mm_tn(a, b, name):
    t, n = a.shape
    k = b.shape[1]
    tn = n if n <= 1024 else n // 2
    tt = _tile(t)
    nt = t // tt

    def body(a_ref, b_ref, o_ref, acc):
        s = pl.program_id(1)

        @pl.when(s == 0)
        def _():
            acc[...] = jnp.zeros_like(acc)

        acc[...] += _tn(a_ref[...], b_ref[...].astype(BF16))

        @pl.when(s == nt - 1)
        def _():
            o_ref[...] = acc[...].astype(BF16)

    return pl.pallas_call(
        body, name=name, grid=(n // tn, nt),
        in_specs=[pl.BlockSpec((tt, tn), lambda j, s: (s, j)), pl.BlockSpec((tt, k), lambda j, s: (s, 0))],
        out_specs=pl.BlockSpec((tn, k), lambda j, s: (j, 0)), out_shape=jax.ShapeDtypeStruct((n, k), BF16),
        scratch_shapes=[pltpu.VMEM((tn, k), F32)],
        compiler_params=_params("parallel", "arbitrary"))(a, b)


def _loss_head(h, g, target, name):
    t = h.shape[0]
    tm = _tile(t)

    def body(h_ref, g_ref, t_ref, dh_ref, dg_ref, loss_ref):
        @pl.when(pl.program_id(0) == 0)
        def _():
            dg_ref[...] = jnp.zeros_like(dg_ref)
            loss_ref[...] = jnp.zeros_like(loss_ref)

        x = h_ref[...]
        gv = g_ref[...]
        err = _rms(x, gv)[0] - t_ref[...]
        per_tok = jnp.mean(err * err, axis=-1, keepdims=True)
        loss_ref[...] += 0.5 * jnp.sum(per_tok, axis=0, keepdims=True)
        d_x, d_g = _rms_bwd(x, gv, err * (1.0 / D))
        dh_ref[...] = d_x
        dg_ref[...] += d_g

    tok = pl.BlockSpec((tm, D), lambda i: (i, 0))
    return pl.pallas_call(
        body, name=name, grid=(t // tm,), in_specs=[tok, _row(D), tok],
        out_specs=[tok, _row(D), _row(1)],
        out_shape=[jax.ShapeDtypeStruct((t, D), F32), jax.ShapeDtypeStruct((1, D), F32), jax.ShapeDtypeStruct((1, 1), F32)],
        compiler_params=_params("arbitrary"))(h, g, target)


def _attn_valid(first):
    qi = lax.broadcasted_iota(jnp.int32, (ATT_BLK, 2 * ATT_BLK), 0)
    r = lax.broadcasted_iota(jnp.int32, (ATT_BLK, 2 * ATT_BLK), 1)
    dist = qi + ATT_BLK - r
    return (dist >= 0) & (dist < ATT_BLK) & ((r >= ATT_BLK) | jnp.logical_not(first))


def _attn_probs(qm, kpair, sink, valid):
    s = jnp.where(valid, _nt(qm, kpair), -1e30)
    m = jnp.maximum(jnp.max(s, axis=-1, keepdims=True), sink)
    p = jnp.exp(s - m)
    es = jnp.exp(sink - m)
    inv = 1.0 / (jnp.sum(p, axis=-1, keepdims=True) + es)
    return p * inv, es * inv


def _attn_specs(nb, order):
    q = pl.BlockSpec((ATT_BLK, 512), lambda b, j: (b * nb + order(j), 0))
    kvc = pl.BlockSpec((ATT_BLK, 256), lambda b, j: (b * nb + order(j), 6))
    kvp = pl.BlockSpec((ATT_BLK, 256), lambda b, j: (jnp.maximum(b * nb + order(j) - 1, 0), 6))
    return q, kvc, kvp


def _window_kv(kvc_ref, kvp_ref):
    kvc, kvp = kvc_ref[...], kvp_ref[...]
    kpair = jnp.concatenate([kvp[:, :128], kvc[:, :128]], axis=0)
    vpair = jnp.concatenate([kvp[:, 128:], kvc[:, 128:]], axis=0)
    return kpair, vpair


def _attn_fwd(z0, sinks, bsz, name):
    t = z0.shape[0]
    nb = t // bsz // ATT_BLK

    def body(s_ref, q_ref, kvc_ref, kvp_ref, o_ref):
        valid = _attn_valid(pl.program_id(1) == 0)
        kpair, vpair = _window_kv(kvc_ref, kvp_ref)
        lo = lax.broadcasted_iota(jnp.int32, (ATT_BLK, 128), 1) < 64
        for g in range(HEAD_PAIRS):
            qs = q_ref[:, g * 128:(g + 1) * 128] * 0.125
            outs = []
            for kh in range(2):
                qm = jnp.where(lo if kh == 0 else ~lo, qs, 0.0).astype(BF16)
                p, _ = _attn_probs(qm, kpair, s_ref[0, kh * 4 + g], valid)
                outs.append(_nn(p.astype(BF16), vpair))
            o_ref[:, g * 128:(g + 1) * 128] = jnp.where(lo, outs[0], outs[1]).astype(BF16)

    q, kvc, kvp = _attn_specs(nb, lambda j: j)
    return pl.pallas_call(
        body, name=name, grid=(bsz, nb),
        in_specs=[pl.BlockSpec(memory_space=pltpu.SMEM), q, kvc, kvp],
        out_specs=pl.BlockSpec((ATT_BLK, 512), lambda b, j: (b * nb + j, 0)),
        out_shape=jax.ShapeDtypeStruct((t, 512), BF16),
        compiler_params=_params("parallel", "parallel"))(sinks, z0, z0, z0)


def _attn_bwd(z0, dmix, sinks, bsz, name):
    t = z0.shape[0]
    nb = t // bsz // ATT_BLK

    def body(s_ref, q_ref, kvc_ref, kvp_ref, do_ref, dq_ref, dkv_ref, dsink_ref, dbq_ref, dbkv_ref, carry):
        b, j = pl.program_id(0), pl.program_id(1)

        @pl.when(j == 0)
        def _():
            carry[...] = jnp.zeros_like(carry)

        @pl.when((b == 0) & (j == 0))
        def _():
            dsink_ref[...] = jnp.zeros_like(dsink_ref)
            dbq_ref[...] = jnp.zeros_like(dbq_ref)
            dbkv_ref[...] = jnp.zeros_like(dbkv_ref)

        valid = _attn_valid(j == nb - 1)
        kpair, vpair = _window_kv(kvc_ref, kvp_ref)
        lo = lax.broadcasted_iota(jnp.int32, (ATT_BLK, 128), 1) < 64
        lane = lax.broadcasted_iota(jnp.int32, (1, 128), 1)
        dk = jnp.zeros((2 * ATT_BLK, 128), F32)
        dv = jnp.zeros((2 * ATT_BLK, 128), F32)
        dsink = jnp.zeros((1, 128), F32)
        for g in range(HEAD_PAIRS):
            qs = q_ref[:, g * 128:(g + 1) * 128] * 0.125
            do = do_ref[:, g * 128:(g + 1) * 128]
            dqs = []
            for kh in range(2):
                msk = lo if kh == 0 else ~lo
                qm = jnp.where(msk, qs, 0.0).astype(BF16)
                dom = jnp.where(msk, do, 0.0).astype(BF16)
                p, ps = _attn_probs(qm, kpair, s_ref[0, kh * 4 + g], valid)
                dp = _nt(dom, vpair)
                delta = jnp.sum(p * dp, axis=-1, keepdims=True)
                ds = (p * (dp - delta)).astype(BF16)
                dqs.append(_nn(ds, kpair))
                dk = dk + _tn(ds, qm)
                dv = dv + _tn(p.astype(BF16), dom)
                dsink = dsink + jnp.where(lane == kh * 4 + g, jnp.sum(-ps * delta, axis=0, keepdims=True), 0.0)
            dq = jnp.where(lo, dqs[0], dqs[1]) * 0.125
            dq_ref[:, g * 128:(g + 1) * 128] = dq.astype(BF16)
            dbq_ref[:, g * 128:(g + 1) * 128] += jnp.sum(dq, axis=0, keepdims=True)
        dkv = jnp.concatenate([dk[ATT_BLK:], dv[ATT_BLK:]], axis=1) + carry[...]
        dkv_ref[...] = dkv.astype(BF16)
        dbkv_ref[...] += jnp.sum(dkv, axis=0, keepdims=True)
        carry[...] = jnp.concatenate([dk[:ATT_BLK], dv[:ATT_BLK]], axis=1)
        dsink_ref[...] += dsink

    q, kvc, kvp = _attn_specs(nb, lambda j: nb - 1 - j)
    return pl.pallas_call(
        body, name=name, grid=(bsz, nb),
        in_specs=[pl.BlockSpec(memory_space=pltpu.SMEM), q, kvc, kvp,
                  pl.BlockSpec((ATT_BLK, 512), lambda b, j: (b * nb + nb - 1 - j, 0))],
        out_specs=[pl.BlockSpec((ATT_BLK, 512), lambda b, j: (b * nb + nb - 1 - j, 0)),
                   pl.BlockSpec((ATT_BLK, 256), lambda b, j: (b * nb + nb - 1 - j, 0)), _row(128), _row(512), _row(256)],
        out_shape=[jax.ShapeDtypeStruct((t, 512), BF16), jax.ShapeDtypeStruct((t, 256), BF16),
                   jax.ShapeDtypeStruct((1, 128), F32), jax.ShapeDtypeStruct((1, 512), F32),
                   jax.ShapeDtypeStruct((1, 256), F32)],
        scratch_shapes=[pltpu.VMEM((ATT_BLK, 256), F32)],
        compiler_params=_params("arbitrary", "arbitrary"))(sinks, z0, z0, z0, dmix)


def _seq_specs(ts, nt, t, width, col):
    per = ts // HALO
    cur = pl.BlockSpec((ts, width), lambda b, i: (b * nt + i, col))
    prev = pl.BlockSpec((HALO, width), lambda b, i: (jnp.maximum((b * nt + i) * per - 1, 0), col))
    nxt = pl.BlockSpec((HALO, width), lambda b, i: (jnp.minimum((b * nt + i + 1) * per, t // HALO - 1), col))
    return prev, cur, nxt


def _conv_taps(buf, w, first, rows):
    acc = buf[pl.ds(first, rows), :] * w[0:1, :]
    for k in range(1, CONV_K):
        acc = acc + buf[pl.ds(first + k, rows), :] * w[k:k + 1, :]
    return acc


def _conv_fwd(z0, conv_w, conv_b, ln_g, ln_b, bsz, name):
    t = z0.shape[0]
    s = t // bsz
    ts = _seq_tile(s)
    nt = s // ts

    def body(ap_ref, ac_ref, gp_ref, gc_ref, w_ref, cb_ref, lg_ref, lb_ref, o_ref, hbuf):
        hp = ap_ref[...].astype(F32) * jax.nn.sigmoid(gp_ref[...].astype(F32))
        hbuf[0:HALO, :] = jnp.where(pl.program_id(1) > 0, hp, 0.0)
        hbuf[HALO:HALO + ts, :] = ac_ref[...].astype(F32) * jax.nn.sigmoid(gc_ref[...].astype(F32))
        y = _conv_taps(hbuf, w_ref[...], HALO - (CONV_K - 1), ts) + cb_ref[...]
        o = _ln(y, lg_ref[...], lb_ref[...])[0]
        o_ref[...] = (o * jax.nn.sigmoid(o)).astype(BF16)

    ap, ac, _ = _seq_specs(ts, nt, t, 512, 1)
    gp, gc, _ = _seq_specs(ts, nt, t, 512, 2)
    return pl.pallas_call(
        body, name=name, grid=(bsz, nt),
        in_specs=[ap, ac, gp, gc, _full((HALO, 512)), _row(512), _row(512), _row(512)],
        out_specs=pl.BlockSpec((ts, 512), lambda b, i: (b * nt + i, 0)),
        out_shape=jax.ShapeDtypeStruct((t, 512), BF16),
        scratch_shapes=[pltpu.VMEM((HALO + ts, 512), F32)],
        compiler_params=_params("parallel", "parallel"))(z0, z0, z0, z0, conv_w, conv_b, ln_g, ln_b)


def _conv_bwd(z0, dmix, conv_w, conv_b, ln_g, ln_b, bsz, name):
    t = z0.shape[0]
    s = t // bsz
    ts = _seq_tile(s)
    nt = s // ts
    rr = ts + HALO

    def body(ap_ref, ac_ref, an_ref, gp_ref, gc_ref, gn_ref, dc_ref, dn_ref, w_ref, cb_ref, lg_ref, lb_ref,
             da_ref, dg_ref, dw_ref, dcb_ref, dlg_ref, dlb_ref, dba_ref, dbg_ref, hbuf, dybuf):
        b, i = pl.program_id(0), pl.program_id(1)

        @pl.when((b == 0) & (i == 0))
        def _():
            for ref in (dw_ref, dcb_ref, dlg_ref, dlb_ref, dba_ref, dbg_ref):
                ref[...] = jnp.zeros_like(ref)

        a_c = ac_ref[...].astype(F32)
        sg_c = jax.nn.sigmoid(gc_ref[...].astype(F32))
        hp = ap_ref[...].astype(F32) * jax.nn.sigmoid(gp_ref[...].astype(F32))
        hbuf[0:HALO, :] = jnp.where(i > 0, hp, 0.0)
        hbuf[HALO:HALO + ts, :] = a_c * sg_c
        hbuf[HALO + ts:2 * HALO + ts, :] = an_ref[...].astype(F32) * jax.nn.sigmoid(gn_ref[...].astype(F32))
        w = w_ref[...]
        y = _conv_taps(hbuf, w, HALO - (CONV_K - 1), rr) + cb_ref[...]
        lg = lg_ref[...]
        o, xhat, rstd = _ln(y, lg, lb_ref[...])
        dn = jnp.where(i < nt - 1, dn_ref[...].astype(F32), 0.0)
        dout = jnp.concatenate([dc_ref[...].astype(F32), dn], axis=0)
        sg_o = jax.nn.sigmoid(o)
        d_o = dout * sg_o * (1.0 + o * (1.0 - sg_o))
        dlg_ref[...] += jnp.sum(d_o[:ts] * xhat[:ts], axis=0, keepdims=True)
        dlb_ref[...] += jnp.sum(d_o[:ts], axis=0, keepdims=True)
        dy = _ln_bwd(d_o, xhat, rstd, lg)
        dybuf[...] = dy
        dy_c = dy[:ts]
        dcb_ref[...] += jnp.sum(dy_c, axis=0, keepdims=True)
        for k in range(CONV_K):
            dw_ref[pl.ds(k, 1), :] += jnp.sum(dy_c * hbuf[pl.ds(HALO - (CONV_K - 1) + k, ts), :], axis=0, keepdims=True)
        dh = dybuf[pl.ds(CONV_K - 1, ts), :] * w[0:1, :]
        for k in range(1, CONV_K):
            dh = dh + dybuf[pl.ds(CONV_K - 1 - k, ts), :] * w[k:k + 1, :]
        d_a = dh * sg_c
        d_g = dh * a_c * sg_c * (1.0 - sg_c)
        da_ref[...] = d_a.astype(BF16)
        dg_ref[...] = d_g.astype(BF16)
        dba_ref[...] += jnp.sum(d_a, axis=0, keepdims=True)
        dbg_ref[...] += jnp.sum(d_g, axis=0, keepdims=True)

    ap, ac, an = _seq_specs(ts, nt, t, 512, 1)
    gp, gc, gn = _seq_specs(ts, nt, t, 512, 2)
    _, dc, dn = _seq_specs(ts, nt, t, 512, 1)
    tile = pl.BlockSpec((ts, 512), lambda b, i: (b * nt + i, 0))
    vec = jax.ShapeDtypeStruct((1, 512), F32)
    return pl.pallas_call(
        body, name=name, grid=(bsz, nt),
        in_specs=[ap, ac, an, gp, gc, gn, dc, dn, _full((HALO, 512)), _row(512), _row(512), _row(512)],
        out_specs=[tile, tile, _full((HALO, 512)), _row(512), _row(512), _row(512), _row(512), _row(512)],
        out_shape=[jax.ShapeDtypeStruct((t, 512), BF16), jax.ShapeDtypeStruct((t, 512), BF16),
                   jax.ShapeDtypeStruct((HALO, 512), F32), vec, vec, vec, vec, vec],
        scratch_shapes=[pltpu.VMEM((2 * HALO + ts, 512), F32), pltpu.VMEM((rr, 512), F32)],
        compiler_params=_params("arbitrary", "arbitrary"))(z0, z0, z0, z0, z0, z0, dmix, dmix, conv_w, conv_b, ln_g, ln_b)


def _pooled(pbuf, g, ts, tok):
    w = 2 << g
    cols = slice(128 * g, 128 * (g + 1))
    sm = pbuf[pl.ds(HALO, ts), cols]
    for d in range(1, w):
        sm = sm + pbuf[pl.ds(HALO - d, ts), cols]
    cnt = jnp.minimum(tok + 1, w).astype(F32)
    return sm / cnt - pbuf[pl.ds(HALO, ts), cols]


def _pool_fwd(z1, w_pool, scale, bsz, name):
    t = z1.shape[0]
    s = t // bsz
    ts = _seq_tile(s)
    nt = s // ts

    def body(zp_ref, zc_ref, wp_ref, sc_ref, o_ref, pbuf):
        i = pl.program_id(1)
        pbuf[0:HALO, :] = jnp.where(i > 0, zp_ref[...].astype(F32), 0.0)
        pbuf[HALO:HALO + ts, :] = zc_ref[...].astype(F32)
        tok = i * ts + lax.broadcasted_iota(jnp.int32, (ts, 1), 0)
        for g in range(4):
            cols = slice(128 * g, 128 * (g + 1))
            pooled = _pooled(pbuf, g, ts, tok).astype(BF16)
            o_ref[:, cols] = (_nn(pooled, wp_ref[g].astype(BF16)) * sc_ref[:, cols]).astype(BF16)

    zp, zc, _ = _seq_specs(ts, nt, t, 512, 0)
    return pl.pallas_call(
        body, name=name, grid=(bsz, nt), in_specs=[zp, zc, _full((4, 128, 128)), _row(512)],
        out_specs=pl.BlockSpec((ts, 512), lambda b, i: (b * nt + i, 0)),
        out_shape=jax.ShapeDtypeStruct((t, 512), BF16),
        scratch_shapes=[pltpu.VMEM((HALO + ts, 512), F32)],
        compiler_params=_params("parallel", "parallel"))(z1, z1, w_pool, scale)


def _pool_bwd(z1, dmix, w_pool, scale, bsz, name):
    t = z1.shape[0]
    s = t // bsz
    ts = _seq_tile(s)
    nt = s // ts
    rr = ts + HALO

    def body(zp_ref, zc_ref, dc_ref, dn_ref, wp_ref, sc_ref, dz_ref, dwp_ref, dsc_ref, pbuf, ebuf):
        b, i = pl.program_id(0), pl.program_id(1)

        @pl.when((b == 0) & (i == 0))
        def _():
            dwp_ref[...] = jnp.zeros_like(dwp_ref)
            dsc_ref[...] = jnp.zeros_like(dsc_ref)

        pbuf[0:HALO, :] = jnp.where(i > 0, zp_ref[...].astype(F32), 0.0)
        pbuf[HALO:HALO + ts, :] = zc_ref[...].astype(F32)
        dn = jnp.where(i < nt - 1, dn_ref[...].astype(F32), 0.0)
        dout = jnp.concatenate([dc_ref[...].astype(F32), dn], axis=0)
        tok = i * ts + lax.broadcasted_iota(jnp.int32, (ts, 1), 0)
        tok_r = i * ts + lax.broadcasted_iota(jnp.int32, (rr, 1), 0)
        for g in range(4):
            w = 2 << g
            cols = slice(128 * g, 128 * (g + 1))
            wg = wp_ref[g].astype(BF16)
            pooled = _pooled(pbuf, g, ts, tok).astype(BF16)
            dsc_ref[:, cols] += jnp.sum(dout[:ts, cols] * _nn(pooled, wg), axis=0, keepdims=True)
            dy = (dout[:, cols] * sc_ref[:, cols]).astype(BF16)
            dwp_ref[g] += _tn(pooled, dy[:ts])
            dpl = _nt(dy, wg)
            ebuf[...] = dpl / jnp.minimum(tok_r + 1, w).astype(F32)
            dz = ebuf[pl.ds(0, ts), :] - dpl[:ts]
            for d in range(1, w):
                dz = dz + ebuf[pl.ds(d, ts), :]
            dz_ref[:, cols] = dz.astype(BF16)

    zp, zc, _ = _seq_specs(ts, nt, t, 512, 0)
    _, dc, dn = _seq_specs(ts, nt, t, 512, 0)
    return pl.pallas_call(
        body, name=name, grid=(bsz, nt), in_specs=[zp, zc, dc, dn, _full((4, 128, 128)), _row(512)],
        out_specs=[pl.BlockSpec((ts, 512), lambda b, i: (b * nt + i, 0)), _full((4, 128, 128)), _row(512)],
        out_shape=[jax.ShapeDtypeStruct((t, 512), BF16), jax.ShapeDtypeStruct((4, 128, 128), F32),
                   jax.ShapeDtypeStruct((1, 512), F32)],
        scratch_shapes=[pltpu.VMEM((HALO + ts, 512), F32), pltpu.VMEM((rr, 128), F32)],
        compiler_params=_params("arbitrary", "arbitrary"))(z1, z1, dmix, dmix, w_pool, scale)


def _tril():
    r = lax.broadcasted_iota(jnp.int32, (SGU_CHUNK, SGU_CHUNK), 0)
    c = lax.broadcasted_iota(jnp.int32, (SGU_CHUNK, SGU_CHUNK), 1)
    return r >= c


def _sgu_fwd(z1, ln_g, ln_b, w_s, b_rows, name):
    t = z1.shape[0]
    ts = _tile(t)

    def body(zu_ref, zv_ref, lg_ref, lb_ref, ws_ref, bs_ref, o_ref):
        v = _gelu(zv_ref[...].astype(F32))[0]
        vb = _ln(v, lg_ref[...], lb_ref[...])[0].astype(BF16)
        tril = _tril()
        for g in range(4):
            cols = slice(128 * g, 128 * (g + 1))
            wg = jnp.where(tril, ws_ref[g], 0.0).astype(BF16)
            for c in range(ts // SGU_CHUNK):
                rows = slice(SGU_CHUNK * c, SGU_CHUNK * (c + 1))
                mixed = _nn(wg, vb[rows, cols]) + bs_ref[g]
                o_ref[rows, cols] = (_gelu(zu_ref[rows, cols].astype(F32))[0] * mixed).astype(BF16)

    return pl.pallas_call(
        body, name=name, grid=(t // ts,),
        in_specs=[pl.BlockSpec((ts, 512), lambda i: (i, 1)), pl.BlockSpec((ts, 512), lambda i: (i, 2)),
                  _row(512), _row(512), _full((4, 128, 128)), _full((4, 128, 128))],
        out_specs=pl.BlockSpec((ts, 512), lambda i: (i, 0)), out_shape=jax.ShapeDtypeStruct((t, 512), BF16),
        compiler_params=_params("parallel"))(z1, z1, ln_g, ln_b, w_s, b_rows)


def _sgu_bwd(z1, dmix, ln_g, ln_b, w_s, b_rows, name):
    t = z1.shape[0]
    ts = _tile(t)

    def body(zu_ref, zv_ref, d_ref, lg_ref, lb_ref, ws_ref, bs_ref,
             dzu_ref, dzv_ref, dws_ref, dbs_ref, dlg_ref, dlb_ref, dvbuf):
        @pl.when(pl.program_id(0) == 0)
        def _():
            for ref in (dws_ref, dbs_ref, dlg_ref, dlb_ref):
                ref[...] = jnp.zeros_like(ref)

        zv = zv_ref[...].astype(F32)
        v, thv = _gelu(zv)
        lg = lg_ref[...]
        vln, xhat, rstd = _ln(v, lg, lb_ref[...])
        vb = vln.astype(BF16)
        tril = _tril()
        for g in range(4):
            cols = slice(128 * g, 128 * (g + 1))
            wg = jnp.where(tril, ws_ref[g], 0.0).astype(BF16)
            dws = jnp.zeros((SGU_CHUNK, SGU_CHUNK), F32)
            dbs = jnp.zeros((1, SGU_CHUNK), F32)
            for c in range(ts // SGU_CHUNK):
                rows = slice(SGU_CHUNK * c, SGU_CHUNK * (c + 1))
                vbc = vb[rows, cols]
                mixed = _nn(wg, vbc) + bs_ref[g]
                zu = zu_ref[rows, cols].astype(F32)
                u, thu = _gelu(zu)
                dout = d_ref[rows, cols].astype(F32)
                dzu_ref[rows, cols] = (dout * mixed * _gelu_grad(zu, thu)).astype(BF16)
                dm = dout * u
                dmb = dm.astype(BF16)
                dws = dws + _nt(dmb, vbc)
                dbs = dbs + jnp.sum(dm.T, axis=0, keepdims=True)
                dvbuf[rows, cols] = _tn(wg, dmb)
            dws_ref[g] += jnp.where(tril, dws, 0.0)
            dbs_ref[pl.ds(g, 1), :] += dbs
        dvln = dvbuf[...]
        dlg_ref[...] += jnp.sum(dvln * xhat, axis=0, keepdims=True)
        dlb_ref[...] += jnp.sum(dvln, axis=0, keepdims=True)
        dzv_ref[...] = (_ln_bwd(dvln, xhat, rstd, lg) * _gelu_grad(zv, thv)).astype(BF16)

    tile = pl.BlockSpec((ts, 512), lambda i: (i, 0))
    vec = jax.ShapeDtypeStruct((1, 512), F32)
    return pl.pallas_call(
        body, name=name, grid=(t // ts,),
        in_specs=[pl.BlockSpec((ts, 512), lambda i: (i, 1)), pl.BlockSpec((ts, 512), lambda i: (i, 2)),
                  pl.BlockSpec((ts, 512), lambda i: (i, 1)), _row(512), _row(512), _full((4, 128, 128)),
                  _full((4, 128, 128))],
        out_specs=[tile, tile, _full((4, 128, 128)), _full((4, 128)), _row(512), _row(512)],
        out_shape=[jax.ShapeDtypeStruct((t, 512), BF16), jax.ShapeDtypeStruct((t, 512), BF16),
                   jax.ShapeDtypeStruct((4, 128, 128), F32), jax.ShapeDtypeStruct((4, 128), F32), vec, vec],
        scratch_shapes=[pltpu.VMEM((ts, 512), F32)],
        compiler_params=_params("arbitrary"))(z1, z1, dmix, ln_g, ln_b, w_s, b_rows)


def _row_tile(r):
    for cand in (512, 352, 256, 192, 128, 64, 32, 16, 8):
        if r % cand == 0:
            return cand
    return r


def _sum_slabs(a, name):
    k, r, c = a.shape
    tr = _row_tile(r)

    def body(*refs):
        acc = refs[0][...].astype(F32)
        for ref in refs[1:-1]:
            acc = acc + ref[...].astype(F32)
        refs[-1][...] = acc

    in_specs = [pl.BlockSpec((None, tr, c), functools.partial(lambda i, s: (s, i, 0), s=s)) for s in range(k)]
    return pl.pallas_call(
        body, name=name, grid=(r // tr,), in_specs=in_specs, out_specs=pl.BlockSpec((tr, c), lambda i: (i, 0)),
        out_shape=jax.ShapeDtypeStruct((r, c), F32), compiler_params=_params("parallel"))(*([a] * k))


def _adamw(w, g, m, v, name):
    r, c = w.shape
    tr = _row_tile(r)

    def body(w_ref, g_ref, m_ref, v_ref, d_ref, mo_ref, vo_ref):
        gv = g_ref[...]
        mn = ADAM_B1 * m_ref[...] + (1.0 - ADAM_B1) * gv
        vn = ADAM_B2 * v_ref[...] + (1.0 - ADAM_B2) * (gv * gv)
        m_hat = mn / (1.0 - ADAM_B1 ** ADAM_STEP)
        v_hat = vn / (1.0 - ADAM_B2 ** ADAM_STEP)
        d_ref[...] = -ADAM_LR * (m_hat / (jnp.sqrt(v_hat) + ADAM_EPS) + ADAM_WD * w_ref[...])
        mo_ref[...] = mn
        vo_ref[...] = vn

    spec = pl.BlockSpec((tr, c), lambda i: (i, 0))
    shape = jax.ShapeDtypeStruct((r, c), F32)
    return pl.pallas_call(
        body, name=name, grid=(r // tr,), in_specs=[spec] * 4, out_specs=[spec] * 3, out_shape=[shape] * 3,
        compiler_params=_params("parallel"))(w, g, m, v)


ANY = pl.BlockSpec(memory_space=pl.ANY)


def _all_gather(block, name):
    r, c_dim = block.shape

    def body(x_ref, out_ref, send_sems, recv_sems, local_sem):
        x, y, c = lax.axis_index("x"), lax.axis_index("y"), lax.axis_index("c")
        me, sibling = (x, y, c), (x, y, 1 - c)
        chips = [(1 - x, y), (x, 1 - y), (1 - x, 1 - y)]

        def rows(px, py, pc):
            return out_ref.at[4 * px + 2 * py + pc]

        def copy(k, blk, to, src=None):
            return pltpu.make_async_remote_copy(
                src_ref=rows(*blk) if src is None else src, dst_ref=rows(*blk), send_sem=send_sems.at[k],
                recv_sem=recv_sems.at[k], device_id=to, device_id_type=MESH)

        mine = pltpu.make_async_copy(x_ref, rows(*me), local_sem)
        mine.start()
        first = [copy(0, me, sibling, src=x_ref)]
        first += [copy(1 + j, me, (*chip, c), src=x_ref) for j, chip in enumerate(chips)]
        for cp in first:
            cp.start()
        passed = [copy(4 + j, (*chip, c), sibling) for j, chip in enumerate(chips)]
        for j, chip in enumerate(chips):
            copy(1 + j, (*chip, c), me).wait_recv()
            passed[j].start()
        copy(0, sibling, me).wait_recv()
        for j, chip in enumerate(chips):
            copy(4 + j, (*chip, 1 - c), me).wait_recv()
        for cp in first + passed:
            cp.wait_send()
        mine.wait()

    return pl.pallas_call(
        body, name=name, in_specs=[ANY], out_specs=ANY,
        out_shape=jax.ShapeDtypeStruct((N_DEV, r, c_dim), block.dtype),
        scratch_shapes=[pltpu.SemaphoreType.DMA((7,)), pltpu.SemaphoreType.DMA((7,)), pltpu.SemaphoreType.DMA],
    )(block)


HBM = pl.BlockSpec(memory_space=pltpu.HBM)
SEM = pl.BlockSpec(memory_space=pltpu.SEMAPHORE)
EFFECT = pltpu.SideEffectType.DATAFLOW_SIDE_EFFECTING


def _exchange_copies(scatter, src_refs, land_refs, send_sems, recv_sems, local_sems):
    x, y, c = lax.axis_index("x"), lax.axis_index("y"), lax.axis_index("c")
    me = 4 * x + 2 * y + c
    sends, arrivals, locals_ = [], [], []
    for a, (src, land) in enumerate(zip(src_refs, land_refs)):
        def pick(idx, src=src):
            return src.at[idx] if scatter else src

        locals_.append(pltpu.make_async_copy(pick(me), land.at[me], local_sems.at[a]))
        for r in range(1, N_DEV):
            px = 1 - x if r & 4 else x
            py = 1 - y if r & 2 else y
            pc = 1 - c if r & 1 else c
            peer, s = 4 * px + 2 * py + pc, 7 * a + r - 1
            sends.append(pltpu.make_async_remote_copy(
                src_ref=pick(peer), dst_ref=land.at[me], send_sem=send_sems.at[s], recv_sem=recv_sems.at[s],
                device_id=(px, py, pc), device_id_type=MESH))
            arrivals.append(pltpu.make_async_remote_copy(
                src_ref=pick(peer), dst_ref=land.at[peer], send_sem=send_sems.at[s], recv_sem=recv_sems.at[s],
                device_id=(px, py, pc), device_id_type=MESH))
    return sends, arrivals, locals_


def _exchange_start(srcs, scatter, name):
    n = len(srcs)
    lands = [lax.empty((N_DEV,) + s.shape[-2:], s.dtype) for s in srcs]

    def body(*refs):
        src_refs, land_refs = refs[:n], refs[n:2 * n]
        send_sems, recv_sems, local_sems = refs[2 * n:2 * n + 3]
        token = refs[-1]
        sends, _, locals_ = _exchange_copies(scatter, src_refs, land_refs, send_sems, recv_sems, local_sems)
        for cp in locals_ + sends:
            cp.start()
        token[...] = jnp.zeros_like(token)

    res = pl.pallas_call(
        body, name=name,
        out_shape=[pltpu.SemaphoreType.DMA((7 * n,)), pltpu.SemaphoreType.DMA((7 * n,)), pltpu.SemaphoreType.DMA((n,))]
        + [pltpu.HBM(a.shape, a.dtype) for a in list(srcs) + lands] + [jax.ShapeDtypeStruct((8, 128), F32)],
        in_specs=[HBM] * (2 * n), out_specs=[SEM] * 3 + [HBM] * (2 * n) + [pl.BlockSpec(memory_space=pltpu.VMEM)],
        input_output_aliases={i: 3 + i for i in range(2 * n)},
        compiler_params=pltpu.CompilerParams(has_side_effects=EFFECT),
    )(*[pltpu.with_memory_space_constraint(a, pltpu.HBM) for a in list(srcs) + lands])
    return (n, scatter, res[:3], res[3:3 + 2 * n]), res[-1]


def _exchange_wait(handle, after, name):
    n, scatter, sems, thru = handle

    def body(*refs):
        src_refs, land_refs = refs[:n], refs[n:2 * n]
        send_sems, recv_sems, local_sems = refs[2 * n:2 * n + 3]
        sends, arrivals, locals_ = _exchange_copies(scatter, src_refs, land_refs, send_sems, recv_sems, local_sems)
        for cp in arrivals:
            cp.wait_recv()
        for cp in sends:
            cp.wait_send()
        for cp in locals_:
            cp.wait()

    res = pl.pallas_call(
        body, name=name, out_shape=[pltpu.HBM(a.shape, a.dtype) for a in thru],
        in_specs=[HBM] * (2 * n) + [SEM] * 3 + [ANY], out_specs=[HBM] * (2 * n),
        input_output_aliases={i: i for i in range(2 * n)},
        compiler_params=pltpu.CompilerParams(has_side_effects=EFFECT),
    )(*thru, *sems, after)
    return res[n:]


def _behind(tokens, a):
    return lax.optimization_barrier((*tokens, a))[-1]


def _perm_heads(a, perm, axis):
    idx = [slice(None)] * a.ndim
    parts = []
    for h in perm:
        idx[axis] = slice(64 * h, 64 * (h + 1))
        parts.append(a[tuple(idx)])
    idx[axis] = slice(512, None)
    if a.shape[axis] > 512:
        parts.append(a[tuple(idx)])
    return jnp.concatenate(parts, axis=axis)


Q_INV = tuple(int(i) for i in np.argsort(Q_PERM))


def _in0_to_kernel(a, axis):
    a = _perm_heads(a, Q_PERM, axis)
    idx = [slice(None)] * a.ndim

    def cut(lo, hi):
        idx[axis] = slice(lo, hi)
        return a[tuple(idx)]

    return jnp.concatenate([cut(0, 512), cut(768, 1792), cut(512, 768)], axis=axis)


def _in0_from_kernel(a, axis):
    idx = [slice(None)] * a.ndim

    def cut(lo, hi):
        idx[axis] = slice(lo, hi)
        return a[tuple(idx)]

    a = jnp.concatenate([cut(0, 512), cut(1536, 1792), cut(512, 1536)], axis=axis)
    return _perm_heads(a, Q_INV, axis)


def _f32_as_u16_rows(v, rows):
    bits = lax.bitcast_convert_type(v, jnp.uint16).reshape(-1)
    return jnp.pad(bits, (0, rows * D - bits.shape[0])).reshape(rows, D)


def _pad_rows(v, rows):
    v = v.reshape(-1)
    return jnp.pad(v, (0, rows * D - v.shape[0])).reshape(rows, D)


def kernel(x, mix_norm, a_w_in, a_b_in, a_sinks, a_conv_w, a_conv_b, a_cln_g, a_cln_b, a_w_out, c_w_in, c_w_pool, c_pool_scale, c_sln_g, c_sln_b, c_w_s, c_b_s, c_w_out, ffn_norm, ffn_w_gate, ffn_w_up, ffn_w_down, final_norm, loss_target, m_mix_norm, m_a_w_in, m_a_b_in, m_a_sinks, m_a_conv_w, m_a_conv_b, m_a_cln_g, m_a_cln_b, m_a_w_out, m_c_w_in, m_c_w_pool, m_c_pool_scale, m_c_sln_g, m_c_sln_b, m_c_w_s, m_c_b_s, m_c_w_out, m_ffn_norm, m_ffn_w_gate, m_ffn_w_up, m_ffn_w_down, m_final_norm, v_mix_norm, v_a_w_in, v_a_b_in, v_a_sinks, v_a_conv_w, v_a_conv_b, v_a_cln_g, v_a_cln_b, v_a_w_out, v_c_w_in, v_c_w_pool, v_c_pool_scale, v_c_sln_g, v_c_sln_b, v_c_w_s, v_c_b_s, v_c_w_out, v_ffn_norm, v_ffn_w_gate, v_ffn_w_up, v_ffn_w_down, v_final_norm):
    bsz, seq, _ = x.shape
    t = bsz * seq
    w_in = dict(mix_norm=mix_norm, a_w_in=a_w_in, a_b_in=a_b_in, a_sinks=a_sinks, a_conv_w=a_conv_w, a_conv_b=a_conv_b,
                a_cln_g=a_cln_g, a_cln_b=a_cln_b, a_w_out=a_w_out, c_w_in=c_w_in, c_w_pool=c_w_pool,
                c_pool_scale=c_pool_scale, c_sln_g=c_sln_g, c_sln_b=c_sln_b, c_w_s=c_w_s, c_b_s=c_b_s, c_w_out=c_w_out,
                ffn_norm=ffn_norm, ffn_w_gate=ffn_w_gate, ffn_w_up=ffn_w_up, ffn_w_down=ffn_w_down, final_norm=final_norm)
    m_in = dict(mix_norm=m_mix_norm, a_w_in=m_a_w_in, a_b_in=m_a_b_in, a_sinks=m_a_sinks, a_conv_w=m_a_conv_w,
                a_conv_b=m_a_conv_b, a_cln_g=m_a_cln_g, a_cln_b=m_a_cln_b, a_w_out=m_a_w_out, c_w_in=m_c_w_in,
                c_w_pool=m_c_w_pool, c_pool_scale=m_c_pool_scale, c_sln_g=m_c_sln_g, c_sln_b=m_c_sln_b, c_w_s=m_c_w_s,
                c_b_s=m_c_b_s, c_w_out=m_c_w_out, ffn_norm=m_ffn_norm, ffn_w_gate=m_ffn_w_gate, ffn_w_up=m_ffn_w_up,
                ffn_w_down=m_ffn_w_down, final_norm=m_final_norm)
    v_in = dict(mix_norm=v_mix_norm, a_w_in=v_a_w_in, a_b_in=v_a_b_in, a_sinks=v_a_sinks, a_conv_w=v_a_conv_w,
                a_conv_b=v_a_conv_b, a_cln_g=v_a_cln_g, a_cln_b=v_a_cln_b, a_w_out=v_a_w_out, c_w_in=v_c_w_in,
                c_w_pool=v_c_w_pool, c_pool_scale=v_c_pool_scale, c_sln_g=v_c_sln_g, c_sln_b=v_c_sln_b, c_w_s=v_c_w_s,
                c_b_s=v_c_b_s, c_w_out=v_c_w_out, ffn_norm=v_ffn_norm, ffn_w_gate=v_ffn_w_gate, ffn_w_up=v_ffn_w_up,
                ffn_w_down=v_ffn_w_down, final_norm=v_final_norm)

    small = jnp.concatenate([a_conv_w[0].reshape(-1), c_pool_scale[0], c_sln_g[0], c_sln_b[0]])
    first_bits = lax.bitcast_convert_type(jnp.concatenate([a_w_in[0].T, a_w_out[0]], axis=0).astype(BF16), jnp.uint16)
    gathered = _all_gather(jnp.concatenate([first_bits, _f32_as_u16_rows(small, W_MISC_ROWS)], axis=0), "gather_mixer0")

    def ffn_shards(l):
        return [ffn_w_gate[l].T.astype(BF16), ffn_w_up[l].T.astype(BF16), ffn_w_down[l].astype(BF16)]

    ffn0_h, tok = _exchange_start(_behind([gathered], ffn_shards(0)), False, "gather_ffn0_start")
    mix1_h, tok = _exchange_start(_behind([tok], [c_w_in[0].T.astype(BF16), c_w_out[0].astype(BF16)]), False,
                                  "gather_mixer1_start")
    ffn1_h, tok = _exchange_start(_behind([tok], ffn_shards(1)), False, "gather_ffn1_start")

    a_in_full = lax.bitcast_convert_type(gathered[:, :224].reshape(IN0, D), BF16)
    a_out_full = lax.bitcast_convert_type(gathered[:, 224:352].reshape(D, D), BF16)
    small_all = lax.bitcast_convert_type(
        gathered[:, 352:].reshape(N_DEV, -1)[:, :2 * SMALL_SHARD].reshape(N_DEV, SMALL_SHARD, 2), F32)
    conv_w = small_all[:, :31 * 64].reshape(N_DEV, 31, 64).transpose(1, 0, 2).reshape(31, 512)
    conv_w = jnp.pad(conv_w, ((0, HALO - CONV_K), (0, 0)))
    pool_scale = small_all[:, 31 * 64:31 * 64 + 64].reshape(1, 512)
    sln_g = small_all[:, 31 * 64 + 64:31 * 64 + 128].reshape(1, 512)
    sln_b = small_all[:, 31 * 64 + 128:].reshape(1, 512)

    wt_in0 = _in0_to_kernel(a_in_full, 0)
    b_in0 = _in0_to_kernel(a_b_in, 1)
    w_out0 = _perm_heads(a_out_full, Q_PERM, 0)
    b_rows = jnp.broadcast_to(c_b_s[0][:, :, None], (4, 128, 128))
    conv_b, cln_g, cln_b = a_conv_b, a_cln_g, a_cln_b

    h0 = x.reshape(t, D)
    target = loss_target.reshape(t, D)
    z0, hn0 = _norm_proj(h0, _behind([tok], mix_norm[0:1]), wt_in0, b_in0, "in_proj0")
    attn = _attn_fwd(z0, a_sinks, bsz, "attn_fwd")
    conv = _conv_fwd(z0, conv_w, conv_b, cln_g, cln_b, bsz, "conv_fwd")
    h1 = _out_proj(attn, conv, w_out0, h0, "out_proj0")
    wtg0, wtu0, wd0 = (w.reshape(D_FF, D) for w in _exchange_wait(ffn0_h, h1, "gather_ffn0_wait"))
    h2, hnf0, gate0, up0 = _ffn_fwd(h1, ffn_norm[0:1], wtg0, wtu0, wd0, "ffn_fwd0")
    wt_in1, w_out1 = (w.reshape(-1, D) for w in _exchange_wait(mix1_h, h2, "gather_mixer1_wait"))
    z1, hn1 = _norm_proj(h2, mix_norm[1:2], wt_in1, None, "in_proj1")
    pool = _pool_fwd(z1, c_w_pool[0], pool_scale, bsz, "pool_fwd")
    sgu = _sgu_fwd(z1, sln_g, sln_b, c_w_s[0], b_rows, "sgu_fwd")
    h3 = _out_proj(pool, sgu, w_out1, h2, "out_proj1")
    wtg1, wtu1, wd1 = (w.reshape(D_FF, D) for w in _exchange_wait(ffn1_h, h3, "gather_ffn1_wait"))
    h4, hnf1, gate1, up1 = _ffn_fwd(h3, ffn_norm[1:2], wtg1, wtu1, wd1, "ffn_fwd1")

    def blocks(g):
        return g.reshape(N_DEV, g.shape[0] // N_DEV, D)

    dh4, d_final_norm, loss_part = _loss_head(h4, final_norm.reshape(1, D), target, "loss_head")
    dh3, dgate1, dup1, act1, d_fn1 = _ffn_bwd(dh4, h3, ffn_norm[1:2], gate1, up1, wtg1, wtu1, wd1, "ffn_bwd1")
    gw_ffn1 = [_mm_tn(dgate1, hnf1, "dw_gate1"), _mm_tn(dup1, hnf1, "dw_up1"), _mm_tn(act1, dh4, "dw_down1")]
    ffn1_g, tok = _exchange_start([blocks(g) for g in gw_ffn1], True, "scatter_ffn1_start")
    dmix1 = _dmix(dh3, _behind([tok], w_out1), "dmix1")
    gw_c_out = jnp.concatenate([_mm_tn(pool, dh3, "dw_out1_pool"), _mm_tn(sgu, dh3, "dw_out1_sgu")], axis=0)
    dzp, d_w_pool, d_pool_scale = _pool_bwd(z1, dmix1, c_w_pool[0], pool_scale, bsz, "pool_bwd")
    dzu, dzv, d_w_s, d_b_s, d_sln_g, d_sln_b = _sgu_bwd(z1, dmix1, sln_g, sln_b, c_w_s[0], b_rows, "sgu_bwd")
    dh2, d_mn1 = _proj_bwd_norm([(dzp, 0), (dzu, 512), (dzv, 1024)], wt_in1, h2, dh3, mix_norm[1:2], "in_proj1_bwd")
    gw_c_in = jnp.concatenate([_mm_tn(dzp, hn1, "dw_in1_pool"), _mm_tn(dzu, hn1, "dw_in1_u"),
                               _mm_tn(dzv, hn1, "dw_in1_v")], axis=0)
    mix1_g, tok = _exchange_start([blocks(gw_c_in), blocks(gw_c_out)], True, "scatter_mixer1_start")
    dh1, dgate0, dup0, act0, d_fn0 = _ffn_bwd(dh2, h1, _behind([tok], ffn_norm[0:1]), gate0, up0, wtg0, wtu0, wd0, "ffn_bwd0")
    gw_ffn0 = [_mm_tn(dgate0, hnf0, "dw_gate0"), _mm_tn(dup0, hnf0, "dw_up0"), _mm_tn(act0, dh2, "dw_down0")]
    ffn0_g, tok = _exchange_start([blocks(g) for g in gw_ffn0], True, "scatter_ffn0_start")
    dmix0 = _dmix(dh1, _behind([tok], w_out0), "dmix0")
    gw_a_out = _perm_heads(jnp.concatenate([_mm_tn(attn, dh1, "dw_out0_attn"), _mm_tn(conv, dh1, "dw_out0_conv")],
                                           axis=0), Q_INV, 0)
    dq, dkv, d_sink_row, d_bq, d_bkv = _attn_bwd(z0, dmix0, a_sinks, bsz, "attn_bwd")
    dca, dcg, d_conv_w, d_conv_b, d_cln_g, d_cln_b, d_ba, d_bg = _conv_bwd(z0, dmix0, conv_w, conv_b, cln_g, cln_b, bsz, "conv_bwd")
    dx, d_mn0 = _proj_bwd_norm([(dq, 0), (dca, 512), (dcg, 1024), (dkv, 1536)], wt_in0, h0, dh1, mix_norm[0:1], "in_proj0_bwd")
    gw_a_in = _in0_from_kernel(jnp.concatenate(
        [_mm_tn(dq, hn0, "dw_in0_q"), _mm_tn(dca, hn0, "dw_in0_a"), _mm_tn(dcg, hn0, "dw_in0_g"),
         _mm_tn(dkv, hn0, "dw_in0_kv")], axis=0), 0)
    d_b_in = _in0_from_kernel(jnp.concatenate([d_bq, d_ba, d_bg, d_bkv], axis=1), 1)

    rep = dict(mix_norm=jnp.concatenate([d_mn0, d_mn1], axis=0), a_b_in=d_b_in, a_sinks=d_sink_row[:, :8],
               a_conv_b=d_conv_b, a_cln_g=d_cln_g, a_cln_b=d_cln_b, c_w_pool=d_w_pool[None], c_w_s=d_w_s[None],
               c_b_s=d_b_s[None], ffn_norm=jnp.concatenate([d_fn0, d_fn1], axis=0), final_norm=d_final_norm.reshape(D))
    rep_flat = jnp.concatenate([rep[nm].reshape(-1) for nm in REP_NAMES])
    rep_flat = jnp.pad(rep_flat, (0, N_DEV * REP_ROWS * D - rep_flat.shape[0])).reshape(N_DEV, REP_ROWS, D)
    small_g = jnp.concatenate([
        d_conv_w[:CONV_K].reshape(31, N_DEV, 64).transpose(1, 0, 2).reshape(N_DEV, 31 * 64),
        d_pool_scale.reshape(N_DEV, 64), d_sln_g.reshape(N_DEV, 64), d_sln_b.reshape(N_DEV, 64)], axis=1)
    small_g = jnp.pad(small_g, ((0, 0), (0, G_SMALL_ROWS * D - SMALL_SHARD))).reshape(N_DEV, G_SMALL_ROWS, D)
    g_misc = jnp.concatenate([small_g, rep_flat], axis=1)
    mix0_g, tok = _exchange_start([blocks(gw_a_in), blocks(gw_a_out), g_misc], True, "scatter_mixer0_start")

    names = list(w_in)
    g_out, delta, new_m, new_v = {}, {}, {}, {}

    def adamw(nm):
        shp = w_in[nm].shape
        two_d = (shp[0] * shp[1], shp[2])
        res = _adamw(w_in[nm].reshape(two_d), g_out[nm].reshape(two_d), m_in[nm].reshape(two_d), v_in[nm].reshape(two_d),
                     "adamw_" + nm)
        delta[nm], new_m[nm], new_v[nm] = (r.reshape(shp) for r in res)

    g1, u1, d1 = (_sum_slabs(a, "sum_ffn1_" + s) for a, s in zip(_exchange_wait(ffn1_g, tok, "scatter_ffn1_wait"), "gud"))
    c_in_g, c_out_g = (_sum_slabs(a, "sum_mixer1_" + s) for a, s in
                       zip(_exchange_wait(mix1_g, g1, "scatter_mixer1_wait"), ("in", "out")))
    g0, u0, d0 = (_sum_slabs(a, "sum_ffn0_" + s) for a, s in zip(_exchange_wait(ffn0_g, c_in_g, "scatter_ffn0_wait"), "gud"))
    g_out.update(c_w_in=c_in_g.T[None], c_w_out=c_out_g[None], ffn_w_gate=jnp.stack([g0.T, g1.T]),
                 ffn_w_up=jnp.stack([u0.T, u1.T]), ffn_w_down=jnp.stack([d0, d1]))
    for nm in ("c_w_in", "c_w_out", "ffn_w_gate", "ffn_w_up", "ffn_w_down"):
        adamw(nm)
    a_in_g, a_out_g, g_tail = (_sum_slabs(a, "sum_mixer0_" + s) for a, s in
                               zip(_exchange_wait(mix0_g, delta["ffn_w_down"], "scatter_mixer0_wait"), ("in", "out", "tail")))
    rep_all = _all_gather(g_tail[G_SMALL_ROWS:], "gather_replicated_grads").reshape(-1)
    small_r = g_tail[:G_SMALL_ROWS].reshape(-1)[:SMALL_SHARD]
    g_out.update(
        a_w_in=a_in_g.T[None], a_w_out=a_out_g[None], a_conv_w=small_r[:31 * 64].reshape(1, 31, 64),
        c_pool_scale=small_r[31 * 64:31 * 64 + 64].reshape(1, 64), c_sln_g=small_r[31 * 64 + 64:31 * 64 + 128].reshape(1, 64),
        c_sln_b=small_r[31 * 64 + 128:].reshape(1, 64))
    off = 0
    for nm in REP_NAMES:
        n = int(np.prod(w_in[nm].shape))
        g_out[nm] = rep_all[off:off + n].reshape(w_in[nm].shape)
        off += n
    adamw("a_w_in")
    adamw("a_w_out")
    for group, rows, label in ((("a_conv_w", "c_pool_scale", "c_sln_g", "c_sln_b"), G_SMALL_ROWS, "adamw_small_sharded"),
                               (REP_NAMES, N_DEV * REP_ROWS, "adamw_replicated")):
        flat = [_pad_rows(jnp.concatenate([d[nm].reshape(-1) for nm in group]), rows) for d in (w_in, g_out, m_in, v_in)]
        res = [r.reshape(-1) for r in _adamw(*flat, label)]
        off = 0
        for nm in group:
            n = int(np.prod(w_in[nm].shape))
            delta[nm], new_m[nm], new_v[nm] = (r[off:off + n].reshape(w_in[nm].shape) for r in res)
            off += n

    loss = lax.psum(loss_part[0, 0], ("x", "y", "c"))
    grad_x = dx.reshape(bsz, seq, D)
    return (loss, grad_x, *[g_out[nm] for nm in names], *[delta[nm] for nm in names],
            *[new_m[nm] for nm in names], *[new_v[nm] for nm in names])
```

```python
import functools

import jax
import jax.numpy as jnp
import numpy as np
from jax import lax
from jax.experimental import pallas as pl
from jax.experimental.pallas import tpu as pltpu

F32 = jnp.float32
BF16 = jnp.bfloat16
MESH = pl.DeviceIdType.MESH

D = 1024
N_DEV = 8
EPS = 1e-5
HEAD_PAIRS = 4
ATT_BLK = 128
CONV_K = 31
HALO = 32
D_FF = 2816
FF_TILE_FWD = D_FF // 2
FF_TILE_BWD = 256
IN0 = 1792
IN1 = 1536
POOL_WINDOWS = (2, 4, 8, 16)
SGU_CHUNK = 128
GELU_C = 0.7978845608028654
GELU_A = 0.044715
ADAM_LR, ADAM_B1, ADAM_B2, ADAM_EPS, ADAM_WD, ADAM_STEP = 0.001, 0.9, 0.999, 1e-08, 0.01, 10
VMEM_LIMIT = 56 << 20

SMALL_SHARD = 31 * 64 + 3 * 64
W_MISC_ROWS = 16
G_MISC_ROWS = 32
G_SMALL_ROWS = 8
REP_ROWS = G_MISC_ROWS - G_SMALL_ROWS
REP_NAMES = ("mix_norm", "a_b_in", "a_sinks", "a_conv_b", "a_cln_g", "a_cln_b", "c_w_pool", "c_w_s", "c_b_s",
             "ffn_norm", "final_norm")
Q_PERM = (0, 4, 1, 5, 2, 6, 3, 7)


def _params(*sem):
    return pltpu.CompilerParams(dimension_semantics=sem, vmem_limit_bytes=VMEM_LIMIT)


def _nn(a, b):
    return jnp.dot(a, b, preferred_element_type=F32)


def _nt(a, b):
    return lax.dot_general(a, b, (((1,), (1,)), ((), ())), preferred_element_type=F32)


def _tn(a, b):
    return lax.dot_general(a, b, (((0,), (0,)), ((), ())), preferred_element_type=F32)


def _tile(n, want=512):
    t = min(want, n)
    assert n % t == 0, (n, t)
    return t


def _seq_tile(s):
    return 512 if s >= 1024 else s // 2


def _rms(x, g):
    r = lax.rsqrt(jnp.mean(x * x, axis=-1, keepdims=True) + EPS)
    return x * r * g, r


def _rms_bwd(x, g, d_y):
    r = lax.rsqrt(jnp.mean(x * x, axis=-1, keepdims=True) + EPS)
    xr = x * r
    u = d_y * g
    d_x = r * (u - xr * jnp.mean(u * xr, axis=-1, keepdims=True))
    return d_x, jnp.sum(d_y * xr, axis=0, keepdims=True)


def _ln(y, g, b):
    mu = jnp.mean(y, axis=-1, keepdims=True)
    yc = y - mu
    rstd = lax.rsqrt(jnp.mean(yc * yc, axis=-1, keepdims=True) + EPS)
    xhat = yc * rstd
    return xhat * g + b, xhat, rstd


def _ln_bwd(d_o, xhat, rstd, g):
    dxh = d_o * g
    return rstd * (dxh - jnp.mean(dxh, axis=-1, keepdims=True) - xhat * jnp.mean(dxh * xhat, axis=-1, keepdims=True))


def _gelu(x):
    th = jnp.tanh(GELU_C * (x + GELU_A * x * x * x))
    return 0.5 * x * (1.0 + th), th


def _gelu_grad(x, th):
    return 0.5 * (1.0 + th) + 0.5 * x * (1.0 - th * th) * GELU_C * (1.0 + 3.0 * GELU_A * x * x)


def _row(c):
    return pl.BlockSpec((1, c), lambda *_: (0, 0))


def _full(shape):
    return pl.BlockSpec(shape, lambda *_: (0,) * len(shape))


def _norm_proj(h, g, wt, bias, name):
    t, n = h.shape[0], wt.shape[0]
    tm = _tile(t)
    has_bias = bias is not None

    def body(*refs):
        h_ref, g_ref, wt_ref = refs[:3]
        z_ref, hn_ref = refs[-2:]
        hn = _rms(h_ref[...], g_ref[...])[0].astype(BF16)
        hn_ref[...] = hn
        z = _nt(hn, wt_ref[...])
        if has_bias:
            z = z + refs[3][...]
        z_ref[...] = z.astype(BF16)

    in_specs = [pl.BlockSpec((tm, D), lambda i: (i, 0)), _row(D), _full((n, D))]
    args = [h, g, wt]
    if has_bias:
        in_specs.append(_row(n))
        args.append(bias)
    return pl.pallas_call(
        body, name=name, grid=(t // tm,), in_specs=in_specs,
        out_specs=[pl.BlockSpec((tm, n), lambda i: (i, 0)), pl.BlockSpec((tm, D), lambda i: (i, 0))],
        out_shape=[jax.ShapeDtypeStruct((t, n), BF16), jax.ShapeDtypeStruct((t, D), BF16)],
        compiler_params=_params("parallel"))(*args)


def _out_proj(a, b, w, h, name):
    t = h.shape[0]
    tm = _tile(t)

    def body(a_ref, b_ref, wa_ref, wb_ref, h_ref, o_ref):
        o_ref[...] = h_ref[...] + _nn(a_ref[...], wa_ref[...]) + _nn(b_ref[...], wb_ref[...])

    half = pl.BlockSpec((tm, 512), lambda i: (i, 0))
    return pl.pallas_call(
        body, name=name, grid=(t // tm,),
        in_specs=[half, half, pl.BlockSpec((512, D), lambda i: (0, 0)), pl.BlockSpec((512, D), lambda i: (1, 0)),
                  pl.BlockSpec((tm, D), lambda i: (i, 0))],
        out_specs=pl.BlockSpec((tm, D), lambda i: (i, 0)), out_shape=jax.ShapeDtypeStruct((t, D), F32),
        compiler_params=_params("parallel"))(a, b, w, w, h)


def _dmix(dh, w, name):
    t = dh.shape[0]
    tm = _tile(t)

    def body(dh_ref, w_ref, o_ref):
        o_ref[...] = _nt(dh_ref[...].astype(BF16), w_ref[...]).astype(BF16)

    return pl.pallas_call(
        body, name=name, grid=(t // tm,), in_specs=[pl.BlockSpec((tm, D), lambda i: (i, 0)), _full((D, D))],
        out_specs=pl.BlockSpec((tm, D), lambda i: (i, 0)), out_shape=jax.ShapeDtypeStruct((t, D), BF16),
        compiler_params=_params("parallel"))(dh, w)


def _ffn_fwd(h, g, wtg, wtu, wd, name):
    t = h.shape[0]
    tm, tf = _tile(t), FF_TILE_FWD
    nf = D_FF // tf

    def body(h_ref, g_ref, wtg_ref, wtu_ref, wd_ref, o_ref, hn_ref, gate_ref, up_ref):
        @pl.when(pl.program_id(1) == 0)
        def _():
            x = h_ref[...]
            hn_ref[...] = _rms(x, g_ref[...])[0].astype(BF16)
            o_ref[...] = x

        hn = hn_ref[...]
        gate = _nt(hn, wtg_ref[...])
        up = _nt(hn, wtu_ref[...])
        gate_ref[...] = gate.astype(BF16)
        up_ref[...] = up.astype(BF16)
        act = (gate * jax.nn.sigmoid(gate) * up).astype(BF16)
        o_ref[...] += _nn(act, wd_ref[...])

    tok = pl.BlockSpec((tm, D), lambda i, f: (i, 0))
    wsp = pl.BlockSpec((tf, D), lambda i, f: (f, 0))
    mid = pl.BlockSpec((tm, tf), lambda i, f: (i, f))
    return pl.pallas_call(
        body, name=name, grid=(t // tm, nf), in_specs=[tok, _row(D), wsp, wsp, wsp],
        out_specs=[tok, tok, mid, mid],
        out_shape=[jax.ShapeDtypeStruct((t, D), F32), jax.ShapeDtypeStruct((t, D), BF16),
                   jax.ShapeDtypeStruct((t, D_FF), BF16), jax.ShapeDtypeStruct((t, D_FF), BF16)],
        compiler_params=_params("parallel", "arbitrary"))(h, g, wtg, wtu, wd)


def _ffn_bwd(dh, h, g, gate, up, wtg, wtu, wd, name):
    t = h.shape[0]
    tm, tf = _tile(t, 1024), FF_TILE_BWD
    nf = D_FF // tf

    def body(dh_ref, h_ref, g_ref, gate_ref, up_ref, wtg_ref, wtu_ref, wd_ref,
             dhin_ref, dgate_ref, dup_ref, act_ref, dg_ref, dhb):
        i, f = pl.program_id(0), pl.program_id(1)

        @pl.when(f == 0)
        def _():
            dhb[...] = dh_ref[...].astype(BF16)
            dhin_ref[...] = jnp.zeros_like(dhin_ref)

        @pl.when((i == 0) & (f == 0))
        def _():
            dg_ref[...] = jnp.zeros_like(dg_ref)

        dact = _nt(dhb[...], wd_ref[...])
        gt = gate_ref[...].astype(F32)
        u = up_ref[...].astype(F32)
        sg = jax.nn.sigmoid(gt)
        sil = gt * sg
        act_ref[...] = (sil * u).astype(BF16)
        dup = (dact * sil).astype(BF16)
        dgate = (dact * u * sg * (1.0 + gt * (1.0 - sg))).astype(BF16)
        dup_ref[...] = dup
        dgate_ref[...] = dgate
        dhin_ref[...] += _nn(dgate, wtg_ref[...]) + _nn(dup, wtu_ref[...])

        @pl.when(f == nf - 1)
        def _():
            d_x, d_g = _rms_bwd(h_ref[...], g_ref[...], dhin_ref[...])
            dhin_ref[...] = dh_ref[...] + d_x
            dg_ref[...] += d_g

    tok = pl.BlockSpec((tm, D), lambda i, f: (i, 0))
    wsp = pl.BlockSpec((tf, D), lambda i, f: (f, 0))
    mid = pl.BlockSpec((tm, tf), lambda i, f: (i, f))
    mid_shape = jax.ShapeDtypeStruct((t, D_FF), BF16)
    return pl.pallas_call(
        body, name=name, grid=(t // tm, nf), in_specs=[tok, tok, _row(D), mid, mid, wsp, wsp, wsp],
        out_specs=[tok, mid, mid, mid, _row(D)],
        out_shape=[jax.ShapeDtypeStruct((t, D), F32), mid_shape, mid_shape, mid_shape, jax.ShapeDtypeStruct((1, D), F32)],
        scratch_shapes=[pltpu.VMEM((tm, D), BF16)],
        compiler_params=_params("arbitrary", "arbitrary"))(dh, h, g, gate, up, wtg, wtu, wd)


def _proj_bwd_norm(pieces, wt, h, dh, g, name):
    t = h.shape[0]
    tm = _tile(t)
    n_p = len(pieces)

    def body(*refs):
        p_refs, w_refs = refs[:n_p], refs[n_p:2 * n_p]
        h_ref, dh_ref, g_ref, o_ref, dg_ref = refs[2 * n_p:]

        @pl.when(pl.program_id(0) == 0)
        def _():
            dg_ref[...] = jnp.zeros_like(dg_ref)

        d_hn = _nn(p_refs[0][...], w_refs[0][...])
        for p_ref, w_ref in zip(p_refs[1:], w_refs[1:]):
            d_hn = d_hn + _nn(p_ref[...], w_ref[...])
        d_x, d_g = _rms_bwd(h_ref[...], g_ref[...], d_hn)
        o_ref[...] = dh_ref[...] + d_x
        dg_ref[...] += d_g

    tok = pl.BlockSpec((tm, D), lambda i: (i, 0))
    in_specs = [pl.BlockSpec((tm, a.shape[1]), lambda i: (i, 0)) for a, _ in pieces]
    for a, off in pieces:
        w = a.shape[1]
        assert off % w == 0
        in_specs.append(pl.BlockSpec((w, D), functools.partial(lambda i, blk: (blk, 0), blk=off // w)))
    in_specs += [tok, tok, _row(D)]
    return pl.pallas_call(
        body, name=name, grid=(t // tm,), in_specs=in_specs, out_specs=[tok, _row(D)],
        out_shape=[jax.ShapeDtypeStruct((t, D), F32), jax.ShapeDtypeStruct((1, D), F32)],
        compiler_params=_params("arbitrary"))(*[a for a, _ in pieces], *([wt] * n_p), h, dh, g)


def _mm_tn(a, b, name):
    t, n = a.shape
    k = b.shape[1]
    tn = n if n <= 1024 else n // 2
    tt = _tile(t)
    nt = t // tt

    def body(a_ref, b_ref, o_ref, acc):
        s = pl.program_id(1)

        @pl.when(s == 0)
        def _():
            acc[...] = jnp.zeros_like(acc)

        acc[...] += _tn(a_ref[...], b_ref[...].astype(BF16))

        @pl.when(s == nt - 1)
        def _():
            o_ref[...] = acc[...].astype(BF16)

    return pl.pallas_call(
        body, name=name, grid=(n // tn, nt),
        in_specs=[pl.BlockSpec((tt, tn), lambda j, s: (s, j)), pl.BlockSpec((tt, k), lambda j, s: (s, 0))],
        out_specs=pl.BlockSpec((tn, k), lambda j, s: (j, 0)), out_shape=jax.ShapeDtypeStruct((n, k), BF16),
        scratch_shapes=[pltpu.VMEM((tn, k), F32)],
        compiler_params=_params("parallel", "arbitrary"))(a, b)


def _loss_head(h, g, target, name):
    t = h.shape[0]
    tm = _tile(t)

    def body(h_ref, g_ref, t_ref, dh_ref, dg_ref, loss_ref):
        @pl.when(pl.program_id(0) == 0)
        def _():
            dg_ref[...] = jnp.zeros_like(dg_ref)
            loss_ref[...] = jnp.zeros_like(loss_ref)

        x = h_ref[...]
        gv = g_ref[...]
        err = _rms(x, gv)[0] - t_ref[...]
        per_tok = jnp.mean(err * err, axis=-1, keepdims=True)
        loss_ref[...] += 0.5 * jnp.sum(per_tok, axis=0, keepdims=True)
        d_x, d_g = _rms_bwd(x, gv, err * (1.0 / D))
        dh_ref[...] = d_x
        dg_ref[...] += d_g

    tok = pl.BlockSpec((tm, D), lambda i: (i, 0))
    return pl.pallas_call(
        body, name=name, grid=(t // tm,), in_specs=[tok, _row(D), tok],
        out_specs=[tok, _row(D), _row(1)],
        out_shape=[jax.ShapeDtypeStruct((t, D), F32), jax.ShapeDtypeStruct((1, D), F32), jax.ShapeDtypeStruct((1, 1), F32)],
        compiler_params=_params("arbitrary"))(h, g, target)


STACK = HEAD_PAIRS * ATT_BLK


def _attn_valid(first):
    qi = lax.broadcasted_iota(jnp.int32, (STACK, 2 * ATT_BLK), 0) % ATT_BLK
    r = lax.broadcasted_iota(jnp.int32, (STACK, 2 * ATT_BLK), 1)
    dist = qi + ATT_BLK - r
    return (dist >= 0) & (dist < ATT_BLK) & ((r >= ATT_BLK) | jnp.logical_not(first))


def _stacked(ref, kh, scale):
    lo = lax.broadcasted_iota(jnp.int32, (ATT_BLK, 128), 1) < 64
    keep = lo if kh == 0 else ~lo
    parts = [jnp.where(keep, ref[:, g * 128:(g + 1) * 128] * scale, 0.0).astype(BF16) for g in range(HEAD_PAIRS)]
    return jnp.concatenate(parts, axis=0)


def _unstacked(a0, a1, g):
    lo = lax.broadcasted_iota(jnp.int32, (ATT_BLK, 128), 1) < 64
    rows = slice(g * ATT_BLK, (g + 1) * ATT_BLK)
    return jnp.where(lo, a0[rows], a1[rows])


def _sink_column(s_ref, kh):
    return jnp.concatenate([jnp.full((ATT_BLK, 1), s_ref[0, kh * 4 + g], F32) for g in range(HEAD_PAIRS)], axis=0)


def _attn_probs(qs, kpair, sink, valid):
    s = jnp.where(valid, _nt(qs, kpair), -1e30)
    m = jnp.maximum(jnp.max(s, axis=-1, keepdims=True), sink)
    p = jnp.exp(s - m)
    es = jnp.exp(sink - m)
    inv = 1.0 / (jnp.sum(p, axis=-1, keepdims=True) + es)
    return p * inv, es * inv


def _attn_specs(nb, order):
    q = pl.BlockSpec((ATT_BLK, 512), lambda b, j: (b * nb + order(j), 0))
    kvc = pl.BlockSpec((ATT_BLK, 256), lambda b, j: (b * nb + order(j), 6))
    kvp = pl.BlockSpec((ATT_BLK, 256), lambda b, j: (jnp.maximum(b * nb + order(j) - 1, 0), 6))
    return q, kvc, kvp


def _window_kv(kvc_ref, kvp_ref):
    kvc, kvp = kvc_ref[...], kvp_ref[...]
    kpair = jnp.concatenate([kvp[:, :128], kvc[:, :128]], axis=0)
    vpair = jnp.concatenate([kvp[:, 128:], kvc[:, 128:]], axis=0)
    return kpair, vpair


def _attn_fwd(z0, sinks, bsz, name):
    t = z0.shape[0]
    nb = t // bsz // ATT_BLK

    def body(s_ref, q_ref, kvc_ref, kvp_ref, o_ref):
        valid = _attn_valid(pl.program_id(1) == 0)
        kpair, vpair = _window_kv(kvc_ref, kvp_ref)
        outs = []
        for kh in range(2):
            p, _ = _attn_probs(_stacked(q_ref, kh, 0.125), kpair, _sink_column(s_ref, kh), valid)
            outs.append(_nn(p.astype(BF16), vpair))
        for g in range(HEAD_PAIRS):
            o_ref[:, g * 128:(g + 1) * 128] = _unstacked(outs[0], outs[1], g).astype(BF16)

    q, kvc, kvp = _attn_specs(nb, lambda j: j)
    return pl.pallas_call(
        body, name=name, grid=(bsz, nb),
        in_specs=[pl.BlockSpec(memory_space=pltpu.SMEM), q, kvc, kvp],
        out_specs=pl.BlockSpec((ATT_BLK, 512), lambda b, j: (b * nb + j, 0)),
        out_shape=jax.ShapeDtypeStruct((t, 512), BF16),
        compiler_params=_params("parallel", "parallel"))(sinks, z0, z0, z0)


def _attn_bwd(z0, dmix, sinks, bsz, name):
    t = z0.shape[0]
    nb = t // bsz // ATT_BLK

    def body(s_ref, q_ref, kvc_ref, kvp_ref, do_ref, dq_ref, dkv_ref, dsink_ref, dbq_ref, dbkv_ref, carry):
        b, j = pl.program_id(0), pl.program_id(1)

        @pl.when(j == 0)
        def _():
            carry[...] = jnp.zeros_like(carry)

        @pl.when((b == 0) & (j == 0))
        def _():
            dsink_ref[...] = jnp.zeros_like(dsink_ref)
            dbq_ref[...] = jnp.zeros_like(dbq_ref)
            dbkv_ref[...] = jnp.zeros_like(dbkv_ref)

        valid = _attn_valid(j == nb - 1)
        kpair, vpair = _window_kv(kvc_ref, kvp_ref)
        lane = lax.broadcasted_iota(jnp.int32, (1, 128), 1)
        dk = jnp.zeros((2 * ATT_BLK, 128), F32)
        dv = jnp.zeros((2 * ATT_BLK, 128), F32)
        dsink = jnp.zeros((1, 128), F32)
        dqs = []
        for kh in range(2):
            qs = _stacked(q_ref, kh, 0.125)
            dos = _stacked(do_ref, kh, 1.0)
            p, ps = _attn_probs(qs, kpair, _sink_column(s_ref, kh), valid)
            dp = _nt(dos, vpair)
            delta = jnp.sum(p * dp, axis=-1, keepdims=True)
            ds = (p * (dp - delta)).astype(BF16)
            dqs.append(_nn(ds, kpair))
            dk = dk + _tn(ds, qs)
            dv = dv + _tn(p.astype(BF16), dos)
            psd = ps * delta
            for g in range(HEAD_PAIRS):
                part = jnp.sum(psd[g * ATT_BLK:(g + 1) * ATT_BLK], axis=0, keepdims=True)
                dsink = dsink - jnp.where(lane == kh * 4 + g, part, 0.0)
        for g in range(HEAD_PAIRS):
            dq = _unstacked(dqs[0], dqs[1], g) * 0.125
            dq_ref[:, g * 128:(g + 1) * 128] = dq.astype(BF16)
            dbq_ref[:, g * 128:(g + 1) * 128] += jnp.sum(dq, axis=0, keepdims=True)
        dkv = jnp.concatenate([dk[ATT_BLK:], dv[ATT_BLK:]], axis=1) + carry[...]
        dkv_ref[...] = dkv.astype(BF16)
        dbkv_ref[...] += jnp.sum(dkv, axis=0, keepdims=True)
        carry[...] = jnp.concatenate([dk[:ATT_BLK], dv[:ATT_BLK]], axis=1)
        dsink_ref[...] += dsink

    q, kvc, kvp = _attn_specs(nb, lambda j: nb - 1 - j)
    return pl.pallas_call(
        body, name=name, grid=(bsz, nb),
        in_specs=[pl.BlockSpec(memory_space=pltpu.SMEM), q, kvc, kvp,
                  pl.BlockSpec((ATT_BLK, 512), lambda b, j: (b * nb + nb - 1 - j, 0))],
        out_specs=[pl.BlockSpec((ATT_BLK, 512), lambda b, j: (b * nb + nb - 1 - j, 0)),
                   pl.BlockSpec((ATT_BLK, 256), lambda b, j: (b * nb + nb - 1 - j, 0)), _row(128), _row(512), _row(256)],
        out_shape=[jax.ShapeDtypeStruct((t, 512), BF16), jax.ShapeDtypeStruct((t, 256), BF16),
                   jax.ShapeDtypeStruct((1, 128), F32), jax.ShapeDtypeStruct((1, 512), F32),
                   jax.ShapeDtypeStruct((1, 256), F32)],
        scratch_shapes=[pltpu.VMEM((ATT_BLK, 256), F32)],
        compiler_params=_params("arbitrary", "arbitrary"))(sinks, z0, z0, z0, dmix)


def _seq_specs(ts, nt, t, width, col):
    per = ts // HALO
    cur = pl.BlockSpec((ts, width), lambda b, i: (b * nt + i, col))
    prev = pl.BlockSpec((HALO, width), lambda b, i: (jnp.maximum((b * nt + i) * per - 1, 0), col))
    nxt = pl.BlockSpec((HALO, width), lambda b, i: (jnp.minimum((b * nt + i + 1) * per, t // HALO - 1), col))
    return prev, cur, nxt


SUB = 8
CONV_ROWS = 64


def _shifted_copies(src, sh, rows_first, rows_rest):
    for r in range(SUB):
        rows = rows_first if r == 0 else rows_rest
        sh[r, pl.ds(0, rows), :] = src[pl.ds(r, rows), :]


def _tap_sum(sh, w, offset, c0, rows):
    acc = None
    for k in range(CONV_K):
        o = offset(k)
        term = sh[o % SUB, pl.ds(c0 + o - o % SUB, rows), :] * w[k:k + 1, :]
        acc = term if acc is None else acc + term
    return acc


def _glu_rows(a_ref, g_ref, rows=slice(None)):
    return a_ref[rows, :].astype(F32) * jax.nn.sigmoid(g_ref[rows, :].astype(F32))


def _conv_fwd(z0, conv_w, conv_b, ln_g, ln_b, bsz, name):
    t = z0.shape[0]
    s = t // bsz
    ts = _seq_tile(s)
    nt = s // ts
    first = HALO - (CONV_K - 1)

    def body(ap_ref, ac_ref, gp_ref, gc_ref, w_ref, cb_ref, lg_ref, lb_ref, o_ref, y_ref, hbuf, sh):
        hbuf[0:HALO, :] = jnp.where(pl.program_id(1) > 0, _glu_rows(ap_ref, gp_ref), 0.0)
        hbuf[HALO:HALO + ts, :] = _glu_rows(ac_ref, gc_ref)
        _shifted_copies(hbuf, sh, ts + HALO, ts + HALO - SUB)
        w, cb, lg, lb = w_ref[...], cb_ref[...], lg_ref[...], lb_ref[...]
        for c0 in range(0, ts, CONV_ROWS):
            y = _tap_sum(sh, w, lambda k: first + k, c0, CONV_ROWS) + cb
            y_ref[c0:c0 + CONV_ROWS, :] = y
            o = _ln(y, lg, lb)[0]
            o_ref[c0:c0 + CONV_ROWS, :] = (o * jax.nn.sigmoid(o)).astype(BF16)

    ap, ac, _ = _seq_specs(ts, nt, t, 512, 1)
    gp, gc, _ = _seq_specs(ts, nt, t, 512, 2)
    tile = pl.BlockSpec((ts, 512), lambda b, i: (b * nt + i, 0))
    return pl.pallas_call(
        body, name=name, grid=(bsz, nt),
        in_specs=[ap, ac, gp, gc, _full((HALO, 512)), _row(512), _row(512), _row(512)],
        out_specs=[tile, tile],
        out_shape=[jax.ShapeDtypeStruct((t, 512), BF16), jax.ShapeDtypeStruct((t, 512), F32)],
        scratch_shapes=[pltpu.VMEM((HALO + ts, 512), F32), pltpu.VMEM((SUB, HALO + ts, 512), F32)],
        compiler_params=_params("parallel", "parallel"))(z0, z0, z0, z0, conv_w, conv_b, ln_g, ln_b)


def _conv_bwd(z0, y, dmix, conv_w, ln_g, ln_b, bsz, name):
    t = z0.shape[0]
    s = t // bsz
    ts = _seq_tile(s)
    nt = s // ts
    first = HALO - (CONV_K - 1)

    def body(ap_ref, ac_ref, gp_ref, gc_ref, yc_ref, yn_ref, dc_ref, dn_ref, w_ref, lg_ref, lb_ref,
             da_ref, dg_ref, dw_ref, dcb_ref, dlg_ref, dlb_ref, dba_ref, dbg_ref, hbuf, dybuf, sh_h, sh_dy):
        b, i = pl.program_id(0), pl.program_id(1)

        @pl.when((b == 0) & (i == 0))
        def _():
            for ref in (dw_ref, dcb_ref, dlg_ref, dlb_ref, dba_ref, dbg_ref):
                ref[...] = jnp.zeros_like(ref)

        w, lg, lb = w_ref[...], lg_ref[...], lb_ref[...]
        hbuf[0:HALO, :] = jnp.where(i > 0, _glu_rows(ap_ref, gp_ref), 0.0)
        hbuf[HALO:HALO + ts, :] = _glu_rows(ac_ref, gc_ref)
        _shifted_copies(hbuf, sh_h, ts + HALO, ts + HALO - SUB)

        def d_conv_out(yv, dout):
            o, xhat, rstd = _ln(yv, lg, lb)
            sg_o = jax.nn.sigmoid(o)
            d_o = dout * sg_o * (1.0 + o * (1.0 - sg_o))
            return _ln_bwd(d_o, xhat, rstd, lg), d_o * xhat, d_o

        dlg = jnp.zeros((1, 512), F32)
        dlb = jnp.zeros((1, 512), F32)
        dcb = jnp.zeros((1, 512), F32)
        for c0 in range(0, ts, CONV_ROWS):
            rows = slice(c0, c0 + CONV_ROWS)
            dy, g_part, b_part = d_conv_out(yc_ref[rows, :], dc_ref[rows, :].astype(F32))
            dybuf[rows, :] = dy
            dlg = dlg + jnp.sum(g_part, axis=0, keepdims=True)
            dlb = dlb + jnp.sum(b_part, axis=0, keepdims=True)
            dcb = dcb + jnp.sum(dy, axis=0, keepdims=True)
        dn = jnp.where(i < nt - 1, dn_ref[...].astype(F32), 0.0)
        dybuf[ts:ts + HALO, :] = d_conv_out(yn_ref[...], dn)[0]
        dlg_ref[...] += dlg
        dlb_ref[...] += dlb
        dcb_ref[...] += dcb
        _shifted_copies(dybuf, sh_dy, ts + HALO - SUB, ts + HALO - SUB)

        for k in range(CONV_K):
            o = first + k
            prod = dybuf[0:ts, :] * sh_h[o % SUB, pl.ds(o - o % SUB, ts), :]
            dw_ref[pl.ds(k, 1), :] += jnp.sum(prod, axis=0, keepdims=True)
        dba = jnp.zeros((1, 512), F32)
        dbg = jnp.zeros((1, 512), F32)
        for c0 in range(0, ts, CONV_ROWS):
            rows = slice(c0, c0 + CONV_ROWS)
            dh = _tap_sum(sh_dy, w, lambda k: CONV_K - 1 - k, c0, CONV_ROWS)
            a_c = ac_ref[rows, :].astype(F32)
            sg_c = jax.nn.sigmoid(gc_ref[rows, :].astype(F32))
            d_a = dh * sg_c
            d_g = dh * a_c * sg_c * (1.0 - sg_c)
            da_ref[rows, :] = d_a.astype(BF16)
            dg_ref[rows, :] = d_g.astype(BF16)
            dba = dba + jnp.sum(d_a, axis=0, keepdims=True)
            dbg = dbg + jnp.sum(d_g, axis=0, keepdims=True)
        dba_ref[...] += dba
        dbg_ref[...] += dbg

    ap, ac, _ = _seq_specs(ts, nt, t, 512, 1)
    gp, gc, _ = _seq_specs(ts, nt, t, 512, 2)
    _, yc, yn = _seq_specs(ts, nt, t, 512, 0)
    _, dc, dn = _seq_specs(ts, nt, t, 512, 1)
    tile = pl.BlockSpec((ts, 512), lambda b, i: (b * nt + i, 0))
    vec = jax.ShapeDtypeStruct((1, 512), F32)
    return pl.pallas_call(
        body, name=name, grid=(bsz, nt),
        in_specs=[ap, ac, gp, gc, yc, yn, dc, dn, _full((HALO, 512)), _row(512), _row(512)],
        out_specs=[tile, tile, _full((HALO, 512)), _row(512), _row(512), _row(512), _row(512), _row(512)],
        out_shape=[jax.ShapeDtypeStruct((t, 512), BF16), jax.ShapeDtypeStruct((t, 512), BF16),
                   jax.ShapeDtypeStruct((HALO, 512), F32), vec, vec, vec, vec, vec],
        scratch_shapes=[pltpu.VMEM((HALO + ts, 512), F32), pltpu.VMEM((ts + HALO, 512), F32),
                        pltpu.VMEM((SUB, HALO + ts, 512), F32), pltpu.VMEM((SUB, HALO + ts, 512), F32)],
        compiler_params=_params("arbitrary", "arbitrary"))(z0, z0, z0, z0, y, y, dmix, dmix, conv_w, ln_g, ln_b)


def _pooled(pbuf, g, ts, tok):
    w = 2 << g
    cols = slice(128 * g, 128 * (g + 1))
    sm = pbuf[pl.ds(HALO, ts), cols]
    for d in range(1, w):
        sm = sm + pbuf[pl.ds(HALO - d, ts), cols]
    cnt = jnp.minimum(tok + 1, w).astype(F32)
    return sm / cnt - pbuf[pl.ds(HALO, ts), cols]


def _pool_fwd(z1, w_pool, scale, bsz, name):
    t = z1.shape[0]
    s = t // bsz
    ts = _seq_tile(s)
    nt = s // ts

    def body(zp_ref, zc_ref, wp_ref, sc_ref, o_ref, pbuf):
        i = pl.program_id(1)
        pbuf[0:HALO, :] = jnp.where(i > 0, zp_ref[...].astype(F32), 0.0)
        pbuf[HALO:HALO + ts, :] = zc_ref[...].astype(F32)
        tok = i * ts + lax.broadcasted_iota(jnp.int32, (ts, 1), 0)
        for g in range(4):
            cols = slice(128 * g, 128 * (g + 1))
            pooled = _pooled(pbuf, g, ts, tok).astype(BF16)
            o_ref[:, cols] = (_nn(pooled, wp_ref[g].astype(BF16)) * sc_ref[:, cols]).astype(BF16)

    zp, zc, _ = _seq_specs(ts, nt, t, 512, 0)
    return pl.pallas_call(
        body, name=name, grid=(bsz, nt), in_specs=[zp, zc, _full((4, 128, 128)), _row(512)],
        out_specs=pl.BlockSpec((ts, 512), lambda b, i: (b * nt + i, 0)),
        out_shape=jax.ShapeDtypeStruct((t, 512), BF16),
        scratch_shapes=[pltpu.VMEM((HALO + ts, 512), F32)],
        compiler_params=_params("parallel", "parallel"))(z1, z1, w_pool, scale)


def _pool_bwd(z1, dmix, w_pool, scale, bsz, name):
    t = z1.shape[0]
    s = t // bsz
    ts = _seq_tile(s)
    nt = s // ts
    rr = ts + HALO

    def body(zp_ref, zc_ref, dc_ref, dn_ref, wp_ref, sc_ref, dz_ref, dwp_ref, dsc_ref, pbuf, ebuf):
        b, i = pl.program_id(0), pl.program_id(1)

        @pl.when((b == 0) & (i == 0))
        def _():
            dwp_ref[...] = jnp.zeros_like(dwp_ref)
            dsc_ref[...] = jnp.zeros_like(dsc_ref)

        pbuf[0:HALO, :] = jnp.where(i > 0, zp_ref[...].astype(F32), 0.0)
        pbuf[HALO:HALO + ts, :] = zc_ref[...].astype(F32)
        dn = jnp.where(i < nt - 1, dn_ref[...].astype(F32), 0.0)
        dout = jnp.concatenate([dc_ref[...].astype(F32), dn], axis=0)
        tok = i * ts + lax.broadcasted_iota(jnp.int32, (ts, 1), 0)
        tok_r = i * ts + lax.broadcasted_iota(jnp.int32, (rr, 1), 0)
        for g in range(4):
            w = 2 << g
            cols = slice(128 * g, 128 * (g + 1))
            wg = wp_ref[g].astype(BF16)
            pooled = _pooled(pbuf, g, ts, tok).astype(BF16)
            dsc_ref[:, cols] += jnp.sum(dout[:ts, cols] * _nn(pooled, wg), axis=0, keepdims=True)
            dy = (dout[:, cols] * sc_ref[:, cols]).astype(BF16)
            dwp_ref[g] += _tn(pooled, dy[:ts])
            dpl = _nt(dy, wg)
            ebuf[...] = dpl / jnp.minimum(tok_r + 1, w).astype(F32)
            dz = ebuf[pl.ds(0, ts), :] - dpl[:ts]
            for d in range(1, w):
                dz = dz + ebuf[pl.ds(d, ts), :]
            dz_ref[:, cols] = dz.astype(BF16)

    zp, zc, _ = _seq_specs(ts, nt, t, 512, 0)
    _, dc, dn = _seq_specs(ts, nt, t, 512, 0)
    return pl.pallas_call(
        body, name=name, grid=(bsz, nt), in_specs=[zp, zc, dc, dn, _full((4, 128, 128)), _row(512)],
        out_specs=[pl.BlockSpec((ts, 512), lambda b, i: (b * nt + i, 0)), _full((4, 128, 128)), _row(512)],
        out_shape=[jax.ShapeDtypeStruct((t, 512), BF16), jax.ShapeDtypeStruct((4, 128, 128), F32),
                   jax.ShapeDtypeStruct((1, 512), F32)],
        scratch_shapes=[pltpu.VMEM((HALO + ts, 512), F32), pltpu.VMEM((rr, 128), F32)],
        compiler_params=_params("arbitrary", "arbitrary"))(z1, z1, dmix, dmix, w_pool, scale)


def _tril():
    r = lax.broadcasted_iota(jnp.int32, (SGU_CHUNK, SGU_CHUNK), 0)
    c = lax.broadcasted_iota(jnp.int32, (SGU_CHUNK, SGU_CHUNK), 1)
    return r >= c


def _sgu_fwd(z1, ln_g, ln_b, w_s, b_rows, name):
    t = z1.shape[0]
    ts = _tile(t)

    def body(zu_ref, zv_ref, lg_ref, lb_ref, ws_ref, bs_ref, o_ref):
        v = _gelu(zv_ref[...].astype(F32))[0]
        vb = _ln(v, lg_ref[...], lb_ref[...])[0].astype(BF16)
        tril = _tril()
        for g in range(4):
            cols = slice(128 * g, 128 * (g + 1))
            wg = jnp.where(tril, ws_ref[g], 0.0).astype(BF16)
            for c in range(ts // SGU_CHUNK):
                rows = slice(SGU_CHUNK * c, SGU_CHUNK * (c + 1))
                mixed = _nn(wg, vb[rows, cols]) + bs_ref[g]
                o_ref[rows, cols] = (_gelu(zu_ref[rows, cols].astype(F32))[0] * mixed).astype(BF16)

    return pl.pallas_call(
        body, name=name, grid=(t // ts,),
        in_specs=[pl.BlockSpec((ts, 512), lambda i: (i, 1)), pl.BlockSpec((ts, 512), lambda i: (i, 2)),
                  _row(512), _row(512), _full((4, 128, 128)), _full((4, 128, 128))],
        out_specs=pl.BlockSpec((ts, 512), lambda i: (i, 0)), out_shape=jax.ShapeDtypeStruct((t, 512), BF16),
        compiler_params=_params("parallel"))(z1, z1, ln_g, ln_b, w_s, b_rows)


def _sgu_bwd(z1, dmix, ln_g, ln_b, w_s, b_rows, name):
    t = z1.shape[0]
    ts = _tile(t)

    def body(zu_ref, zv_ref, d_ref, lg_ref, lb_ref, ws_ref, bs_ref,
             dzu_ref, dzv_ref, dws_ref, dbs_ref, dlg_ref, dlb_ref, dvbuf):
        @pl.when(pl.program_id(0) == 0)
        def _():
            for ref in (dws_ref, dbs_ref, dlg_ref, dlb_ref):
                ref[...] = jnp.zeros_like(ref)

        zv = zv_ref[...].astype(F32)
        v, thv = _gelu(zv)
        lg = lg_ref[...]
        vln, xhat, rstd = _ln(v, lg, lb_ref[...])
        vb = vln.astype(BF16)
        tril = _tril()
        for g in range(4):
            cols = slice(128 * g, 128 * (g + 1))
            wg = jnp.where(tril, ws_ref[g], 0.0).astype(BF16)
            dws = jnp.zeros((SGU_CHUNK, SGU_CHUNK), F32)
            dbs = jnp.zeros((1, SGU_CHUNK), F32)
            for c in range(ts // SGU_CHUNK):
                rows = slice(SGU_CHUNK * c, SGU_CHUNK * (c + 1))
                vbc = vb[rows, cols]
                mixed = _nn(wg, vbc) + bs_ref[g]
                zu = zu_ref[rows, cols].astype(F32)
                u, thu = _gelu(zu)
                dout = d_ref[rows, cols].astype(F32)
                dzu_ref[rows, cols] = (dout * mixed * _gelu_grad(zu, thu)).astype(BF16)
                dm = dout * u
                dmb = dm.astype(BF16)
                dws = dws + _nt(dmb, vbc)
                dbs = dbs + jnp.sum(dm.T, axis=0, keepdims=True)
                dvbuf[rows, cols] = _tn(wg, dmb)
            dws_ref[g] += jnp.where(tril, dws, 0.0)
            dbs_ref[pl.ds(g, 1), :] += dbs
        dvln = dvbuf[...]
        dlg_ref[...] += jnp.sum(dvln * xhat, axis=0, keepdims=True)
        dlb_ref[...] += jnp.sum(dvln, axis=0, keepdims=True)
        dzv_ref[...] = (_ln_bwd(dvln, xhat, rstd, lg) * _gelu_grad(zv, thv)).astype(BF16)

    tile = pl.BlockSpec((ts, 512), lambda i: (i, 0))
    vec = jax.ShapeDtypeStruct((1, 512), F32)
    return pl.pallas_call(
        body, name=name, grid=(t // ts,),
        in_specs=[pl.BlockSpec((ts, 512), lambda i: (i, 1)), pl.BlockSpec((ts, 512), lambda i: (i, 2)),
                  pl.BlockSpec((ts, 512), lambda i: (i, 1)), _row(512), _row(512), _full((4, 128, 128)),
                  _full((4, 128, 128))],
        out_specs=[tile, tile, _full((4, 128, 128)), _full((4, 128)), _row(512), _row(512)],
        out_shape=[jax.ShapeDtypeStruct((t, 512), BF16), jax.ShapeDtypeStruct((t, 512), BF16),
                   jax.ShapeDtypeStruct((4, 128, 128), F32), jax.ShapeDtypeStruct((4, 128), F32), vec, vec],
        scratch_shapes=[pltpu.VMEM((ts, 512), F32)],
        compiler_params=_params("arbitrary"))(z1, z1, dmix, ln_g, ln_b, w_s, b_rows)


def _row_tile(r):
    for cand in (512, 352, 256, 192, 128, 64, 32, 16, 8):
        if r % cand == 0:
            return cand
    return r


def _sum_slabs(a, name):
    k, r, c = a.shape
    tr = _row_tile(r)

    def body(*refs):
        acc = refs[0][...].astype(F32)
        for ref in refs[1:-1]:
            acc = acc + ref[...].astype(F32)
        refs[-1][...] = acc

    in_specs = [pl.BlockSpec((None, tr, c), functools.partial(lambda i, s: (s, i, 0), s=s)) for s in range(k)]
    return pl.pallas_call(
        body, name=name, grid=(r // tr,), in_specs=in_specs, out_specs=pl.BlockSpec((tr, c), lambda i: (i, 0)),
        out_shape=jax.ShapeDtypeStruct((r, c), F32), compiler_params=_params("parallel"))(*([a] * k))


def _adamw(w, g, m, v, name):
    r, c = w.shape
    tr = _row_tile(r)

    def body(w_ref, g_ref, m_ref, v_ref, d_ref, mo_ref, vo_ref):
        gv = g_ref[...]
        mn = ADAM_B1 * m_ref[...] + (1.0 - ADAM_B1) * gv
        vn = ADAM_B2 * v_ref[...] + (1.0 - ADAM_B2) * (gv * gv)
        m_hat = mn / (1.0 - ADAM_B1 ** ADAM_STEP)
        v_hat = vn / (1.0 - ADAM_B2 ** ADAM_STEP)
        d_ref[...] = -ADAM_LR * (m_hat / (jnp.sqrt(v_hat) + ADAM_EPS) + ADAM_WD * w_ref[...])
        mo_ref[...] = mn
        vo_ref[...] = vn

    spec = pl.BlockSpec((tr, c), lambda i: (i, 0))
    shape = jax.ShapeDtypeStruct((r, c), F32)
    return pl.pallas_call(
        body, name=name, grid=(r // tr,), in_specs=[spec] * 4, out_specs=[spec] * 3, out_shape=[shape] * 3,
        compiler_params=_params("parallel"))(w, g, m, v)


ANY = pl.BlockSpec(memory_space=pl.ANY)


def _all_gather(block, name):
    r, c_dim = block.shape

    def body(x_ref, out_ref, token, send_sems, recv_sems, local_sem):
        token[...] = jnp.zeros_like(token)
        x, y, c = lax.axis_index("x"), lax.axis_index("y"), lax.axis_index("c")
        me, sibling = (x, y, c), (x, y, 1 - c)
        chips = [(1 - x, y), (x, 1 - y), (1 - x, 1 - y)]

        def rows(px, py, pc):
            return out_ref.at[4 * px + 2 * py + pc]

        def copy(k, blk, to, src=None):
            return pltpu.make_async_remote_copy(
                src_ref=rows(*blk) if src is None else src, dst_ref=rows(*blk), send_sem=send_sems.at[k],
                recv_sem=recv_sems.at[k], device_id=to, device_id_type=MESH)

        mine = pltpu.make_async_copy(x_ref, rows(*me), local_sem)
        mine.start()
        first = [copy(0, me, sibling, src=x_ref)]
        first += [copy(1 + j, me, (*chip, c), src=x_ref) for j, chip in enumerate(chips)]
        for cp in first:
            cp.start()
        passed = [copy(4 + j, (*chip, c), sibling) for j, chip in enumerate(chips)]
        for j, chip in enumerate(chips):
            copy(1 + j, (*chip, c), me).wait_recv()
            passed[j].start()
        copy(0, sibling, me).wait_recv()
        for j, chip in enumerate(chips):
            copy(4 + j, (*chip, 1 - c), me).wait_recv()
        for cp in first + passed:
            cp.wait_send()
        mine.wait()

    return pl.pallas_call(
        body, name=name, in_specs=[ANY], out_specs=[ANY, pl.BlockSpec(memory_space=pltpu.VMEM)],
        out_shape=[jax.ShapeDtypeStruct((N_DEV, r, c_dim), block.dtype), jax.ShapeDtypeStruct((8, 128), F32)],
        scratch_shapes=[pltpu.SemaphoreType.DMA((7,)), pltpu.SemaphoreType.DMA((7,)), pltpu.SemaphoreType.DMA],
    )(block)


HBM = pl.BlockSpec(memory_space=pltpu.HBM)
SEM = pl.BlockSpec(memory_space=pltpu.SEMAPHORE)
EFFECT = pltpu.SideEffectType.DATAFLOW_SIDE_EFFECTING


def _exchange_copies(scatter, src_refs, land_refs, send_sems, recv_sems, local_sems):
    x, y, c = lax.axis_index("x"), lax.axis_index("y"), lax.axis_index("c")
    me = 4 * x + 2 * y + c
    sends, arrivals, locals_ = [], [], []
    for a, (src, land) in enumerate(zip(src_refs, land_refs)):
        def pick(idx, src=src):
            return src.at[idx] if scatter else src

        locals_.append(pltpu.make_async_copy(pick(me), land.at[me], local_sems.at[a]))
        for r in range(1, N_DEV):
            px = 1 - x if r & 4 else x
            py = 1 - y if r & 2 else y
            pc = 1 - c if r & 1 else c
            peer, s = 4 * px + 2 * py + pc, 7 * a + r - 1
            sends.append(pltpu.make_async_remote_copy(
                src_ref=pick(peer), dst_ref=land.at[me], send_sem=send_sems.at[s], recv_sem=recv_sems.at[s],
                device_id=(px, py, pc), device_id_type=MESH))
            arrivals.append(pltpu.make_async_remote_copy(
                src_ref=pick(peer), dst_ref=land.at[peer], send_sem=send_sems.at[s], recv_sem=recv_sems.at[s],
                device_id=(px, py, pc), device_id_type=MESH))
    return sends, arrivals, locals_


def _exchange_start(srcs, scatter, name):
    n = len(srcs)
    lands = [lax.empty((N_DEV,) + s.shape[-2:], s.dtype) for s in srcs]

    def body(*refs):
        src_refs, land_refs = refs[:n], refs[n:2 * n]
        send_sems, recv_sems, local_sems = refs[2 * n:2 * n + 3]
        token = refs[-1]
        sends, _, locals_ = _exchange_copies(scatter, src_refs, land_refs, send_sems, recv_sems, local_sems)
        for cp in locals_ + sends:
            cp.start()
        token[...] = jnp.zeros_like(token)

    res = pl.pallas_call(
        body, name=name,
        out_shape=[pltpu.SemaphoreType.DMA((7 * n,)), pltpu.SemaphoreType.DMA((7 * n,)), pltpu.SemaphoreType.DMA((n,))]
        + [pltpu.HBM(a.shape, a.dtype) for a in list(srcs) + lands] + [jax.ShapeDtypeStruct((8, 128), F32)],
        in_specs=[HBM] * (2 * n), out_specs=[SEM] * 3 + [HBM] * (2 * n) + [pl.BlockSpec(memory_space=pltpu.VMEM)],
        input_output_aliases={i: 3 + i for i in range(2 * n)},
        compiler_params=pltpu.CompilerParams(has_side_effects=EFFECT),
    )(*[pltpu.with_memory_space_constraint(a, pltpu.HBM) for a in list(srcs) + lands])
    return (n, scatter, res[:3], res[3:3 + 2 * n]), res[-1]


def _exchange_wait(handle, after, name):
    n, scatter, sems, thru = handle

    def body(*refs):
        src_refs, land_refs = refs[:n], refs[n:2 * n]
        send_sems, recv_sems, local_sems = refs[2 * n:2 * n + 3]
        sends, arrivals, locals_ = _exchange_copies(scatter, src_refs, land_refs, send_sems, recv_sems, local_sems)
        for cp in arrivals:
            cp.wait_recv()
        for cp in sends:
            cp.wait_send()
        for cp in locals_:
            cp.wait()

    res = pl.pallas_call(
        body, name=name, out_shape=[pltpu.HBM(a.shape, a.dtype) for a in thru],
        in_specs=[HBM] * (2 * n) + [SEM] * 3 + [ANY], out_specs=[HBM] * (2 * n),
        input_output_aliases={i: i for i in range(2 * n)},
        compiler_params=pltpu.CompilerParams(has_side_effects=EFFECT),
    )(*thru, *sems, after)
    return res[n:]


def _behind(tokens, a):
    zero = sum(tok[0, 0] for tok in tokens)
    return jax.tree.map(lambda v: v + zero.astype(v.dtype), a)


def _perm_heads(a, perm, axis):
    idx = [slice(None)] * a.ndim
    parts = []
    for h in perm:
        idx[axis] = slice(64 * h, 64 * (h + 1))
        parts.append(a[tuple(idx)])
    idx[axis] = slice(512, None)
    if a.shape[axis] > 512:
        parts.append(a[tuple(idx)])
    return jnp.concatenate(parts, axis=axis)


Q_INV = tuple(int(i) for i in np.argsort(Q_PERM))


def _in0_to_kernel(a, axis):
    a = _perm_heads(a, Q_PERM, axis)
    idx = [slice(None)] * a.ndim

    def cut(lo, hi):
        idx[axis] = slice(lo, hi)
        return a[tuple(idx)]

    return jnp.concatenate([cut(0, 512), cut(768, 1792), cut(512, 768)], axis=axis)


def _in0_from_kernel(a, axis):
    idx = [slice(None)] * a.ndim

    def cut(lo, hi):
        idx[axis] = slice(lo, hi)
        return a[tuple(idx)]

    a = jnp.concatenate([cut(0, 512), cut(1536, 1792), cut(512, 1536)], axis=axis)
    return _perm_heads(a, Q_INV, axis)


def _f32_as_u16_rows(v, rows):
    bits = lax.bitcast_convert_type(v, jnp.uint16).reshape(-1)
    return jnp.pad(bits, (0, rows * D - bits.shape[0])).reshape(rows, D)


def _pad_rows(v, rows):
    v = v.reshape(-1)
    return jnp.pad(v, (0, rows * D - v.shape[0])).reshape(rows, D)


def kernel(x, mix_norm, a_w_in, a_b_in, a_sinks, a_conv_w, a_conv_b, a_cln_g, a_cln_b, a_w_out, c_w_in, c_w_pool, c_pool_scale, c_sln_g, c_sln_b, c_w_s, c_b_s, c_w_out, ffn_norm, ffn_w_gate, ffn_w_up, ffn_w_down, final_norm, loss_target, m_mix_norm, m_a_w_in, m_a_b_in, m_a_sinks, m_a_conv_w, m_a_conv_b, m_a_cln_g, m_a_cln_b, m_a_w_out, m_c_w_in, m_c_w_pool, m_c_pool_scale, m_c_sln_g, m_c_sln_b, m_c_w_s, m_c_b_s, m_c_w_out, m_ffn_norm, m_ffn_w_gate, m_ffn_w_up, m_ffn_w_down, m_final_norm, v_mix_norm, v_a_w_in, v_a_b_in, v_a_sinks, v_a_conv_w, v_a_conv_b, v_a_cln_g, v_a_cln_b, v_a_w_out, v_c_w_in, v_c_w_pool, v_c_pool_scale, v_c_sln_g, v_c_sln_b, v_c_w_s, v_c_b_s, v_c_w_out, v_ffn_norm, v_ffn_w_gate, v_ffn_w_up, v_ffn_w_down, v_final_norm):
    bsz, seq, _ = x.shape
    t = bsz * seq
    w_in = dict(mix_norm=mix_norm, a_w_in=a_w_in, a_b_in=a_b_in, a_sinks=a_sinks, a_conv_w=a_conv_w, a_conv_b=a_conv_b,
                a_cln_g=a_cln_g, a_cln_b=a_cln_b, a_w_out=a_w_out, c_w_in=c_w_in, c_w_pool=c_w_pool,
                c_pool_scale=c_pool_scale, c_sln_g=c_sln_g, c_sln_b=c_sln_b, c_w_s=c_w_s, c_b_s=c_b_s, c_w_out=c_w_out,
                ffn_norm=ffn_norm, ffn_w_gate=ffn_w_gate, ffn_w_up=ffn_w_up, ffn_w_down=ffn_w_down, final_norm=final_norm)
    m_in = dict(mix_norm=m_mix_norm, a_w_in=m_a_w_in, a_b_in=m_a_b_in, a_sinks=m_a_sinks, a_conv_w=m_a_conv_w,
                a_conv_b=m_a_conv_b, a_cln_g=m_a_cln_g, a_cln_b=m_a_cln_b, a_w_out=m_a_w_out, c_w_in=m_c_w_in,
                c_w_pool=m_c_w_pool, c_pool_scale=m_c_pool_scale, c_sln_g=m_c_sln_g, c_sln_b=m_c_sln_b, c_w_s=m_c_w_s,
                c_b_s=m_c_b_s, c_w_out=m_c_w_out, ffn_norm=m_ffn_norm, ffn_w_gate=m_ffn_w_gate, ffn_w_up=m_ffn_w_up,
                ffn_w_down=m_ffn_w_down, final_norm=m_final_norm)
    v_in = dict(mix_norm=v_mix_norm, a_w_in=v_a_w_in, a_b_in=v_a_b_in, a_sinks=v_a_sinks, a_conv_w=v_a_conv_w,
                a_conv_b=v_a_conv_b, a_cln_g=v_a_cln_g, a_cln_b=v_a_cln_b, a_w_out=v_a_w_out, c_w_in=v_c_w_in,
                c_w_pool=v_c_w_pool, c_pool_scale=v_c_pool_scale, c_sln_g=v_c_sln_g, c_sln_b=v_c_sln_b, c_w_s=v_c_w_s,
                c_b_s=v_c_b_s, c_w_out=v_c_w_out, ffn_norm=v_ffn_norm, ffn_w_gate=v_ffn_w_gate, ffn_w_up=v_ffn_w_up,
                ffn_w_down=v_ffn_w_down, final_norm=v_final_norm)

    small = jnp.concatenate([a_conv_w[0].reshape(-1), c_pool_scale[0], c_sln_g[0], c_sln_b[0]])
    first_bits = lax.bitcast_convert_type(jnp.concatenate([a_w_in[0].T, a_w_out[0]], axis=0).astype(BF16), jnp.uint16)
    gathered, tok = _all_gather(jnp.concatenate([first_bits, _f32_as_u16_rows(small, W_MISC_ROWS)], axis=0), "gather_mixer0")

    def ffn_shards(l):
        return [ffn_w_gate[l].T.astype(BF16), ffn_w_up[l].T.astype(BF16), ffn_w_down[l].astype(BF16)]

    ffn0_h, tok = _exchange_start(_behind([tok], ffn_shards(0)), False, "gather_ffn0_start")
    mix1_h, tok = _exchange_start(_behind([tok], [c_w_in[0].T.astype(BF16), c_w_out[0].astype(BF16)]), False,
                                  "gather_mixer1_start")
    ffn1_h, tok = _exchange_start(_behind([tok], ffn_shards(1)), False, "gather_ffn1_start")

    a_in_full = lax.bitcast_convert_type(gathered[:, :224].reshape(IN0, D), BF16)
    a_out_full = lax.bitcast_convert_type(gathered[:, 224:352].reshape(D, D), BF16)
    small_all = lax.bitcast_convert_type(
        gathered[:, 352:].reshape(N_DEV, -1)[:, :2 * SMALL_SHARD].reshape(N_DEV, SMALL_SHARD, 2), F32)
    conv_w = small_all[:, :31 * 64].reshape(N_DEV, 31, 64).transpose(1, 0, 2).reshape(31, 512)
    conv_w = jnp.pad(conv_w, ((0, HALO - CONV_K), (0, 0)))
    pool_scale = small_all[:, 31 * 64:31 * 64 + 64].reshape(1, 512)
    sln_g = small_all[:, 31 * 64 + 64:31 * 64 + 128].reshape(1, 512)
    sln_b = small_all[:, 31 * 64 + 128:].reshape(1, 512)

    wt_in0 = _in0_to_kernel(a_in_full, 0)
    b_in0 = _in0_to_kernel(a_b_in, 1)
    w_out0 = _perm_heads(a_out_full, Q_PERM, 0)
    b_rows = jnp.broadcast_to(c_b_s[0][:, :, None], (4, 128, 128))
    conv_b, cln_g, cln_b = a_conv_b, a_cln_g, a_cln_b

    h0 = x.reshape(t, D)
    target = loss_target.reshape(t, D)
    z0, hn0 = _norm_proj(h0, _behind([tok], mix_norm[0:1]), wt_in0, b_in0, "in_proj0")
    attn = _attn_fwd(z0, a_sinks, bsz, "attn_fwd")
    conv, conv_y = _conv_fwd(z0, conv_w, conv_b, cln_g, cln_b, bsz, "conv_fwd")
    h1 = _out_proj(attn, conv, w_out0, h0, "out_proj0")
    wtg0, wtu0, wd0 = (w.reshape(D_FF, D) for w in _exchange_wait(ffn0_h, h1, "gather_ffn0_wait"))
    h2, hnf0, gate0, up0 = _ffn_fwd(h1, ffn_norm[0:1], wtg0, wtu0, wd0, "ffn_fwd0")
    wt_in1, w_out1 = (w.reshape(-1, D) for w in _exchange_wait(mix1_h, h2, "gather_mixer1_wait"))
    z1, hn1 = _norm_proj(h2, mix_norm[1:2], wt_in1, None, "in_proj1")
    pool = _pool_fwd(z1, c_w_pool[0], pool_scale, bsz, "pool_fwd")
    sgu = _sgu_fwd(z1, sln_g, sln_b, c_w_s[0], b_rows, "sgu_fwd")
    h3 = _out_proj(pool, sgu, w_out1, h2, "out_proj1")
    wtg1, wtu1, wd1 = (w.reshape(D_FF, D) for w in _exchange_wait(ffn1_h, h3, "gather_ffn1_wait"))
    h4, hnf1, gate1, up1 = _ffn_fwd(h3, ffn_norm[1:2], wtg1, wtu1, wd1, "ffn_fwd1")

    def blocks(g):
        return g.reshape(N_DEV, g.shape[0] // N_DEV, D)

    dh4, d_final_norm, loss_part = _loss_head(h4, final_norm.reshape(1, D), target, "loss_head")
    dh3, dgate1, dup1, act1, d_fn1 = _ffn_bwd(dh4, h3, ffn_norm[1:2], gate1, up1, wtg1, wtu1, wd1, "ffn_bwd1")
    gw_ffn1 = [_mm_tn(dgate1, hnf1, "dw_gate1"), _mm_tn(dup1, hnf1, "dw_up1"), _mm_tn(act1, dh4, "dw_down1")]
    ffn1_g, tok = _exchange_start([blocks(g) for g in gw_ffn1], True, "scatter_ffn1_start")
    dmix1 = _dmix(dh3, _behind([tok], w_out1), "dmix1")
    gw_c_out = jnp.concatenate([_mm_tn(pool, dh3, "dw_out1_pool"), _mm_tn(sgu, dh3, "dw_out1_sgu")], axis=0)
    dzp, d_w_pool, d_pool_scale = _pool_bwd(z1, dmix1, c_w_pool[0], pool_scale, bsz, "pool_bwd")
    dzu, dzv, d_w_s, d_b_s, d_sln_g, d_sln_b = _sgu_bwd(z1, dmix1, sln_g, sln_b, c_w_s[0], b_rows, "sgu_bwd")
    dh2, d_mn1 = _proj_bwd_norm([(dzp, 0), (dzu, 512), (dzv, 1024)], wt_in1, h2, dh3, mix_norm[1:2], "in_proj1_bwd")
    gw_c_in = jnp.concatenate([_mm_tn(dzp, hn1, "dw_in1_pool"), _mm_tn(dzu, hn1, "dw_in1_u"),
                               _mm_tn(dzv, hn1, "dw_in1_v")], axis=0)
    mix1_g, tok = _exchange_start([blocks(gw_c_in), blocks(gw_c_out)], True, "scatter_mixer1_start")
    dh1, dgate0, dup0, act0, d_fn0 = _ffn_bwd(dh2, h1, _behind([tok], ffn_norm[0:1]), gate0, up0, wtg0, wtu0, wd0, "ffn_bwd0")
    gw_ffn0 = [_mm_tn(dgate0, hnf0, "dw_gate0"), _mm_tn(dup0, hnf0, "dw_up0"), _mm_tn(act0, dh2, "dw_down0")]
    ffn0_g, tok = _exchange_start([blocks(g) for g in gw_ffn0], True, "scatter_ffn0_start")
    dmix0 = _dmix(dh1, _behind([tok], w_out0), "dmix0")
    gw_a_out = _perm_heads(jnp.concatenate([_mm_tn(attn, dh1, "dw_out0_attn"), _mm_tn(conv, dh1, "dw_out0_conv")],
                                           axis=0), Q_INV, 0)
    dq, dkv, d_sink_row, d_bq, d_bkv = _attn_bwd(z0, dmix0, a_sinks, bsz, "attn_bwd")
    dca, dcg, d_conv_w, d_conv_b, d_cln_g, d_cln_b, d_ba, d_bg = _conv_bwd(z0, conv_y, dmix0, conv_w, cln_g, cln_b, bsz, "conv_bwd")
    dx, d_mn0 = _proj_bwd_norm([(dq, 0), (dca, 512), (dcg, 1024), (dkv, 1536)], wt_in0, h0, dh1, mix_norm[0:1], "in_proj0_bwd")
    gw_a_in = _in0_from_kernel(jnp.concatenate(
        [_mm_tn(dq, hn0, "dw_in0_q"), _mm_tn(dca, hn0, "dw_in0_a"), _mm_tn(dcg, hn0, "dw_in0_g"),
         _mm_tn(dkv, hn0, "dw_in0_kv")], axis=0), 0)
    d_b_in = _in0_from_kernel(jnp.concatenate([d_bq, d_ba, d_bg, d_bkv], axis=1), 1)

    rep = dict(mix_norm=jnp.concatenate([d_mn0, d_mn1], axis=0), a_b_in=d_b_in, a_sinks=d_sink_row[:, :8],
               a_conv_b=d_conv_b, a_cln_g=d_cln_g, a_cln_b=d_cln_b, c_w_pool=d_w_pool[None], c_w_s=d_w_s[None],
               c_b_s=d_b_s[None], ffn_norm=jnp.concatenate([d_fn0, d_fn1], axis=0), final_norm=d_final_norm.reshape(D))
    rep_flat = jnp.concatenate([rep[nm].reshape(-1) for nm in REP_NAMES])
    rep_flat = jnp.pad(rep_flat, (0, N_DEV * REP_ROWS * D - rep_flat.shape[0])).reshape(N_DEV, REP_ROWS, D)
    small_g = jnp.concatenate([
        d_conv_w[:CONV_K].reshape(31, N_DEV, 64).transpose(1, 0, 2).reshape(N_DEV, 31 * 64),
        d_pool_scale.reshape(N_DEV, 64), d_sln_g.reshape(N_DEV, 64), d_sln_b.reshape(N_DEV, 64)], axis=1)
    small_g = jnp.pad(small_g, ((0, 0), (0, G_SMALL_ROWS * D - SMALL_SHARD))).reshape(N_DEV, G_SMALL_ROWS, D)
    g_misc = jnp.concatenate([small_g, rep_flat], axis=1)
    mix0_g, tok = _exchange_start([blocks(gw_a_in), blocks(gw_a_out), g_misc], True, "scatter_mixer0_start")

    names = list(w_in)
    g_out, delta, new_m, new_v = {}, {}, {}, {}

    def adamw(nm):
        shp = w_in[nm].shape
        two_d = (shp[0] * shp[1], shp[2])
        res = _adamw(w_in[nm].reshape(two_d), g_out[nm].reshape(two_d), m_in[nm].reshape(two_d), v_in[nm].reshape(two_d),
                     "adamw_" + nm)
        delta[nm], new_m[nm], new_v[nm] = (r.reshape(shp) for r in res)

    g1, u1, d1 = (_sum_slabs(a, "sum_ffn1_" + s) for a, s in zip(_exchange_wait(ffn1_g, tok, "scatter_ffn1_wait"), "gud"))
    c_in_g, c_out_g = (_sum_slabs(a, "sum_mixer1_" + s) for a, s in
                       zip(_exchange_wait(mix1_g, g1, "scatter_mixer1_wait"), ("in", "out")))
    g0, u0, d0 = (_sum_slabs(a, "sum_ffn0_" + s) for a, s in zip(_exchange_wait(ffn0_g, c_in_g, "scatter_ffn0_wait"), "gud"))
    g_out.update(c_w_in=c_in_g.T[None], c_w_out=c_out_g[None], ffn_w_gate=jnp.stack([g0.T, g1.T]),
                 ffn_w_up=jnp.stack([u0.T, u1.T]), ffn_w_down=jnp.stack([d0, d1]))
    for nm in ("c_w_in", "c_w_out", "ffn_w_gate", "ffn_w_up", "ffn_w_down"):
        adamw(nm)
    a_in_g, a_out_g, g_tail = (_sum_slabs(a, "sum_mixer0_" + s) for a, s in
                               zip(_exchange_wait(mix0_g, delta["ffn_w_down"], "scatter_mixer0_wait"), ("in", "out", "tail")))
    rep_all = _all_gather(g_tail[G_SMALL_ROWS:], "gather_replicated_grads")[0].reshape(-1)
    small_r = g_tail[:G_SMALL_ROWS].reshape(-1)[:SMALL_SHARD]
    g_out.update(
        a_w_in=a_in_g.T[None], a_w_out=a_out_g[None], a_conv_w=small_r[:31 * 64].reshape(1, 31, 64),
        c_pool_scale=small_r[31 * 64:31 * 64 + 64].reshape(1, 64), c_sln_g=small_r[31 * 64 + 64:31 * 64 + 128].reshape(1, 64),
        c_sln_b=small_r[31 * 64 + 128:].reshape(1, 64))
    off = 0
    for nm in REP_NAMES:
        n = int(np.prod(w_in[nm].shape))
        g_out[nm] = rep_all[off:off + n].reshape(w_in[nm].shape)
        off += n
    adamw("a_w_in")
    adamw("a_w_out")
    for group, rows, label in ((("a_conv_w", "c_pool_scale", "c_sln_g", "c_sln_b"), G_SMALL_ROWS, "adamw_small_sharded"),
                               (REP_NAMES, N_DEV * REP_ROWS, "adamw_replicated")):
        flat = [_pad_rows(jnp.concatenate([d[nm].reshape(-1) for nm in group]), rows) for d in (w_in, g_out, m_in, v_in)]
        res = [r.reshape(-1) for r in _adamw(*flat, label)]
        off = 0
        for nm in group:
            n = int(np.prod(w_in[nm].shape))
            delta[nm], new_m[nm], new_v[nm] = (r[off:off + n].reshape(w_in[nm].shape) for r in res)
            off += n

    loss = lax.psum(loss_part[0, 0], ("x", "y", "c"))
    grad_x = dx.reshape(bsz, seq, D)
    return (loss, grad_x, *[g_out[nm] for nm in names], *[delta[nm] for nm in names],
            *[new_m[nm] for nm in names], *[new_v[nm] for nm in names])
```

```python
import functools

import jax
import jax.numpy as jnp
import numpy as np
from jax import lax
from jax.experimental import pallas as pl
from jax.experimental.pallas import tpu as pltpu

F32 = jnp.float32
BF16 = jnp.bfloat16
MESH = pl.DeviceIdType.MESH

D = 1024
N_DEV = 8
EPS = 1e-5
HEAD_PAIRS = 4
ATT_BLK = 128
CONV_K = 31
HALO = 32
D_FF = 2816
FF_TILE_FWD = D_FF // 2
FF_TILE_BWD = 256
IN0 = 1792
IN1 = 1536
POOL_WINDOWS = (2, 4, 8, 16)
SGU_CHUNK = 128
GELU_C = 0.7978845608028654
GELU_A = 0.044715
ADAM_LR, ADAM_B1, ADAM_B2, ADAM_EPS, ADAM_WD, ADAM_STEP = 0.001, 0.9, 0.999, 1e-08, 0.01, 10
VMEM_LIMIT = 56 << 20

SMALL_SHARD = 31 * 64 + 3 * 64
W_MISC_ROWS = 16
G_MISC_ROWS = 32
G_SMALL_ROWS = 8
REP_ROWS = G_MISC_ROWS - G_SMALL_ROWS
REP_NAMES = ("mix_norm", "a_b_in", "a_sinks", "a_conv_b", "a_cln_g", "a_cln_b", "c_w_pool", "c_w_s", "c_b_s",
             "ffn_norm", "final_norm")
Q_PERM = (0, 4, 1, 5, 2, 6, 3, 7)


def _params(*sem):
    return pltpu.CompilerParams(dimension_semantics=sem, vmem_limit_bytes=VMEM_LIMIT)


def _nn(a, b):
    return jnp.dot(a, b, preferred_element_type=F32)


def _nt(a, b):
    return lax.dot_general(a, b, (((1,), (1,)), ((), ())), preferred_element_type=F32)


def _tn(a, b):
    return lax.dot_general(a, b, (((0,), (0,)), ((), ())), preferred_element_type=F32)


def _tile(n, want=512):
    t = min(want, n)
    assert n % t == 0, (n, t)
    return t


def _seq_tile(s):
    return 512 if s >= 1024 else s // 2


def _rms(x, g):
    r = lax.rsqrt(jnp.mean(x * x, axis=-1, keepdims=True) + EPS)
    return x * r * g, r


def _rms_bwd(x, g, d_y):
    r = lax.rsqrt(jnp.mean(x * x, axis=-1, keepdims=True) + EPS)
    xr = x * r
    u = d_y * g
    d_x = r * (u - xr * jnp.mean(u * xr, axis=-1, keepdims=True))
    return d_x, jnp.sum(d_y * xr, axis=0, keepdims=True)


def _ln(y, g, b):
    mu = jnp.mean(y, axis=-1, keepdims=True)
    yc = y - mu
    rstd = lax.rsqrt(jnp.mean(yc * yc, axis=-1, keepdims=True) + EPS)
    xhat = yc * rstd
    return xhat * g + b, xhat, rstd


def _ln_bwd(d_o, xhat, rstd, g):
    dxh = d_o * g
    return rstd * (dxh - jnp.mean(dxh, axis=-1, keepdims=True) - xhat * jnp.mean(dxh * xhat, axis=-1, keepdims=True))


def _gelu(x):
    th = jnp.tanh(GELU_C * (x + GELU_A * x * x * x))
    return 0.5 * x * (1.0 + th), th


def _gelu_grad(x, th):
    return 0.5 * (1.0 + th) + 0.5 * x * (1.0 - th * th) * GELU_C * (1.0 + 3.0 * GELU_A * x * x)


def _row(c):
    return pl.BlockSpec((1, c), lambda *_: (0, 0))


def _full(shape):
    return pl.BlockSpec(shape, lambda *_: (0,) * len(shape))


def _norm_proj(h, g, wt, bias, name):
    t, n = h.shape[0], wt.shape[0]
    tm = _tile(t)
    has_bias = bias is not None

    def body(*refs):
        h_ref, g_ref, wt_ref = refs[:3]
        z_ref, hn_ref = refs[-2:]
        hn = _rms(h_ref[...], g_ref[...])[0].astype(BF16)
        hn_ref[...] = hn
        z = _nt(hn, wt_ref[...])
        if has_bias:
            z = z + refs[3][...]
        z_ref[...] = z.astype(BF16)

    in_specs = [pl.BlockSpec((tm, D), lambda i: (i, 0)), _row(D), _full((n, D))]
    args = [h, g, wt]
    if has_bias:
        in_specs.append(_row(n))
        args.append(bias)
    return pl.pallas_call(
        body, name=name, grid=(t // tm,), in_specs=in_specs,
        out_specs=[pl.BlockSpec((tm, n), lambda i: (i, 0)), pl.BlockSpec((tm, D), lambda i: (i, 0))],
        out_shape=[jax.ShapeDtypeStruct((t, n), BF16), jax.ShapeDtypeStruct((t, D), BF16)],
        compiler_params=_params("parallel"))(*args)


def _out_proj(a, b, w, h, name):
    t = h.shape[0]
    tm = _tile(t)

    def body(a_ref, b_ref, wa_ref, wb_ref, h_ref, o_ref):
        o_ref[...] = h_ref[...] + _nn(a_ref[...], wa_ref[...]) + _nn(b_ref[...], wb_ref[...])

    half = pl.BlockSpec((tm, 512), lambda i: (i, 0))
    return pl.pallas_call(
        body, name=name, grid=(t // tm,),
        in_specs=[half, half, pl.BlockSpec((512, D), lambda i: (0, 0)), pl.BlockSpec((512, D), lambda i: (1, 0)),
                  pl.BlockSpec((tm, D), lambda i: (i, 0))],
        out_specs=pl.BlockSpec((tm, D), lambda i: (i, 0)), out_shape=jax.ShapeDtypeStruct((t, D), F32),
        compiler_params=_params("parallel"))(a, b, w, w, h)


def _dmix(dh, w, name):
    t = dh.shape[0]
    tm = _tile(t)

    def body(dh_ref, w_ref, o_ref):
        o_ref[...] = _nt(dh_ref[...].astype(BF16), w_ref[...]).astype(BF16)

    return pl.pallas_call(
        body, name=name, grid=(t // tm,), in_specs=[pl.BlockSpec((tm, D), lambda i: (i, 0)), _full((D, D))],
        out_specs=pl.BlockSpec((tm, D), lambda i: (i, 0)), out_shape=jax.ShapeDtypeStruct((t, D), BF16),
        compiler_params=_params("parallel"))(dh, w)


def _ffn_fwd(h, g, wtg, wtu, wd, name):
    t = h.shape[0]
    tm, tf = _tile(t), FF_TILE_FWD
    nf = D_FF // tf

    def body(h_ref, g_ref, wtg_ref, wtu_ref, wd_ref, o_ref, hn_ref, gate_ref, up_ref):
        @pl.when(pl.program_id(1) == 0)
        def _():
            x = h_ref[...]
            hn_ref[...] = _rms(x, g_ref[...])[0].astype(BF16)
            o_ref[...] = x

        hn = hn_ref[...]
        gate = _nt(hn, wtg_ref[...])
        up = _nt(hn, wtu_ref[...])
        gate_ref[...] = gate.astype(BF16)
        up_ref[...] = up.astype(BF16)
        act = (gate * jax.nn.sigmoid(gate) * up).astype(BF16)
        o_ref[...] += _nn(act, wd_ref[...])

    tok = pl.BlockSpec((tm, D), lambda i, f: (i, 0))
    wsp = pl.BlockSpec((tf, D), lambda i, f: (f, 0))
    mid = pl.BlockSpec((tm, tf), lambda i, f: (i, f))
    return pl.pallas_call(
        body, name=name, grid=(t // tm, nf), in_specs=[tok, _row(D), wsp, wsp, wsp],
        out_specs=[tok, tok, mid, mid],
        out_shape=[jax.ShapeDtypeStruct((t, D), F32), jax.ShapeDtypeStruct((t, D), BF16),
                   jax.ShapeDtypeStruct((t, D_FF), BF16), jax.ShapeDtypeStruct((t, D_FF), BF16)],
        compiler_params=_params("parallel", "arbitrary"))(h, g, wtg, wtu, wd)


def _ffn_bwd(dh, h, g, gate, up, wtg, wtu, wd, name):
    t = h.shape[0]
    tm, tf = _tile(t, 1024), FF_TILE_BWD
    nf = D_FF // tf

    def body(dh_ref, h_ref, g_ref, gate_ref, up_ref, wtg_ref, wtu_ref, wd_ref,
             dhin_ref, dgate_ref, dup_ref, act_ref, dg_ref, dhb):
        i, f = pl.program_id(0), pl.program_id(1)

        @pl.when(f == 0)
        def _():
            dhb[...] = dh_ref[...].astype(BF16)
            dhin_ref[...] = jnp.zeros_like(dhin_ref)

        @pl.when((i == 0) & (f == 0))
        def _():
            dg_ref[...] = jnp.zeros_like(dg_ref)

        dact = _nt(dhb[...], wd_ref[...])
        gt = gate_ref[...].astype(F32)
        u = up_ref[...].astype(F32)
        sg = jax.nn.sigmoid(gt)
        sil = gt * sg
        act_ref[...] = (sil * u).astype(BF16)
        dup = (dact * sil).astype(BF16)
        dgate = (dact * u * sg * (1.0 + gt * (1.0 - sg))).astype(BF16)
        dup_ref[...] = dup
        dgate_ref[...] = dgate
        dhin_ref[...] += _nn(dgate, wtg_ref[...]) + _nn(dup, wtu_ref[...])

        @pl.when(f == nf - 1)
        def _():
            d_x, d_g = _rms_bwd(h_ref[...], g_ref[...], dhin_ref[...])
            dhin_ref[...] = dh_ref[...] + d_x
            dg_ref[...] += d_g

    tok = pl.BlockSpec((tm, D), lambda i, f: (i, 0))
    wsp = pl.BlockSpec((tf, D), lambda i, f: (f, 0))
    mid = pl.BlockSpec((tm, tf), lambda i, f: (i, f))
    mid_shape = jax.ShapeDtypeStruct((t, D_FF), BF16)
    return pl.pallas_call(
        body, name=name, grid=(t // tm, nf), in_specs=[tok, tok, _row(D), mid, mid, wsp, wsp, wsp],
        out_specs=[tok, mid, mid, mid, _row(D)],
        out_shape=[jax.ShapeDtypeStruct((t, D), F32), mid_shape, mid_shape, mid_shape, jax.ShapeDtypeStruct((1, D), F32)],
        scratch_shapes=[pltpu.VMEM((tm, D), BF16)],
        compiler_params=_params("arbitrary", "arbitrary"))(dh, h, g, gate, up, wtg, wtu, wd)


def _proj_bwd_norm(pieces, wt, h, dh, g, name):
    t = h.shape[0]
    tm = _tile(t)
    n_p = len(pieces)

    def body(*refs):
        p_refs, w_refs = refs[:n_p], refs[n_p:2 * n_p]
        h_ref, dh_ref, g_ref, o_ref, dg_ref = refs[2 * n_p:]

        @pl.when(pl.program_id(0) == 0)
        def _():
            dg_ref[...] = jnp.zeros_like(dg_ref)

        d_hn = _nn(p_refs[0][...], w_refs[0][...])
        for p_ref, w_ref in zip(p_refs[1:], w_refs[1:]):
            d_hn = d_hn + _nn(p_ref[...], w_ref[...])
        d_x, d_g = _rms_bwd(h_ref[...], g_ref[...], d_hn)
        o_ref[...] = dh_ref[...] + d_x
        dg_ref[...] += d_g

    tok = pl.BlockSpec((tm, D), lambda i: (i, 0))
    in_specs = [pl.BlockSpec((tm, a.shape[1]), lambda i: (i, 0)) for a, _ in pieces]
    for a, off in pieces:
        w = a.shape[1]
        assert off % w == 0
        in_specs.append(pl.BlockSpec((w, D), functools.partial(lambda i, blk: (blk, 0), blk=off // w)))
    in_specs += [tok, tok, _row(D)]
    return pl.pallas_call(
        body, name=name, grid=(t // tm,), in_specs=in_specs, out_specs=[tok, _row(D)],
        out_shape=[jax.ShapeDtypeStruct((t, D), F32), jax.ShapeDtypeStruct((1, D), F32)],
        compiler_params=_params("arbitrary"))(*[a for a, _ in pieces], *([wt] * n_p), h, dh, g)


def _mm_tn(a, b, name):
    t, n = a.shape
    k = b.shape[1]
    tn = n if n <= 1024 else n // 2
    tt = _tile(t, 1024)
    nt = t // tt

    def body(a_ref, b_ref, o_ref, acc):
        s = pl.program_id(1)

        @pl.when(s == 0)
        def _():
            acc[...] = jnp.zeros_like(acc)

        acc[...] += _tn(a_ref[...], b_ref[...].astype(BF16))

        @pl.when(s == nt - 1)
        def _():
            o_ref[...] = acc[...].astype(BF16)

    return pl.pallas_call(
        body, name=name, grid=(n // tn, nt),
        in_specs=[pl.BlockSpec((tt, tn), lambda j, s: (s, j)), pl.BlockSpec((tt, k), lambda j, s: (s, 0))],
        out_specs=pl.BlockSpec((tn, k), lambda j, s: (j, 0)), out_shape=jax.ShapeDtypeStruct((n, k), BF16),
        scratch_shapes=[pltpu.VMEM((tn, k), F32)],
        compiler_params=_params("parallel", "arbitrary"))(a, b)


def _loss_head(h, g, target, name):
    t = h.shape[0]
    tm = _tile(t)

    def body(h_ref, g_ref, t_ref, dh_ref, dg_ref, loss_ref):
        @pl.when(pl.program_id(0) == 0)
        def _():
            dg_ref[...] = jnp.zeros_like(dg_ref)
            loss_ref[...] = jnp.zeros_like(loss_ref)

        x = h_ref[...]
        gv = g_ref[...]
        err = _rms(x, gv)[0] - t_ref[...]
        per_tok = jnp.mean(err * err, axis=-1, keepdims=True)
        loss_ref[...] += 0.5 * jnp.sum(per_tok, axis=0, keepdims=True)
        d_x, d_g = _rms_bwd(x, gv, err * (1.0 / D))
        dh_ref[...] = d_x
        dg_ref[...] += d_g

    tok = pl.BlockSpec((tm, D), lambda i: (i, 0))
    return pl.pallas_call(
        body, name=name, grid=(t // tm,), in_specs=[tok, _row(D), tok],
        out_specs=[tok, _row(D), _row(1)],
        out_shape=[jax.ShapeDtypeStruct((t, D), F32), jax.ShapeDtypeStruct((1, D), F32), jax.ShapeDtypeStruct((1, 1), F32)],
        compiler_params=_params("arbitrary"))(h, g, target)


STACK = HEAD_PAIRS * ATT_BLK


def _attn_valid(first, rows):
    qi = lax.broadcasted_iota(jnp.int32, (rows, 2 * ATT_BLK), 0) % ATT_BLK
    r = lax.broadcasted_iota(jnp.int32, (rows, 2 * ATT_BLK), 1)
    dist = qi + ATT_BLK - r
    return (dist >= 0) & (dist < ATT_BLK) & ((r >= ATT_BLK) | jnp.logical_not(first))


def _stacked(ref, kh, scale):
    lo = lax.broadcasted_iota(jnp.int32, (ATT_BLK, 128), 1) < 64
    keep = lo if kh == 0 else ~lo
    parts = [jnp.where(keep, ref[:, g * 128:(g + 1) * 128] * scale, 0.0).astype(BF16) for g in range(HEAD_PAIRS)]
    return jnp.concatenate(parts, axis=0)


def _unstacked(a0, a1, g):
    lo = lax.broadcasted_iota(jnp.int32, (ATT_BLK, 128), 1) < 64
    rows = slice(g * ATT_BLK, (g + 1) * ATT_BLK)
    return jnp.where(lo, a0[rows], a1[rows])


def _sink_column(s_ref, kh):
    return jnp.concatenate([jnp.full((ATT_BLK, 1), s_ref[0, kh * 4 + g], F32) for g in range(HEAD_PAIRS)], axis=0)


def _attn_probs(qs, kpair, sink, valid):
    s = jnp.where(valid, _nt(qs, kpair), -1e30)
    m = jnp.maximum(jnp.max(s, axis=-1, keepdims=True), sink)
    p = jnp.exp(s - m)
    es = jnp.exp(sink - m)
    inv = 1.0 / (jnp.sum(p, axis=-1, keepdims=True) + es)
    return p * inv, es * inv


def _attn_specs(nb, order):
    q = pl.BlockSpec((ATT_BLK, 512), lambda b, j: (b * nb + order(j), 0))
    kvc = pl.BlockSpec((ATT_BLK, 256), lambda b, j: (b * nb + order(j), 6))
    kvp = pl.BlockSpec((ATT_BLK, 256), lambda b, j: (jnp.maximum(b * nb + order(j) - 1, 0), 6))
    return q, kvc, kvp


def _window_kv(kvc_ref, kvp_ref):
    kvc, kvp = kvc_ref[...], kvp_ref[...]
    kpair = jnp.concatenate([kvp[:, :128], kvc[:, :128]], axis=0)
    vpair = jnp.concatenate([kvp[:, 128:], kvc[:, 128:]], axis=0)
    return kpair, vpair


def _attn_fwd(z0, sinks, bsz, name):
    t = z0.shape[0]
    nb = t // bsz // ATT_BLK

    def body(s_ref, q_ref, kvc_ref, kvp_ref, o_ref):
        valid = _attn_valid(pl.program_id(1) == 0, ATT_BLK)
        kpair, vpair = _window_kv(kvc_ref, kvp_ref)
        lo = lax.broadcasted_iota(jnp.int32, (ATT_BLK, 128), 1) < 64
        for g in range(HEAD_PAIRS):
            qs = q_ref[:, g * 128:(g + 1) * 128] * 0.125
            outs = []
            for kh in range(2):
                qm = jnp.where(lo if kh == 0 else ~lo, qs, 0.0).astype(BF16)
                p, _ = _attn_probs(qm, kpair, s_ref[0, kh * 4 + g], valid)
                outs.append(_nn(p.astype(BF16), vpair))
            o_ref[:, g * 128:(g + 1) * 128] = jnp.where(lo, outs[0], outs[1]).astype(BF16)

    q, kvc, kvp = _attn_specs(nb, lambda j: j)
    return pl.pallas_call(
        body, name=name, grid=(bsz, nb),
        in_specs=[pl.BlockSpec(memory_space=pltpu.SMEM), q, kvc, kvp],
        out_specs=pl.BlockSpec((ATT_BLK, 512), lambda b, j: (b * nb + j, 0)),
        out_shape=jax.ShapeDtypeStruct((t, 512), BF16),
        compiler_params=_params("parallel", "parallel"))(sinks, z0, z0, z0)


def _attn_bwd(z0, dmix, sinks, bsz, name):
    t = z0.shape[0]
    nb = t // bsz // ATT_BLK

    def body(s_ref, q_ref, kvc_ref, kvp_ref, do_ref, dq_ref, dkv_ref, dsink_ref, dbq_ref, dbkv_ref, carry):
        b, j = pl.program_id(0), pl.program_id(1)

        @pl.when(j == 0)
        def _():
            carry[...] = jnp.zeros_like(carry)

        @pl.when((b == 0) & (j == 0))
        def _():
            dsink_ref[...] = jnp.zeros_like(dsink_ref)
            dbq_ref[...] = jnp.zeros_like(dbq_ref)
            dbkv_ref[...] = jnp.zeros_like(dbkv_ref)

        valid = _attn_valid(j == nb - 1, STACK)
        kpair, vpair = _window_kv(kvc_ref, kvp_ref)
        lane = lax.broadcasted_iota(jnp.int32, (1, 128), 1)
        dk = jnp.zeros((2 * ATT_BLK, 128), F32)
        dv = jnp.zeros((2 * ATT_BLK, 128), F32)
        dsink = jnp.zeros((1, 128), F32)
        dqs = []
        for kh in range(2):
            qs = _stacked(q_ref, kh, 0.125)
            dos = _stacked(do_ref, kh, 1.0)
            p, ps = _attn_probs(qs, kpair, _sink_column(s_ref, kh), valid)
            dp = _nt(dos, vpair)
            delta = jnp.sum(p * dp, axis=-1, keepdims=True)
            ds = (p * (dp - delta)).astype(BF16)
            dqs.append(_nn(ds, kpair))
            dk = dk + _tn(ds, qs)
            dv = dv + _tn(p.astype(BF16), dos)
            psd = ps * delta
            for g in range(HEAD_PAIRS):
                part = jnp.sum(psd[g * ATT_BLK:(g + 1) * ATT_BLK], axis=0, keepdims=True)
                dsink = dsink - jnp.where(lane == kh * 4 + g, part, 0.0)
        for g in range(HEAD_PAIRS):
            dq = _unstacked(dqs[0], dqs[1], g) * 0.125
            dq_ref[:, g * 128:(g + 1) * 128] = dq.astype(BF16)
            dbq_ref[:, g * 128:(g + 1) * 128] += jnp.sum(dq, axis=0, keepdims=True)
        dkv = jnp.concatenate([dk[ATT_BLK:], dv[ATT_BLK:]], axis=1) + carry[...]
        dkv_ref[...] = dkv.astype(BF16)
        dbkv_ref[...] += jnp.sum(dkv, axis=0, keepdims=True)
        carry[...] = jnp.concatenate([dk[:ATT_BLK], dv[:ATT_BLK]], axis=1)
        dsink_ref[...] += dsink

    q, kvc, kvp = _attn_specs(nb, lambda j: nb - 1 - j)
    return pl.pallas_call(
        body, name=name, grid=(bsz, nb),
        in_specs=[pl.BlockSpec(memory_space=pltpu.SMEM), q, kvc, kvp,
                  pl.BlockSpec((ATT_BLK, 512), lambda b, j: (b * nb + nb - 1 - j, 0))],
        out_specs=[pl.BlockSpec((ATT_BLK, 512), lambda b, j: (b * nb + nb - 1 - j, 0)),
                   pl.BlockSpec((ATT_BLK, 256), lambda b, j: (b * nb + nb - 1 - j, 0)), _row(128), _row(512), _row(256)],
        out_shape=[jax.ShapeDtypeStruct((t, 512), BF16), jax.ShapeDtypeStruct((t, 256), BF16),
                   jax.ShapeDtypeStruct((1, 128), F32), jax.ShapeDtypeStruct((1, 512), F32),
                   jax.ShapeDtypeStruct((1, 256), F32)],
        scratch_shapes=[pltpu.VMEM((ATT_BLK, 256), F32)],
        compiler_params=_params("arbitrary", "arbitrary"))(sinks, z0, z0, z0, dmix)


def _seq_specs(ts, nt, t, width, col):
    per = ts // HALO
    cur = pl.BlockSpec((ts, width), lambda b, i: (b * nt + i, col))
    prev = pl.BlockSpec((HALO, width), lambda b, i: (jnp.maximum((b * nt + i) * per - 1, 0), col))
    nxt = pl.BlockSpec((HALO, width), lambda b, i: (jnp.minimum((b * nt + i + 1) * per, t // HALO - 1), col))
    return prev, cur, nxt


SUB = 8
CONV_ROWS = 64


def _shifted_copies(src, sh, rows_first, rows_rest):
    for r in range(SUB):
        rows = rows_first if r == 0 else rows_rest
        sh[r, pl.ds(0, rows), :] = src[pl.ds(r, rows), :]


def _tap_sum(sh, w, offset, c0, rows):
    acc = None
    for k in range(CONV_K):
        o = offset(k)
        term = sh[o % SUB, pl.ds(c0 + o - o % SUB, rows), :] * w[k:k + 1, :]
        acc = term if acc is None else acc + term
    return acc


def _glu_rows(a_ref, g_ref, rows=slice(None)):
    return a_ref[rows, :].astype(F32) * jax.nn.sigmoid(g_ref[rows, :].astype(F32))


def _conv_fwd(z0, conv_w, conv_b, ln_g, ln_b, bsz, name):
    t = z0.shape[0]
    s = t // bsz
    ts = _seq_tile(s)
    nt = s // ts
    first = HALO - (CONV_K - 1)

    def body(ap_ref, ac_ref, gp_ref, gc_ref, w_ref, cb_ref, lg_ref, lb_ref, o_ref, y_ref, hbuf, sh):
        hbuf[0:HALO, :] = jnp.where(pl.program_id(1) > 0, _glu_rows(ap_ref, gp_ref), 0.0)
        hbuf[HALO:HALO + ts, :] = _glu_rows(ac_ref, gc_ref)
        _shifted_copies(hbuf, sh, ts + HALO, ts + HALO - SUB)
        w, cb, lg, lb = w_ref[...], cb_ref[...], lg_ref[...], lb_ref[...]
        for c0 in range(0, ts, CONV_ROWS):
            y = _tap_sum(sh, w, lambda k: first + k, c0, CONV_ROWS) + cb
            y_ref[c0:c0 + CONV_ROWS, :] = y
            o = _ln(y, lg, lb)[0]
            o_ref[c0:c0 + CONV_ROWS, :] = (o * jax.nn.sigmoid(o)).astype(BF16)

    ap, ac, _ = _seq_specs(ts, nt, t, 512, 1)
    gp, gc, _ = _seq_specs(ts, nt, t, 512, 2)
    tile = pl.BlockSpec((ts, 512), lambda b, i: (b * nt + i, 0))
    return pl.pallas_call(
        body, name=name, grid=(bsz, nt),
        in_specs=[ap, ac, gp, gc, _full((HALO, 512)), _row(512), _row(512), _row(512)],
        out_specs=[tile, tile],
        out_shape=[jax.ShapeDtypeStruct((t, 512), BF16), jax.ShapeDtypeStruct((t, 512), F32)],
        scratch_shapes=[pltpu.VMEM((HALO + ts, 512), F32), pltpu.VMEM((SUB, HALO + ts, 512), F32)],
        compiler_params=_params("parallel", "parallel"))(z0, z0, z0, z0, conv_w, conv_b, ln_g, ln_b)


def _conv_bwd(z0, y, dmix, conv_w, ln_g, ln_b, bsz, name):
    t = z0.shape[0]
    s = t // bsz
    ts = _seq_tile(s)
    nt = s // ts
    first = HALO - (CONV_K - 1)

    def body(ap_ref, ac_ref, gp_ref, gc_ref, yc_ref, yn_ref, dc_ref, dn_ref, w_ref, lg_ref, lb_ref,
             da_ref, dg_ref, dw_ref, dcb_ref, dlg_ref, dlb_ref, dba_ref, dbg_ref, hbuf, dybuf, sh_h, sh_dy):
        b, i = pl.program_id(0), pl.program_id(1)

        @pl.when((b == 0) & (i == 0))
        def _():
            for ref in (dw_ref, dcb_ref, dlg_ref, dlb_ref, dba_ref, dbg_ref):
                ref[...] = jnp.zeros_like(ref)

        w, lg, lb = w_ref[...], lg_ref[...], lb_ref[...]
        hbuf[0:HALO, :] = jnp.where(i > 0, _glu_rows(ap_ref, gp_ref), 0.0)
        hbuf[HALO:HALO + ts, :] = _glu_rows(ac_ref, gc_ref)
        _shifted_copies(hbuf, sh_h, ts + HALO, ts + HALO - SUB)

        def d_conv_out(yv, dout):
            o, xhat, rstd = _ln(yv, lg, lb)
            sg_o = jax.nn.sigmoid(o)
            d_o = dout * sg_o * (1.0 + o * (1.0 - sg_o))
            return _ln_bwd(d_o, xhat, rstd, lg), d_o * xhat, d_o

        dlg = jnp.zeros((1, 512), F32)
        dlb = jnp.zeros((1, 512), F32)
        dcb = jnp.zeros((1, 512), F32)
        for c0 in range(0, ts, CONV_ROWS):
            rows = slice(c0, c0 + CONV_ROWS)
            dy, g_part, b_part = d_conv_out(yc_ref[rows, :], dc_ref[rows, :].astype(F32))
            dybuf[rows, :] = dy
            dlg = dlg + jnp.sum(g_part, axis=0, keepdims=True)
            dlb = dlb + jnp.sum(b_part, axis=0, keepdims=True)
            dcb = dcb + jnp.sum(dy, axis=0, keepdims=True)
        dn = jnp.where(i < nt - 1, dn_ref[...].astype(F32), 0.0)
        dybuf[ts:ts + HALO, :] = d_conv_out(yn_ref[...], dn)[0]
        dlg_ref[...] += dlg
        dlb_ref[...] += dlb
        dcb_ref[...] += dcb
        _shifted_copies(dybuf, sh_dy, ts + HALO - SUB, ts + HALO - SUB)

        for k in range(CONV_K):
            o = first + k
            prod = dybuf[0:ts, :] * sh_h[o % SUB, pl.ds(o - o % SUB, ts), :]
            dw_ref[pl.ds(k, 1), :] += jnp.sum(prod, axis=0, keepdims=True)
        dba = jnp.zeros((1, 512), F32)
        dbg = jnp.zeros((1, 512), F32)
        for c0 in range(0, ts, CONV_ROWS):
            rows = slice(c0, c0 + CONV_ROWS)
            dh = _tap_sum(sh_dy, w, lambda k: CONV_K - 1 - k, c0, CONV_ROWS)
            a_c = ac_ref[rows, :].astype(F32)
            sg_c = jax.nn.sigmoid(gc_ref[rows, :].astype(F32))
            d_a = dh * sg_c
            d_g = dh * a_c * sg_c * (1.0 - sg_c)
            da_ref[rows, :] = d_a.astype(BF16)
            dg_ref[rows, :] = d_g.astype(BF16)
            dba = dba + jnp.sum(d_a, axis=0, keepdims=True)
            dbg = dbg + jnp.sum(d_g, axis=0, keepdims=True)
        dba_ref[...] += dba
        dbg_ref[...] += dbg

    ap, ac, _ = _seq_specs(ts, nt, t, 512, 1)
    gp, gc, _ = _seq_specs(ts, nt, t, 512, 2)
    _, yc, yn = _seq_specs(ts, nt, t, 512, 0)
    _, dc, dn = _seq_specs(ts, nt, t, 512, 1)
    tile = pl.BlockSpec((ts, 512), lambda b, i: (b * nt + i, 0))
    vec = jax.ShapeDtypeStruct((1, 512), F32)
    return pl.pallas_call(
        body, name=name, grid=(bsz, nt),
        in_specs=[ap, ac, gp, gc, yc, yn, dc, dn, _full((HALO, 512)), _row(512), _row(512)],
        out_specs=[tile, tile, _full((HALO, 512)), _row(512), _row(512), _row(512), _row(512), _row(512)],
        out_shape=[jax.ShapeDtypeStruct((t, 512), BF16), jax.ShapeDtypeStruct((t, 512), BF16),
                   jax.ShapeDtypeStruct((HALO, 512), F32), vec, vec, vec, vec, vec],
        scratch_shapes=[pltpu.VMEM((HALO + ts, 512), F32), pltpu.VMEM((ts + HALO, 512), F32),
                        pltpu.VMEM((SUB, HALO + ts, 512), F32), pltpu.VMEM((SUB, HALO + ts, 512), F32)],
        compiler_params=_params("arbitrary", "arbitrary"))(z0, z0, z0, z0, y, y, dmix, dmix, conv_w, ln_g, ln_b)


def _pooled(pbuf, g, ts, tok):
    w = 2 << g
    cols = slice(128 * g, 128 * (g + 1))
    sm = pbuf[pl.ds(HALO, ts), cols]
    for d in range(1, w):
        sm = sm + pbuf[pl.ds(HALO - d, ts), cols]
    cnt = jnp.minimum(tok + 1, w).astype(F32)
    return sm / cnt - pbuf[pl.ds(HALO, ts), cols]


def _pool_fwd(z1, w_pool, scale, bsz, name):
    t = z1.shape[0]
    s = t // bsz
    ts = _seq_tile(s)
    nt = s // ts

    def body(zp_ref, zc_ref, wp_ref, sc_ref, o_ref, pbuf):
        i = pl.program_id(1)
        pbuf[0:HALO, :] = jnp.where(i > 0, zp_ref[...].astype(F32), 0.0)
        pbuf[HALO:HALO + ts, :] = zc_ref[...].astype(F32)
        tok = i * ts + lax.broadcasted_iota(jnp.int32, (ts, 1), 0)
        for g in range(4):
            cols = slice(128 * g, 128 * (g + 1))
            pooled = _pooled(pbuf, g, ts, tok).astype(BF16)
            o_ref[:, cols] = (_nn(pooled, wp_ref[g].astype(BF16)) * sc_ref[:, cols]).astype(BF16)

    zp, zc, _ = _seq_specs(ts, nt, t, 512, 0)
    return pl.pallas_call(
        body, name=name, grid=(bsz, nt), in_specs=[zp, zc, _full((4, 128, 128)), _row(512)],
        out_specs=pl.BlockSpec((ts, 512), lambda b, i: (b * nt + i, 0)),
        out_shape=jax.ShapeDtypeStruct((t, 512), BF16),
        scratch_shapes=[pltpu.VMEM((HALO + ts, 512), F32)],
        compiler_params=_params("parallel", "parallel"))(z1, z1, w_pool, scale)


def _pool_bwd(z1, dmix, w_pool, scale, bsz, name):
    t = z1.shape[0]
    s = t // bsz
    ts = _seq_tile(s)
    nt = s // ts
    rr = ts + HALO

    def body(zp_ref, zc_ref, dc_ref, dn_ref, wp_ref, sc_ref, dz_ref, dwp_ref, dsc_ref, pbuf, ebuf):
        b, i = pl.program_id(0), pl.program_id(1)

        @pl.when((b == 0) & (i == 0))
        def _():
            dwp_ref[...] = jnp.zeros_like(dwp_ref)
            dsc_ref[...] = jnp.zeros_like(dsc_ref)

        pbuf[0:HALO, :] = jnp.where(i > 0, zp_ref[...].astype(F32), 0.0)
        pbuf[HALO:HALO + ts, :] = zc_ref[...].astype(F32)
        dn = jnp.where(i < nt - 1, dn_ref[...].astype(F32), 0.0)
        dout = jnp.concatenate([dc_ref[...].astype(F32), dn], axis=0)
        tok = i * ts + lax.broadcasted_iota(jnp.int32, (ts, 1), 0)
        tok_r = i * ts + lax.broadcasted_iota(jnp.int32, (rr, 1), 0)
        for g in range(4):
            w = 2 << g
            cols = slice(128 * g, 128 * (g + 1))
            wg = wp_ref[g].astype(BF16)
            pooled = _pooled(pbuf, g, ts, tok).astype(BF16)
            dsc_ref[:, cols] += jnp.sum(dout[:ts, cols] * _nn(pooled, wg), axis=0, keepdims=True)
            dy = (dout[:, cols] * sc_ref[:, cols]).astype(BF16)
            dwp_ref[g] += _tn(pooled, dy[:ts])
            dpl = _nt(dy, wg)
            ebuf[...] = dpl / jnp.minimum(tok_r + 1, w).astype(F32)
            dz = ebuf[pl.ds(0, ts), :] - dpl[:ts]
            for d in range(1, w):
                dz = dz + ebuf[pl.ds(d, ts), :]
            dz_ref[:, cols] = dz.astype(BF16)

    zp, zc, _ = _seq_specs(ts, nt, t, 512, 0)
    _, dc, dn = _seq_specs(ts, nt, t, 512, 0)
    return pl.pallas_call(
        body, name=name, grid=(bsz, nt), in_specs=[zp, zc, dc, dn, _full((4, 128, 128)), _row(512)],
        out_specs=[pl.BlockSpec((ts, 512), lambda b, i: (b * nt + i, 0)), _full((4, 128, 128)), _row(512)],
        out_shape=[jax.ShapeDtypeStruct((t, 512), BF16), jax.ShapeDtypeStruct((4, 128, 128), F32),
                   jax.ShapeDtypeStruct((1, 512), F32)],
        scratch_shapes=[pltpu.VMEM((HALO + ts, 512), F32), pltpu.VMEM((rr, 128), F32)],
        compiler_params=_params("arbitrary", "arbitrary"))(z1, z1, dmix, dmix, w_pool, scale)


def _tril():
    r = lax.broadcasted_iota(jnp.int32, (SGU_CHUNK, SGU_CHUNK), 0)
    c = lax.broadcasted_iota(jnp.int32, (SGU_CHUNK, SGU_CHUNK), 1)
    return r >= c


def _sgu_fwd(z1, ln_g, ln_b, w_s, b_rows, name):
    t = z1.shape[0]
    ts = _tile(t)

    def body(zu_ref, zv_ref, lg_ref, lb_ref, ws_ref, bs_ref, o_ref):
        v = _gelu(zv_ref[...].astype(F32))[0]
        vb = _ln(v, lg_ref[...], lb_ref[...])[0].astype(BF16)
        tril = _tril()
        for g in range(4):
            cols = slice(128 * g, 128 * (g + 1))
            wg = jnp.where(tril, ws_ref[g], 0.0).astype(BF16)
            for c in range(ts // SGU_CHUNK):
                rows = slice(SGU_CHUNK * c, SGU_CHUNK * (c + 1))
                mixed = _nn(wg, vb[rows, cols]) + bs_ref[g]
                o_ref[rows, cols] = (_gelu(zu_ref[rows, cols].astype(F32))[0] * mixed).astype(BF16)

    return pl.pallas_call(
        body, name=name, grid=(t // ts,),
        in_specs=[pl.BlockSpec((ts, 512), lambda i: (i, 1)), pl.BlockSpec((ts, 512), lambda i: (i, 2)),
                  _row(512), _row(512), _full((4, 128, 128)), _full((4, 128, 128))],
        out_specs=pl.BlockSpec((ts, 512), lambda i: (i, 0)), out_shape=jax.ShapeDtypeStruct((t, 512), BF16),
        compiler_params=_params("parallel"))(z1, z1, ln_g, ln_b, w_s, b_rows)


def _sgu_bwd(z1, dmix, ln_g, ln_b, w_s, b_rows, name):
    t = z1.shape[0]
    ts = _tile(t)

    def body(zu_ref, zv_ref, d_ref, lg_ref, lb_ref, ws_ref, bs_ref,
             dzu_ref, dzv_ref, dws_ref, dbs_ref, dlg_ref, dlb_ref, dvbuf):
        @pl.when(pl.program_id(0) == 0)
        def _():
            for ref in (dws_ref, dbs_ref, dlg_ref, dlb_ref):
                ref[...] = jnp.zeros_like(ref)

        zv = zv_ref[...].astype(F32)
        v, thv = _gelu(zv)
        lg = lg_ref[...]
        vln, xhat, rstd = _ln(v, lg, lb_ref[...])
        vb = vln.astype(BF16)
        tril = _tril()
        for g in range(4):
            cols = slice(128 * g, 128 * (g + 1))
            wg = jnp.where(tril, ws_ref[g], 0.0).astype(BF16)
            dws = jnp.zeros((SGU_CHUNK, SGU_CHUNK), F32)
            dbs = jnp.zeros((1, SGU_CHUNK), F32)
            for c in range(ts // SGU_CHUNK):
                rows = slice(SGU_CHUNK * c, SGU_CHUNK * (c + 1))
                vbc = vb[rows, cols]
                mixed = _nn(wg, vbc) + bs_ref[g]
                zu = zu_ref[rows, cols].astype(F32)
                u, thu = _gelu(zu)
                dout = d_ref[rows, cols].astype(F32)
                dzu_ref[rows, cols] = (dout * mixed * _gelu_grad(zu, thu)).astype(BF16)
                dm = dout * u
                dmb = dm.astype(BF16)
                dws = dws + _nt(dmb, vbc)
                dbs = dbs + jnp.sum(dm.T, axis=0, keepdims=True)
                dvbuf[rows, cols] = _tn(wg, dmb)
            dws_ref[g] += jnp.where(tril, dws, 0.0)
            dbs_ref[pl.ds(g, 1), :] += dbs
        dvln = dvbuf[...]
        dlg_ref[...] += jnp.sum(dvln * xhat, axis=0, keepdims=True)
        dlb_ref[...] += jnp.sum(dvln, axis=0, keepdims=True)
        dzv_ref[...] = (_ln_bwd(dvln, xhat, rstd, lg) * _gelu_grad(zv, thv)).astype(BF16)

    tile = pl.BlockSpec((ts, 512), lambda i: (i, 0))
    vec = jax.ShapeDtypeStruct((1, 512), F32)
    return pl.pallas_call(
        body, name=name, grid=(t // ts,),
        in_specs=[pl.BlockSpec((ts, 512), lambda i: (i, 1)), pl.BlockSpec((ts, 512), lambda i: (i, 2)),
                  pl.BlockSpec((ts, 512), lambda i: (i, 1)), _row(512), _row(512), _full((4, 128, 128)),
                  _full((4, 128, 128))],
        out_specs=[tile, tile, _full((4, 128, 128)), _full((4, 128)), _row(512), _row(512)],
        out_shape=[jax.ShapeDtypeStruct((t, 512), BF16), jax.ShapeDtypeStruct((t, 512), BF16),
                   jax.ShapeDtypeStruct((4, 128, 128), F32), jax.ShapeDtypeStruct((4, 128), F32), vec, vec],
        scratch_shapes=[pltpu.VMEM((ts, 512), F32)],
        compiler_params=_params("arbitrary"))(z1, z1, dmix, ln_g, ln_b, w_s, b_rows)


def _row_tile(r):
    for cand in (512, 352, 256, 192, 128, 64, 32, 16, 8):
        if r % cand == 0:
            return cand
    return r


def _sum_slabs(a, name):
    k, r, c = a.shape
    tr = _row_tile(r)

    def body(*refs):
        acc = refs[0][...].astype(F32)
        for ref in refs[1:-1]:
            acc = acc + ref[...].astype(F32)
        refs[-1][...] = acc

    in_specs = [pl.BlockSpec((None, tr, c), functools.partial(lambda i, s: (s, i, 0), s=s)) for s in range(k)]
    return pl.pallas_call(
        body, name=name, grid=(r // tr,), in_specs=in_specs, out_specs=pl.BlockSpec((tr, c), lambda i: (i, 0)),
        out_shape=jax.ShapeDtypeStruct((r, c), F32), compiler_params=_params("parallel"))(*([a] * k))


def _adamw_math(w, g, m, v):
    mn = ADAM_B1 * m + (1.0 - ADAM_B1) * g
    vn = ADAM_B2 * v + (1.0 - ADAM_B2) * (g * g)
    m_hat = mn / (1.0 - ADAM_B1 ** ADAM_STEP)
    v_hat = vn / (1.0 - ADAM_B2 ** ADAM_STEP)
    return -ADAM_LR * (m_hat / (jnp.sqrt(v_hat) + ADAM_EPS) + ADAM_WD * w), mn, vn


def _reduce_adamw(landing, w, m, v, name, layer=None, into=None):
    k, r, c = landing.shape
    tr = _row_tile(r)
    n_into = 0 if into is None else 4

    def body(*refs):
        slabs, (w_ref, m_ref, v_ref) = refs[:k], refs[k:k + 3]
        g_ref, d_ref, mo_ref, vo_ref = refs[k + 3 + n_into:]
        g = slabs[0][...].astype(F32)
        for ref in slabs[1:]:
            g = g + ref[...].astype(F32)
        g_ref[...] = g
        d_ref[...], mo_ref[...], vo_ref[...] = _adamw_math(w_ref[...], g, m_ref[...], v_ref[...])

    if layer is None:
        spec = pl.BlockSpec((tr, c), lambda i: (i, 0))
    else:
        spec = pl.BlockSpec((None, tr, c), lambda i: (layer, i, 0))
    in_specs = [pl.BlockSpec((None, tr, c), functools.partial(lambda i, s: (s, i, 0), s=s)) for s in range(k)]
    in_specs += [spec] * 3 + [ANY] * n_into
    shape = jax.ShapeDtypeStruct(w.shape, F32)
    return pl.pallas_call(
        body, name=name, grid=(r // tr,), in_specs=in_specs, out_specs=[spec] * 4, out_shape=[shape] * 4,
        input_output_aliases={k + 3 + j: j for j in range(n_into)},
        compiler_params=_params("parallel"))(*([landing] * k), w, m, v, *(into or ()))


def _adamw(w, g, m, v, name):
    r, c = w.shape
    tr = _row_tile(r)

    def body(w_ref, g_ref, m_ref, v_ref, d_ref, mo_ref, vo_ref):
        d_ref[...], mo_ref[...], vo_ref[...] = _adamw_math(w_ref[...], g_ref[...], m_ref[...], v_ref[...])

    spec = pl.BlockSpec((tr, c), lambda i: (i, 0))
    shape = jax.ShapeDtypeStruct((r, c), F32)
    return pl.pallas_call(
        body, name=name, grid=(r // tr,), in_specs=[spec] * 4, out_specs=[spec] * 3, out_shape=[shape] * 3,
        compiler_params=_params("parallel"))(w, g, m, v)


ANY = pl.BlockSpec(memory_space=pl.ANY)


def _all_gather(block, name):
    r, c_dim = block.shape

    def body(x_ref, out_ref, token, send_sems, recv_sems, local_sem):
        token[...] = jnp.zeros_like(token)
        x, y, c = lax.axis_index("x"), lax.axis_index("y"), lax.axis_index("c")
        me, sibling = (x, y, c), (x, y, 1 - c)
        chips = [(1 - x, y), (x, 1 - y), (1 - x, 1 - y)]

        def rows(px, py, pc):
            return out_ref.at[4 * px + 2 * py + pc]

        def copy(k, blk, to, src=None):
            return pltpu.make_async_remote_copy(
                src_ref=rows(*blk) if src is None else src, dst_ref=rows(*blk), send_sem=send_sems.at[k],
                recv_sem=recv_sems.at[k], device_id=to, device_id_type=MESH)

        mine = pltpu.make_async_copy(x_ref, rows(*me), local_sem)
        mine.start()
        first = [copy(0, me, sibling, src=x_ref)]
        first += [copy(1 + j, me, (*chip, c), src=x_ref) for j, chip in enumerate(chips)]
        for cp in first:
            cp.start()
        passed = [copy(4 + j, (*chip, c), sibling) for j, chip in enumerate(chips)]
        for j, chip in enumerate(chips):
            copy(1 + j, (*chip, c), me).wait_recv()
            passed[j].start()
        copy(0, sibling, me).wait_recv()
        for j, chip in enumerate(chips):
            copy(4 + j, (*chip, 1 - c), me).wait_recv()
        for cp in first + passed:
            cp.wait_send()
        mine.wait()

    return pl.pallas_call(
        body, name=name, in_specs=[ANY], out_specs=[ANY, pl.BlockSpec(memory_space=pltpu.VMEM)],
        out_shape=[jax.ShapeDtypeStruct((N_DEV, r, c_dim), block.dtype), jax.ShapeDtypeStruct((8, 128), F32)],
        scratch_shapes=[pltpu.SemaphoreType.DMA((7,)), pltpu.SemaphoreType.DMA((7,)), pltpu.SemaphoreType.DMA],
    )(block)


HBM = pl.BlockSpec(memory_space=pltpu.HBM)
SEM = pl.BlockSpec(memory_space=pltpu.SEMAPHORE)
EFFECT = pltpu.SideEffectType.DATAFLOW_SIDE_EFFECTING


def _exchange_copies(scatter, src_refs, land_refs, send_sems, recv_sems, local_sems):
    x, y, c = lax.axis_index("x"), lax.axis_index("y"), lax.axis_index("c")
    me = 4 * x + 2 * y + c
    sends, arrivals, locals_ = [], [], []
    for a, (src, land) in enumerate(zip(src_refs, land_refs)):
        def pick(idx, src=src):
            return src.at[idx] if scatter else src

        locals_.append(pltpu.make_async_copy(pick(me), land.at[me], local_sems.at[a]))
        for r in range(1, N_DEV):
            px = 1 - x if r & 4 else x
            py = 1 - y if r & 2 else y
            pc = 1 - c if r & 1 else c
            peer, s = 4 * px + 2 * py + pc, 7 * a + r - 1
            sends.append(pltpu.make_async_remote_copy(
                src_ref=pick(peer), dst_ref=land.at[me], send_sem=send_sems.at[s], recv_sem=recv_sems.at[s],
                device_id=(px, py, pc), device_id_type=MESH))
            arrivals.append(pltpu.make_async_remote_copy(
                src_ref=pick(peer), dst_ref=land.at[peer], send_sem=send_sems.at[s], recv_sem=recv_sems.at[s],
                device_id=(px, py, pc), device_id_type=MESH))
    return sends, arrivals, locals_


def _exchange_start(srcs, scatter, name):
    n = len(srcs)
    lands = [lax.empty((N_DEV,) + s.shape[-2:], s.dtype) for s in srcs]

    def body(*refs):
        src_refs, land_refs = refs[:n], refs[n:2 * n]
        send_sems, recv_sems, local_sems = refs[2 * n:2 * n + 3]
        token = refs[-1]
        sends, _, locals_ = _exchange_copies(scatter, src_refs, land_refs, send_sems, recv_sems, local_sems)
        for cp in locals_ + sends:
            cp.start()
        token[...] = jnp.zeros_like(token)

    res = pl.pallas_call(
        body, name=name,
        out_shape=[pltpu.SemaphoreType.DMA((7 * n,)), pltpu.SemaphoreType.DMA((7 * n,)), pltpu.SemaphoreType.DMA((n,))]
        + [pltpu.HBM(a.shape, a.dtype) for a in list(srcs) + lands] + [jax.ShapeDtypeStruct((8, 128), F32)],
        in_specs=[HBM] * (2 * n), out_specs=[SEM] * 3 + [HBM] * (2 * n) + [pl.BlockSpec(memory_space=pltpu.VMEM)],
        input_output_aliases={i: 3 + i for i in range(2 * n)},
        compiler_params=pltpu.CompilerParams(has_side_effects=EFFECT),
    )(*[pltpu.with_memory_space_constraint(a, pltpu.HBM) for a in list(srcs) + lands])
    return (n, scatter, res[:3], res[3:3 + 2 * n]), res[-1]


def _exchange_wait(handle, after, name):
    n, scatter, sems, thru = handle

    def body(*refs):
        src_refs, land_refs = refs[:n], refs[n:2 * n]
        send_sems, recv_sems, local_sems = refs[2 * n:2 * n + 3]
        sends, arrivals, locals_ = _exchange_copies(scatter, src_refs, land_refs, send_sems, recv_sems, local_sems)
        for cp in arrivals:
            cp.wait_recv()
        for cp in sends:
            cp.wait_send()
        for cp in locals_:
            cp.wait()

    res = pl.pallas_call(
        body, name=name, out_shape=[pltpu.HBM(a.shape, a.dtype) for a in thru],
        in_specs=[HBM] * (2 * n) + [SEM] * 3 + [ANY], out_specs=[HBM] * (2 * n),
        input_output_aliases={i: i for i in range(2 * n)},
        compiler_params=pltpu.CompilerParams(has_side_effects=EFFECT),
    )(*thru, *sems, after)
    return res[n:]


def _behind(tokens, a):
    zero = sum(tok[0, 0] for tok in tokens)
    return jax.tree.map(lambda v: v + zero.astype(v.dtype), a)


def _perm_heads(a, perm, axis):
    idx = [slice(None)] * a.ndim
    parts = []
    for h in perm:
        idx[axis] = slice(64 * h, 64 * (h + 1))
        parts.append(a[tuple(idx)])
    idx[axis] = slice(512, None)
    if a.shape[axis] > 512:
        parts.append(a[tuple(idx)])
    return jnp.concatenate(parts, axis=axis)


Q_INV = tuple(int(i) for i in np.argsort(Q_PERM))


def _in0_to_kernel(a, axis):
    a = _perm_heads(a, Q_PERM, axis)
    idx = [slice(None)] * a.ndim

    def cut(lo, hi):
        idx[axis] = slice(lo, hi)
        return a[tuple(idx)]

    return jnp.concatenate([cut(0, 512), cut(768, 1792), cut(512, 768)], axis=axis)


def _in0_from_kernel(a, axis):
    idx = [slice(None)] * a.ndim

    def cut(lo, hi):
        idx[axis] = slice(lo, hi)
        return a[tuple(idx)]

    a = jnp.concatenate([cut(0, 512), cut(1536, 1792), cut(512, 1536)], axis=axis)
    return _perm_heads(a, Q_INV, axis)


def _f32_as_u16_rows(v, rows):
    bits = lax.bitcast_convert_type(v, jnp.uint16).reshape(-1)
    return jnp.pad(bits, (0, rows * D - bits.shape[0])).reshape(rows, D)


def _pad_rows(v, rows):
    v = v.reshape(-1)
    return jnp.pad(v, (0, rows * D - v.shape[0])).reshape(rows, D)


def kernel(x, mix_norm, a_w_in, a_b_in, a_sinks, a_conv_w, a_conv_b, a_cln_g, a_cln_b, a_w_out, c_w_in, c_w_pool, c_pool_scale, c_sln_g, c_sln_b, c_w_s, c_b_s, c_w_out, ffn_norm, ffn_w_gate, ffn_w_up, ffn_w_down, final_norm, loss_target, m_mix_norm, m_a_w_in, m_a_b_in, m_a_sinks, m_a_conv_w, m_a_conv_b, m_a_cln_g, m_a_cln_b, m_a_w_out, m_c_w_in, m_c_w_pool, m_c_pool_scale, m_c_sln_g, m_c_sln_b, m_c_w_s, m_c_b_s, m_c_w_out, m_ffn_norm, m_ffn_w_gate, m_ffn_w_up, m_ffn_w_down, m_final_norm, v_mix_norm, v_a_w_in, v_a_b_in, v_a_sinks, v_a_conv_w, v_a_conv_b, v_a_cln_g, v_a_cln_b, v_a_w_out, v_c_w_in, v_c_w_pool, v_c_pool_scale, v_c_sln_g, v_c_sln_b, v_c_w_s, v_c_b_s, v_c_w_out, v_ffn_norm, v_ffn_w_gate, v_ffn_w_up, v_ffn_w_down, v_final_norm):
    bsz, seq, _ = x.shape
    t = bsz * seq
    w_in = dict(mix_norm=mix_norm, a_w_in=a_w_in, a_b_in=a_b_in, a_sinks=a_sinks, a_conv_w=a_conv_w, a_conv_b=a_conv_b,
                a_cln_g=a_cln_g, a_cln_b=a_cln_b, a_w_out=a_w_out, c_w_in=c_w_in, c_w_pool=c_w_pool,
                c_pool_scale=c_pool_scale, c_sln_g=c_sln_g, c_sln_b=c_sln_b, c_w_s=c_w_s, c_b_s=c_b_s, c_w_out=c_w_out,
                ffn_norm=ffn_norm, ffn_w_gate=ffn_w_gate, ffn_w_up=ffn_w_up, ffn_w_down=ffn_w_down, final_norm=final_norm)
    m_in = dict(mix_norm=m_mix_norm, a_w_in=m_a_w_in, a_b_in=m_a_b_in, a_sinks=m_a_sinks, a_conv_w=m_a_conv_w,
                a_conv_b=m_a_conv_b, a_cln_g=m_a_cln_g, a_cln_b=m_a_cln_b, a_w_out=m_a_w_out, c_w_in=m_c_w_in,
                c_w_pool=m_c_w_pool, c_pool_scale=m_c_pool_scale, c_sln_g=m_c_sln_g, c_sln_b=m_c_sln_b, c_w_s=m_c_w_s,
                c_b_s=m_c_b_s, c_w_out=m_c_w_out, ffn_norm=m_ffn_norm, ffn_w_gate=m_ffn_w_gate, ffn_w_up=m_ffn_w_up,
                ffn_w_down=m_ffn_w_down, final_norm=m_final_norm)
    v_in = dict(mix_norm=v_mix_norm, a_w_in=v_a_w_in, a_b_in=v_a_b_in, a_sinks=v_a_sinks, a_conv_w=v_a_conv_w,
                a_conv_b=v_a_conv_b, a_cln_g=v_a_cln_g, a_cln_b=v_a_cln_b, a_w_out=v_a_w_out, c_w_in=v_c_w_in,
                c_w_pool=v_c_w_pool, c_pool_scale=v_c_pool_scale, c_sln_g=v_c_sln_g, c_sln_b=v_c_sln_b, c_w_s=v_c_w_s,
                c_b_s=v_c_b_s, c_w_out=v_c_w_out, ffn_norm=v_ffn_norm, ffn_w_gate=v_ffn_w_gate, ffn_w_up=v_ffn_w_up,
                ffn_w_down=v_ffn_w_down, final_norm=v_final_norm)

    small = jnp.concatenate([a_conv_w[0].reshape(-1), c_pool_scale[0], c_sln_g[0], c_sln_b[0]])
    first_bits = lax.bitcast_convert_type(jnp.concatenate([a_w_in[0].T, a_w_out[0]], axis=0).astype(BF16), jnp.uint16)
    gathered, tok = _all_gather(jnp.concatenate([first_bits, _f32_as_u16_rows(small, W_MISC_ROWS)], axis=0), "gather_mixer0")

    def ffn_shards(l):
        return [ffn_w_gate[l].T.astype(BF16), ffn_w_up[l].T.astype(BF16), ffn_w_down[l].astype(BF16)]

    ffn0_h, tok = _exchange_start(_behind([tok], ffn_shards(0)), False, "gather_ffn0_start")
    mix1_h, tok = _exchange_start(_behind([tok], [c_w_in[0].T.astype(BF16), c_w_out[0].astype(BF16)]), False,
                                  "gather_mixer1_start")
    ffn1_h, tok = _exchange_start(_behind([tok], ffn_shards(1)), False, "gather_ffn1_start")

    a_in_full = lax.bitcast_convert_type(gathered[:, :224].reshape(IN0, D), BF16)
    a_out_full = lax.bitcast_convert_type(gathered[:, 224:352].reshape(D, D), BF16)
    small_all = lax.bitcast_convert_type(
        gathered[:, 352:].reshape(N_DEV, -1)[:, :2 * SMALL_SHARD].reshape(N_DEV, SMALL_SHARD, 2), F32)
    conv_w = small_all[:, :31 * 64].reshape(N_DEV, 31, 64).transpose(1, 0, 2).reshape(31, 512)
    conv_w = jnp.pad(conv_w, ((0, HALO - CONV_K), (0, 0)))
    pool_scale = small_all[:, 31 * 64:31 * 64 + 64].reshape(1, 512)
    sln_g = small_all[:, 31 * 64 + 64:31 * 64 + 128].reshape(1, 512)
    sln_b = small_all[:, 31 * 64 + 128:].reshape(1, 512)

    wt_in0 = _in0_to_kernel(a_in_full, 0)
    b_in0 = _in0_to_kernel(a_b_in, 1)
    w_out0 = _perm_heads(a_out_full, Q_PERM, 0)
    b_rows = jnp.broadcast_to(c_b_s[0][:, :, None], (4, 128, 128))
    conv_b, cln_g, cln_b = a_conv_b, a_cln_g, a_cln_b

    h0 = x.reshape(t, D)
    target = loss_target.reshape(t, D)
    z0, hn0 = _norm_proj(h0, _behind([tok], mix_norm[0:1]), wt_in0, b_in0, "in_proj0")
    attn = _attn_fwd(z0, a_sinks, bsz, "attn_fwd")
    conv, conv_y = _conv_fwd(z0, conv_w, conv_b, cln_g, cln_b, bsz, "conv_fwd")
    h1 = _out_proj(attn, conv, w_out0, h0, "out_proj0")
    wtg0, wtu0, wd0 = (w.reshape(D_FF, D) for w in _exchange_wait(ffn0_h, h1, "gather_ffn0_wait"))
    h2, hnf0, gate0, up0 = _ffn_fwd(h1, ffn_norm[0:1], wtg0, wtu0, wd0, "ffn_fwd0")
    wt_in1, w_out1 = (w.reshape(-1, D) for w in _exchange_wait(mix1_h, h2, "gather_mixer1_wait"))
    z1, hn1 = _norm_proj(h2, mix_norm[1:2], wt_in1, None, "in_proj1")
    pool = _pool_fwd(z1, c_w_pool[0], pool_scale, bsz, "pool_fwd")
    sgu = _sgu_fwd(z1, sln_g, sln_b, c_w_s[0], b_rows, "sgu_fwd")
    h3 = _out_proj(pool, sgu, w_out1, h2, "out_proj1")
    wtg1, wtu1, wd1 = (w.reshape(D_FF, D) for w in _exchange_wait(ffn1_h, h3, "gather_ffn1_wait"))
    h4, hnf1, gate1, up1 = _ffn_fwd(h3, ffn_norm[1:2], wtg1, wtu1, wd1, "ffn_fwd1")

    def blocks(g):
        return g.reshape(N_DEV, g.shape[0] // N_DEV, D)

    dh4, d_final_norm, loss_part = _loss_head(h4, final_norm.reshape(1, D), target, "loss_head")
    loss = lax.psum(loss_part[0, 0], ("x", "y", "c"))
    dh3, dgate1, dup1, act1, d_fn1 = _ffn_bwd(dh4, h3, ffn_norm[1:2], gate1, up1, wtg1, wtu1, wd1, "ffn_bwd1")
    gw_ffn1 = [_mm_tn(dgate1, hnf1, "dw_gate1"), _mm_tn(dup1, hnf1, "dw_up1"), _mm_tn(act1, dh4, "dw_down1")]
    ffn1_g, tok = _exchange_start([blocks(g) for g in gw_ffn1], True, "scatter_ffn1_start")
    dmix1 = _dmix(dh3, _behind([tok], w_out1), "dmix1")
    gw_c_out = jnp.concatenate([_mm_tn(pool, dh3, "dw_out1_pool"), _mm_tn(sgu, dh3, "dw_out1_sgu")], axis=0)
    dzp, d_w_pool, d_pool_scale = _pool_bwd(z1, dmix1, c_w_pool[0], pool_scale, bsz, "pool_bwd")
    dzu, dzv, d_w_s, d_b_s, d_sln_g, d_sln_b = _sgu_bwd(z1, dmix1, sln_g, sln_b, c_w_s[0], b_rows, "sgu_bwd")
    dh2, d_mn1 = _proj_bwd_norm([(dzp, 0), (dzu, 512), (dzv, 1024)], wt_in1, h2, dh3, mix_norm[1:2], "in_proj1_bwd")
    gw_c_in = jnp.concatenate([_mm_tn(dzp, hn1, "dw_in1_pool"), _mm_tn(dzu, hn1, "dw_in1_u"),
                               _mm_tn(dzv, hn1, "dw_in1_v")], axis=0)
    mix1_g, tok = _exchange_start([blocks(gw_c_in), blocks(gw_c_out)], True, "scatter_mixer1_start")
    dh1, dgate0, dup0, act0, d_fn0 = _ffn_bwd(dh2, h1, _behind([tok], ffn_norm[0:1]), gate0, up0, wtg0, wtu0, wd0, "ffn_bwd0")
    gw_ffn0 = [_mm_tn(dgate0, hnf0, "dw_gate0"), _mm_tn(dup0, hnf0, "dw_up0"), _mm_tn(act0, dh2, "dw_down0")]
    ffn0_g, tok = _exchange_start([blocks(g) for g in gw_ffn0], True, "scatter_ffn0_start")
    dmix0 = _dmix(dh1, _behind([tok], w_out0), "dmix0")
    gw_a_out = _perm_heads(jnp.concatenate([_mm_tn(attn, dh1, "dw_out0_attn"), _mm_tn(conv, dh1, "dw_out0_conv")],
                                           axis=0), Q_INV, 0)
    dq, dkv, d_sink_row, d_bq, d_bkv = _attn_bwd(z0, dmix0, a_sinks, bsz, "attn_bwd")
    dca, dcg, d_conv_w, d_conv_b, d_cln_g, d_cln_b, d_ba, d_bg = _conv_bwd(z0, conv_y, dmix0, conv_w, cln_g, cln_b, bsz, "conv_bwd")
    dx, d_mn0 = _proj_bwd_norm([(dq, 0), (dca, 512), (dcg, 1024), (dkv, 1536)], wt_in0, h0, dh1, mix_norm[0:1], "in_proj0_bwd")
    gw_a_in = _in0_from_kernel(jnp.concatenate(
        [_mm_tn(dq, hn0, "dw_in0_q"), _mm_tn(dca, hn0, "dw_in0_a"), _mm_tn(dcg, hn0, "dw_in0_g"),
         _mm_tn(dkv, hn0, "dw_in0_kv")], axis=0), 0)
    d_b_in = _in0_from_kernel(jnp.concatenate([d_bq, d_ba, d_bg, d_bkv], axis=1), 1)

    rep = dict(mix_norm=jnp.concatenate([d_mn0, d_mn1], axis=0), a_b_in=d_b_in, a_sinks=d_sink_row[:, :8],
               a_conv_b=d_conv_b, a_cln_g=d_cln_g, a_cln_b=d_cln_b, c_w_pool=d_w_pool[None], c_w_s=d_w_s[None],
               c_b_s=d_b_s[None], ffn_norm=jnp.concatenate([d_fn0, d_fn1], axis=0), final_norm=d_final_norm.reshape(D))
    rep_flat = jnp.concatenate([rep[nm].reshape(-1) for nm in REP_NAMES])
    rep_flat = jnp.pad(rep_flat, (0, N_DEV * REP_ROWS * D - rep_flat.shape[0])).reshape(N_DEV, REP_ROWS, D)
    small_g = jnp.concatenate([
        d_conv_w[:CONV_K].reshape(31, N_DEV, 64).transpose(1, 0, 2).reshape(N_DEV, 31 * 64),
        d_pool_scale.reshape(N_DEV, 64), d_sln_g.reshape(N_DEV, 64), d_sln_b.reshape(N_DEV, 64)], axis=1)
    small_g = jnp.pad(small_g, ((0, 0), (0, G_SMALL_ROWS * D - SMALL_SHARD))).reshape(N_DEV, G_SMALL_ROWS, D)
    g_misc = jnp.concatenate([small_g, rep_flat], axis=1)
    mix0_g, tok = _exchange_start([blocks(gw_a_in), blocks(gw_a_out), g_misc], True, "scatter_mixer0_start")

    names = list(w_in)
    g_out, delta, new_m, new_v = {}, {}, {}, {}
    column_sharded = ("a_w_in", "c_w_in", "ffn_w_gate", "ffn_w_up")

    def rows_of(a, nm):
        return jnp.swapaxes(a, 1, 2) if nm in column_sharded else a

    def reduce_adamw(nm, landing, layer, into=None):
        args = [rows_of(d[nm], nm) for d in (w_in, m_in, v_in)]
        if args[0].shape[0] == 1:
            args, layer = [a[0] for a in args], None
        return _reduce_adamw(landing, *args, "adamw_%s_%s" % (nm, layer), layer=layer, into=into)

    def keep(nm, res):
        res = [r if r.ndim == 3 else r[None] for r in res]
        g_out[nm], delta[nm], new_m[nm], new_v[nm] = (rows_of(r, nm) for r in res)

    ffn_names = ("ffn_w_gate", "ffn_w_up", "ffn_w_down")
    landed = _exchange_wait(ffn1_g, tok, "scatter_ffn1_wait")
    ffn_res = [reduce_adamw(nm, a, 1) for nm, a in zip(ffn_names, landed)]
    landed = _exchange_wait(mix1_g, ffn_res[-1][0], "scatter_mixer1_wait")
    for nm, a in zip(("c_w_in", "c_w_out"), landed):
        keep(nm, reduce_adamw(nm, a, 0))
    landed = _exchange_wait(ffn0_g, g_out["c_w_out"], "scatter_ffn0_wait")
    for nm, a, res in zip(ffn_names, landed, ffn_res):
        keep(nm, reduce_adamw(nm, a, 0, into=res))
    landed = _exchange_wait(mix0_g, g_out["ffn_w_down"], "scatter_mixer0_wait")
    for nm, a in zip(("a_w_in", "a_w_out"), landed[:2]):
        keep(nm, reduce_adamw(nm, a, 0))
    g_tail = _sum_slabs(landed[2], "sum_tail")
    rep_all = _all_gather(g_tail[G_SMALL_ROWS:], "gather_replicated_grads")[0].reshape(-1)
    small_r = g_tail[:G_SMALL_ROWS].reshape(-1)[:SMALL_SHARD]
    g_out.update(
        a_conv_w=small_r[:31 * 64].reshape(1, 31, 64), c_pool_scale=small_r[31 * 64:31 * 64 + 64].reshape(1, 64),
        c_sln_g=small_r[31 * 64 + 64:31 * 64 + 128].reshape(1, 64), c_sln_b=small_r[31 * 64 + 128:].reshape(1, 64))
    off = 0
    for nm in REP_NAMES:
        n = int(np.prod(w_in[nm].shape))
        g_out[nm] = rep_all[off:off + n].reshape(w_in[nm].shape)
        off += n
    for group, rows, label in ((("a_conv_w", "c_pool_scale", "c_sln_g", "c_sln_b"), G_SMALL_ROWS, "adamw_small_sharded"),
                               (REP_NAMES, N_DEV * REP_ROWS, "adamw_replicated")):
        flat = [_pad_rows(jnp.concatenate([d[nm].reshape(-1) for nm in group]), rows) for d in (w_in, g_out, m_in, v_in)]
        res = [r.reshape(-1) for r in _adamw(*flat, label)]
        off = 0
        for nm in group:
            n = int(np.prod(w_in[nm].shape))
            delta[nm], new_m[nm], new_v[nm] = (r[off:off + n].reshape(w_in[nm].shape) for r in res)
            off += n

    grad_x = dx.reshape(bsz, seq, D)
    return (loss, grad_x, *[g_out[nm] for nm in names], *[delta[nm] for nm in names],
            *[new_m[nm] for nm in names], *[new_v[nm] for nm in names])
```

```python
import functools

import jax
import jax.numpy as jnp
import numpy as np
from jax import lax
from jax.experimental import pallas as pl
from jax.experimental.pallas import tpu as pltpu

F32 = jnp.float32
BF16 = jnp.bfloat16
MESH = pl.DeviceIdType.MESH

D = 1024
N_DEV = 8
EPS = 1e-5
HEAD_PAIRS = 4
ATT_BLK = 128
CONV_K = 31
HALO = 32
D_FF = 2816
FF_TILE_FWD = D_FF // 2
FF_TILE_BWD = D_FF // 2
IN0 = 1792
IN1 = 1536
POOL_WINDOWS = (2, 4, 8, 16)
SGU_CHUNK = 128
GELU_C = 0.7978845608028654
GELU_A = 0.044715
ADAM_LR, ADAM_B1, ADAM_B2, ADAM_EPS, ADAM_WD, ADAM_STEP = 0.001, 0.9, 0.999, 1e-08, 0.01, 10
VMEM_LIMIT = 56 << 20

SMALL_SHARD = 31 * 64 + 3 * 64
W_MISC_ROWS = 16
G_MISC_ROWS = 32
G_SMALL_ROWS = 8
REP_ROWS = G_MISC_ROWS - G_SMALL_ROWS
REP_NAMES = ("mix_norm", "a_b_in", "a_sinks", "a_conv_b", "a_cln_g", "a_cln_b", "c_w_pool", "c_w_s", "c_b_s",
             "ffn_norm", "final_norm")
Q_PERM = (0, 4, 1, 5, 2, 6, 3, 7)


def _params(*sem):
    return pltpu.CompilerParams(dimension_semantics=sem, vmem_limit_bytes=VMEM_LIMIT)


def _nn(a, b):
    return jnp.dot(a, b, preferred_element_type=F32)


def _nt(a, b):
    return lax.dot_general(a, b, (((1,), (1,)), ((), ())), preferred_element_type=F32)


def _tn(a, b):
    return lax.dot_general(a, b, (((0,), (0,)), ((), ())), preferred_element_type=F32)


def _tile(n, want=512):
    t = min(want, n)
    assert n % t == 0, (n, t)
    return t


def _seq_tile(s):
    return 512 if s >= 1024 else s // 2


def _rms(x, g):
    r = lax.rsqrt(jnp.mean(x * x, axis=-1, keepdims=True) + EPS)
    return x * r * g, r


def _rms_bwd(x, g, d_y):
    r = lax.rsqrt(jnp.mean(x * x, axis=-1, keepdims=True) + EPS)
    xr = x * r
    u = d_y * g
    d_x = r * (u - xr * jnp.mean(u * xr, axis=-1, keepdims=True))
    return d_x, jnp.sum(d_y * xr, axis=0, keepdims=True)


def _ln(y, g, b):
    mu = jnp.mean(y, axis=-1, keepdims=True)
    yc = y - mu
    rstd = lax.rsqrt(jnp.mean(yc * yc, axis=-1, keepdims=True) + EPS)
    xhat = yc * rstd
    return xhat * g + b, xhat, rstd


def _ln_bwd(d_o, xhat, rstd, g):
    dxh = d_o * g
    return rstd * (dxh - jnp.mean(dxh, axis=-1, keepdims=True) - xhat * jnp.mean(dxh * xhat, axis=-1, keepdims=True))


def _gelu(x):
    th = jnp.tanh(GELU_C * (x + GELU_A * x * x * x))
    return 0.5 * x * (1.0 + th), th


def _gelu_grad(x, th):
    return 0.5 * (1.0 + th) + 0.5 * x * (1.0 - th * th) * GELU_C * (1.0 + 3.0 * GELU_A * x * x)


def _row(c):
    return pl.BlockSpec((1, c), lambda *_: (0, 0))


def _full(shape):
    return pl.BlockSpec(shape, lambda *_: (0,) * len(shape))


def _norm_proj(h, g, wt, bias, name):
    t, n = h.shape[0], wt.shape[0]
    tm = _tile(t)
    has_bias = bias is not None

    def body(*refs):
        h_ref, g_ref, wt_ref = refs[:3]
        z_ref, hn_ref = refs[-2:]
        hn = _rms(h_ref[...], g_ref[...])[0].astype(BF16)
        hn_ref[...] = hn
        z = _nt(hn, wt_ref[...])
        if has_bias:
            z = z + refs[3][...]
        z_ref[...] = z.astype(BF16)

    in_specs = [pl.BlockSpec((tm, D), lambda i: (i, 0)), _row(D), _full((n, D))]
    args = [h, g, wt]
    if has_bias:
        in_specs.append(_row(n))
        args.append(bias)
    return pl.pallas_call(
        body, name=name, grid=(t // tm,), in_specs=in_specs,
        out_specs=[pl.BlockSpec((tm, n), lambda i: (i, 0)), pl.BlockSpec((tm, D), lambda i: (i, 0))],
        out_shape=[jax.ShapeDtypeStruct((t, n), BF16), jax.ShapeDtypeStruct((t, D), BF16)],
        compiler_params=_params("parallel"))(*args)


def _out_proj(a, b, w, h, name):
    t = h.shape[0]
    tm = _tile(t)

    def body(a_ref, b_ref, wa_ref, wb_ref, h_ref, o_ref):
        o_ref[...] = h_ref[...] + _nn(a_ref[...], wa_ref[...]) + _nn(b_ref[...], wb_ref[...])

    half = pl.BlockSpec((tm, 512), lambda i: (i, 0))
    return pl.pallas_call(
        body, name=name, grid=(t // tm,),
        in_specs=[half, half, pl.BlockSpec((512, D), lambda i: (0, 0)), pl.BlockSpec((512, D), lambda i: (1, 0)),
                  pl.BlockSpec((tm, D), lambda i: (i, 0))],
        out_specs=pl.BlockSpec((tm, D), lambda i: (i, 0)), out_shape=jax.ShapeDtypeStruct((t, D), F32),
        compiler_params=_params("parallel"))(a, b, w, w, h)


def _dmix(dh, w, name):
    t = dh.shape[0]
    tm = _tile(t)

    def body(dh_ref, w_ref, o_ref):
        o_ref[...] = _nt(dh_ref[...].astype(BF16), w_ref[...]).astype(BF16)

    return pl.pallas_call(
        body, name=name, grid=(t // tm,), in_specs=[pl.BlockSpec((tm, D), lambda i: (i, 0)), _full((D, D))],
        out_specs=pl.BlockSpec((tm, D), lambda i: (i, 0)), out_shape=jax.ShapeDtypeStruct((t, D), BF16),
        compiler_params=_params("parallel"))(dh, w)


def _ffn_fwd(h, g, wtg, wtu, wd, name):
    t = h.shape[0]
    tm, tf = _tile(t), FF_TILE_FWD
    nf = D_FF // tf

    def body(h_ref, g_ref, wtg_ref, wtu_ref, wd_ref, o_ref, hn_ref, gate_ref, up_ref):
        @pl.when(pl.program_id(1) == 0)
        def _():
            x = h_ref[...]
            hn_ref[...] = _rms(x, g_ref[...])[0].astype(BF16)
            o_ref[...] = x

        hn = hn_ref[...]
        gate = _nt(hn, wtg_ref[...])
        up = _nt(hn, wtu_ref[...])
        gate_ref[...] = gate.astype(BF16)
        up_ref[...] = up.astype(BF16)
        act = (gate * jax.nn.sigmoid(gate) * up).astype(BF16)
        o_ref[...] += _nn(act, wd_ref[...])

    tok = pl.BlockSpec((tm, D), lambda i, f: (i, 0))
    wsp = pl.BlockSpec((tf, D), lambda i, f: (f, 0))
    mid = pl.BlockSpec((tm, tf), lambda i, f: (i, f))
    return pl.pallas_call(
        body, name=name, grid=(t // tm, nf), in_specs=[tok, _row(D), wsp, wsp, wsp],
        out_specs=[tok, tok, mid, mid],
        out_shape=[jax.ShapeDtypeStruct((t, D), F32), jax.ShapeDtypeStruct((t, D), BF16),
                   jax.ShapeDtypeStruct((t, D_FF), BF16), jax.ShapeDtypeStruct((t, D_FF), BF16)],
        compiler_params=_params("parallel", "arbitrary"))(h, g, wtg, wtu, wd)


def _ffn_bwd(dh, h, g, gate, up, wtg, wtu, wd, name):
    t = h.shape[0]
    tm, tf = _tile(t), FF_TILE_BWD
    nf = D_FF // tf

    def body(dh_ref, h_ref, g_ref, gate_ref, up_ref, wtg_ref, wtu_ref, wd_ref,
             dhin_ref, dgate_ref, dup_ref, act_ref, dg_ref, dhb, dact):
        i, f = pl.program_id(0), pl.program_id(1)

        @pl.when(f == 0)
        def _():
            dhb[...] = dh_ref[...].astype(BF16)
            dhin_ref[...] = jnp.zeros_like(dhin_ref)

        @pl.when((i == 0) & (f == 0))
        def _():
            dg_ref[...] = jnp.zeros_like(dg_ref)

        dact[...] = _nt(dhb[...], wd_ref[...])
        for c0 in range(0, tf, 128):
            cols = slice(c0, c0 + 128)
            gt = gate_ref[:, cols].astype(F32)
            u = up_ref[:, cols].astype(F32)
            da = dact[:, cols]
            sg = jax.nn.sigmoid(gt)
            sil = gt * sg
            act_ref[:, cols] = (sil * u).astype(BF16)
            dup_ref[:, cols] = (da * sil).astype(BF16)
            dgate_ref[:, cols] = (da * u * sg * (1.0 + gt * (1.0 - sg))).astype(BF16)
        dhin_ref[...] += _nn(dgate_ref[...], wtg_ref[...]) + _nn(dup_ref[...], wtu_ref[...])

        @pl.when(f == nf - 1)
        def _():
            d_x, d_g = _rms_bwd(h_ref[...], g_ref[...], dhin_ref[...])
            dhin_ref[...] = dh_ref[...] + d_x
            dg_ref[...] += d_g

    tok = pl.BlockSpec((tm, D), lambda i, f: (i, 0))
    wsp = pl.BlockSpec((tf, D), lambda i, f: (f, 0))
    mid = pl.BlockSpec((tm, tf), lambda i, f: (i, f))
    mid_shape = jax.ShapeDtypeStruct((t, D_FF), BF16)
    return pl.pallas_call(
        body, name=name, grid=(t // tm, nf), in_specs=[tok, tok, _row(D), mid, mid, wsp, wsp, wsp],
        out_specs=[tok, mid, mid, mid, _row(D)],
        out_shape=[jax.ShapeDtypeStruct((t, D), F32), mid_shape, mid_shape, mid_shape, jax.ShapeDtypeStruct((1, D), F32)],
        scratch_shapes=[pltpu.VMEM((tm, D), BF16), pltpu.VMEM((tm, tf), F32)],
        compiler_params=_params("arbitrary", "arbitrary"))(dh, h, g, gate, up, wtg, wtu, wd)


def _proj_bwd_norm(pieces, wt, h, dh, g, name):
    t = h.shape[0]
    tm = _tile(t)
    n_p = len(pieces)

    def body(*refs):
        p_refs, w_refs = refs[:n_p], refs[n_p:2 * n_p]
        h_ref, dh_ref, g_ref, o_ref, dg_ref = refs[2 * n_p:]

        @pl.when(pl.program_id(0) == 0)
        def _():
            dg_ref[...] = jnp.zeros_like(dg_ref)

        d_hn = _nn(p_refs[0][...], w_refs[0][...])
        for p_ref, w_ref in zip(p_refs[1:], w_refs[1:]):
            d_hn = d_hn + _nn(p_ref[...], w_ref[...])
        d_x, d_g = _rms_bwd(h_ref[...], g_ref[...], d_hn)
        o_ref[...] = dh_ref[...] + d_x
        dg_ref[...] += d_g

    tok = pl.BlockSpec((tm, D), lambda i: (i, 0))
    in_specs = [pl.BlockSpec((tm, a.shape[1]), lambda i: (i, 0)) for a, _ in pieces]
    for a, off in pieces:
        w = a.shape[1]
        assert off % w == 0
        in_specs.append(pl.BlockSpec((w, D), functools.partial(lambda i, blk: (blk, 0), blk=off // w)))
    in_specs += [tok, tok, _row(D)]
    return pl.pallas_call(
        body, name=name, grid=(t // tm,), in_specs=in_specs, out_specs=[tok, _row(D)],
        out_shape=[jax.ShapeDtypeStruct((t, D), F32), jax.ShapeDtypeStruct((1, D), F32)],
        compiler_params=_params("arbitrary"))(*[a for a, _ in pieces], *([wt] * n_p), h, dh, g)


def _mm_tn(a, b, name):
    t, n = a.shape
    k = b.shape[1]
    tn = n if n <= 1024 else n // 2
    tt = _tile(t, 1024)
    nt = t // tt

    def body(a_ref, b_ref, o_ref, acc):
        s = pl.program_id(1)

        @pl.when(s == 0)
        def _():
            acc[...] = jnp.zeros_like(acc)

        acc[...] += _tn(a_ref[...], b_ref[...].astype(BF16))

        @pl.when(s == nt - 1)
        def _():
            o_ref[...] = acc[...].astype(BF16)

    return pl.pallas_call(
        body, name=name, grid=(n // tn, nt),
        in_specs=[pl.BlockSpec((tt, tn), lambda j, s: (s, j)), pl.BlockSpec((tt, k), lambda j, s: (s, 0))],
        out_specs=pl.BlockSpec((tn, k), lambda j, s: (j, 0)), out_shape=jax.ShapeDtypeStruct((n, k), BF16),
        scratch_shapes=[pltpu.VMEM((tn, k), F32)],
        compiler_params=_params("parallel", "arbitrary"))(a, b)


def _loss_head(h, g, target, name):
    t = h.shape[0]
    tm = _tile(t)

    def body(h_ref, g_ref, t_ref, dh_ref, dg_ref, loss_ref):
        @pl.when(pl.program_id(0) == 0)
        def _():
            dg_ref[...] = jnp.zeros_like(dg_ref)
            loss_ref[...] = jnp.zeros_like(loss_ref)

        x = h_ref[...]
        gv = g_ref[...]
        err = _rms(x, gv)[0] - t_ref[...]
        per_tok = jnp.mean(err * err, axis=-1, keepdims=True)
        loss_ref[...] += 0.5 * jnp.sum(per_tok, axis=0, keepdims=True)
        d_x, d_g = _rms_bwd(x, gv, err * (1.0 / D))
        dh_ref[...] = d_x
        dg_ref[...] += d_g

    tok = pl.BlockSpec((tm, D), lambda i: (i, 0))
    return pl.pallas_call(
        body, name=name, grid=(t // tm,), in_specs=[tok, _row(D), tok],
        out_specs=[tok, _row(D), _row(1)],
        out_shape=[jax.ShapeDtypeStruct((t, D), F32), jax.ShapeDtypeStruct((1, D), F32), jax.ShapeDtypeStruct((1, 1), F32)],
        compiler_params=_params("arbitrary"))(h, g, target)


STACK = HEAD_PAIRS * ATT_BLK


def _attn_valid(first, rows):
    qi = lax.broadcasted_iota(jnp.int32, (rows, 2 * ATT_BLK), 0) % ATT_BLK
    r = lax.broadcasted_iota(jnp.int32, (rows, 2 * ATT_BLK), 1)
    dist = qi + ATT_BLK - r
    return (dist >= 0) & (dist < ATT_BLK) & ((r >= ATT_BLK) | jnp.logical_not(first))


def _stacked(ref, kh, scale):
    lo = lax.broadcasted_iota(jnp.int32, (ATT_BLK, 128), 1) < 64
    keep = lo if kh == 0 else ~lo
    parts = [jnp.where(keep, ref[:, g * 128:(g + 1) * 128] * scale, 0.0).astype(BF16) for g in range(HEAD_PAIRS)]
    return jnp.concatenate(parts, axis=0)


def _unstacked(a0, a1, g):
    lo = lax.broadcasted_iota(jnp.int32, (ATT_BLK, 128), 1) < 64
    rows = slice(g * ATT_BLK, (g + 1) * ATT_BLK)
    return jnp.where(lo, a0[rows], a1[rows])


def _sink_rows(s_ref, kh):
    return jnp.concatenate([jnp.full((ATT_BLK, 128), s_ref[0, kh * 4 + g], F32) for g in range(HEAD_PAIRS)], axis=0)


def _row_sums(a):
    hi = a.astype(BF16)
    lo = (a - hi.astype(F32)).astype(BF16)
    ones = jnp.ones((2 * ATT_BLK, 128), BF16)
    return _nn(hi, ones) + _nn(lo, ones)


def _both(a):
    return jnp.concatenate([a, a], axis=1)


def _attn_probs(qs, kpair, sink, valid):
    s = jnp.where(valid, _nt(qs, kpair), -1e30)
    m = jnp.maximum(jnp.broadcast_to(jnp.max(s, axis=-1, keepdims=True), (s.shape[0], 128)), sink)
    p = jnp.exp(s - _both(m))
    es = jnp.exp(sink - m)
    inv = 1.0 / (_row_sums(p) + es)
    return p * _both(inv), es * inv


def _attn_probs_head(qm, kpair, sink, valid):
    s = jnp.where(valid, _nt(qm, kpair), -1e30)
    m = jnp.maximum(jnp.max(s, axis=-1, keepdims=True), sink)
    p = jnp.exp(s - m)
    return p * (1.0 / (jnp.sum(p, axis=-1, keepdims=True) + jnp.exp(sink - m)))


def _attn_specs(nb, order):
    q = pl.BlockSpec((ATT_BLK, 512), lambda b, j: (b * nb + order(j), 0))
    kvc = pl.BlockSpec((ATT_BLK, 256), lambda b, j: (b * nb + order(j), 6))
    kvp = pl.BlockSpec((ATT_BLK, 256), lambda b, j: (jnp.maximum(b * nb + order(j) - 1, 0), 6))
    return q, kvc, kvp


def _window_kv(kvc_ref, kvp_ref):
    kvc, kvp = kvc_ref[...], kvp_ref[...]
    kpair = jnp.concatenate([kvp[:, :128], kvc[:, :128]], axis=0)
    vpair = jnp.concatenate([kvp[:, 128:], kvc[:, 128:]], axis=0)
    return kpair, vpair


def _attn_fwd(z0, sinks, bsz, name):
    t = z0.shape[0]
    nb = t // bsz // ATT_BLK

    def body(s_ref, q_ref, kvc_ref, kvp_ref, o_ref):
        valid = _attn_valid(pl.program_id(1) == 0, ATT_BLK)
        kpair, vpair = _window_kv(kvc_ref, kvp_ref)
        lo = lax.broadcasted_iota(jnp.int32, (ATT_BLK, 128), 1) < 64
        for g in range(HEAD_PAIRS):
            qs = q_ref[:, g * 128:(g + 1) * 128] * 0.125
            outs = []
            for kh in range(2):
                qm = jnp.where(lo if kh == 0 else ~lo, qs, 0.0).astype(BF16)
                p = _attn_probs_head(qm, kpair, s_ref[0, kh * 4 + g], valid)
                outs.append(_nn(p.astype(BF16), vpair))
            o_ref[:, g * 128:(g + 1) * 128] = jnp.where(lo, outs[0], outs[1]).astype(BF16)

    q, kvc, kvp = _attn_specs(nb, lambda j: j)
    return pl.pallas_call(
        body, name=name, grid=(bsz, nb),
        in_specs=[pl.BlockSpec(memory_space=pltpu.SMEM), q, kvc, kvp],
        out_specs=pl.BlockSpec((ATT_BLK, 512), lambda b, j: (b * nb + j, 0)),
        out_shape=jax.ShapeDtypeStruct((t, 512), BF16),
        compiler_params=_params("parallel", "parallel"))(sinks, z0, z0, z0)


def _attn_bwd(z0, dmix, sinks, bsz, name):
    t = z0.shape[0]
    nb = t // bsz // ATT_BLK

    def body(s_ref, q_ref, kvc_ref, kvp_ref, do_ref, dq_ref, dkv_ref, dsink_ref, dbq_ref, dbkv_ref, carry):
        b, j = pl.program_id(0), pl.program_id(1)

        @pl.when(j == 0)
        def _():
            carry[...] = jnp.zeros_like(carry)

        @pl.when((b == 0) & (j == 0))
        def _():
            dsink_ref[...] = jnp.zeros_like(dsink_ref)
            dbq_ref[...] = jnp.zeros_like(dbq_ref)
            dbkv_ref[...] = jnp.zeros_like(dbkv_ref)

        valid = _attn_valid(j == nb - 1, STACK)
        kpair, vpair = _window_kv(kvc_ref, kvp_ref)
        lane = lax.broadcasted_iota(jnp.int32, (1, 128), 1)
        dk = jnp.zeros((2 * ATT_BLK, 128), F32)
        dv = jnp.zeros((2 * ATT_BLK, 128), F32)
        dsink = jnp.zeros((1, 128), F32)
        dqs = []
        for kh in range(2):
            qs = _stacked(q_ref, kh, 0.125)
            dos = _stacked(do_ref, kh, 1.0)
            p, ps = _attn_probs(qs, kpair, _sink_rows(s_ref, kh), valid)
            dp = _nt(dos, vpair)
            delta = _row_sums(p * dp)
            ds = (p * (dp - _both(delta))).astype(BF16)
            dqs.append(_nn(ds, kpair))
            dk = dk + _tn(ds, qs)
            dv = dv + _tn(p.astype(BF16), dos)
            psd = ps * delta
            for g in range(HEAD_PAIRS):
                part = jnp.sum(psd[g * ATT_BLK:(g + 1) * ATT_BLK], axis=0, keepdims=True)
                dsink = dsink - jnp.where(lane == kh * 4 + g, part, 0.0)
        for g in range(HEAD_PAIRS):
            dq = _unstacked(dqs[0], dqs[1], g) * 0.125
            dq_ref[:, g * 128:(g + 1) * 128] = dq.astype(BF16)
            dbq_ref[:, g * 128:(g + 1) * 128] += jnp.sum(dq, axis=0, keepdims=True)
        dkv = jnp.concatenate([dk[ATT_BLK:], dv[ATT_BLK:]], axis=1) + carry[...]
        dkv_ref[...] = dkv.astype(BF16)
        dbkv_ref[...] += jnp.sum(dkv, axis=0, keepdims=True)
        carry[...] = jnp.concatenate([dk[:ATT_BLK], dv[:ATT_BLK]], axis=1)
        dsink_ref[...] += dsink

    q, kvc, kvp = _attn_specs(nb, lambda j: nb - 1 - j)
    return pl.pallas_call(
        body, name=name, grid=(bsz, nb),
        in_specs=[pl.BlockSpec(memory_space=pltpu.SMEM), q, kvc, kvp,
                  pl.BlockSpec((ATT_BLK, 512), lambda b, j: (b * nb + nb - 1 - j, 0))],
        out_specs=[pl.BlockSpec((ATT_BLK, 512), lambda b, j: (b * nb + nb - 1 - j, 0)),
                   pl.BlockSpec((ATT_BLK, 256), lambda b, j: (b * nb + nb - 1 - j, 0)), _row(128), _row(512), _row(256)],
        out_shape=[jax.ShapeDtypeStruct((t, 512), BF16), jax.ShapeDtypeStruct((t, 256), BF16),
                   jax.ShapeDtypeStruct((1, 128), F32), jax.ShapeDtypeStruct((1, 512), F32),
                   jax.ShapeDtypeStruct((1, 256), F32)],
        scratch_shapes=[pltpu.VMEM((ATT_BLK, 256), F32)],
        compiler_params=_params("arbitrary", "arbitrary"))(sinks, z0, z0, z0, dmix)


def _seq_specs(ts, nt, t, width, col):
    per = ts // HALO
    cur = pl.BlockSpec((ts, width), lambda b, i: (b * nt + i, col))
    prev = pl.BlockSpec((HALO, width), lambda b, i: (jnp.maximum((b * nt + i) * per - 1, 0), col))
    nxt = pl.BlockSpec((HALO, width), lambda b, i: (jnp.minimum((b * nt + i + 1) * per, t // HALO - 1), col))
    return prev, cur, nxt


SUB = 8
CONV_ROWS = 64


def _shifted_copies(src, sh, rows_first, rows_rest):
    for r in range(SUB):
        rows = rows_first if r == 0 else rows_rest
        sh[r, pl.ds(0, rows), :] = src[pl.ds(r, rows), :]


def _tap_sum(sh, w, offset, c0, rows):
    acc = None
    for k in range(CONV_K):
        o = offset(k)
        term = sh[o % SUB, pl.ds(c0 + o - o % SUB, rows), :] * w[k:k + 1, :]
        acc = term if acc is None else acc + term
    return acc


def _glu_rows(a_ref, g_ref, rows=slice(None)):
    return a_ref[rows, :].astype(F32) * jax.nn.sigmoid(g_ref[rows, :].astype(F32))


def _conv_fwd(z0, conv_w, conv_b, ln_g, ln_b, bsz, name):
    t = z0.shape[0]
    s = t // bsz
    ts = _seq_tile(s)
    nt = s // ts
    first = HALO - (CONV_K - 1)

    def body(ap_ref, ac_ref, gp_ref, gc_ref, w_ref, cb_ref, lg_ref, lb_ref, o_ref, y_ref, hbuf, sh):
        hbuf[0:HALO, :] = jnp.where(pl.program_id(1) > 0, _glu_rows(ap_ref, gp_ref), 0.0)
        hbuf[HALO:HALO + ts, :] = _glu_rows(ac_ref, gc_ref)
        _shifted_copies(hbuf, sh, ts + HALO, ts + HALO - SUB)
        w, cb, lg, lb = w_ref[...], cb_ref[...], lg_ref[...], lb_ref[...]
        for c0 in range(0, ts, CONV_ROWS):
            y = _tap_sum(sh, w, lambda k: first + k, c0, CONV_ROWS) + cb
            y_ref[c0:c0 + CONV_ROWS, :] = y
            o = _ln(y, lg, lb)[0]
            o_ref[c0:c0 + CONV_ROWS, :] = (o * jax.nn.sigmoid(o)).astype(BF16)

    ap, ac, _ = _seq_specs(ts, nt, t, 512, 1)
    gp, gc, _ = _seq_specs(ts, nt, t, 512, 2)
    tile = pl.BlockSpec((ts, 512), lambda b, i: (b * nt + i, 0))
    return pl.pallas_call(
        body, name=name, grid=(bsz, nt),
        in_specs=[ap, ac, gp, gc, _full((HALO, 512)), _row(512), _row(512), _row(512)],
        out_specs=[tile, tile],
        out_shape=[jax.ShapeDtypeStruct((t, 512), BF16), jax.ShapeDtypeStruct((t, 512), F32)],
        scratch_shapes=[pltpu.VMEM((HALO + ts, 512), F32), pltpu.VMEM((SUB, HALO + ts, 512), F32)],
        compiler_params=_params("parallel", "parallel"))(z0, z0, z0, z0, conv_w, conv_b, ln_g, ln_b)


def _conv_bwd(z0, y, dmix, conv_w, ln_g, ln_b, bsz, name):
    t = z0.shape[0]
    s = t // bsz
    ts = _seq_tile(s)
    nt = s // ts
    first = HALO - (CONV_K - 1)

    def body(ap_ref, ac_ref, gp_ref, gc_ref, yc_ref, yn_ref, dc_ref, dn_ref, w_ref, lg_ref, lb_ref,
             da_ref, dg_ref, dw_ref, dcb_ref, dlg_ref, dlb_ref, dba_ref, dbg_ref, hbuf, dybuf, sh_h, sh_dy):
        b, i = pl.program_id(0), pl.program_id(1)

        @pl.when((b == 0) & (i == 0))
        def _():
            for ref in (dw_ref, dcb_ref, dlg_ref, dlb_ref, dba_ref, dbg_ref):
                ref[...] = jnp.zeros_like(ref)

        w, lg, lb = w_ref[...], lg_ref[...], lb_ref[...]
        hbuf[0:HALO, :] = jnp.where(i > 0, _glu_rows(ap_ref, gp_ref), 0.0)
        hbuf[HALO:HALO + ts, :] = _glu_rows(ac_ref, gc_ref)
        _shifted_copies(hbuf, sh_h, ts + HALO, ts + HALO - SUB)

        def d_conv_out(yv, dout):
            o, xhat, rstd = _ln(yv, lg, lb)
            sg_o = jax.nn.sigmoid(o)
            d_o = dout * sg_o * (1.0 + o * (1.0 - sg_o))
            return _ln_bwd(d_o, xhat, rstd, lg), d_o * xhat, d_o

        dlg = jnp.zeros((1, 512), F32)
        dlb = jnp.zeros((1, 512), F32)
        dcb = jnp.zeros((1, 512), F32)
        for c0 in range(0, ts, CONV_ROWS):
            rows = slice(c0, c0 + CONV_ROWS)
            dy, g_part, b_part = d_conv_out(yc_ref[rows, :], dc_ref[rows, :].astype(F32))
            dybuf[rows, :] = dy
            dlg = dlg + jnp.sum(g_part, axis=0, keepdims=True)
            dlb = dlb + jnp.sum(b_part, axis=0, keepdims=True)
            dcb = dcb + jnp.sum(dy, axis=0, keepdims=True)
        dn = jnp.where(i < nt - 1, dn_ref[...].astype(F32), 0.0)
        dybuf[ts:ts + HALO, :] = d_conv_out(yn_ref[...], dn)[0]
        dlg_ref[...] += dlg
        dlb_ref[...] += dlb
        dcb_ref[...] += dcb
        _shifted_copies(dybuf, sh_dy, ts + HALO - SUB, ts + HALO - SUB)

        for k in range(CONV_K):
            o = first + k
            prod = dybuf[0:ts, :] * sh_h[o % SUB, pl.ds(o - o % SUB, ts), :]
            dw_ref[pl.ds(k, 1), :] += jnp.sum(prod, axis=0, keepdims=True)
        dba = jnp.zeros((1, 512), F32)
        dbg = jnp.zeros((1, 512), F32)
        for c0 in range(0, ts, CONV_ROWS):
            rows = slice(c0, c0 + CONV_ROWS)
            dh = _tap_sum(sh_dy, w, lambda k: CONV_K - 1 - k, c0, CONV_ROWS)
            a_c = ac_ref[rows, :].astype(F32)
            sg_c = jax.nn.sigmoid(gc_ref[rows, :].astype(F32))
            d_a = dh * sg_c
            d_g = dh * a_c * sg_c * (1.0 - sg_c)
            da_ref[rows, :] = d_a.astype(BF16)
            dg_ref[rows, :] = d_g.astype(BF16)
            dba = dba + jnp.sum(d_a, axis=0, keepdims=True)
            dbg = dbg + jnp.sum(d_g, axis=0, keepdims=True)
        dba_ref[...] += dba
        dbg_ref[...] += dbg

    ap, ac, _ = _seq_specs(ts, nt, t, 512, 1)
    gp, gc, _ = _seq_specs(ts, nt, t, 512, 2)
    _, yc, yn = _seq_specs(ts, nt, t, 512, 0)
    _, dc, dn = _seq_specs(ts, nt, t, 512, 1)
    tile = pl.BlockSpec((ts, 512), lambda b, i: (b * nt + i, 0))
    vec = jax.ShapeDtypeStruct((1, 512), F32)
    return pl.pallas_call(
        body, name=name, grid=(bsz, nt),
        in_specs=[ap, ac, gp, gc, yc, yn, dc, dn, _full((HALO, 512)), _row(512), _row(512)],
        out_specs=[tile, tile, _full((HALO, 512)), _row(512), _row(512), _row(512), _row(512), _row(512)],
        out_shape=[jax.ShapeDtypeStruct((t, 512), BF16), jax.ShapeDtypeStruct((t, 512), BF16),
                   jax.ShapeDtypeStruct((HALO, 512), F32), vec, vec, vec, vec, vec],
        scratch_shapes=[pltpu.VMEM((HALO + ts, 512), F32), pltpu.VMEM((ts + HALO, 512), F32),
                        pltpu.VMEM((SUB, HALO + ts, 512), F32), pltpu.VMEM((SUB, HALO + ts, 512), F32)],
        compiler_params=_params("arbitrary", "arbitrary"))(z0, z0, z0, z0, y, y, dmix, dmix, conv_w, ln_g, ln_b)


def _pooled(pbuf, g, ts, tok):
    w = 2 << g
    cols = slice(128 * g, 128 * (g + 1))
    sm = pbuf[pl.ds(HALO, ts), cols]
    for d in range(1, w):
        sm = sm + pbuf[pl.ds(HALO - d, ts), cols]
    cnt = jnp.minimum(tok + 1, w).astype(F32)
    return sm / cnt - pbuf[pl.ds(HALO, ts), cols]


def _pool_fwd(z1, w_pool, scale, bsz, name):
    t = z1.shape[0]
    s = t // bsz
    ts = _seq_tile(s)
    nt = s // ts

    def body(zp_ref, zc_ref, wp_ref, sc_ref, o_ref, pbuf):
        i = pl.program_id(1)
        pbuf[0:HALO, :] = jnp.where(i > 0, zp_ref[...].astype(F32), 0.0)
        pbuf[HALO:HALO + ts, :] = zc_ref[...].astype(F32)
        tok = i * ts + lax.broadcasted_iota(jnp.int32, (ts, 1), 0)
        for g in range(4):
            cols = slice(128 * g, 128 * (g + 1))
            pooled = _pooled(pbuf, g, ts, tok).astype(BF16)
            o_ref[:, cols] = (_nn(pooled, wp_ref[g].astype(BF16)) * sc_ref[:, cols]).astype(BF16)

    zp, zc, _ = _seq_specs(ts, nt, t, 512, 0)
    return pl.pallas_call(
        body, name=name, grid=(bsz, nt), in_specs=[zp, zc, _full((4, 128, 128)), _row(512)],
        out_specs=pl.BlockSpec((ts, 512), lambda b, i: (b * nt + i, 0)),
        out_shape=jax.ShapeDtypeStruct((t, 512), BF16),
        scratch_shapes=[pltpu.VMEM((HALO + ts, 512), F32)],
        compiler_params=_params("parallel", "parallel"))(z1, z1, w_pool, scale)


def _pool_bwd(z1, dmix, w_pool, scale, bsz, name):
    t = z1.shape[0]
    s = t // bsz
    ts = _seq_tile(s)
    nt = s // ts
    rr = ts + HALO

    def body(zp_ref, zc_ref, dc_ref, dn_ref, wp_ref, sc_ref, dz_ref, dwp_ref, dsc_ref, pbuf, ebuf):
        b, i = pl.program_id(0), pl.program_id(1)

        @pl.when((b == 0) & (i == 0))
        def _():
            dwp_ref[...] = jnp.zeros_like(dwp_ref)
            dsc_ref[...] = jnp.zeros_like(dsc_ref)

        pbuf[0:HALO, :] = jnp.where(i > 0, zp_ref[...].astype(F32), 0.0)
        pbuf[HALO:HALO + ts, :] = zc_ref[...].astype(F32)
        dn = jnp.where(i < nt - 1, dn_ref[...].astype(F32), 0.0)
        dout = jnp.concatenate([dc_ref[...].astype(F32), dn], axis=0)
        tok = i * ts + lax.broadcasted_iota(jnp.int32, (ts, 1), 0)
        tok_r = i * ts + lax.broadcasted_iota(jnp.int32, (rr, 1), 0)
        for g in range(4):
            w = 2 << g
            cols = slice(128 * g, 128 * (g + 1))
            wg = wp_ref[g].astype(BF16)
            pooled = _pooled(pbuf, g, ts, tok).astype(BF16)
            dsc_ref[:, cols] += jnp.sum(dout[:ts, cols] * _nn(pooled, wg), axis=0, keepdims=True)
            dy = (dout[:, cols] * sc_ref[:, cols]).astype(BF16)
            dwp_ref[g] += _tn(pooled, dy[:ts])
            dpl = _nt(dy, wg)
            ebuf[...] = dpl / jnp.minimum(tok_r + 1, w).astype(F32)
            dz = ebuf[pl.ds(0, ts), :] - dpl[:ts]
            for d in range(1, w):
                dz = dz + ebuf[pl.ds(d, ts), :]
            dz_ref[:, cols] = dz.astype(BF16)

    zp, zc, _ = _seq_specs(ts, nt, t, 512, 0)
    _, dc, dn = _seq_specs(ts, nt, t, 512, 0)
    return pl.pallas_call(
        body, name=name, grid=(bsz, nt), in_specs=[zp, zc, dc, dn, _full((4, 128, 128)), _row(512)],
        out_specs=[pl.BlockSpec((ts, 512), lambda b, i: (b * nt + i, 0)), _full((4, 128, 128)), _row(512)],
        out_shape=[jax.ShapeDtypeStruct((t, 512), BF16), jax.ShapeDtypeStruct((4, 128, 128), F32),
                   jax.ShapeDtypeStruct((1, 512), F32)],
        scratch_shapes=[pltpu.VMEM((HALO + ts, 512), F32), pltpu.VMEM((rr, 128), F32)],
        compiler_params=_params("arbitrary", "arbitrary"))(z1, z1, dmix, dmix, w_pool, scale)


def _tril():
    r = lax.broadcasted_iota(jnp.int32, (SGU_CHUNK, SGU_CHUNK), 0)
    c = lax.broadcasted_iota(jnp.int32, (SGU_CHUNK, SGU_CHUNK), 1)
    return r >= c


def _sgu_fwd(z1, ln_g, ln_b, w_s, b_rows, name):
    t = z1.shape[0]
    ts = _tile(t)

    def body(zu_ref, zv_ref, lg_ref, lb_ref, ws_ref, bs_ref, o_ref):
        v = _gelu(zv_ref[...].astype(F32))[0]
        vb = _ln(v, lg_ref[...], lb_ref[...])[0].astype(BF16)
        tril = _tril()
        for g in range(4):
            cols = slice(128 * g, 128 * (g + 1))
            wg = jnp.where(tril, ws_ref[g], 0.0).astype(BF16)
            for c in range(ts // SGU_CHUNK):
                rows = slice(SGU_CHUNK * c, SGU_CHUNK * (c + 1))
                mixed = _nn(wg, vb[rows, cols]) + bs_ref[g]
                o_ref[rows, cols] = (_gelu(zu_ref[rows, cols].astype(F32))[0] * mixed).astype(BF16)

    return pl.pallas_call(
        body, name=name, grid=(t // ts,),
        in_specs=[pl.BlockSpec((ts, 512), lambda i: (i, 1)), pl.BlockSpec((ts, 512), lambda i: (i, 2)),
                  _row(512), _row(512), _full((4, 128, 128)), _full((4, 128, 128))],
        out_specs=pl.BlockSpec((ts, 512), lambda i: (i, 0)), out_shape=jax.ShapeDtypeStruct((t, 512), BF16),
        compiler_params=_params("parallel"))(z1, z1, ln_g, ln_b, w_s, b_rows)


def _sgu_bwd(z1, dmix, ln_g, ln_b, w_s, b_rows, name):
    t = z1.shape[0]
    ts = _tile(t)

    def body(zu_ref, zv_ref, d_ref, lg_ref, lb_ref, ws_ref, bs_ref,
             dzu_ref, dzv_ref, dws_ref, dbs_ref, dlg_ref, dlb_ref, dvbuf):
        @pl.when(pl.program_id(0) == 0)
        def _():
            for ref in (dws_ref, dbs_ref, dlg_ref, dlb_ref):
                ref[...] = jnp.zeros_like(ref)

        zv = zv_ref[...].astype(F32)
        v, thv = _gelu(zv)
        lg = lg_ref[...]
        vln, xhat, rstd = _ln(v, lg, lb_ref[...])
        vb = vln.astype(BF16)
        tril = _tril()
        for g in range(4):
            cols = slice(128 * g, 128 * (g + 1))
            wg = jnp.where(tril, ws_ref[g], 0.0).astype(BF16)
            dws = jnp.zeros((SGU_CHUNK, SGU_CHUNK), F32)
            dbs = jnp.zeros((1, SGU_CHUNK), F32)
            for c in range(ts // SGU_CHUNK):
                rows = slice(SGU_CHUNK * c, SGU_CHUNK * (c + 1))
                vbc = vb[rows, cols]
                mixed = _nn(wg, vbc) + bs_ref[g]
                zu = zu_ref[rows, cols].astype(F32)
                u, thu = _gelu(zu)
                dout = d_ref[rows, cols].astype(F32)
                dzu_ref[rows, cols] = (dout * mixed * _gelu_grad(zu, thu)).astype(BF16)
                dm = dout * u
                dmb = dm.astype(BF16)
                dws = dws + _nt(dmb, vbc)
                dbs = dbs + jnp.sum(dm.T, axis=0, keepdims=True)
                dvbuf[rows, cols] = _tn(wg, dmb)
            dws_ref[g] += jnp.where(tril, dws, 0.0)
            dbs_ref[pl.ds(g, 1), :] += dbs
        dvln = dvbuf[...]
        dlg_ref[...] += jnp.sum(dvln * xhat, axis=0, keepdims=True)
        dlb_ref[...] += jnp.sum(dvln, axis=0, keepdims=True)
        dzv_ref[...] = (_ln_bwd(dvln, xhat, rstd, lg) * _gelu_grad(zv, thv)).astype(BF16)

    tile = pl.BlockSpec((ts, 512), lambda i: (i, 0))
    vec = jax.ShapeDtypeStruct((1, 512), F32)
    return pl.pallas_call(
        body, name=name, grid=(t // ts,),
        in_specs=[pl.BlockSpec((ts, 512), lambda i: (i, 1)), pl.BlockSpec((ts, 512), lambda i: (i, 2)),
                  pl.BlockSpec((ts, 512), lambda i: (i, 1)), _row(512), _row(512), _full((4, 128, 128)),
                  _full((4, 128, 128))],
        out_specs=[tile, tile, _full((4, 128, 128)), _full((4, 128)), _row(512), _row(512)],
        out_shape=[jax.ShapeDtypeStruct((t, 512), BF16), jax.ShapeDtypeStruct((t, 512), BF16),
                   jax.ShapeDtypeStruct((4, 128, 128), F32), jax.ShapeDtypeStruct((4, 128), F32), vec, vec],
        scratch_shapes=[pltpu.VMEM((ts, 512), F32)],
        compiler_params=_params("arbitrary"))(z1, z1, dmix, ln_g, ln_b, w_s, b_rows)


def _row_tile(r):
    for cand in (512, 352, 256, 192, 128, 64, 32, 16, 8):
        if r % cand == 0:
            return cand
    return r


def _sum_slabs(a, name):
    k, r, c = a.shape
    tr = _row_tile(r)

    def body(*refs):
        acc = refs[0][...].astype(F32)
        for ref in refs[1:-1]:
            acc = acc + ref[...].astype(F32)
        refs[-1][...] = acc

    in_specs = [pl.BlockSpec((None, tr, c), functools.partial(lambda i, s: (s, i, 0), s=s)) for s in range(k)]
    return pl.pallas_call(
        body, name=name, grid=(r // tr,), in_specs=in_specs, out_specs=pl.BlockSpec((tr, c), lambda i: (i, 0)),
        out_shape=jax.ShapeDtypeStruct((r, c), F32), compiler_params=_params("parallel"))(*([a] * k))


def _adamw_math(w, g, m, v):
    mn = ADAM_B1 * m + (1.0 - ADAM_B1) * g
    vn = ADAM_B2 * v + (1.0 - ADAM_B2) * (g * g)
    m_hat = mn / (1.0 - ADAM_B1 ** ADAM_STEP)
    v_hat = vn / (1.0 - ADAM_B2 ** ADAM_STEP)
    return -ADAM_LR * (m_hat / (jnp.sqrt(v_hat) + ADAM_EPS) + ADAM_WD * w), mn, vn


def _reduce_adamw(landing, w, m, v, name, layer=None, into=None):
    k, r, c = landing.shape
    tr = _row_tile(r)
    n_into = 0 if into is None else 4

    def body(*refs):
        slabs, (w_ref, m_ref, v_ref) = refs[:k], refs[k:k + 3]
        g_ref, d_ref, mo_ref, vo_ref = refs[k + 3 + n_into:]
        g = slabs[0][...].astype(F32)
        for ref in slabs[1:]:
            g = g + ref[...].astype(F32)
        g_ref[...] = g
        d_ref[...], mo_ref[...], vo_ref[...] = _adamw_math(w_ref[...], g, m_ref[...], v_ref[...])

    if layer is None:
        spec = pl.BlockSpec((tr, c), lambda i: (i, 0))
    else:
        spec = pl.BlockSpec((None, tr, c), lambda i: (layer, i, 0))
    in_specs = [pl.BlockSpec((None, tr, c), functools.partial(lambda i, s: (s, i, 0), s=s)) for s in range(k)]
    in_specs += [spec] * 3 + [ANY] * n_into
    shape = jax.ShapeDtypeStruct(w.shape, F32)
    return pl.pallas_call(
        body, name=name, grid=(r // tr,), in_specs=in_specs, out_specs=[spec] * 4, out_shape=[shape] * 4,
        input_output_aliases={k + 3 + j: j for j in range(n_into)},
        compiler_params=_params("parallel"))(*([landing] * k), w, m, v, *(into or ()))


def _adamw(w, g, m, v, name):
    r, c = w.shape
    tr = _row_tile(r)

    def body(w_ref, g_ref, m_ref, v_ref, d_ref, mo_ref, vo_ref):
        d_ref[...], mo_ref[...], vo_ref[...] = _adamw_math(w_ref[...], g_ref[...], m_ref[...], v_ref[...])

    spec = pl.BlockSpec((tr, c), lambda i: (i, 0))
    shape = jax.ShapeDtypeStruct((r, c), F32)
    return pl.pallas_call(
        body, name=name, grid=(r // tr,), in_specs=[spec] * 4, out_specs=[spec] * 3, out_shape=[shape] * 3,
        compiler_params=_params("parallel"))(w, g, m, v)


ANY = pl.BlockSpec(memory_space=pl.ANY)


def _all_gather(block, name):
    r, c_dim = block.shape

    def body(x_ref, out_ref, token, send_sems, recv_sems, local_sem):
        token[...] = jnp.zeros_like(token)
        x, y, c = lax.axis_index("x"), lax.axis_index("y"), lax.axis_index("c")
        me, sibling = (x, y, c), (x, y, 1 - c)
        chips = [(1 - x, y), (x, 1 - y), (1 - x, 1 - y)]

        def rows(px, py, pc):
            return out_ref.at[4 * px + 2 * py + pc]

        def copy(k, blk, to, src=None):
            return pltpu.make_async_remote_copy(
                src_ref=rows(*blk) if src is None else src, dst_ref=rows(*blk), send_sem=send_sems.at[k],
                recv_sem=recv_sems.at[k], device_id=to, device_id_type=MESH)

        mine = pltpu.make_async_copy(x_ref, rows(*me), local_sem)
        mine.start()
        first = [copy(0, me, sibling, src=x_ref)]
        first += [copy(1 + j, me, (*chip, c), src=x_ref) for j, chip in enumerate(chips)]
        for cp in first:
            cp.start()
        passed = [copy(4 + j, (*chip, c), sibling) for j, chip in enumerate(chips)]
        for j, chip in enumerate(chips):
            copy(1 + j, (*chip, c), me).wait_recv()
            passed[j].start()
        copy(0, sibling, me).wait_recv()
        for j, chip in enumerate(chips):
            copy(4 + j, (*chip, 1 - c), me).wait_recv()
        for cp in first + passed:
            cp.wait_send()
        mine.wait()

    return pl.pallas_call(
        body, name=name, in_specs=[ANY], out_specs=[ANY, pl.BlockSpec(memory_space=pltpu.VMEM)],
        out_shape=[jax.ShapeDtypeStruct((N_DEV, r, c_dim), block.dtype), jax.ShapeDtypeStruct((8, 128), F32)],
        scratch_shapes=[pltpu.SemaphoreType.DMA((7,)), pltpu.SemaphoreType.DMA((7,)), pltpu.SemaphoreType.DMA],
    )(block)


HBM = pl.BlockSpec(memory_space=pltpu.HBM)
SEM = pl.BlockSpec(memory_space=pltpu.SEMAPHORE)
EFFECT = pltpu.SideEffectType.DATAFLOW_SIDE_EFFECTING


def _exchange_copies(scatter, src_refs, land_refs, send_sems, recv_sems, local_sems):
    x, y, c = lax.axis_index("x"), lax.axis_index("y"), lax.axis_index("c")
    me = 4 * x + 2 * y + c
    sends, arrivals, locals_ = [], [], []
    for a, (src, land) in enumerate(zip(src_refs, land_refs)):
        def pick(idx, src=src):
            return src.at[idx] if scatter else src

        locals_.append(pltpu.make_async_copy(pick(me), land.at[me], local_sems.at[a]))
        for r in range(1, N_DEV):
            px = 1 - x if r & 4 else x
            py = 1 - y if r & 2 else y
            pc = 1 - c if r & 1 else c
            peer, s = 4 * px + 2 * py + pc, 7 * a + r - 1
            sends.append(pltpu.make_async_remote_copy(
                src_ref=pick(peer), dst_ref=land.at[me], send_sem=send_sems.at[s], recv_sem=recv_sems.at[s],
                device_id=(px, py, pc), device_id_type=MESH))
            arrivals.append(pltpu.make_async_remote_copy(
                src_ref=pick(peer), dst_ref=land.at[peer], send_sem=send_sems.at[s], recv_sem=recv_sems.at[s],
                device_id=(px, py, pc), device_id_type=MESH))
    return sends, arrivals, locals_


def _exchange_start(srcs, scatter, name):
    n = len(srcs)
    lands = [lax.empty((N_DEV,) + s.shape[-2:], s.dtype) for s in srcs]

    def body(*refs):
        src_refs, land_refs = refs[:n], refs[n:2 * n]
        send_sems, recv_sems, local_sems = refs[2 * n:2 * n + 3]
        token = refs[-1]
        sends, _, locals_ = _exchange_copies(scatter, src_refs, land_refs, send_sems, recv_sems, local_sems)
        for cp in locals_ + sends:
            cp.start()
        token[...] = jnp.zeros_like(token)

    res = pl.pallas_call(
        body, name=name,
        out_shape=[pltpu.SemaphoreType.DMA((7 * n,)), pltpu.SemaphoreType.DMA((7 * n,)), pltpu.SemaphoreType.DMA((n,))]
        + [pltpu.HBM(a.shape, a.dtype) for a in list(srcs) + lands] + [jax.ShapeDtypeStruct((8, 128), F32)],
        in_specs=[HBM] * (2 * n), out_specs=[SEM] * 3 + [HBM] * (2 * n) + [pl.BlockSpec(memory_space=pltpu.VMEM)],
        input_output_aliases={i: 3 + i for i in range(2 * n)},
        compiler_params=pltpu.CompilerParams(has_side_effects=EFFECT),
    )(*[pltpu.with_memory_space_constraint(a, pltpu.HBM) for a in list(srcs) + lands])
    return (n, scatter, res[:3], res[3:3 + 2 * n]), res[-1]


def _exchange_wait(handle, after, name):
    n, scatter, sems, thru = handle

    def body(*refs):
        src_refs, land_refs = refs[:n], refs[n:2 * n]
        send_sems, recv_sems, local_sems = refs[2 * n:2 * n + 3]
        sends, arrivals, locals_ = _exchange_copies(scatter, src_refs, land_refs, send_sems, recv_sems, local_sems)
        for cp in arrivals:
            cp.wait_recv()
        for cp in sends:
            cp.wait_send()
        for cp in locals_:
            cp.wait()

    res = pl.pallas_call(
        body, name=name, out_shape=[pltpu.HBM(a.shape, a.dtype) for a in thru],
        in_specs=[HBM] * (2 * n) + [SEM] * 3 + [ANY], out_specs=[HBM] * (2 * n),
        input_output_aliases={i: i for i in range(2 * n)},
        compiler_params=pltpu.CompilerParams(has_side_effects=EFFECT),
    )(*thru, *sems, after)
    return res[n:]


def _behind(tokens, a):
    zero = sum(tok[0, 0] for tok in tokens)
    return jax.tree.map(lambda v: v + zero.astype(v.dtype), a)


def _perm_heads(a, perm, axis):
    idx = [slice(None)] * a.ndim
    parts = []
    for h in perm:
        idx[axis] = slice(64 * h, 64 * (h + 1))
        parts.append(a[tuple(idx)])
    idx[axis] = slice(512, None)
    if a.shape[axis] > 512:
        parts.append(a[tuple(idx)])
    return jnp.concatenate(parts, axis=axis)


Q_INV = tuple(int(i) for i in np.argsort(Q_PERM))


def _in0_to_kernel(a, axis):
    a = _perm_heads(a, Q_PERM, axis)
    idx = [slice(None)] * a.ndim

    def cut(lo, hi):
        idx[axis] = slice(lo, hi)
        return a[tuple(idx)]

    return jnp.concatenate([cut(0, 512), cut(768, 1792), cut(512, 768)], axis=axis)


def _in0_from_kernel(a, axis):
    idx = [slice(None)] * a.ndim

    def cut(lo, hi):
        idx[axis] = slice(lo, hi)
        return a[tuple(idx)]

    a = jnp.concatenate([cut(0, 512), cut(1536, 1792), cut(512, 1536)], axis=axis)
    return _perm_heads(a, Q_INV, axis)


def _f32_as_u16_rows(v, rows):
    bits = lax.bitcast_convert_type(v, jnp.uint16).reshape(-1)
    return jnp.pad(bits, (0, rows * D - bits.shape[0])).reshape(rows, D)


def _pad_rows(v, rows):
    v = v.reshape(-1)
    return jnp.pad(v, (0, rows * D - v.shape[0])).reshape(rows, D)


def kernel(x, mix_norm, a_w_in, a_b_in, a_sinks, a_conv_w, a_conv_b, a_cln_g, a_cln_b, a_w_out, c_w_in, c_w_pool, c_pool_scale, c_sln_g, c_sln_b, c_w_s, c_b_s, c_w_out, ffn_norm, ffn_w_gate, ffn_w_up, ffn_w_down, final_norm, loss_target, m_mix_norm, m_a_w_in, m_a_b_in, m_a_sinks, m_a_conv_w, m_a_conv_b, m_a_cln_g, m_a_cln_b, m_a_w_out, m_c_w_in, m_c_w_pool, m_c_pool_scale, m_c_sln_g, m_c_sln_b, m_c_w_s, m_c_b_s, m_c_w_out, m_ffn_norm, m_ffn_w_gate, m_ffn_w_up, m_ffn_w_down, m_final_norm, v_mix_norm, v_a_w_in, v_a_b_in, v_a_sinks, v_a_conv_w, v_a_conv_b, v_a_cln_g, v_a_cln_b, v_a_w_out, v_c_w_in, v_c_w_pool, v_c_pool_scale, v_c_sln_g, v_c_sln_b, v_c_w_s, v_c_b_s, v_c_w_out, v_ffn_norm, v_ffn_w_gate, v_ffn_w_up, v_ffn_w_down, v_final_norm):
    bsz, seq, _ = x.shape
    t = bsz * seq
    w_in = dict(mix_norm=mix_norm, a_w_in=a_w_in, a_b_in=a_b_in, a_sinks=a_sinks, a_conv_w=a_conv_w, a_conv_b=a_conv_b,
                a_cln_g=a_cln_g, a_cln_b=a_cln_b, a_w_out=a_w_out, c_w_in=c_w_in, c_w_pool=c_w_pool,
                c_pool_scale=c_pool_scale, c_sln_g=c_sln_g, c_sln_b=c_sln_b, c_w_s=c_w_s, c_b_s=c_b_s, c_w_out=c_w_out,
                ffn_norm=ffn_norm, ffn_w_gate=ffn_w_gate, ffn_w_up=ffn_w_up, ffn_w_down=ffn_w_down, final_norm=final_norm)
    m_in = dict(mix_norm=m_mix_norm, a_w_in=m_a_w_in, a_b_in=m_a_b_in, a_sinks=m_a_sinks, a_conv_w=m_a_conv_w,
                a_conv_b=m_a_conv_b, a_cln_g=m_a_cln_g, a_cln_b=m_a_cln_b, a_w_out=m_a_w_out, c_w_in=m_c_w_in,
                c_w_pool=m_c_w_pool, c_pool_scale=m_c_pool_scale, c_sln_g=m_c_sln_g, c_sln_b=m_c_sln_b, c_w_s=m_c_w_s,
                c_b_s=m_c_b_s, c_w_out=m_c_w_out, ffn_norm=m_ffn_norm, ffn_w_gate=m_ffn_w_gate, ffn_w_up=m_ffn_w_up,
                ffn_w_down=m_ffn_w_down, final_norm=m_final_norm)
    v_in = dict(mix_norm=v_mix_norm, a_w_in=v_a_w_in, a_b_in=v_a_b_in, a_sinks=v_a_sinks, a_conv_w=v_a_conv_w,
                a_conv_b=v_a_conv_b, a_cln_g=v_a_cln_g, a_cln_b=v_a_cln_b, a_w_out=v_a_w_out, c_w_in=v_c_w_in,
                c_w_pool=v_c_w_pool, c_pool_scale=v_c_pool_scale, c_sln_g=v_c_sln_g, c_sln_b=v_c_sln_b, c_w_s=v_c_w_s,
                c_b_s=v_c_b_s, c_w_out=v_c_w_out, ffn_norm=v_ffn_norm, ffn_w_gate=v_ffn_w_gate, ffn_w_up=v_ffn_w_up,
                ffn_w_down=v_ffn_w_down, final_norm=v_final_norm)

    small = jnp.concatenate([a_conv_w[0].reshape(-1), c_pool_scale[0], c_sln_g[0], c_sln_b[0]])
    first_bits = lax.bitcast_convert_type(jnp.concatenate([a_w_in[0].T, a_w_out[0]], axis=0).astype(BF16), jnp.uint16)
    gathered, tok = _all_gather(jnp.concatenate([first_bits, _f32_as_u16_rows(small, W_MISC_ROWS)], axis=0), "gather_mixer0")

    def ffn_shards(l):
        return [ffn_w_gate[l].T.astype(BF16), ffn_w_up[l].T.astype(BF16), ffn_w_down[l].astype(BF16)]

    ffn0_h, tok = _exchange_start(_behind([tok], ffn_shards(0)), False, "gather_ffn0_start")
    mix1_h, tok = _exchange_start(_behind([tok], [c_w_in[0].T.astype(BF16), c_w_out[0].astype(BF16)]), False,
                                  "gather_mixer1_start")
    ffn1_h, tok = _exchange_start(_behind([tok], ffn_shards(1)), False, "gather_ffn1_start")

    a_in_full = lax.bitcast_convert_type(gathered[:, :224].reshape(IN0, D), BF16)
    a_out_full = lax.bitcast_convert_type(gathered[:, 224:352].reshape(D, D), BF16)
    small_all = lax.bitcast_convert_type(
        gathered[:, 352:].reshape(N_DEV, -1)[:, :2 * SMALL_SHARD].reshape(N_DEV, SMALL_SHARD, 2), F32)
    conv_w = small_all[:, :31 * 64].reshape(N_DEV, 31, 64).transpose(1, 0, 2).reshape(31, 512)
    conv_w = jnp.pad(conv_w, ((0, HALO - CONV_K), (0, 0)))
    pool_scale = small_all[:, 31 * 64:31 * 64 + 64].reshape(1, 512)
    sln_g = small_all[:, 31 * 64 + 64:31 * 64 + 128].reshape(1, 512)
    sln_b = small_all[:, 31 * 64 + 128:].reshape(1, 512)

    wt_in0 = _in0_to_kernel(a_in_full, 0)
    b_in0 = _in0_to_kernel(a_b_in, 1)
    w_out0 = _perm_heads(a_out_full, Q_PERM, 0)
    b_rows = jnp.broadcast_to(c_b_s[0][:, :, None], (4, 128, 128))
    conv_b, cln_g, cln_b = a_conv_b, a_cln_g, a_cln_b

    h0 = x.reshape(t, D)
    target = loss_target.reshape(t, D)
    z0, hn0 = _norm_proj(h0, _behind([tok], mix_norm[0:1]), wt_in0, b_in0, "in_proj0")
    attn = _attn_fwd(z0, a_sinks, bsz, "attn_fwd")
    conv, conv_y = _conv_fwd(z0, conv_w, conv_b, cln_g, cln_b, bsz, "conv_fwd")
    h1 = _out_proj(attn, conv, w_out0, h0, "out_proj0")
    wtg0, wtu0, wd0 = (w.reshape(D_FF, D) for w in _exchange_wait(ffn0_h, h1, "gather_ffn0_wait"))
    h2, hnf0, gate0, up0 = _ffn_fwd(h1, ffn_norm[0:1], wtg0, wtu0, wd0, "ffn_fwd0")
    wt_in1, w_out1 = (w.reshape(-1, D) for w in _exchange_wait(mix1_h, h2, "gather_mixer1_wait"))
    z1, hn1 = _norm_proj(h2, mix_norm[1:2], wt_in1, None, "in_proj1")
    pool = _pool_fwd(z1, c_w_pool[0], pool_scale, bsz, "pool_fwd")
    sgu = _sgu_fwd(z1, sln_g, sln_b, c_w_s[0], b_rows, "sgu_fwd")
    h3 = _out_proj(pool, sgu, w_out1, h2, "out_proj1")
    wtg1, wtu1, wd1 = (w.reshape(D_FF, D) for w in _exchange_wait(ffn1_h, h3, "gather_ffn1_wait"))
    h4, hnf1, gate1, up1 = _ffn_fwd(h3, ffn_norm[1:2], wtg1, wtu1, wd1, "ffn_fwd1")

    def blocks(g):
        return g.reshape(N_DEV, g.shape[0] // N_DEV, D)

    dh4, d_final_norm, loss_part = _loss_head(h4, final_norm.reshape(1, D), target, "loss_head")
    dh3, dgate1, dup1, act1, d_fn1 = _ffn_bwd(dh4, h3, ffn_norm[1:2], gate1, up1, wtg1, wtu1, wd1, "ffn_bwd1")
    gw_ffn1 = [_mm_tn(dgate1, hnf1, "dw_gate1"), _mm_tn(dup1, hnf1, "dw_up1"), _mm_tn(act1, dh4, "dw_down1")]
    ffn1_g, tok = _exchange_start([blocks(g) for g in gw_ffn1], True, "scatter_ffn1_start")
    dmix1 = _dmix(dh3, _behind([tok], w_out1), "dmix1")
    gw_c_out = jnp.concatenate([_mm_tn(pool, dh3, "dw_out1_pool"), _mm_tn(sgu, dh3, "dw_out1_sgu")], axis=0)
    dzp, d_w_pool, d_pool_scale = _pool_bwd(z1, dmix1, c_w_pool[0], pool_scale, bsz, "pool_bwd")
    dzu, dzv, d_w_s, d_b_s, d_sln_g, d_sln_b = _sgu_bwd(z1, dmix1, sln_g, sln_b, c_w_s[0], b_rows, "sgu_bwd")
    dh2, d_mn1 = _proj_bwd_norm([(dzp, 0), (dzu, 512), (dzv, 1024)], wt_in1, h2, dh3, mix_norm[1:2], "in_proj1_bwd")
    gw_c_in = jnp.concatenate([_mm_tn(dzp, hn1, "dw_in1_pool"), _mm_tn(dzu, hn1, "dw_in1_u"),
                               _mm_tn(dzv, hn1, "dw_in1_v")], axis=0)
    mix1_g, tok = _exchange_start([blocks(gw_c_in), blocks(gw_c_out)], True, "scatter_mixer1_start")
    dh1, dgate0, dup0, act0, d_fn0 = _ffn_bwd(dh2, h1, _behind([tok], ffn_norm[0:1]), gate0, up0, wtg0, wtu0, wd0, "ffn_bwd0")
    gw_ffn0 = [_mm_tn(dgate0, hnf0, "dw_gate0"), _mm_tn(dup0, hnf0, "dw_up0"), _mm_tn(act0, dh2, "dw_down0")]
    ffn0_g, tok = _exchange_start([blocks(g) for g in gw_ffn0], True, "scatter_ffn0_start")
    dmix0 = _dmix(dh1, _behind([tok], w_out0), "dmix0")
    gw_a_out = _perm_heads(jnp.concatenate([_mm_tn(attn, dh1, "dw_out0_attn"), _mm_tn(conv, dh1, "dw_out0_conv")],
                                           axis=0), Q_INV, 0)
    dq, dkv, d_sink_row, d_bq, d_bkv = _attn_bwd(z0, dmix0, a_sinks, bsz, "attn_bwd")
    dca, dcg, d_conv_w, d_conv_b, d_cln_g, d_cln_b, d_ba, d_bg = _conv_bwd(z0, conv_y, dmix0, conv_w, cln_g, cln_b, bsz, "conv_bwd")
    dx, d_mn0 = _proj_bwd_norm([(dq, 0), (dca, 512), (dcg, 1024), (dkv, 1536)], wt_in0, h0, dh1, mix_norm[0:1], "in_proj0_bwd")
    gw_a_in = _in0_from_kernel(jnp.concatenate(
        [_mm_tn(dq, hn0, "dw_in0_q"), _mm_tn(dca, hn0, "dw_in0_a"), _mm_tn(dcg, hn0, "dw_in0_g"),
         _mm_tn(dkv, hn0, "dw_in0_kv")], axis=0), 0)
    d_b_in = _in0_from_kernel(jnp.concatenate([d_bq, d_ba, d_bg, d_bkv], axis=1), 1)

    rep = dict(mix_norm=jnp.concatenate([d_mn0, d_mn1], axis=0), a_b_in=d_b_in, a_sinks=d_sink_row[:, :8],
               a_conv_b=d_conv_b, a_cln_g=d_cln_g, a_cln_b=d_cln_b, c_w_pool=d_w_pool[None], c_w_s=d_w_s[None],
               c_b_s=d_b_s[None], ffn_norm=jnp.concatenate([d_fn0, d_fn1], axis=0), final_norm=d_final_norm.reshape(D))
    rep_flat = jnp.concatenate([rep[nm].reshape(-1) for nm in REP_NAMES] + [loss_part.reshape(1)])
    rep_flat = jnp.pad(rep_flat, (0, N_DEV * REP_ROWS * D - rep_flat.shape[0])).reshape(N_DEV, REP_ROWS, D)
    small_g = jnp.concatenate([
        d_conv_w[:CONV_K].reshape(31, N_DEV, 64).transpose(1, 0, 2).reshape(N_DEV, 31 * 64),
        d_pool_scale.reshape(N_DEV, 64), d_sln_g.reshape(N_DEV, 64), d_sln_b.reshape(N_DEV, 64)], axis=1)
    small_g = jnp.pad(small_g, ((0, 0), (0, G_SMALL_ROWS * D - SMALL_SHARD))).reshape(N_DEV, G_SMALL_ROWS, D)
    g_misc = jnp.concatenate([small_g, rep_flat], axis=1)
    mix0_g, tok = _exchange_start([blocks(gw_a_in), blocks(gw_a_out), g_misc], True, "scatter_mixer0_start")

    names = list(w_in)
    g_out, delta, new_m, new_v = {}, {}, {}, {}
    column_sharded = ("a_w_in", "c_w_in", "ffn_w_gate", "ffn_w_up")

    def rows_of(a, nm):
        return jnp.swapaxes(a, 1, 2) if nm in column_sharded else a

    def reduce_adamw(nm, landing, layer, into=None):
        args = [rows_of(d[nm], nm) for d in (w_in, m_in, v_in)]
        if args[0].shape[0] == 1:
            args, layer = [a[0] for a in args], None
        return _reduce_adamw(landing, *args, "adamw_%s_%s" % (nm, layer), layer=layer, into=into)

    def keep(nm, res):
        res = [r if r.ndim == 3 else r[None] for r in res]
        g_out[nm], delta[nm], new_m[nm], new_v[nm] = (rows_of(r, nm) for r in res)

    ffn_names = ("ffn_w_gate", "ffn_w_up", "ffn_w_down")
    landed = _exchange_wait(ffn1_g, tok, "scatter_ffn1_wait")
    ffn_res = [reduce_adamw(nm, a, 1) for nm, a in zip(ffn_names, landed)]
    landed = _exchange_wait(mix1_g, ffn_res[-1][0], "scatter_mixer1_wait")
    for nm, a in zip(("c_w_in", "c_w_out"), landed):
        keep(nm, reduce_adamw(nm, a, 0))
    landed = _exchange_wait(ffn0_g, g_out["c_w_out"], "scatter_ffn0_wait")
    for nm, a, res in zip(ffn_names, landed, ffn_res):
        keep(nm, reduce_adamw(nm, a, 0, into=res))
    landed = _exchange_wait(mix0_g, g_out["ffn_w_down"], "scatter_mixer0_wait")
    for nm, a in zip(("a_w_in", "a_w_out"), landed[:2]):
        keep(nm, reduce_adamw(nm, a, 0))
    g_tail = _sum_slabs(landed[2], "sum_tail")
    rep_all = _all_gather(g_tail[G_SMALL_ROWS:], "gather_replicated_grads")[0].reshape(-1)
    small_r = g_tail[:G_SMALL_ROWS].reshape(-1)[:SMALL_SHARD]
    g_out.update(
        a_conv_w=small_r[:31 * 64].reshape(1, 31, 64), c_pool_scale=small_r[31 * 64:31 * 64 + 64].reshape(1, 64),
        c_sln_g=small_r[31 * 64 + 64:31 * 64 + 128].reshape(1, 64), c_sln_b=small_r[31 * 64 + 128:].reshape(1, 64))
    off = 0
    for nm in REP_NAMES:
        n = int(np.prod(w_in[nm].shape))
        g_out[nm] = rep_all[off:off + n].reshape(w_in[nm].shape)
        off += n
    loss = rep_all[off]
    for group, rows, label in ((("a_conv_w", "c_pool_scale", "c_sln_g", "c_sln_b"), G_SMALL_ROWS, "adamw_small_sharded"),
                               (REP_NAMES, N_DEV * REP_ROWS, "adamw_replicated")):
        flat = [_pad_rows(jnp.concatenate([d[nm].reshape(-1) for nm in group]), rows) for d in (w_in, g_out, m_in, v_in)]
        res = [r.reshape(-1) for r in _adamw(*flat, label)]
        off = 0
        for nm in group:
            n = int(np.prod(w_in[nm].shape))
            delta[nm], new_m[nm], new_v[nm] = (r[off:off + n].reshape(w_in[nm].shape) for r in res)
            off += n

    grad_x = dx.reshape(bsz, seq, D)
    return (loss, grad_x, *[g_out[nm] for nm in names], *[delta[nm] for nm in names],
            *[new_m[nm] for nm in names], *[new_v[nm] for nm in names])
```

```python
import functools

import jax
import jax.numpy as jnp
import numpy as np
from jax import lax
from jax.experimental import pallas as pl
from jax.experimental.pallas import tpu as pltpu

F32 = jnp.float32
BF16 = jnp.bfloat16
MESH = pl.DeviceIdType.MESH

D = 1024
N_DEV = 8
EPS = 1e-5
HEAD_PAIRS = 4
ATT_BLK = 128
CONV_K = 31
HALO = 32
D_FF = 2816
FF_TILE_FWD = D_FF // 2
FF_TILE_BWD = D_FF // 2
IN0 = 1792
IN1 = 1536
POOL_WINDOWS = (2, 4, 8, 16)
SGU_CHUNK = 128
GELU_C = 0.7978845608028654
GELU_A = 0.044715
ADAM_LR, ADAM_B1, ADAM_B2, ADAM_EPS, ADAM_WD, ADAM_STEP = 0.001, 0.9, 0.999, 1e-08, 0.01, 10
VMEM_LIMIT = 56 << 20

SMALL_SHARD = 31 * 64 + 3 * 64
W_MISC_ROWS = 16
G_MISC_ROWS = 32
G_SMALL_ROWS = 8
REP_ROWS = G_MISC_ROWS - G_SMALL_ROWS
REP_NAMES = ("mix_norm", "a_b_in", "a_sinks", "a_conv_b", "a_cln_g", "a_cln_b", "c_w_pool", "c_w_s", "c_b_s",
             "ffn_norm", "final_norm")
Q_PERM = (0, 4, 1, 5, 2, 6, 3, 7)


def _params(*sem):
    return pltpu.CompilerParams(dimension_semantics=sem, vmem_limit_bytes=VMEM_LIMIT)


def _nn(a, b):
    return jnp.dot(a, b, preferred_element_type=F32)


def _nt(a, b):
    return lax.dot_general(a, b, (((1,), (1,)), ((), ())), preferred_element_type=F32)


def _tn(a, b):
    return lax.dot_general(a, b, (((0,), (0,)), ((), ())), preferred_element_type=F32)


def _tile(n, want=512):
    t = min(want, n)
    assert n % t == 0, (n, t)
    return t


def _seq_tile(s):
    return 512 if s >= 1024 else s // 2


def _rms(x, g):
    r = lax.rsqrt(jnp.mean(x * x, axis=-1, keepdims=True) + EPS)
    return x * r * g, r


def _rms_bwd(x, g, d_y):
    r = lax.rsqrt(jnp.mean(x * x, axis=-1, keepdims=True) + EPS)
    xr = x * r
    u = d_y * g
    d_x = r * (u - xr * jnp.mean(u * xr, axis=-1, keepdims=True))
    return d_x, jnp.sum(d_y * xr, axis=0, keepdims=True)


def _ln(y, g, b):
    mu = jnp.mean(y, axis=-1, keepdims=True)
    yc = y - mu
    rstd = lax.rsqrt(jnp.mean(yc * yc, axis=-1, keepdims=True) + EPS)
    xhat = yc * rstd
    return xhat * g + b, xhat, rstd


def _ln_bwd(d_o, xhat, rstd, g):
    dxh = d_o * g
    return rstd * (dxh - jnp.mean(dxh, axis=-1, keepdims=True) - xhat * jnp.mean(dxh * xhat, axis=-1, keepdims=True))


def _gelu(x):
    th = jnp.tanh(GELU_C * (x + GELU_A * x * x * x))
    return 0.5 * x * (1.0 + th), th


def _gelu_grad(x, th):
    return 0.5 * (1.0 + th) + 0.5 * x * (1.0 - th * th) * GELU_C * (1.0 + 3.0 * GELU_A * x * x)


def _row(c):
    return pl.BlockSpec((1, c), lambda *_: (0, 0))


def _full(shape):
    return pl.BlockSpec(shape, lambda *_: (0,) * len(shape))


def _norm_proj(h, g, wt, bias, name):
    t, n = h.shape[0], wt.shape[0]
    tm = _tile(t)
    has_bias = bias is not None

    def body(*refs):
        h_ref, g_ref, wt_ref = refs[:3]
        z_ref, hn_ref = refs[-2:]
        hn = _rms(h_ref[...], g_ref[...])[0].astype(BF16)
        hn_ref[...] = hn
        z = _nt(hn, wt_ref[...])
        if has_bias:
            z = z + refs[3][...]
        z_ref[...] = z.astype(BF16)

    in_specs = [pl.BlockSpec((tm, D), lambda i: (i, 0)), _row(D), _full((n, D))]
    args = [h, g, wt]
    if has_bias:
        in_specs.append(_row(n))
        args.append(bias)
    return pl.pallas_call(
        body, name=name, grid=(t // tm,), in_specs=in_specs,
        out_specs=[pl.BlockSpec((tm, n), lambda i: (i, 0)), pl.BlockSpec((tm, D), lambda i: (i, 0))],
        out_shape=[jax.ShapeDtypeStruct((t, n), BF16), jax.ShapeDtypeStruct((t, D), BF16)],
        compiler_params=_params("parallel"))(*args)


def _out_proj(a, b, w, h, name):
    t = h.shape[0]
    tm = _tile(t)

    def body(a_ref, b_ref, wa_ref, wb_ref, h_ref, o_ref):
        o_ref[...] = h_ref[...] + _nn(a_ref[...], wa_ref[...]) + _nn(b_ref[...], wb_ref[...])

    half = pl.BlockSpec((tm, 512), lambda i: (i, 0))
    return pl.pallas_call(
        body, name=name, grid=(t // tm,),
        in_specs=[half, half, pl.BlockSpec((512, D), lambda i: (0, 0)), pl.BlockSpec((512, D), lambda i: (1, 0)),
                  pl.BlockSpec((tm, D), lambda i: (i, 0))],
        out_specs=pl.BlockSpec((tm, D), lambda i: (i, 0)), out_shape=jax.ShapeDtypeStruct((t, D), F32),
        compiler_params=_params("parallel"))(a, b, w, w, h)


def _dmix(dh, w, name):
    t = dh.shape[0]
    tm = _tile(t)

    def body(dh_ref, w_ref, o_ref):
        o_ref[...] = _nt(dh_ref[...].astype(BF16), w_ref[...]).astype(BF16)

    return pl.pallas_call(
        body, name=name, grid=(t // tm,), in_specs=[pl.BlockSpec((tm, D), lambda i: (i, 0)), _full((D, D))],
        out_specs=pl.BlockSpec((tm, D), lambda i: (i, 0)), out_shape=jax.ShapeDtypeStruct((t, D), BF16),
        compiler_params=_params("parallel"))(dh, w)


def _ffn_fwd(h, g, wtg, wtu, wd, name):
    t = h.shape[0]
    tm, tf = _tile(t), FF_TILE_FWD
    nf = D_FF // tf

    def body(h_ref, g_ref, wtg_ref, wtu_ref, wd_ref, o_ref, hn_ref, gate_ref, up_ref):
        @pl.when(pl.program_id(1) == 0)
        def _():
            x = h_ref[...]
            hn_ref[...] = _rms(x, g_ref[...])[0].astype(BF16)
            o_ref[...] = x

        hn = hn_ref[...]
        gate = _nt(hn, wtg_ref[...])
        up = _nt(hn, wtu_ref[...])
        gate_ref[...] = gate.astype(BF16)
        up_ref[...] = up.astype(BF16)
        act = (gate * jax.nn.sigmoid(gate) * up).astype(BF16)
        o_ref[...] += _nn(act, wd_ref[...])

    tok = pl.BlockSpec((tm, D), lambda i, f: (i, 0))
    wsp = pl.BlockSpec((tf, D), lambda i, f: (f, 0))
    mid = pl.BlockSpec((tm, tf), lambda i, f: (i, f))
    return pl.pallas_call(
        body, name=name, grid=(t // tm, nf), in_specs=[tok, _row(D), wsp, wsp, wsp],
        out_specs=[tok, tok, mid, mid],
        out_shape=[jax.ShapeDtypeStruct((t, D), F32), jax.ShapeDtypeStruct((t, D), BF16),
                   jax.ShapeDtypeStruct((t, D_FF), BF16), jax.ShapeDtypeStruct((t, D_FF), BF16)],
        compiler_params=_params("parallel", "arbitrary"))(h, g, wtg, wtu, wd)


def _ffn_bwd(dh, h, g, gate, up, wtg, wtu, wd, name):
    t = h.shape[0]
    tm, tf = _tile(t), FF_TILE_BWD
    nf = D_FF // tf

    def body(dh_ref, h_ref, g_ref, gate_ref, up_ref, wtg_ref, wtu_ref, wd_ref,
             dhin_ref, dgate_ref, dup_ref, act_ref, dg_ref, dhb, dact):
        i, f = pl.program_id(0), pl.program_id(1)

        @pl.when(f == 0)
        def _():
            dhb[...] = dh_ref[...].astype(BF16)
            dhin_ref[...] = jnp.zeros_like(dhin_ref)

        @pl.when((i == 0) & (f == 0))
        def _():
            dg_ref[...] = jnp.zeros_like(dg_ref)

        dact[...] = _nt(dhb[...], wd_ref[...])
        for c0 in range(0, tf, 128):
            cols = slice(c0, c0 + 128)
            gt = gate_ref[:, cols].astype(F32)
            u = up_ref[:, cols].astype(F32)
            da = dact[:, cols]
            sg = jax.nn.sigmoid(gt)
            sil = gt * sg
            act_ref[:, cols] = (sil * u).astype(BF16)
            dup_ref[:, cols] = (da * sil).astype(BF16)
            dgate_ref[:, cols] = (da * u * sg * (1.0 + gt * (1.0 - sg))).astype(BF16)
        dhin_ref[...] += _nn(dgate_ref[...], wtg_ref[...]) + _nn(dup_ref[...], wtu_ref[...])

        @pl.when(f == nf - 1)
        def _():
            d_x, d_g = _rms_bwd(h_ref[...], g_ref[...], dhin_ref[...])
            dhin_ref[...] = dh_ref[...] + d_x
            dg_ref[...] += d_g

    tok = pl.BlockSpec((tm, D), lambda i, f: (i, 0))
    wsp = pl.BlockSpec((tf, D), lambda i, f: (f, 0))
    mid = pl.BlockSpec((tm, tf), lambda i, f: (i, f))
    mid_shape = jax.ShapeDtypeStruct((t, D_FF), BF16)
    return pl.pallas_call(
        body, name=name, grid=(t // tm, nf), in_specs=[tok, tok, _row(D), mid, mid, wsp, wsp, wsp],
        out_specs=[tok, mid, mid, mid, _row(D)],
        out_shape=[jax.ShapeDtypeStruct((t, D), F32), mid_shape, mid_shape, mid_shape, jax.ShapeDtypeStruct((1, D), F32)],
        scratch_shapes=[pltpu.VMEM((tm, D), BF16), pltpu.VMEM((tm, tf), F32)],
        compiler_params=_params("arbitrary", "arbitrary"))(dh, h, g, gate, up, wtg, wtu, wd)


def _proj_bwd_norm(pieces, wt, h, dh, g, name):
    t = h.shape[0]
    tm = _tile(t)
    n_p = len(pieces)

    def body(*refs):
        p_refs, w_refs = refs[:n_p], refs[n_p:2 * n_p]
        h_ref, dh_ref, g_ref, o_ref, dg_ref = refs[2 * n_p:]

        @pl.when(pl.program_id(0) == 0)
        def _():
            dg_ref[...] = jnp.zeros_like(dg_ref)

        d_hn = _nn(p_refs[0][...], w_refs[0][...])
        for p_ref, w_ref in zip(p_refs[1:], w_refs[1:]):
            d_hn = d_hn + _nn(p_ref[...], w_ref[...])
        d_x, d_g = _rms_bwd(h_ref[...], g_ref[...], d_hn)
        o_ref[...] = dh_ref[...] + d_x
        dg_ref[...] += d_g

    tok = pl.BlockSpec((tm, D), lambda i: (i, 0))
    in_specs = [pl.BlockSpec((tm, a.shape[1]), lambda i: (i, 0)) for a, _ in pieces]
    for a, off in pieces:
        w = a.shape[1]
        assert off % w == 0
        in_specs.append(pl.BlockSpec((w, D), functools.partial(lambda i, blk: (blk, 0), blk=off // w)))
    in_specs += [tok, tok, _row(D)]
    return pl.pallas_call(
        body, name=name, grid=(t // tm,), in_specs=in_specs, out_specs=[tok, _row(D)],
        out_shape=[jax.ShapeDtypeStruct((t, D), F32), jax.ShapeDtypeStruct((1, D), F32)],
        compiler_params=_params("arbitrary"))(*[a for a, _ in pieces], *([wt] * n_p), h, dh, g)


def _mm_tn(a, b, name):
    t, n = a.shape
    k = b.shape[1]
    tn = n if n <= 1024 else n // 2
    tt = _tile(t, 1024)
    nt = t // tt

    def body(a_ref, b_ref, o_ref, acc):
        s = pl.program_id(1)

        @pl.when(s == 0)
        def _():
            acc[...] = jnp.zeros_like(acc)

        acc[...] += _tn(a_ref[...], b_ref[...].astype(BF16))

        @pl.when(s == nt - 1)
        def _():
            o_ref[...] = acc[...].astype(BF16)

    return pl.pallas_call(
        body, name=name, grid=(n // tn, nt),
        in_specs=[pl.BlockSpec((tt, tn), lambda j, s: (s, j)), pl.BlockSpec((tt, k), lambda j, s: (s, 0))],
        out_specs=pl.BlockSpec((tn, k), lambda j, s: (j, 0)), out_shape=jax.ShapeDtypeStruct((n, k), BF16),
        scratch_shapes=[pltpu.VMEM((tn, k), F32)],
        compiler_params=_params("parallel", "arbitrary"))(a, b)


def _mm_tn_pieces(pieces, b, name):
    t, k = b.shape
    widths = [p.shape[1] for p in pieces]
    n, n_p = sum(widths), len(pieces)
    tt = _tile(t, 1024)
    nt = t // tt

    def body(*refs):
        b_ref, o_ref, acc = refs[n_p:]
        s = pl.program_id(0)

        @pl.when(s == 0)
        def _():
            acc[...] = jnp.zeros_like(acc)

        bb = b_ref[...].astype(BF16)
        off = 0
        for p_ref, w in zip(refs[:n_p], widths):
            acc[off:off + w, :] += _tn(p_ref[...], bb)
            off += w

        @pl.when(s == nt - 1)
        def _():
            o_ref[...] = acc[...].astype(BF16)

    return pl.pallas_call(
        body, name=name, grid=(nt,),
        in_specs=[pl.BlockSpec((tt, w), lambda s: (s, 0)) for w in widths] + [pl.BlockSpec((tt, k), lambda s: (s, 0))],
        out_specs=_full((n, k)), out_shape=jax.ShapeDtypeStruct((n, k), BF16),
        scratch_shapes=[pltpu.VMEM((n, k), F32)], compiler_params=_params("arbitrary"))(*pieces, b)


def _loss_head(h, g, target, name):
    t = h.shape[0]
    tm = _tile(t)

    def body(h_ref, g_ref, t_ref, dh_ref, dg_ref, loss_ref):
        @pl.when(pl.program_id(0) == 0)
        def _():
            dg_ref[...] = jnp.zeros_like(dg_ref)
            loss_ref[...] = jnp.zeros_like(loss_ref)

        x = h_ref[...]
        gv = g_ref[...]
        err = _rms(x, gv)[0] - t_ref[...]
        per_tok = jnp.mean(err * err, axis=-1, keepdims=True)
        loss_ref[...] += 0.5 * jnp.sum(per_tok, axis=0, keepdims=True)
        d_x, d_g = _rms_bwd(x, gv, err * (1.0 / D))
        dh_ref[...] = d_x
        dg_ref[...] += d_g

    tok = pl.BlockSpec((tm, D), lambda i: (i, 0))
    return pl.pallas_call(
        body, name=name, grid=(t // tm,), in_specs=[tok, _row(D), tok],
        out_specs=[tok, _row(D), _row(1)],
        out_shape=[jax.ShapeDtypeStruct((t, D), F32), jax.ShapeDtypeStruct((1, D), F32), jax.ShapeDtypeStruct((1, 1), F32)],
        compiler_params=_params("arbitrary"))(h, g, target)


STACK = HEAD_PAIRS * ATT_BLK


def _attn_valid(first, rows):
    qi = lax.broadcasted_iota(jnp.int32, (rows, 2 * ATT_BLK), 0) % ATT_BLK
    r = lax.broadcasted_iota(jnp.int32, (rows, 2 * ATT_BLK), 1)
    dist = qi + ATT_BLK - r
    return (dist >= 0) & (dist < ATT_BLK) & ((r >= ATT_BLK) | jnp.logical_not(first))


def _stacked(ref, kh, scale):
    lo = lax.broadcasted_iota(jnp.int32, (ATT_BLK, 128), 1) < 64
    keep = lo if kh == 0 else ~lo
    parts = [jnp.where(keep, ref[:, g * 128:(g + 1) * 128] * scale, 0.0).astype(BF16) for g in range(HEAD_PAIRS)]
    return jnp.concatenate(parts, axis=0)


def _unstacked(a0, a1, g):
    lo = lax.broadcasted_iota(jnp.int32, (ATT_BLK, 128), 1) < 64
    rows = slice(g * ATT_BLK, (g + 1) * ATT_BLK)
    return jnp.where(lo, a0[rows], a1[rows])


def _sink_rows(s_ref, kh):
    return jnp.concatenate([jnp.full((ATT_BLK, 128), s_ref[0, kh * 4 + g], F32) for g in range(HEAD_PAIRS)], axis=0)


def _row_sums(a):
    hi = a.astype(BF16)
    lo = (a - hi.astype(F32)).astype(BF16)
    ones = jnp.ones((2 * ATT_BLK, 128), BF16)
    return _nn(hi, ones) + _nn(lo, ones)


def _both(a):
    return jnp.concatenate([a, a], axis=1)


def _attn_probs(qs, kpair, sink, valid):
    s = jnp.where(valid, _nt(qs, kpair), -1e30)
    m = jnp.maximum(jnp.broadcast_to(jnp.max(s, axis=-1, keepdims=True), (s.shape[0], 128)), sink)
    p = jnp.exp(s - _both(m))
    es = jnp.exp(sink - m)
    inv = 1.0 / (_row_sums(p) + es)
    return p * _both(inv), es * inv


def _attn_probs_head(qm, kpair, sink, valid):
    s = jnp.where(valid, _nt(qm, kpair), -1e30)
    m = jnp.maximum(jnp.max(s, axis=-1, keepdims=True), sink)
    p = jnp.exp(s - m)
    return p * (1.0 / (jnp.sum(p, axis=-1, keepdims=True) + jnp.exp(sink - m)))


def _attn_specs(bsz, order):
    q = pl.BlockSpec((bsz, ATT_BLK, 512), lambda j: (0, order(j), 0))
    kvc = pl.BlockSpec((bsz, ATT_BLK, 256), lambda j: (0, order(j), 6))
    kvp = pl.BlockSpec((bsz, ATT_BLK, 256), lambda j: (0, jnp.maximum(order(j) - 1, 0), 6))
    return q, kvc, kvp


def _window_kv(kvc_ref, kvp_ref):
    kvc, kvp = kvc_ref[...], kvp_ref[...]
    kpair = jnp.concatenate([kvp[:, :128], kvc[:, :128]], axis=0)
    vpair = jnp.concatenate([kvp[:, 128:], kvc[:, 128:]], axis=0)
    return kpair, vpair


def _attn_fwd(z0, sinks, bsz, name):
    t = z0.shape[0]
    seq = t // bsz
    nb = seq // ATT_BLK

    def body(s_ref, q_ref, kvc_ref, kvp_ref, o_ref):
        valid = _attn_valid(pl.program_id(0) == 0, ATT_BLK)
        lo = lax.broadcasted_iota(jnp.int32, (ATT_BLK, 128), 1) < 64
        for b in range(bsz):
            kpair, vpair = _window_kv(kvc_ref.at[b], kvp_ref.at[b])
            for g in range(HEAD_PAIRS):
                qs = q_ref[b, :, g * 128:(g + 1) * 128] * 0.125
                outs = []
                for kh in range(2):
                    qm = jnp.where(lo if kh == 0 else ~lo, qs, 0.0).astype(BF16)
                    p = _attn_probs_head(qm, kpair, s_ref[0, kh * 4 + g], valid)
                    outs.append(_nn(p.astype(BF16), vpair))
                o_ref[b, :, g * 128:(g + 1) * 128] = jnp.where(lo, outs[0], outs[1]).astype(BF16)

    q, kvc, kvp = _attn_specs(bsz, lambda j: j)
    z3 = z0.reshape(bsz, seq, z0.shape[1])
    return pl.pallas_call(
        body, name=name, grid=(nb,),
        in_specs=[pl.BlockSpec(memory_space=pltpu.SMEM), q, kvc, kvp],
        out_specs=pl.BlockSpec((bsz, ATT_BLK, 512), lambda j: (0, j, 0)),
        out_shape=jax.ShapeDtypeStruct((bsz, seq, 512), BF16),
        compiler_params=_params("parallel"))(sinks, z3, z3, z3).reshape(t, 512)


def _attn_bwd(z0, dmix, sinks, bsz, name):
    t = z0.shape[0]
    seq = t // bsz
    nb = seq // ATT_BLK

    def body(s_ref, q_ref, kvc_ref, kvp_ref, do_ref, dq_ref, dkv_ref, dsink_ref, dbq_ref, dbkv_ref, carry):
        j = pl.program_id(0)

        @pl.when(j == 0)
        def _():
            carry[...] = jnp.zeros_like(carry)
            dsink_ref[...] = jnp.zeros_like(dsink_ref)
            dbq_ref[...] = jnp.zeros_like(dbq_ref)
            dbkv_ref[...] = jnp.zeros_like(dbkv_ref)

        valid = _attn_valid(j == nb - 1, STACK)
        lane = lax.broadcasted_iota(jnp.int32, (1, 128), 1)
        dsink = jnp.zeros((1, 128), F32)
        dbq = [jnp.zeros((1, 128), F32)] * HEAD_PAIRS
        dbkv = jnp.zeros((1, 256), F32)
        for b in range(bsz):
            kpair, vpair = _window_kv(kvc_ref.at[b], kvp_ref.at[b])
            dk = jnp.zeros((2 * ATT_BLK, 128), F32)
            dv = jnp.zeros((2 * ATT_BLK, 128), F32)
            dqs = []
            for kh in range(2):
                qs = _stacked(q_ref.at[b], kh, 0.125)
                dos = _stacked(do_ref.at[b], kh, 1.0)
                p, ps = _attn_probs(qs, kpair, _sink_rows(s_ref, kh), valid)
                dp = _nt(dos, vpair)
                delta = _row_sums(p * dp)
                ds = (p * (dp - _both(delta))).astype(BF16)
                dqs.append(_nn(ds, kpair))
                dk = dk + _tn(ds, qs)
                dv = dv + _tn(p.astype(BF16), dos)
                psd = ps * delta
                for g in range(HEAD_PAIRS):
                    part = jnp.sum(psd[g * ATT_BLK:(g + 1) * ATT_BLK], axis=0, keepdims=True)
                    dsink = dsink - jnp.where(lane == kh * 4 + g, part, 0.0)
            for g in range(HEAD_PAIRS):
                dq = _unstacked(dqs[0], dqs[1], g) * 0.125
                dq_ref[b, :, g * 128:(g + 1) * 128] = dq.astype(BF16)
                dbq[g] = dbq[g] + jnp.sum(dq, axis=0, keepdims=True)
            dkv = jnp.concatenate([dk[ATT_BLK:], dv[ATT_BLK:]], axis=1) + carry[b]
            dkv_ref[b] = dkv.astype(BF16)
            dbkv = dbkv + jnp.sum(dkv, axis=0, keepdims=True)
            carry[b] = jnp.concatenate([dk[:ATT_BLK], dv[:ATT_BLK]], axis=1)
        dsink_ref[...] += dsink
        dbq_ref[...] += jnp.concatenate(dbq, axis=1)
        dbkv_ref[...] += dbkv

    q, kvc, kvp = _attn_specs(bsz, lambda j: nb - 1 - j)
    z3 = z0.reshape(bsz, seq, z0.shape[1])
    d3 = dmix.reshape(bsz, seq, dmix.shape[1])
    dq, dkv, dsink, dbq, dbkv = pl.pallas_call(
        body, name=name, grid=(nb,),
        in_specs=[pl.BlockSpec(memory_space=pltpu.SMEM), q, kvc, kvp,
                  pl.BlockSpec((bsz, ATT_BLK, 512), lambda j: (0, nb - 1 - j, 0))],
        out_specs=[pl.BlockSpec((bsz, ATT_BLK, 512), lambda j: (0, nb - 1 - j, 0)),
                   pl.BlockSpec((bsz, ATT_BLK, 256), lambda j: (0, nb - 1 - j, 0)), _row(128), _row(512), _row(256)],
        out_shape=[jax.ShapeDtypeStruct((bsz, seq, 512), BF16), jax.ShapeDtypeStruct((bsz, seq, 256), BF16),
                   jax.ShapeDtypeStruct((1, 128), F32), jax.ShapeDtypeStruct((1, 512), F32),
                   jax.ShapeDtypeStruct((1, 256), F32)],
        scratch_shapes=[pltpu.VMEM((bsz, ATT_BLK, 256), F32)],
        compiler_params=_params("arbitrary"))(sinks, z3, z3, z3, d3)
    return dq.reshape(t, 512), dkv.reshape(t, 256), dsink, dbq, dbkv


def _seq_specs(ts, nt, t, width, col):
    per = ts // HALO
    cur = pl.BlockSpec((ts, width), lambda b, i: (b * nt + i, col))
    prev = pl.BlockSpec((HALO, width), lambda b, i: (jnp.maximum((b * nt + i) * per - 1, 0), col))
    nxt = pl.BlockSpec((HALO, width), lambda b, i: (jnp.minimum((b * nt + i + 1) * per, t // HALO - 1), col))
    return prev, cur, nxt


SUB = 8
CONV_ROWS = 64


def _shifted_copies(src, sh, rows_first, rows_rest):
    for r in range(SUB):
        rows = rows_first if r == 0 else rows_rest
        sh[r, pl.ds(0, rows), :] = src[pl.ds(r, rows), :]


def _tap_sum(sh, w, offset, c0, rows):
    acc = None
    for k in range(CONV_K):
        o = offset(k)
        term = sh[o % SUB, pl.ds(c0 + o - o % SUB, rows), :] * w[k:k + 1, :]
        acc = term if acc is None else acc + term
    return acc


def _glu_rows(a_ref, g_ref, rows=slice(None)):
    return a_ref[rows, :].astype(F32) * jax.nn.sigmoid(g_ref[rows, :].astype(F32))


def _conv_fwd(z0, conv_w, conv_b, ln_g, ln_b, bsz, name):
    t = z0.shape[0]
    s = t // bsz
    ts = _seq_tile(s)
    nt = s // ts
    first = HALO - (CONV_K - 1)

    def body(ap_ref, ac_ref, gp_ref, gc_ref, w_ref, cb_ref, lg_ref, lb_ref, o_ref, y_ref, hbuf, sh):
        hbuf[0:HALO, :] = jnp.where(pl.program_id(1) > 0, _glu_rows(ap_ref, gp_ref), 0.0)
        hbuf[HALO:HALO + ts, :] = _glu_rows(ac_ref, gc_ref)
        _shifted_copies(hbuf, sh, ts + HALO, ts + HALO - SUB)
        w, cb, lg, lb = w_ref[...], cb_ref[...], lg_ref[...], lb_ref[...]
        for c0 in range(0, ts, CONV_ROWS):
            y = _tap_sum(sh, w, lambda k: first + k, c0, CONV_ROWS) + cb
            y_ref[c0:c0 + CONV_ROWS, :] = y
            o = _ln(y, lg, lb)[0]
            o_ref[c0:c0 + CONV_ROWS, :] = (o * jax.nn.sigmoid(o)).astype(BF16)

    ap, ac, _ = _seq_specs(ts, nt, t, 512, 1)
    gp, gc, _ = _seq_specs(ts, nt, t, 512, 2)
    tile = pl.BlockSpec((ts, 512), lambda b, i: (b * nt + i, 0))
    return pl.pallas_call(
        body, name=name, grid=(bsz, nt),
        in_specs=[ap, ac, gp, gc, _full((HALO, 512)), _row(512), _row(512), _row(512)],
        out_specs=[tile, tile],
        out_shape=[jax.ShapeDtypeStruct((t, 512), BF16), jax.ShapeDtypeStruct((t, 512), F32)],
        scratch_shapes=[pltpu.VMEM((HALO + ts, 512), F32), pltpu.VMEM((SUB, HALO + ts, 512), F32)],
        compiler_params=_params("parallel", "parallel"))(z0, z0, z0, z0, conv_w, conv_b, ln_g, ln_b)


def _conv_bwd(z0, y, dmix, conv_w, ln_g, ln_b, bsz, name):
    t = z0.shape[0]
    s = t // bsz
    ts = _seq_tile(s)
    nt = s // ts
    first = HALO - (CONV_K - 1)

    def body(ap_ref, ac_ref, gp_ref, gc_ref, yc_ref, yn_ref, dc_ref, dn_ref, w_ref, lg_ref, lb_ref,
             da_ref, dg_ref, dw_ref, dcb_ref, dlg_ref, dlb_ref, dba_ref, dbg_ref, hbuf, dybuf, sh_h, sh_dy):
        b, i = pl.program_id(0), pl.program_id(1)

        @pl.when((b == 0) & (i == 0))
        def _():
            for ref in (dw_ref, dcb_ref, dlg_ref, dlb_ref, dba_ref, dbg_ref):
                ref[...] = jnp.zeros_like(ref)

        w, lg, lb = w_ref[...], lg_ref[...], lb_ref[...]
        hbuf[0:HALO, :] = jnp.where(i > 0, _glu_rows(ap_ref, gp_ref), 0.0)
        hbuf[HALO:HALO + ts, :] = _glu_rows(ac_ref, gc_ref)
        _shifted_copies(hbuf, sh_h, ts + HALO, ts + HALO - SUB)

        def d_conv_out(yv, dout):
            o, xhat, rstd = _ln(yv, lg, lb)
            sg_o = jax.nn.sigmoid(o)
            d_o = dout * sg_o * (1.0 + o * (1.0 - sg_o))
            return _ln_bwd(d_o, xhat, rstd, lg), d_o * xhat, d_o

        dlg = jnp.zeros((1, 512), F32)
        dlb = jnp.zeros((1, 512), F32)
        dcb = jnp.zeros((1, 512), F32)
        for c0 in range(0, ts, CONV_ROWS):
            rows = slice(c0, c0 + CONV_ROWS)
            dy, g_part, b_part = d_conv_out(yc_ref[rows, :], dc_ref[rows, :].astype(F32))
            dybuf[rows, :] = dy
            dlg = dlg + jnp.sum(g_part, axis=0, keepdims=True)
            dlb = dlb + jnp.sum(b_part, axis=0, keepdims=True)
            dcb = dcb + jnp.sum(dy, axis=0, keepdims=True)
        dn = jnp.where(i < nt - 1, dn_ref[...].astype(F32), 0.0)
        dybuf[ts:ts + HALO, :] = d_conv_out(yn_ref[...], dn)[0]
        dlg_ref[...] += dlg
        dlb_ref[...] += dlb
        dcb_ref[...] += dcb
        _shifted_copies(dybuf, sh_dy, ts + HALO - SUB, ts + HALO - SUB)

        for k in range(CONV_K):
            o = first + k
            prod = dybuf[0:ts, :] * sh_h[o % SUB, pl.ds(o - o % SUB, ts), :]
            dw_ref[pl.ds(k, 1), :] += jnp.sum(prod, axis=0, keepdims=True)
        dba = jnp.zeros((1, 512), F32)
        dbg = jnp.zeros((1, 512), F32)
        for c0 in range(0, ts, CONV_ROWS):
            rows = slice(c0, c0 + CONV_ROWS)
            dh = _tap_sum(sh_dy, w, lambda k: CONV_K - 1 - k, c0, CONV_ROWS)
            a_c = ac_ref[rows, :].astype(F32)
            sg_c = jax.nn.sigmoid(gc_ref[rows, :].astype(F32))
            d_a = dh * sg_c
            d_g = dh * a_c * sg_c * (1.0 - sg_c)
            da_ref[rows, :] = d_a.astype(BF16)
            dg_ref[rows, :] = d_g.astype(BF16)
            dba = dba + jnp.sum(d_a, axis=0, keepdims=True)
            dbg = dbg + jnp.sum(d_g, axis=0, keepdims=True)
        dba_ref[...] += dba
        dbg_ref[...] += dbg

    ap, ac, _ = _seq_specs(ts, nt, t, 512, 1)
    gp, gc, _ = _seq_specs(ts, nt, t, 512, 2)
    _, yc, yn = _seq_specs(ts, nt, t, 512, 0)
    _, dc, dn = _seq_specs(ts, nt, t, 512, 1)
    tile = pl.BlockSpec((ts, 512), lambda b, i: (b * nt + i, 0))
    vec = jax.ShapeDtypeStruct((1, 512), F32)
    return pl.pallas_call(
        body, name=name, grid=(bsz, nt),
        in_specs=[ap, ac, gp, gc, yc, yn, dc, dn, _full((HALO, 512)), _row(512), _row(512)],
        out_specs=[tile, tile, _full((HALO, 512)), _row(512), _row(512), _row(512), _row(512), _row(512)],
        out_shape=[jax.ShapeDtypeStruct((t, 512), BF16), jax.ShapeDtypeStruct((t, 512), BF16),
                   jax.ShapeDtypeStruct((HALO, 512), F32), vec, vec, vec, vec, vec],
        scratch_shapes=[pltpu.VMEM((HALO + ts, 512), F32), pltpu.VMEM((ts + HALO, 512), F32),
                        pltpu.VMEM((SUB, HALO + ts, 512), F32), pltpu.VMEM((SUB, HALO + ts, 512), F32)],
        compiler_params=_params("arbitrary", "arbitrary"))(z0, z0, z0, z0, y, y, dmix, dmix, conv_w, ln_g, ln_b)


def _pooled(pbuf, g, ts, tok):
    w = 2 << g
    cols = slice(128 * g, 128 * (g + 1))
    sm = pbuf[pl.ds(HALO, ts), cols]
    for d in range(1, w):
        sm = sm + pbuf[pl.ds(HALO - d, ts), cols]
    cnt = jnp.minimum(tok + 1, w).astype(F32)
    return sm / cnt - pbuf[pl.ds(HALO, ts), cols]


def _pool_fwd(z1, w_pool, scale, bsz, name):
    t = z1.shape[0]
    s = t // bsz
    ts = _seq_tile(s)
    nt = s // ts

    def body(zp_ref, zc_ref, wp_ref, sc_ref, o_ref, pbuf):
        i = pl.program_id(1)
        pbuf[0:HALO, :] = jnp.where(i > 0, zp_ref[...].astype(F32), 0.0)
        pbuf[HALO:HALO + ts, :] = zc_ref[...].astype(F32)
        tok = i * ts + lax.broadcasted_iota(jnp.int32, (ts, 1), 0)
        for g in range(4):
            cols = slice(128 * g, 128 * (g + 1))
            pooled = _pooled(pbuf, g, ts, tok).astype(BF16)
            o_ref[:, cols] = (_nn(pooled, wp_ref[g].astype(BF16)) * sc_ref[:, cols]).astype(BF16)

    zp, zc, _ = _seq_specs(ts, nt, t, 512, 0)
    return pl.pallas_call(
        body, name=name, grid=(bsz, nt), in_specs=[zp, zc, _full((4, 128, 128)), _row(512)],
        out_specs=pl.BlockSpec((ts, 512), lambda b, i: (b * nt + i, 0)),
        out_shape=jax.ShapeDtypeStruct((t, 512), BF16),
        scratch_shapes=[pltpu.VMEM((HALO + ts, 512), F32)],
        compiler_params=_params("parallel", "parallel"))(z1, z1, w_pool, scale)


def _pool_bwd(z1, dmix, w_pool, scale, bsz, name):
    t = z1.shape[0]
    s = t // bsz
    ts = _seq_tile(s)
    nt = s // ts
    rr = ts + HALO

    def body(zp_ref, zc_ref, dc_ref, dn_ref, wp_ref, sc_ref, dz_ref, dwp_ref, dsc_ref, pbuf, ebuf):
        b, i = pl.program_id(0), pl.program_id(1)

        @pl.when((b == 0) & (i == 0))
        def _():
            dwp_ref[...] = jnp.zeros_like(dwp_ref)
            dsc_ref[...] = jnp.zeros_like(dsc_ref)

        pbuf[0:HALO, :] = jnp.where(i > 0, zp_ref[...].astype(F32), 0.0)
        pbuf[HALO:HALO + ts, :] = zc_ref[...].astype(F32)
        dn = jnp.where(i < nt - 1, dn_ref[...].astype(F32), 0.0)
        dout = jnp.concatenate([dc_ref[...].astype(F32), dn], axis=0)
        tok = i * ts + lax.broadcasted_iota(jnp.int32, (ts, 1), 0)
        tok_r = i * ts + lax.broadcasted_iota(jnp.int32, (rr, 1), 0)
        for g in range(4):
            w = 2 << g
            cols = slice(128 * g, 128 * (g + 1))
            wg = wp_ref[g].astype(BF16)
            pooled = _pooled(pbuf, g, ts, tok).astype(BF16)
            dsc_ref[:, cols] += jnp.sum(dout[:ts, cols] * _nn(pooled, wg), axis=0, keepdims=True)
            dy = (dout[:, cols] * sc_ref[:, cols]).astype(BF16)
            dwp_ref[g] += _tn(pooled, dy[:ts])
            dpl = _nt(dy, wg)
            ebuf[...] = dpl / jnp.minimum(tok_r + 1, w).astype(F32)
            dz = ebuf[pl.ds(0, ts), :] - dpl[:ts]
            for d in range(1, w):
                dz = dz + ebuf[pl.ds(d, ts), :]
            dz_ref[:, cols] = dz.astype(BF16)

    zp, zc, _ = _seq_specs(ts, nt, t, 512, 0)
    _, dc, dn = _seq_specs(ts, nt, t, 512, 0)
    return pl.pallas_call(
        body, name=name, grid=(bsz, nt), in_specs=[zp, zc, dc, dn, _full((4, 128, 128)), _row(512)],
        out_specs=[pl.BlockSpec((ts, 512), lambda b, i: (b * nt + i, 0)), _full((4, 128, 128)), _row(512)],
        out_shape=[jax.ShapeDtypeStruct((t, 512), BF16), jax.ShapeDtypeStruct((4, 128, 128), F32),
                   jax.ShapeDtypeStruct((1, 512), F32)],
        scratch_shapes=[pltpu.VMEM((HALO + ts, 512), F32), pltpu.VMEM((rr, 128), F32)],
        compiler_params=_params("arbitrary", "arbitrary"))(z1, z1, dmix, dmix, w_pool, scale)


def _tril():
    r = lax.broadcasted_iota(jnp.int32, (SGU_CHUNK, SGU_CHUNK), 0)
    c = lax.broadcasted_iota(jnp.int32, (SGU_CHUNK, SGU_CHUNK), 1)
    return r >= c


def _sgu_fwd(z1, ln_g, ln_b, w_s, b_rows, name):
    t = z1.shape[0]
    ts = _tile(t)

    def body(zu_ref, zv_ref, lg_ref, lb_ref, ws_ref, bs_ref, o_ref):
        v = _gelu(zv_ref[...].astype(F32))[0]
        vb = _ln(v, lg_ref[...], lb_ref[...])[0].astype(BF16)
        tril = _tril()
        for g in range(4):
            cols = slice(128 * g, 128 * (g + 1))
            wg = jnp.where(tril, ws_ref[g], 0.0).astype(BF16)
            for c in range(ts // SGU_CHUNK):
                rows = slice(SGU_CHUNK * c, SGU_CHUNK * (c + 1))
                mixed = _nn(wg, vb[rows, cols]) + bs_ref[g]
                o_ref[rows, cols] = (_gelu(zu_ref[rows, cols].astype(F32))[0] * mixed).astype(BF16)

    return pl.pallas_call(
        body, name=name, grid=(t // ts,),
        in_specs=[pl.BlockSpec((ts, 512), lambda i: (i, 1)), pl.BlockSpec((ts, 512), lambda i: (i, 2)),
                  _row(512), _row(512), _full((4, 128, 128)), _full((4, 128, 128))],
        out_specs=pl.BlockSpec((ts, 512), lambda i: (i, 0)), out_shape=jax.ShapeDtypeStruct((t, 512), BF16),
        compiler_params=_params("parallel"))(z1, z1, ln_g, ln_b, w_s, b_rows)


def _sgu_bwd(z1, dmix, ln_g, ln_b, w_s, b_rows, name):
    t = z1.shape[0]
    ts = _tile(t)

    def body(zu_ref, zv_ref, d_ref, lg_ref, lb_ref, ws_ref, bs_ref,
             dzu_ref, dzv_ref, dws_ref, dbs_ref, dlg_ref, dlb_ref, dvbuf):
        @pl.when(pl.program_id(0) == 0)
        def _():
            for ref in (dws_ref, dbs_ref, dlg_ref, dlb_ref):
                ref[...] = jnp.zeros_like(ref)

        zv = zv_ref[...].astype(F32)
        v, thv = _gelu(zv)
        lg = lg_ref[...]
        vln, xhat, rstd = _ln(v, lg, lb_ref[...])
        vb = vln.astype(BF16)
        tril = _tril()
        for g in range(4):
            cols = slice(128 * g, 128 * (g + 1))
            wg = jnp.where(tril, ws_ref[g], 0.0).astype(BF16)
            dws = jnp.zeros((SGU_CHUNK, SGU_CHUNK), F32)
            dbs = jnp.zeros((1, SGU_CHUNK), F32)
            for c in range(ts // SGU_CHUNK):
                rows = slice(SGU_CHUNK * c, SGU_CHUNK * (c + 1))
                vbc = vb[rows, cols]
                mixed = _nn(wg, vbc) + bs_ref[g]
                zu = zu_ref[rows, cols].astype(F32)
                u, thu = _gelu(zu)
                dout = d_ref[rows, cols].astype(F32)
                dzu_ref[rows, cols] = (dout * mixed * _gelu_grad(zu, thu)).astype(BF16)
                dm = dout * u
                dmb = dm.astype(BF16)
                dws = dws + _nt(dmb, vbc)
                dbs = dbs + jnp.sum(dm.T, axis=0, keepdims=True)
                dvbuf[rows, cols] = _tn(wg, dmb)
            dws_ref[g] += jnp.where(tril, dws, 0.0)
            dbs_ref[pl.ds(g, 1), :] += dbs
        dvln = dvbuf[...]
        dlg_ref[...] += jnp.sum(dvln * xhat, axis=0, keepdims=True)
        dlb_ref[...] += jnp.sum(dvln, axis=0, keepdims=True)
        dzv_ref[...] = (_ln_bwd(dvln, xhat, rstd, lg) * _gelu_grad(zv, thv)).astype(BF16)

    tile = pl.BlockSpec((ts, 512), lambda i: (i, 0))
    vec = jax.ShapeDtypeStruct((1, 512), F32)
    return pl.pallas_call(
        body, name=name, grid=(t // ts,),
        in_specs=[pl.BlockSpec((ts, 512), lambda i: (i, 1)), pl.BlockSpec((ts, 512), lambda i: (i, 2)),
                  pl.BlockSpec((ts, 512), lambda i: (i, 1)), _row(512), _row(512), _full((4, 128, 128)),
                  _full((4, 128, 128))],
        out_specs=[tile, tile, _full((4, 128, 128)), _full((4, 128)), _row(512), _row(512)],
        out_shape=[jax.ShapeDtypeStruct((t, 512), BF16), jax.ShapeDtypeStruct((t, 512), BF16),
                   jax.ShapeDtypeStruct((4, 128, 128), F32), jax.ShapeDtypeStruct((4, 128), F32), vec, vec],
        scratch_shapes=[pltpu.VMEM((ts, 512), F32)],
        compiler_params=_params("arbitrary"))(z1, z1, dmix, ln_g, ln_b, w_s, b_rows)


def _row_tile(r):
    for cand in (512, 352, 256, 192, 128, 64, 32, 16, 8):
        if r % cand == 0:
            return cand
    return r


def _sum_slabs(a, name):
    k, r, c = a.shape
    tr = _row_tile(r)

    def body(*refs):
        acc = refs[0][...].astype(F32)
        for ref in refs[1:-1]:
            acc = acc + ref[...].astype(F32)
        refs[-1][...] = acc

    in_specs = [pl.BlockSpec((None, tr, c), functools.partial(lambda i, s: (s, i, 0), s=s)) for s in range(k)]
    return pl.pallas_call(
        body, name=name, grid=(r // tr,), in_specs=in_specs, out_specs=pl.BlockSpec((tr, c), lambda i: (i, 0)),
        out_shape=jax.ShapeDtypeStruct((r, c), F32), compiler_params=_params("parallel"))(*([a] * k))


def _adamw_math(w, g, m, v):
    mn = ADAM_B1 * m + (1.0 - ADAM_B1) * g
    vn = ADAM_B2 * v + (1.0 - ADAM_B2) * (g * g)
    m_hat = mn / (1.0 - ADAM_B1 ** ADAM_STEP)
    v_hat = vn / (1.0 - ADAM_B2 ** ADAM_STEP)
    return -ADAM_LR * (m_hat / (jnp.sqrt(v_hat) + ADAM_EPS) + ADAM_WD * w), mn, vn


def _reduce_adamw(landing, w, m, v, name, layer=None, into=None):
    k, r, c = landing.shape
    tr = _row_tile(r)
    n_into = 0 if into is None else 4

    def body(*refs):
        slabs, (w_ref, m_ref, v_ref) = refs[:k], refs[k:k + 3]
        g_ref, d_ref, mo_ref, vo_ref = refs[k + 3 + n_into:]
        g = slabs[0][...].astype(F32)
        for ref in slabs[1:]:
            g = g + ref[...].astype(F32)
        g_ref[...] = g
        d_ref[...], mo_ref[...], vo_ref[...] = _adamw_math(w_ref[...], g, m_ref[...], v_ref[...])

    if layer is None:
        spec = pl.BlockSpec((tr, c), lambda i: (i, 0))
    else:
        spec = pl.BlockSpec((None, tr, c), lambda i: (layer, i, 0))
    in_specs = [pl.BlockSpec((None, tr, c), functools.partial(lambda i, s: (s, i, 0), s=s)) for s in range(k)]
    in_specs += [spec] * 3 + [ANY] * n_into
    shape = jax.ShapeDtypeStruct(w.shape, F32)
    return pl.pallas_call(
        body, name=name, grid=(r // tr,), in_specs=in_specs, out_specs=[spec] * 4, out_shape=[shape] * 4,
        input_output_aliases={k + 3 + j: j for j in range(n_into)},
        compiler_params=_params("parallel"))(*([landing] * k), w, m, v, *(into or ()))


def _adamw(w, g, m, v, name):
    r, c = w.shape
    tr = _row_tile(r)

    def body(w_ref, g_ref, m_ref, v_ref, d_ref, mo_ref, vo_ref):
        d_ref[...], mo_ref[...], vo_ref[...] = _adamw_math(w_ref[...], g_ref[...], m_ref[...], v_ref[...])

    spec = pl.BlockSpec((tr, c), lambda i: (i, 0))
    shape = jax.ShapeDtypeStruct((r, c), F32)
    return pl.pallas_call(
        body, name=name, grid=(r // tr,), in_specs=[spec] * 4, out_specs=[spec] * 3, out_shape=[shape] * 3,
        compiler_params=_params("parallel"))(w, g, m, v)


ANY = pl.BlockSpec(memory_space=pl.ANY)


def _all_gather(block, name):
    r, c_dim = block.shape

    def body(x_ref, out_ref, token, send_sems, recv_sems, local_sem):
        token[...] = jnp.zeros_like(token)
        x, y, c = lax.axis_index("x"), lax.axis_index("y"), lax.axis_index("c")
        me, sibling = (x, y, c), (x, y, 1 - c)
        chips = [(1 - x, y), (x, 1 - y), (1 - x, 1 - y)]

        def rows(px, py, pc):
            return out_ref.at[4 * px + 2 * py + pc]

        def copy(k, blk, to, src=None):
            return pltpu.make_async_remote_copy(
                src_ref=rows(*blk) if src is None else src, dst_ref=rows(*blk), send_sem=send_sems.at[k],
                recv_sem=recv_sems.at[k], device_id=to, device_id_type=MESH)

        mine = pltpu.make_async_copy(x_ref, rows(*me), local_sem)
        mine.start()
        first = [copy(0, me, sibling, src=x_ref)]
        first += [copy(1 + j, me, (*chip, c), src=x_ref) for j, chip in enumerate(chips)]
        for cp in first:
            cp.start()
        passed = [copy(4 + j, (*chip, c), sibling) for j, chip in enumerate(chips)]
        for j, chip in enumerate(chips):
            copy(1 + j, (*chip, c), me).wait_recv()
            passed[j].start()
        copy(0, sibling, me).wait_recv()
        for j, chip in enumerate(chips):
            copy(4 + j, (*chip, 1 - c), me).wait_recv()
        for cp in first + passed:
            cp.wait_send()
        mine.wait()

    return pl.pallas_call(
        body, name=name, in_specs=[ANY], out_specs=[ANY, pl.BlockSpec(memory_space=pltpu.VMEM)],
        out_shape=[jax.ShapeDtypeStruct((N_DEV, r, c_dim), block.dtype), jax.ShapeDtypeStruct((8, 128), F32)],
        scratch_shapes=[pltpu.SemaphoreType.DMA((7,)), pltpu.SemaphoreType.DMA((7,)), pltpu.SemaphoreType.DMA],
    )(block)


HBM = pl.BlockSpec(memory_space=pltpu.HBM)
SEM = pl.BlockSpec(memory_space=pltpu.SEMAPHORE)
EFFECT = pltpu.SideEffectType.DATAFLOW_SIDE_EFFECTING


def _exchange_copies(scatter, src_refs, land_refs, send_sems, recv_sems, local_sems):
    x, y, c = lax.axis_index("x"), lax.axis_index("y"), lax.axis_index("c")
    me = 4 * x + 2 * y + c
    sends, arrivals, locals_ = [], [], []
    for a, (src, land) in enumerate(zip(src_refs, land_refs)):
        def pick(idx, src=src):
            return src.at[idx] if scatter else src

        locals_.append(pltpu.make_async_copy(pick(me), land.at[me], local_sems.at[a]))
        for r in range(1, N_DEV):
            px = 1 - x if r & 4 else x
            py = 1 - y if r & 2 else y
            pc = 1 - c if r & 1 else c
            peer, s = 4 * px + 2 * py + pc, 7 * a + r - 1
            sends.append(pltpu.make_async_remote_copy(
                src_ref=pick(peer), dst_ref=land.at[me], send_sem=send_sems.at[s], recv_sem=recv_sems.at[s],
                device_id=(px, py, pc), device_id_type=MESH))
            arrivals.append(pltpu.make_async_remote_copy(
                src_ref=pick(peer), dst_ref=land.at[peer], send_sem=send_sems.at[s], recv_sem=recv_sems.at[s],
                device_id=(px, py, pc), device_id_type=MESH))
    return sends, arrivals, locals_


def _exchange_start(srcs, scatter, name):
    n = len(srcs)
    lands = [lax.empty((N_DEV,) + s.shape[-2:], s.dtype) for s in srcs]

    def body(*refs):
        src_refs, land_refs = refs[:n], refs[n:2 * n]
        send_sems, recv_sems, local_sems = refs[2 * n:2 * n + 3]
        token = refs[-1]
        sends, _, locals_ = _exchange_copies(scatter, src_refs, land_refs, send_sems, recv_sems, local_sems)
        for cp in locals_ + sends:
            cp.start()
        token[...] = jnp.zeros_like(token)

    res = pl.pallas_call(
        body, name=name,
        out_shape=[pltpu.SemaphoreType.DMA((7 * n,)), pltpu.SemaphoreType.DMA((7 * n,)), pltpu.SemaphoreType.DMA((n,))]
        + [pltpu.HBM(a.shape, a.dtype) for a in list(srcs) + lands] + [jax.ShapeDtypeStruct((8, 128), F32)],
        in_specs=[HBM] * (2 * n), out_specs=[SEM] * 3 + [HBM] * (2 * n) + [pl.BlockSpec(memory_space=pltpu.VMEM)],
        input_output_aliases={i: 3 + i for i in range(2 * n)},
        compiler_params=pltpu.CompilerParams(has_side_effects=EFFECT),
    )(*[pltpu.with_memory_space_constraint(a, pltpu.HBM) for a in list(srcs) + lands])
    return (n, scatter, res[:3], res[3:3 + 2 * n]), res[-1]


def _exchange_wait(handle, after, name):
    n, scatter, sems, thru = handle

    def body(*refs):
        src_refs, land_refs = refs[:n], refs[n:2 * n]
        send_sems, recv_sems, local_sems = refs[2 * n:2 * n + 3]
        sends, arrivals, locals_ = _exchange_copies(scatter, src_refs, land_refs, send_sems, recv_sems, local_sems)
        for cp in arrivals:
            cp.wait_recv()
        for cp in sends:
            cp.wait_send()
        for cp in locals_:
            cp.wait()

    res = pl.pallas_call(
        body, name=name, out_shape=[pltpu.HBM(a.shape, a.dtype) for a in thru],
        in_specs=[HBM] * (2 * n) + [SEM] * 3 + [ANY], out_specs=[HBM] * (2 * n),
        input_output_aliases={i: i for i in range(2 * n)},
        compiler_params=pltpu.CompilerParams(has_side_effects=EFFECT),
    )(*thru, *sems, after)
    return res[n:]


def _behind(tokens, a):
    zero = sum(tok[0, 0] for tok in tokens)
    return jax.tree.map(lambda v: v + zero.astype(v.dtype), a)


def _perm_heads(a, perm, axis):
    idx = [slice(None)] * a.ndim
    parts = []
    for h in perm:
        idx[axis] = slice(64 * h, 64 * (h + 1))
        parts.append(a[tuple(idx)])
    idx[axis] = slice(512, None)
    if a.shape[axis] > 512:
        parts.append(a[tuple(idx)])
    return jnp.concatenate(parts, axis=axis)


Q_INV = tuple(int(i) for i in np.argsort(Q_PERM))


def _in0_to_kernel(a, axis):
    a = _perm_heads(a, Q_PERM, axis)
    idx = [slice(None)] * a.ndim

    def cut(lo, hi):
        idx[axis] = slice(lo, hi)
        return a[tuple(idx)]

    return jnp.concatenate([cut(0, 512), cut(768, 1792), cut(512, 768)], axis=axis)


def _in0_from_kernel(a, axis):
    idx = [slice(None)] * a.ndim

    def cut(lo, hi):
        idx[axis] = slice(lo, hi)
        return a[tuple(idx)]

    a = jnp.concatenate([cut(0, 512), cut(1536, 1792), cut(512, 1536)], axis=axis)
    return _perm_heads(a, Q_INV, axis)


def _f32_as_u16_rows(v, rows):
    bits = lax.bitcast_convert_type(v, jnp.uint16).reshape(-1)
    return jnp.pad(bits, (0, rows * D - bits.shape[0])).reshape(rows, D)


def _pad_rows(v, rows):
    v = v.reshape(-1)
    return jnp.pad(v, (0, rows * D - v.shape[0])).reshape(rows, D)


def kernel(x, mix_norm, a_w_in, a_b_in, a_sinks, a_conv_w, a_conv_b, a_cln_g, a_cln_b, a_w_out, c_w_in, c_w_pool, c_pool_scale, c_sln_g, c_sln_b, c_w_s, c_b_s, c_w_out, ffn_norm, ffn_w_gate, ffn_w_up, ffn_w_down, final_norm, loss_target, m_mix_norm, m_a_w_in, m_a_b_in, m_a_sinks, m_a_conv_w, m_a_conv_b, m_a_cln_g, m_a_cln_b, m_a_w_out, m_c_w_in, m_c_w_pool, m_c_pool_scale, m_c_sln_g, m_c_sln_b, m_c_w_s, m_c_b_s, m_c_w_out, m_ffn_norm, m_ffn_w_gate, m_ffn_w_up, m_ffn_w_down, m_final_norm, v_mix_norm, v_a_w_in, v_a_b_in, v_a_sinks, v_a_conv_w, v_a_conv_b, v_a_cln_g, v_a_cln_b, v_a_w_out, v_c_w_in, v_c_w_pool, v_c_pool_scale, v_c_sln_g, v_c_sln_b, v_c_w_s, v_c_b_s, v_c_w_out, v_ffn_norm, v_ffn_w_gate, v_ffn_w_up, v_ffn_w_down, v_final_norm):
    bsz, seq, _ = x.shape
    t = bsz * seq
    w_in = dict(mix_norm=mix_norm, a_w_in=a_w_in, a_b_in=a_b_in, a_sinks=a_sinks, a_conv_w=a_conv_w, a_conv_b=a_conv_b,
                a_cln_g=a_cln_g, a_cln_b=a_cln_b, a_w_out=a_w_out, c_w_in=c_w_in, c_w_pool=c_w_pool,
                c_pool_scale=c_pool_scale, c_sln_g=c_sln_g, c_sln_b=c_sln_b, c_w_s=c_w_s, c_b_s=c_b_s, c_w_out=c_w_out,
                ffn_norm=ffn_norm, ffn_w_gate=ffn_w_gate, ffn_w_up=ffn_w_up, ffn_w_down=ffn_w_down, final_norm=final_norm)
    m_in = dict(mix_norm=m_mix_norm, a_w_in=m_a_w_in, a_b_in=m_a_b_in, a_sinks=m_a_sinks, a_conv_w=m_a_conv_w,
                a_conv_b=m_a_conv_b, a_cln_g=m_a_cln_g, a_cln_b=m_a_cln_b, a_w_out=m_a_w_out, c_w_in=m_c_w_in,
                c_w_pool=m_c_w_pool, c_pool_scale=m_c_pool_scale, c_sln_g=m_c_sln_g, c_sln_b=m_c_sln_b, c_w_s=m_c_w_s,
                c_b_s=m_c_b_s, c_w_out=m_c_w_out, ffn_norm=m_ffn_norm, ffn_w_gate=m_ffn_w_gate, ffn_w_up=m_ffn_w_up,
                ffn_w_down=m_ffn_w_down, final_norm=m_final_norm)
    v_in = dict(mix_norm=v_mix_norm, a_w_in=v_a_w_in, a_b_in=v_a_b_in, a_sinks=v_a_sinks, a_conv_w=v_a_conv_w,
                a_conv_b=v_a_conv_b, a_cln_g=v_a_cln_g, a_cln_b=v_a_cln_b, a_w_out=v_a_w_out, c_w_in=v_c_w_in,
                c_w_pool=v_c_w_pool, c_pool_scale=v_c_pool_scale, c_sln_g=v_c_sln_g, c_sln_b=v_c_sln_b, c_w_s=v_c_w_s,
                c_b_s=v_c_b_s, c_w_out=v_c_w_out, ffn_norm=v_ffn_norm, ffn_w_gate=v_ffn_w_gate, ffn_w_up=v_ffn_w_up,
                ffn_w_down=v_ffn_w_down, final_norm=v_final_norm)

    small = jnp.concatenate([a_conv_w[0].reshape(-1), c_pool_scale[0], c_sln_g[0], c_sln_b[0]])
    first_bits = lax.bitcast_convert_type(jnp.concatenate([a_w_in[0].T, a_w_out[0]], axis=0).astype(BF16), jnp.uint16)
    gathered, tok = _all_gather(jnp.concatenate([first_bits, _f32_as_u16_rows(small, W_MISC_ROWS)], axis=0), "gather_mixer0")

    def ffn_shards(l):
        return [ffn_w_gate[l].T.astype(BF16), ffn_w_up[l].T.astype(BF16), ffn_w_down[l].astype(BF16)]

    ffn0_h, tok = _exchange_start(_behind([tok], ffn_shards(0)), False, "gather_ffn0_start")
    mix1_h, tok = _exchange_start(_behind([tok], [c_w_in[0].T.astype(BF16), c_w_out[0].astype(BF16)]), False,
                                  "gather_mixer1_start")
    ffn1_h, tok = _exchange_start(_behind([tok], ffn_shards(1)), False, "gather_ffn1_start")

    a_in_full = lax.bitcast_convert_type(gathered[:, :224].reshape(IN0, D), BF16)
    a_out_full = lax.bitcast_convert_type(gathered[:, 224:352].reshape(D, D), BF16)
    small_all = lax.bitcast_convert_type(
        gathered[:, 352:].reshape(N_DEV, -1)[:, :2 * SMALL_SHARD].reshape(N_DEV, SMALL_SHARD, 2), F32)
    conv_w = small_all[:, :31 * 64].reshape(N_DEV, 31, 64).transpose(1, 0, 2).reshape(31, 512)
    conv_w = jnp.pad(conv_w, ((0, HALO - CONV_K), (0, 0)))
    pool_scale = small_all[:, 31 * 64:31 * 64 + 64].reshape(1, 512)
    sln_g = small_all[:, 31 * 64 + 64:31 * 64 + 128].reshape(1, 512)
    sln_b = small_all[:, 31 * 64 + 128:].reshape(1, 512)

    wt_in0 = _in0_to_kernel(a_in_full, 0)
    b_in0 = _in0_to_kernel(a_b_in, 1)
    w_out0 = _perm_heads(a_out_full, Q_PERM, 0)
    b_rows = jnp.broadcast_to(c_b_s[0][:, :, None], (4, 128, 128))
    conv_b, cln_g, cln_b = a_conv_b, a_cln_g, a_cln_b

    h0 = x.reshape(t, D)
    target = loss_target.reshape(t, D)
    z0, hn0 = _norm_proj(h0, _behind([tok], mix_norm[0:1]), wt_in0, b_in0, "in_proj0")
    attn = _attn_fwd(z0, a_sinks, bsz, "attn_fwd")
    conv, conv_y = _conv_fwd(z0, conv_w, conv_b, cln_g, cln_b, bsz, "conv_fwd")
    h1 = _out_proj(attn, conv, w_out0, h0, "out_proj0")
    wtg0, wtu0, wd0 = (w.reshape(D_FF, D) for w in _exchange_wait(ffn0_h, h1, "gather_ffn0_wait"))
    h2, hnf0, gate0, up0 = _ffn_fwd(h1, ffn_norm[0:1], wtg0, wtu0, wd0, "ffn_fwd0")
    wt_in1, w_out1 = (w.reshape(-1, D) for w in _exchange_wait(mix1_h, h2, "gather_mixer1_wait"))
    z1, hn1 = _norm_proj(h2, mix_norm[1:2], wt_in1, None, "in_proj1")
    pool = _pool_fwd(z1, c_w_pool[0], pool_scale, bsz, "pool_fwd")
    sgu = _sgu_fwd(z1, sln_g, sln_b, c_w_s[0], b_rows, "sgu_fwd")
    h3 = _out_proj(pool, sgu, w_out1, h2, "out_proj1")
    wtg1, wtu1, wd1 = (w.reshape(D_FF, D) for w in _exchange_wait(ffn1_h, h3, "gather_ffn1_wait"))
    h4, hnf1, gate1, up1 = _ffn_fwd(h3, ffn_norm[1:2], wtg1, wtu1, wd1, "ffn_fwd1")

    def blocks(g):
        return g.reshape(N_DEV, g.shape[0] // N_DEV, D)

    dh4, d_final_norm, loss_part = _loss_head(h4, final_norm.reshape(1, D), target, "loss_head")
    dh3, dgate1, dup1, act1, d_fn1 = _ffn_bwd(dh4, h3, ffn_norm[1:2], gate1, up1, wtg1, wtu1, wd1, "ffn_bwd1")
    gw_ffn1 = [_mm_tn(dgate1, hnf1, "dw_gate1"), _mm_tn(dup1, hnf1, "dw_up1"), _mm_tn(act1, dh4, "dw_down1")]
    ffn1_g, tok = _exchange_start([blocks(g) for g in gw_ffn1], True, "scatter_ffn1_start")
    dmix1 = _dmix(dh3, _behind([tok], w_out1), "dmix1")
    gw_c_out = _mm_tn_pieces([pool, sgu], dh3, "dw_out1")
    dzp, d_w_pool, d_pool_scale = _pool_bwd(z1, dmix1, c_w_pool[0], pool_scale, bsz, "pool_bwd")
    dzu, dzv, d_w_s, d_b_s, d_sln_g, d_sln_b = _sgu_bwd(z1, dmix1, sln_g, sln_b, c_w_s[0], b_rows, "sgu_bwd")
    dh2, d_mn1 = _proj_bwd_norm([(dzp, 0), (dzu, 512), (dzv, 1024)], wt_in1, h2, dh3, mix_norm[1:2], "in_proj1_bwd")
    gw_c_in = _mm_tn_pieces([dzp, dzu, dzv], hn1, "dw_in1")
    mix1_g, tok = _exchange_start([blocks(gw_c_in), blocks(gw_c_out)], True, "scatter_mixer1_start")
    dh1, dgate0, dup0, act0, d_fn0 = _ffn_bwd(dh2, h1, _behind([tok], ffn_norm[0:1]), gate0, up0, wtg0, wtu0, wd0, "ffn_bwd0")
    gw_ffn0 = [_mm_tn(dgate0, hnf0, "dw_gate0"), _mm_tn(dup0, hnf0, "dw_up0"), _mm_tn(act0, dh2, "dw_down0")]
    ffn0_g, tok = _exchange_start([blocks(g) for g in gw_ffn0], True, "scatter_ffn0_start")
    dmix0 = _dmix(dh1, _behind([tok], w_out0), "dmix0")
    gw_a_out = _perm_heads(_mm_tn_pieces([attn, conv], dh1, "dw_out0"), Q_INV, 0)
    dq, dkv, d_sink_row, d_bq, d_bkv = _attn_bwd(z0, dmix0, a_sinks, bsz, "attn_bwd")
    dca, dcg, d_conv_w, d_conv_b, d_cln_g, d_cln_b, d_ba, d_bg = _conv_bwd(z0, conv_y, dmix0, conv_w, cln_g, cln_b, bsz, "conv_bwd")
    gw_a_in = _in0_from_kernel(_mm_tn_pieces([dq, dca, dcg, dkv], hn0, "dw_in0"), 0)
    mix0_g, tok = _exchange_start([blocks(gw_a_in), blocks(gw_a_out)], True, "scatter_mixer0_start")
    dx, d_mn0 = _proj_bwd_norm([(dq, 0), (dca, 512), (dcg, 1024), (dkv, 1536)], wt_in0, h0, dh1,
                               _behind([tok], mix_norm[0:1]), "in_proj0_bwd")
    d_b_in = _in0_from_kernel(jnp.concatenate([d_bq, d_ba, d_bg, d_bkv], axis=1), 1)

    rep = dict(mix_norm=jnp.concatenate([d_mn0, d_mn1], axis=0), a_b_in=d_b_in, a_sinks=d_sink_row[:, :8],
               a_conv_b=d_conv_b, a_cln_g=d_cln_g, a_cln_b=d_cln_b, c_w_pool=d_w_pool[None], c_w_s=d_w_s[None],
               c_b_s=d_b_s[None], ffn_norm=jnp.concatenate([d_fn0, d_fn1], axis=0), final_norm=d_final_norm.reshape(D))
    rep_flat = jnp.concatenate([rep[nm].reshape(-1) for nm in REP_NAMES] + [loss_part.reshape(1)])
    rep_flat = jnp.pad(rep_flat, (0, N_DEV * REP_ROWS * D - rep_flat.shape[0])).reshape(N_DEV, REP_ROWS, D)
    small_g = jnp.concatenate([
        d_conv_w[:CONV_K].reshape(31, N_DEV, 64).transpose(1, 0, 2).reshape(N_DEV, 31 * 64),
        d_pool_scale.reshape(N_DEV, 64), d_sln_g.reshape(N_DEV, 64), d_sln_b.reshape(N_DEV, 64)], axis=1)
    small_g = jnp.pad(small_g, ((0, 0), (0, G_SMALL_ROWS * D - SMALL_SHARD))).reshape(N_DEV, G_SMALL_ROWS, D)
    tail_g, tok = _exchange_start([jnp.concatenate([small_g, rep_flat], axis=1)], True, "scatter_tail_start")

    names = list(w_in)
    g_out, delta, new_m, new_v = {}, {}, {}, {}
    column_sharded = ("a_w_in", "c_w_in", "ffn_w_gate", "ffn_w_up")

    def rows_of(a, nm):
        return jnp.swapaxes(a, 1, 2) if nm in column_sharded else a

    def reduce_adamw(nm, landing, layer, into=None):
        args = [rows_of(d[nm], nm) for d in (w_in, m_in, v_in)]
        if args[0].shape[0] == 1:
            args, layer = [a[0] for a in args], None
        return _reduce_adamw(landing, *args, "adamw_%s_%s" % (nm, layer), layer=layer, into=into)

    def keep(nm, res):
        res = [r if r.ndim == 3 else r[None] for r in res]
        g_out[nm], delta[nm], new_m[nm], new_v[nm] = (rows_of(r, nm) for r in res)

    ffn_names = ("ffn_w_gate", "ffn_w_up", "ffn_w_down")
    landed = _exchange_wait(ffn1_g, tok, "scatter_ffn1_wait")
    ffn_res = [reduce_adamw(nm, a, 1) for nm, a in zip(ffn_names, landed)]
    landed = _exchange_wait(mix1_g, ffn_res[-1][0], "scatter_mixer1_wait")
    for nm, a in zip(("c_w_in", "c_w_out"), landed):
        keep(nm, reduce_adamw(nm, a, 0))
    landed = _exchange_wait(ffn0_g, g_out["c_w_out"], "scatter_ffn0_wait")
    for nm, a, res in zip(ffn_names, landed, ffn_res):
        keep(nm, reduce_adamw(nm, a, 0, into=res))
    landed = _exchange_wait(mix0_g, g_out["ffn_w_down"], "scatter_mixer0_wait")
    for nm, a in zip(("a_w_in", "a_w_out"), landed):
        keep(nm, reduce_adamw(nm, a, 0))
    g_tail = _sum_slabs(_exchange_wait(tail_g, g_out["a_w_out"], "scatter_tail_wait")[0], "sum_tail")
    rep_all = _all_gather(g_tail[G_SMALL_ROWS:], "gather_replicated_grads")[0].reshape(-1)
    small_r = g_tail[:G_SMALL_ROWS].reshape(-1)[:SMALL_SHARD]
    g_out.update(
        a_conv_w=small_r[:31 * 64].reshape(1, 31, 64), c_pool_scale=small_r[31 * 64:31 * 64 + 64].reshape(1, 64),
        c_sln_g=small_r[31 * 64 + 64:31 * 64 + 128].reshape(1, 64), c_sln_b=small_r[31 * 64 + 128:].reshape(1, 64))
    off = 0
    for nm in REP_NAMES:
        n = int(np.prod(w_in[nm].shape))
        g_out[nm] = rep_all[off:off + n].reshape(w_in[nm].shape)
        off += n
    loss = rep_all[off]
    for group, rows, label in ((("a_conv_w", "c_pool_scale", "c_sln_g", "c_sln_b"), G_SMALL_ROWS, "adamw_small_sharded"),
                               (REP_NAMES, N_DEV * REP_ROWS, "adamw_replicated")):
        flat = [_pad_rows(jnp.concatenate([d[nm].reshape(-1) for nm in group]), rows) for d in (w_in, g_out, m_in, v_in)]
        res = [r.reshape(-1) for r in _adamw(*flat, label)]
        off = 0
        for nm in group:
            n = int(np.prod(w_in[nm].shape))
            delta[nm], new_m[nm], new_v[nm] = (r[off:off + n].reshape(w_in[nm].shape) for r in res)
            off += n

    grad_x = dx.reshape(bsz, seq, D)
    return (loss, grad_x, *[g_out[nm] for nm in names], *[delta[nm] for nm in names],
            *[new_m[nm] for nm in names], *[new_v[nm] for nm in names])
```

```python
import functools

import jax
import jax.numpy as jnp
import numpy as np
from jax import lax
from jax.experimental import pallas as pl
from jax.experimental.pallas import tpu as pltpu

F32 = jnp.float32
BF16 = jnp.bfloat16
MESH = pl.DeviceIdType.MESH

D = 1024
N_DEV = 8
EPS = 1e-5
HEAD_PAIRS = 4
ATT_BLK = 128
CONV_K = 31
HALO = 32
D_FF = 2816
FF_TILE_FWD = D_FF // 2
FF_TILE_BWD = D_FF // 2
IN0 = 1792
IN1 = 1536
POOL_WINDOWS = (2, 4, 8, 16)
SGU_CHUNK = 128
GELU_C = 0.7978845608028654
GELU_A = 0.044715
ADAM_LR, ADAM_B1, ADAM_B2, ADAM_EPS, ADAM_WD, ADAM_STEP = 0.001, 0.9, 0.999, 1e-08, 0.01, 10
VMEM_LIMIT = 56 << 20

SMALL_SHARD = 31 * 64 + 3 * 64
W_MISC_ROWS = 16
G_MISC_ROWS = 32
G_SMALL_ROWS = 8
REP_ROWS = G_MISC_ROWS - G_SMALL_ROWS
REP_NAMES = ("mix_norm", "a_b_in", "a_sinks", "a_conv_b", "a_cln_g", "a_cln_b", "c_w_pool", "c_w_s", "c_b_s",
             "ffn_norm", "final_norm")
Q_PERM = (0, 4, 1, 5, 2, 6, 3, 7)


def _params(*sem):
    return pltpu.CompilerParams(dimension_semantics=sem, vmem_limit_bytes=VMEM_LIMIT)


def _nn(a, b):
    return jnp.dot(a, b, preferred_element_type=F32)


def _nt(a, b):
    return lax.dot_general(a, b, (((1,), (1,)), ((), ())), preferred_element_type=F32)


def _tn(a, b):
    return lax.dot_general(a, b, (((0,), (0,)), ((), ())), preferred_element_type=F32)


def _tile(n, want=512):
    t = min(want, n)
    assert n % t == 0, (n, t)
    return t


def _seq_tile(s):
    return 512 if s >= 1024 else s // 2


def _rms(x, g):
    r = lax.rsqrt(jnp.mean(x * x, axis=-1, keepdims=True) + EPS)
    return x * r * g, r


def _rms_bwd(x, g, d_y):
    r = lax.rsqrt(jnp.mean(x * x, axis=-1, keepdims=True) + EPS)
    xr = x * r
    u = d_y * g
    d_x = r * (u - xr * jnp.mean(u * xr, axis=-1, keepdims=True))
    return d_x, jnp.sum(d_y * xr, axis=0, keepdims=True)


def _ln(y, g, b):
    mu = jnp.mean(y, axis=-1, keepdims=True)
    yc = y - mu
    rstd = lax.rsqrt(jnp.mean(yc * yc, axis=-1, keepdims=True) + EPS)
    xhat = yc * rstd
    return xhat * g + b, xhat, rstd


def _ln_bwd(d_o, xhat, rstd, g):
    dxh = d_o * g
    return rstd * (dxh - jnp.mean(dxh, axis=-1, keepdims=True) - xhat * jnp.mean(dxh * xhat, axis=-1, keepdims=True))


def _gelu(x):
    th = jnp.tanh(GELU_C * (x + GELU_A * x * x * x))
    return 0.5 * x * (1.0 + th), th


def _gelu_grad(x, th):
    return 0.5 * (1.0 + th) + 0.5 * x * (1.0 - th * th) * GELU_C * (1.0 + 3.0 * GELU_A * x * x)


def _row(c):
    return pl.BlockSpec((1, c), lambda *_: (0, 0))


def _full(shape):
    return pl.BlockSpec(shape, lambda *_: (0,) * len(shape))


def _norm_proj(h, g, wt, bias, name):
    t, n = h.shape[0], wt.shape[0]
    tm = _tile(t)
    has_bias = bias is not None

    def body(*refs):
        h_ref, g_ref, wt_ref = refs[:3]
        z_ref, hn_ref = refs[-2:]
        hn = _rms(h_ref[...].astype(F32), g_ref[...])[0].astype(BF16)
        hn_ref[...] = hn
        z = _nt(hn, wt_ref[...])
        if has_bias:
            z = z + refs[3][...]
        z_ref[...] = z.astype(BF16)

    in_specs = [pl.BlockSpec((tm, D), lambda i: (i, 0)), _row(D), _full((n, D))]
    args = [h, g, wt]
    if has_bias:
        in_specs.append(_row(n))
        args.append(bias)
    return pl.pallas_call(
        body, name=name, grid=(t // tm,), in_specs=in_specs,
        out_specs=[pl.BlockSpec((tm, n), lambda i: (i, 0)), pl.BlockSpec((tm, D), lambda i: (i, 0))],
        out_shape=[jax.ShapeDtypeStruct((t, n), BF16), jax.ShapeDtypeStruct((t, D), BF16)],
        compiler_params=_params("parallel"))(*args)


def _out_proj(a, b, w, h, name):
    t = h.shape[0]
    tm = _tile(t)

    def body(a_ref, b_ref, wa_ref, wb_ref, h_ref, o_ref):
        o_ref[...] = (h_ref[...].astype(F32) + _nn(a_ref[...], wa_ref[...]) + _nn(b_ref[...], wb_ref[...])).astype(BF16)

    half = pl.BlockSpec((tm, 512), lambda i: (i, 0))
    return pl.pallas_call(
        body, name=name, grid=(t // tm,),
        in_specs=[half, half, pl.BlockSpec((512, D), lambda i: (0, 0)), pl.BlockSpec((512, D), lambda i: (1, 0)),
                  pl.BlockSpec((tm, D), lambda i: (i, 0))],
        out_specs=pl.BlockSpec((tm, D), lambda i: (i, 0)), out_shape=jax.ShapeDtypeStruct((t, D), BF16),
        compiler_params=_params("parallel"))(a, b, w, w, h)


def _dmix(dh, w, name):
    t = dh.shape[0]
    tm = _tile(t)

    def body(dh_ref, w_ref, o_ref):
        o_ref[...] = _nt(dh_ref[...].astype(BF16), w_ref[...]).astype(BF16)

    return pl.pallas_call(
        body, name=name, grid=(t // tm,), in_specs=[pl.BlockSpec((tm, D), lambda i: (i, 0)), _full((D, D))],
        out_specs=pl.BlockSpec((tm, D), lambda i: (i, 0)), out_shape=jax.ShapeDtypeStruct((t, D), BF16),
        compiler_params=_params("parallel"))(dh, w)


def _ffn_fwd(h, g, wtg, wtu, wd, name):
    t = h.shape[0]
    tm, tf = _tile(t), FF_TILE_FWD
    nf = D_FF // tf

    def body(h_ref, g_ref, wtg_ref, wtu_ref, wd_ref, o_ref, hn_ref, gate_ref, up_ref, acc):
        f = pl.program_id(1)

        @pl.when(f == 0)
        def _():
            x = h_ref[...].astype(F32)
            hn_ref[...] = _rms(x, g_ref[...])[0].astype(BF16)
            acc[...] = x

        hn = hn_ref[...]
        gate = _nt(hn, wtg_ref[...])
        up = _nt(hn, wtu_ref[...])
        gate_ref[...] = gate.astype(BF16)
        up_ref[...] = up.astype(BF16)
        act = (gate * jax.nn.sigmoid(gate) * up).astype(BF16)
        acc[...] += _nn(act, wd_ref[...])

        @pl.when(f == nf - 1)
        def _():
            o_ref[...] = acc[...].astype(BF16)

    tok = pl.BlockSpec((tm, D), lambda i, f: (i, 0))
    wsp = pl.BlockSpec((tf, D), lambda i, f: (f, 0))
    mid = pl.BlockSpec((tm, tf), lambda i, f: (i, f))
    return pl.pallas_call(
        body, name=name, grid=(t // tm, nf), in_specs=[tok, _row(D), wsp, wsp, wsp],
        out_specs=[tok, tok, mid, mid],
        out_shape=[jax.ShapeDtypeStruct((t, D), BF16), jax.ShapeDtypeStruct((t, D), BF16),
                   jax.ShapeDtypeStruct((t, D_FF), BF16), jax.ShapeDtypeStruct((t, D_FF), BF16)],
        scratch_shapes=[pltpu.VMEM((tm, D), F32)],
        compiler_params=_params("parallel", "arbitrary"))(h, g, wtg, wtu, wd)


def _ffn_bwd(dh, h, g, gate, up, wtg, wtu, wd, name):
    t = h.shape[0]
    tm, tf = _tile(t), FF_TILE_BWD
    nf = D_FF // tf

    def body(dh_ref, h_ref, g_ref, gate_ref, up_ref, wtg_ref, wtu_ref, wd_ref,
             dhin_ref, dgate_ref, dup_ref, act_ref, dg_ref, d_hn, dact):
        i, f = pl.program_id(0), pl.program_id(1)

        @pl.when(f == 0)
        def _():
            d_hn[...] = jnp.zeros_like(d_hn)

        @pl.when((i == 0) & (f == 0))
        def _():
            dg_ref[...] = jnp.zeros_like(dg_ref)

        dact[...] = _nt(dh_ref[...], wd_ref[...])
        for c0 in range(0, tf, 128):
            cols = slice(c0, c0 + 128)
            gt = gate_ref[:, cols].astype(F32)
            u = up_ref[:, cols].astype(F32)
            da = dact[:, cols]
            sg = jax.nn.sigmoid(gt)
            sil = gt * sg
            act_ref[:, cols] = (sil * u).astype(BF16)
            dup_ref[:, cols] = (da * sil).astype(BF16)
            dgate_ref[:, cols] = (da * u * sg * (1.0 + gt * (1.0 - sg))).astype(BF16)
        d_hn[...] += _nn(dgate_ref[...], wtg_ref[...]) + _nn(dup_ref[...], wtu_ref[...])

        @pl.when(f == nf - 1)
        def _():
            d_x, d_g = _rms_bwd(h_ref[...].astype(F32), g_ref[...], d_hn[...])
            dhin_ref[...] = (dh_ref[...].astype(F32) + d_x).astype(BF16)
            dg_ref[...] += d_g

    tok = pl.BlockSpec((tm, D), lambda i, f: (i, 0))
    wsp = pl.BlockSpec((tf, D), lambda i, f: (f, 0))
    mid = pl.BlockSpec((tm, tf), lambda i, f: (i, f))
    mid_shape = jax.ShapeDtypeStruct((t, D_FF), BF16)
    return pl.pallas_call(
        body, name=name, grid=(t // tm, nf), in_specs=[tok, tok, _row(D), mid, mid, wsp, wsp, wsp],
        out_specs=[tok, mid, mid, mid, _row(D)],
        out_shape=[jax.ShapeDtypeStruct((t, D), BF16), mid_shape, mid_shape, mid_shape, jax.ShapeDtypeStruct((1, D), F32)],
        scratch_shapes=[pltpu.VMEM((tm, D), F32), pltpu.VMEM((tm, tf), F32)],
        compiler_params=_params("arbitrary", "arbitrary"))(dh, h, g, gate, up, wtg, wtu, wd)


def _proj_bwd_norm(pieces, wt, h, dh, g, dtype, name):
    t = h.shape[0]
    tm = _tile(t)
    n_p = len(pieces)

    def body(*refs):
        p_refs, w_refs = refs[:n_p], refs[n_p:2 * n_p]
        h_ref, dh_ref, g_ref, o_ref, dg_ref = refs[2 * n_p:]

        @pl.when(pl.program_id(0) == 0)
        def _():
            dg_ref[...] = jnp.zeros_like(dg_ref)

        d_hn = _nn(p_refs[0][...], w_refs[0][...])
        for p_ref, w_ref in zip(p_refs[1:], w_refs[1:]):
            d_hn = d_hn + _nn(p_ref[...], w_ref[...])
        d_x, d_g = _rms_bwd(h_ref[...].astype(F32), g_ref[...], d_hn)
        o_ref[...] = (dh_ref[...].astype(F32) + d_x).astype(dtype)
        dg_ref[...] += d_g

    tok = pl.BlockSpec((tm, D), lambda i: (i, 0))
    in_specs = [pl.BlockSpec((tm, a.shape[1]), lambda i: (i, 0)) for a, _ in pieces]
    for a, off in pieces:
        w = a.shape[1]
        assert off % w == 0
        in_specs.append(pl.BlockSpec((w, D), functools.partial(lambda i, blk: (blk, 0), blk=off // w)))
    in_specs += [tok, tok, _row(D)]
    return pl.pallas_call(
        body, name=name, grid=(t // tm,), in_specs=in_specs, out_specs=[tok, _row(D)],
        out_shape=[jax.ShapeDtypeStruct((t, D), dtype), jax.ShapeDtypeStruct((1, D), F32)],
        compiler_params=_params("arbitrary"))(*[a for a, _ in pieces], *([wt] * n_p), h, dh, g)


def _mm_tn(a, b, name):
    t, n = a.shape
    k = b.shape[1]
    tn = n if n <= 1024 else n // 2
    tt = _tile(t, 1024)
    nt = t // tt

    def body(a_ref, b_ref, o_ref, acc):
        s = pl.program_id(1)

        @pl.when(s == 0)
        def _():
            acc[...] = jnp.zeros_like(acc)

        acc[...] += _tn(a_ref[...], b_ref[...].astype(BF16))

        @pl.when(s == nt - 1)
        def _():
            o_ref[...] = acc[...].astype(BF16)

    return pl.pallas_call(
        body, name=name, grid=(n // tn, nt),
        in_specs=[pl.BlockSpec((tt, tn), lambda j, s: (s, j)), pl.BlockSpec((tt, k), lambda j, s: (s, 0))],
        out_specs=pl.BlockSpec((tn, k), lambda j, s: (j, 0)), out_shape=jax.ShapeDtypeStruct((n, k), BF16),
        scratch_shapes=[pltpu.VMEM((tn, k), F32)],
        compiler_params=_params("parallel", "arbitrary"))(a, b)


def _mm_tn_pieces(pieces, b, name):
    t, k = b.shape
    widths = [p.shape[1] for p in pieces]
    n, n_p = sum(widths), len(pieces)
    tt = _tile(t, 1024)
    nt = t // tt

    def body(*refs):
        b_ref, o_ref, acc = refs[n_p:]
        s = pl.program_id(0)

        @pl.when(s == 0)
        def _():
            acc[...] = jnp.zeros_like(acc)

        bb = b_ref[...].astype(BF16)
        off = 0
        for p_ref, w in zip(refs[:n_p], widths):
            acc[off:off + w, :] += _tn(p_ref[...], bb)
            off += w

        @pl.when(s == nt - 1)
        def _():
            o_ref[...] = acc[...].astype(BF16)

    return pl.pallas_call(
        body, name=name, grid=(nt,),
        in_specs=[pl.BlockSpec((tt, w), lambda s: (s, 0)) for w in widths] + [pl.BlockSpec((tt, k), lambda s: (s, 0))],
        out_specs=_full((n, k)), out_shape=jax.ShapeDtypeStruct((n, k), BF16),
        scratch_shapes=[pltpu.VMEM((n, k), F32)], compiler_params=_params("arbitrary"))(*pieces, b)


def _loss_head(h, g, target, name):
    t = h.shape[0]
    tm = _tile(t)

    def body(h_ref, g_ref, t_ref, dh_ref, dg_ref, loss_ref):
        @pl.when(pl.program_id(0) == 0)
        def _():
            dg_ref[...] = jnp.zeros_like(dg_ref)
            loss_ref[...] = jnp.zeros_like(loss_ref)

        x = h_ref[...].astype(F32)
        gv = g_ref[...]
        err = _rms(x, gv)[0] - t_ref[...]
        per_tok = jnp.mean(err * err, axis=-1, keepdims=True)
        loss_ref[...] += 0.5 * jnp.sum(per_tok, axis=0, keepdims=True)
        d_x, d_g = _rms_bwd(x, gv, err * (1.0 / D))
        dh_ref[...] = d_x.astype(BF16)
        dg_ref[...] += d_g

    tok = pl.BlockSpec((tm, D), lambda i: (i, 0))
    return pl.pallas_call(
        body, name=name, grid=(t // tm,), in_specs=[tok, _row(D), tok],
        out_specs=[tok, _row(D), _row(1)],
        out_shape=[jax.ShapeDtypeStruct((t, D), BF16), jax.ShapeDtypeStruct((1, D), F32), jax.ShapeDtypeStruct((1, 1), F32)],
        compiler_params=_params("arbitrary"))(h, g, target)


STACK = HEAD_PAIRS * ATT_BLK


def _attn_valid(first, rows):
    qi = lax.broadcasted_iota(jnp.int32, (rows, 2 * ATT_BLK), 0) % ATT_BLK
    r = lax.broadcasted_iota(jnp.int32, (rows, 2 * ATT_BLK), 1)
    dist = qi + ATT_BLK - r
    return (dist >= 0) & (dist < ATT_BLK) & ((r >= ATT_BLK) | jnp.logical_not(first))


def _stacked(ref, kh, scale):
    lo = lax.broadcasted_iota(jnp.int32, (ATT_BLK, 128), 1) < 64
    keep = lo if kh == 0 else ~lo
    parts = [jnp.where(keep, ref[:, g * 128:(g + 1) * 128] * scale, 0.0).astype(BF16) for g in range(HEAD_PAIRS)]
    return jnp.concatenate(parts, axis=0)


def _unstacked(a0, a1, g):
    lo = lax.broadcasted_iota(jnp.int32, (ATT_BLK, 128), 1) < 64
    rows = slice(g * ATT_BLK, (g + 1) * ATT_BLK)
    return jnp.where(lo, a0[rows], a1[rows])


def _sink_rows(s_ref, kh):
    return jnp.concatenate([jnp.full((ATT_BLK, 128), s_ref[0, kh * 4 + g], F32) for g in range(HEAD_PAIRS)], axis=0)


def _row_sums(a):
    hi = a.astype(BF16)
    lo = (a - hi.astype(F32)).astype(BF16)
    ones = jnp.ones((2 * ATT_BLK, 128), BF16)
    return _nn(hi, ones) + _nn(lo, ones)


def _both(a):
    return jnp.concatenate([a, a], axis=1)


def _attn_probs(qs, kpair, sink, valid):
    s = jnp.where(valid, _nt(qs, kpair), -1e30)
    m = jnp.maximum(jnp.broadcast_to(jnp.max(s, axis=-1, keepdims=True), (s.shape[0], 128)), sink)
    p = jnp.exp(s - _both(m))
    es = jnp.exp(sink - m)
    inv = 1.0 / (_row_sums(p) + es)
    return p * _both(inv), es * inv


def _attn_probs_head(qm, kpair, sink, valid):
    s = jnp.where(valid, _nt(qm, kpair), -1e30)
    m = jnp.maximum(jnp.max(s, axis=-1, keepdims=True), sink)
    p = jnp.exp(s - m)
    return p * (1.0 / (jnp.sum(p, axis=-1, keepdims=True) + jnp.exp(sink - m)))


def _attn_specs(bsz, order):
    q = pl.BlockSpec((bsz, ATT_BLK, 512), lambda j: (0, order(j), 0))
    kvc = pl.BlockSpec((bsz, ATT_BLK, 256), lambda j: (0, order(j), 6))
    kvp = pl.BlockSpec((bsz, ATT_BLK, 256), lambda j: (0, jnp.maximum(order(j) - 1, 0), 6))
    return q, kvc, kvp


def _window_kv(kvc_ref, kvp_ref):
    kvc, kvp = kvc_ref[...], kvp_ref[...]
    kpair = jnp.concatenate([kvp[:, :128], kvc[:, :128]], axis=0)
    vpair = jnp.concatenate([kvp[:, 128:], kvc[:, 128:]], axis=0)
    return kpair, vpair


def _attn_fwd(z0, sinks, bsz, name):
    t = z0.shape[0]
    seq = t // bsz
    nb = seq // ATT_BLK

    def body(s_ref, q_ref, kvc_ref, kvp_ref, o_ref):
        valid = _attn_valid(pl.program_id(0) == 0, ATT_BLK)
        lo = lax.broadcasted_iota(jnp.int32, (ATT_BLK, 128), 1) < 64
        for b in range(bsz):
            kpair, vpair = _window_kv(kvc_ref.at[b], kvp_ref.at[b])
            for g in range(HEAD_PAIRS):
                qs = q_ref[b, :, g * 128:(g + 1) * 128] * 0.125
                outs = []
                for kh in range(2):
                    qm = jnp.where(lo if kh == 0 else ~lo, qs, 0.0).astype(BF16)
                    p = _attn_probs_head(qm, kpair, s_ref[0, kh * 4 + g], valid)
                    outs.append(_nn(p.astype(BF16), vpair))
                o_ref[b, :, g * 128:(g + 1) * 128] = jnp.where(lo, outs[0], outs[1]).astype(BF16)

    q, kvc, kvp = _attn_specs(bsz, lambda j: j)
    z3 = z0.reshape(bsz, seq, z0.shape[1])
    return pl.pallas_call(
        body, name=name, grid=(nb,),
        in_specs=[pl.BlockSpec(memory_space=pltpu.SMEM), q, kvc, kvp],
        out_specs=pl.BlockSpec((bsz, ATT_BLK, 512), lambda j: (0, j, 0)),
        out_shape=jax.ShapeDtypeStruct((bsz, seq, 512), BF16),
        compiler_params=_params("parallel"))(sinks, z3, z3, z3).reshape(t, 512)


def _attn_bwd(z0, dmix, sinks, bsz, name):
    t = z0.shape[0]
    seq = t // bsz
    nb = seq // ATT_BLK

    def body(s_ref, q_ref, kvc_ref, kvp_ref, do_ref, dq_ref, dkv_ref, dsink_ref, dbq_ref, dbkv_ref, carry):
        j = pl.program_id(0)

        @pl.when(j == 0)
        def _():
            carry[...] = jnp.zeros_like(carry)
            dsink_ref[...] = jnp.zeros_like(dsink_ref)
            dbq_ref[...] = jnp.zeros_like(dbq_ref)
            dbkv_ref[...] = jnp.zeros_like(dbkv_ref)

        valid = _attn_valid(j == nb - 1, STACK)
        lane = lax.broadcasted_iota(jnp.int32, (1, 128), 1)
        dsink = jnp.zeros((1, 128), F32)
        dbq = [jnp.zeros((1, 128), F32)] * HEAD_PAIRS
        dbkv = jnp.zeros((1, 256), F32)
        for b in range(bsz):
            kpair, vpair = _window_kv(kvc_ref.at[b], kvp_ref.at[b])
            dk = jnp.zeros((2 * ATT_BLK, 128), F32)
            dv = jnp.zeros((2 * ATT_BLK, 128), F32)
            dqs = []
            for kh in range(2):
                qs = _stacked(q_ref.at[b], kh, 0.125)
                dos = _stacked(do_ref.at[b], kh, 1.0)
                p, ps = _attn_probs(qs, kpair, _sink_rows(s_ref, kh), valid)
                dp = _nt(dos, vpair)
                delta = _row_sums(p * dp)
                ds = (p * (dp - _both(delta))).astype(BF16)
                dqs.append(_nn(ds, kpair))
                dk = dk + _tn(ds, qs)
                dv = dv + _tn(p.astype(BF16), dos)
                psd = ps * delta
                for g in range(HEAD_PAIRS):
                    part = jnp.sum(psd[g * ATT_BLK:(g + 1) * ATT_BLK], axis=0, keepdims=True)
                    dsink = dsink - jnp.where(lane == kh * 4 + g, part, 0.0)
            for g in range(HEAD_PAIRS):
                dq = _unstacked(dqs[0], dqs[1], g) * 0.125
                dq_ref[b, :, g * 128:(g + 1) * 128] = dq.astype(BF16)
                dbq[g] = dbq[g] + jnp.sum(dq, axis=0, keepdims=True)
            dkv = jnp.concatenate([dk[ATT_BLK:], dv[ATT_BLK:]], axis=1) + carry[b]
            dkv_ref[b] = dkv.astype(BF16)
            dbkv = dbkv + jnp.sum(dkv, axis=0, keepdims=True)
            carry[b] = jnp.concatenate([dk[:ATT_BLK], dv[:ATT_BLK]], axis=1)
        dsink_ref[...] += dsink
        dbq_ref[...] += jnp.concatenate(dbq, axis=1)
        dbkv_ref[...] += dbkv

    q, kvc, kvp = _attn_specs(bsz, lambda j: nb - 1 - j)
    z3 = z0.reshape(bsz, seq, z0.shape[1])
    d3 = dmix.reshape(bsz, seq, dmix.shape[1])
    dq, dkv, dsink, dbq, dbkv = pl.pallas_call(
        body, name=name, grid=(nb,),
        in_specs=[pl.BlockSpec(memory_space=pltpu.SMEM), q, kvc, kvp,
                  pl.BlockSpec((bsz, ATT_BLK, 512), lambda j: (0, nb - 1 - j, 0))],
        out_specs=[pl.BlockSpec((bsz, ATT_BLK, 512), lambda j: (0, nb - 1 - j, 0)),
                   pl.BlockSpec((bsz, ATT_BLK, 256), lambda j: (0, nb - 1 - j, 0)), _row(128), _row(512), _row(256)],
        out_shape=[jax.ShapeDtypeStruct((bsz, seq, 512), BF16), jax.ShapeDtypeStruct((bsz, seq, 256), BF16),
                   jax.ShapeDtypeStruct((1, 128), F32), jax.ShapeDtypeStruct((1, 512), F32),
                   jax.ShapeDtypeStruct((1, 256), F32)],
        scratch_shapes=[pltpu.VMEM((bsz, ATT_BLK, 256), F32)],
        compiler_params=_params("arbitrary"))(sinks, z3, z3, z3, d3)
    return dq.reshape(t, 512), dkv.reshape(t, 256), dsink, dbq, dbkv


def _seq_specs(ts, nt, t, width, col):
    per = ts // HALO
    cur = pl.BlockSpec((ts, width), lambda b, i: (b * nt + i, col))
    prev = pl.BlockSpec((HALO, width), lambda b, i: (jnp.maximum((b * nt + i) * per - 1, 0), col))
    nxt = pl.BlockSpec((HALO, width), lambda b, i: (jnp.minimum((b * nt + i + 1) * per, t // HALO - 1), col))
    return prev, cur, nxt


SUB = 8
CONV_ROWS = 64


def _shifted_copies(src, sh, rows_first, rows_rest):
    for r in range(SUB):
        rows = rows_first if r == 0 else rows_rest
        sh[r, pl.ds(0, rows), :] = src[pl.ds(r, rows), :]


def _tap_sum(sh, w, offset, c0, rows):
    acc = None
    for k in range(CONV_K):
        o = offset(k)
        term = sh[o % SUB, pl.ds(c0 + o - o % SUB, rows), :] * w[k:k + 1, :]
        acc = term if acc is None else acc + term
    return acc


def _glu_rows(a_ref, g_ref, rows=slice(None)):
    return a_ref[rows, :].astype(F32) * jax.nn.sigmoid(g_ref[rows, :].astype(F32))


def _conv_fwd(z0, conv_w, conv_b, ln_g, ln_b, bsz, name):
    t = z0.shape[0]
    s = t // bsz
    ts = _seq_tile(s)
    nt = s // ts
    first = HALO - (CONV_K - 1)

    def body(ap_ref, ac_ref, gp_ref, gc_ref, w_ref, cb_ref, lg_ref, lb_ref, o_ref, y_ref, hbuf, sh):
        hbuf[0:HALO, :] = jnp.where(pl.program_id(1) > 0, _glu_rows(ap_ref, gp_ref), 0.0)
        hbuf[HALO:HALO + ts, :] = _glu_rows(ac_ref, gc_ref)
        _shifted_copies(hbuf, sh, ts + HALO, ts + HALO - SUB)
        w, cb, lg, lb = w_ref[...], cb_ref[...], lg_ref[...], lb_ref[...]
        for c0 in range(0, ts, CONV_ROWS):
            y = _tap_sum(sh, w, lambda k: first + k, c0, CONV_ROWS) + cb
            y_ref[c0:c0 + CONV_ROWS, :] = y
            o = _ln(y, lg, lb)[0]
            o_ref[c0:c0 + CONV_ROWS, :] = (o * jax.nn.sigmoid(o)).astype(BF16)

    ap, ac, _ = _seq_specs(ts, nt, t, 512, 1)
    gp, gc, _ = _seq_specs(ts, nt, t, 512, 2)
    tile = pl.BlockSpec((ts, 512), lambda b, i: (b * nt + i, 0))
    return pl.pallas_call(
        body, name=name, grid=(bsz, nt),
        in_specs=[ap, ac, gp, gc, _full((HALO, 512)), _row(512), _row(512), _row(512)],
        out_specs=[tile, tile],
        out_shape=[jax.ShapeDtypeStruct((t, 512), BF16), jax.ShapeDtypeStruct((t, 512), F32)],
        scratch_shapes=[pltpu.VMEM((HALO + ts, 512), F32), pltpu.VMEM((SUB, HALO + ts, 512), F32)],
        compiler_params=_params("parallel", "parallel"))(z0, z0, z0, z0, conv_w, conv_b, ln_g, ln_b)


def _conv_bwd(z0, y, dmix, conv_w, ln_g, ln_b, bsz, name):
    t = z0.shape[0]
    s = t // bsz
    ts = _seq_tile(s)
    nt = s // ts
    first = HALO - (CONV_K - 1)

    def body(ap_ref, ac_ref, gp_ref, gc_ref, yc_ref, yn_ref, dc_ref, dn_ref, w_ref, lg_ref, lb_ref,
             da_ref, dg_ref, dw_ref, dcb_ref, dlg_ref, dlb_ref, dba_ref, dbg_ref, hbuf, dybuf, sh_h, sh_dy):
        b, i = pl.program_id(0), pl.program_id(1)

        @pl.when((b == 0) & (i == 0))
        def _():
            for ref in (dw_ref, dcb_ref, dlg_ref, dlb_ref, dba_ref, dbg_ref):
                ref[...] = jnp.zeros_like(ref)

        w, lg, lb = w_ref[...], lg_ref[...], lb_ref[...]
        hbuf[0:HALO, :] = jnp.where(i > 0, _glu_rows(ap_ref, gp_ref), 0.0)
        hbuf[HALO:HALO + ts, :] = _glu_rows(ac_ref, gc_ref)
        _shifted_copies(hbuf, sh_h, ts + HALO, ts + HALO - SUB)

        def d_conv_out(yv, dout):
            o, xhat, rstd = _ln(yv, lg, lb)
            sg_o = jax.nn.sigmoid(o)
            d_o = dout * sg_o * (1.0 + o * (1.0 - sg_o))
            return _ln_bwd(d_o, xhat, rstd, lg), d_o * xhat, d_o

        dlg = jnp.zeros((1, 512), F32)
        dlb = jnp.zeros((1, 512), F32)
        dcb = jnp.zeros((1, 512), F32)
        for c0 in range(0, ts, CONV_ROWS):
            rows = slice(c0, c0 + CONV_ROWS)
            dy, g_part, b_part = d_conv_out(yc_ref[rows, :], dc_ref[rows, :].astype(F32))
            dybuf[rows, :] = dy
            dlg = dlg + jnp.sum(g_part, axis=0, keepdims=True)
            dlb = dlb + jnp.sum(b_part, axis=0, keepdims=True)
            dcb = dcb + jnp.sum(dy, axis=0, keepdims=True)
        dn = jnp.where(i < nt - 1, dn_ref[...].astype(F32), 0.0)
        dybuf[ts:ts + HALO, :] = d_conv_out(yn_ref[...], dn)[0]
        dlg_ref[...] += dlg
        dlb_ref[...] += dlb
        dcb_ref[...] += dcb
        _shifted_copies(dybuf, sh_dy, ts + HALO - SUB, ts + HALO - SUB)

        for k in range(CONV_K):
            o = first + k
            prod = dybuf[0:ts, :] * sh_h[o % SUB, pl.ds(o - o % SUB, ts), :]
            dw_ref[pl.ds(k, 1), :] += jnp.sum(prod, axis=0, keepdims=True)
        dba = jnp.zeros((1, 512), F32)
        dbg = jnp.zeros((1, 512), F32)
        for c0 in range(0, ts, CONV_ROWS):
            rows = slice(c0, c0 + CONV_ROWS)
            dh = _tap_sum(sh_dy, w, lambda k: CONV_K - 1 - k, c0, CONV_ROWS)
            a_c = ac_ref[rows, :].astype(F32)
            sg_c = jax.nn.sigmoid(gc_ref[rows, :].astype(F32))
            d_a = dh * sg_c
            d_g = dh * a_c * sg_c * (1.0 - sg_c)
            da_ref[rows, :] = d_a.astype(BF16)
            dg_ref[rows, :] = d_g.astype(BF16)
            dba = dba + jnp.sum(d_a, axis=0, keepdims=True)
            dbg = dbg + jnp.sum(d_g, axis=0, keepdims=True)
        dba_ref[...] += dba
        dbg_ref[...] += dbg

    ap, ac, _ = _seq_specs(ts, nt, t, 512, 1)
    gp, gc, _ = _seq_specs(ts, nt, t, 512, 2)
    _, yc, yn = _seq_specs(ts, nt, t, 512, 0)
    _, dc, dn = _seq_specs(ts, nt, t, 512, 1)
    tile = pl.BlockSpec((ts, 512), lambda b, i: (b * nt + i, 0))
    vec = jax.ShapeDtypeStruct((1, 512), F32)
    return pl.pallas_call(
        body, name=name, grid=(bsz, nt),
        in_specs=[ap, ac, gp, gc, yc, yn, dc, dn, _full((HALO, 512)), _row(512), _row(512)],
        out_specs=[tile, tile, _full((HALO, 512)), _row(512), _row(512), _row(512), _row(512), _row(512)],
        out_shape=[jax.ShapeDtypeStruct((t, 512), BF16), jax.ShapeDtypeStruct((t, 512), BF16),
                   jax.ShapeDtypeStruct((HALO, 512), F32), vec, vec, vec, vec, vec],
        scratch_shapes=[pltpu.VMEM((HALO + ts, 512), F32), pltpu.VMEM((ts + HALO, 512), F32),
                        pltpu.VMEM((SUB, HALO + ts, 512), F32), pltpu.VMEM((SUB, HALO + ts, 512), F32)],
        compiler_params=_params("arbitrary", "arbitrary"))(z0, z0, z0, z0, y, y, dmix, dmix, conv_w, ln_g, ln_b)


def _pooled(pbuf, g, ts, tok):
    w = 2 << g
    cols = slice(128 * g, 128 * (g + 1))
    sm = pbuf[pl.ds(HALO, ts), cols]
    for d in range(1, w):
        sm = sm + pbuf[pl.ds(HALO - d, ts), cols]
    cnt = jnp.minimum(tok + 1, w).astype(F32)
    return sm / cnt - pbuf[pl.ds(HALO, ts), cols]


def _pool_fwd(z1, w_pool, scale, bsz, name):
    t = z1.shape[0]
    s = t // bsz
    ts = _seq_tile(s)
    nt = s // ts

    def body(zp_ref, zc_ref, wp_ref, sc_ref, o_ref, pbuf):
        i = pl.program_id(1)
        pbuf[0:HALO, :] = jnp.where(i > 0, zp_ref[...].astype(F32), 0.0)
        pbuf[HALO:HALO + ts, :] = zc_ref[...].astype(F32)
        tok = i * ts + lax.broadcasted_iota(jnp.int32, (ts, 1), 0)
        for g in range(4):
            cols = slice(128 * g, 128 * (g + 1))
            pooled = _pooled(pbuf, g, ts, tok).astype(BF16)
            o_ref[:, cols] = (_nn(pooled, wp_ref[g].astype(BF16)) * sc_ref[:, cols]).astype(BF16)

    zp, zc, _ = _seq_specs(ts, nt, t, 512, 0)
    return pl.pallas_call(
        body, name=name, grid=(bsz, nt), in_specs=[zp, zc, _full((4, 128, 128)), _row(512)],
        out_specs=pl.BlockSpec((ts, 512), lambda b, i: (b * nt + i, 0)),
        out_shape=jax.ShapeDtypeStruct((t, 512), BF16),
        scratch_shapes=[pltpu.VMEM((HALO + ts, 512), F32)],
        compiler_params=_params("parallel", "parallel"))(z1, z1, w_pool, scale)


def _pool_bwd(z1, dmix, w_pool, scale, bsz, name):
    t = z1.shape[0]
    s = t // bsz
    ts = _seq_tile(s)
    nt = s // ts
    rr = ts + HALO

    def body(zp_ref, zc_ref, dc_ref, dn_ref, wp_ref, sc_ref, dz_ref, dwp_ref, dsc_ref, pbuf, ebuf):
        b, i = pl.program_id(0), pl.program_id(1)

        @pl.when((b == 0) & (i == 0))
        def _():
            dwp_ref[...] = jnp.zeros_like(dwp_ref)
            dsc_ref[...] = jnp.zeros_like(dsc_ref)

        pbuf[0:HALO, :] = jnp.where(i > 0, zp_ref[...].astype(F32), 0.0)
        pbuf[HALO:HALO + ts, :] = zc_ref[...].astype(F32)
        dn = jnp.where(i < nt - 1, dn_ref[...].astype(F32), 0.0)
        dout = jnp.concatenate([dc_ref[...].astype(F32), dn], axis=0)
        tok = i * ts + lax.broadcasted_iota(jnp.int32, (ts, 1), 0)
        tok_r = i * ts + lax.broadcasted_iota(jnp.int32, (rr, 1), 0)
        for g in range(4):
            w = 2 << g
            cols = slice(128 * g, 128 * (g + 1))
            wg = wp_ref[g].astype(BF16)
            pooled = _pooled(pbuf, g, ts, tok).astype(BF16)
            dsc_ref[:, cols] += jnp.sum(dout[:ts, cols] * _nn(pooled, wg), axis=0, keepdims=True)
            dy = (dout[:, cols] * sc_ref[:, cols]).astype(BF16)
            dwp_ref[g] += _tn(pooled, dy[:ts])
            dpl = _nt(dy, wg)
            ebuf[...] = dpl / jnp.minimum(tok_r + 1, w).astype(F32)
            dz = ebuf[pl.ds(0, ts), :] - dpl[:ts]
            for d in range(1, w):
                dz = dz + ebuf[pl.ds(d, ts), :]
            dz_ref[:, cols] = dz.astype(BF16)

    zp, zc, _ = _seq_specs(ts, nt, t, 512, 0)
    _, dc, dn = _seq_specs(ts, nt, t, 512, 0)
    return pl.pallas_call(
        body, name=name, grid=(bsz, nt), in_specs=[zp, zc, dc, dn, _full((4, 128, 128)), _row(512)],
        out_specs=[pl.BlockSpec((ts, 512), lambda b, i: (b * nt + i, 0)), _full((4, 128, 128)), _row(512)],
        out_shape=[jax.ShapeDtypeStruct((t, 512), BF16), jax.ShapeDtypeStruct((4, 128, 128), F32),
                   jax.ShapeDtypeStruct((1, 512), F32)],
        scratch_shapes=[pltpu.VMEM((HALO + ts, 512), F32), pltpu.VMEM((rr, 128), F32)],
        compiler_params=_params("arbitrary", "arbitrary"))(z1, z1, dmix, dmix, w_pool, scale)


def _tril():
    r = lax.broadcasted_iota(jnp.int32, (SGU_CHUNK, SGU_CHUNK), 0)
    c = lax.broadcasted_iota(jnp.int32, (SGU_CHUNK, SGU_CHUNK), 1)
    return r >= c


def _sgu_fwd(z1, ln_g, ln_b, w_s, b_rows, name):
    t = z1.shape[0]
    ts = _tile(t)

    def body(zu_ref, zv_ref, lg_ref, lb_ref, ws_ref, bs_ref, o_ref):
        v = _gelu(zv_ref[...].astype(F32))[0]
        vb = _ln(v, lg_ref[...], lb_ref[...])[0].astype(BF16)
        tril = _tril()
        for g in range(4):
            cols = slice(128 * g, 128 * (g + 1))
            wg = jnp.where(tril, ws_ref[g], 0.0).astype(BF16)
            for c in range(ts // SGU_CHUNK):
                rows = slice(SGU_CHUNK * c, SGU_CHUNK * (c + 1))
                mixed = _nn(wg, vb[rows, cols]) + bs_ref[g]
                o_ref[rows, cols] = (_gelu(zu_ref[rows, cols].astype(F32))[0] * mixed).astype(BF16)

    return pl.pallas_call(
        body, name=name, grid=(t // ts,),
        in_specs=[pl.BlockSpec((ts, 512), lambda i: (i, 1)), pl.BlockSpec((ts, 512), lambda i: (i, 2)),
                  _row(512), _row(512), _full((4, 128, 128)), _full((4, 128, 128))],
        out_specs=pl.BlockSpec((ts, 512), lambda i: (i, 0)), out_shape=jax.ShapeDtypeStruct((t, 512), BF16),
        compiler_params=_params("parallel"))(z1, z1, ln_g, ln_b, w_s, b_rows)


def _sgu_bwd(z1, dmix, ln_g, ln_b, w_s, b_rows, name):
    t = z1.shape[0]
    ts = _tile(t)

    def body(zu_ref, zv_ref, d_ref, lg_ref, lb_ref, ws_ref, bs_ref,
             dzu_ref, dzv_ref, dws_ref, dbs_ref, dlg_ref, dlb_ref, dvbuf):
        @pl.when(pl.program_id(0) == 0)
        def _():
            for ref in (dws_ref, dbs_ref, dlg_ref, dlb_ref):
                ref[...] = jnp.zeros_like(ref)

        zv = zv_ref[...].astype(F32)
        v, thv = _gelu(zv)
        lg = lg_ref[...]
        vln, xhat, rstd = _ln(v, lg, lb_ref[...])
        vb = vln.astype(BF16)
        tril = _tril()
        for g in range(4):
            cols = slice(128 * g, 128 * (g + 1))
            wg = jnp.where(tril, ws_ref[g], 0.0).astype(BF16)
            dws = jnp.zeros((SGU_CHUNK, SGU_CHUNK), F32)
            dbs = jnp.zeros((1, SGU_CHUNK), F32)
            for c in range(ts // SGU_CHUNK):
                rows = slice(SGU_CHUNK * c, SGU_CHUNK * (c + 1))
                vbc = vb[rows, cols]
                mixed = _nn(wg, vbc) + bs_ref[g]
                zu = zu_ref[rows, cols].astype(F32)
                u, thu = _gelu(zu)
                dout = d_ref[rows, cols].astype(F32)
                dzu_ref[rows, cols] = (dout * mixed * _gelu_grad(zu, thu)).astype(BF16)
                dm = dout * u
                dmb = dm.astype(BF16)
                dws = dws + _nt(dmb, vbc)
                dbs = dbs + jnp.sum(dm.T, axis=0, keepdims=True)
                dvbuf[rows, cols] = _tn(wg, dmb)
            dws_ref[g] += jnp.where(tril, dws, 0.0)
            dbs_ref[pl.ds(g, 1), :] += dbs
        dvln = dvbuf[...]
        dlg_ref[...] += jnp.sum(dvln * xhat, axis=0, keepdims=True)
        dlb_ref[...] += jnp.sum(dvln, axis=0, keepdims=True)
        dzv_ref[...] = (_ln_bwd(dvln, xhat, rstd, lg) * _gelu_grad(zv, thv)).astype(BF16)

    tile = pl.BlockSpec((ts, 512), lambda i: (i, 0))
    vec = jax.ShapeDtypeStruct((1, 512), F32)
    return pl.pallas_call(
        body, name=name, grid=(t // ts,),
        in_specs=[pl.BlockSpec((ts, 512), lambda i: (i, 1)), pl.BlockSpec((ts, 512), lambda i: (i, 2)),
                  pl.BlockSpec((ts, 512), lambda i: (i, 1)), _row(512), _row(512), _full((4, 128, 128)),
                  _full((4, 128, 128))],
        out_specs=[tile, tile, _full((4, 128, 128)), _full((4, 128)), _row(512), _row(512)],
        out_shape=[jax.ShapeDtypeStruct((t, 512), BF16), jax.ShapeDtypeStruct((t, 512), BF16),
                   jax.ShapeDtypeStruct((4, 128, 128), F32), jax.ShapeDtypeStruct((4, 128), F32), vec, vec],
        scratch_shapes=[pltpu.VMEM((ts, 512), F32)],
        compiler_params=_params("arbitrary"))(z1, z1, dmix, ln_g, ln_b, w_s, b_rows)


def _row_tile(r):
    for cand in (512, 352, 256, 192, 128, 64, 32, 16, 8):
        if r % cand == 0:
            return cand
    return r


def _sum_slabs(a, name):
    k, r, c = a.shape
    tr = _row_tile(r)

    def body(*refs):
        acc = refs[0][...].astype(F32)
        for ref in refs[1:-1]:
            acc = acc + ref[...].astype(F32)
        refs[-1][...] = acc

    in_specs = [pl.BlockSpec((None, tr, c), functools.partial(lambda i, s: (s, i, 0), s=s)) for s in range(k)]
    return pl.pallas_call(
        body, name=name, grid=(r // tr,), in_specs=in_specs, out_specs=pl.BlockSpec((tr, c), lambda i: (i, 0)),
        out_shape=jax.ShapeDtypeStruct((r, c), F32), compiler_params=_params("parallel"))(*([a] * k))


def _adamw_math(w, g, m, v):
    mn = ADAM_B1 * m + (1.0 - ADAM_B1) * g
    vn = ADAM_B2 * v + (1.0 - ADAM_B2) * (g * g)
    m_hat = mn / (1.0 - ADAM_B1 ** ADAM_STEP)
    v_hat = vn / (1.0 - ADAM_B2 ** ADAM_STEP)
    return -ADAM_LR * (m_hat / (jnp.sqrt(v_hat) + ADAM_EPS) + ADAM_WD * w), mn, vn


def _reduce_adamw(landing, w, m, v, name, layer=None, into=None):
    k, r, c = landing.shape
    tr = _row_tile(r)
    n_into = 0 if into is None else 4

    def body(*refs):
        slabs, (w_ref, m_ref, v_ref) = refs[:k], refs[k:k + 3]
        g_ref, d_ref, mo_ref, vo_ref = refs[k + 3 + n_into:]
        g = slabs[0][...].astype(F32)
        for ref in slabs[1:]:
            g = g + ref[...].astype(F32)
        g_ref[...] = g
        d_ref[...], mo_ref[...], vo_ref[...] = _adamw_math(w_ref[...], g, m_ref[...], v_ref[...])

    if layer is None:
        spec = pl.BlockSpec((tr, c), lambda i: (i, 0))
    else:
        spec = pl.BlockSpec((None, tr, c), lambda i: (layer, i, 0))
    in_specs = [pl.BlockSpec((None, tr, c), functools.partial(lambda i, s: (s, i, 0), s=s)) for s in range(k)]
    in_specs += [spec] * 3 + [ANY] * n_into
    shape = jax.ShapeDtypeStruct(w.shape, F32)
    return pl.pallas_call(
        body, name=name, grid=(r // tr,), in_specs=in_specs, out_specs=[spec] * 4, out_shape=[shape] * 4,
        input_output_aliases={k + 3 + j: j for j in range(n_into)},
        compiler_params=_params("parallel"))(*([landing] * k), w, m, v, *(into or ()))


def _adamw(w, g, m, v, name):
    r, c = w.shape
    tr = _row_tile(r)

    def body(w_ref, g_ref, m_ref, v_ref, d_ref, mo_ref, vo_ref):
        d_ref[...], mo_ref[...], vo_ref[...] = _adamw_math(w_ref[...], g_ref[...], m_ref[...], v_ref[...])

    spec = pl.BlockSpec((tr, c), lambda i: (i, 0))
    shape = jax.ShapeDtypeStruct((r, c), F32)
    return pl.pallas_call(
        body, name=name, grid=(r // tr,), in_specs=[spec] * 4, out_specs=[spec] * 3, out_shape=[shape] * 3,
        compiler_params=_params("parallel"))(w, g, m, v)


ANY = pl.BlockSpec(memory_space=pl.ANY)


def _all_gather(block, name):
    r, c_dim = block.shape

    def body(x_ref, out_ref, token, send_sems, recv_sems, local_sem):
        token[...] = jnp.zeros_like(token)
        x, y, c = lax.axis_index("x"), lax.axis_index("y"), lax.axis_index("c")
        me, sibling = (x, y, c), (x, y, 1 - c)
        chips = [(1 - x, y), (x, 1 - y), (1 - x, 1 - y)]

        def rows(px, py, pc):
            return out_ref.at[4 * px + 2 * py + pc]

        def copy(k, blk, to, src=None):
            return pltpu.make_async_remote_copy(
                src_ref=rows(*blk) if src is None else src, dst_ref=rows(*blk), send_sem=send_sems.at[k],
                recv_sem=recv_sems.at[k], device_id=to, device_id_type=MESH)

        mine = pltpu.make_async_copy(x_ref, rows(*me), local_sem)
        mine.start()
        first = [copy(0, me, sibling, src=x_ref)]
        first += [copy(1 + j, me, (*chip, c), src=x_ref) for j, chip in enumerate(chips)]
        for cp in first:
            cp.start()
        passed = [copy(4 + j, (*chip, c), sibling) for j, chip in enumerate(chips)]
        for j, chip in enumerate(chips):
            copy(1 + j, (*chip, c), me).wait_recv()
            passed[j].start()
        copy(0, sibling, me).wait_recv()
        for j, chip in enumerate(chips):
            copy(4 + j, (*chip, 1 - c), me).wait_recv()
        for cp in first + passed:
            cp.wait_send()
        mine.wait()

    return pl.pallas_call(
        body, name=name, in_specs=[ANY], out_specs=[ANY, pl.BlockSpec(memory_space=pltpu.VMEM)],
        out_shape=[jax.ShapeDtypeStruct((N_DEV, r, c_dim), block.dtype), jax.ShapeDtypeStruct((8, 128), F32)],
        scratch_shapes=[pltpu.SemaphoreType.DMA((7,)), pltpu.SemaphoreType.DMA((7,)), pltpu.SemaphoreType.DMA],
    )(block)


HBM = pl.BlockSpec(memory_space=pltpu.HBM)
SEM = pl.BlockSpec(memory_space=pltpu.SEMAPHORE)
EFFECT = pltpu.SideEffectType.DATAFLOW_SIDE_EFFECTING


def _exchange_copies(scatter, src_refs, land_refs, send_sems, recv_sems, local_sems):
    x, y, c = lax.axis_index("x"), lax.axis_index("y"), lax.axis_index("c")
    me = 4 * x + 2 * y + c
    sends, arrivals, locals_ = [], [], []
    for a, (src, land) in enumerate(zip(src_refs, land_refs)):
        def pick(idx, src=src):
            return src.at[idx] if scatter else src

        locals_.append(pltpu.make_async_copy(pick(me), land.at[me], local_sems.at[a]))
        for r in range(1, N_DEV):
            px = 1 - x if r & 4 else x
            py = 1 - y if r & 2 else y
            pc = 1 - c if r & 1 else c
            peer, s = 4 * px + 2 * py + pc, 7 * a + r - 1
            sends.append(pltpu.make_async_remote_copy(
                src_ref=pick(peer), dst_ref=land.at[me], send_sem=send_sems.at[s], recv_sem=recv_sems.at[s],
                device_id=(px, py, pc), device_id_type=MESH))
            arrivals.append(pltpu.make_async_remote_copy(
                src_ref=pick(peer), dst_ref=land.at[peer], send_sem=send_sems.at[s], recv_sem=recv_sems.at[s],
                device_id=(px, py, pc), device_id_type=MESH))
    return sends, arrivals, locals_


def _exchange_start(srcs, scatter, name):
    n = len(srcs)
    lands = [lax.empty((N_DEV,) + s.shape[-2:], s.dtype) for s in srcs]

    def body(*refs):
        src_refs, land_refs = refs[:n], refs[n:2 * n]
        send_sems, recv_sems, local_sems = refs[2 * n:2 * n + 3]
        token = refs[-1]
        sends, _, locals_ = _exchange_copies(scatter, src_refs, land_refs, send_sems, recv_sems, local_sems)
        for cp in locals_ + sends:
            cp.start()
        token[...] = jnp.zeros_like(token)

    res = pl.pallas_call(
        body, name=name,
        out_shape=[pltpu.SemaphoreType.DMA((7 * n,)), pltpu.SemaphoreType.DMA((7 * n,)), pltpu.SemaphoreType.DMA((n,))]
        + [pltpu.HBM(a.shape, a.dtype) for a in list(srcs) + lands] + [jax.ShapeDtypeStruct((8, 128), F32)],
        in_specs=[HBM] * (2 * n), out_specs=[SEM] * 3 + [HBM] * (2 * n) + [pl.BlockSpec(memory_space=pltpu.VMEM)],
        input_output_aliases={i: 3 + i for i in range(2 * n)},
        compiler_params=pltpu.CompilerParams(has_side_effects=EFFECT),
    )(*[pltpu.with_memory_space_constraint(a, pltpu.HBM) for a in list(srcs) + lands])
    return (n, scatter, res[:3], res[3:3 + 2 * n]), res[-1]


def _exchange_wait(handle, after, name):
    n, scatter, sems, thru = handle

    def body(*refs):
        src_refs, land_refs = refs[:n], refs[n:2 * n]
        send_sems, recv_sems, local_sems = refs[2 * n:2 * n + 3]
        sends, arrivals, locals_ = _exchange_copies(scatter, src_refs, land_refs, send_sems, recv_sems, local_sems)
        for cp in arrivals:
            cp.wait_recv()
        for cp in sends:
            cp.wait_send()
        for cp in locals_:
            cp.wait()

    res = pl.pallas_call(
        body, name=name, out_shape=[pltpu.HBM(a.shape, a.dtype) for a in thru],
        in_specs=[HBM] * (2 * n) + [SEM] * 3 + [ANY], out_specs=[HBM] * (2 * n),
        input_output_aliases={i: i for i in range(2 * n)},
        compiler_params=pltpu.CompilerParams(has_side_effects=EFFECT),
    )(*thru, *sems, after)
    return res[n:]


def _behind(tokens, a):
    zero = sum(tok[0, 0] for tok in tokens)
    return jax.tree.map(lambda v: v + zero.astype(v.dtype), a)


def _perm_heads(a, perm, axis):
    idx = [slice(None)] * a.ndim
    parts = []
    for h in perm:
        idx[axis] = slice(64 * h, 64 * (h + 1))
        parts.append(a[tuple(idx)])
    idx[axis] = slice(512, None)
    if a.shape[axis] > 512:
        parts.append(a[tuple(idx)])
    return jnp.concatenate(parts, axis=axis)


Q_INV = tuple(int(i) for i in np.argsort(Q_PERM))


def _in0_to_kernel(a, axis):
    a = _perm_heads(a, Q_PERM, axis)
    idx = [slice(None)] * a.ndim

    def cut(lo, hi):
        idx[axis] = slice(lo, hi)
        return a[tuple(idx)]

    return jnp.concatenate([cut(0, 512), cut(768, 1792), cut(512, 768)], axis=axis)


def _in0_from_kernel(a, axis):
    idx = [slice(None)] * a.ndim

    def cut(lo, hi):
        idx[axis] = slice(lo, hi)
        return a[tuple(idx)]

    a = jnp.concatenate([cut(0, 512), cut(1536, 1792), cut(512, 1536)], axis=axis)
    return _perm_heads(a, Q_INV, axis)


def _f32_as_u16_rows(v, rows):
    bits = lax.bitcast_convert_type(v, jnp.uint16).reshape(-1)
    return jnp.pad(bits, (0, rows * D - bits.shape[0])).reshape(rows, D)


def _pad_rows(v, rows):
    v = v.reshape(-1)
    return jnp.pad(v, (0, rows * D - v.shape[0])).reshape(rows, D)


def kernel(x, mix_norm, a_w_in, a_b_in, a_sinks, a_conv_w, a_conv_b, a_cln_g, a_cln_b, a_w_out, c_w_in, c_w_pool, c_pool_scale, c_sln_g, c_sln_b, c_w_s, c_b_s, c_w_out, ffn_norm, ffn_w_gate, ffn_w_up, ffn_w_down, final_norm, loss_target, m_mix_norm, m_a_w_in, m_a_b_in, m_a_sinks, m_a_conv_w, m_a_conv_b, m_a_cln_g, m_a_cln_b, m_a_w_out, m_c_w_in, m_c_w_pool, m_c_pool_scale, m_c_sln_g, m_c_sln_b, m_c_w_s, m_c_b_s, m_c_w_out, m_ffn_norm, m_ffn_w_gate, m_ffn_w_up, m_ffn_w_down, m_final_norm, v_mix_norm, v_a_w_in, v_a_b_in, v_a_sinks, v_a_conv_w, v_a_conv_b, v_a_cln_g, v_a_cln_b, v_a_w_out, v_c_w_in, v_c_w_pool, v_c_pool_scale, v_c_sln_g, v_c_sln_b, v_c_w_s, v_c_b_s, v_c_w_out, v_ffn_norm, v_ffn_w_gate, v_ffn_w_up, v_ffn_w_down, v_final_norm):
    bsz, seq, _ = x.shape
    t = bsz * seq
    w_in = dict(mix_norm=mix_norm, a_w_in=a_w_in, a_b_in=a_b_in, a_sinks=a_sinks, a_conv_w=a_conv_w, a_conv_b=a_conv_b,
                a_cln_g=a_cln_g, a_cln_b=a_cln_b, a_w_out=a_w_out, c_w_in=c_w_in, c_w_pool=c_w_pool,
                c_pool_scale=c_pool_scale, c_sln_g=c_sln_g, c_sln_b=c_sln_b, c_w_s=c_w_s, c_b_s=c_b_s, c_w_out=c_w_out,
                ffn_norm=ffn_norm, ffn_w_gate=ffn_w_gate, ffn_w_up=ffn_w_up, ffn_w_down=ffn_w_down, final_norm=final_norm)
    m_in = dict(mix_norm=m_mix_norm, a_w_in=m_a_w_in, a_b_in=m_a_b_in, a_sinks=m_a_sinks, a_conv_w=m_a_conv_w,
                a_conv_b=m_a_conv_b, a_cln_g=m_a_cln_g, a_cln_b=m_a_cln_b, a_w_out=m_a_w_out, c_w_in=m_c_w_in,
                c_w_pool=m_c_w_pool, c_pool_scale=m_c_pool_scale, c_sln_g=m_c_sln_g, c_sln_b=m_c_sln_b, c_w_s=m_c_w_s,
                c_b_s=m_c_b_s, c_w_out=m_c_w_out, ffn_norm=m_ffn_norm, ffn_w_gate=m_ffn_w_gate, ffn_w_up=m_ffn_w_up,
                ffn_w_down=m_ffn_w_down, final_norm=m_final_norm)
    v_in = dict(mix_norm=v_mix_norm, a_w_in=v_a_w_in, a_b_in=v_a_b_in, a_sinks=v_a_sinks, a_conv_w=v_a_conv_w,
                a_conv_b=v_a_conv_b, a_cln_g=v_a_cln_g, a_cln_b=v_a_cln_b, a_w_out=v_a_w_out, c_w_in=v_c_w_in,
                c_w_pool=v_c_w_pool, c_pool_scale=v_c_pool_scale, c_sln_g=v_c_sln_g, c_sln_b=v_c_sln_b, c_w_s=v_c_w_s,
                c_b_s=v_c_b_s, c_w_out=v_c_w_out, ffn_norm=v_ffn_norm, ffn_w_gate=v_ffn_w_gate, ffn_w_up=v_ffn_w_up,
                ffn_w_down=v_ffn_w_down, final_norm=v_final_norm)

    small = jnp.concatenate([a_conv_w[0].reshape(-1), c_pool_scale[0], c_sln_g[0], c_sln_b[0]])
    first_bits = lax.bitcast_convert_type(jnp.concatenate([a_w_in[0].T, a_w_out[0]], axis=0).astype(BF16), jnp.uint16)
    gathered, tok = _all_gather(jnp.concatenate([first_bits, _f32_as_u16_rows(small, W_MISC_ROWS)], axis=0), "gather_mixer0")

    def ffn_shards(l):
        return [ffn_w_gate[l].T.astype(BF16), ffn_w_up[l].T.astype(BF16), ffn_w_down[l].astype(BF16)]

    ffn0_h, tok = _exchange_start(_behind([tok], ffn_shards(0)), False, "gather_ffn0_start")
    mix1_h, tok = _exchange_start(_behind([tok], [c_w_in[0].T.astype(BF16), c_w_out[0].astype(BF16)]), False,
                                  "gather_mixer1_start")
    ffn1_h, tok = _exchange_start(_behind([tok], ffn_shards(1)), False, "gather_ffn1_start")

    a_in_full = lax.bitcast_convert_type(gathered[:, :224].reshape(IN0, D), BF16)
    a_out_full = lax.bitcast_convert_type(gathered[:, 224:352].reshape(D, D), BF16)
    small_all = lax.bitcast_convert_type(
        gathered[:, 352:].reshape(N_DEV, -1)[:, :2 * SMALL_SHARD].reshape(N_DEV, SMALL_SHARD, 2), F32)
    conv_w = small_all[:, :31 * 64].reshape(N_DEV, 31, 64).transpose(1, 0, 2).reshape(31, 512)
    conv_w = jnp.pad(conv_w, ((0, HALO - CONV_K), (0, 0)))
    pool_scale = small_all[:, 31 * 64:31 * 64 + 64].reshape(1, 512)
    sln_g = small_all[:, 31 * 64 + 64:31 * 64 + 128].reshape(1, 512)
    sln_b = small_all[:, 31 * 64 + 128:].reshape(1, 512)

    wt_in0 = _in0_to_kernel(a_in_full, 0)
    b_in0 = _in0_to_kernel(a_b_in, 1)
    w_out0 = _perm_heads(a_out_full, Q_PERM, 0)
    b_rows = jnp.broadcast_to(c_b_s[0][:, :, None], (4, 128, 128))
    conv_b, cln_g, cln_b = a_conv_b, a_cln_g, a_cln_b

    h0 = x.reshape(t, D)
    target = loss_target.reshape(t, D)
    z0, hn0 = _norm_proj(h0, _behind([tok], mix_norm[0:1]), wt_in0, b_in0, "in_proj0")
    attn = _attn_fwd(z0, a_sinks, bsz, "attn_fwd")
    conv, conv_y = _conv_fwd(z0, conv_w, conv_b, cln_g, cln_b, bsz, "conv_fwd")
    h1 = _out_proj(attn, conv, w_out0, h0, "out_proj0")
    wtg0, wtu0, wd0 = (w.reshape(D_FF, D) for w in _exchange_wait(ffn0_h, h1, "gather_ffn0_wait"))
    h2, hnf0, gate0, up0 = _ffn_fwd(h1, ffn_norm[0:1], wtg0, wtu0, wd0, "ffn_fwd0")
    wt_in1, w_out1 = (w.reshape(-1, D) for w in _exchange_wait(mix1_h, h2, "gather_mixer1_wait"))
    z1, hn1 = _norm_proj(h2, mix_norm[1:2], wt_in1, None, "in_proj1")
    pool = _pool_fwd(z1, c_w_pool[0], pool_scale, bsz, "pool_fwd")
    sgu = _sgu_fwd(z1, sln_g, sln_b, c_w_s[0], b_rows, "sgu_fwd")
    h3 = _out_proj(pool, sgu, w_out1, h2, "out_proj1")
    wtg1, wtu1, wd1 = (w.reshape(D_FF, D) for w in _exchange_wait(ffn1_h, h3, "gather_ffn1_wait"))
    h4, hnf1, gate1, up1 = _ffn_fwd(h3, ffn_norm[1:2], wtg1, wtu1, wd1, "ffn_fwd1")

    def blocks(g):
        return g.reshape(N_DEV, g.shape[0] // N_DEV, D)

    dh4, d_final_norm, loss_part = _loss_head(h4, final_norm.reshape(1, D), target, "loss_head")
    dh3, dgate1, dup1, act1, d_fn1 = _ffn_bwd(dh4, h3, ffn_norm[1:2], gate1, up1, wtg1, wtu1, wd1, "ffn_bwd1")
    gw_ffn1 = [_mm_tn(dgate1, hnf1, "dw_gate1"), _mm_tn(dup1, hnf1, "dw_up1"), _mm_tn(act1, dh4, "dw_down1")]
    ffn1_g, tok = _exchange_start([blocks(g) for g in gw_ffn1], True, "scatter_ffn1_start")
    dmix1 = _dmix(dh3, _behind([tok], w_out1), "dmix1")
    gw_c_out = _mm_tn_pieces([pool, sgu], dh3, "dw_out1")
    dzp, d_w_pool, d_pool_scale = _pool_bwd(z1, dmix1, c_w_pool[0], pool_scale, bsz, "pool_bwd")
    dzu, dzv, d_w_s, d_b_s, d_sln_g, d_sln_b = _sgu_bwd(z1, dmix1, sln_g, sln_b, c_w_s[0], b_rows, "sgu_bwd")
    dh2, d_mn1 = _proj_bwd_norm([(dzp, 0), (dzu, 512), (dzv, 1024)], wt_in1, h2, dh3, mix_norm[1:2], BF16, "in_proj1_bwd")
    gw_c_in = _mm_tn_pieces([dzp, dzu, dzv], hn1, "dw_in1")
    mix1_g, tok = _exchange_start([blocks(gw_c_in), blocks(gw_c_out)], True, "scatter_mixer1_start")
    dh1, dgate0, dup0, act0, d_fn0 = _ffn_bwd(dh2, h1, _behind([tok], ffn_norm[0:1]), gate0, up0, wtg0, wtu0, wd0, "ffn_bwd0")
    gw_ffn0 = [_mm_tn(dgate0, hnf0, "dw_gate0"), _mm_tn(dup0, hnf0, "dw_up0"), _mm_tn(act0, dh2, "dw_down0")]
    ffn0_g, tok = _exchange_start([blocks(g) for g in gw_ffn0], True, "scatter_ffn0_start")
    dmix0 = _dmix(dh1, _behind([tok], w_out0), "dmix0")
    gw_a_out = _perm_heads(_mm_tn_pieces([attn, conv], dh1, "dw_out0"), Q_INV, 0)
    dq, dkv, d_sink_row, d_bq, d_bkv = _attn_bwd(z0, dmix0, a_sinks, bsz, "attn_bwd")
    dca, dcg, d_conv_w, d_conv_b, d_cln_g, d_cln_b, d_ba, d_bg = _conv_bwd(z0, conv_y, dmix0, conv_w, cln_g, cln_b, bsz, "conv_bwd")
    gw_a_in = _in0_from_kernel(_mm_tn_pieces([dq, dca, dcg, dkv], hn0, "dw_in0"), 0)
    mix0_g, tok = _exchange_start([blocks(gw_a_in), blocks(gw_a_out)], True, "scatter_mixer0_start")
    dx, d_mn0 = _proj_bwd_norm([(dq, 0), (dca, 512), (dcg, 1024), (dkv, 1536)], wt_in0, h0, dh1,
                               _behind([tok], mix_norm[0:1]), F32, "in_proj0_bwd")
    d_b_in = _in0_from_kernel(jnp.concatenate([d_bq, d_ba, d_bg, d_bkv], axis=1), 1)

    rep = dict(mix_norm=jnp.concatenate([d_mn0, d_mn1], axis=0), a_b_in=d_b_in, a_sinks=d_sink_row[:, :8],
               a_conv_b=d_conv_b, a_cln_g=d_cln_g, a_cln_b=d_cln_b, c_w_pool=d_w_pool[None], c_w_s=d_w_s[None],
               c_b_s=d_b_s[None], ffn_norm=jnp.concatenate([d_fn0, d_fn1], axis=0), final_norm=d_final_norm.reshape(D))
    rep_flat = jnp.concatenate([rep[nm].reshape(-1) for nm in REP_NAMES] + [loss_part.reshape(1)])
    rep_flat = jnp.pad(rep_flat, (0, N_DEV * REP_ROWS * D - rep_flat.shape[0])).reshape(N_DEV, REP_ROWS, D)
    small_g = jnp.concatenate([
        d_conv_w[:CONV_K].reshape(31, N_DEV, 64).transpose(1, 0, 2).reshape(N_DEV, 31 * 64),
        d_pool_scale.reshape(N_DEV, 64), d_sln_g.reshape(N_DEV, 64), d_sln_b.reshape(N_DEV, 64)], axis=1)
    small_g = jnp.pad(small_g, ((0, 0), (0, G_SMALL_ROWS * D - SMALL_SHARD))).reshape(N_DEV, G_SMALL_ROWS, D)
    tail_g, tok = _exchange_start([jnp.concatenate([small_g, rep_flat], axis=1)], True, "scatter_tail_start")

    names = list(w_in)
    g_out, delta, new_m, new_v = {}, {}, {}, {}
    column_sharded = ("a_w_in", "c_w_in", "ffn_w_gate", "ffn_w_up")

    def rows_of(a, nm):
        return jnp.swapaxes(a, 1, 2) if nm in column_sharded else a

    def reduce_adamw(nm, landing, layer, into=None):
        args = [rows_of(d[nm], nm) for d in (w_in, m_in, v_in)]
        if args[0].shape[0] == 1:
            args, layer = [a[0] for a in args], None
        return _reduce_adamw(landing, *args, "adamw_%s_%s" % (nm, layer), layer=layer, into=into)

    def keep(nm, res):
        res = [r if r.ndim == 3 else r[None] for r in res]
        g_out[nm], delta[nm], new_m[nm], new_v[nm] = (rows_of(r, nm) for r in res)

    ffn_names = ("ffn_w_gate", "ffn_w_up", "ffn_w_down")
    landed = _exchange_wait(ffn1_g, tok, "scatter_ffn1_wait")
    ffn_res = [reduce_adamw(nm, a, 1) for nm, a in zip(ffn_names, landed)]
    landed = _exchange_wait(mix1_g, ffn_res[-1][0], "scatter_mixer1_wait")
    for nm, a in zip(("c_w_in", "c_w_out"), landed):
        keep(nm, reduce_adamw(nm, a, 0))
    landed = _exchange_wait(ffn0_g, g_out["c_w_out"], "scatter_ffn0_wait")
    for nm, a, res in zip(ffn_names, landed, ffn_res):
        keep(nm, reduce_adamw(nm, a, 0, into=res))
    landed = _exchange_wait(mix0_g, g_out["ffn_w_down"], "scatter_mixer0_wait")
    for nm, a in zip(("a_w_in", "a_w_out"), landed):
        keep(nm, reduce_adamw(nm, a, 0))
    g_tail = _sum_slabs(_exchange_wait(tail_g, g_out["a_w_out"], "scatter_tail_wait")[0], "sum_tail")
    rep_all = _all_gather(g_tail[G_SMALL_ROWS:], "gather_replicated_grads")[0].reshape(-1)
    small_r = g_tail[:G_SMALL_ROWS].reshape(-1)[:SMALL_SHARD]
    g_out.update(
        a_conv_w=small_r[:31 * 64].reshape(1, 31, 64), c_pool_scale=small_r[31 * 64:31 * 64 + 64].reshape(1, 64),
        c_sln_g=small_r[31 * 64 + 64:31 * 64 + 128].reshape(1, 64), c_sln_b=small_r[31 * 64 + 128:].reshape(1, 64))
    off = 0
    for nm in REP_NAMES:
        n = int(np.prod(w_in[nm].shape))
        g_out[nm] = rep_all[off:off + n].reshape(w_in[nm].shape)
        off += n
    loss = rep_all[off]
    for group, rows, label in ((("a_conv_w", "c_pool_scale", "c_sln_g", "c_sln_b"), G_SMALL_ROWS, "adamw_small_sharded"),
                               (REP_NAMES, N_DEV * REP_ROWS, "adamw_replicated")):
        flat = [_pad_rows(jnp.concatenate([d[nm].reshape(-1) for nm in group]), rows) for d in (w_in, g_out, m_in, v_in)]
        res = [r.reshape(-1) for r in _adamw(*flat, label)]
        off = 0
        for nm in group:
            n = int(np.prod(w_in[nm].shape))
            delta[nm], new_m[nm], new_v[nm] = (r[off:off + n].reshape(w_in[nm].shape) for r in res)
            off += n

    grad_x = dx.reshape(bsz, seq, D)
    return (loss, grad_x, *[g_out[nm] for nm in names], *[delta[nm] for nm in names],
            *[new_m[nm] for nm in names], *[new_v[nm] for nm in names])
```

```python
import functools

import jax
import jax.numpy as jnp
import numpy as np
from jax import lax
from jax.experimental import pallas as pl
from jax.experimental.pallas import tpu as pltpu

F32 = jnp.float32
BF16 = jnp.bfloat16
MESH = pl.DeviceIdType.MESH

D = 1024
N_DEV = 8
EPS = 1e-5
HEAD_PAIRS = 4
ATT_BLK = 128
CONV_K = 31
HALO = 32
D_FF = 2816
FF_TILE_FWD = D_FF // 2
FF_TILE_BWD = D_FF // 2
IN0 = 1792
IN1 = 1536
POOL_WINDOWS = (2, 4, 8, 16)
SGU_CHUNK = 128
GELU_C = 0.7978845608028654
GELU_A = 0.044715
ADAM_LR, ADAM_B1, ADAM_B2, ADAM_EPS, ADAM_WD, ADAM_STEP = 0.001, 0.9, 0.999, 1e-08, 0.01, 10
VMEM_LIMIT = 56 << 20

SMALL_SHARD = 31 * 64 + 3 * 64
W_MISC_ROWS = 16
G_MISC_ROWS = 32
G_SMALL_ROWS = 8
REP_ROWS = G_MISC_ROWS - G_SMALL_ROWS
REP_NAMES = ("mix_norm", "a_b_in", "a_sinks", "a_conv_b", "a_cln_g", "a_cln_b", "c_w_pool", "c_w_s", "c_b_s",
             "ffn_norm", "final_norm")
Q_PERM = (0, 4, 1, 5, 2, 6, 3, 7)


def _params(*sem):
    return pltpu.CompilerParams(dimension_semantics=sem, vmem_limit_bytes=VMEM_LIMIT)


def _nn(a, b):
    return jnp.dot(a, b, preferred_element_type=F32)


def _nt(a, b):
    return lax.dot_general(a, b, (((1,), (1,)), ((), ())), preferred_element_type=F32)


def _tn(a, b):
    return lax.dot_general(a, b, (((0,), (0,)), ((), ())), preferred_element_type=F32)


def _tile(n, want=512):
    t = min(want, n)
    assert n % t == 0, (n, t)
    return t


def _seq_tile(s):
    return 512 if s >= 1024 else s // 2


def _rms(x, g):
    r = lax.rsqrt(jnp.mean(x * x, axis=-1, keepdims=True) + EPS)
    return x * r * g, r


def _rms_bwd(x, g, d_y):
    r = lax.rsqrt(jnp.mean(x * x, axis=-1, keepdims=True) + EPS)
    xr = x * r
    u = d_y * g
    d_x = r * (u - xr * jnp.mean(u * xr, axis=-1, keepdims=True))
    return d_x, jnp.sum(d_y * xr, axis=0, keepdims=True)


def _ln(y, g, b):
    mu = jnp.mean(y, axis=-1, keepdims=True)
    yc = y - mu
    rstd = lax.rsqrt(jnp.mean(yc * yc, axis=-1, keepdims=True) + EPS)
    xhat = yc * rstd
    return xhat * g + b, xhat, rstd


def _ln_bwd(d_o, xhat, rstd, g):
    dxh = d_o * g
    return rstd * (dxh - jnp.mean(dxh, axis=-1, keepdims=True) - xhat * jnp.mean(dxh * xhat, axis=-1, keepdims=True))


def _gelu(x):
    th = jnp.tanh(GELU_C * (x + GELU_A * x * x * x))
    return 0.5 * x * (1.0 + th), th


def _gelu_grad(x, th):
    return 0.5 * (1.0 + th) + 0.5 * x * (1.0 - th * th) * GELU_C * (1.0 + 3.0 * GELU_A * x * x)


def _row(c):
    return pl.BlockSpec((1, c), lambda *_: (0, 0))


def _full(shape):
    return pl.BlockSpec(shape, lambda *_: (0,) * len(shape))


def _norm_proj(h, g, wt, bias, name):
    t, n = h.shape[0], wt.shape[0]
    tm = _tile(t)
    has_bias = bias is not None

    def body(*refs):
        h_ref, g_ref, wt_ref = refs[:3]
        z_ref, hn_ref = refs[-2:]
        hn = _rms(h_ref[...].astype(F32), g_ref[...])[0].astype(BF16)
        hn_ref[...] = hn
        z = _nt(hn, wt_ref[...])
        if has_bias:
            z = z + refs[3][...]
        z_ref[...] = z.astype(BF16)

    in_specs = [pl.BlockSpec((tm, D), lambda i: (i, 0)), _row(D), _full((n, D))]
    args = [h, g, wt]
    if has_bias:
        in_specs.append(_row(n))
        args.append(bias)
    return pl.pallas_call(
        body, name=name, grid=(t // tm,), in_specs=in_specs,
        out_specs=[pl.BlockSpec((tm, n), lambda i: (i, 0)), pl.BlockSpec((tm, D), lambda i: (i, 0))],
        out_shape=[jax.ShapeDtypeStruct((t, n), BF16), jax.ShapeDtypeStruct((t, D), BF16)],
        compiler_params=_params("parallel"))(*args)


def _ff_pieces(tf, width=256):
    return [(c0, min(width, tf - c0)) for c0 in range(0, tf, width)]


def _ffn_fwd(h_prev, a, b, w_out, g, wtg, wtu, wd, name):
    t = h_prev.shape[0]
    tm, tf = _tile(t), FF_TILE_FWD
    nf = D_FF // tf

    def body(hp_ref, a_ref, b_ref, wa_ref, wb_ref, g_ref, wtg_ref, wtu_ref, wd_ref,
             hmid_ref, o_ref, hn_ref, gate_ref, up_ref, acc, act):
        f = pl.program_id(1)

        @pl.when(f == 0)
        def _():
            x = hp_ref[...].astype(F32) + _nn(a_ref[...], wa_ref[...]) + _nn(b_ref[...], wb_ref[...])
            hmid_ref[...] = x.astype(BF16)
            hn_ref[...] = _rms(x, g_ref[...])[0].astype(BF16)
            acc[...] = x

        hn = hn_ref[...]
        for c0, cw in _ff_pieces(tf):
            rows, cols = slice(c0, c0 + cw), slice(c0, c0 + cw)
            gate = _nt(hn, wtg_ref[rows, :])
            up = _nt(hn, wtu_ref[rows, :])
            gate_ref[:, cols] = gate.astype(BF16)
            up_ref[:, cols] = up.astype(BF16)
            act[:, cols] = (gate * jax.nn.sigmoid(gate) * up).astype(BF16)
        acc[...] += _nn(act[...], wd_ref[...])

        @pl.when(f == nf - 1)
        def _():
            o_ref[...] = acc[...].astype(BF16)

    tok = pl.BlockSpec((tm, D), lambda i, f: (i, 0))
    wsp = pl.BlockSpec((tf, D), lambda i, f: (f, 0))
    mid = pl.BlockSpec((tm, tf), lambda i, f: (i, f))
    half = pl.BlockSpec((tm, 512), lambda i, f: (i, 0))
    res = jax.ShapeDtypeStruct((t, D), BF16)
    return pl.pallas_call(
        body, name=name, grid=(t // tm, nf),
        in_specs=[tok, half, half, pl.BlockSpec((512, D), lambda i, f: (0, 0)), pl.BlockSpec((512, D), lambda i, f: (1, 0)),
                  _row(D), wsp, wsp, wsp],
        out_specs=[tok, tok, tok, mid, mid],
        out_shape=[res, res, res, jax.ShapeDtypeStruct((t, D_FF), BF16), jax.ShapeDtypeStruct((t, D_FF), BF16)],
        scratch_shapes=[pltpu.VMEM((tm, D), F32), pltpu.VMEM((tm, tf), BF16)],
        compiler_params=_params("parallel", "arbitrary"))(h_prev, a, b, w_out, w_out, g, wtg, wtu, wd)


def _ffn_bwd(dh, h, g, gate, up, wtg, wtu, wd, w_out, name):
    t = h.shape[0]
    tm, tf = _tile(t), FF_TILE_BWD
    nf = D_FF // tf

    def body(dh_ref, h_ref, g_ref, gate_ref, up_ref, wtg_ref, wtu_ref, wd_ref, wout_ref,
             dhin_ref, dmix_ref, dgate_ref, dup_ref, act_ref, dg_ref, d_hn):
        i, f = pl.program_id(0), pl.program_id(1)

        @pl.when(f == 0)
        def _():
            d_hn[...] = jnp.zeros_like(d_hn)

        @pl.when((i == 0) & (f == 0))
        def _():
            dg_ref[...] = jnp.zeros_like(dg_ref)

        dh = dh_ref[...]
        for c0, cw in _ff_pieces(tf):
            cols = slice(c0, c0 + cw)
            da = _nt(dh, wd_ref[c0:c0 + cw, :])
            gt = gate_ref[:, cols].astype(F32)
            u = up_ref[:, cols].astype(F32)
            sg = jax.nn.sigmoid(gt)
            sil = gt * sg
            act_ref[:, cols] = (sil * u).astype(BF16)
            dup_ref[:, cols] = (da * sil).astype(BF16)
            dgate_ref[:, cols] = (da * u * sg * (1.0 + gt * (1.0 - sg))).astype(BF16)
        d_hn[...] += _nn(dgate_ref[...], wtg_ref[...]) + _nn(dup_ref[...], wtu_ref[...])

        @pl.when(f == nf - 1)
        def _():
            d_x, d_g = _rms_bwd(h_ref[...].astype(F32), g_ref[...], d_hn[...])
            dhin = (dh_ref[...].astype(F32) + d_x).astype(BF16)
            dhin_ref[...] = dhin
            dmix_ref[...] = _nt(dhin, wout_ref[...]).astype(BF16)
            dg_ref[...] += d_g

    tok = pl.BlockSpec((tm, D), lambda i, f: (i, 0))
    wsp = pl.BlockSpec((tf, D), lambda i, f: (f, 0))
    mid = pl.BlockSpec((tm, tf), lambda i, f: (i, f))
    mid_shape = jax.ShapeDtypeStruct((t, D_FF), BF16)
    res = jax.ShapeDtypeStruct((t, D), BF16)
    return pl.pallas_call(
        body, name=name, grid=(t // tm, nf), in_specs=[tok, tok, _row(D), mid, mid, wsp, wsp, wsp, _full((D, D))],
        out_specs=[tok, tok, mid, mid, mid, _row(D)],
        out_shape=[res, res, mid_shape, mid_shape, mid_shape, jax.ShapeDtypeStruct((1, D), F32)],
        scratch_shapes=[pltpu.VMEM((tm, D), F32)],
        compiler_params=_params("arbitrary", "arbitrary"))(dh, h, g, gate, up, wtg, wtu, wd, w_out)


def _proj_bwd_norm(pieces, wt, h, dh, g, dtype, name):
    t = h.shape[0]
    tm = _tile(t)
    n_p = len(pieces)

    def body(*refs):
        p_refs, w_refs = refs[:n_p], refs[n_p:2 * n_p]
        h_ref, dh_ref, g_ref, o_ref, dg_ref = refs[2 * n_p:]

        @pl.when(pl.program_id(0) == 0)
        def _():
            dg_ref[...] = jnp.zeros_like(dg_ref)

        d_hn = _nn(p_refs[0][...], w_refs[0][...])
        for p_ref, w_ref in zip(p_refs[1:], w_refs[1:]):
            d_hn = d_hn + _nn(p_ref[...], w_ref[...])
        d_x, d_g = _rms_bwd(h_ref[...].astype(F32), g_ref[...], d_hn)
        o_ref[...] = (dh_ref[...].astype(F32) + d_x).astype(dtype)
        dg_ref[...] += d_g

    tok = pl.BlockSpec((tm, D), lambda i: (i, 0))
    in_specs = [pl.BlockSpec((tm, a.shape[1]), lambda i: (i, 0)) for a, _ in pieces]
    for a, off in pieces:
        w = a.shape[1]
        assert off % w == 0
        in_specs.append(pl.BlockSpec((w, D), functools.partial(lambda i, blk: (blk, 0), blk=off // w)))
    in_specs += [tok, tok, _row(D)]
    return pl.pallas_call(
        body, name=name, grid=(t // tm,), in_specs=in_specs, out_specs=[tok, _row(D)],
        out_shape=[jax.ShapeDtypeStruct((t, D), dtype), jax.ShapeDtypeStruct((1, D), F32)],
        compiler_params=_params("arbitrary"))(*[a for a, _ in pieces], *([wt] * n_p), h, dh, g)


def _mm_tn(a, b, name):
    t, n = a.shape
    k = b.shape[1]
    tn = n if n <= 1024 else n // 2
    tt = _tile(t, 1024)
    nt = t // tt

    def body(a_ref, b_ref, o_ref, acc):
        s = pl.program_id(1)

        @pl.when(s == 0)
        def _():
            acc[...] = jnp.zeros_like(acc)

        acc[...] += _tn(a_ref[...], b_ref[...].astype(BF16))

        @pl.when(s == nt - 1)
        def _():
            o_ref[...] = acc[...].astype(BF16)

    return pl.pallas_call(
        body, name=name, grid=(n // tn, nt),
        in_specs=[pl.BlockSpec((tt, tn), lambda j, s: (s, j)), pl.BlockSpec((tt, k), lambda j, s: (s, 0))],
        out_specs=pl.BlockSpec((tn, k), lambda j, s: (j, 0)), out_shape=jax.ShapeDtypeStruct((n, k), BF16),
        scratch_shapes=[pltpu.VMEM((tn, k), F32)],
        compiler_params=_params("parallel", "arbitrary"))(a, b)


def _mm_tn_pieces(pieces, b, name):
    t, k = b.shape
    widths = [p.shape[1] for p in pieces]
    n, n_p = sum(widths), len(pieces)
    tt = _tile(t, 1024)
    nt = t // tt

    def body(*refs):
        b_ref, o_ref, acc = refs[n_p:]
        s = pl.program_id(0)

        @pl.when(s == 0)
        def _():
            acc[...] = jnp.zeros_like(acc)

        bb = b_ref[...].astype(BF16)
        off = 0
        for p_ref, w in zip(refs[:n_p], widths):
            acc[off:off + w, :] += _tn(p_ref[...], bb)
            off += w

        @pl.when(s == nt - 1)
        def _():
            o_ref[...] = acc[...].astype(BF16)

    return pl.pallas_call(
        body, name=name, grid=(nt,),
        in_specs=[pl.BlockSpec((tt, w), lambda s: (s, 0)) for w in widths] + [pl.BlockSpec((tt, k), lambda s: (s, 0))],
        out_specs=_full((n, k)), out_shape=jax.ShapeDtypeStruct((n, k), BF16),
        scratch_shapes=[pltpu.VMEM((n, k), F32)], compiler_params=_params("arbitrary"))(*pieces, b)


def _loss_head(h, g, target, name):
    t = h.shape[0]
    tm = _tile(t)

    def body(h_ref, g_ref, t_ref, dh_ref, dg_ref, loss_ref):
        @pl.when(pl.program_id(0) == 0)
        def _():
            dg_ref[...] = jnp.zeros_like(dg_ref)
            loss_ref[...] = jnp.zeros_like(loss_ref)

        x = h_ref[...].astype(F32)
        gv = g_ref[...]
        err = _rms(x, gv)[0] - t_ref[...]
        per_tok = jnp.mean(err * err, axis=-1, keepdims=True)
        loss_ref[...] += 0.5 * jnp.sum(per_tok, axis=0, keepdims=True)
        d_x, d_g = _rms_bwd(x, gv, err * (1.0 / D))
        dh_ref[...] = d_x.astype(BF16)
        dg_ref[...] += d_g

    tok = pl.BlockSpec((tm, D), lambda i: (i, 0))
    return pl.pallas_call(
        body, name=name, grid=(t // tm,), in_specs=[tok, _row(D), tok],
        out_specs=[tok, _row(D), _row(1)],
        out_shape=[jax.ShapeDtypeStruct((t, D), BF16), jax.ShapeDtypeStruct((1, D), F32), jax.ShapeDtypeStruct((1, 1), F32)],
        compiler_params=_params("arbitrary"))(h, g, target)


STACK = HEAD_PAIRS * ATT_BLK


def _attn_valid(first, rows):
    qi = lax.broadcasted_iota(jnp.int32, (rows, 2 * ATT_BLK), 0) % ATT_BLK
    r = lax.broadcasted_iota(jnp.int32, (rows, 2 * ATT_BLK), 1)
    dist = qi + ATT_BLK - r
    return (dist >= 0) & (dist < ATT_BLK) & ((r >= ATT_BLK) | jnp.logical_not(first))


def _stacked(ref, kh, scale):
    lo = lax.broadcasted_iota(jnp.int32, (ATT_BLK, 128), 1) < 64
    keep = lo if kh == 0 else ~lo
    parts = [jnp.where(keep, ref[:, g * 128:(g + 1) * 128] * scale, 0.0).astype(BF16) for g in range(HEAD_PAIRS)]
    return jnp.concatenate(parts, axis=0)


def _unstacked(a0, a1, g):
    lo = lax.broadcasted_iota(jnp.int32, (ATT_BLK, 128), 1) < 64
    rows = slice(g * ATT_BLK, (g + 1) * ATT_BLK)
    return jnp.where(lo, a0[rows], a1[rows])


def _sink_rows(s_ref, kh):
    return jnp.concatenate([jnp.full((ATT_BLK, 128), s_ref[0, kh * 4 + g], F32) for g in range(HEAD_PAIRS)], axis=0)


def _row_sums(a):
    hi = a.astype(BF16)
    lo = (a - hi.astype(F32)).astype(BF16)
    ones = jnp.ones((2 * ATT_BLK, 128), BF16)
    return _nn(hi, ones) + _nn(lo, ones)


def _both(a):
    return jnp.concatenate([a, a], axis=1)


def _attn_probs(qs, kpair, sink, valid):
    s = jnp.where(valid, _nt(qs, kpair), -1e30)
    m = jnp.maximum(jnp.broadcast_to(jnp.max(s, axis=-1, keepdims=True), (s.shape[0], 128)), sink)
    p = jnp.exp(s - _both(m))
    es = jnp.exp(sink - m)
    inv = 1.0 / (_row_sums(p) + es)
    return p * _both(inv), es * inv


def _attn_probs_head(qm, kpair, sink, valid):
    s = jnp.where(valid, _nt(qm, kpair), -1e30)
    m = jnp.maximum(jnp.max(s, axis=-1, keepdims=True), sink)
    p = jnp.exp(s - m)
    return p * (1.0 / (jnp.sum(p, axis=-1, keepdims=True) + jnp.exp(sink - m)))


def _attn_specs(bsz, order):
    q = pl.BlockSpec((bsz, ATT_BLK, 512), lambda j: (0, order(j), 0))
    kvc = pl.BlockSpec((bsz, ATT_BLK, 256), lambda j: (0, order(j), 6))
    kvp = pl.BlockSpec((bsz, ATT_BLK, 256), lambda j: (0, jnp.maximum(order(j) - 1, 0), 6))
    return q, kvc, kvp


def _window_kv(kvc_ref, kvp_ref):
    kvc, kvp = kvc_ref[...], kvp_ref[...]
    kpair = jnp.concatenate([kvp[:, :128], kvc[:, :128]], axis=0)
    vpair = jnp.concatenate([kvp[:, 128:], kvc[:, 128:]], axis=0)
    return kpair, vpair


def _attn_fwd(z0, sinks, bsz, name):
    t = z0.shape[0]
    seq = t // bsz
    nb = seq // ATT_BLK

    def body(s_ref, q_ref, kvc_ref, kvp_ref, o_ref):
        valid = _attn_valid(pl.program_id(0) == 0, ATT_BLK)
        lo = lax.broadcasted_iota(jnp.int32, (ATT_BLK, 128), 1) < 64
        for b in range(bsz):
            kpair, vpair = _window_kv(kvc_ref.at[b], kvp_ref.at[b])
            for g in range(HEAD_PAIRS):
                qs = q_ref[b, :, g * 128:(g + 1) * 128] * 0.125
                outs = []
                for kh in range(2):
                    qm = jnp.where(lo if kh == 0 else ~lo, qs, 0.0).astype(BF16)
                    p = _attn_probs_head(qm, kpair, s_ref[0, kh * 4 + g], valid)
                    outs.append(_nn(p.astype(BF16), vpair))
                o_ref[b, :, g * 128:(g + 1) * 128] = jnp.where(lo, outs[0], outs[1]).astype(BF16)

    q, kvc, kvp = _attn_specs(bsz, lambda j: j)
    z3 = z0.reshape(bsz, seq, z0.shape[1])
    return pl.pallas_call(
        body, name=name, grid=(nb,),
        in_specs=[pl.BlockSpec(memory_space=pltpu.SMEM), q, kvc, kvp],
        out_specs=pl.BlockSpec((bsz, ATT_BLK, 512), lambda j: (0, j, 0)),
        out_shape=jax.ShapeDtypeStruct((bsz, seq, 512), BF16),
        compiler_params=_params("parallel"))(sinks, z3, z3, z3).reshape(t, 512)


def _attn_bwd(z0, dmix, sinks, bsz, name):
    t = z0.shape[0]
    seq = t // bsz
    nb = seq // ATT_BLK

    def body(s_ref, q_ref, kvc_ref, kvp_ref, do_ref, dq_ref, dkv_ref, dsink_ref, dbq_ref, dbkv_ref, carry):
        j = pl.program_id(0)

        @pl.when(j == 0)
        def _():
            carry[...] = jnp.zeros_like(carry)
            dsink_ref[...] = jnp.zeros_like(dsink_ref)
            dbq_ref[...] = jnp.zeros_like(dbq_ref)
            dbkv_ref[...] = jnp.zeros_like(dbkv_ref)

        valid = _attn_valid(j == nb - 1, STACK)
        lane = lax.broadcasted_iota(jnp.int32, (1, 128), 1)
        dsink = jnp.zeros((1, 128), F32)
        dbq = [jnp.zeros((1, 128), F32)] * HEAD_PAIRS
        dbkv = jnp.zeros((1, 256), F32)
        for b in range(bsz):
            kpair, vpair = _window_kv(kvc_ref.at[b], kvp_ref.at[b])
            dk = jnp.zeros((2 * ATT_BLK, 128), F32)
            dv = jnp.zeros((2 * ATT_BLK, 128), F32)
            dqs = []
            for kh in range(2):
                qs = _stacked(q_ref.at[b], kh, 0.125)
                dos = _stacked(do_ref.at[b], kh, 1.0)
                p, ps = _attn_probs(qs, kpair, _sink_rows(s_ref, kh), valid)
                dp = _nt(dos, vpair)
                delta = _row_sums(p * dp)
                ds = (p * (dp - _both(delta))).astype(BF16)
                dqs.append(_nn(ds, kpair))
                dk = dk + _tn(ds, qs)
                dv = dv + _tn(p.astype(BF16), dos)
                psd = ps * delta
                for g in range(HEAD_PAIRS):
                    part = jnp.sum(psd[g * ATT_BLK:(g + 1) * ATT_BLK], axis=0, keepdims=True)
                    dsink = dsink - jnp.where(lane == kh * 4 + g, part, 0.0)
            for g in range(HEAD_PAIRS):
                dq = _unstacked(dqs[0], dqs[1], g) * 0.125
                dq_ref[b, :, g * 128:(g + 1) * 128] = dq.astype(BF16)
                dbq[g] = dbq[g] + jnp.sum(dq, axis=0, keepdims=True)
            dkv = jnp.concatenate([dk[ATT_BLK:], dv[ATT_BLK:]], axis=1) + carry[b]
            dkv_ref[b] = dkv.astype(BF16)
            dbkv = dbkv + jnp.sum(dkv, axis=0, keepdims=True)
            carry[b] = jnp.concatenate([dk[:ATT_BLK], dv[:ATT_BLK]], axis=1)
        dsink_ref[...] += dsink
        dbq_ref[...] += jnp.concatenate(dbq, axis=1)
        dbkv_ref[...] += dbkv

    q, kvc, kvp = _attn_specs(bsz, lambda j: nb - 1 - j)
    z3 = z0.reshape(bsz, seq, z0.shape[1])
    d3 = dmix.reshape(bsz, seq, dmix.shape[1])
    dq, dkv, dsink, dbq, dbkv = pl.pallas_call(
        body, name=name, grid=(nb,),
        in_specs=[pl.BlockSpec(memory_space=pltpu.SMEM), q, kvc, kvp,
                  pl.BlockSpec((bsz, ATT_BLK, 512), lambda j: (0, nb - 1 - j, 0))],
        out_specs=[pl.BlockSpec((bsz, ATT_BLK, 512), lambda j: (0, nb - 1 - j, 0)),
                   pl.BlockSpec((bsz, ATT_BLK, 256), lambda j: (0, nb - 1 - j, 0)), _row(128), _row(512), _row(256)],
        out_shape=[jax.ShapeDtypeStruct((bsz, seq, 512), BF16), jax.ShapeDtypeStruct((bsz, seq, 256), BF16),
                   jax.ShapeDtypeStruct((1, 128), F32), jax.ShapeDtypeStruct((1, 512), F32),
                   jax.ShapeDtypeStruct((1, 256), F32)],
        scratch_shapes=[pltpu.VMEM((bsz, ATT_BLK, 256), F32)],
        compiler_params=_params("arbitrary"))(sinks, z3, z3, z3, d3)
    return dq.reshape(t, 512), dkv.reshape(t, 256), dsink, dbq, dbkv


def _seq_specs(ts, nt, t, width, col):
    per = ts // HALO
    cur = pl.BlockSpec((ts, width), lambda b, i: (b * nt + i, col))
    prev = pl.BlockSpec((HALO, width), lambda b, i: (jnp.maximum((b * nt + i) * per - 1, 0), col))
    nxt = pl.BlockSpec((HALO, width), lambda b, i: (jnp.minimum((b * nt + i + 1) * per, t // HALO - 1), col))
    return prev, cur, nxt


SUB = 8
CONV_ROWS = 64


def _shifted_copies(src, sh, rows_first, rows_rest):
    for r in range(SUB):
        rows = rows_first if r == 0 else rows_rest
        sh[r, pl.ds(0, rows), :] = src[pl.ds(r, rows), :]


def _tap_sum(sh, w, offset, c0, rows):
    acc = None
    for k in range(CONV_K):
        o = offset(k)
        term = sh[o % SUB, pl.ds(c0 + o - o % SUB, rows), :] * w[k:k + 1, :]
        acc = term if acc is None else acc + term
    return acc


def _glu_rows(a_ref, g_ref, rows=slice(None)):
    return a_ref[rows, :].astype(F32) * jax.nn.sigmoid(g_ref[rows, :].astype(F32))


def _conv_fwd(z0, conv_w, conv_b, ln_g, ln_b, bsz, name):
    t = z0.shape[0]
    s = t // bsz
    ts = _seq_tile(s)
    nt = s // ts
    first = HALO - (CONV_K - 1)

    def body(ap_ref, ac_ref, gp_ref, gc_ref, w_ref, cb_ref, lg_ref, lb_ref, o_ref, y_ref, hbuf, sh):
        hbuf[0:HALO, :] = jnp.where(pl.program_id(1) > 0, _glu_rows(ap_ref, gp_ref), 0.0)
        hbuf[HALO:HALO + ts, :] = _glu_rows(ac_ref, gc_ref)
        _shifted_copies(hbuf, sh, ts + HALO, ts + HALO - SUB)
        w, cb, lg, lb = w_ref[...], cb_ref[...], lg_ref[...], lb_ref[...]
        for c0 in range(0, ts, CONV_ROWS):
            y = _tap_sum(sh, w, lambda k: first + k, c0, CONV_ROWS) + cb
            y_ref[c0:c0 + CONV_ROWS, :] = y
            o = _ln(y, lg, lb)[0]
            o_ref[c0:c0 + CONV_ROWS, :] = (o * jax.nn.sigmoid(o)).astype(BF16)

    ap, ac, _ = _seq_specs(ts, nt, t, 512, 1)
    gp, gc, _ = _seq_specs(ts, nt, t, 512, 2)
    tile = pl.BlockSpec((ts, 512), lambda b, i: (b * nt + i, 0))
    return pl.pallas_call(
        body, name=name, grid=(bsz, nt),
        in_specs=[ap, ac, gp, gc, _full((HALO, 512)), _row(512), _row(512), _row(512)],
        out_specs=[tile, tile],
        out_shape=[jax.ShapeDtypeStruct((t, 512), BF16), jax.ShapeDtypeStruct((t, 512), F32)],
        scratch_shapes=[pltpu.VMEM((HALO + ts, 512), F32), pltpu.VMEM((SUB, HALO + ts, 512), F32)],
        compiler_params=_params("parallel", "parallel"))(z0, z0, z0, z0, conv_w, conv_b, ln_g, ln_b)


def _conv_bwd(z0, y, dmix, conv_w, ln_g, ln_b, bsz, name):
    t = z0.shape[0]
    s = t // bsz
    ts = _seq_tile(s)
    nt = s // ts

    def body(ac_ref, gc_ref, yc_ref, yn_ref, dc_ref, dn_ref, w_ref, lg_ref, lb_ref,
             da_ref, dg_ref, dw_ref, dcb_ref, dlg_ref, dlb_ref, dba_ref, dbg_ref, hcur, dybuf, sh_dy):
        b, i = pl.program_id(0), pl.program_id(1)

        @pl.when((b == 0) & (i == 0))
        def _():
            for ref in (dw_ref, dcb_ref, dlg_ref, dlb_ref, dba_ref, dbg_ref):
                ref[...] = jnp.zeros_like(ref)

        w, lg, lb = w_ref[...], lg_ref[...], lb_ref[...]
        hcur[...] = _glu_rows(ac_ref, gc_ref)

        def d_conv_out(yv, dout):
            o, xhat, rstd = _ln(yv, lg, lb)
            sg_o = jax.nn.sigmoid(o)
            d_o = dout * sg_o * (1.0 + o * (1.0 - sg_o))
            return _ln_bwd(d_o, xhat, rstd, lg), d_o * xhat, d_o

        dlg = jnp.zeros((1, 512), F32)
        dlb = jnp.zeros((1, 512), F32)
        dcb = jnp.zeros((1, 512), F32)
        for c0 in range(0, ts, CONV_ROWS):
            rows = slice(c0, c0 + CONV_ROWS)
            dy, g_part, b_part = d_conv_out(yc_ref[rows, :], dc_ref[rows, :].astype(F32))
            dybuf[rows, :] = dy
            dlg = dlg + jnp.sum(g_part, axis=0, keepdims=True)
            dlb = dlb + jnp.sum(b_part, axis=0, keepdims=True)
            dcb = dcb + jnp.sum(dy, axis=0, keepdims=True)
        dn = jnp.where(i < nt - 1, dn_ref[...].astype(F32), 0.0)
        dybuf[ts:ts + HALO, :] = d_conv_out(yn_ref[...], dn)[0]
        dlg_ref[...] += dlg
        dlb_ref[...] += dlb
        dcb_ref[...] += dcb
        _shifted_copies(dybuf, sh_dy, ts + HALO - SUB, ts + HALO - SUB)

        for k in range(CONV_K):
            o = CONV_K - 1 - k
            prod = hcur[...] * sh_dy[o % SUB, pl.ds(o - o % SUB, ts), :]
            dw_ref[pl.ds(k, 1), :] += jnp.sum(prod, axis=0, keepdims=True)
        dba = jnp.zeros((1, 512), F32)
        dbg = jnp.zeros((1, 512), F32)
        for c0 in range(0, ts, CONV_ROWS):
            rows = slice(c0, c0 + CONV_ROWS)
            dh = _tap_sum(sh_dy, w, lambda k: CONV_K - 1 - k, c0, CONV_ROWS)
            a_c = ac_ref[rows, :].astype(F32)
            sg_c = jax.nn.sigmoid(gc_ref[rows, :].astype(F32))
            d_a = dh * sg_c
            d_g = dh * a_c * sg_c * (1.0 - sg_c)
            da_ref[rows, :] = d_a.astype(BF16)
            dg_ref[rows, :] = d_g.astype(BF16)
            dba = dba + jnp.sum(d_a, axis=0, keepdims=True)
            dbg = dbg + jnp.sum(d_g, axis=0, keepdims=True)
        dba_ref[...] += dba
        dbg_ref[...] += dbg

    _, ac, _ = _seq_specs(ts, nt, t, 512, 1)
    _, gc, _ = _seq_specs(ts, nt, t, 512, 2)
    _, yc, yn = _seq_specs(ts, nt, t, 512, 0)
    _, dc, dn = _seq_specs(ts, nt, t, 512, 1)
    tile = pl.BlockSpec((ts, 512), lambda b, i: (b * nt + i, 0))
    vec = jax.ShapeDtypeStruct((1, 512), F32)
    return pl.pallas_call(
        body, name=name, grid=(bsz, nt),
        in_specs=[ac, gc, yc, yn, dc, dn, _full((HALO, 512)), _row(512), _row(512)],
        out_specs=[tile, tile, _full((HALO, 512)), _row(512), _row(512), _row(512), _row(512), _row(512)],
        out_shape=[jax.ShapeDtypeStruct((t, 512), BF16), jax.ShapeDtypeStruct((t, 512), BF16),
                   jax.ShapeDtypeStruct((HALO, 512), F32), vec, vec, vec, vec, vec],
        scratch_shapes=[pltpu.VMEM((ts, 512), F32), pltpu.VMEM((ts + HALO, 512), F32),
                        pltpu.VMEM((SUB, HALO + ts, 512), F32)],
        compiler_params=_params("arbitrary", "arbitrary"))(z0, z0, y, y, dmix, dmix, conv_w, ln_g, ln_b)


def _pooled(pbuf, g, ts, tok):
    w = 2 << g
    cols = slice(128 * g, 128 * (g + 1))
    sm = pbuf[pl.ds(HALO, ts), cols]
    for d in range(1, w):
        sm = sm + pbuf[pl.ds(HALO - d, ts), cols]
    cnt = jnp.minimum(tok + 1, w).astype(F32)
    return sm / cnt - pbuf[pl.ds(HALO, ts), cols]


def _pool_fwd(z1, w_pool, scale, bsz, name):
    t = z1.shape[0]
    s = t // bsz
    ts = _seq_tile(s)
    nt = s // ts

    def body(zp_ref, zc_ref, wp_ref, sc_ref, o_ref, pbuf):
        i = pl.program_id(1)
        pbuf[0:HALO, :] = jnp.where(i > 0, zp_ref[...].astype(F32), 0.0)
        pbuf[HALO:HALO + ts, :] = zc_ref[...].astype(F32)
        tok = i * ts + lax.broadcasted_iota(jnp.int32, (ts, 1), 0)
        for g in range(4):
            cols = slice(128 * g, 128 * (g + 1))
            pooled = _pooled(pbuf, g, ts, tok).astype(BF16)
            o_ref[:, cols] = (_nn(pooled, wp_ref[g].astype(BF16)) * sc_ref[:, cols]).astype(BF16)

    zp, zc, _ = _seq_specs(ts, nt, t, 512, 0)
    return pl.pallas_call(
        body, name=name, grid=(bsz, nt), in_specs=[zp, zc, _full((4, 128, 128)), _row(512)],
        out_specs=pl.BlockSpec((ts, 512), lambda b, i: (b * nt + i, 0)),
        out_shape=jax.ShapeDtypeStruct((t, 512), BF16),
        scratch_shapes=[pltpu.VMEM((HALO + ts, 512), F32)],
        compiler_params=_params("parallel", "parallel"))(z1, z1, w_pool, scale)


def _pool_bwd(z1, dmix, w_pool, scale, bsz, name):
    t = z1.shape[0]
    s = t // bsz
    ts = _seq_tile(s)
    nt = s // ts
    rr = ts + HALO

    def body(zp_ref, zc_ref, dc_ref, dn_ref, wp_ref, sc_ref, dz_ref, dwp_ref, dsc_ref, pbuf, ebuf):
        b, i = pl.program_id(0), pl.program_id(1)

        @pl.when((b == 0) & (i == 0))
        def _():
            dwp_ref[...] = jnp.zeros_like(dwp_ref)
            dsc_ref[...] = jnp.zeros_like(dsc_ref)

        pbuf[0:HALO, :] = jnp.where(i > 0, zp_ref[...].astype(F32), 0.0)
        pbuf[HALO:HALO + ts, :] = zc_ref[...].astype(F32)
        dn = jnp.where(i < nt - 1, dn_ref[...].astype(F32), 0.0)
        dout = jnp.concatenate([dc_ref[...].astype(F32), dn], axis=0)
        tok = i * ts + lax.broadcasted_iota(jnp.int32, (ts, 1), 0)
        tok_r = i * ts + lax.broadcasted_iota(jnp.int32, (rr, 1), 0)
        for g in range(4):
            w = 2 << g
            cols = slice(128 * g, 128 * (g + 1))
            wg = wp_ref[g].astype(BF16)
            pooled = _pooled(pbuf, g, ts, tok).astype(BF16)
            dsc_ref[:, cols] += jnp.sum(dout[:ts, cols] * _nn(pooled, wg), axis=0, keepdims=True)
            dy = (dout[:, cols] * sc_ref[:, cols]).astype(BF16)
            dwp_ref[g] += _tn(pooled, dy[:ts])
            dpl = _nt(dy, wg)
            ebuf[...] = dpl / jnp.minimum(tok_r + 1, w).astype(F32)
            dz = ebuf[pl.ds(0, ts), :] - dpl[:ts]
            for d in range(1, w):
                dz = dz + ebuf[pl.ds(d, ts), :]
            dz_ref[:, cols] = dz.astype(BF16)

    zp, zc, _ = _seq_specs(ts, nt, t, 512, 0)
    _, dc, dn = _seq_specs(ts, nt, t, 512, 0)
    return pl.pallas_call(
        body, name=name, grid=(bsz, nt), in_specs=[zp, zc, dc, dn, _full((4, 128, 128)), _row(512)],
        out_specs=[pl.BlockSpec((ts, 512), lambda b, i: (b * nt + i, 0)), _full((4, 128, 128)), _row(512)],
        out_shape=[jax.ShapeDtypeStruct((t, 512), BF16), jax.ShapeDtypeStruct((4, 128, 128), F32),
                   jax.ShapeDtypeStruct((1, 512), F32)],
        scratch_shapes=[pltpu.VMEM((HALO + ts, 512), F32), pltpu.VMEM((rr, 128), F32)],
        compiler_params=_params("arbitrary", "arbitrary"))(z1, z1, dmix, dmix, w_pool, scale)


def _tril():
    r = lax.broadcasted_iota(jnp.int32, (SGU_CHUNK, SGU_CHUNK), 0)
    c = lax.broadcasted_iota(jnp.int32, (SGU_CHUNK, SGU_CHUNK), 1)
    return r >= c


def _sgu_fwd(z1, ln_g, ln_b, w_s, b_rows, name):
    t = z1.shape[0]
    ts = _tile(t)

    def body(zu_ref, zv_ref, lg_ref, lb_ref, ws_ref, bs_ref, o_ref):
        v = _gelu(zv_ref[...].astype(F32))[0]
        vb = _ln(v, lg_ref[...], lb_ref[...])[0].astype(BF16)
        tril = _tril()
        for g in range(4):
            cols = slice(128 * g, 128 * (g + 1))
            wg = jnp.where(tril, ws_ref[g], 0.0).astype(BF16)
            for c in range(ts // SGU_CHUNK):
                rows = slice(SGU_CHUNK * c, SGU_CHUNK * (c + 1))
                mixed = _nn(wg, vb[rows, cols]) + bs_ref[g]
                o_ref[rows, cols] = (_gelu(zu_ref[rows, cols].astype(F32))[0] * mixed).astype(BF16)

    return pl.pallas_call(
        body, name=name, grid=(t // ts,),
        in_specs=[pl.BlockSpec((ts, 512), lambda i: (i, 1)), pl.BlockSpec((ts, 512), lambda i: (i, 2)),
                  _row(512), _row(512), _full((4, 128, 128)), _full((4, 128, 128))],
        out_specs=pl.BlockSpec((ts, 512), lambda i: (i, 0)), out_shape=jax.ShapeDtypeStruct((t, 512), BF16),
        compiler_params=_params("parallel"))(z1, z1, ln_g, ln_b, w_s, b_rows)


def _sgu_bwd(z1, dmix, ln_g, ln_b, w_s, b_rows, name):
    t = z1.shape[0]
    ts = _tile(t)

    def body(zu_ref, zv_ref, d_ref, lg_ref, lb_ref, ws_ref, bs_ref,
             dzu_ref, dzv_ref, dws_ref, dbs_ref, dlg_ref, dlb_ref, dvbuf):
        @pl.when(pl.program_id(0) == 0)
        def _():
            for ref in (dws_ref, dbs_ref, dlg_ref, dlb_ref):
                ref[...] = jnp.zeros_like(ref)

        zv = zv_ref[...].astype(F32)
        v, thv = _gelu(zv)
        lg = lg_ref[...]
        vln, xhat, rstd = _ln(v, lg, lb_ref[...])
        vb = vln.astype(BF16)
        tril = _tril()
        for g in range(4):
            cols = slice(128 * g, 128 * (g + 1))
            wg = jnp.where(tril, ws_ref[g], 0.0).astype(BF16)
            dws = jnp.zeros((SGU_CHUNK, SGU_CHUNK), F32)
            dbs = jnp.zeros((1, SGU_CHUNK), F32)
            for c in range(ts // SGU_CHUNK):
                rows = slice(SGU_CHUNK * c, SGU_CHUNK * (c + 1))
                vbc = vb[rows, cols]
                mixed = _nn(wg, vbc) + bs_ref[g]
                zu = zu_ref[rows, cols].astype(F32)
                u, thu = _gelu(zu)
                dout = d_ref[rows, cols].astype(F32)
                dzu_ref[rows, cols] = (dout * mixed * _gelu_grad(zu, thu)).astype(BF16)
                dm = dout * u
                dmb = dm.astype(BF16)
                dws = dws + _nt(dmb, vbc)
                dbs = dbs + jnp.sum(dm.T, axis=0, keepdims=True)
                dvbuf[rows, cols] = _tn(wg, dmb)
            dws_ref[g] += jnp.where(tril, dws, 0.0)
            dbs_ref[pl.ds(g, 1), :] += dbs
        dvln = dvbuf[...]
        dlg_ref[...] += jnp.sum(dvln * xhat, axis=0, keepdims=True)
        dlb_ref[...] += jnp.sum(dvln, axis=0, keepdims=True)
        dzv_ref[...] = (_ln_bwd(dvln, xhat, rstd, lg) * _gelu_grad(zv, thv)).astype(BF16)

    tile = pl.BlockSpec((ts, 512), lambda i: (i, 0))
    vec = jax.ShapeDtypeStruct((1, 512), F32)
    return pl.pallas_call(
        body, name=name, grid=(t // ts,),
        in_specs=[pl.BlockSpec((ts, 512), lambda i: (i, 1)), pl.BlockSpec((ts, 512), lambda i: (i, 2)),
                  pl.BlockSpec((ts, 512), lambda i: (i, 1)), _row(512), _row(512), _full((4, 128, 128)),
                  _full((4, 128, 128))],
        out_specs=[tile, tile, _full((4, 128, 128)), _full((4, 128)), _row(512), _row(512)],
        out_shape=[jax.ShapeDtypeStruct((t, 512), BF16), jax.ShapeDtypeStruct((t, 512), BF16),
                   jax.ShapeDtypeStruct((4, 128, 128), F32), jax.ShapeDtypeStruct((4, 128), F32), vec, vec],
        scratch_shapes=[pltpu.VMEM((ts, 512), F32)],
        compiler_params=_params("arbitrary"))(z1, z1, dmix, ln_g, ln_b, w_s, b_rows)


def _row_tile(r):
    for cand in (512, 352, 256, 192, 128, 64, 32, 16, 8):
        if r % cand == 0:
            return cand
    return r


def _sum_slabs(a, name):
    k, r, c = a.shape
    tr = _row_tile(r)

    def body(*refs):
        acc = refs[0][...].astype(F32)
        for ref in refs[1:-1]:
            acc = acc + ref[...].astype(F32)
        refs[-1][...] = acc

    in_specs = [pl.BlockSpec((None, tr, c), functools.partial(lambda i, s: (s, i, 0), s=s)) for s in range(k)]
    return pl.pallas_call(
        body, name=name, grid=(r // tr,), in_specs=in_specs, out_specs=pl.BlockSpec((tr, c), lambda i: (i, 0)),
        out_shape=jax.ShapeDtypeStruct((r, c), F32), compiler_params=_params("parallel"))(*([a] * k))


def _adamw_math(w, g, m, v):
    mn = ADAM_B1 * m + (1.0 - ADAM_B1) * g
    vn = ADAM_B2 * v + (1.0 - ADAM_B2) * (g * g)
    m_hat = mn / (1.0 - ADAM_B1 ** ADAM_STEP)
    v_hat = vn / (1.0 - ADAM_B2 ** ADAM_STEP)
    return -ADAM_LR * (m_hat / (jnp.sqrt(v_hat) + ADAM_EPS) + ADAM_WD * w), mn, vn


def _reduce_adamw(landing, w, m, v, name, layer=None, into=None):
    k, r, c = landing.shape
    tr = _row_tile(r)
    n_into = 0 if into is None else 4

    def body(*refs):
        slabs, (w_ref, m_ref, v_ref) = refs[:k], refs[k:k + 3]
        g_ref, d_ref, mo_ref, vo_ref = refs[k + 3 + n_into:]
        g = slabs[0][...].astype(F32)
        for ref in slabs[1:]:
            g = g + ref[...].astype(F32)
        g_ref[...] = g
        d_ref[...], mo_ref[...], vo_ref[...] = _adamw_math(w_ref[...], g, m_ref[...], v_ref[...])

    if layer is None:
        spec = pl.BlockSpec((tr, c), lambda i: (i, 0))
    else:
        spec = pl.BlockSpec((None, tr, c), lambda i: (layer, i, 0))
    in_specs = [pl.BlockSpec((None, tr, c), functools.partial(lambda i, s: (s, i, 0), s=s)) for s in range(k)]
    in_specs += [spec] * 3 + [ANY] * n_into
    shape = jax.ShapeDtypeStruct(w.shape, F32)
    return pl.pallas_call(
        body, name=name, grid=(r // tr,), in_specs=in_specs, out_specs=[spec] * 4, out_shape=[shape] * 4,
        input_output_aliases={k + 3 + j: j for j in range(n_into)},
        compiler_params=_params("parallel"))(*([landing] * k), w, m, v, *(into or ()))


def _adamw(w, g, m, v, name):
    r, c = w.shape
    tr = _row_tile(r)

    def body(w_ref, g_ref, m_ref, v_ref, d_ref, mo_ref, vo_ref):
        d_ref[...], mo_ref[...], vo_ref[...] = _adamw_math(w_ref[...], g_ref[...], m_ref[...], v_ref[...])

    spec = pl.BlockSpec((tr, c), lambda i: (i, 0))
    shape = jax.ShapeDtypeStruct((r, c), F32)
    return pl.pallas_call(
        body, name=name, grid=(r // tr,), in_specs=[spec] * 4, out_specs=[spec] * 3, out_shape=[shape] * 3,
        compiler_params=_params("parallel"))(w, g, m, v)


ANY = pl.BlockSpec(memory_space=pl.ANY)


def _all_gather(block, name):
    r, c_dim = block.shape

    def body(x_ref, out_ref, token, send_sems, recv_sems, local_sem):
        token[...] = jnp.zeros_like(token)
        x, y, c = lax.axis_index("x"), lax.axis_index("y"), lax.axis_index("c")
        me, sibling = (x, y, c), (x, y, 1 - c)
        chips = [(1 - x, y), (x, 1 - y), (1 - x, 1 - y)]

        def rows(px, py, pc):
            return out_ref.at[4 * px + 2 * py + pc]

        def copy(k, blk, to, src=None):
            return pltpu.make_async_remote_copy(
                src_ref=rows(*blk) if src is None else src, dst_ref=rows(*blk), send_sem=send_sems.at[k],
                recv_sem=recv_sems.at[k], device_id=to, device_id_type=MESH)

        mine = pltpu.make_async_copy(x_ref, rows(*me), local_sem)
        mine.start()
        first = [copy(0, me, sibling, src=x_ref)]
        first += [copy(1 + j, me, (*chip, c), src=x_ref) for j, chip in enumerate(chips)]
        for cp in first:
            cp.start()
        passed = [copy(4 + j, (*chip, c), sibling) for j, chip in enumerate(chips)]
        for j, chip in enumerate(chips):
            copy(1 + j, (*chip, c), me).wait_recv()
            passed[j].start()
        copy(0, sibling, me).wait_recv()
        for j, chip in enumerate(chips):
            copy(4 + j, (*chip, 1 - c), me).wait_recv()
        for cp in first + passed:
            cp.wait_send()
        mine.wait()

    return pl.pallas_call(
        body, name=name, in_specs=[ANY], out_specs=[ANY, pl.BlockSpec(memory_space=pltpu.VMEM)],
        out_shape=[jax.ShapeDtypeStruct((N_DEV, r, c_dim), block.dtype), jax.ShapeDtypeStruct((8, 128), F32)],
        scratch_shapes=[pltpu.SemaphoreType.DMA((7,)), pltpu.SemaphoreType.DMA((7,)), pltpu.SemaphoreType.DMA],
    )(block)


HBM = pl.BlockSpec(memory_space=pltpu.HBM)
SEM = pl.BlockSpec(memory_space=pltpu.SEMAPHORE)
EFFECT = pltpu.SideEffectType.DATAFLOW_SIDE_EFFECTING


def _exchange_copies(scatter, src_refs, land_refs, send_sems, recv_sems, local_sems):
    x, y, c = lax.axis_index("x"), lax.axis_index("y"), lax.axis_index("c")
    me = 4 * x + 2 * y + c
    sends, arrivals, locals_ = [], [], []
    for a, (src, land) in enumerate(zip(src_refs, land_refs)):
        def pick(idx, src=src):
            return src.at[idx] if scatter else src

        locals_.append(pltpu.make_async_copy(pick(me), land.at[me], local_sems.at[a]))
        for r in range(1, N_DEV):
            px = 1 - x if r & 4 else x
            py = 1 - y if r & 2 else y
            pc = 1 - c if r & 1 else c
            peer, s = 4 * px + 2 * py + pc, 7 * a + r - 1
            sends.append(pltpu.make_async_remote_copy(
                src_ref=pick(peer), dst_ref=land.at[me], send_sem=send_sems.at[s], recv_sem=recv_sems.at[s],
                device_id=(px, py, pc), device_id_type=MESH))
            arrivals.append(pltpu.make_async_remote_copy(
                src_ref=pick(peer), dst_ref=land.at[peer], send_sem=send_sems.at[s], recv_sem=recv_sems.at[s],
                device_id=(px, py, pc), device_id_type=MESH))
    return sends, arrivals, locals_


def _exchange_start(srcs, scatter, name):
    n = len(srcs)
    lands = [lax.empty((N_DEV,) + s.shape[-2:], s.dtype) for s in srcs]

    def body(*refs):
        src_refs, land_refs = refs[:n], refs[n:2 * n]
        send_sems, recv_sems, local_sems = refs[2 * n:2 * n + 3]
        token = refs[-1]
        sends, _, locals_ = _exchange_copies(scatter, src_refs, land_refs, send_sems, recv_sems, local_sems)
        for cp in locals_ + sends:
            cp.start()
        token[...] = jnp.zeros_like(token)

    res = pl.pallas_call(
        body, name=name,
        out_shape=[pltpu.SemaphoreType.DMA((7 * n,)), pltpu.SemaphoreType.DMA((7 * n,)), pltpu.SemaphoreType.DMA((n,))]
        + [pltpu.HBM(a.shape, a.dtype) for a in list(srcs) + lands] + [jax.ShapeDtypeStruct((8, 128), F32)],
        in_specs=[HBM] * (2 * n), out_specs=[SEM] * 3 + [HBM] * (2 * n) + [pl.BlockSpec(memory_space=pltpu.VMEM)],
        input_output_aliases={i: 3 + i for i in range(2 * n)},
        compiler_params=pltpu.CompilerParams(has_side_effects=EFFECT),
    )(*[pltpu.with_memory_space_constraint(a, pltpu.HBM) for a in list(srcs) + lands])
    return (n, scatter, res[:3], res[3:3 + 2 * n]), res[-1]


def _exchange_wait(handle, after, name):
    n, scatter, sems, thru = handle

    def body(*refs):
        src_refs, land_refs = refs[:n], refs[n:2 * n]
        send_sems, recv_sems, local_sems = refs[2 * n:2 * n + 3]
        sends, arrivals, locals_ = _exchange_copies(scatter, src_refs, land_refs, send_sems, recv_sems, local_sems)
        for cp in arrivals:
            cp.wait_recv()
        for cp in sends:
            cp.wait_send()
        for cp in locals_:
            cp.wait()

    res = pl.pallas_call(
        body, name=name, out_shape=[pltpu.HBM(a.shape, a.dtype) for a in thru],
        in_specs=[HBM] * (2 * n) + [SEM] * 3 + [ANY], out_specs=[HBM] * (2 * n),
        input_output_aliases={i: i for i in range(2 * n)},
        compiler_params=pltpu.CompilerParams(has_side_effects=EFFECT),
    )(*thru, *sems, after)
    return res[n:]


def _behind(tokens, a):
    zero = sum(tok[0, 0] for tok in tokens)
    return jax.tree.map(lambda v: v + zero.astype(v.dtype), a)


def _perm_heads(a, perm, axis):
    idx = [slice(None)] * a.ndim
    parts = []
    for h in perm:
        idx[axis] = slice(64 * h, 64 * (h + 1))
        parts.append(a[tuple(idx)])
    idx[axis] = slice(512, None)
    if a.shape[axis] > 512:
        parts.append(a[tuple(idx)])
    return jnp.concatenate(parts, axis=axis)


Q_INV = tuple(int(i) for i in np.argsort(Q_PERM))


def _in0_to_kernel(a, axis):
    a = _perm_heads(a, Q_PERM, axis)
    idx = [slice(None)] * a.ndim

    def cut(lo, hi):
        idx[axis] = slice(lo, hi)
        return a[tuple(idx)]

    return jnp.concatenate([cut(0, 512), cut(768, 1792), cut(512, 768)], axis=axis)


def _in0_from_kernel(a, axis):
    idx = [slice(None)] * a.ndim

    def cut(lo, hi):
        idx[axis] = slice(lo, hi)
        return a[tuple(idx)]

    a = jnp.concatenate([cut(0, 512), cut(1536, 1792), cut(512, 1536)], axis=axis)
    return _perm_heads(a, Q_INV, axis)


def _f32_as_u16_rows(v, rows):
    bits = lax.bitcast_convert_type(v, jnp.uint16).reshape(-1)
    return jnp.pad(bits, (0, rows * D - bits.shape[0])).reshape(rows, D)


def _pad_rows(v, rows):
    v = v.reshape(-1)
    return jnp.pad(v, (0, rows * D - v.shape[0])).reshape(rows, D)


def kernel(x, mix_norm, a_w_in, a_b_in, a_sinks, a_conv_w, a_conv_b, a_cln_g, a_cln_b, a_w_out, c_w_in, c_w_pool, c_pool_scale, c_sln_g, c_sln_b, c_w_s, c_b_s, c_w_out, ffn_norm, ffn_w_gate, ffn_w_up, ffn_w_down, final_norm, loss_target, m_mix_norm, m_a_w_in, m_a_b_in, m_a_sinks, m_a_conv_w, m_a_conv_b, m_a_cln_g, m_a_cln_b, m_a_w_out, m_c_w_in, m_c_w_pool, m_c_pool_scale, m_c_sln_g, m_c_sln_b, m_c_w_s, m_c_b_s, m_c_w_out, m_ffn_norm, m_ffn_w_gate, m_ffn_w_up, m_ffn_w_down, m_final_norm, v_mix_norm, v_a_w_in, v_a_b_in, v_a_sinks, v_a_conv_w, v_a_conv_b, v_a_cln_g, v_a_cln_b, v_a_w_out, v_c_w_in, v_c_w_pool, v_c_pool_scale, v_c_sln_g, v_c_sln_b, v_c_w_s, v_c_b_s, v_c_w_out, v_ffn_norm, v_ffn_w_gate, v_ffn_w_up, v_ffn_w_down, v_final_norm):
    bsz, seq, _ = x.shape
    t = bsz * seq
    w_in = dict(mix_norm=mix_norm, a_w_in=a_w_in, a_b_in=a_b_in, a_sinks=a_sinks, a_conv_w=a_conv_w, a_conv_b=a_conv_b,
                a_cln_g=a_cln_g, a_cln_b=a_cln_b, a_w_out=a_w_out, c_w_in=c_w_in, c_w_pool=c_w_pool,
                c_pool_scale=c_pool_scale, c_sln_g=c_sln_g, c_sln_b=c_sln_b, c_w_s=c_w_s, c_b_s=c_b_s, c_w_out=c_w_out,
                ffn_norm=ffn_norm, ffn_w_gate=ffn_w_gate, ffn_w_up=ffn_w_up, ffn_w_down=ffn_w_down, final_norm=final_norm)
    m_in = dict(mix_norm=m_mix_norm, a_w_in=m_a_w_in, a_b_in=m_a_b_in, a_sinks=m_a_sinks, a_conv_w=m_a_conv_w,
                a_conv_b=m_a_conv_b, a_cln_g=m_a_cln_g, a_cln_b=m_a_cln_b, a_w_out=m_a_w_out, c_w_in=m_c_w_in,
                c_w_pool=m_c_w_pool, c_pool_scale=m_c_pool_scale, c_sln_g=m_c_sln_g, c_sln_b=m_c_sln_b, c_w_s=m_c_w_s,
                c_b_s=m_c_b_s, c_w_out=m_c_w_out, ffn_norm=m_ffn_norm, ffn_w_gate=m_ffn_w_gate, ffn_w_up=m_ffn_w_up,
                ffn_w_down=m_ffn_w_down, final_norm=m_final_norm)
    v_in = dict(mix_norm=v_mix_norm, a_w_in=v_a_w_in, a_b_in=v_a_b_in, a_sinks=v_a_sinks, a_conv_w=v_a_conv_w,
                a_conv_b=v_a_conv_b, a_cln_g=v_a_cln_g, a_cln_b=v_a_cln_b, a_w_out=v_a_w_out, c_w_in=v_c_w_in,
                c_w_pool=v_c_w_pool, c_pool_scale=v_c_pool_scale, c_sln_g=v_c_sln_g, c_sln_b=v_c_sln_b, c_w_s=v_c_w_s,
                c_b_s=v_c_b_s, c_w_out=v_c_w_out, ffn_norm=v_ffn_norm, ffn_w_gate=v_ffn_w_gate, ffn_w_up=v_ffn_w_up,
                ffn_w_down=v_ffn_w_down, final_norm=v_final_norm)

    small = jnp.concatenate([a_conv_w[0].reshape(-1), c_pool_scale[0], c_sln_g[0], c_sln_b[0]])
    first_bits = lax.bitcast_convert_type(jnp.concatenate([a_w_in[0].T, a_w_out[0]], axis=0).astype(BF16), jnp.uint16)
    gathered, tok = _all_gather(jnp.concatenate([first_bits, _f32_as_u16_rows(small, W_MISC_ROWS)], axis=0), "gather_mixer0")

    def ffn_shards(l):
        return [ffn_w_gate[l].T.astype(BF16), ffn_w_up[l].T.astype(BF16), ffn_w_down[l].astype(BF16)]

    ffn0_h, tok = _exchange_start(_behind([tok], ffn_shards(0)), False, "gather_ffn0_start")
    mix1_h, tok = _exchange_start(_behind([tok], [c_w_in[0].T.astype(BF16), c_w_out[0].astype(BF16)]), False,
                                  "gather_mixer1_start")
    ffn1_h, tok = _exchange_start(_behind([tok], ffn_shards(1)), False, "gather_ffn1_start")

    a_in_full = lax.bitcast_convert_type(gathered[:, :224].reshape(IN0, D), BF16)
    a_out_full = lax.bitcast_convert_type(gathered[:, 224:352].reshape(D, D), BF16)
    small_all = lax.bitcast_convert_type(
        gathered[:, 352:].reshape(N_DEV, -1)[:, :2 * SMALL_SHARD].reshape(N_DEV, SMALL_SHARD, 2), F32)
    conv_w = small_all[:, :31 * 64].reshape(N_DEV, 31, 64).transpose(1, 0, 2).reshape(31, 512)
    conv_w = jnp.pad(conv_w, ((0, HALO - CONV_K), (0, 0)))
    pool_scale = small_all[:, 31 * 64:31 * 64 + 64].reshape(1, 512)
    sln_g = small_all[:, 31 * 64 + 64:31 * 64 + 128].reshape(1, 512)
    sln_b = small_all[:, 31 * 64 + 128:].reshape(1, 512)

    wt_in0 = _in0_to_kernel(a_in_full, 0)
    b_in0 = _in0_to_kernel(a_b_in, 1)
    w_out0 = _perm_heads(a_out_full, Q_PERM, 0)
    b_rows = jnp.broadcast_to(c_b_s[0][:, :, None], (4, 128, 128))
    conv_b, cln_g, cln_b = a_conv_b, a_cln_g, a_cln_b

    h0 = x.reshape(t, D)
    target = loss_target.reshape(t, D)
    z0, hn0 = _norm_proj(h0, _behind([tok], mix_norm[0:1]), wt_in0, b_in0, "in_proj0")
    attn = _attn_fwd(z0, a_sinks, bsz, "attn_fwd")
    conv, conv_y = _conv_fwd(z0, conv_w, conv_b, cln_g, cln_b, bsz, "conv_fwd")
    wtg0, wtu0, wd0 = (w.reshape(D_FF, D) for w in _exchange_wait(ffn0_h, attn, "gather_ffn0_wait"))
    h1, h2, hnf0, gate0, up0 = _ffn_fwd(h0, attn, conv, w_out0, ffn_norm[0:1], wtg0, wtu0, wd0, "ffn_fwd0")
    wt_in1, w_out1 = (w.reshape(-1, D) for w in _exchange_wait(mix1_h, h2, "gather_mixer1_wait"))
    z1, hn1 = _norm_proj(h2, mix_norm[1:2], wt_in1, None, "in_proj1")
    pool = _pool_fwd(z1, c_w_pool[0], pool_scale, bsz, "pool_fwd")
    sgu = _sgu_fwd(z1, sln_g, sln_b, c_w_s[0], b_rows, "sgu_fwd")
    wtg1, wtu1, wd1 = (w.reshape(D_FF, D) for w in _exchange_wait(ffn1_h, sgu, "gather_ffn1_wait"))
    h3, h4, hnf1, gate1, up1 = _ffn_fwd(h2, pool, sgu, w_out1, ffn_norm[1:2], wtg1, wtu1, wd1, "ffn_fwd1")

    def blocks(g):
        return g.reshape(N_DEV, g.shape[0] // N_DEV, D)

    dh4, d_final_norm, loss_part = _loss_head(h4, final_norm.reshape(1, D), target, "loss_head")
    dh3, dmix1, dgate1, dup1, act1, d_fn1 = _ffn_bwd(dh4, h3, ffn_norm[1:2], gate1, up1, wtg1, wtu1, wd1, w_out1, "ffn_bwd1")
    gw_ffn1 = [_mm_tn(dgate1, hnf1, "dw_gate1"), _mm_tn(dup1, hnf1, "dw_up1"), _mm_tn(act1, dh4, "dw_down1")]
    ffn1_g, tok = _exchange_start([blocks(g) for g in gw_ffn1], True, "scatter_ffn1_start")
    gw_c_out = _mm_tn_pieces([pool, sgu], dh3, "dw_out1")
    dzp, d_w_pool, d_pool_scale = _pool_bwd(z1, dmix1, c_w_pool[0], _behind([tok], pool_scale), bsz, "pool_bwd")
    dzu, dzv, d_w_s, d_b_s, d_sln_g, d_sln_b = _sgu_bwd(z1, dmix1, sln_g, sln_b, c_w_s[0], b_rows, "sgu_bwd")
    dh2, d_mn1 = _proj_bwd_norm([(dzp, 0), (dzu, 512), (dzv, 1024)], wt_in1, h2, dh3, mix_norm[1:2], BF16, "in_proj1_bwd")
    gw_c_in = _mm_tn_pieces([dzp, dzu, dzv], hn1, "dw_in1")
    mix1_g, tok = _exchange_start([blocks(gw_c_in), blocks(gw_c_out)], True, "scatter_mixer1_start")
    dh1, dmix0, dgate0, dup0, act0, d_fn0 = _ffn_bwd(dh2, h1, _behind([tok], ffn_norm[0:1]), gate0, up0, wtg0, wtu0, wd0,
                                                      w_out0, "ffn_bwd0")
    gw_ffn0 = [_mm_tn(dgate0, hnf0, "dw_gate0"), _mm_tn(dup0, hnf0, "dw_up0"), _mm_tn(act0, dh2, "dw_down0")]
    ffn0_g, tok = _exchange_start([blocks(g) for g in gw_ffn0], True, "scatter_ffn0_start")
    gw_a_out = _perm_heads(_mm_tn_pieces([attn, conv], dh1, "dw_out0"), Q_INV, 0)
    dq, dkv, d_sink_row, d_bq, d_bkv = _attn_bwd(z0, dmix0, _behind([tok], a_sinks), bsz, "attn_bwd")
    dca, dcg, d_conv_w, d_conv_b, d_cln_g, d_cln_b, d_ba, d_bg = _conv_bwd(z0, conv_y, dmix0, conv_w, cln_g, cln_b, bsz, "conv_bwd")
    gw_a_in = _in0_from_kernel(_mm_tn_pieces([dq, dca, dcg, dkv], hn0, "dw_in0"), 0)
    mix0_g, tok = _exchange_start([blocks(gw_a_in), blocks(gw_a_out)], True, "scatter_mixer0_start")
    dx, d_mn0 = _proj_bwd_norm([(dq, 0), (dca, 512), (dcg, 1024), (dkv, 1536)], wt_in0, h0, dh1,
                               _behind([tok], mix_norm[0:1]), F32, "in_proj0_bwd")
    d_b_in = _in0_from_kernel(jnp.concatenate([d_bq, d_ba, d_bg, d_bkv], axis=1), 1)

    rep = dict(mix_norm=jnp.concatenate([d_mn0, d_mn1], axis=0), a_b_in=d_b_in, a_sinks=d_sink_row[:, :8],
               a_conv_b=d_conv_b, a_cln_g=d_cln_g, a_cln_b=d_cln_b, c_w_pool=d_w_pool[None], c_w_s=d_w_s[None],
               c_b_s=d_b_s[None], ffn_norm=jnp.concatenate([d_fn0, d_fn1], axis=0), final_norm=d_final_norm.reshape(D))
    rep_flat = jnp.concatenate([rep[nm].reshape(-1) for nm in REP_NAMES] + [loss_part.reshape(1)])
    rep_flat = jnp.pad(rep_flat, (0, N_DEV * REP_ROWS * D - rep_flat.shape[0])).reshape(N_DEV, REP_ROWS, D)
    small_g = jnp.concatenate([
        d_conv_w[:CONV_K].reshape(31, N_DEV, 64).transpose(1, 0, 2).reshape(N_DEV, 31 * 64),
        d_pool_scale.reshape(N_DEV, 64), d_sln_g.reshape(N_DEV, 64), d_sln_b.reshape(N_DEV, 64)], axis=1)
    small_g = jnp.pad(small_g, ((0, 0), (0, G_SMALL_ROWS * D - SMALL_SHARD))).reshape(N_DEV, G_SMALL_ROWS, D)
    tail_g, tok = _exchange_start([jnp.concatenate([small_g, rep_flat], axis=1)], True, "scatter_tail_start")

    names = list(w_in)
    g_out, delta, new_m, new_v = {}, {}, {}, {}
    column_sharded = ("a_w_in", "c_w_in", "ffn_w_gate", "ffn_w_up")

    def rows_of(a, nm):
        return jnp.swapaxes(a, 1, 2) if nm in column_sharded else a

    def reduce_adamw(nm, landing, layer, into=None):
        args = [rows_of(d[nm], nm) for d in (w_in, m_in, v_in)]
        if args[0].shape[0] == 1:
            args, layer = [a[0] for a in args], None
        return _reduce_adamw(landing, *args, "adamw_%s_%s" % (nm, layer), layer=layer, into=into)

    def keep(nm, res):
        res = [r if r.ndim == 3 else r[None] for r in res]
        g_out[nm], delta[nm], new_m[nm], new_v[nm] = (rows_of(r, nm) for r in res)

    ffn_names = ("ffn_w_gate", "ffn_w_up", "ffn_w_down")
    landed = _exchange_wait(ffn1_g, tok, "scatter_ffn1_wait")
    ffn_res = [reduce_adamw(nm, a, 1) for nm, a in zip(ffn_names, landed)]
    landed = _exchange_wait(mix1_g, ffn_res[-1][0], "scatter_mixer1_wait")
    for nm, a in zip(("c_w_in", "c_w_out"), landed):
        keep(nm, reduce_adamw(nm, a, 0))
    landed = _exchange_wait(ffn0_g, g_out["c_w_out"], "scatter_ffn0_wait")
    for nm, a, res in zip(ffn_names, landed, ffn_res):
        keep(nm, reduce_adamw(nm, a, 0, into=res))
    landed = _exchange_wait(mix0_g, g_out["ffn_w_down"], "scatter_mixer0_wait")
    for nm, a in zip(("a_w_in", "a_w_out"), landed):
        keep(nm, reduce_adamw(nm, a, 0))
    g_tail = _sum_slabs(_exchange_wait(tail_g, g_out["a_w_out"], "scatter_tail_wait")[0], "sum_tail")
    rep_all = _all_gather(g_tail[G_SMALL_ROWS:], "gather_replicated_grads")[0].reshape(-1)
    small_r = g_tail[:G_SMALL_ROWS].reshape(-1)[:SMALL_SHARD]
    g_out.update(
        a_conv_w=small_r[:31 * 64].reshape(1, 31, 64), c_pool_scale=small_r[31 * 64:31 * 64 + 64].reshape(1, 64),
        c_sln_g=small_r[31 * 64 + 64:31 * 64 + 128].reshape(1, 64), c_sln_b=small_r[31 * 64 + 128:].reshape(1, 64))
    off = 0
    for nm in REP_NAMES:
        n = int(np.prod(w_in[nm].shape))
        g_out[nm] = rep_all[off:off + n].reshape(w_in[nm].shape)
        off += n
    loss = rep_all[off]
    for group, rows, label in ((("a_conv_w", "c_pool_scale", "c_sln_g", "c_sln_b"), G_SMALL_ROWS, "adamw_small_sharded"),
                               (REP_NAMES, N_DEV * REP_ROWS, "adamw_replicated")):
        flat = [_pad_rows(jnp.concatenate([d[nm].reshape(-1) for nm in group]), rows) for d in (w_in, g_out, m_in, v_in)]
        res = [r.reshape(-1) for r in _adamw(*flat, label)]
        off = 0
        for nm in group:
            n = int(np.prod(w_in[nm].shape))
            delta[nm], new_m[nm], new_v[nm] = (r[off:off + n].reshape(w_in[nm].shape) for r in res)
            off += n

    grad_x = dx.reshape(bsz, seq, D)
    return (loss, grad_x, *[g_out[nm] for nm in names], *[delta[nm] for nm in names],
            *[new_m[nm] for nm in names], *[new_v[nm] for nm in names])
```

```python
import functools

import jax
import jax.numpy as jnp
import numpy as np
from jax import lax
from jax.experimental import pallas as pl
from jax.experimental.pallas import tpu as pltpu

F32 = jnp.float32
BF16 = jnp.bfloat16
MESH = pl.DeviceIdType.MESH

D = 1024
N_DEV = 8
EPS = 1e-5
HEAD_PAIRS = 4
ATT_BLK = 128
CONV_K = 31
HALO = 32
D_FF = 2816
FF_TILE_FWD = D_FF // 2
FF_TILE_BWD = D_FF // 2
IN0 = 1792
IN1 = 1536
POOL_WINDOWS = (2, 4, 8, 16)
SGU_CHUNK = 128
GELU_C = 0.7978845608028654
GELU_A = 0.044715
ADAM_LR, ADAM_B1, ADAM_B2, ADAM_EPS, ADAM_WD, ADAM_STEP = 0.001, 0.9, 0.999, 1e-08, 0.01, 10
VMEM_LIMIT = 56 << 20

SMALL_SHARD = 31 * 64 + 3 * 64
W_MISC_ROWS = 16
G_MISC_ROWS = 32
G_SMALL_ROWS = 8
REP_ROWS = G_MISC_ROWS - G_SMALL_ROWS
REP_NAMES = ("mix_norm", "a_b_in", "a_sinks", "a_conv_b", "a_cln_g", "a_cln_b", "c_w_pool", "c_w_s", "c_b_s",
             "ffn_norm", "final_norm")
Q_PERM = (0, 4, 1, 5, 2, 6, 3, 7)


def _params(*sem):
    return pltpu.CompilerParams(dimension_semantics=sem, vmem_limit_bytes=VMEM_LIMIT)


def _nn(a, b):
    return jnp.dot(a, b, preferred_element_type=F32)


def _nt(a, b):
    return lax.dot_general(a, b, (((1,), (1,)), ((), ())), preferred_element_type=F32)


def _tn(a, b):
    return lax.dot_general(a, b, (((0,), (0,)), ((), ())), preferred_element_type=F32)


def _tile(n, want=512):
    t = min(want, n)
    assert n % t == 0, (n, t)
    return t


def _seq_tile(s):
    return 512 if s >= 1024 else s // 2


def _rms(x, g):
    r = lax.rsqrt(jnp.mean(x * x, axis=-1, keepdims=True) + EPS)
    return x * r * g, r


def _rms_bwd(x, g, d_y):
    r = lax.rsqrt(jnp.mean(x * x, axis=-1, keepdims=True) + EPS)
    xr = x * r
    u = d_y * g
    d_x = r * (u - xr * jnp.mean(u * xr, axis=-1, keepdims=True))
    return d_x, jnp.sum(d_y * xr, axis=0, keepdims=True)


def _ln(y, g, b):
    mu = jnp.mean(y, axis=-1, keepdims=True)
    yc = y - mu
    rstd = lax.rsqrt(jnp.mean(yc * yc, axis=-1, keepdims=True) + EPS)
    xhat = yc * rstd
    return xhat * g + b, xhat, rstd


def _ln_bwd(d_o, xhat, rstd, g):
    dxh = d_o * g
    return rstd * (dxh - jnp.mean(dxh, axis=-1, keepdims=True) - xhat * jnp.mean(dxh * xhat, axis=-1, keepdims=True))


def _gelu(x):
    th = jnp.tanh(GELU_C * (x + GELU_A * x * x * x))
    return 0.5 * x * (1.0 + th), th


def _gelu_grad(x, th):
    return 0.5 * (1.0 + th) + 0.5 * x * (1.0 - th * th) * GELU_C * (1.0 + 3.0 * GELU_A * x * x)


def _row(c):
    return pl.BlockSpec((1, c), lambda *_: (0, 0))


def _full(shape):
    return pl.BlockSpec(shape, lambda *_: (0,) * len(shape))


def _norm_proj(h, g, wt, bias, name):
    t, n = h.shape[0], wt.shape[0]
    tm = _tile(t)
    has_bias = bias is not None

    def body(*refs):
        h_ref, g_ref, wt_ref = refs[:3]
        z_ref, hn_ref = refs[-2:]
        hn = _rms(h_ref[...].astype(F32), g_ref[...])[0].astype(BF16)
        hn_ref[...] = hn
        z = _nt(hn, wt_ref[...])
        if has_bias:
            z = z + refs[3][...]
        z_ref[...] = z.astype(BF16)

    in_specs = [pl.BlockSpec((tm, D), lambda i: (i, 0)), _row(D), _full((n, D))]
    args = [h, g, wt]
    if has_bias:
        in_specs.append(_row(n))
        args.append(bias)
    return pl.pallas_call(
        body, name=name, grid=(t // tm,), in_specs=in_specs,
        out_specs=[pl.BlockSpec((tm, n), lambda i: (i, 0)), pl.BlockSpec((tm, D), lambda i: (i, 0))],
        out_shape=[jax.ShapeDtypeStruct((t, n), BF16), jax.ShapeDtypeStruct((t, D), BF16)],
        compiler_params=_params("parallel"))(*args)


def _ff_pieces(tf, width=256):
    return [(c0, min(width, tf - c0)) for c0 in range(0, tf, width)]


def _ffn_fwd(h_prev, a, b, w_out, g, wtg, wtu, wd, name):
    t = h_prev.shape[0]
    tm, tf = _tile(t), FF_TILE_FWD
    nf = D_FF // tf

    def body(hp_ref, a_ref, b_ref, wa_ref, wb_ref, g_ref, wtg_ref, wtu_ref, wd_ref,
             hmid_ref, o_ref, hn_ref, gate_ref, up_ref, acc, act):
        f = pl.program_id(1)

        @pl.when(f == 0)
        def _():
            x = hp_ref[...].astype(F32) + _nn(a_ref[...], wa_ref[...]) + _nn(b_ref[...], wb_ref[...])
            hmid_ref[...] = x.astype(BF16)
            hn_ref[...] = _rms(x, g_ref[...])[0].astype(BF16)
            acc[...] = x

        hn = hn_ref[...]
        for c0, cw in _ff_pieces(tf):
            rows, cols = slice(c0, c0 + cw), slice(c0, c0 + cw)
            gate = _nt(hn, wtg_ref[rows, :])
            up = _nt(hn, wtu_ref[rows, :])
            gate_ref[:, cols] = gate.astype(BF16)
            up_ref[:, cols] = up.astype(BF16)
            act[:, cols] = (gate * jax.nn.sigmoid(gate) * up).astype(BF16)
        acc[...] += _nn(act[...], wd_ref[...])

        @pl.when(f == nf - 1)
        def _():
            o_ref[...] = acc[...].astype(BF16)

    tok = pl.BlockSpec((tm, D), lambda i, f: (i, 0))
    wsp = pl.BlockSpec((tf, D), lambda i, f: (f, 0))
    mid = pl.BlockSpec((tm, tf), lambda i, f: (i, f))
    half = pl.BlockSpec((tm, 512), lambda i, f: (i, 0))
    res = jax.ShapeDtypeStruct((t, D), BF16)
    return pl.pallas_call(
        body, name=name, grid=(t // tm, nf),
        in_specs=[tok, half, half, pl.BlockSpec((512, D), lambda i, f: (0, 0)), pl.BlockSpec((512, D), lambda i, f: (1, 0)),
                  _row(D), wsp, wsp, wsp],
        out_specs=[tok, tok, tok, mid, mid],
        out_shape=[res, res, res, jax.ShapeDtypeStruct((t, D_FF), BF16), jax.ShapeDtypeStruct((t, D_FF), BF16)],
        scratch_shapes=[pltpu.VMEM((tm, D), F32), pltpu.VMEM((tm, tf), BF16)],
        compiler_params=_params("parallel", "arbitrary"))(h_prev, a, b, w_out, w_out, g, wtg, wtu, wd)


def _ffn_bwd(dh, h, g, gate, up, wtg, wtu, wd, w_out, name):
    t = h.shape[0]
    tm, tf = _tile(t), FF_TILE_BWD
    nf = D_FF // tf

    def body(dh_ref, h_ref, g_ref, gate_ref, up_ref, wtg_ref, wtu_ref, wd_ref, wout_ref,
             dhin_ref, dmix_ref, dgate_ref, dup_ref, act_ref, dg_ref, d_hn):
        i, f = pl.program_id(0), pl.program_id(1)

        @pl.when(f == 0)
        def _():
            d_hn[...] = jnp.zeros_like(d_hn)

        @pl.when((i == 0) & (f == 0))
        def _():
            dg_ref[...] = jnp.zeros_like(dg_ref)

        dh = dh_ref[...]
        for c0, cw in _ff_pieces(tf):
            cols = slice(c0, c0 + cw)
            da = _nt(dh, wd_ref[c0:c0 + cw, :])
            gt = gate_ref[:, cols].astype(F32)
            u = up_ref[:, cols].astype(F32)
            sg = jax.nn.sigmoid(gt)
            sil = gt * sg
            act_ref[:, cols] = (sil * u).astype(BF16)
            dup_ref[:, cols] = (da * sil).astype(BF16)
            dgate_ref[:, cols] = (da * u * sg * (1.0 + gt * (1.0 - sg))).astype(BF16)
        d_hn[...] += _nn(dgate_ref[...], wtg_ref[...]) + _nn(dup_ref[...], wtu_ref[...])

        @pl.when(f == nf - 1)
        def _():
            d_x, d_g = _rms_bwd(h_ref[...].astype(F32), g_ref[...], d_hn[...])
            dhin = (dh_ref[...].astype(F32) + d_x).astype(BF16)
            dhin_ref[...] = dhin
            dmix_ref[...] = _nt(dhin, wout_ref[...]).astype(BF16)
            dg_ref[...] += d_g

    tok = pl.BlockSpec((tm, D), lambda i, f: (i, 0))
    wsp = pl.BlockSpec((tf, D), lambda i, f: (f, 0))
    mid = pl.BlockSpec((tm, tf), lambda i, f: (i, f))
    mid_shape = jax.ShapeDtypeStruct((t, D_FF), BF16)
    res = jax.ShapeDtypeStruct((t, D), BF16)
    return pl.pallas_call(
        body, name=name, grid=(t // tm, nf), in_specs=[tok, tok, _row(D), mid, mid, wsp, wsp, wsp, _full((D, D))],
        out_specs=[tok, tok, mid, mid, mid, _row(D)],
        out_shape=[res, res, mid_shape, mid_shape, mid_shape, jax.ShapeDtypeStruct((1, D), F32)],
        scratch_shapes=[pltpu.VMEM((tm, D), F32)],
        compiler_params=_params("arbitrary", "arbitrary"))(dh, h, g, gate, up, wtg, wtu, wd, w_out)


def _proj_bwd_norm(pieces, wt, h, dh, g, dtype, name):
    t = h.shape[0]
    tm = _tile(t)
    n_p = len(pieces)

    def body(*refs):
        p_refs, w_refs = refs[:n_p], refs[n_p:2 * n_p]
        h_ref, dh_ref, g_ref, o_ref, dg_ref = refs[2 * n_p:]

        @pl.when(pl.program_id(0) == 0)
        def _():
            dg_ref[...] = jnp.zeros_like(dg_ref)

        d_hn = _nn(p_refs[0][...], w_refs[0][...])
        for p_ref, w_ref in zip(p_refs[1:], w_refs[1:]):
            d_hn = d_hn + _nn(p_ref[...], w_ref[...])
        d_x, d_g = _rms_bwd(h_ref[...].astype(F32), g_ref[...], d_hn)
        o_ref[...] = (dh_ref[...].astype(F32) + d_x).astype(dtype)
        dg_ref[...] += d_g

    tok = pl.BlockSpec((tm, D), lambda i: (i, 0))
    in_specs = [pl.BlockSpec((tm, a.shape[1]), lambda i: (i, 0)) for a, _ in pieces]
    for a, off in pieces:
        w = a.shape[1]
        assert off % w == 0
        in_specs.append(pl.BlockSpec((w, D), functools.partial(lambda i, blk: (blk, 0), blk=off // w)))
    in_specs += [tok, tok, _row(D)]
    return pl.pallas_call(
        body, name=name, grid=(t // tm,), in_specs=in_specs, out_specs=[tok, _row(D)],
        out_shape=[jax.ShapeDtypeStruct((t, D), dtype), jax.ShapeDtypeStruct((1, D), F32)],
        compiler_params=_params("arbitrary"))(*[a for a, _ in pieces], *([wt] * n_p), h, dh, g)


def _mm_tn(a, b, name):
    t, n = a.shape
    k = b.shape[1]
    tn = n if n <= 1024 else n // 2
    tt = _tile(t, 1024)
    nt = t // tt

    def body(a_ref, b_ref, o_ref, acc):
        s = pl.program_id(1)

        @pl.when(s == 0)
        def _():
            acc[...] = jnp.zeros_like(acc)

        acc[...] += _tn(a_ref[...], b_ref[...].astype(BF16))

        @pl.when(s == nt - 1)
        def _():
            o_ref[...] = acc[...].astype(BF16)

    return pl.pallas_call(
        body, name=name, grid=(n // tn, nt),
        in_specs=[pl.BlockSpec((tt, tn), lambda j, s: (s, j)), pl.BlockSpec((tt, k), lambda j, s: (s, 0))],
        out_specs=pl.BlockSpec((tn, k), lambda j, s: (j, 0)), out_shape=jax.ShapeDtypeStruct((n, k), BF16),
        scratch_shapes=[pltpu.VMEM((tn, k), F32)],
        compiler_params=_params("parallel", "arbitrary"))(a, b)


def _mm_tn_pieces(pieces, b, name):
    t, k = b.shape
    widths = [p.shape[1] for p in pieces]
    n, n_p = sum(widths), len(pieces)
    tt = _tile(t, 1024)
    nt = t // tt

    def body(*refs):
        b_ref, o_ref, acc = refs[n_p:]
        s = pl.program_id(0)

        @pl.when(s == 0)
        def _():
            acc[...] = jnp.zeros_like(acc)

        bb = b_ref[...].astype(BF16)
        off = 0
        for p_ref, w in zip(refs[:n_p], widths):
            acc[off:off + w, :] += _tn(p_ref[...], bb)
            off += w

        @pl.when(s == nt - 1)
        def _():
            o_ref[...] = acc[...].astype(BF16)

    return pl.pallas_call(
        body, name=name, grid=(nt,),
        in_specs=[pl.BlockSpec((tt, w), lambda s: (s, 0)) for w in widths] + [pl.BlockSpec((tt, k), lambda s: (s, 0))],
        out_specs=_full((n, k)), out_shape=jax.ShapeDtypeStruct((n, k), BF16),
        scratch_shapes=[pltpu.VMEM((n, k), F32)], compiler_params=_params("arbitrary"))(*pieces, b)


def _loss_head(h, g, target, name):
    t = h.shape[0]
    tm = _tile(t)

    def body(h_ref, g_ref, t_ref, dh_ref, dg_ref, loss_ref):
        @pl.when(pl.program_id(0) == 0)
        def _():
            dg_ref[...] = jnp.zeros_like(dg_ref)
            loss_ref[...] = jnp.zeros_like(loss_ref)

        x = h_ref[...].astype(F32)
        gv = g_ref[...]
        err = _rms(x, gv)[0] - t_ref[...]
        per_tok = jnp.mean(err * err, axis=-1, keepdims=True)
        loss_ref[...] += 0.5 * jnp.sum(per_tok, axis=0, keepdims=True)
        d_x, d_g = _rms_bwd(x, gv, err * (1.0 / D))
        dh_ref[...] = d_x.astype(BF16)
        dg_ref[...] += d_g

    tok = pl.BlockSpec((tm, D), lambda i: (i, 0))
    return pl.pallas_call(
        body, name=name, grid=(t // tm,), in_specs=[tok, _row(D), tok],
        out_specs=[tok, _row(D), _row(1)],
        out_shape=[jax.ShapeDtypeStruct((t, D), BF16), jax.ShapeDtypeStruct((1, D), F32), jax.ShapeDtypeStruct((1, 1), F32)],
        compiler_params=_params("arbitrary"))(h, g, target)


STACK = HEAD_PAIRS * ATT_BLK


def _attn_valid(first, rows):
    qi = lax.broadcasted_iota(jnp.int32, (rows, 2 * ATT_BLK), 0) % ATT_BLK
    r = lax.broadcasted_iota(jnp.int32, (rows, 2 * ATT_BLK), 1)
    dist = qi + ATT_BLK - r
    return (dist >= 0) & (dist < ATT_BLK) & ((r >= ATT_BLK) | jnp.logical_not(first))


def _stacked(ref, kh, scale):
    lo = lax.broadcasted_iota(jnp.int32, (ATT_BLK, 128), 1) < 64
    keep = lo if kh == 0 else ~lo
    parts = [jnp.where(keep, ref[:, g * 128:(g + 1) * 128] * scale, 0.0).astype(BF16) for g in range(HEAD_PAIRS)]
    return jnp.concatenate(parts, axis=0)


def _unstacked(a0, a1, g):
    lo = lax.broadcasted_iota(jnp.int32, (ATT_BLK, 128), 1) < 64
    rows = slice(g * ATT_BLK, (g + 1) * ATT_BLK)
    return jnp.where(lo, a0[rows], a1[rows])


def _sink_rows(s_ref, kh):
    return jnp.concatenate([jnp.full((ATT_BLK, 128), s_ref[0, kh * 4 + g], F32) for g in range(HEAD_PAIRS)], axis=0)


def _row_sums(a):
    hi = a.astype(BF16)
    lo = (a - hi.astype(F32)).astype(BF16)
    ones = jnp.ones((2 * ATT_BLK, 128), BF16)
    return _nn(hi, ones) + _nn(lo, ones)


def _both(a):
    return jnp.concatenate([a, a], axis=1)


def _attn_probs(qs, kpair, sink, valid):
    s = jnp.where(valid, _nt(qs, kpair), -1e30)
    m = jnp.maximum(jnp.broadcast_to(jnp.max(s, axis=-1, keepdims=True), (s.shape[0], 128)), sink)
    p = jnp.exp(s - _both(m))
    es = jnp.exp(sink - m)
    inv = 1.0 / (_row_sums(p) + es)
    return p * _both(inv), es * inv


def _attn_probs_head(qm, kpair, sink, valid):
    s = jnp.where(valid, _nt(qm, kpair), -1e30)
    m = jnp.maximum(jnp.max(s, axis=-1, keepdims=True), sink)
    p = jnp.exp(s - m)
    return p * (1.0 / (jnp.sum(p, axis=-1, keepdims=True) + jnp.exp(sink - m)))


def _attn_specs(bsz, order):
    q = pl.BlockSpec((bsz, ATT_BLK, 512), lambda j: (0, order(j), 0))
    kvc = pl.BlockSpec((bsz, ATT_BLK, 256), lambda j: (0, order(j), 6))
    kvp = pl.BlockSpec((bsz, ATT_BLK, 256), lambda j: (0, jnp.maximum(order(j) - 1, 0), 6))
    return q, kvc, kvp


def _window_kv(kvc_ref, kvp_ref):
    kvc, kvp = kvc_ref[...], kvp_ref[...]
    kpair = jnp.concatenate([kvp[:, :128], kvc[:, :128]], axis=0)
    vpair = jnp.concatenate([kvp[:, 128:], kvc[:, 128:]], axis=0)
    return kpair, vpair


def _attn_fwd(z0, sinks, bsz, name):
    t = z0.shape[0]
    seq = t // bsz
    nb = seq // ATT_BLK

    def body(s_ref, q_ref, kvc_ref, kvp_ref, o_ref, token):
        token[...] = jnp.zeros_like(token)
        valid = _attn_valid(pl.program_id(0) == 0, ATT_BLK)
        lo = lax.broadcasted_iota(jnp.int32, (ATT_BLK, 128), 1) < 64
        for b in range(bsz):
            kpair, vpair = _window_kv(kvc_ref.at[b], kvp_ref.at[b])
            for g in range(HEAD_PAIRS):
                qs = q_ref[b, :, g * 128:(g + 1) * 128] * 0.125
                outs = []
                for kh in range(2):
                    qm = jnp.where(lo if kh == 0 else ~lo, qs, 0.0).astype(BF16)
                    p = _attn_probs_head(qm, kpair, s_ref[0, kh * 4 + g], valid)
                    outs.append(_nn(p.astype(BF16), vpair))
                o_ref[b, :, g * 128:(g + 1) * 128] = jnp.where(lo, outs[0], outs[1]).astype(BF16)

    q, kvc, kvp = _attn_specs(bsz, lambda j: j)
    z3 = z0.reshape(bsz, seq, z0.shape[1])
    out, token = pl.pallas_call(
        body, name=name, grid=(nb,),
        in_specs=[pl.BlockSpec(memory_space=pltpu.SMEM), q, kvc, kvp],
        out_specs=[pl.BlockSpec((bsz, ATT_BLK, 512), lambda j: (0, j, 0)), _full((8, 128))],
        out_shape=[jax.ShapeDtypeStruct((bsz, seq, 512), BF16), jax.ShapeDtypeStruct((8, 128), F32)],
        compiler_params=_params("arbitrary"))(sinks, z3, z3, z3)
    return out.reshape(t, 512), token


def _attn_bwd(z0, dmix, sinks, bsz, name):
    t = z0.shape[0]
    seq = t // bsz
    nb = seq // ATT_BLK

    def body(s_ref, q_ref, kvc_ref, kvp_ref, do_ref, dq_ref, dkv_ref, dsink_ref, dbq_ref, dbkv_ref, carry):
        j = pl.program_id(0)

        @pl.when(j == 0)
        def _():
            carry[...] = jnp.zeros_like(carry)
            dsink_ref[...] = jnp.zeros_like(dsink_ref)
            dbq_ref[...] = jnp.zeros_like(dbq_ref)
            dbkv_ref[...] = jnp.zeros_like(dbkv_ref)

        valid = _attn_valid(j == nb - 1, STACK)
        lane = lax.broadcasted_iota(jnp.int32, (1, 128), 1)
        dsink = jnp.zeros((1, 128), F32)
        dbq = [jnp.zeros((1, 128), F32)] * HEAD_PAIRS
        dbkv = jnp.zeros((1, 256), F32)
        for b in range(bsz):
            kpair, vpair = _window_kv(kvc_ref.at[b], kvp_ref.at[b])
            dk = jnp.zeros((2 * ATT_BLK, 128), F32)
            dv = jnp.zeros((2 * ATT_BLK, 128), F32)
            dqs = []
            for kh in range(2):
                qs = _stacked(q_ref.at[b], kh, 0.125)
                dos = _stacked(do_ref.at[b], kh, 1.0)
                p, ps = _attn_probs(qs, kpair, _sink_rows(s_ref, kh), valid)
                dp = _nt(dos, vpair)
                delta = _row_sums(p * dp)
                ds = (p * (dp - _both(delta))).astype(BF16)
                dqs.append(_nn(ds, kpair))
                dk = dk + _tn(ds, qs)
                dv = dv + _tn(p.astype(BF16), dos)
                psd = ps * delta
                for g in range(HEAD_PAIRS):
                    part = jnp.sum(psd[g * ATT_BLK:(g + 1) * ATT_BLK], axis=0, keepdims=True)
                    dsink = dsink - jnp.where(lane == kh * 4 + g, part, 0.0)
            for g in range(HEAD_PAIRS):
                dq = _unstacked(dqs[0], dqs[1], g) * 0.125
                dq_ref[b, :, g * 128:(g + 1) * 128] = dq.astype(BF16)
                dbq[g] = dbq[g] + jnp.sum(dq, axis=0, keepdims=True)
            dkv = jnp.concatenate([dk[ATT_BLK:], dv[ATT_BLK:]], axis=1) + carry[b]
            dkv_ref[b] = dkv.astype(BF16)
            dbkv = dbkv + jnp.sum(dkv, axis=0, keepdims=True)
            carry[b] = jnp.concatenate([dk[:ATT_BLK], dv[:ATT_BLK]], axis=1)
        dsink_ref[...] += dsink
        dbq_ref[...] += jnp.concatenate(dbq, axis=1)
        dbkv_ref[...] += dbkv

    q, kvc, kvp = _attn_specs(bsz, lambda j: nb - 1 - j)
    z3 = z0.reshape(bsz, seq, z0.shape[1])
    d3 = dmix.reshape(bsz, seq, dmix.shape[1])
    dq, dkv, dsink, dbq, dbkv = pl.pallas_call(
        body, name=name, grid=(nb,),
        in_specs=[pl.BlockSpec(memory_space=pltpu.SMEM), q, kvc, kvp,
                  pl.BlockSpec((bsz, ATT_BLK, 512), lambda j: (0, nb - 1 - j, 0))],
        out_specs=[pl.BlockSpec((bsz, ATT_BLK, 512), lambda j: (0, nb - 1 - j, 0)),
                   pl.BlockSpec((bsz, ATT_BLK, 256), lambda j: (0, nb - 1 - j, 0)), _row(128), _row(512), _row(256)],
        out_shape=[jax.ShapeDtypeStruct((bsz, seq, 512), BF16), jax.ShapeDtypeStruct((bsz, seq, 256), BF16),
                   jax.ShapeDtypeStruct((1, 128), F32), jax.ShapeDtypeStruct((1, 512), F32),
                   jax.ShapeDtypeStruct((1, 256), F32)],
        scratch_shapes=[pltpu.VMEM((bsz, ATT_BLK, 256), F32)],
        compiler_params=_params("arbitrary"))(sinks, z3, z3, z3, d3)
    return dq.reshape(t, 512), dkv.reshape(t, 256), dsink, dbq, dbkv


def _seq_specs(ts, nt, t, width, col):
    per = ts // HALO
    cur = pl.BlockSpec((ts, width), lambda b, i: (b * nt + i, col))
    prev = pl.BlockSpec((HALO, width), lambda b, i: (jnp.maximum((b * nt + i) * per - 1, 0), col))
    nxt = pl.BlockSpec((HALO, width), lambda b, i: (jnp.minimum((b * nt + i + 1) * per, t // HALO - 1), col))
    return prev, cur, nxt


SUB = 8
CONV_ROWS = 64


def _shifted_copies(src, sh, rows_first, rows_rest):
    for r in range(SUB):
        rows = rows_first if r == 0 else rows_rest
        sh[r, pl.ds(0, rows), :] = src[pl.ds(r, rows), :]


def _tap_sum(sh, w, offset, c0, rows):
    acc = None
    for k in range(CONV_K):
        o = offset(k)
        term = sh[o % SUB, pl.ds(c0 + o - o % SUB, rows), :] * w[k:k + 1, :]
        acc = term if acc is None else acc + term
    return acc


def _glu_rows(a_ref, g_ref, rows=slice(None)):
    return a_ref[rows, :].astype(F32) * jax.nn.sigmoid(g_ref[rows, :].astype(F32))


def _conv_fwd(z0, conv_w, conv_b, ln_g, ln_b, bsz, name):
    t = z0.shape[0]
    s = t // bsz
    ts = _seq_tile(s)
    nt = s // ts
    first = HALO - (CONV_K - 1)

    def body(ap_ref, ac_ref, gp_ref, gc_ref, w_ref, cb_ref, lg_ref, lb_ref, o_ref, y_ref, hbuf, sh):
        hbuf[0:HALO, :] = jnp.where(pl.program_id(1) > 0, _glu_rows(ap_ref, gp_ref), 0.0)
        hbuf[HALO:HALO + ts, :] = _glu_rows(ac_ref, gc_ref)
        _shifted_copies(hbuf, sh, ts + HALO, ts + HALO - SUB)
        w, cb, lg, lb = w_ref[...], cb_ref[...], lg_ref[...], lb_ref[...]
        for c0 in range(0, ts, CONV_ROWS):
            y = _tap_sum(sh, w, lambda k: first + k, c0, CONV_ROWS) + cb
            y_ref[c0:c0 + CONV_ROWS, :] = y
            o = _ln(y, lg, lb)[0]
            o_ref[c0:c0 + CONV_ROWS, :] = (o * jax.nn.sigmoid(o)).astype(BF16)

    ap, ac, _ = _seq_specs(ts, nt, t, 512, 1)
    gp, gc, _ = _seq_specs(ts, nt, t, 512, 2)
    tile = pl.BlockSpec((ts, 512), lambda b, i: (b * nt + i, 0))
    return pl.pallas_call(
        body, name=name, grid=(bsz, nt),
        in_specs=[ap, ac, gp, gc, _full((HALO, 512)), _row(512), _row(512), _row(512)],
        out_specs=[tile, tile],
        out_shape=[jax.ShapeDtypeStruct((t, 512), BF16), jax.ShapeDtypeStruct((t, 512), F32)],
        scratch_shapes=[pltpu.VMEM((HALO + ts, 512), F32), pltpu.VMEM((SUB, HALO + ts, 512), F32)],
        compiler_params=_params("parallel", "parallel"))(z0, z0, z0, z0, conv_w, conv_b, ln_g, ln_b)


def _conv_bwd(z0, y, dmix, conv_w, ln_g, ln_b, bsz, name):
    t = z0.shape[0]
    s = t // bsz
    ts = _seq_tile(s)
    nt = s // ts

    def body(ac_ref, gc_ref, yc_ref, yn_ref, dc_ref, dn_ref, w_ref, lg_ref, lb_ref,
             da_ref, dg_ref, dw_ref, dcb_ref, dlg_ref, dlb_ref, dba_ref, dbg_ref, hcur, dybuf, sh_dy):
        b, i = pl.program_id(0), pl.program_id(1)

        @pl.when((b == 0) & (i == 0))
        def _():
            for ref in (dw_ref, dcb_ref, dlg_ref, dlb_ref, dba_ref, dbg_ref):
                ref[...] = jnp.zeros_like(ref)

        w, lg, lb = w_ref[...], lg_ref[...], lb_ref[...]
        hcur[...] = _glu_rows(ac_ref, gc_ref)

        def d_conv_out(yv, dout):
            o, xhat, rstd = _ln(yv, lg, lb)
            sg_o = jax.nn.sigmoid(o)
            d_o = dout * sg_o * (1.0 + o * (1.0 - sg_o))
            return _ln_bwd(d_o, xhat, rstd, lg), d_o * xhat, d_o

        dlg = jnp.zeros((1, 512), F32)
        dlb = jnp.zeros((1, 512), F32)
        dcb = jnp.zeros((1, 512), F32)
        for c0 in range(0, ts, CONV_ROWS):
            rows = slice(c0, c0 + CONV_ROWS)
            dy, g_part, b_part = d_conv_out(yc_ref[rows, :], dc_ref[rows, :].astype(F32))
            dybuf[rows, :] = dy
            dlg = dlg + jnp.sum(g_part, axis=0, keepdims=True)
            dlb = dlb + jnp.sum(b_part, axis=0, keepdims=True)
            dcb = dcb + jnp.sum(dy, axis=0, keepdims=True)
        dn = jnp.where(i < nt - 1, dn_ref[...].astype(F32), 0.0)
        dybuf[ts:ts + HALO, :] = d_conv_out(yn_ref[...], dn)[0]
        dlg_ref[...] += dlg
        dlb_ref[...] += dlb
        dcb_ref[...] += dcb
        _shifted_copies(dybuf, sh_dy, ts + HALO - SUB, ts + HALO - SUB)

        for k in range(CONV_K):
            o = CONV_K - 1 - k
            prod = hcur[...] * sh_dy[o % SUB, pl.ds(o - o % SUB, ts), :]
            dw_ref[pl.ds(k, 1), :] += jnp.sum(prod, axis=0, keepdims=True)
        dba = jnp.zeros((1, 512), F32)
        dbg = jnp.zeros((1, 512), F32)
        for c0 in range(0, ts, CONV_ROWS):
            rows = slice(c0, c0 + CONV_ROWS)
            dh = _tap_sum(sh_dy, w, lambda k: CONV_K - 1 - k, c0, CONV_ROWS)
            a_c = ac_ref[rows, :].astype(F32)
            sg_c = jax.nn.sigmoid(gc_ref[rows, :].astype(F32))
            d_a = dh * sg_c
            d_g = dh * a_c * sg_c * (1.0 - sg_c)
            da_ref[rows, :] = d_a.astype(BF16)
            dg_ref[rows, :] = d_g.astype(BF16)
            dba = dba + jnp.sum(d_a, axis=0, keepdims=True)
            dbg = dbg + jnp.sum(d_g, axis=0, keepdims=True)
        dba_ref[...] += dba
        dbg_ref[...] += dbg

    _, ac, _ = _seq_specs(ts, nt, t, 512, 1)
    _, gc, _ = _seq_specs(ts, nt, t, 512, 2)
    _, yc, yn = _seq_specs(ts, nt, t, 512, 0)
    _, dc, dn = _seq_specs(ts, nt, t, 512, 1)
    tile = pl.BlockSpec((ts, 512), lambda b, i: (b * nt + i, 0))
    vec = jax.ShapeDtypeStruct((1, 512), F32)
    return pl.pallas_call(
        body, name=name, grid=(bsz, nt),
        in_specs=[ac, gc, yc, yn, dc, dn, _full((HALO, 512)), _row(512), _row(512)],
        out_specs=[tile, tile, _full((HALO, 512)), _row(512), _row(512), _row(512), _row(512), _row(512)],
        out_shape=[jax.ShapeDtypeStruct((t, 512), BF16), jax.ShapeDtypeStruct((t, 512), BF16),
                   jax.ShapeDtypeStruct((HALO, 512), F32), vec, vec, vec, vec, vec],
        scratch_shapes=[pltpu.VMEM((ts, 512), F32), pltpu.VMEM((ts + HALO, 512), F32),
                        pltpu.VMEM((SUB, HALO + ts, 512), F32)],
        compiler_params=_params("arbitrary", "arbitrary"))(z0, z0, y, y, dmix, dmix, conv_w, ln_g, ln_b)


def _pooled(pbuf, g, ts, tok):
    w = 2 << g
    cols = slice(128 * g, 128 * (g + 1))
    sm = pbuf[pl.ds(HALO, ts), cols]
    for d in range(1, w):
        sm = sm + pbuf[pl.ds(HALO - d, ts), cols]
    cnt = jnp.minimum(tok + 1, w).astype(F32)
    return sm / cnt - pbuf[pl.ds(HALO, ts), cols]


def _pool_fwd(z1, w_pool, scale, bsz, name):
    t = z1.shape[0]
    s = t // bsz
    ts = _seq_tile(s)
    nt = s // ts

    def body(zp_ref, zc_ref, wp_ref, sc_ref, o_ref, token, pbuf):
        token[...] = jnp.zeros_like(token)
        i = pl.program_id(1)
        pbuf[0:HALO, :] = jnp.where(i > 0, zp_ref[...].astype(F32), 0.0)
        pbuf[HALO:HALO + ts, :] = zc_ref[...].astype(F32)
        tok = i * ts + lax.broadcasted_iota(jnp.int32, (ts, 1), 0)
        for g in range(4):
            cols = slice(128 * g, 128 * (g + 1))
            pooled = _pooled(pbuf, g, ts, tok).astype(BF16)
            o_ref[:, cols] = (_nn(pooled, wp_ref[g].astype(BF16)) * sc_ref[:, cols]).astype(BF16)

    zp, zc, _ = _seq_specs(ts, nt, t, 512, 0)
    return pl.pallas_call(
        body, name=name, grid=(bsz, nt), in_specs=[zp, zc, _full((4, 128, 128)), _row(512)],
        out_specs=[pl.BlockSpec((ts, 512), lambda b, i: (b * nt + i, 0)), _full((8, 128))],
        out_shape=[jax.ShapeDtypeStruct((t, 512), BF16), jax.ShapeDtypeStruct((8, 128), F32)],
        scratch_shapes=[pltpu.VMEM((HALO + ts, 512), F32)],
        compiler_params=_params("arbitrary", "arbitrary"))(z1, z1, w_pool, scale)


def _pool_bwd(z1, dmix, w_pool, scale, bsz, name):
    t = z1.shape[0]
    s = t // bsz
    ts = _seq_tile(s)
    nt = s // ts
    rr = ts + HALO

    def body(zp_ref, zc_ref, dc_ref, dn_ref, wp_ref, sc_ref, dz_ref, dwp_ref, dsc_ref, pbuf, ebuf):
        b, i = pl.program_id(0), pl.program_id(1)

        @pl.when((b == 0) & (i == 0))
        def _():
            dwp_ref[...] = jnp.zeros_like(dwp_ref)
            dsc_ref[...] = jnp.zeros_like(dsc_ref)

        pbuf[0:HALO, :] = jnp.where(i > 0, zp_ref[...].astype(F32), 0.0)
        pbuf[HALO:HALO + ts, :] = zc_ref[...].astype(F32)
        dn = jnp.where(i < nt - 1, dn_ref[...].astype(F32), 0.0)
        dout = jnp.concatenate([dc_ref[...].astype(F32), dn], axis=0)
        tok = i * ts + lax.broadcasted_iota(jnp.int32, (ts, 1), 0)
        tok_r = i * ts + lax.broadcasted_iota(jnp.int32, (rr, 1), 0)
        for g in range(4):
            w = 2 << g
            cols = slice(128 * g, 128 * (g + 1))
            wg = wp_ref[g].astype(BF16)
            pooled = _pooled(pbuf, g, ts, tok).astype(BF16)
            dsc_ref[:, cols] += jnp.sum(dout[:ts, cols] * _nn(pooled, wg), axis=0, keepdims=True)
            dy = (dout[:, cols] * sc_ref[:, cols]).astype(BF16)
            dwp_ref[g] += _tn(pooled, dy[:ts])
            dpl = _nt(dy, wg)
            ebuf[...] = dpl / jnp.minimum(tok_r + 1, w).astype(F32)
            dz = ebuf[pl.ds(0, ts), :] - dpl[:ts]
            for d in range(1, w):
                dz = dz + ebuf[pl.ds(d, ts), :]
            dz_ref[:, cols] = dz.astype(BF16)

    zp, zc, _ = _seq_specs(ts, nt, t, 512, 0)
    _, dc, dn = _seq_specs(ts, nt, t, 512, 0)
    return pl.pallas_call(
        body, name=name, grid=(bsz, nt), in_specs=[zp, zc, dc, dn, _full((4, 128, 128)), _row(512)],
        out_specs=[pl.BlockSpec((ts, 512), lambda b, i: (b * nt + i, 0)), _full((4, 128, 128)), _row(512)],
        out_shape=[jax.ShapeDtypeStruct((t, 512), BF16), jax.ShapeDtypeStruct((4, 128, 128), F32),
                   jax.ShapeDtypeStruct((1, 512), F32)],
        scratch_shapes=[pltpu.VMEM((HALO + ts, 512), F32), pltpu.VMEM((rr, 128), F32)],
        compiler_params=_params("arbitrary", "arbitrary"))(z1, z1, dmix, dmix, w_pool, scale)


def _tril():
    r = lax.broadcasted_iota(jnp.int32, (SGU_CHUNK, SGU_CHUNK), 0)
    c = lax.broadcasted_iota(jnp.int32, (SGU_CHUNK, SGU_CHUNK), 1)
    return r >= c


def _sgu_fwd(z1, ln_g, ln_b, w_s, b_rows, name):
    t = z1.shape[0]
    ts = _tile(t)

    def body(zu_ref, zv_ref, lg_ref, lb_ref, ws_ref, bs_ref, o_ref):
        v = _gelu(zv_ref[...].astype(F32))[0]
        vb = _ln(v, lg_ref[...], lb_ref[...])[0].astype(BF16)
        tril = _tril()
        for g in range(4):
            cols = slice(128 * g, 128 * (g + 1))
            wg = jnp.where(tril, ws_ref[g], 0.0).astype(BF16)
            for c in range(ts // SGU_CHUNK):
                rows = slice(SGU_CHUNK * c, SGU_CHUNK * (c + 1))
                mixed = _nn(wg, vb[rows, cols]) + bs_ref[g]
                o_ref[rows, cols] = (_gelu(zu_ref[rows, cols].astype(F32))[0] * mixed).astype(BF16)

    return pl.pallas_call(
        body, name=name, grid=(t // ts,),
        in_specs=[pl.BlockSpec((ts, 512), lambda i: (i, 1)), pl.BlockSpec((ts, 512), lambda i: (i, 2)),
                  _row(512), _row(512), _full((4, 128, 128)), _full((4, 128, 128))],
        out_specs=pl.BlockSpec((ts, 512), lambda i: (i, 0)), out_shape=jax.ShapeDtypeStruct((t, 512), BF16),
        compiler_params=_params("parallel"))(z1, z1, ln_g, ln_b, w_s, b_rows)


def _sgu_bwd(z1, dmix, ln_g, ln_b, w_s, b_rows, name):
    t = z1.shape[0]
    ts = _tile(t)

    def body(zu_ref, zv_ref, d_ref, lg_ref, lb_ref, ws_ref, bs_ref,
             dzu_ref, dzv_ref, dws_ref, dbs_ref, dlg_ref, dlb_ref, dvbuf):
        @pl.when(pl.program_id(0) == 0)
        def _():
            for ref in (dws_ref, dbs_ref, dlg_ref, dlb_ref):
                ref[...] = jnp.zeros_like(ref)

        zv = zv_ref[...].astype(F32)
        v, thv = _gelu(zv)
        lg = lg_ref[...]
        vln, xhat, rstd = _ln(v, lg, lb_ref[...])
        vb = vln.astype(BF16)
        tril = _tril()
        for g in range(4):
            cols = slice(128 * g, 128 * (g + 1))
            wg = jnp.where(tril, ws_ref[g], 0.0).astype(BF16)
            dws = jnp.zeros((SGU_CHUNK, SGU_CHUNK), F32)
            dbs = jnp.zeros((1, SGU_CHUNK), F32)
            for c in range(ts // SGU_CHUNK):
                rows = slice(SGU_CHUNK * c, SGU_CHUNK * (c + 1))
                vbc = vb[rows, cols]
                mixed = _nn(wg, vbc) + bs_ref[g]
                zu = zu_ref[rows, cols].astype(F32)
                u, thu = _gelu(zu)
                dout = d_ref[rows, cols].astype(F32)
                dzu_ref[rows, cols] = (dout * mixed * _gelu_grad(zu, thu)).astype(BF16)
                dm = dout * u
                dmb = dm.astype(BF16)
                dws = dws + _nt(dmb, vbc)
                dbs = dbs + jnp.sum(dm.T, axis=0, keepdims=True)
                dvbuf[rows, cols] = _tn(wg, dmb)
            dws_ref[g] += jnp.where(tril, dws, 0.0)
            dbs_ref[pl.ds(g, 1), :] += dbs
        dvln = dvbuf[...]
        dlg_ref[...] += jnp.sum(dvln * xhat, axis=0, keepdims=True)
        dlb_ref[...] += jnp.sum(dvln, axis=0, keepdims=True)
        dzv_ref[...] = (_ln_bwd(dvln, xhat, rstd, lg) * _gelu_grad(zv, thv)).astype(BF16)

    tile = pl.BlockSpec((ts, 512), lambda i: (i, 0))
    vec = jax.ShapeDtypeStruct((1, 512), F32)
    return pl.pallas_call(
        body, name=name, grid=(t // ts,),
        in_specs=[pl.BlockSpec((ts, 512), lambda i: (i, 1)), pl.BlockSpec((ts, 512), lambda i: (i, 2)),
                  pl.BlockSpec((ts, 512), lambda i: (i, 1)), _row(512), _row(512), _full((4, 128, 128)),
                  _full((4, 128, 128))],
        out_specs=[tile, tile, _full((4, 128, 128)), _full((4, 128)), _row(512), _row(512)],
        out_shape=[jax.ShapeDtypeStruct((t, 512), BF16), jax.ShapeDtypeStruct((t, 512), BF16),
                   jax.ShapeDtypeStruct((4, 128, 128), F32), jax.ShapeDtypeStruct((4, 128), F32), vec, vec],
        scratch_shapes=[pltpu.VMEM((ts, 512), F32)],
        compiler_params=_params("arbitrary"))(z1, z1, dmix, ln_g, ln_b, w_s, b_rows)


def _row_tile(r):
    for cand in (512, 352, 256, 192, 128, 64, 32, 16, 8):
        if r % cand == 0:
            return cand
    return r


def _sum_slabs(a, name):
    k, r, c = a.shape
    tr = _row_tile(r)

    def body(*refs):
        acc = refs[0][...].astype(F32)
        for ref in refs[1:-1]:
            acc = acc + ref[...].astype(F32)
        refs[-1][...] = acc

    in_specs = [pl.BlockSpec((None, tr, c), functools.partial(lambda i, s: (s, i, 0), s=s)) for s in range(k)]
    return pl.pallas_call(
        body, name=name, grid=(r // tr,), in_specs=in_specs, out_specs=pl.BlockSpec((tr, c), lambda i: (i, 0)),
        out_shape=jax.ShapeDtypeStruct((r, c), F32), compiler_params=_params("parallel"))(*([a] * k))


def _adamw_math(w, g, m, v):
    mn = ADAM_B1 * m + (1.0 - ADAM_B1) * g
    vn = ADAM_B2 * v + (1.0 - ADAM_B2) * (g * g)
    m_hat = mn / (1.0 - ADAM_B1 ** ADAM_STEP)
    v_hat = vn / (1.0 - ADAM_B2 ** ADAM_STEP)
    return -ADAM_LR * (m_hat / (jnp.sqrt(v_hat) + ADAM_EPS) + ADAM_WD * w), mn, vn


def _reduce_adamw(landing, w, m, v, name, layer=None, into=None):
    k, r, c = landing.shape
    tr = _row_tile(r)
    n_into = 0 if into is None else 4

    def body(*refs):
        slabs, (w_ref, m_ref, v_ref) = refs[:k], refs[k:k + 3]
        g_ref, d_ref, mo_ref, vo_ref = refs[k + 3 + n_into:]
        g = slabs[0][...].astype(F32)
        for ref in slabs[1:]:
            g = g + ref[...].astype(F32)
        g_ref[...] = g
        d_ref[...], mo_ref[...], vo_ref[...] = _adamw_math(w_ref[...], g, m_ref[...], v_ref[...])

    if layer is None:
        spec = pl.BlockSpec((tr, c), lambda i: (i, 0))
    else:
        spec = pl.BlockSpec((None, tr, c), lambda i: (layer, i, 0))
    in_specs = [pl.BlockSpec((None, tr, c), functools.partial(lambda i, s: (s, i, 0), s=s)) for s in range(k)]
    in_specs += [spec] * 3 + [ANY] * n_into
    shape = jax.ShapeDtypeStruct(w.shape, F32)
    return pl.pallas_call(
        body, name=name, grid=(r // tr,), in_specs=in_specs, out_specs=[spec] * 4, out_shape=[shape] * 4,
        input_output_aliases={k + 3 + j: j for j in range(n_into)},
        compiler_params=_params("parallel"))(*([landing] * k), w, m, v, *(into or ()))


def _adamw(w, g, m, v, name):
    r, c = w.shape
    tr = _row_tile(r)

    def body(w_ref, g_ref, m_ref, v_ref, d_ref, mo_ref, vo_ref):
        d_ref[...], mo_ref[...], vo_ref[...] = _adamw_math(w_ref[...], g_ref[...], m_ref[...], v_ref[...])

    spec = pl.BlockSpec((tr, c), lambda i: (i, 0))
    shape = jax.ShapeDtypeStruct((r, c), F32)
    return pl.pallas_call(
        body, name=name, grid=(r // tr,), in_specs=[spec] * 4, out_specs=[spec] * 3, out_shape=[shape] * 3,
        compiler_params=_params("parallel"))(w, g, m, v)


ANY = pl.BlockSpec(memory_space=pl.ANY)


def _all_gather(block, name):
    r, c_dim = block.shape

    def body(x_ref, out_ref, token, send_sems, recv_sems, local_sem):
        token[...] = jnp.zeros_like(token)
        x, y, c = lax.axis_index("x"), lax.axis_index("y"), lax.axis_index("c")
        me, sibling = (x, y, c), (x, y, 1 - c)
        chips = [(1 - x, y), (x, 1 - y), (1 - x, 1 - y)]

        def rows(px, py, pc):
            return out_ref.at[4 * px + 2 * py + pc]

        def copy(k, blk, to, src=None):
            return pltpu.make_async_remote_copy(
                src_ref=rows(*blk) if src is None else src, dst_ref=rows(*blk), send_sem=send_sems.at[k],
                recv_sem=recv_sems.at[k], device_id=to, device_id_type=MESH)

        mine = pltpu.make_async_copy(x_ref, rows(*me), local_sem)
        mine.start()
        first = [copy(0, me, sibling, src=x_ref)]
        first += [copy(1 + j, me, (*chip, c), src=x_ref) for j, chip in enumerate(chips)]
        for cp in first:
            cp.start()
        passed = [copy(4 + j, (*chip, c), sibling) for j, chip in enumerate(chips)]
        for j, chip in enumerate(chips):
            copy(1 + j, (*chip, c), me).wait_recv()
            passed[j].start()
        copy(0, sibling, me).wait_recv()
        for j, chip in enumerate(chips):
            copy(4 + j, (*chip, 1 - c), me).wait_recv()
        for cp in first + passed:
            cp.wait_send()
        mine.wait()

    return pl.pallas_call(
        body, name=name, in_specs=[ANY], out_specs=[ANY, pl.BlockSpec(memory_space=pltpu.VMEM)],
        out_shape=[jax.ShapeDtypeStruct((N_DEV, r, c_dim), block.dtype), jax.ShapeDtypeStruct((8, 128), F32)],
        scratch_shapes=[pltpu.SemaphoreType.DMA((7,)), pltpu.SemaphoreType.DMA((7,)), pltpu.SemaphoreType.DMA],
    )(block)


HBM = pl.BlockSpec(memory_space=pltpu.HBM)
SEM = pl.BlockSpec(memory_space=pltpu.SEMAPHORE)
EFFECT = pltpu.SideEffectType.DATAFLOW_SIDE_EFFECTING


def _exchange_copies(scatter, src_refs, land_refs, send_sems, recv_sems, local_sems):
    x, y, c = lax.axis_index("x"), lax.axis_index("y"), lax.axis_index("c")
    me = 4 * x + 2 * y + c
    sends, arrivals, locals_ = [], [], []
    for a, (src, land) in enumerate(zip(src_refs, land_refs)):
        def pick(idx, src=src):
            return src.at[idx] if scatter else src

        locals_.append(pltpu.make_async_copy(pick(me), land.at[me], local_sems.at[a]))
        for r in range(1, N_DEV):
            px = 1 - x if r & 4 else x
            py = 1 - y if r & 2 else y
            pc = 1 - c if r & 1 else c
            peer, s = 4 * px + 2 * py + pc, 7 * a + r - 1
            sends.append(pltpu.make_async_remote_copy(
                src_ref=pick(peer), dst_ref=land.at[me], send_sem=send_sems.at[s], recv_sem=recv_sems.at[s],
                device_id=(px, py, pc), device_id_type=MESH))
            arrivals.append(pltpu.make_async_remote_copy(
                src_ref=pick(peer), dst_ref=land.at[peer], send_sem=send_sems.at[s], recv_sem=recv_sems.at[s],
                device_id=(px, py, pc), device_id_type=MESH))
    return sends, arrivals, locals_


def _exchange_start(srcs, scatter, name):
    n = len(srcs)
    lands = [lax.empty((N_DEV,) + s.shape[-2:], s.dtype) for s in srcs]

    def body(*refs):
        src_refs, land_refs = refs[:n], refs[n:2 * n]
        send_sems, recv_sems, local_sems = refs[2 * n:2 * n + 3]
        token = refs[-1]
        sends, _, locals_ = _exchange_copies(scatter, src_refs, land_refs, send_sems, recv_sems, local_sems)
        for cp in locals_ + sends:
            cp.start()
        token[...] = jnp.zeros_like(token)

    res = pl.pallas_call(
        body, name=name,
        out_shape=[pltpu.SemaphoreType.DMA((7 * n,)), pltpu.SemaphoreType.DMA((7 * n,)), pltpu.SemaphoreType.DMA((n,))]
        + [pltpu.HBM(a.shape, a.dtype) for a in list(srcs) + lands] + [jax.ShapeDtypeStruct((8, 128), F32)],
        in_specs=[HBM] * (2 * n), out_specs=[SEM] * 3 + [HBM] * (2 * n) + [pl.BlockSpec(memory_space=pltpu.VMEM)],
        input_output_aliases={i: 3 + i for i in range(2 * n)},
        compiler_params=pltpu.CompilerParams(has_side_effects=EFFECT),
    )(*[pltpu.with_memory_space_constraint(a, pltpu.HBM) for a in list(srcs) + lands])
    return (n, scatter, res[:3], res[3:3 + 2 * n]), res[-1]


def _exchange_wait(handle, after, name):
    n, scatter, sems, thru = handle

    def body(*refs):
        src_refs, land_refs = refs[:n], refs[n:2 * n]
        send_sems, recv_sems, local_sems = refs[2 * n:2 * n + 3]
        sends, arrivals, locals_ = _exchange_copies(scatter, src_refs, land_refs, send_sems, recv_sems, local_sems)
        for cp in arrivals:
            cp.wait_recv()
        for cp in sends:
            cp.wait_send()
        for cp in locals_:
            cp.wait()

    res = pl.pallas_call(
        body, name=name, out_shape=[pltpu.HBM(a.shape, a.dtype) for a in thru],
        in_specs=[HBM] * (2 * n) + [SEM] * 3 + [ANY], out_specs=[HBM] * (2 * n),
        input_output_aliases={i: i for i in range(2 * n)},
        compiler_params=pltpu.CompilerParams(has_side_effects=EFFECT),
    )(*thru, *sems, after)
    return res[n:]


def _behind(tokens, a):
    zero = sum(tok[0, 0] for tok in tokens)
    return jax.tree.map(lambda v: v + zero.astype(v.dtype), a)


def _perm_heads(a, perm, axis):
    idx = [slice(None)] * a.ndim
    parts = []
    for h in perm:
        idx[axis] = slice(64 * h, 64 * (h + 1))
        parts.append(a[tuple(idx)])
    idx[axis] = slice(512, None)
    if a.shape[axis] > 512:
        parts.append(a[tuple(idx)])
    return jnp.concatenate(parts, axis=axis)


Q_INV = tuple(int(i) for i in np.argsort(Q_PERM))


def _in0_to_kernel(a, axis):
    a = _perm_heads(a, Q_PERM, axis)
    idx = [slice(None)] * a.ndim

    def cut(lo, hi):
        idx[axis] = slice(lo, hi)
        return a[tuple(idx)]

    return jnp.concatenate([cut(0, 512), cut(768, 1792), cut(512, 768)], axis=axis)


def _in0_from_kernel(a, axis):
    idx = [slice(None)] * a.ndim

    def cut(lo, hi):
        idx[axis] = slice(lo, hi)
        return a[tuple(idx)]

    a = jnp.concatenate([cut(0, 512), cut(1536, 1792), cut(512, 1536)], axis=axis)
    return _perm_heads(a, Q_INV, axis)


def _f32_as_u16_rows(v, rows):
    bits = lax.bitcast_convert_type(v, jnp.uint16).reshape(-1)
    return jnp.pad(bits, (0, rows * D - bits.shape[0])).reshape(rows, D)


def _pad_rows(v, rows):
    v = v.reshape(-1)
    return jnp.pad(v, (0, rows * D - v.shape[0])).reshape(rows, D)


def kernel(x, mix_norm, a_w_in, a_b_in, a_sinks, a_conv_w, a_conv_b, a_cln_g, a_cln_b, a_w_out, c_w_in, c_w_pool, c_pool_scale, c_sln_g, c_sln_b, c_w_s, c_b_s, c_w_out, ffn_norm, ffn_w_gate, ffn_w_up, ffn_w_down, final_norm, loss_target, m_mix_norm, m_a_w_in, m_a_b_in, m_a_sinks, m_a_conv_w, m_a_conv_b, m_a_cln_g, m_a_cln_b, m_a_w_out, m_c_w_in, m_c_w_pool, m_c_pool_scale, m_c_sln_g, m_c_sln_b, m_c_w_s, m_c_b_s, m_c_w_out, m_ffn_norm, m_ffn_w_gate, m_ffn_w_up, m_ffn_w_down, m_final_norm, v_mix_norm, v_a_w_in, v_a_b_in, v_a_sinks, v_a_conv_w, v_a_conv_b, v_a_cln_g, v_a_cln_b, v_a_w_out, v_c_w_in, v_c_w_pool, v_c_pool_scale, v_c_sln_g, v_c_sln_b, v_c_w_s, v_c_b_s, v_c_w_out, v_ffn_norm, v_ffn_w_gate, v_ffn_w_up, v_ffn_w_down, v_final_norm):
    bsz, seq, _ = x.shape
    t = bsz * seq
    w_in = dict(mix_norm=mix_norm, a_w_in=a_w_in, a_b_in=a_b_in, a_sinks=a_sinks, a_conv_w=a_conv_w, a_conv_b=a_conv_b,
                a_cln_g=a_cln_g, a_cln_b=a_cln_b, a_w_out=a_w_out, c_w_in=c_w_in, c_w_pool=c_w_pool,
                c_pool_scale=c_pool_scale, c_sln_g=c_sln_g, c_sln_b=c_sln_b, c_w_s=c_w_s, c_b_s=c_b_s, c_w_out=c_w_out,
                ffn_norm=ffn_norm, ffn_w_gate=ffn_w_gate, ffn_w_up=ffn_w_up, ffn_w_down=ffn_w_down, final_norm=final_norm)
    m_in = dict(mix_norm=m_mix_norm, a_w_in=m_a_w_in, a_b_in=m_a_b_in, a_sinks=m_a_sinks, a_conv_w=m_a_conv_w,
                a_conv_b=m_a_conv_b, a_cln_g=m_a_cln_g, a_cln_b=m_a_cln_b, a_w_out=m_a_w_out, c_w_in=m_c_w_in,
                c_w_pool=m_c_w_pool, c_pool_scale=m_c_pool_scale, c_sln_g=m_c_sln_g, c_sln_b=m_c_sln_b, c_w_s=m_c_w_s,
                c_b_s=m_c_b_s, c_w_out=m_c_w_out, ffn_norm=m_ffn_norm, ffn_w_gate=m_ffn_w_gate, ffn_w_up=m_ffn_w_up,
                ffn_w_down=m_ffn_w_down, final_norm=m_final_norm)
    v_in = dict(mix_norm=v_mix_norm, a_w_in=v_a_w_in, a_b_in=v_a_b_in, a_sinks=v_a_sinks, a_conv_w=v_a_conv_w,
                a_conv_b=v_a_conv_b, a_cln_g=v_a_cln_g, a_cln_b=v_a_cln_b, a_w_out=v_a_w_out, c_w_in=v_c_w_in,
                c_w_pool=v_c_w_pool, c_pool_scale=v_c_pool_scale, c_sln_g=v_c_sln_g, c_sln_b=v_c_sln_b, c_w_s=v_c_w_s,
                c_b_s=v_c_b_s, c_w_out=v_c_w_out, ffn_norm=v_ffn_norm, ffn_w_gate=v_ffn_w_gate, ffn_w_up=v_ffn_w_up,
                ffn_w_down=v_ffn_w_down, final_norm=v_final_norm)

    small = jnp.concatenate([a_conv_w[0].reshape(-1), c_pool_scale[0], c_sln_g[0], c_sln_b[0]])
    first_bits = lax.bitcast_convert_type(jnp.concatenate([a_w_in[0].T, a_w_out[0]], axis=0).astype(BF16), jnp.uint16)
    gathered, tok = _all_gather(jnp.concatenate([first_bits, _f32_as_u16_rows(small, W_MISC_ROWS)], axis=0), "gather_mixer0")

    def ffn_shards(l):
        return [ffn_w_gate[l].T.astype(BF16), ffn_w_up[l].T.astype(BF16), ffn_w_down[l].astype(BF16)]

    ffn0_h, tok = _exchange_start(_behind([tok], ffn_shards(0)), False, "gather_ffn0_start")
    mix1_h, tok = _exchange_start(_behind([tok], [c_w_in[0].T.astype(BF16), c_w_out[0].astype(BF16)]), False,
                                  "gather_mixer1_start")
    ffn1_h, tok = _exchange_start(_behind([tok], ffn_shards(1)), False, "gather_ffn1_start")

    a_in_full = lax.bitcast_convert_type(gathered[:, :224].reshape(IN0, D), BF16)
    a_out_full = lax.bitcast_convert_type(gathered[:, 224:352].reshape(D, D), BF16)
    small_all = lax.bitcast_convert_type(
        gathered[:, 352:].reshape(N_DEV, -1)[:, :2 * SMALL_SHARD].reshape(N_DEV, SMALL_SHARD, 2), F32)
    conv_w = small_all[:, :31 * 64].reshape(N_DEV, 31, 64).transpose(1, 0, 2).reshape(31, 512)
    conv_w = jnp.pad(conv_w, ((0, HALO - CONV_K), (0, 0)))
    pool_scale = small_all[:, 31 * 64:31 * 64 + 64].reshape(1, 512)
    sln_g = small_all[:, 31 * 64 + 64:31 * 64 + 128].reshape(1, 512)
    sln_b = small_all[:, 31 * 64 + 128:].reshape(1, 512)

    wt_in0 = _in0_to_kernel(a_in_full, 0)
    b_in0 = _in0_to_kernel(a_b_in, 1)
    w_out0 = _perm_heads(a_out_full, Q_PERM, 0)
    b_rows = jnp.broadcast_to(c_b_s[0][:, :, None], (4, 128, 128))
    conv_b, cln_g, cln_b = a_conv_b, a_cln_g, a_cln_b

    h0 = x.reshape(t, D)
    target = loss_target.reshape(t, D)
    z0, hn0 = _norm_proj(h0, _behind([tok], mix_norm[0:1]), wt_in0, b_in0, "in_proj0")
    attn, tok = _attn_fwd(z0, a_sinks, bsz, "attn_fwd")
    conv, conv_y = _conv_fwd(z0, conv_w, conv_b, cln_g, _behind([tok], cln_b), bsz, "conv_fwd")
    wtg0, wtu0, wd0 = (w.reshape(D_FF, D) for w in _exchange_wait(ffn0_h, conv, "gather_ffn0_wait"))
    h1, h2, hnf0, gate0, up0 = _ffn_fwd(h0, attn, conv, w_out0, ffn_norm[0:1], wtg0, wtu0, wd0, "ffn_fwd0")
    wt_in1, w_out1 = (w.reshape(-1, D) for w in _exchange_wait(mix1_h, h2, "gather_mixer1_wait"))
    z1, hn1 = _norm_proj(h2, mix_norm[1:2], wt_in1, None, "in_proj1")
    pool, tok = _pool_fwd(z1, c_w_pool[0], pool_scale, bsz, "pool_fwd")
    sgu = _sgu_fwd(z1, sln_g, _behind([tok], sln_b), c_w_s[0], b_rows, "sgu_fwd")
    wtg1, wtu1, wd1 = (w.reshape(D_FF, D) for w in _exchange_wait(ffn1_h, sgu, "gather_ffn1_wait"))
    h3, h4, hnf1, gate1, up1 = _ffn_fwd(h2, pool, sgu, w_out1, ffn_norm[1:2], wtg1, wtu1, wd1, "ffn_fwd1")

    def blocks(g):
        return g.reshape(N_DEV, g.shape[0] // N_DEV, D)

    dh4, d_final_norm, loss_part = _loss_head(h4, final_norm.reshape(1, D), target, "loss_head")
    dh3, dmix1, dgate1, dup1, act1, d_fn1 = _ffn_bwd(dh4, h3, ffn_norm[1:2], gate1, up1, wtg1, wtu1, wd1, w_out1, "ffn_bwd1")
    gw_ffn1 = [_mm_tn(dgate1, hnf1, "dw_gate1"), _mm_tn(dup1, hnf1, "dw_up1"), _mm_tn(act1, dh4, "dw_down1")]
    ffn1_g, tok = _exchange_start([blocks(g) for g in gw_ffn1], True, "scatter_ffn1_start")
    gw_c_out = _mm_tn_pieces([pool, sgu], dh3, "dw_out1")
    dzp, d_w_pool, d_pool_scale = _pool_bwd(z1, dmix1, c_w_pool[0], _behind([tok], pool_scale), bsz, "pool_bwd")
    dzu, dzv, d_w_s, d_b_s, d_sln_g, d_sln_b = _sgu_bwd(z1, dmix1, sln_g, sln_b, c_w_s[0], b_rows, "sgu_bwd")
    dh2, d_mn1 = _proj_bwd_norm([(dzp, 0), (dzu, 512), (dzv, 1024)], wt_in1, h2, dh3, mix_norm[1:2], BF16, "in_proj1_bwd")
    gw_c_in = _mm_tn_pieces([dzp, dzu, dzv], hn1, "dw_in1")
    mix1_g, tok = _exchange_start([blocks(gw_c_in), blocks(gw_c_out)], True, "scatter_mixer1_start")
    dh1, dmix0, dgate0, dup0, act0, d_fn0 = _ffn_bwd(dh2, h1, _behind([tok], ffn_norm[0:1]), gate0, up0, wtg0, wtu0, wd0,
                                                      w_out0, "ffn_bwd0")
    gw_ffn0 = [_mm_tn(dgate0, hnf0, "dw_gate0"), _mm_tn(dup0, hnf0, "dw_up0"), _mm_tn(act0, dh2, "dw_down0")]
    ffn0_g, tok = _exchange_start([blocks(g) for g in gw_ffn0], True, "scatter_ffn0_start")
    gw_a_out = _perm_heads(_mm_tn_pieces([attn, conv], dh1, "dw_out0"), Q_INV, 0)
    dq, dkv, d_sink_row, d_bq, d_bkv = _attn_bwd(z0, dmix0, _behind([tok], a_sinks), bsz, "attn_bwd")
    dca, dcg, d_conv_w, d_conv_b, d_cln_g, d_cln_b, d_ba, d_bg = _conv_bwd(z0, conv_y, dmix0, conv_w, cln_g, cln_b, bsz, "conv_bwd")
    gw_a_in = _in0_from_kernel(_mm_tn_pieces([dq, dca, dcg, dkv], hn0, "dw_in0"), 0)
    mix0_g, tok = _exchange_start([blocks(gw_a_in), blocks(gw_a_out)], True, "scatter_mixer0_start")
    dx, d_mn0 = _proj_bwd_norm([(dq, 0), (dca, 512), (dcg, 1024), (dkv, 1536)], wt_in0, h0, dh1,
                               _behind([tok], mix_norm[0:1]), F32, "in_proj0_bwd")
    d_b_in = _in0_from_kernel(jnp.concatenate([d_bq, d_ba, d_bg, d_bkv], axis=1), 1)

    rep = dict(mix_norm=jnp.concatenate([d_mn0, d_mn1], axis=0), a_b_in=d_b_in, a_sinks=d_sink_row[:, :8],
               a_conv_b=d_conv_b, a_cln_g=d_cln_g, a_cln_b=d_cln_b, c_w_pool=d_w_pool[None], c_w_s=d_w_s[None],
               c_b_s=d_b_s[None], ffn_norm=jnp.concatenate([d_fn0, d_fn1], axis=0), final_norm=d_final_norm.reshape(D))
    rep_flat = jnp.concatenate([rep[nm].reshape(-1) for nm in REP_NAMES] + [loss_part.reshape(1)])
    rep_flat = jnp.pad(rep_flat, (0, N_DEV * REP_ROWS * D - rep_flat.shape[0])).reshape(N_DEV, REP_ROWS, D)
    small_g = jnp.concatenate([
        d_conv_w[:CONV_K].reshape(31, N_DEV, 64).transpose(1, 0, 2).reshape(N_DEV, 31 * 64),
        d_pool_scale.reshape(N_DEV, 64), d_sln_g.reshape(N_DEV, 64), d_sln_b.reshape(N_DEV, 64)], axis=1)
    small_g = jnp.pad(small_g, ((0, 0), (0, G_SMALL_ROWS * D - SMALL_SHARD))).reshape(N_DEV, G_SMALL_ROWS, D)
    tail_g, tok = _exchange_start([jnp.concatenate([small_g, rep_flat], axis=1)], True, "scatter_tail_start")

    names = list(w_in)
    g_out, delta, new_m, new_v = {}, {}, {}, {}
    column_sharded = ("a_w_in", "c_w_in", "ffn_w_gate", "ffn_w_up")

    def rows_of(a, nm):
        return jnp.swapaxes(a, 1, 2) if nm in column_sharded else a

    def reduce_adamw(nm, landing, layer, into=None):
        args = [rows_of(d[nm], nm) for d in (w_in, m_in, v_in)]
        if args[0].shape[0] == 1:
            args, layer = [a[0] for a in args], None
        return _reduce_adamw(landing, *args, "adamw_%s_%s" % (nm, layer), layer=layer, into=into)

    def keep(nm, res):
        res = [r if r.ndim == 3 else r[None] for r in res]
        g_out[nm], delta[nm], new_m[nm], new_v[nm] = (rows_of(r, nm) for r in res)

    ffn_names = ("ffn_w_gate", "ffn_w_up", "ffn_w_down")
    landed = _exchange_wait(ffn1_g, tok, "scatter_ffn1_wait")
    ffn_res = [reduce_adamw(nm, a, 1) for nm, a in zip(ffn_names, landed)]
    landed = _exchange_wait(mix1_g, ffn_res[-1][0], "scatter_mixer1_wait")
    for nm, a in zip(("c_w_in", "c_w_out"), landed):
        keep(nm, reduce_adamw(nm, a, 0))
    landed = _exchange_wait(ffn0_g, g_out["c_w_out"], "scatter_ffn0_wait")
    for nm, a, res in zip(ffn_names, landed, ffn_res):
        keep(nm, reduce_adamw(nm, a, 0, into=res))
    landed = _exchange_wait(mix0_g, g_out["ffn_w_down"], "scatter_mixer0_wait")
    for nm, a in zip(("a_w_in", "a_w_out"), landed):
        keep(nm, reduce_adamw(nm, a, 0))
    g_tail = _sum_slabs(_exchange_wait(tail_g, g_out["a_w_out"], "scatter_tail_wait")[0], "sum_tail")
    rep_all = _all_gather(g_tail[G_SMALL_ROWS:], "gather_replicated_grads")[0].reshape(-1)
    small_r = g_tail[:G_SMALL_ROWS].reshape(-1)[:SMALL_SHARD]
    g_out.update(
        a_conv_w=small_r[:31 * 64].reshape(1, 31, 64), c_pool_scale=small_r[31 * 64:31 * 64 + 64].reshape(1, 64),
        c_sln_g=small_r[31 * 64 + 64:31 * 64 + 128].reshape(1, 64), c_sln_b=small_r[31 * 64 + 128:].reshape(1, 64))
    off = 0
    for nm in REP_NAMES:
        n = int(np.prod(w_in[nm].shape))
        g_out[nm] = rep_all[off:off + n].reshape(w_in[nm].shape)
        off += n
    loss = rep_all[off]
    for group, rows, label in ((("a_conv_w", "c_pool_scale", "c_sln_g", "c_sln_b"), G_SMALL_ROWS, "adamw_small_sharded"),
                               (REP_NAMES, N_DEV * REP_ROWS, "adamw_replicated")):
        flat = [_pad_rows(jnp.concatenate([d[nm].reshape(-1) for nm in group]), rows) for d in (w_in, g_out, m_in, v_in)]
        res = [r.reshape(-1) for r in _adamw(*flat, label)]
        off = 0
        for nm in group:
            n = int(np.prod(w_in[nm].shape))
            delta[nm], new_m[nm], new_v[nm] = (r[off:off + n].reshape(w_in[nm].shape) for r in res)
            off += n

    grad_x = dx.reshape(bsz, seq, D)
    return (loss, grad_x, *[g_out[nm] for nm in names], *[delta[nm] for nm in names],
            *[new_m[nm] for nm in names], *[new_v[nm] for nm in names])
```

```python
import functools

import jax
import jax.numpy as jnp
import numpy as np
from jax import lax
from jax.experimental import pallas as pl
from jax.experimental.pallas import tpu as pltpu

F32 = jnp.float32
BF16 = jnp.bfloat16
MESH = pl.DeviceIdType.MESH

D = 1024
N_DEV = 8
EPS = 1e-5
HEAD_PAIRS = 4
ATT_BLK = 128
CONV_K = 31
HALO = 32
D_FF = 2816
FF_TILE_FWD = D_FF // 2
FF_TILE_BWD = D_FF // 2
IN0 = 1792
IN1 = 1536
POOL_WINDOWS = (2, 4, 8, 16)
SGU_CHUNK = 128
GELU_C = 0.7978845608028654
GELU_A = 0.044715
ADAM_LR, ADAM_B1, ADAM_B2, ADAM_EPS, ADAM_WD, ADAM_STEP = 0.001, 0.9, 0.999, 1e-08, 0.01, 10
VMEM_LIMIT = 56 << 20

SMALL_SHARD = 31 * 64 + 3 * 64
W_MISC_ROWS = 16
G_MISC_ROWS = 32
G_SMALL_ROWS = 8
REP_ROWS = G_MISC_ROWS - G_SMALL_ROWS
REP_NAMES = ("mix_norm", "a_b_in", "a_sinks", "a_conv_b", "a_cln_g", "a_cln_b", "c_w_pool", "c_w_s", "c_b_s",
             "ffn_norm", "final_norm")
Q_PERM = (0, 4, 1, 5, 2, 6, 3, 7)


def _params(*sem):
    return pltpu.CompilerParams(dimension_semantics=sem, vmem_limit_bytes=VMEM_LIMIT)


def _nn(a, b):
    return jnp.dot(a, b, preferred_element_type=F32)


def _nt(a, b):
    return lax.dot_general(a, b, (((1,), (1,)), ((), ())), preferred_element_type=F32)


def _tn(a, b):
    return lax.dot_general(a, b, (((0,), (0,)), ((), ())), preferred_element_type=F32)


def _tile(n, want=512):
    t = min(want, n)
    assert n % t == 0, (n, t)
    return t


def _seq_tile(s):
    return 512 if s >= 1024 else s // 2


def _rms(x, g):
    r = lax.rsqrt(jnp.mean(x * x, axis=-1, keepdims=True) + EPS)
    return x * r * g, r


def _rms_bwd(x, g, d_y):
    r = lax.rsqrt(jnp.mean(x * x, axis=-1, keepdims=True) + EPS)
    xr = x * r
    u = d_y * g
    d_x = r * (u - xr * jnp.mean(u * xr, axis=-1, keepdims=True))
    return d_x, jnp.sum(d_y * xr, axis=0, keepdims=True)


def _ln(y, g, b):
    mu = jnp.mean(y, axis=-1, keepdims=True)
    yc = y - mu
    rstd = lax.rsqrt(jnp.mean(yc * yc, axis=-1, keepdims=True) + EPS)
    xhat = yc * rstd
    return xhat * g + b, xhat, rstd


def _ln_bwd(d_o, xhat, rstd, g):
    dxh = d_o * g
    return rstd * (dxh - jnp.mean(dxh, axis=-1, keepdims=True) - xhat * jnp.mean(dxh * xhat, axis=-1, keepdims=True))


def _gelu(x):
    th = jnp.tanh(GELU_C * (x + GELU_A * x * x * x))
    return 0.5 * x * (1.0 + th), th


def _gelu_grad(x, th):
    return 0.5 * (1.0 + th) + 0.5 * x * (1.0 - th * th) * GELU_C * (1.0 + 3.0 * GELU_A * x * x)


def _row(c):
    return pl.BlockSpec((1, c), lambda *_: (0, 0))


def _full(shape):
    return pl.BlockSpec(shape, lambda *_: (0,) * len(shape))


def _norm_proj(h, g, wt, bias, name):
    t, n = h.shape[0], wt.shape[0]
    tm = _tile(t)
    has_bias = bias is not None

    def body(*refs):
        h_ref, g_ref, wt_ref = refs[:3]
        z_ref, hn_ref = refs[-2:]
        hn = _rms(h_ref[...].astype(F32), g_ref[...])[0].astype(BF16)
        hn_ref[...] = hn
        z = _nt(hn, wt_ref[...])
        if has_bias:
            z = z + refs[3][...]
        z_ref[...] = z.astype(BF16)

    in_specs = [pl.BlockSpec((tm, D), lambda i: (i, 0)), _row(D), _full((n, D))]
    args = [h, g, wt]
    if has_bias:
        in_specs.append(_row(n))
        args.append(bias)
    return pl.pallas_call(
        body, name=name, grid=(t // tm,), in_specs=in_specs,
        out_specs=[pl.BlockSpec((tm, n), lambda i: (i, 0)), pl.BlockSpec((tm, D), lambda i: (i, 0))],
        out_shape=[jax.ShapeDtypeStruct((t, n), BF16), jax.ShapeDtypeStruct((t, D), BF16)],
        compiler_params=_params("parallel"))(*args)


def _ff_pieces(tf, width=256):
    return [(c0, min(width, tf - c0)) for c0 in range(0, tf, width)]


def _ffn_fwd(h_prev, a, b, w_out, g, wtg, wtu, wd, name, head=None):
    t = h_prev.shape[0]
    tm, tf = _tile(t), FF_TILE_FWD
    nf = D_FF // tf
    n_head = 0 if head is None else 2

    def body(*refs):
        hp_ref, a_ref, b_ref, wa_ref, wb_ref, g_ref, wtg_ref, wtu_ref, wd_ref = refs[:9]
        hmid_ref, o_ref, hn_ref, gate_ref, up_ref = refs[9 + n_head:14 + n_head]
        acc, act = refs[-2:]
        i, f = pl.program_id(0), pl.program_id(1)

        if head is not None:
            fg_ref, t_ref = refs[9:11]
            dfg_ref, loss_ref = refs[16:18]

            @pl.when((i == 0) & (f == 0))
            def _():
                dfg_ref[...] = jnp.zeros_like(dfg_ref)
                loss_ref[...] = jnp.zeros_like(loss_ref)

        @pl.when(f == 0)
        def _():
            x = hp_ref[...].astype(F32) + _nn(a_ref[...], wa_ref[...]) + _nn(b_ref[...], wb_ref[...])
            hmid_ref[...] = x.astype(BF16)
            hn_ref[...] = _rms(x, g_ref[...])[0].astype(BF16)
            acc[...] = x

        hn = hn_ref[...]
        for c0, cw in _ff_pieces(tf):
            rows, cols = slice(c0, c0 + cw), slice(c0, c0 + cw)
            gate = _nt(hn, wtg_ref[rows, :])
            up = _nt(hn, wtu_ref[rows, :])
            gate_ref[:, cols] = gate.astype(BF16)
            up_ref[:, cols] = up.astype(BF16)
            act[:, cols] = (gate * jax.nn.sigmoid(gate) * up).astype(BF16)
        acc[...] += _nn(act[...], wd_ref[...])

        @pl.when(f == nf - 1)
        def _():
            if head is None:
                o_ref[...] = acc[...].astype(BF16)
            else:
                x, gv = acc[...], fg_ref[...]
                err = _rms(x, gv)[0] - t_ref[...]
                loss_ref[...] += 0.5 * jnp.sum(jnp.mean(err * err, axis=-1, keepdims=True), axis=0, keepdims=True)
                d_x, d_g = _rms_bwd(x, gv, err * (1.0 / D))
                o_ref[...] = d_x.astype(BF16)
                dfg_ref[...] += d_g

    tok = pl.BlockSpec((tm, D), lambda i, f: (i, 0))
    wsp = pl.BlockSpec((tf, D), lambda i, f: (f, 0))
    mid = pl.BlockSpec((tm, tf), lambda i, f: (i, f))
    half = pl.BlockSpec((tm, 512), lambda i, f: (i, 0))
    res = jax.ShapeDtypeStruct((t, D), BF16)
    in_specs = [tok, half, half, pl.BlockSpec((512, D), lambda i, f: (0, 0)), pl.BlockSpec((512, D), lambda i, f: (1, 0)),
                _row(D), wsp, wsp, wsp]
    out_specs = [tok, tok, tok, mid, mid]
    out_shape = [res, res, res, jax.ShapeDtypeStruct((t, D_FF), BF16), jax.ShapeDtypeStruct((t, D_FF), BF16)]
    if head is not None:
        in_specs += [_row(D), tok]
        out_specs += [_row(D), _row(1)]
        out_shape += [jax.ShapeDtypeStruct((1, D), F32), jax.ShapeDtypeStruct((1, 1), F32)]
    return pl.pallas_call(
        body, name=name, grid=(t // tm, nf), in_specs=in_specs, out_specs=out_specs, out_shape=out_shape,
        scratch_shapes=[pltpu.VMEM((tm, D), F32), pltpu.VMEM((tm, tf), BF16)],
        compiler_params=_params("parallel" if head is None else "arbitrary", "arbitrary"),
    )(h_prev, a, b, w_out, w_out, g, wtg, wtu, wd, *(head or ()))


def _ffn_bwd(dh, h, g, gate, up, wtg, wtu, wd, w_out, name):
    t = h.shape[0]
    tm, tf = _tile(t), FF_TILE_BWD
    nf = D_FF // tf

    def body(dh_ref, h_ref, g_ref, gate_ref, up_ref, wtg_ref, wtu_ref, wd_ref, wout_ref,
             dhin_ref, dmix_ref, dgate_ref, dup_ref, act_ref, dg_ref, d_hn):
        i, f = pl.program_id(0), pl.program_id(1)

        @pl.when(f == 0)
        def _():
            d_hn[...] = jnp.zeros_like(d_hn)

        @pl.when((i == 0) & (f == 0))
        def _():
            dg_ref[...] = jnp.zeros_like(dg_ref)

        dh = dh_ref[...]
        for c0, cw in _ff_pieces(tf):
            cols = slice(c0, c0 + cw)
            da = _nt(dh, wd_ref[c0:c0 + cw, :])
            gt = gate_ref[:, cols].astype(F32)
            u = up_ref[:, cols].astype(F32)
            sg = jax.nn.sigmoid(gt)
            sil = gt * sg
            act_ref[:, cols] = (sil * u).astype(BF16)
            dup_ref[:, cols] = (da * sil).astype(BF16)
            dgate_ref[:, cols] = (da * u * sg * (1.0 + gt * (1.0 - sg))).astype(BF16)
        d_hn[...] += _nn(dgate_ref[...], wtg_ref[...]) + _nn(dup_ref[...], wtu_ref[...])

        @pl.when(f == nf - 1)
        def _():
            d_x, d_g = _rms_bwd(h_ref[...].astype(F32), g_ref[...], d_hn[...])
            dhin = (dh_ref[...].astype(F32) + d_x).astype(BF16)
            dhin_ref[...] = dhin
            dmix_ref[...] = _nt(dhin, wout_ref[...]).astype(BF16)
            dg_ref[...] += d_g

    tok = pl.BlockSpec((tm, D), lambda i, f: (i, 0))
    wsp = pl.BlockSpec((tf, D), lambda i, f: (f, 0))
    mid = pl.BlockSpec((tm, tf), lambda i, f: (i, f))
    mid_shape = jax.ShapeDtypeStruct((t, D_FF), BF16)
    res = jax.ShapeDtypeStruct((t, D), BF16)
    return pl.pallas_call(
        body, name=name, grid=(t // tm, nf), in_specs=[tok, tok, _row(D), mid, mid, wsp, wsp, wsp, _full((D, D))],
        out_specs=[tok, tok, mid, mid, mid, _row(D)],
        out_shape=[res, res, mid_shape, mid_shape, mid_shape, jax.ShapeDtypeStruct((1, D), F32)],
        scratch_shapes=[pltpu.VMEM((tm, D), F32)],
        compiler_params=_params("arbitrary", "arbitrary"))(dh, h, g, gate, up, wtg, wtu, wd, w_out)


def _proj_bwd_norm(pieces, wt, h, dh, g, dtype, name):
    t = h.shape[0]
    tm = _tile(t)
    n_p = len(pieces)

    def body(*refs):
        p_refs, w_refs = refs[:n_p], refs[n_p:2 * n_p]
        h_ref, dh_ref, g_ref, o_ref, dg_ref = refs[2 * n_p:]

        @pl.when(pl.program_id(0) == 0)
        def _():
            dg_ref[...] = jnp.zeros_like(dg_ref)

        d_hn = _nn(p_refs[0][...], w_refs[0][...])
        for p_ref, w_ref in zip(p_refs[1:], w_refs[1:]):
            d_hn = d_hn + _nn(p_ref[...], w_ref[...])
        d_x, d_g = _rms_bwd(h_ref[...].astype(F32), g_ref[...], d_hn)
        o_ref[...] = (dh_ref[...].astype(F32) + d_x).astype(dtype)
        dg_ref[...] += d_g

    tok = pl.BlockSpec((tm, D), lambda i: (i, 0))
    in_specs = [pl.BlockSpec((tm, a.shape[1]), lambda i: (i, 0)) for a, _ in pieces]
    for a, off in pieces:
        w = a.shape[1]
        assert off % w == 0
        in_specs.append(pl.BlockSpec((w, D), functools.partial(lambda i, blk: (blk, 0), blk=off // w)))
    in_specs += [tok, tok, _row(D)]
    return pl.pallas_call(
        body, name=name, grid=(t // tm,), in_specs=in_specs, out_specs=[tok, _row(D)],
        out_shape=[jax.ShapeDtypeStruct((t, D), dtype), jax.ShapeDtypeStruct((1, D), F32)],
        compiler_params=_params("arbitrary"))(*[a for a, _ in pieces], *([wt] * n_p), h, dh, g)


def _mm_tn(a, b, name):
    t, n = a.shape
    k = b.shape[1]
    tn = n if n <= 1024 else n // 2
    tt = _tile(t, 1024)
    nt = t // tt

    def body(a_ref, b_ref, o_ref, acc):
        s = pl.program_id(1)

        @pl.when(s == 0)
        def _():
            acc[...] = jnp.zeros_like(acc)

        acc[...] += _tn(a_ref[...], b_ref[...].astype(BF16))

        @pl.when(s == nt - 1)
        def _():
            o_ref[...] = acc[...].astype(BF16)

    return pl.pallas_call(
        body, name=name, grid=(n // tn, nt),
        in_specs=[pl.BlockSpec((tt, tn), lambda j, s: (s, j)), pl.BlockSpec((tt, k), lambda j, s: (s, 0))],
        out_specs=pl.BlockSpec((tn, k), lambda j, s: (j, 0)), out_shape=jax.ShapeDtypeStruct((n, k), BF16),
        scratch_shapes=[pltpu.VMEM((tn, k), F32)],
        compiler_params=_params("parallel", "arbitrary"))(a, b)


def _mm_tn_pieces(pieces, b, name):
    t, k = b.shape
    widths = [p.shape[1] for p in pieces]
    n, n_p = sum(widths), len(pieces)
    tt = _tile(t, 1024)
    nt = t // tt

    def body(*refs):
        b_ref, o_ref, acc = refs[n_p:]
        s = pl.program_id(0)

        @pl.when(s == 0)
        def _():
            acc[...] = jnp.zeros_like(acc)

        bb = b_ref[...].astype(BF16)
        off = 0
        for p_ref, w in zip(refs[:n_p], widths):
            acc[off:off + w, :] += _tn(p_ref[...], bb)
            off += w

        @pl.when(s == nt - 1)
        def _():
            o_ref[...] = acc[...].astype(BF16)

    return pl.pallas_call(
        body, name=name, grid=(nt,),
        in_specs=[pl.BlockSpec((tt, w), lambda s: (s, 0)) for w in widths] + [pl.BlockSpec((tt, k), lambda s: (s, 0))],
        out_specs=_full((n, k)), out_shape=jax.ShapeDtypeStruct((n, k), BF16),
        scratch_shapes=[pltpu.VMEM((n, k), F32)], compiler_params=_params("arbitrary"))(*pieces, b)


STACK = HEAD_PAIRS * ATT_BLK


def _attn_valid(first, rows):
    qi = lax.broadcasted_iota(jnp.int32, (rows, 2 * ATT_BLK), 0) % ATT_BLK
    r = lax.broadcasted_iota(jnp.int32, (rows, 2 * ATT_BLK), 1)
    dist = qi + ATT_BLK - r
    return (dist >= 0) & (dist < ATT_BLK) & ((r >= ATT_BLK) | jnp.logical_not(first))


def _stacked(ref, kh, scale):
    lo = lax.broadcasted_iota(jnp.int32, (ATT_BLK, 128), 1) < 64
    keep = lo if kh == 0 else ~lo
    parts = [jnp.where(keep, ref[:, g * 128:(g + 1) * 128] * scale, 0.0).astype(BF16) for g in range(HEAD_PAIRS)]
    return jnp.concatenate(parts, axis=0)


def _unstacked(a0, a1, g):
    lo = lax.broadcasted_iota(jnp.int32, (ATT_BLK, 128), 1) < 64
    rows = slice(g * ATT_BLK, (g + 1) * ATT_BLK)
    return jnp.where(lo, a0[rows], a1[rows])


def _sink_rows(s_ref, kh):
    return jnp.concatenate([jnp.full((ATT_BLK, 128), s_ref[0, kh * 4 + g], F32) for g in range(HEAD_PAIRS)], axis=0)


def _row_sums(a, split):
    hi = a.astype(BF16)
    ones = jnp.ones((2 * ATT_BLK, 128), BF16)
    if not split:
        return _nn(hi, ones)
    lo = (a - hi.astype(F32)).astype(BF16)
    return _nn(hi, ones) + _nn(lo, ones)


def _both(a):
    return jnp.concatenate([a, a], axis=1)


def _attn_probs(qs, kpair, sink, valid):
    s = jnp.where(valid, _nt(qs, kpair), -1e30)
    m = jnp.maximum(jnp.broadcast_to(jnp.max(s, axis=-1, keepdims=True), (s.shape[0], 128)), sink)
    p = jnp.exp(s - _both(m))
    es = jnp.exp(sink - m)
    inv = 1.0 / (_row_sums(p, split=True) + es)
    return p * _both(inv), es * inv


def _attn_probs_head(qm, kpair, sink, valid):
    s = jnp.where(valid, _nt(qm, kpair), -1e30)
    m = jnp.maximum(jnp.max(s, axis=-1, keepdims=True), sink)
    p = jnp.exp(s - m)
    return p * (1.0 / (jnp.sum(p, axis=-1, keepdims=True) + jnp.exp(sink - m)))


def _attn_specs(bsz, order):
    q = pl.BlockSpec((bsz, ATT_BLK, 512), lambda j: (0, order(j), 0))
    kvc = pl.BlockSpec((bsz, ATT_BLK, 256), lambda j: (0, order(j), 6))
    kvp = pl.BlockSpec((bsz, ATT_BLK, 256), lambda j: (0, jnp.maximum(order(j) - 1, 0), 6))
    return q, kvc, kvp


def _window_kv(kvc_ref, kvp_ref):
    kvc, kvp = kvc_ref[...], kvp_ref[...]
    kpair = jnp.concatenate([kvp[:, :128], kvc[:, :128]], axis=0)
    vpair = jnp.concatenate([kvp[:, 128:], kvc[:, 128:]], axis=0)
    return kpair, vpair


def _attn_fwd(z0, sinks, bsz, name):
    t = z0.shape[0]
    seq = t // bsz
    nb = seq // ATT_BLK

    def body(s_ref, q_ref, kvc_ref, kvp_ref, o_ref, token):
        token[...] = jnp.zeros_like(token)
        valid = _attn_valid(pl.program_id(0) == 0, ATT_BLK)
        lo = lax.broadcasted_iota(jnp.int32, (ATT_BLK, 128), 1) < 64
        for b in range(bsz):
            kpair, vpair = _window_kv(kvc_ref.at[b], kvp_ref.at[b])
            for g in range(HEAD_PAIRS):
                qs = q_ref[b, :, g * 128:(g + 1) * 128] * 0.125
                outs = []
                for kh in range(2):
                    qm = jnp.where(lo if kh == 0 else ~lo, qs, 0.0).astype(BF16)
                    p = _attn_probs_head(qm, kpair, s_ref[0, kh * 4 + g], valid)
                    outs.append(_nn(p.astype(BF16), vpair))
                o_ref[b, :, g * 128:(g + 1) * 128] = jnp.where(lo, outs[0], outs[1]).astype(BF16)

    q, kvc, kvp = _attn_specs(bsz, lambda j: j)
    z3 = z0.reshape(bsz, seq, z0.shape[1])
    out, token = pl.pallas_call(
        body, name=name, grid=(nb,),
        in_specs=[pl.BlockSpec(memory_space=pltpu.SMEM), q, kvc, kvp],
        out_specs=[pl.BlockSpec((bsz, ATT_BLK, 512), lambda j: (0, j, 0)), _full((8, 128))],
        out_shape=[jax.ShapeDtypeStruct((bsz, seq, 512), BF16), jax.ShapeDtypeStruct((8, 128), F32)],
        compiler_params=_params("arbitrary"))(sinks, z3, z3, z3)
    return out.reshape(t, 512), token


def _attn_bwd(z0, dmix, sinks, bsz, name):
    t = z0.shape[0]
    seq = t // bsz
    nb = seq // ATT_BLK

    def body(s_ref, q_ref, kvc_ref, kvp_ref, do_ref, dq_ref, dkv_ref, dsink_ref, dbq_ref, dbkv_ref, carry):
        j = pl.program_id(0)

        @pl.when(j == 0)
        def _():
            carry[...] = jnp.zeros_like(carry)
            dsink_ref[...] = jnp.zeros_like(dsink_ref)
            dbq_ref[...] = jnp.zeros_like(dbq_ref)
            dbkv_ref[...] = jnp.zeros_like(dbkv_ref)

        valid = _attn_valid(j == nb - 1, STACK)
        lane = lax.broadcasted_iota(jnp.int32, (1, 128), 1)
        dsink = jnp.zeros((1, 128), F32)
        dbq = [jnp.zeros((1, 128), F32)] * HEAD_PAIRS
        dbkv = jnp.zeros((1, 256), F32)
        for b in range(bsz):
            kpair, vpair = _window_kv(kvc_ref.at[b], kvp_ref.at[b])
            dk = jnp.zeros((2 * ATT_BLK, 128), F32)
            dv = jnp.zeros((2 * ATT_BLK, 128), F32)
            dqs = []
            for kh in range(2):
                qs = _stacked(q_ref.at[b], kh, 0.125)
                dos = _stacked(do_ref.at[b], kh, 1.0)
                p, ps = _attn_probs(qs, kpair, _sink_rows(s_ref, kh), valid)
                dp = _nt(dos, vpair)
                delta = _row_sums(p * dp, split=False)
                ds = (p * (dp - _both(delta))).astype(BF16)
                dqs.append(_nn(ds, kpair))
                dk = dk + _tn(ds, qs)
                dv = dv + _tn(p.astype(BF16), dos)
                psd = ps * delta
                for g in range(HEAD_PAIRS):
                    part = jnp.sum(psd[g * ATT_BLK:(g + 1) * ATT_BLK], axis=0, keepdims=True)
                    dsink = dsink - jnp.where(lane == kh * 4 + g, part, 0.0)
            for g in range(HEAD_PAIRS):
                dq = _unstacked(dqs[0], dqs[1], g) * 0.125
                dq_ref[b, :, g * 128:(g + 1) * 128] = dq.astype(BF16)
                dbq[g] = dbq[g] + jnp.sum(dq, axis=0, keepdims=True)
            dkv = jnp.concatenate([dk[ATT_BLK:], dv[ATT_BLK:]], axis=1) + carry[b]
            dkv_ref[b] = dkv.astype(BF16)
            dbkv = dbkv + jnp.sum(dkv, axis=0, keepdims=True)
            carry[b] = jnp.concatenate([dk[:ATT_BLK], dv[:ATT_BLK]], axis=1)
        dsink_ref[...] += dsink
        dbq_ref[...] += jnp.concatenate(dbq, axis=1)
        dbkv_ref[...] += dbkv

    q, kvc, kvp = _attn_specs(bsz, lambda j: nb - 1 - j)
    z3 = z0.reshape(bsz, seq, z0.shape[1])
    d3 = dmix.reshape(bsz, seq, dmix.shape[1])
    dq, dkv, dsink, dbq, dbkv = pl.pallas_call(
        body, name=name, grid=(nb,),
        in_specs=[pl.BlockSpec(memory_space=pltpu.SMEM), q, kvc, kvp,
                  pl.BlockSpec((bsz, ATT_BLK, 512), lambda j: (0, nb - 1 - j, 0))],
        out_specs=[pl.BlockSpec((bsz, ATT_BLK, 512), lambda j: (0, nb - 1 - j, 0)),
                   pl.BlockSpec((bsz, ATT_BLK, 256), lambda j: (0, nb - 1 - j, 0)), _row(128), _row(512), _row(256)],
        out_shape=[jax.ShapeDtypeStruct((bsz, seq, 512), BF16), jax.ShapeDtypeStruct((bsz, seq, 256), BF16),
                   jax.ShapeDtypeStruct((1, 128), F32), jax.ShapeDtypeStruct((1, 512), F32),
                   jax.ShapeDtypeStruct((1, 256), F32)],
        scratch_shapes=[pltpu.VMEM((bsz, ATT_BLK, 256), F32)],
        compiler_params=_params("arbitrary"))(sinks, z3, z3, z3, d3)
    return dq.reshape(t, 512), dkv.reshape(t, 256), dsink, dbq, dbkv


def _seq_specs(ts, nt, t, width, col):
    per = ts // HALO
    cur = pl.BlockSpec((ts, width), lambda b, i: (b * nt + i, col))
    prev = pl.BlockSpec((HALO, width), lambda b, i: (jnp.maximum((b * nt + i) * per - 1, 0), col))
    nxt = pl.BlockSpec((HALO, width), lambda b, i: (jnp.minimum((b * nt + i + 1) * per, t // HALO - 1), col))
    return prev, cur, nxt


SUB = 8
CONV_ROWS = 64


def _shifted_copies(src, sh, rows_first, rows_rest):
    for r in range(SUB):
        rows = rows_first if r == 0 else rows_rest
        sh[r, pl.ds(0, rows), :] = src[pl.ds(r, rows), :]


def _tap_sum(sh, w, offset, c0, rows):
    acc = None
    for k in range(CONV_K):
        o = offset(k)
        term = sh[o % SUB, pl.ds(c0 + o - o % SUB, rows), :] * w[k:k + 1, :]
        acc = term if acc is None else acc + term
    return acc


def _glu_rows(a_ref, g_ref, rows=slice(None)):
    return a_ref[rows, :].astype(F32) * jax.nn.sigmoid(g_ref[rows, :].astype(F32))


def _conv_fwd(z0, conv_w, conv_b, ln_g, ln_b, bsz, name):
    t = z0.shape[0]
    s = t // bsz
    ts = _seq_tile(s)
    nt = s // ts
    first = HALO - (CONV_K - 1)

    def body(ap_ref, ac_ref, gp_ref, gc_ref, w_ref, cb_ref, lg_ref, lb_ref, o_ref, y_ref, hbuf, sh):
        hbuf[0:HALO, :] = jnp.where(pl.program_id(1) > 0, _glu_rows(ap_ref, gp_ref), 0.0)
        hbuf[HALO:HALO + ts, :] = _glu_rows(ac_ref, gc_ref)
        _shifted_copies(hbuf, sh, ts + HALO, ts + HALO - SUB)
        w, cb, lg, lb = w_ref[...], cb_ref[...], lg_ref[...], lb_ref[...]
        for c0 in range(0, ts, CONV_ROWS):
            y = _tap_sum(sh, w, lambda k: first + k, c0, CONV_ROWS) + cb
            y_ref[c0:c0 + CONV_ROWS, :] = y
            o = _ln(y, lg, lb)[0]
            o_ref[c0:c0 + CONV_ROWS, :] = (o * jax.nn.sigmoid(o)).astype(BF16)

    ap, ac, _ = _seq_specs(ts, nt, t, 512, 1)
    gp, gc, _ = _seq_specs(ts, nt, t, 512, 2)
    tile = pl.BlockSpec((ts, 512), lambda b, i: (b * nt + i, 0))
    return pl.pallas_call(
        body, name=name, grid=(bsz, nt),
        in_specs=[ap, ac, gp, gc, _full((HALO, 512)), _row(512), _row(512), _row(512)],
        out_specs=[tile, tile],
        out_shape=[jax.ShapeDtypeStruct((t, 512), BF16), jax.ShapeDtypeStruct((t, 512), F32)],
        scratch_shapes=[pltpu.VMEM((HALO + ts, 512), F32), pltpu.VMEM((SUB, HALO + ts, 512), F32)],
        compiler_params=_params("parallel", "parallel"))(z0, z0, z0, z0, conv_w, conv_b, ln_g, ln_b)


def _conv_bwd(z0, y, dmix, conv_w, ln_g, ln_b, bsz, name):
    t = z0.shape[0]
    s = t // bsz
    ts = _seq_tile(s)
    nt = s // ts

    def body(ac_ref, gc_ref, yc_ref, yn_ref, dc_ref, dn_ref, w_ref, lg_ref, lb_ref,
             da_ref, dg_ref, dw_ref, dcb_ref, dlg_ref, dlb_ref, dba_ref, dbg_ref, hcur, dybuf, sh_dy):
        b, i = pl.program_id(0), pl.program_id(1)

        @pl.when((b == 0) & (i == 0))
        def _():
            for ref in (dw_ref, dcb_ref, dlg_ref, dlb_ref, dba_ref, dbg_ref):
                ref[...] = jnp.zeros_like(ref)

        w, lg, lb = w_ref[...], lg_ref[...], lb_ref[...]
        hcur[...] = _glu_rows(ac_ref, gc_ref)

        def d_conv_out(yv, dout):
            o, xhat, rstd = _ln(yv, lg, lb)
            sg_o = jax.nn.sigmoid(o)
            d_o = dout * sg_o * (1.0 + o * (1.0 - sg_o))
            return _ln_bwd(d_o, xhat, rstd, lg), d_o * xhat, d_o

        dlg = jnp.zeros((1, 512), F32)
        dlb = jnp.zeros((1, 512), F32)
        dcb = jnp.zeros((1, 512), F32)
        for c0 in range(0, ts, CONV_ROWS):
            rows = slice(c0, c0 + CONV_ROWS)
            dy, g_part, b_part = d_conv_out(yc_ref[rows, :], dc_ref[rows, :].astype(F32))
            dybuf[rows, :] = dy
            dlg = dlg + jnp.sum(g_part, axis=0, keepdims=True)
            dlb = dlb + jnp.sum(b_part, axis=0, keepdims=True)
            dcb = dcb + jnp.sum(dy, axis=0, keepdims=True)
        dn = jnp.where(i < nt - 1, dn_ref[...].astype(F32), 0.0)
        dybuf[ts:ts + HALO, :] = d_conv_out(yn_ref[...], dn)[0]
        dlg_ref[...] += dlg
        dlb_ref[...] += dlb
        dcb_ref[...] += dcb
        _shifted_copies(dybuf, sh_dy, ts + HALO - SUB, ts + HALO - SUB)

        for k in range(CONV_K):
            o = CONV_K - 1 - k
            prod = hcur[...] * sh_dy[o % SUB, pl.ds(o - o % SUB, ts), :]
            dw_ref[pl.ds(k, 1), :] += jnp.sum(prod, axis=0, keepdims=True)
        dba = jnp.zeros((1, 512), F32)
        dbg = jnp.zeros((1, 512), F32)
        for c0 in range(0, ts, CONV_ROWS):
            rows = slice(c0, c0 + CONV_ROWS)
            dh = _tap_sum(sh_dy, w, lambda k: CONV_K - 1 - k, c0, CONV_ROWS)
            a_c = ac_ref[rows, :].astype(F32)
            sg_c = jax.nn.sigmoid(gc_ref[rows, :].astype(F32))
            d_a = dh * sg_c
            d_g = dh * a_c * sg_c * (1.0 - sg_c)
            da_ref[rows, :] = d_a.astype(BF16)
            dg_ref[rows, :] = d_g.astype(BF16)
            dba = dba + jnp.sum(d_a, axis=0, keepdims=True)
            dbg = dbg + jnp.sum(d_g, axis=0, keepdims=True)
        dba_ref[...] += dba
        dbg_ref[...] += dbg

    _, ac, _ = _seq_specs(ts, nt, t, 512, 1)
    _, gc, _ = _seq_specs(ts, nt, t, 512, 2)
    _, yc, yn = _seq_specs(ts, nt, t, 512, 0)
    _, dc, dn = _seq_specs(ts, nt, t, 512, 1)
    tile = pl.BlockSpec((ts, 512), lambda b, i: (b * nt + i, 0))
    vec = jax.ShapeDtypeStruct((1, 512), F32)
    return pl.pallas_call(
        body, name=name, grid=(bsz, nt),
        in_specs=[ac, gc, yc, yn, dc, dn, _full((HALO, 512)), _row(512), _row(512)],
        out_specs=[tile, tile, _full((HALO, 512)), _row(512), _row(512), _row(512), _row(512), _row(512)],
        out_shape=[jax.ShapeDtypeStruct((t, 512), BF16), jax.ShapeDtypeStruct((t, 512), BF16),
                   jax.ShapeDtypeStruct((HALO, 512), F32), vec, vec, vec, vec, vec],
        scratch_shapes=[pltpu.VMEM((ts, 512), F32), pltpu.VMEM((ts + HALO, 512), F32),
                        pltpu.VMEM((SUB, HALO + ts, 512), F32)],
        compiler_params=_params("arbitrary", "arbitrary"))(z0, z0, y, y, dmix, dmix, conv_w, ln_g, ln_b)


def _pooled(pbuf, g, ts, tok):
    w = 2 << g
    cols = slice(128 * g, 128 * (g + 1))
    sm = pbuf[pl.ds(HALO, ts), cols]
    for d in range(1, w):
        sm = sm + pbuf[pl.ds(HALO - d, ts), cols]
    cnt = jnp.minimum(tok + 1, w).astype(F32)
    return sm / cnt - pbuf[pl.ds(HALO, ts), cols]


def _pool_fwd(z1, w_pool, scale, bsz, name):
    t = z1.shape[0]
    s = t // bsz
    ts = _seq_tile(s)
    nt = s // ts

    def body(zp_ref, zc_ref, wp_ref, sc_ref, o_ref, token, pbuf):
        token[...] = jnp.zeros_like(token)
        i = pl.program_id(1)
        pbuf[0:HALO, :] = jnp.where(i > 0, zp_ref[...].astype(F32), 0.0)
        pbuf[HALO:HALO + ts, :] = zc_ref[...].astype(F32)
        tok = i * ts + lax.broadcasted_iota(jnp.int32, (ts, 1), 0)
        for g in range(4):
            cols = slice(128 * g, 128 * (g + 1))
            pooled = _pooled(pbuf, g, ts, tok).astype(BF16)
            o_ref[:, cols] = (_nn(pooled, wp_ref[g].astype(BF16)) * sc_ref[:, cols]).astype(BF16)

    zp, zc, _ = _seq_specs(ts, nt, t, 512, 0)
    return pl.pallas_call(
        body, name=name, grid=(bsz, nt), in_specs=[zp, zc, _full((4, 128, 128)), _row(512)],
        out_specs=[pl.BlockSpec((ts, 512), lambda b, i: (b * nt + i, 0)), _full((8, 128))],
        out_shape=[jax.ShapeDtypeStruct((t, 512), BF16), jax.ShapeDtypeStruct((8, 128), F32)],
        scratch_shapes=[pltpu.VMEM((HALO + ts, 512), F32)],
        compiler_params=_params("arbitrary", "arbitrary"))(z1, z1, w_pool, scale)


def _pool_bwd(z1, dmix, w_pool, scale, bsz, name):
    t = z1.shape[0]
    s = t // bsz
    ts = _seq_tile(s)
    nt = s // ts
    rr = ts + HALO

    def body(zp_ref, zc_ref, dc_ref, dn_ref, wp_ref, sc_ref, dz_ref, dwp_ref, dsc_ref, pbuf, ebuf):
        b, i = pl.program_id(0), pl.program_id(1)

        @pl.when((b == 0) & (i == 0))
        def _():
            dwp_ref[...] = jnp.zeros_like(dwp_ref)
            dsc_ref[...] = jnp.zeros_like(dsc_ref)

        pbuf[0:HALO, :] = jnp.where(i > 0, zp_ref[...].astype(F32), 0.0)
        pbuf[HALO:HALO + ts, :] = zc_ref[...].astype(F32)
        dn = jnp.where(i < nt - 1, dn_ref[...].astype(F32), 0.0)
        dout = jnp.concatenate([dc_ref[...].astype(F32), dn], axis=0)
        tok = i * ts + lax.broadcasted_iota(jnp.int32, (ts, 1), 0)
        tok_r = i * ts + lax.broadcasted_iota(jnp.int32, (rr, 1), 0)
        for g in range(4):
            w = 2 << g
            cols = slice(128 * g, 128 * (g + 1))
            wg = wp_ref[g].astype(BF16)
            pooled = _pooled(pbuf, g, ts, tok).astype(BF16)
            dsc_ref[:, cols] += jnp.sum(dout[:ts, cols] * _nn(pooled, wg), axis=0, keepdims=True)
            dy = (dout[:, cols] * sc_ref[:, cols]).astype(BF16)
            dwp_ref[g] += _tn(pooled, dy[:ts])
            dpl = _nt(dy, wg)
            ebuf[...] = dpl / jnp.minimum(tok_r + 1, w).astype(F32)
            dz = ebuf[pl.ds(0, ts), :] - dpl[:ts]
            for d in range(1, w):
                dz = dz + ebuf[pl.ds(d, ts), :]
            dz_ref[:, cols] = dz.astype(BF16)

    zp, zc, _ = _seq_specs(ts, nt, t, 512, 0)
    _, dc, dn = _seq_specs(ts, nt, t, 512, 0)
    return pl.pallas_call(
        body, name=name, grid=(bsz, nt), in_specs=[zp, zc, dc, dn, _full((4, 128, 128)), _row(512)],
        out_specs=[pl.BlockSpec((ts, 512), lambda b, i: (b * nt + i, 0)), _full((4, 128, 128)), _row(512)],
        out_shape=[jax.ShapeDtypeStruct((t, 512), BF16), jax.ShapeDtypeStruct((4, 128, 128), F32),
                   jax.ShapeDtypeStruct((1, 512), F32)],
        scratch_shapes=[pltpu.VMEM((HALO + ts, 512), F32), pltpu.VMEM((rr, 128), F32)],
        compiler_params=_params("arbitrary", "arbitrary"))(z1, z1, dmix, dmix, w_pool, scale)


def _tril():
    r = lax.broadcasted_iota(jnp.int32, (SGU_CHUNK, SGU_CHUNK), 0)
    c = lax.broadcasted_iota(jnp.int32, (SGU_CHUNK, SGU_CHUNK), 1)
    return r >= c


def _sgu_fwd(z1, ln_g, ln_b, w_s, b_rows, name):
    t = z1.shape[0]
    ts = _tile(t)

    def body(zu_ref, zv_ref, lg_ref, lb_ref, ws_ref, bs_ref, o_ref):
        v = _gelu(zv_ref[...].astype(F32))[0]
        vb = _ln(v, lg_ref[...], lb_ref[...])[0].astype(BF16)
        tril = _tril()
        for g in range(4):
            cols = slice(128 * g, 128 * (g + 1))
            wg = jnp.where(tril, ws_ref[g], 0.0).astype(BF16)
            for c in range(ts // SGU_CHUNK):
                rows = slice(SGU_CHUNK * c, SGU_CHUNK * (c + 1))
                mixed = _nn(wg, vb[rows, cols]) + bs_ref[g]
                o_ref[rows, cols] = (_gelu(zu_ref[rows, cols].astype(F32))[0] * mixed).astype(BF16)

    return pl.pallas_call(
        body, name=name, grid=(t // ts,),
        in_specs=[pl.BlockSpec((ts, 512), lambda i: (i, 1)), pl.BlockSpec((ts, 512), lambda i: (i, 2)),
                  _row(512), _row(512), _full((4, 128, 128)), _full((4, 128, 128))],
        out_specs=pl.BlockSpec((ts, 512), lambda i: (i, 0)), out_shape=jax.ShapeDtypeStruct((t, 512), BF16),
        compiler_params=_params("parallel"))(z1, z1, ln_g, ln_b, w_s, b_rows)


def _sgu_bwd(z1, dmix, ln_g, ln_b, w_s, b_rows, name):
    t = z1.shape[0]
    ts = _tile(t)

    def body(zu_ref, zv_ref, d_ref, lg_ref, lb_ref, ws_ref, bs_ref,
             dzu_ref, dzv_ref, dws_ref, dbs_ref, dlg_ref, dlb_ref, dvbuf):
        @pl.when(pl.program_id(0) == 0)
        def _():
            for ref in (dws_ref, dbs_ref, dlg_ref, dlb_ref):
                ref[...] = jnp.zeros_like(ref)

        zv = zv_ref[...].astype(F32)
        v, thv = _gelu(zv)
        lg = lg_ref[...]
        vln, xhat, rstd = _ln(v, lg, lb_ref[...])
        vb = vln.astype(BF16)
        tril = _tril()
        for g in range(4):
            cols = slice(128 * g, 128 * (g + 1))
            wg = jnp.where(tril, ws_ref[g], 0.0).astype(BF16)
            dws = jnp.zeros((SGU_CHUNK, SGU_CHUNK), F32)
            dbs = jnp.zeros((1, SGU_CHUNK), F32)
            for c in range(ts // SGU_CHUNK):
                rows = slice(SGU_CHUNK * c, SGU_CHUNK * (c + 1))
                vbc = vb[rows, cols]
                mixed = _nn(wg, vbc) + bs_ref[g]
                zu = zu_ref[rows, cols].astype(F32)
                u, thu = _gelu(zu)
                dout = d_ref[rows, cols].astype(F32)
                dzu_ref[rows, cols] = (dout * mixed * _gelu_grad(zu, thu)).astype(BF16)
                dm = dout * u
                dmb = dm.astype(BF16)
                dws = dws + _nt(dmb, vbc)
                dbs = dbs + jnp.sum(dm.T, axis=0, keepdims=True)
                dvbuf[rows, cols] = _tn(wg, dmb)
            dws_ref[g] += jnp.where(tril, dws, 0.0)
            dbs_ref[pl.ds(g, 1), :] += dbs
        dvln = dvbuf[...]
        dlg_ref[...] += jnp.sum(dvln * xhat, axis=0, keepdims=True)
        dlb_ref[...] += jnp.sum(dvln, axis=0, keepdims=True)
        dzv_ref[...] = (_ln_bwd(dvln, xhat, rstd, lg) * _gelu_grad(zv, thv)).astype(BF16)

    tile = pl.BlockSpec((ts, 512), lambda i: (i, 0))
    vec = jax.ShapeDtypeStruct((1, 512), F32)
    return pl.pallas_call(
        body, name=name, grid=(t // ts,),
        in_specs=[pl.BlockSpec((ts, 512), lambda i: (i, 1)), pl.BlockSpec((ts, 512), lambda i: (i, 2)),
                  pl.BlockSpec((ts, 512), lambda i: (i, 1)), _row(512), _row(512), _full((4, 128, 128)),
                  _full((4, 128, 128))],
        out_specs=[tile, tile, _full((4, 128, 128)), _full((4, 128)), _row(512), _row(512)],
        out_shape=[jax.ShapeDtypeStruct((t, 512), BF16), jax.ShapeDtypeStruct((t, 512), BF16),
                   jax.ShapeDtypeStruct((4, 128, 128), F32), jax.ShapeDtypeStruct((4, 128), F32), vec, vec],
        scratch_shapes=[pltpu.VMEM((ts, 512), F32)],
        compiler_params=_params("arbitrary"))(z1, z1, dmix, ln_g, ln_b, w_s, b_rows)


def _row_tile(r):
    for cand in (512, 352, 256, 192, 128, 64, 32, 16, 8):
        if r % cand == 0:
            return cand
    return r


def _sum_slabs(a, name):
    k, r, c = a.shape
    tr = _row_tile(r)

    def body(*refs):
        acc = refs[0][...].astype(F32)
        for ref in refs[1:-1]:
            acc = acc + ref[...].astype(F32)
        refs[-1][...] = acc

    in_specs = [pl.BlockSpec((None, tr, c), functools.partial(lambda i, s: (s, i, 0), s=s)) for s in range(k)]
    return pl.pallas_call(
        body, name=name, grid=(r // tr,), in_specs=in_specs, out_specs=pl.BlockSpec((tr, c), lambda i: (i, 0)),
        out_shape=jax.ShapeDtypeStruct((r, c), F32), compiler_params=_params("parallel"))(*([a] * k))


def _adamw_math(w, g, m, v):
    mn = ADAM_B1 * m + (1.0 - ADAM_B1) * g
    vn = ADAM_B2 * v + (1.0 - ADAM_B2) * (g * g)
    m_hat = mn / (1.0 - ADAM_B1 ** ADAM_STEP)
    v_hat = vn / (1.0 - ADAM_B2 ** ADAM_STEP)
    return -ADAM_LR * (m_hat / (jnp.sqrt(v_hat) + ADAM_EPS) + ADAM_WD * w), mn, vn


def _reduce_adamw(landing, w, m, v, name, layer=None, into=None):
    k, r, c = landing.shape
    tr = _row_tile(r)
    n_into = 0 if into is None else 4

    def body(*refs):
        slabs, (w_ref, m_ref, v_ref) = refs[:k], refs[k:k + 3]
        g_ref, d_ref, mo_ref, vo_ref = refs[k + 3 + n_into:]
        g = slabs[0][...].astype(F32)
        for ref in slabs[1:]:
            g = g + ref[...].astype(F32)
        g_ref[...] = g
        d_ref[...], mo_ref[...], vo_ref[...] = _adamw_math(w_ref[...], g, m_ref[...], v_ref[...])

    if layer is None:
        spec = pl.BlockSpec((tr, c), lambda i: (i, 0))
    else:
        spec = pl.BlockSpec((None, tr, c), lambda i: (layer, i, 0))
    in_specs = [pl.BlockSpec((None, tr, c), functools.partial(lambda i, s: (s, i, 0), s=s)) for s in range(k)]
    in_specs += [spec] * 3 + [ANY] * n_into
    shape = jax.ShapeDtypeStruct(w.shape, F32)
    return pl.pallas_call(
        body, name=name, grid=(r // tr,), in_specs=in_specs, out_specs=[spec] * 4, out_shape=[shape] * 4,
        input_output_aliases={k + 3 + j: j for j in range(n_into)},
        compiler_params=_params("parallel"))(*([landing] * k), w, m, v, *(into or ()))


def _adamw(w, g, m, v, name):
    r, c = w.shape
    tr = _row_tile(r)

    def body(w_ref, g_ref, m_ref, v_ref, d_ref, mo_ref, vo_ref):
        d_ref[...], mo_ref[...], vo_ref[...] = _adamw_math(w_ref[...], g_ref[...], m_ref[...], v_ref[...])

    spec = pl.BlockSpec((tr, c), lambda i: (i, 0))
    shape = jax.ShapeDtypeStruct((r, c), F32)
    return pl.pallas_call(
        body, name=name, grid=(r // tr,), in_specs=[spec] * 4, out_specs=[spec] * 3, out_shape=[shape] * 3,
        compiler_params=_params("parallel"))(w, g, m, v)


ANY = pl.BlockSpec(memory_space=pl.ANY)


def _all_gather(block, name):
    r, c_dim = block.shape

    def body(x_ref, out_ref, token, send_sems, recv_sems, local_sem):
        token[...] = jnp.zeros_like(token)
        x, y, c = lax.axis_index("x"), lax.axis_index("y"), lax.axis_index("c")
        me, sibling = (x, y, c), (x, y, 1 - c)
        chips = [(1 - x, y), (x, 1 - y), (1 - x, 1 - y)]

        def rows(px, py, pc):
            return out_ref.at[4 * px + 2 * py + pc]

        def copy(k, blk, to, src=None):
            return pltpu.make_async_remote_copy(
                src_ref=rows(*blk) if src is None else src, dst_ref=rows(*blk), send_sem=send_sems.at[k],
                recv_sem=recv_sems.at[k], device_id=to, device_id_type=MESH)

        mine = pltpu.make_async_copy(x_ref, rows(*me), local_sem)
        mine.start()
        first = [copy(0, me, sibling, src=x_ref)]
        first += [copy(1 + j, me, (*chip, c), src=x_ref) for j, chip in enumerate(chips)]
        for cp in first:
            cp.start()
        passed = [copy(4 + j, (*chip, c), sibling) for j, chip in enumerate(chips)]
        for j, chip in enumerate(chips):
            copy(1 + j, (*chip, c), me).wait_recv()
            passed[j].start()
        copy(0, sibling, me).wait_recv()
        for j, chip in enumerate(chips):
            copy(4 + j, (*chip, 1 - c), me).wait_recv()
        for cp in first + passed:
            cp.wait_send()
        mine.wait()

    return pl.pallas_call(
        body, name=name, in_specs=[ANY], out_specs=[ANY, pl.BlockSpec(memory_space=pltpu.VMEM)],
        out_shape=[jax.ShapeDtypeStruct((N_DEV, r, c_dim), block.dtype), jax.ShapeDtypeStruct((8, 128), F32)],
        scratch_shapes=[pltpu.SemaphoreType.DMA((7,)), pltpu.SemaphoreType.DMA((7,)), pltpu.SemaphoreType.DMA],
    )(block)


HBM = pl.BlockSpec(memory_space=pltpu.HBM)
SEM = pl.BlockSpec(memory_space=pltpu.SEMAPHORE)
EFFECT = pltpu.SideEffectType.DATAFLOW_SIDE_EFFECTING


def _exchange_copies(scatter, src_refs, land_refs, send_sems, recv_sems, local_sems):
    x, y, c = lax.axis_index("x"), lax.axis_index("y"), lax.axis_index("c")
    me = 4 * x + 2 * y + c
    sends, arrivals, locals_ = [], [], []
    for a, (src, land) in enumerate(zip(src_refs, land_refs)):
        def pick(idx, src=src):
            return src.at[idx] if scatter else src

        locals_.append(pltpu.make_async_copy(pick(me), land.at[me], local_sems.at[a]))
        for r in range(1, N_DEV):
            px = 1 - x if r & 4 else x
            py = 1 - y if r & 2 else y
            pc = 1 - c if r & 1 else c
            peer, s = 4 * px + 2 * py + pc, 7 * a + r - 1
            sends.append(pltpu.make_async_remote_copy(
                src_ref=pick(peer), dst_ref=land.at[me], send_sem=send_sems.at[s], recv_sem=recv_sems.at[s],
                device_id=(px, py, pc), device_id_type=MESH))
            arrivals.append(pltpu.make_async_remote_copy(
                src_ref=pick(peer), dst_ref=land.at[peer], send_sem=send_sems.at[s], recv_sem=recv_sems.at[s],
                device_id=(px, py, pc), device_id_type=MESH))
    return sends, arrivals, locals_


def _exchange_start(srcs, scatter, name):
    n = len(srcs)
    lands = [lax.empty((N_DEV,) + s.shape[-2:], s.dtype) for s in srcs]

    def body(*refs):
        src_refs, land_refs = refs[:n], refs[n:2 * n]
        send_sems, recv_sems, local_sems = refs[2 * n:2 * n + 3]
        token = refs[-1]
        sends, _, locals_ = _exchange_copies(scatter, src_refs, land_refs, send_sems, recv_sems, local_sems)
        for cp in locals_ + sends:
            cp.start()
        token[...] = jnp.zeros_like(token)

    res = pl.pallas_call(
        body, name=name,
        out_shape=[pltpu.SemaphoreType.DMA((7 * n,)), pltpu.SemaphoreType.DMA((7 * n,)), pltpu.SemaphoreType.DMA((n,))]
        + [pltpu.HBM(a.shape, a.dtype) for a in list(srcs) + lands] + [jax.ShapeDtypeStruct((8, 128), F32)],
        in_specs=[HBM] * (2 * n), out_specs=[SEM] * 3 + [HBM] * (2 * n) + [pl.BlockSpec(memory_space=pltpu.VMEM)],
        input_output_aliases={i: 3 + i for i in range(2 * n)},
        compiler_params=pltpu.CompilerParams(has_side_effects=EFFECT),
    )(*[pltpu.with_memory_space_constraint(a, pltpu.HBM) for a in list(srcs) + lands])
    return (n, scatter, res[:3], res[3:3 + 2 * n]), res[-1]


def _exchange_wait(handle, after, name):
    n, scatter, sems, thru = handle

    def body(*refs):
        src_refs, land_refs = refs[:n], refs[n:2 * n]
        send_sems, recv_sems, local_sems = refs[2 * n:2 * n + 3]
        sends, arrivals, locals_ = _exchange_copies(scatter, src_refs, land_refs, send_sems, recv_sems, local_sems)
        for cp in arrivals:
            cp.wait_recv()
        for cp in sends:
            cp.wait_send()
        for cp in locals_:
            cp.wait()

    res = pl.pallas_call(
        body, name=name, out_shape=[pltpu.HBM(a.shape, a.dtype) for a in thru],
        in_specs=[HBM] * (2 * n) + [SEM] * 3 + [ANY], out_specs=[HBM] * (2 * n),
        input_output_aliases={i: i for i in range(2 * n)},
        compiler_params=pltpu.CompilerParams(has_side_effects=EFFECT),
    )(*thru, *sems, after)
    return res[n:]


def _behind(tokens, a):
    zero = sum(tok[0, 0] for tok in tokens)
    return jax.tree.map(lambda v: v + zero.astype(v.dtype), a)


def _perm_heads(a, perm, axis):
    idx = [slice(None)] * a.ndim
    parts = []
    for h in perm:
        idx[axis] = slice(64 * h, 64 * (h + 1))
        parts.append(a[tuple(idx)])
    idx[axis] = slice(512, None)
    if a.shape[axis] > 512:
        parts.append(a[tuple(idx)])
    return jnp.concatenate(parts, axis=axis)


Q_INV = tuple(int(i) for i in np.argsort(Q_PERM))


def _in0_to_kernel(a, axis):
    a = _perm_heads(a, Q_PERM, axis)
    idx = [slice(None)] * a.ndim

    def cut(lo, hi):
        idx[axis] = slice(lo, hi)
        return a[tuple(idx)]

    return jnp.concatenate([cut(0, 512), cut(768, 1792), cut(512, 768)], axis=axis)


def _in0_from_kernel(a, axis):
    idx = [slice(None)] * a.ndim

    def cut(lo, hi):
        idx[axis] = slice(lo, hi)
        return a[tuple(idx)]

    a = jnp.concatenate([cut(0, 512), cut(1536, 1792), cut(512, 1536)], axis=axis)
    return _perm_heads(a, Q_INV, axis)


def _f32_as_u16_rows(v, rows):
    bits = lax.bitcast_convert_type(v, jnp.uint16).reshape(-1)
    return jnp.pad(bits, (0, rows * D - bits.shape[0])).reshape(rows, D)


def _pad_rows(v, rows):
    v = v.reshape(-1)
    return jnp.pad(v, (0, rows * D - v.shape[0])).reshape(rows, D)


def kernel(x, mix_norm, a_w_in, a_b_in, a_sinks, a_conv_w, a_conv_b, a_cln_g, a_cln_b, a_w_out, c_w_in, c_w_pool, c_pool_scale, c_sln_g, c_sln_b, c_w_s, c_b_s, c_w_out, ffn_norm, ffn_w_gate, ffn_w_up, ffn_w_down, final_norm, loss_target, m_mix_norm, m_a_w_in, m_a_b_in, m_a_sinks, m_a_conv_w, m_a_conv_b, m_a_cln_g, m_a_cln_b, m_a_w_out, m_c_w_in, m_c_w_pool, m_c_pool_scale, m_c_sln_g, m_c_sln_b, m_c_w_s, m_c_b_s, m_c_w_out, m_ffn_norm, m_ffn_w_gate, m_ffn_w_up, m_ffn_w_down, m_final_norm, v_mix_norm, v_a_w_in, v_a_b_in, v_a_sinks, v_a_conv_w, v_a_conv_b, v_a_cln_g, v_a_cln_b, v_a_w_out, v_c_w_in, v_c_w_pool, v_c_pool_scale, v_c_sln_g, v_c_sln_b, v_c_w_s, v_c_b_s, v_c_w_out, v_ffn_norm, v_ffn_w_gate, v_ffn_w_up, v_ffn_w_down, v_final_norm):
    bsz, seq, _ = x.shape
    t = bsz * seq
    w_in = dict(mix_norm=mix_norm, a_w_in=a_w_in, a_b_in=a_b_in, a_sinks=a_sinks, a_conv_w=a_conv_w, a_conv_b=a_conv_b,
                a_cln_g=a_cln_g, a_cln_b=a_cln_b, a_w_out=a_w_out, c_w_in=c_w_in, c_w_pool=c_w_pool,
                c_pool_scale=c_pool_scale, c_sln_g=c_sln_g, c_sln_b=c_sln_b, c_w_s=c_w_s, c_b_s=c_b_s, c_w_out=c_w_out,
                ffn_norm=ffn_norm, ffn_w_gate=ffn_w_gate, ffn_w_up=ffn_w_up, ffn_w_down=ffn_w_down, final_norm=final_norm)
    m_in = dict(mix_norm=m_mix_norm, a_w_in=m_a_w_in, a_b_in=m_a_b_in, a_sinks=m_a_sinks, a_conv_w=m_a_conv_w,
                a_conv_b=m_a_conv_b, a_cln_g=m_a_cln_g, a_cln_b=m_a_cln_b, a_w_out=m_a_w_out, c_w_in=m_c_w_in,
                c_w_pool=m_c_w_pool, c_pool_scale=m_c_pool_scale, c_sln_g=m_c_sln_g, c_sln_b=m_c_sln_b, c_w_s=m_c_w_s,
                c_b_s=m_c_b_s, c_w_out=m_c_w_out, ffn_norm=m_ffn_norm, ffn_w_gate=m_ffn_w_gate, ffn_w_up=m_ffn_w_up,
                ffn_w_down=m_ffn_w_down, final_norm=m_final_norm)
    v_in = dict(mix_norm=v_mix_norm, a_w_in=v_a_w_in, a_b_in=v_a_b_in, a_sinks=v_a_sinks, a_conv_w=v_a_conv_w,
                a_conv_b=v_a_conv_b, a_cln_g=v_a_cln_g, a_cln_b=v_a_cln_b, a_w_out=v_a_w_out, c_w_in=v_c_w_in,
                c_w_pool=v_c_w_pool, c_pool_scale=v_c_pool_scale, c_sln_g=v_c_sln_g, c_sln_b=v_c_sln_b, c_w_s=v_c_w_s,
                c_b_s=v_c_b_s, c_w_out=v_c_w_out, ffn_norm=v_ffn_norm, ffn_w_gate=v_ffn_w_gate, ffn_w_up=v_ffn_w_up,
                ffn_w_down=v_ffn_w_down, final_norm=v_final_norm)

    small = jnp.concatenate([a_conv_w[0].reshape(-1), c_pool_scale[0], c_sln_g[0], c_sln_b[0]])
    first_bits = lax.bitcast_convert_type(a_w_in[0].T.astype(BF16), jnp.uint16)
    gathered, tok = _all_gather(jnp.concatenate([first_bits, _f32_as_u16_rows(small, W_MISC_ROWS)], axis=0), "gather_mixer0")

    def ffn_shards(l):
        return [ffn_w_gate[l].T.astype(BF16), ffn_w_up[l].T.astype(BF16), ffn_w_down[l].astype(BF16)]

    ffn0_h, tok = _exchange_start(_behind([tok], ffn_shards(0) + [a_w_out[0].astype(BF16)]), False, "gather_ffn0_start")
    mix1_h, tok = _exchange_start(_behind([tok], [c_w_in[0].T.astype(BF16), c_w_out[0].astype(BF16)]), False,
                                  "gather_mixer1_start")
    ffn1_h, tok = _exchange_start(_behind([tok], ffn_shards(1)), False, "gather_ffn1_start")

    a_in_full = lax.bitcast_convert_type(gathered[:, :224].reshape(IN0, D), BF16)
    small_all = lax.bitcast_convert_type(
        gathered[:, 224:].reshape(N_DEV, -1)[:, :2 * SMALL_SHARD].reshape(N_DEV, SMALL_SHARD, 2), F32)
    conv_w = small_all[:, :31 * 64].reshape(N_DEV, 31, 64).transpose(1, 0, 2).reshape(31, 512)
    conv_w = jnp.pad(conv_w, ((0, HALO - CONV_K), (0, 0)))
    pool_scale = small_all[:, 31 * 64:31 * 64 + 64].reshape(1, 512)
    sln_g = small_all[:, 31 * 64 + 64:31 * 64 + 128].reshape(1, 512)
    sln_b = small_all[:, 31 * 64 + 128:].reshape(1, 512)

    wt_in0 = _in0_to_kernel(a_in_full, 0)
    b_in0 = _in0_to_kernel(a_b_in, 1)
    b_rows = jnp.broadcast_to(c_b_s[0][:, :, None], (4, 128, 128))
    conv_b, cln_g, cln_b = a_conv_b, a_cln_g, a_cln_b

    h0 = x.reshape(t, D)
    target = loss_target.reshape(t, D)
    z0, hn0 = _norm_proj(h0, _behind([tok], mix_norm[0:1]), wt_in0, b_in0, "in_proj0")
    attn, tok = _attn_fwd(z0, a_sinks, bsz, "attn_fwd")
    conv, conv_y = _conv_fwd(z0, conv_w, conv_b, cln_g, _behind([tok], cln_b), bsz, "conv_fwd")
    wtg0, wtu0, wd0, a_out_full = (w.reshape(-1, D) for w in _exchange_wait(ffn0_h, conv, "gather_ffn0_wait"))
    w_out0 = _perm_heads(a_out_full, Q_PERM, 0)
    h1, h2, hnf0, gate0, up0 = _ffn_fwd(h0, attn, conv, w_out0, ffn_norm[0:1], wtg0, wtu0, wd0, "ffn_fwd0")
    wt_in1, w_out1 = (w.reshape(-1, D) for w in _exchange_wait(mix1_h, h2, "gather_mixer1_wait"))
    z1, hn1 = _norm_proj(h2, mix_norm[1:2], wt_in1, None, "in_proj1")
    pool, tok = _pool_fwd(z1, c_w_pool[0], pool_scale, bsz, "pool_fwd")
    sgu = _sgu_fwd(z1, sln_g, _behind([tok], sln_b), c_w_s[0], b_rows, "sgu_fwd")
    wtg1, wtu1, wd1 = (w.reshape(D_FF, D) for w in _exchange_wait(ffn1_h, sgu, "gather_ffn1_wait"))
    h3, dh4, hnf1, gate1, up1, d_final_norm, loss_part = _ffn_fwd(
        h2, pool, sgu, w_out1, ffn_norm[1:2], wtg1, wtu1, wd1, "ffn_fwd1", head=(final_norm.reshape(1, D), target))

    def blocks(g):
        return g.reshape(N_DEV, g.shape[0] // N_DEV, D)

    dh3, dmix1, dgate1, dup1, act1, d_fn1 = _ffn_bwd(dh4, h3, ffn_norm[1:2], gate1, up1, wtg1, wtu1, wd1, w_out1, "ffn_bwd1")
    gw_ffn1 = [_mm_tn(dgate1, hnf1, "dw_gate1"), _mm_tn(dup1, hnf1, "dw_up1"), _mm_tn(act1, dh4, "dw_down1")]
    ffn1_g, tok = _exchange_start([blocks(g) for g in gw_ffn1], True, "scatter_ffn1_start")
    gw_c_out = _mm_tn_pieces([pool, sgu], dh3, "dw_out1")
    dzp, d_w_pool, d_pool_scale = _pool_bwd(z1, dmix1, c_w_pool[0], _behind([tok], pool_scale), bsz, "pool_bwd")
    dzu, dzv, d_w_s, d_b_s, d_sln_g, d_sln_b = _sgu_bwd(z1, dmix1, sln_g, sln_b, c_w_s[0], b_rows, "sgu_bwd")
    dh2, d_mn1 = _proj_bwd_norm([(dzp, 0), (dzu, 512), (dzv, 1024)], wt_in1, h2, dh3, mix_norm[1:2], BF16, "in_proj1_bwd")
    gw_c_in = _mm_tn_pieces([dzp, dzu, dzv], hn1, "dw_in1")
    mix1_g, tok = _exchange_start([blocks(gw_c_in), blocks(gw_c_out)], True, "scatter_mixer1_start")
    dh1, dmix0, dgate0, dup0, act0, d_fn0 = _ffn_bwd(dh2, h1, _behind([tok], ffn_norm[0:1]), gate0, up0, wtg0, wtu0, wd0,
                                                      w_out0, "ffn_bwd0")
    gw_ffn0 = [_mm_tn(dgate0, hnf0, "dw_gate0"), _mm_tn(dup0, hnf0, "dw_up0"), _mm_tn(act0, dh2, "dw_down0")]
    ffn0_g, tok = _exchange_start([blocks(g) for g in gw_ffn0], True, "scatter_ffn0_start")
    gw_a_out = _perm_heads(_mm_tn_pieces([attn, conv], dh1, "dw_out0"), Q_INV, 0)
    dq, dkv, d_sink_row, d_bq, d_bkv = _attn_bwd(z0, dmix0, _behind([tok], a_sinks), bsz, "attn_bwd")
    dca, dcg, d_conv_w, d_conv_b, d_cln_g, d_cln_b, d_ba, d_bg = _conv_bwd(z0, conv_y, dmix0, conv_w, cln_g, cln_b, bsz, "conv_bwd")
    gw_a_in = _in0_from_kernel(_mm_tn_pieces([dq, dca, dcg, dkv], hn0, "dw_in0"), 0)
    mix0_g, tok = _exchange_start([blocks(gw_a_in), blocks(gw_a_out)], True, "scatter_mixer0_start")
    dx, d_mn0 = _proj_bwd_norm([(dq, 0), (dca, 512), (dcg, 1024), (dkv, 1536)], wt_in0, h0, dh1,
                               _behind([tok], mix_norm[0:1]), F32, "in_proj0_bwd")
    d_b_in = _in0_from_kernel(jnp.concatenate([d_bq, d_ba, d_bg, d_bkv], axis=1), 1)

    rep = dict(mix_norm=jnp.concatenate([d_mn0, d_mn1], axis=0), a_b_in=d_b_in, a_sinks=d_sink_row[:, :8],
               a_conv_b=d_conv_b, a_cln_g=d_cln_g, a_cln_b=d_cln_b, c_w_pool=d_w_pool[None], c_w_s=d_w_s[None],
               c_b_s=d_b_s[None], ffn_norm=jnp.concatenate([d_fn0, d_fn1], axis=0), final_norm=d_final_norm.reshape(D))
    rep_flat = jnp.concatenate([rep[nm].reshape(-1) for nm in REP_NAMES] + [loss_part.reshape(1)])
    rep_flat = jnp.pad(rep_flat, (0, N_DEV * REP_ROWS * D - rep_flat.shape[0])).reshape(N_DEV, REP_ROWS, D)
    small_g = jnp.concatenate([
        d_conv_w[:CONV_K].reshape(31, N_DEV, 64).transpose(1, 0, 2).reshape(N_DEV, 31 * 64),
        d_pool_scale.reshape(N_DEV, 64), d_sln_g.reshape(N_DEV, 64), d_sln_b.reshape(N_DEV, 64)], axis=1)
    small_g = jnp.pad(small_g, ((0, 0), (0, G_SMALL_ROWS * D - SMALL_SHARD))).reshape(N_DEV, G_SMALL_ROWS, D)
    tail_g, tok = _exchange_start([jnp.concatenate([small_g, rep_flat], axis=1)], True, "scatter_tail_start")

    names = list(w_in)
    g_out, delta, new_m, new_v = {}, {}, {}, {}
    column_sharded = ("a_w_in", "c_w_in", "ffn_w_gate", "ffn_w_up")

    def rows_of(a, nm):
        return jnp.swapaxes(a, 1, 2) if nm in column_sharded else a

    def reduce_adamw(nm, landing, layer, into=None):
        args = [rows_of(d[nm], nm) for d in (w_in, m_in, v_in)]
        if args[0].shape[0] == 1:
            args, layer = [a[0] for a in args], None
        return _reduce_adamw(landing, *args, "adamw_%s_%s" % (nm, layer), layer=layer, into=into)

    def keep(nm, res):
        res = [r if r.ndim == 3 else r[None] for r in res]
        g_out[nm], delta[nm], new_m[nm], new_v[nm] = (rows_of(r, nm) for r in res)

    ffn_names = ("ffn_w_gate", "ffn_w_up", "ffn_w_down")
    landed = _exchange_wait(ffn1_g, tok, "scatter_ffn1_wait")
    ffn_res = [reduce_adamw(nm, a, 1) for nm, a in zip(ffn_names, landed)]
    landed = _exchange_wait(mix1_g, ffn_res[-1][0], "scatter_mixer1_wait")
    for nm, a in zip(("c_w_in", "c_w_out"), landed):
        keep(nm, reduce_adamw(nm, a, 0))
    landed = _exchange_wait(ffn0_g, g_out["c_w_out"], "scatter_ffn0_wait")
    for nm, a, res in zip(ffn_names, landed, ffn_res):
        keep(nm, reduce_adamw(nm, a, 0, into=res))
    landed = _exchange_wait(mix0_g, g_out["ffn_w_down"], "scatter_mixer0_wait")
    for nm, a in zip(("a_w_in", "a_w_out"), landed):
        keep(nm, reduce_adamw(nm, a, 0))
    g_tail = _sum_slabs(_exchange_wait(tail_g, g_out["a_w_out"], "scatter_tail_wait")[0], "sum_tail")
    rep_all = _all_gather(g_tail[G_SMALL_ROWS:], "gather_replicated_grads")[0].reshape(-1)
    small_r = g_tail[:G_SMALL_ROWS].reshape(-1)[:SMALL_SHARD]
    g_out.update(
        a_conv_w=small_r[:31 * 64].reshape(1, 31, 64), c_pool_scale=small_r[31 * 64:31 * 64 + 64].reshape(1, 64),
        c_sln_g=small_r[31 * 64 + 64:31 * 64 + 128].reshape(1, 64), c_sln_b=small_r[31 * 64 + 128:].reshape(1, 64))
    off = 0
    for nm in REP_NAMES:
        n = int(np.prod(w_in[nm].shape))
        g_out[nm] = rep_all[off:off + n].reshape(w_in[nm].shape)
        off += n
    loss = rep_all[off]
    for group, rows, label in ((("a_conv_w", "c_pool_scale", "c_sln_g", "c_sln_b"), G_SMALL_ROWS, "adamw_small_sharded"),
                               (REP_NAMES, N_DEV * REP_ROWS, "adamw_replicated")):
        flat = [_pad_rows(jnp.concatenate([d[nm].reshape(-1) for nm in group]), rows) for d in (w_in, g_out, m_in, v_in)]
        res = [r.reshape(-1) for r in _adamw(*flat, label)]
        off = 0
        for nm in group:
            n = int(np.prod(w_in[nm].shape))
            delta[nm], new_m[nm], new_v[nm] = (r[off:off + n].reshape(w_in[nm].shape) for r in res)
            off += n

    grad_x = dx.reshape(bsz, seq, D)
    return (loss, grad_x, *[g_out[nm] for nm in names], *[delta[nm] for nm in names],
            *[new_m[nm] for nm in names], *[new_v[nm] for nm in names])
```

```python
import functools

import jax
import jax.numpy as jnp
import numpy as np
from jax import lax
from jax.experimental import pallas as pl
from jax.experimental.pallas import tpu as pltpu

F32 = jnp.float32
BF16 = jnp.bfloat16
MESH = pl.DeviceIdType.MESH

D = 1024
N_DEV = 8
EPS = 1e-5
HEAD_PAIRS = 4
ATT_BLK = 128
CONV_K = 31
HALO = 32
D_FF = 2816
FF_TILE_FWD = D_FF // 2
FF_TILE_BWD = D_FF // 2
IN0 = 1792
IN1 = 1536
POOL_WINDOWS = (2, 4, 8, 16)
SGU_CHUNK = 128
GELU_C = 0.7978845608028654
GELU_A = 0.044715
ADAM_LR, ADAM_B1, ADAM_B2, ADAM_EPS, ADAM_WD, ADAM_STEP = 0.001, 0.9, 0.999, 1e-08, 0.01, 10
VMEM_LIMIT = 56 << 20

SMALL_SHARD = 31 * 64 + 3 * 64
W_MISC_ROWS = 16
G_MISC_ROWS = 32
G_SMALL_ROWS = 8
REP_ROWS = G_MISC_ROWS - G_SMALL_ROWS
REP_NAMES = ("mix_norm", "a_b_in", "a_sinks", "a_conv_b", "a_cln_g", "a_cln_b", "c_w_pool", "c_w_s", "c_b_s",
             "ffn_norm", "final_norm")
Q_PERM = (0, 4, 1, 5, 2, 6, 3, 7)


def _params(*sem):
    return pltpu.CompilerParams(dimension_semantics=sem, vmem_limit_bytes=VMEM_LIMIT)


def _nn(a, b):
    return jnp.dot(a, b, preferred_element_type=F32)


def _nt(a, b):
    return lax.dot_general(a, b, (((1,), (1,)), ((), ())), preferred_element_type=F32)


def _tn(a, b):
    return lax.dot_general(a, b, (((0,), (0,)), ((), ())), preferred_element_type=F32)


def _tile(n, want=512):
    t = min(want, n)
    assert n % t == 0, (n, t)
    return t


def _seq_tile(s):
    return 512 if s >= 1024 else s // 2


def _rms(x, g):
    r = lax.rsqrt(jnp.mean(x * x, axis=-1, keepdims=True) + EPS)
    return x * r * g, r


def _rms_bwd(x, g, d_y):
    r = lax.rsqrt(jnp.mean(x * x, axis=-1, keepdims=True) + EPS)
    xr = x * r
    u = d_y * g
    d_x = r * (u - xr * jnp.mean(u * xr, axis=-1, keepdims=True))
    return d_x, jnp.sum(d_y * xr, axis=0, keepdims=True)


def _ln(y, g, b):
    mu = jnp.mean(y, axis=-1, keepdims=True)
    yc = y - mu
    rstd = lax.rsqrt(jnp.mean(yc * yc, axis=-1, keepdims=True) + EPS)
    xhat = yc * rstd
    return xhat * g + b, xhat, rstd


def _ln_bwd(d_o, xhat, rstd, g):
    dxh = d_o * g
    return rstd * (dxh - jnp.mean(dxh, axis=-1, keepdims=True) - xhat * jnp.mean(dxh * xhat, axis=-1, keepdims=True))


def _gelu(x):
    th = jnp.tanh(GELU_C * (x + GELU_A * x * x * x))
    return 0.5 * x * (1.0 + th), th


def _gelu_grad(x, th):
    return 0.5 * (1.0 + th) + 0.5 * x * (1.0 - th * th) * GELU_C * (1.0 + 3.0 * GELU_A * x * x)


def _row(c):
    return pl.BlockSpec((1, c), lambda *_: (0, 0))


def _full(shape):
    return pl.BlockSpec(shape, lambda *_: (0,) * len(shape))


def _norm_proj(h, g, wt, bias, name):
    t, n = h.shape[0], wt.shape[0]
    tm = _tile(t)
    has_bias = bias is not None

    def body(*refs):
        h_ref, g_ref, wt_ref = refs[:3]
        z_ref, hn_ref = refs[-2:]
        hn = _rms(h_ref[...].astype(F32), g_ref[...])[0].astype(BF16)
        hn_ref[...] = hn
        z = _nt(hn, wt_ref[...])
        if has_bias:
            z = z + refs[3][...]
        z_ref[...] = z.astype(BF16)

    in_specs = [pl.BlockSpec((tm, D), lambda i: (i, 0)), _row(D), _full((n, D))]
    args = [h, g, wt]
    if has_bias:
        in_specs.append(_row(n))
        args.append(bias)
    return pl.pallas_call(
        body, name=name, grid=(t // tm,), in_specs=in_specs,
        out_specs=[pl.BlockSpec((tm, n), lambda i: (i, 0)), pl.BlockSpec((tm, D), lambda i: (i, 0))],
        out_shape=[jax.ShapeDtypeStruct((t, n), BF16), jax.ShapeDtypeStruct((t, D), BF16)],
        compiler_params=_params("parallel"))(*args)


def _ff_pieces(tf, width=256):
    return [(c0, min(width, tf - c0)) for c0 in range(0, tf, width)]


def _ffn_fwd(h_prev, a, b, w_out, g, wtg, wtu, wd, name, head=None):
    t = h_prev.shape[0]
    tm, tf = _tile(t), FF_TILE_FWD
    nf = D_FF // tf
    n_head = 0 if head is None else 2

    def body(*refs):
        hp_ref, a_ref, b_ref, wa_ref, wb_ref, g_ref, wtg_ref, wtu_ref, wd_ref = refs[:9]
        hmid_ref, o_ref, hn_ref, gate_ref, up_ref = refs[9 + n_head:14 + n_head]
        acc, act = refs[-2:]
        i, f = pl.program_id(0), pl.program_id(1)

        if head is not None:
            fg_ref, t_ref = refs[9:11]
            dfg_ref, loss_ref = refs[16:18]

            @pl.when((i == 0) & (f == 0))
            def _():
                dfg_ref[...] = jnp.zeros_like(dfg_ref)
                loss_ref[...] = jnp.zeros_like(loss_ref)

        @pl.when(f == 0)
        def _():
            x = hp_ref[...].astype(F32) + _nn(a_ref[...], wa_ref[...]) + _nn(b_ref[...], wb_ref[...])
            hmid_ref[...] = x.astype(BF16)
            hn_ref[...] = _rms(x, g_ref[...])[0].astype(BF16)
            acc[...] = x

        hn = hn_ref[...]
        for c0, cw in _ff_pieces(tf):
            rows, cols = slice(c0, c0 + cw), slice(c0, c0 + cw)
            gate = _nt(hn, wtg_ref[rows, :])
            up = _nt(hn, wtu_ref[rows, :])
            gate_ref[:, cols] = gate.astype(BF16)
            up_ref[:, cols] = up.astype(BF16)
            act[:, cols] = (gate * jax.nn.sigmoid(gate) * up).astype(BF16)
        acc[...] += _nn(act[...], wd_ref[...])

        @pl.when(f == nf - 1)
        def _():
            if head is None:
                o_ref[...] = acc[...].astype(BF16)
            else:
                x, gv = acc[...], fg_ref[...]
                err = _rms(x, gv)[0] - t_ref[...]
                loss_ref[...] += 0.5 * jnp.sum(jnp.mean(err * err, axis=-1, keepdims=True), axis=0, keepdims=True)
                d_x, d_g = _rms_bwd(x, gv, err * (1.0 / D))
                o_ref[...] = d_x.astype(BF16)
                dfg_ref[...] += d_g

    tok = pl.BlockSpec((tm, D), lambda i, f: (i, 0))
    wsp = pl.BlockSpec((tf, D), lambda i, f: (f, 0))
    mid = pl.BlockSpec((tm, tf), lambda i, f: (i, f))
    half = pl.BlockSpec((tm, 512), lambda i, f: (i, 0))
    res = jax.ShapeDtypeStruct((t, D), BF16)
    in_specs = [tok, half, half, pl.BlockSpec((512, D), lambda i, f: (0, 0)), pl.BlockSpec((512, D), lambda i, f: (1, 0)),
                _row(D), wsp, wsp, wsp]
    out_specs = [tok, tok, tok, mid, mid]
    out_shape = [res, res, res, jax.ShapeDtypeStruct((t, D_FF), BF16), jax.ShapeDtypeStruct((t, D_FF), BF16)]
    if head is not None:
        in_specs += [_row(D), tok]
        out_specs += [_row(D), _row(1)]
        out_shape += [jax.ShapeDtypeStruct((1, D), F32), jax.ShapeDtypeStruct((1, 1), F32)]
    return pl.pallas_call(
        body, name=name, grid=(t // tm, nf), in_specs=in_specs, out_specs=out_specs, out_shape=out_shape,
        scratch_shapes=[pltpu.VMEM((tm, D), F32), pltpu.VMEM((tm, tf), BF16)],
        compiler_params=_params("parallel" if head is None else "arbitrary", "arbitrary"),
    )(h_prev, a, b, w_out, w_out, g, wtg, wtu, wd, *(head or ()))


def _ffn_bwd(dh, h, g, gate, up, wtg, wtu, wd, w_out, name):
    t = h.shape[0]
    tm, tf = _tile(t), FF_TILE_BWD
    nf = D_FF // tf

    def body(dh_ref, h_ref, g_ref, gate_ref, up_ref, wtg_ref, wtu_ref, wd_ref, wout_ref,
             dhin_ref, dmix_ref, dgate_ref, dup_ref, act_ref, dg_ref, d_hn):
        i, f = pl.program_id(0), pl.program_id(1)

        @pl.when(f == 0)
        def _():
            d_hn[...] = jnp.zeros_like(d_hn)

        @pl.when((i == 0) & (f == 0))
        def _():
            dg_ref[...] = jnp.zeros_like(dg_ref)

        dh = dh_ref[...]
        for c0, cw in _ff_pieces(tf):
            cols = slice(c0, c0 + cw)
            da = _nt(dh, wd_ref[c0:c0 + cw, :])
            gt = gate_ref[:, cols].astype(F32)
            u = up_ref[:, cols].astype(F32)
            sg = jax.nn.sigmoid(gt)
            sil = gt * sg
            act_ref[:, cols] = (sil * u).astype(BF16)
            dup_ref[:, cols] = (da * sil).astype(BF16)
            dgate_ref[:, cols] = (da * u * sg * (1.0 + gt * (1.0 - sg))).astype(BF16)
        d_hn[...] += _nn(dgate_ref[...], wtg_ref[...]) + _nn(dup_ref[...], wtu_ref[...])

        @pl.when(f == nf - 1)
        def _():
            d_x, d_g = _rms_bwd(h_ref[...].astype(F32), g_ref[...], d_hn[...])
            dhin = (dh_ref[...].astype(F32) + d_x).astype(BF16)
            dhin_ref[...] = dhin
            dmix_ref[...] = _nt(dhin, wout_ref[...]).astype(BF16)
            dg_ref[...] += d_g

    tok = pl.BlockSpec((tm, D), lambda i, f: (i, 0))
    wsp = pl.BlockSpec((tf, D), lambda i, f: (f, 0))
    mid = pl.BlockSpec((tm, tf), lambda i, f: (i, f))
    mid_shape = jax.ShapeDtypeStruct((t, D_FF), BF16)
    res = jax.ShapeDtypeStruct((t, D), BF16)
    return pl.pallas_call(
        body, name=name, grid=(t // tm, nf), in_specs=[tok, tok, _row(D), mid, mid, wsp, wsp, wsp, _full((D, D))],
        out_specs=[tok, tok, mid, mid, mid, _row(D)],
        out_shape=[res, res, mid_shape, mid_shape, mid_shape, jax.ShapeDtypeStruct((1, D), F32)],
        scratch_shapes=[pltpu.VMEM((tm, D), F32)],
        compiler_params=_params("arbitrary", "arbitrary"))(dh, h, g, gate, up, wtg, wtu, wd, w_out)


def _proj_bwd_norm(pieces, wt, h, dh, g, dtype, name):
    t = h.shape[0]
    tm = _tile(t)
    n_p = len(pieces)

    def body(*refs):
        p_refs, w_refs = refs[:n_p], refs[n_p:2 * n_p]
        h_ref, dh_ref, g_ref, o_ref, dg_ref = refs[2 * n_p:]

        @pl.when(pl.program_id(0) == 0)
        def _():
            dg_ref[...] = jnp.zeros_like(dg_ref)

        d_hn = _nn(p_refs[0][...], w_refs[0][...])
        for p_ref, w_ref in zip(p_refs[1:], w_refs[1:]):
            d_hn = d_hn + _nn(p_ref[...], w_ref[...])
        d_x, d_g = _rms_bwd(h_ref[...].astype(F32), g_ref[...], d_hn)
        o_ref[...] = (dh_ref[...].astype(F32) + d_x).astype(dtype)
        dg_ref[...] += d_g

    tok = pl.BlockSpec((tm, D), lambda i: (i, 0))
    in_specs = [pl.BlockSpec((tm, a.shape[1]), lambda i: (i, 0)) for a, _ in pieces]
    for a, off in pieces:
        w = a.shape[1]
        assert off % w == 0
        in_specs.append(pl.BlockSpec((w, D), functools.partial(lambda i, blk: (blk, 0), blk=off // w)))
    in_specs += [tok, tok, _row(D)]
    return pl.pallas_call(
        body, name=name, grid=(t // tm,), in_specs=in_specs, out_specs=[tok, _row(D)],
        out_shape=[jax.ShapeDtypeStruct((t, D), dtype), jax.ShapeDtypeStruct((1, D), F32)],
        compiler_params=_params("arbitrary"))(*[a for a, _ in pieces], *([wt] * n_p), h, dh, g)


def _mm_tn(a, b, name):
    t, n = a.shape
    k = b.shape[1]
    tn = n if n <= 1024 else n // 2
    tt = _tile(t, 1024)
    nt = t // tt

    def body(a_ref, b_ref, o_ref, acc):
        s = pl.program_id(1)

        @pl.when(s == 0)
        def _():
            acc[...] = jnp.zeros_like(acc)

        acc[...] += _tn(a_ref[...], b_ref[...].astype(BF16))

        @pl.when(s == nt - 1)
        def _():
            o_ref[...] = acc[...].astype(BF16)

    return pl.pallas_call(
        body, name=name, grid=(n // tn, nt),
        in_specs=[pl.BlockSpec((tt, tn), lambda j, s: (s, j)), pl.BlockSpec((tt, k), lambda j, s: (s, 0))],
        out_specs=pl.BlockSpec((tn, k), lambda j, s: (j, 0)), out_shape=jax.ShapeDtypeStruct((n, k), BF16),
        scratch_shapes=[pltpu.VMEM((tn, k), F32)],
        compiler_params=_params("parallel", "arbitrary"))(a, b)


def _mm_tn_pieces(pieces, b, name):
    t, k = b.shape
    widths = [p.shape[1] for p in pieces]
    n, n_p = sum(widths), len(pieces)
    tt = _tile(t, 1024)
    nt = t // tt

    def body(*refs):
        b_ref, o_ref, acc = refs[n_p:]
        s = pl.program_id(0)

        @pl.when(s == 0)
        def _():
            acc[...] = jnp.zeros_like(acc)

        bb = b_ref[...].astype(BF16)
        off = 0
        for p_ref, w in zip(refs[:n_p], widths):
            acc[off:off + w, :] += _tn(p_ref[...], bb)
            off += w

        @pl.when(s == nt - 1)
        def _():
            o_ref[...] = acc[...].astype(BF16)

    return pl.pallas_call(
        body, name=name, grid=(nt,),
        in_specs=[pl.BlockSpec((tt, w), lambda s: (s, 0)) for w in widths] + [pl.BlockSpec((tt, k), lambda s: (s, 0))],
        out_specs=_full((n, k)), out_shape=jax.ShapeDtypeStruct((n, k), BF16),
        scratch_shapes=[pltpu.VMEM((n, k), F32)], compiler_params=_params("arbitrary"))(*pieces, b)


STACK = HEAD_PAIRS * ATT_BLK


def _attn_valid(first, rows):
    qi = lax.broadcasted_iota(jnp.int32, (rows, 2 * ATT_BLK), 0) % ATT_BLK
    r = lax.broadcasted_iota(jnp.int32, (rows, 2 * ATT_BLK), 1)
    dist = qi + ATT_BLK - r
    return (dist >= 0) & (dist < ATT_BLK) & ((r >= ATT_BLK) | jnp.logical_not(first))


def _stacked(ref, kh, scale):
    lo = lax.broadcasted_iota(jnp.int32, (ATT_BLK, 128), 1) < 64
    keep = lo if kh == 0 else ~lo
    parts = [jnp.where(keep, ref[:, g * 128:(g + 1) * 128] * scale, 0.0).astype(BF16) for g in range(HEAD_PAIRS)]
    return jnp.concatenate(parts, axis=0)


def _unstacked(a0, a1, g):
    lo = lax.broadcasted_iota(jnp.int32, (ATT_BLK, 128), 1) < 64
    rows = slice(g * ATT_BLK, (g + 1) * ATT_BLK)
    return jnp.where(lo, a0[rows], a1[rows])


def _sink_rows(s_ref, kh):
    return jnp.concatenate([jnp.full((ATT_BLK, 128), s_ref[0, kh * 4 + g], F32) for g in range(HEAD_PAIRS)], axis=0)


def _row_sums(a, split):
    hi = a.astype(BF16)
    ones = jnp.ones((2 * ATT_BLK, 128), BF16)
    if not split:
        return _nn(hi, ones)
    lo = (a - hi.astype(F32)).astype(BF16)
    return _nn(hi, ones) + _nn(lo, ones)


def _both(a):
    return jnp.concatenate([a, a], axis=1)


def _attn_probs(qs, kpair, sink, valid):
    s = jnp.where(valid, _nt(qs, kpair), -1e30)
    m = jnp.maximum(jnp.broadcast_to(jnp.max(s, axis=-1, keepdims=True), (s.shape[0], 128)), sink)
    p = jnp.exp(s - _both(m))
    es = jnp.exp(sink - m)
    inv = 1.0 / (_row_sums(p, split=True) + es)
    return p * _both(inv), es * inv


def _attn_probs_head(qm, kpair, sink, valid):
    s = jnp.where(valid, _nt(qm, kpair), -1e30)
    m = jnp.maximum(jnp.max(s, axis=-1, keepdims=True), sink)
    p = jnp.exp(s - m)
    return p * (1.0 / (jnp.sum(p, axis=-1, keepdims=True) + jnp.exp(sink - m)))


def _attn_specs(bsz, order):
    q = pl.BlockSpec((bsz, ATT_BLK, 512), lambda j: (0, order(j), 0))
    kvc = pl.BlockSpec((bsz, ATT_BLK, 256), lambda j: (0, order(j), 6))
    kvp = pl.BlockSpec((bsz, ATT_BLK, 256), lambda j: (0, jnp.maximum(order(j) - 1, 0), 6))
    return q, kvc, kvp


def _window_kv(kvc_ref, kvp_ref):
    kvc, kvp = kvc_ref[...], kvp_ref[...]
    kpair = jnp.concatenate([kvp[:, :128], kvc[:, :128]], axis=0)
    vpair = jnp.concatenate([kvp[:, 128:], kvc[:, 128:]], axis=0)
    return kpair, vpair


def _attn_fwd(z0, sinks, bsz, name):
    t = z0.shape[0]
    seq = t // bsz
    nb = seq // ATT_BLK

    def body(s_ref, q_ref, kvc_ref, kvp_ref, o_ref, token):
        token[...] = jnp.zeros_like(token)
        valid = _attn_valid(pl.program_id(0) == 0, ATT_BLK)
        lo = lax.broadcasted_iota(jnp.int32, (ATT_BLK, 128), 1) < 64
        for b in range(bsz):
            kpair, vpair = _window_kv(kvc_ref.at[b], kvp_ref.at[b])
            for g in range(HEAD_PAIRS):
                qs = q_ref[b, :, g * 128:(g + 1) * 128] * 0.125
                outs = []
                for kh in range(2):
                    qm = jnp.where(lo if kh == 0 else ~lo, qs, 0.0).astype(BF16)
                    p = _attn_probs_head(qm, kpair, s_ref[0, kh * 4 + g], valid)
                    outs.append(_nn(p.astype(BF16), vpair))
                o_ref[b, :, g * 128:(g + 1) * 128] = jnp.where(lo, outs[0], outs[1]).astype(BF16)

    q, kvc, kvp = _attn_specs(bsz, lambda j: j)
    z3 = z0.reshape(bsz, seq, z0.shape[1])
    out, token = pl.pallas_call(
        body, name=name, grid=(nb,),
        in_specs=[pl.BlockSpec(memory_space=pltpu.SMEM), q, kvc, kvp],
        out_specs=[pl.BlockSpec((bsz, ATT_BLK, 512), lambda j: (0, j, 0)), _full((8, 128))],
        out_shape=[jax.ShapeDtypeStruct((bsz, seq, 512), BF16), jax.ShapeDtypeStruct((8, 128), F32)],
        compiler_params=_params("arbitrary"))(sinks, z3, z3, z3)
    return out.reshape(t, 512), token


def _attn_bwd(z0, dmix, sinks, bsz, name):
    t = z0.shape[0]
    seq = t // bsz
    nb = seq // ATT_BLK

    def body(s_ref, q_ref, kvc_ref, kvp_ref, do_ref, dq_ref, dkv_ref, dsink_ref, dbq_ref, dbkv_ref, carry):
        j = pl.program_id(0)

        @pl.when(j == 0)
        def _():
            carry[...] = jnp.zeros_like(carry)
            dsink_ref[...] = jnp.zeros_like(dsink_ref)
            dbq_ref[...] = jnp.zeros_like(dbq_ref)
            dbkv_ref[...] = jnp.zeros_like(dbkv_ref)

        valid = _attn_valid(j == nb - 1, STACK)
        lane = lax.broadcasted_iota(jnp.int32, (1, 128), 1)
        dsink = jnp.zeros((1, 128), F32)
        dbq = [jnp.zeros((1, 128), F32)] * HEAD_PAIRS
        dbkv = jnp.zeros((1, 256), F32)
        for b in range(bsz):
            kpair, vpair = _window_kv(kvc_ref.at[b], kvp_ref.at[b])
            dk = jnp.zeros((2 * ATT_BLK, 128), F32)
            dv = jnp.zeros((2 * ATT_BLK, 128), F32)
            dqs = []
            for kh in range(2):
                qs = _stacked(q_ref.at[b], kh, 0.125)
                dos = _stacked(do_ref.at[b], kh, 1.0)
                p, ps = _attn_probs(qs, kpair, _sink_rows(s_ref, kh), valid)
                dp = _nt(dos, vpair)
                delta = _row_sums(p * dp, split=False)
                ds = (p * (dp - _both(delta))).astype(BF16)
                dqs.append(_nn(ds, kpair))
                dk = dk + _tn(ds, qs)
                dv = dv + _tn(p.astype(BF16), dos)
                psd = ps * delta
                for g in range(HEAD_PAIRS):
                    part = jnp.sum(psd[g * ATT_BLK:(g + 1) * ATT_BLK], axis=0, keepdims=True)
                    dsink = dsink - jnp.where(lane == kh * 4 + g, part, 0.0)
            for g in range(HEAD_PAIRS):
                dq = _unstacked(dqs[0], dqs[1], g) * 0.125
                dq_ref[b, :, g * 128:(g + 1) * 128] = dq.astype(BF16)
                dbq[g] = dbq[g] + jnp.sum(dq, axis=0, keepdims=True)
            dkv = jnp.concatenate([dk[ATT_BLK:], dv[ATT_BLK:]], axis=1) + carry[b]
            dkv_ref[b] = dkv.astype(BF16)
            dbkv = dbkv + jnp.sum(dkv, axis=0, keepdims=True)
            carry[b] = jnp.concatenate([dk[:ATT_BLK], dv[:ATT_BLK]], axis=1)
        dsink_ref[...] += dsink
        dbq_ref[...] += jnp.concatenate(dbq, axis=1)
        dbkv_ref[...] += dbkv

    q, kvc, kvp = _attn_specs(bsz, lambda j: nb - 1 - j)
    z3 = z0.reshape(bsz, seq, z0.shape[1])
    d3 = dmix.reshape(bsz, seq, dmix.shape[1])
    dq, dkv, dsink, dbq, dbkv = pl.pallas_call(
        body, name=name, grid=(nb,),
        in_specs=[pl.BlockSpec(memory_space=pltpu.SMEM), q, kvc, kvp,
                  pl.BlockSpec((bsz, ATT_BLK, 512), lambda j: (0, nb - 1 - j, 0))],
        out_specs=[pl.BlockSpec((bsz, ATT_BLK, 512), lambda j: (0, nb - 1 - j, 0)),
                   pl.BlockSpec((bsz, ATT_BLK, 256), lambda j: (0, nb - 1 - j, 0)), _row(128), _row(512), _row(256)],
        out_shape=[jax.ShapeDtypeStruct((bsz, seq, 512), BF16), jax.ShapeDtypeStruct((bsz, seq, 256), BF16),
                   jax.ShapeDtypeStruct((1, 128), F32), jax.ShapeDtypeStruct((1, 512), F32),
                   jax.ShapeDtypeStruct((1, 256), F32)],
        scratch_shapes=[pltpu.VMEM((bsz, ATT_BLK, 256), F32)],
        compiler_params=_params("arbitrary"))(sinks, z3, z3, z3, d3)
    return dq.reshape(t, 512), dkv.reshape(t, 256), dsink, dbq, dbkv


def _seq_specs(ts, nt, t, width, col):
    per = ts // HALO
    cur = pl.BlockSpec((ts, width), lambda b, i: (b * nt + i, col))
    prev = pl.BlockSpec((HALO, width), lambda b, i: (jnp.maximum((b * nt + i) * per - 1, 0), col))
    nxt = pl.BlockSpec((HALO, width), lambda b, i: (jnp.minimum((b * nt + i + 1) * per, t // HALO - 1), col))
    return prev, cur, nxt


SUB = 8
CONV_ROWS = 64


def _shifted_copies(src, sh, rows_first, rows_rest):
    for r in range(SUB):
        rows = rows_first if r == 0 else rows_rest
        sh[r, pl.ds(0, rows), :] = src[pl.ds(r, rows), :]


def _tap_sum(sh, w, offset, c0, rows):
    acc = None
    for k in range(CONV_K):
        o = offset(k)
        term = sh[o % SUB, pl.ds(c0 + o - o % SUB, rows), :] * w[k:k + 1, :]
        acc = term if acc is None else acc + term
    return acc


def _glu_rows(a_ref, g_ref, rows=slice(None)):
    return a_ref[rows, :].astype(F32) * jax.nn.sigmoid(g_ref[rows, :].astype(F32))


def _conv_fwd(z0, conv_w, conv_b, ln_g, ln_b, bsz, name):
    t = z0.shape[0]
    s = t // bsz
    ts = _seq_tile(s)
    nt = s // ts
    first = HALO - (CONV_K - 1)

    def body(ap_ref, ac_ref, gp_ref, gc_ref, w_ref, cb_ref, lg_ref, lb_ref, o_ref, y_ref, token, hbuf, sh):
        token[...] = jnp.zeros_like(token)
        hbuf[0:HALO, :] = jnp.where(pl.program_id(1) > 0, _glu_rows(ap_ref, gp_ref), 0.0)
        hbuf[HALO:HALO + ts, :] = _glu_rows(ac_ref, gc_ref)
        _shifted_copies(hbuf, sh, ts + HALO, ts + HALO - SUB)
        w, cb, lg, lb = w_ref[...], cb_ref[...], lg_ref[...], lb_ref[...]
        for c0 in range(0, ts, CONV_ROWS):
            y = _tap_sum(sh, w, lambda k: first + k, c0, CONV_ROWS) + cb
            y_ref[c0:c0 + CONV_ROWS, :] = y
            o = _ln(y, lg, lb)[0]
            o_ref[c0:c0 + CONV_ROWS, :] = (o * jax.nn.sigmoid(o)).astype(BF16)

    ap, ac, _ = _seq_specs(ts, nt, t, 512, 1)
    gp, gc, _ = _seq_specs(ts, nt, t, 512, 2)
    tile = pl.BlockSpec((ts, 512), lambda b, i: (b * nt + i, 0))
    return pl.pallas_call(
        body, name=name, grid=(bsz, nt),
        in_specs=[ap, ac, gp, gc, _full((HALO, 512)), _row(512), _row(512), _row(512)],
        out_specs=[tile, tile, _full((8, 128))],
        out_shape=[jax.ShapeDtypeStruct((t, 512), BF16), jax.ShapeDtypeStruct((t, 512), F32),
                   jax.ShapeDtypeStruct((8, 128), F32)],
        scratch_shapes=[pltpu.VMEM((HALO + ts, 512), F32), pltpu.VMEM((SUB, HALO + ts, 512), F32)],
        compiler_params=_params("arbitrary", "arbitrary"))(z0, z0, z0, z0, conv_w, conv_b, ln_g, ln_b)


def _conv_bwd(z0, y, dmix, conv_w, ln_g, ln_b, bsz, name):
    t = z0.shape[0]
    s = t // bsz
    ts = _seq_tile(s)
    nt = s // ts

    def body(ac_ref, gc_ref, yc_ref, yn_ref, dc_ref, dn_ref, w_ref, lg_ref, lb_ref,
             da_ref, dg_ref, dw_ref, dcb_ref, dlg_ref, dlb_ref, dba_ref, dbg_ref, hcur, dybuf, sh_dy):
        b, i = pl.program_id(0), pl.program_id(1)

        @pl.when((b == 0) & (i == 0))
        def _():
            for ref in (dw_ref, dcb_ref, dlg_ref, dlb_ref, dba_ref, dbg_ref):
                ref[...] = jnp.zeros_like(ref)

        w, lg, lb = w_ref[...], lg_ref[...], lb_ref[...]
        hcur[...] = _glu_rows(ac_ref, gc_ref)

        def d_conv_out(yv, dout):
            o, xhat, rstd = _ln(yv, lg, lb)
            sg_o = jax.nn.sigmoid(o)
            d_o = dout * sg_o * (1.0 + o * (1.0 - sg_o))
            return _ln_bwd(d_o, xhat, rstd, lg), d_o * xhat, d_o

        dlg = jnp.zeros((1, 512), F32)
        dlb = jnp.zeros((1, 512), F32)
        dcb = jnp.zeros((1, 512), F32)
        for c0 in range(0, ts, CONV_ROWS):
            rows = slice(c0, c0 + CONV_ROWS)
            dy, g_part, b_part = d_conv_out(yc_ref[rows, :], dc_ref[rows, :].astype(F32))
            dybuf[rows, :] = dy
            dlg = dlg + jnp.sum(g_part, axis=0, keepdims=True)
            dlb = dlb + jnp.sum(b_part, axis=0, keepdims=True)
            dcb = dcb + jnp.sum(dy, axis=0, keepdims=True)
        dn = jnp.where(i < nt - 1, dn_ref[...].astype(F32), 0.0)
        dybuf[ts:ts + HALO, :] = d_conv_out(yn_ref[...], dn)[0]
        dlg_ref[...] += dlg
        dlb_ref[...] += dlb
        dcb_ref[...] += dcb
        _shifted_copies(dybuf, sh_dy, ts + HALO - SUB, ts + HALO - SUB)

        for k in range(CONV_K):
            o = CONV_K - 1 - k
            prod = hcur[...] * sh_dy[o % SUB, pl.ds(o - o % SUB, ts), :]
            dw_ref[pl.ds(k, 1), :] += jnp.sum(prod, axis=0, keepdims=True)
        dba = jnp.zeros((1, 512), F32)
        dbg = jnp.zeros((1, 512), F32)
        for c0 in range(0, ts, CONV_ROWS):
            rows = slice(c0, c0 + CONV_ROWS)
            dh = _tap_sum(sh_dy, w, lambda k: CONV_K - 1 - k, c0, CONV_ROWS)
            a_c = ac_ref[rows, :].astype(F32)
            sg_c = jax.nn.sigmoid(gc_ref[rows, :].astype(F32))
            d_a = dh * sg_c
            d_g = dh * a_c * sg_c * (1.0 - sg_c)
            da_ref[rows, :] = d_a.astype(BF16)
            dg_ref[rows, :] = d_g.astype(BF16)
            dba = dba + jnp.sum(d_a, axis=0, keepdims=True)
            dbg = dbg + jnp.sum(d_g, axis=0, keepdims=True)
        dba_ref[...] += dba
        dbg_ref[...] += dbg

    _, ac, _ = _seq_specs(ts, nt, t, 512, 1)
    _, gc, _ = _seq_specs(ts, nt, t, 512, 2)
    _, yc, yn = _seq_specs(ts, nt, t, 512, 0)
    _, dc, dn = _seq_specs(ts, nt, t, 512, 1)
    tile = pl.BlockSpec((ts, 512), lambda b, i: (b * nt + i, 0))
    vec = jax.ShapeDtypeStruct((1, 512), F32)
    return pl.pallas_call(
        body, name=name, grid=(bsz, nt),
        in_specs=[ac, gc, yc, yn, dc, dn, _full((HALO, 512)), _row(512), _row(512)],
        out_specs=[tile, tile, _full((HALO, 512)), _row(512), _row(512), _row(512), _row(512), _row(512)],
        out_shape=[jax.ShapeDtypeStruct((t, 512), BF16), jax.ShapeDtypeStruct((t, 512), BF16),
                   jax.ShapeDtypeStruct((HALO, 512), F32), vec, vec, vec, vec, vec],
        scratch_shapes=[pltpu.VMEM((ts, 512), F32), pltpu.VMEM((ts + HALO, 512), F32),
                        pltpu.VMEM((SUB, HALO + ts, 512), F32)],
        compiler_params=_params("arbitrary", "arbitrary"))(z0, z0, y, y, dmix, dmix, conv_w, ln_g, ln_b)


def _pooled(pbuf, g, ts, tok):
    w = 2 << g
    cols = slice(128 * g, 128 * (g + 1))
    sm = pbuf[pl.ds(HALO, ts), cols]
    for d in range(1, w):
        sm = sm + pbuf[pl.ds(HALO - d, ts), cols]
    cnt = jnp.minimum(tok + 1, w).astype(F32)
    return sm / cnt - pbuf[pl.ds(HALO, ts), cols]


def _pool_fwd(z1, w_pool, scale, bsz, name):
    t = z1.shape[0]
    s = t // bsz
    ts = _seq_tile(s)
    nt = s // ts

    def body(zp_ref, zc_ref, wp_ref, sc_ref, o_ref, token, pbuf):
        token[...] = jnp.zeros_like(token)
        i = pl.program_id(1)
        pbuf[0:HALO, :] = jnp.where(i > 0, zp_ref[...].astype(F32), 0.0)
        pbuf[HALO:HALO + ts, :] = zc_ref[...].astype(F32)
        tok = i * ts + lax.broadcasted_iota(jnp.int32, (ts, 1), 0)
        for g in range(4):
            cols = slice(128 * g, 128 * (g + 1))
            pooled = _pooled(pbuf, g, ts, tok).astype(BF16)
            o_ref[:, cols] = (_nn(pooled, wp_ref[g].astype(BF16)) * sc_ref[:, cols]).astype(BF16)

    zp, zc, _ = _seq_specs(ts, nt, t, 512, 0)
    return pl.pallas_call(
        body, name=name, grid=(bsz, nt), in_specs=[zp, zc, _full((4, 128, 128)), _row(512)],
        out_specs=[pl.BlockSpec((ts, 512), lambda b, i: (b * nt + i, 0)), _full((8, 128))],
        out_shape=[jax.ShapeDtypeStruct((t, 512), BF16), jax.ShapeDtypeStruct((8, 128), F32)],
        scratch_shapes=[pltpu.VMEM((HALO + ts, 512), F32)],
        compiler_params=_params("arbitrary", "arbitrary"))(z1, z1, w_pool, scale)


def _pool_bwd(z1, dmix, w_pool, scale, bsz, name):
    t = z1.shape[0]
    s = t // bsz
    ts = _seq_tile(s)
    nt = s // ts
    rr = ts + HALO

    def body(zp_ref, zc_ref, dc_ref, dn_ref, wp_ref, sc_ref, dz_ref, dwp_ref, dsc_ref, pbuf, ebuf):
        b, i = pl.program_id(0), pl.program_id(1)

        @pl.when((b == 0) & (i == 0))
        def _():
            dwp_ref[...] = jnp.zeros_like(dwp_ref)
            dsc_ref[...] = jnp.zeros_like(dsc_ref)

        pbuf[0:HALO, :] = jnp.where(i > 0, zp_ref[...].astype(F32), 0.0)
        pbuf[HALO:HALO + ts, :] = zc_ref[...].astype(F32)
        dn = jnp.where(i < nt - 1, dn_ref[...].astype(F32), 0.0)
        dout = jnp.concatenate([dc_ref[...].astype(F32), dn], axis=0)
        tok = i * ts + lax.broadcasted_iota(jnp.int32, (ts, 1), 0)
        tok_r = i * ts + lax.broadcasted_iota(jnp.int32, (rr, 1), 0)
        for g in range(4):
            w = 2 << g
            cols = slice(128 * g, 128 * (g + 1))
            wg = wp_ref[g].astype(BF16)
            pooled = _pooled(pbuf, g, ts, tok).astype(BF16)
            dsc_ref[:, cols] += jnp.sum(dout[:ts, cols] * _nn(pooled, wg), axis=0, keepdims=True)
            dy = (dout[:, cols] * sc_ref[:, cols]).astype(BF16)
            dwp_ref[g] += _tn(pooled, dy[:ts])
            dpl = _nt(dy, wg)
            ebuf[...] = dpl / jnp.minimum(tok_r + 1, w).astype(F32)
            dz = ebuf[pl.ds(0, ts), :] - dpl[:ts]
            for d in range(1, w):
                dz = dz + ebuf[pl.ds(d, ts), :]
            dz_ref[:, cols] = dz.astype(BF16)

    zp, zc, _ = _seq_specs(ts, nt, t, 512, 0)
    _, dc, dn = _seq_specs(ts, nt, t, 512, 0)
    return pl.pallas_call(
        body, name=name, grid=(bsz, nt), in_specs=[zp, zc, dc, dn, _full((4, 128, 128)), _row(512)],
        out_specs=[pl.BlockSpec((ts, 512), lambda b, i: (b * nt + i, 0)), _full((4, 128, 128)), _row(512)],
        out_shape=[jax.ShapeDtypeStruct((t, 512), BF16), jax.ShapeDtypeStruct((4, 128, 128), F32),
                   jax.ShapeDtypeStruct((1, 512), F32)],
        scratch_shapes=[pltpu.VMEM((HALO + ts, 512), F32), pltpu.VMEM((rr, 128), F32)],
        compiler_params=_params("arbitrary", "arbitrary"))(z1, z1, dmix, dmix, w_pool, scale)


def _tril():
    r = lax.broadcasted_iota(jnp.int32, (SGU_CHUNK, SGU_CHUNK), 0)
    c = lax.broadcasted_iota(jnp.int32, (SGU_CHUNK, SGU_CHUNK), 1)
    return r >= c


def _sgu_fwd(z1, ln_g, ln_b, w_s, b_rows, name):
    t = z1.shape[0]
    ts = _tile(t)

    def body(zu_ref, zv_ref, lg_ref, lb_ref, ws_ref, bs_ref, o_ref):
        v = _gelu(zv_ref[...].astype(F32))[0]
        vb = _ln(v, lg_ref[...], lb_ref[...])[0].astype(BF16)
        tril = _tril()
        for g in range(4):
            cols = slice(128 * g, 128 * (g + 1))
            wg = jnp.where(tril, ws_ref[g], 0.0).astype(BF16)
            for c in range(ts // SGU_CHUNK):
                rows = slice(SGU_CHUNK * c, SGU_CHUNK * (c + 1))
                mixed = _nn(wg, vb[rows, cols]) + bs_ref[g]
                o_ref[rows, cols] = (_gelu(zu_ref[rows, cols].astype(F32))[0] * mixed).astype(BF16)

    return pl.pallas_call(
        body, name=name, grid=(t // ts,),
        in_specs=[pl.BlockSpec((ts, 512), lambda i: (i, 1)), pl.BlockSpec((ts, 512), lambda i: (i, 2)),
                  _row(512), _row(512), _full((4, 128, 128)), _full((4, 128, 128))],
        out_specs=pl.BlockSpec((ts, 512), lambda i: (i, 0)), out_shape=jax.ShapeDtypeStruct((t, 512), BF16),
        compiler_params=_params("parallel"))(z1, z1, ln_g, ln_b, w_s, b_rows)


def _sgu_bwd(z1, dmix, ln_g, ln_b, w_s, b_rows, name):
    t = z1.shape[0]
    ts = _tile(t)

    def body(zu_ref, zv_ref, d_ref, lg_ref, lb_ref, ws_ref, bs_ref,
             dzu_ref, dzv_ref, dws_ref, dbs_ref, dlg_ref, dlb_ref, dvbuf):
        @pl.when(pl.program_id(0) == 0)
        def _():
            for ref in (dws_ref, dbs_ref, dlg_ref, dlb_ref):
                ref[...] = jnp.zeros_like(ref)

        zv = zv_ref[...].astype(F32)
        v, thv = _gelu(zv)
        lg = lg_ref[...]
        vln, xhat, rstd = _ln(v, lg, lb_ref[...])
        vb = vln.astype(BF16)
        tril = _tril()
        for g in range(4):
            cols = slice(128 * g, 128 * (g + 1))
            wg = jnp.where(tril, ws_ref[g], 0.0).astype(BF16)
            dws = jnp.zeros((SGU_CHUNK, SGU_CHUNK), F32)
            dbs = jnp.zeros((1, SGU_CHUNK), F32)
            for c in range(ts // SGU_CHUNK):
                rows = slice(SGU_CHUNK * c, SGU_CHUNK * (c + 1))
                vbc = vb[rows, cols]
                mixed = _nn(wg, vbc) + bs_ref[g]
                zu = zu_ref[rows, cols].astype(F32)
                u, thu = _gelu(zu)
                dout = d_ref[rows, cols].astype(F32)
                dzu_ref[rows, cols] = (dout * mixed * _gelu_grad(zu, thu)).astype(BF16)
                dm = dout * u
                dmb = dm.astype(BF16)
                dws = dws + _nt(dmb, vbc)
                dbs = dbs + jnp.sum(dm.T, axis=0, keepdims=True)
                dvbuf[rows, cols] = _tn(wg, dmb)
            dws_ref[g] += jnp.where(tril, dws, 0.0)
            dbs_ref[pl.ds(g, 1), :] += dbs
        dvln = dvbuf[...]
        dlg_ref[...] += jnp.sum(dvln * xhat, axis=0, keepdims=True)
        dlb_ref[...] += jnp.sum(dvln, axis=0, keepdims=True)
        dzv_ref[...] = (_ln_bwd(dvln, xhat, rstd, lg) * _gelu_grad(zv, thv)).astype(BF16)

    tile = pl.BlockSpec((ts, 512), lambda i: (i, 0))
    vec = jax.ShapeDtypeStruct((1, 512), F32)
    return pl.pallas_call(
        body, name=name, grid=(t // ts,),
        in_specs=[pl.BlockSpec((ts, 512), lambda i: (i, 1)), pl.BlockSpec((ts, 512), lambda i: (i, 2)),
                  pl.BlockSpec((ts, 512), lambda i: (i, 1)), _row(512), _row(512), _full((4, 128, 128)),
                  _full((4, 128, 128))],
        out_specs=[tile, tile, _full((4, 128, 128)), _full((4, 128)), _row(512), _row(512)],
        out_shape=[jax.ShapeDtypeStruct((t, 512), BF16), jax.ShapeDtypeStruct((t, 512), BF16),
                   jax.ShapeDtypeStruct((4, 128, 128), F32), jax.ShapeDtypeStruct((4, 128), F32), vec, vec],
        scratch_shapes=[pltpu.VMEM((ts, 512), F32)],
        compiler_params=_params("arbitrary"))(z1, z1, dmix, ln_g, ln_b, w_s, b_rows)


def _row_tile(r):
    for cand in (512, 352, 256, 192, 128, 64, 32, 16, 8):
        if r % cand == 0:
            return cand
    return r


def _sum_slabs(a, name):
    k, r, c = a.shape
    tr = _row_tile(r)

    def body(*refs):
        acc = refs[0][...].astype(F32)
        for ref in refs[1:-1]:
            acc = acc + ref[...].astype(F32)
        refs[-1][...] = acc

    in_specs = [pl.BlockSpec((None, tr, c), functools.partial(lambda i, s: (s, i, 0), s=s)) for s in range(k)]
    return pl.pallas_call(
        body, name=name, grid=(r // tr,), in_specs=in_specs, out_specs=pl.BlockSpec((tr, c), lambda i: (i, 0)),
        out_shape=jax.ShapeDtypeStruct((r, c), F32), compiler_params=_params("parallel"))(*([a] * k))


def _adamw_math(w, g, m, v):
    mn = ADAM_B1 * m + (1.0 - ADAM_B1) * g
    vn = ADAM_B2 * v + (1.0 - ADAM_B2) * (g * g)
    m_hat = mn / (1.0 - ADAM_B1 ** ADAM_STEP)
    v_hat = vn / (1.0 - ADAM_B2 ** ADAM_STEP)
    return -ADAM_LR * (m_hat / (jnp.sqrt(v_hat) + ADAM_EPS) + ADAM_WD * w), mn, vn


def _reduce_adamw(landing, w, m, v, name, layer=None, into=None):
    k, r, c = landing.shape
    tr = _row_tile(r)
    n_into = 0 if into is None else 4

    def body(*refs):
        slabs, (w_ref, m_ref, v_ref) = refs[:k], refs[k:k + 3]
        g_ref, d_ref, mo_ref, vo_ref = refs[k + 3 + n_into:]
        g = slabs[0][...].astype(F32)
        for ref in slabs[1:]:
            g = g + ref[...].astype(F32)
        g_ref[...] = g
        d_ref[...], mo_ref[...], vo_ref[...] = _adamw_math(w_ref[...], g, m_ref[...], v_ref[...])

    if layer is None:
        spec = pl.BlockSpec((tr, c), lambda i: (i, 0))
    else:
        spec = pl.BlockSpec((None, tr, c), lambda i: (layer, i, 0))
    in_specs = [pl.BlockSpec((None, tr, c), functools.partial(lambda i, s: (s, i, 0), s=s)) for s in range(k)]
    in_specs += [spec] * 3 + [ANY] * n_into
    shape = jax.ShapeDtypeStruct(w.shape, F32)
    return pl.pallas_call(
        body, name=name, grid=(r // tr,), in_specs=in_specs, out_specs=[spec] * 4, out_shape=[shape] * 4,
        input_output_aliases={k + 3 + j: j for j in range(n_into)},
        compiler_params=_params("parallel"))(*([landing] * k), w, m, v, *(into or ()))


def _adamw(w, g, m, v, name):
    r, c = w.shape
    tr = _row_tile(r)

    def body(w_ref, g_ref, m_ref, v_ref, d_ref, mo_ref, vo_ref):
        d_ref[...], mo_ref[...], vo_ref[...] = _adamw_math(w_ref[...], g_ref[...], m_ref[...], v_ref[...])

    spec = pl.BlockSpec((tr, c), lambda i: (i, 0))
    shape = jax.ShapeDtypeStruct((r, c), F32)
    return pl.pallas_call(
        body, name=name, grid=(r // tr,), in_specs=[spec] * 4, out_specs=[spec] * 3, out_shape=[shape] * 3,
        compiler_params=_params("parallel"))(w, g, m, v)


ANY = pl.BlockSpec(memory_space=pl.ANY)


def _all_gather(block, name):
    r, c_dim = block.shape

    def body(x_ref, out_ref, token, send_sems, recv_sems, local_sem):
        token[...] = jnp.zeros_like(token)
        x, y, c = lax.axis_index("x"), lax.axis_index("y"), lax.axis_index("c")
        me, sibling = (x, y, c), (x, y, 1 - c)
        chips = [(1 - x, y), (x, 1 - y), (1 - x, 1 - y)]

        def rows(px, py, pc):
            return out_ref.at[4 * px + 2 * py + pc]

        def copy(k, blk, to, src=None):
            return pltpu.make_async_remote_copy(
                src_ref=rows(*blk) if src is None else src, dst_ref=rows(*blk), send_sem=send_sems.at[k],
                recv_sem=recv_sems.at[k], device_id=to, device_id_type=MESH)

        mine = pltpu.make_async_copy(x_ref, rows(*me), local_sem)
        mine.start()
        first = [copy(0, me, sibling, src=x_ref)]
        first += [copy(1 + j, me, (*chip, c), src=x_ref) for j, chip in enumerate(chips)]
        for cp in first:
            cp.start()
        passed = [copy(4 + j, (*chip, c), sibling) for j, chip in enumerate(chips)]
        for j, chip in enumerate(chips):
            copy(1 + j, (*chip, c), me).wait_recv()
            passed[j].start()
        copy(0, sibling, me).wait_recv()
        for j, chip in enumerate(chips):
            copy(4 + j, (*chip, 1 - c), me).wait_recv()
        for cp in first + passed:
            cp.wait_send()
        mine.wait()

    return pl.pallas_call(
        body, name=name, in_specs=[ANY], out_specs=[ANY, pl.BlockSpec(memory_space=pltpu.VMEM)],
        out_shape=[jax.ShapeDtypeStruct((N_DEV, r, c_dim), block.dtype), jax.ShapeDtypeStruct((8, 128), F32)],
        scratch_shapes=[pltpu.SemaphoreType.DMA((7,)), pltpu.SemaphoreType.DMA((7,)), pltpu.SemaphoreType.DMA],
    )(block)


HBM = pl.BlockSpec(memory_space=pltpu.HBM)
SEM = pl.BlockSpec(memory_space=pltpu.SEMAPHORE)
EFFECT = pltpu.SideEffectType.DATAFLOW_SIDE_EFFECTING


def _exchange_copies(scatter, src_refs, land_refs, send_sems, recv_sems, local_sems):
    x, y, c = lax.axis_index("x"), lax.axis_index("y"), lax.axis_index("c")
    me = 4 * x + 2 * y + c
    sends, arrivals, locals_ = [], [], []
    for a, (src, land) in enumerate(zip(src_refs, land_refs)):
        def pick(idx, src=src):
            return src.at[idx] if scatter else src

        locals_.append(pltpu.make_async_copy(pick(me), land.at[me], local_sems.at[a]))
        for r in range(1, N_DEV):
            px = 1 - x if r & 4 else x
            py = 1 - y if r & 2 else y
            pc = 1 - c if r & 1 else c
            peer, s = 4 * px + 2 * py + pc, 7 * a + r - 1
            sends.append(pltpu.make_async_remote_copy(
                src_ref=pick(peer), dst_ref=land.at[me], send_sem=send_sems.at[s], recv_sem=recv_sems.at[s],
                device_id=(px, py, pc), device_id_type=MESH))
            arrivals.append(pltpu.make_async_remote_copy(
                src_ref=pick(peer), dst_ref=land.at[peer], send_sem=send_sems.at[s], recv_sem=recv_sems.at[s],
                device_id=(px, py, pc), device_id_type=MESH))
    return sends, arrivals, locals_


def _exchange_start(srcs, scatter, name):
    n = len(srcs)
    lands = [lax.empty((N_DEV,) + s.shape[-2:], s.dtype) for s in srcs]

    def body(*refs):
        src_refs, land_refs = refs[:n], refs[n:2 * n]
        send_sems, recv_sems, local_sems = refs[2 * n:2 * n + 3]
        token = refs[-1]
        sends, _, locals_ = _exchange_copies(scatter, src_refs, land_refs, send_sems, recv_sems, local_sems)
        for cp in locals_ + sends:
            cp.start()
        token[...] = jnp.zeros_like(token)

    res = pl.pallas_call(
        body, name=name,
        out_shape=[pltpu.SemaphoreType.DMA((7 * n,)), pltpu.SemaphoreType.DMA((7 * n,)), pltpu.SemaphoreType.DMA((n,))]
        + [pltpu.HBM(a.shape, a.dtype) for a in list(srcs) + lands] + [jax.ShapeDtypeStruct((8, 128), F32)],
        in_specs=[HBM] * (2 * n), out_specs=[SEM] * 3 + [HBM] * (2 * n) + [pl.BlockSpec(memory_space=pltpu.VMEM)],
        input_output_aliases={i: 3 + i for i in range(2 * n)},
        compiler_params=pltpu.CompilerParams(has_side_effects=EFFECT),
    )(*[pltpu.with_memory_space_constraint(a, pltpu.HBM) for a in list(srcs) + lands])
    return (n, scatter, res[:3], res[3:3 + 2 * n]), res[-1]


def _exchange_wait(handle, after, name):
    n, scatter, sems, thru = handle

    def body(*refs):
        src_refs, land_refs = refs[:n], refs[n:2 * n]
        send_sems, recv_sems, local_sems = refs[2 * n:2 * n + 3]
        sends, arrivals, locals_ = _exchange_copies(scatter, src_refs, land_refs, send_sems, recv_sems, local_sems)
        for cp in arrivals:
            cp.wait_recv()
        for cp in sends:
            cp.wait_send()
        for cp in locals_:
            cp.wait()

    res = pl.pallas_call(
        body, name=name, out_shape=[pltpu.HBM(a.shape, a.dtype) for a in thru],
        in_specs=[HBM] * (2 * n) + [SEM] * 3 + [ANY], out_specs=[HBM] * (2 * n),
        input_output_aliases={i: i for i in range(2 * n)},
        compiler_params=pltpu.CompilerParams(has_side_effects=EFFECT),
    )(*thru, *sems, after)
    return res[n:]


def _behind(tokens, a):
    zero = sum(tok[0, 0] for tok in tokens)
    return jax.tree.map(lambda v: v + zero.astype(v.dtype), a)


def _perm_heads(a, perm, axis):
    idx = [slice(None)] * a.ndim
    parts = []
    for h in perm:
        idx[axis] = slice(64 * h, 64 * (h + 1))
        parts.append(a[tuple(idx)])
    idx[axis] = slice(512, None)
    if a.shape[axis] > 512:
        parts.append(a[tuple(idx)])
    return jnp.concatenate(parts, axis=axis)


Q_INV = tuple(int(i) for i in np.argsort(Q_PERM))


def _in0_to_kernel(a, axis):
    a = _perm_heads(a, Q_PERM, axis)
    idx = [slice(None)] * a.ndim

    def cut(lo, hi):
        idx[axis] = slice(lo, hi)
        return a[tuple(idx)]

    return jnp.concatenate([cut(0, 512), cut(768, 1792), cut(512, 768)], axis=axis)


def _in0_from_kernel(a, axis):
    idx = [slice(None)] * a.ndim

    def cut(lo, hi):
        idx[axis] = slice(lo, hi)
        return a[tuple(idx)]

    a = jnp.concatenate([cut(0, 512), cut(1536, 1792), cut(512, 1536)], axis=axis)
    return _perm_heads(a, Q_INV, axis)


def _f32_as_u16_rows(v, rows):
    bits = lax.bitcast_convert_type(v, jnp.uint16).reshape(-1)
    return jnp.pad(bits, (0, rows * D - bits.shape[0])).reshape(rows, D)


def _pad_rows(v, rows):
    v = v.reshape(-1)
    return jnp.pad(v, (0, rows * D - v.shape[0])).reshape(rows, D)


def kernel(x, mix_norm, a_w_in, a_b_in, a_sinks, a_conv_w, a_conv_b, a_cln_g, a_cln_b, a_w_out, c_w_in, c_w_pool, c_pool_scale, c_sln_g, c_sln_b, c_w_s, c_b_s, c_w_out, ffn_norm, ffn_w_gate, ffn_w_up, ffn_w_down, final_norm, loss_target, m_mix_norm, m_a_w_in, m_a_b_in, m_a_sinks, m_a_conv_w, m_a_conv_b, m_a_cln_g, m_a_cln_b, m_a_w_out, m_c_w_in, m_c_w_pool, m_c_pool_scale, m_c_sln_g, m_c_sln_b, m_c_w_s, m_c_b_s, m_c_w_out, m_ffn_norm, m_ffn_w_gate, m_ffn_w_up, m_ffn_w_down, m_final_norm, v_mix_norm, v_a_w_in, v_a_b_in, v_a_sinks, v_a_conv_w, v_a_conv_b, v_a_cln_g, v_a_cln_b, v_a_w_out, v_c_w_in, v_c_w_pool, v_c_pool_scale, v_c_sln_g, v_c_sln_b, v_c_w_s, v_c_b_s, v_c_w_out, v_ffn_norm, v_ffn_w_gate, v_ffn_w_up, v_ffn_w_down, v_final_norm):
    bsz, seq, _ = x.shape
    t = bsz * seq
    w_in = dict(mix_norm=mix_norm, a_w_in=a_w_in, a_b_in=a_b_in, a_sinks=a_sinks, a_conv_w=a_conv_w, a_conv_b=a_conv_b,
                a_cln_g=a_cln_g, a_cln_b=a_cln_b, a_w_out=a_w_out, c_w_in=c_w_in, c_w_pool=c_w_pool,
                c_pool_scale=c_pool_scale, c_sln_g=c_sln_g, c_sln_b=c_sln_b, c_w_s=c_w_s, c_b_s=c_b_s, c_w_out=c_w_out,
                ffn_norm=ffn_norm, ffn_w_gate=ffn_w_gate, ffn_w_up=ffn_w_up, ffn_w_down=ffn_w_down, final_norm=final_norm)
    m_in = dict(mix_norm=m_mix_norm, a_w_in=m_a_w_in, a_b_in=m_a_b_in, a_sinks=m_a_sinks, a_conv_w=m_a_conv_w,
                a_conv_b=m_a_conv_b, a_cln_g=m_a_cln_g, a_cln_b=m_a_cln_b, a_w_out=m_a_w_out, c_w_in=m_c_w_in,
                c_w_pool=m_c_w_pool, c_pool_scale=m_c_pool_scale, c_sln_g=m_c_sln_g, c_sln_b=m_c_sln_b, c_w_s=m_c_w_s,
                c_b_s=m_c_b_s, c_w_out=m_c_w_out, ffn_norm=m_ffn_norm, ffn_w_gate=m_ffn_w_gate, ffn_w_up=m_ffn_w_up,
                ffn_w_down=m_ffn_w_down, final_norm=m_final_norm)
    v_in = dict(mix_norm=v_mix_norm, a_w_in=v_a_w_in, a_b_in=v_a_b_in, a_sinks=v_a_sinks, a_conv_w=v_a_conv_w,
                a_conv_b=v_a_conv_b, a_cln_g=v_a_cln_g, a_cln_b=v_a_cln_b, a_w_out=v_a_w_out, c_w_in=v_c_w_in,
                c_w_pool=v_c_w_pool, c_pool_scale=v_c_pool_scale, c_sln_g=v_c_sln_g, c_sln_b=v_c_sln_b, c_w_s=v_c_w_s,
                c_b_s=v_c_b_s, c_w_out=v_c_w_out, ffn_norm=v_ffn_norm, ffn_w_gate=v_ffn_w_gate, ffn_w_up=v_ffn_w_up,
                ffn_w_down=v_ffn_w_down, final_norm=v_final_norm)

    small = jnp.concatenate([a_conv_w[0].reshape(-1), c_pool_scale[0], c_sln_g[0], c_sln_b[0]])
    first_bits = lax.bitcast_convert_type(a_w_in[0].T.astype(BF16), jnp.uint16)
    gathered, tok = _all_gather(jnp.concatenate([first_bits, _f32_as_u16_rows(small, W_MISC_ROWS)], axis=0), "gather_mixer0")

    def ffn_shards(l):
        return [ffn_w_gate[l].T.astype(BF16), ffn_w_up[l].T.astype(BF16), ffn_w_down[l].astype(BF16)]

    ffn0_h, tok = _exchange_start(_behind([tok], ffn_shards(0) + [a_w_out[0].astype(BF16)]), False, "gather_ffn0_start")

    a_in_full = lax.bitcast_convert_type(gathered[:, :224].reshape(IN0, D), BF16)
    small_all = lax.bitcast_convert_type(
        gathered[:, 224:].reshape(N_DEV, -1)[:, :2 * SMALL_SHARD].reshape(N_DEV, SMALL_SHARD, 2), F32)
    conv_w = small_all[:, :31 * 64].reshape(N_DEV, 31, 64).transpose(1, 0, 2).reshape(31, 512)
    conv_w = jnp.pad(conv_w, ((0, HALO - CONV_K), (0, 0)))
    pool_scale = small_all[:, 31 * 64:31 * 64 + 64].reshape(1, 512)
    sln_g = small_all[:, 31 * 64 + 64:31 * 64 + 128].reshape(1, 512)
    sln_b = small_all[:, 31 * 64 + 128:].reshape(1, 512)

    wt_in0 = _in0_to_kernel(a_in_full, 0)
    b_in0 = _in0_to_kernel(a_b_in, 1)
    b_rows = jnp.broadcast_to(c_b_s[0][:, :, None], (4, 128, 128))
    conv_b, cln_g, cln_b = a_conv_b, a_cln_g, a_cln_b

    h0 = x.reshape(t, D)
    target = loss_target.reshape(t, D)
    z0, hn0 = _norm_proj(h0, _behind([tok], mix_norm[0:1]), wt_in0, b_in0, "in_proj0")
    attn, tok = _attn_fwd(z0, a_sinks, bsz, "attn_fwd")
    conv, conv_y, tok = _conv_fwd(z0, conv_w, conv_b, cln_g, _behind([tok], cln_b), bsz, "conv_fwd")
    mix1_h, tok = _exchange_start(_behind([tok], [c_w_in[0].T.astype(BF16), c_w_out[0].astype(BF16)]), False,
                                  "gather_mixer1_start")
    ffn1_h, tok = _exchange_start(_behind([tok], ffn_shards(1)), False, "gather_ffn1_start")
    wtg0, wtu0, wd0, a_out_full = (w.reshape(-1, D) for w in _exchange_wait(ffn0_h, tok, "gather_ffn0_wait"))
    w_out0 = _perm_heads(a_out_full, Q_PERM, 0)
    h1, h2, hnf0, gate0, up0 = _ffn_fwd(h0, attn, conv, w_out0, ffn_norm[0:1], wtg0, wtu0, wd0, "ffn_fwd0")
    wt_in1, w_out1 = (w.reshape(-1, D) for w in _exchange_wait(mix1_h, h2, "gather_mixer1_wait"))
    z1, hn1 = _norm_proj(h2, mix_norm[1:2], wt_in1, None, "in_proj1")
    pool, tok = _pool_fwd(z1, c_w_pool[0], pool_scale, bsz, "pool_fwd")
    sgu = _sgu_fwd(z1, sln_g, _behind([tok], sln_b), c_w_s[0], b_rows, "sgu_fwd")
    wtg1, wtu1, wd1 = (w.reshape(D_FF, D) for w in _exchange_wait(ffn1_h, sgu, "gather_ffn1_wait"))
    h3, dh4, hnf1, gate1, up1, d_final_norm, loss_part = _ffn_fwd(
        h2, pool, sgu, w_out1, ffn_norm[1:2], wtg1, wtu1, wd1, "ffn_fwd1", head=(final_norm.reshape(1, D), target))

    def blocks(g):
        return g.reshape(N_DEV, g.shape[0] // N_DEV, D)

    dh3, dmix1, dgate1, dup1, act1, d_fn1 = _ffn_bwd(dh4, h3, ffn_norm[1:2], gate1, up1, wtg1, wtu1, wd1, w_out1, "ffn_bwd1")
    gw_ffn1 = [_mm_tn(dgate1, hnf1, "dw_gate1"), _mm_tn(dup1, hnf1, "dw_up1"), _mm_tn(act1, dh4, "dw_down1")]
    ffn1_g, tok = _exchange_start([blocks(g) for g in gw_ffn1], True, "scatter_ffn1_start")
    gw_c_out = _mm_tn_pieces([pool, sgu], dh3, "dw_out1")
    dzp, d_w_pool, d_pool_scale = _pool_bwd(z1, dmix1, c_w_pool[0], _behind([tok], pool_scale), bsz, "pool_bwd")
    dzu, dzv, d_w_s, d_b_s, d_sln_g, d_sln_b = _sgu_bwd(z1, dmix1, sln_g, sln_b, c_w_s[0], b_rows, "sgu_bwd")
    dh2, d_mn1 = _proj_bwd_norm([(dzp, 0), (dzu, 512), (dzv, 1024)], wt_in1, h2, dh3, mix_norm[1:2], BF16, "in_proj1_bwd")
    gw_c_in = _mm_tn_pieces([dzp, dzu, dzv], hn1, "dw_in1")
    mix1_g, tok = _exchange_start([blocks(gw_c_in), blocks(gw_c_out)], True, "scatter_mixer1_start")
    dh1, dmix0, dgate0, dup0, act0, d_fn0 = _ffn_bwd(dh2, h1, _behind([tok], ffn_norm[0:1]), gate0, up0, wtg0, wtu0, wd0,
                                                      w_out0, "ffn_bwd0")
    gw_ffn0 = [_mm_tn(dgate0, hnf0, "dw_gate0"), _mm_tn(dup0, hnf0, "dw_up0"), _mm_tn(act0, dh2, "dw_down0")]
    ffn0_g, tok = _exchange_start([blocks(g) for g in gw_ffn0], True, "scatter_ffn0_start")
    gw_a_out = _perm_heads(_mm_tn_pieces([attn, conv], dh1, "dw_out0"), Q_INV, 0)
    dq, dkv, d_sink_row, d_bq, d_bkv = _attn_bwd(z0, dmix0, _behind([tok], a_sinks), bsz, "attn_bwd")
    dca, dcg, d_conv_w, d_conv_b, d_cln_g, d_cln_b, d_ba, d_bg = _conv_bwd(z0, conv_y, dmix0, conv_w, cln_g, cln_b, bsz, "conv_bwd")
    gw_a_in = _in0_from_kernel(_mm_tn_pieces([dq, dca, dcg, dkv], hn0, "dw_in0"), 0)
    mix0_g, tok = _exchange_start([blocks(gw_a_in), blocks(gw_a_out)], True, "scatter_mixer0_start")
    dx, d_mn0 = _proj_bwd_norm([(dq, 0), (dca, 512), (dcg, 1024), (dkv, 1536)], wt_in0, h0, dh1,
                               _behind([tok], mix_norm[0:1]), F32, "in_proj0_bwd")
    d_b_in = _in0_from_kernel(jnp.concatenate([d_bq, d_ba, d_bg, d_bkv], axis=1), 1)

    rep = dict(mix_norm=jnp.concatenate([d_mn0, d_mn1], axis=0), a_b_in=d_b_in, a_sinks=d_sink_row[:, :8],
               a_conv_b=d_conv_b, a_cln_g=d_cln_g, a_cln_b=d_cln_b, c_w_pool=d_w_pool[None], c_w_s=d_w_s[None],
               c_b_s=d_b_s[None], ffn_norm=jnp.concatenate([d_fn0, d_fn1], axis=0), final_norm=d_final_norm.reshape(D))
    rep_flat = jnp.concatenate([rep[nm].reshape(-1) for nm in REP_NAMES] + [loss_part.reshape(1)])
    rep_flat = jnp.pad(rep_flat, (0, N_DEV * REP_ROWS * D - rep_flat.shape[0])).reshape(N_DEV, REP_ROWS, D)
    small_g = jnp.concatenate([
        d_conv_w[:CONV_K].reshape(31, N_DEV, 64).transpose(1, 0, 2).reshape(N_DEV, 31 * 64),
        d_pool_scale.reshape(N_DEV, 64), d_sln_g.reshape(N_DEV, 64), d_sln_b.reshape(N_DEV, 64)], axis=1)
    small_g = jnp.pad(small_g, ((0, 0), (0, G_SMALL_ROWS * D - SMALL_SHARD))).reshape(N_DEV, G_SMALL_ROWS, D)
    tail_g, tok = _exchange_start([jnp.concatenate([small_g, rep_flat], axis=1)], True, "scatter_tail_start")

    names = list(w_in)
    g_out, delta, new_m, new_v = {}, {}, {}, {}
    column_sharded = ("a_w_in", "c_w_in", "ffn_w_gate", "ffn_w_up")

    def rows_of(a, nm):
        return jnp.swapaxes(a, 1, 2) if nm in column_sharded else a

    def reduce_adamw(nm, landing, layer, into=None):
        args = [rows_of(d[nm], nm) for d in (w_in, m_in, v_in)]
        if args[0].shape[0] == 1:
            args, layer = [a[0] for a in args], None
        return _reduce_adamw(landing, *args, "adamw_%s_%s" % (nm, layer), layer=layer, into=into)

    def keep(nm, res):
        res = [r if r.ndim == 3 else r[None] for r in res]
        g_out[nm], delta[nm], new_m[nm], new_v[nm] = (rows_of(r, nm) for r in res)

    ffn_names = ("ffn_w_gate", "ffn_w_up", "ffn_w_down")
    landed = _exchange_wait(ffn1_g, tok, "scatter_ffn1_wait")
    ffn_res = [reduce_adamw(nm, a, 1) for nm, a in zip(ffn_names, landed)]
    landed = _exchange_wait(mix1_g, ffn_res[-1][0], "scatter_mixer1_wait")
    for nm, a in zip(("c_w_in", "c_w_out"), landed):
        keep(nm, reduce_adamw(nm, a, 0))
    landed = _exchange_wait(ffn0_g, g_out["c_w_out"], "scatter_ffn0_wait")
    for nm, a, res in zip(ffn_names, landed, ffn_res):
        keep(nm, reduce_adamw(nm, a, 0, into=res))
    landed = _exchange_wait(mix0_g, g_out["ffn_w_down"], "scatter_mixer0_wait")
    for nm, a in zip(("a_w_in", "a_w_out"), landed):
        keep(nm, reduce_adamw(nm, a, 0))
    g_tail = _sum_slabs(_exchange_wait(tail_g, g_out["a_w_out"], "scatter_tail_wait")[0], "sum_tail")
    rep_all = _all_gather(g_tail[G_SMALL_ROWS:], "gather_replicated_grads")[0].reshape(-1)
    small_r = g_tail[:G_SMALL_ROWS].reshape(-1)[:SMALL_SHARD]
    g_out.update(
        a_conv_w=small_r[:31 * 64].reshape(1, 31, 64), c_pool_scale=small_r[31 * 64:31 * 64 + 64].reshape(1, 64),
        c_sln_g=small_r[31 * 64 + 64:31 * 64 + 128].reshape(1, 64), c_sln_b=small_r[31 * 64 + 128:].reshape(1, 64))
    off = 0
    for nm in REP_NAMES:
        n = int(np.prod(w_in[nm].shape))
        g_out[nm] = rep_all[off:off + n].reshape(w_in[nm].shape)
        off += n
    loss = rep_all[off]
    for group, rows, label in ((("a_conv_w", "c_pool_scale", "c_sln_g", "c_sln_b"), G_SMALL_ROWS, "adamw_small_sharded"),
                               (REP_NAMES, N_DEV * REP_ROWS, "adamw_replicated")):
        flat = [_pad_rows(jnp.concatenate([d[nm].reshape(-1) for nm in group]), rows) for d in (w_in, g_out, m_in, v_in)]
        res = [r.reshape(-1) for r in _adamw(*flat, label)]
        off = 0
        for nm in group:
            n = int(np.prod(w_in[nm].shape))
            delta[nm], new_m[nm], new_v[nm] = (r[off:off + n].reshape(w_in[nm].shape) for r in res)
            off += n

    grad_x = dx.reshape(bsz, seq, D)
    return (loss, grad_x, *[g_out[nm] for nm in names], *[delta[nm] for nm in names],
            *[new_m[nm] for nm in names], *[new_v[nm] for nm in names])
```

```python
import functools

import jax
import jax.numpy as jnp
import numpy as np
from jax import lax
from jax.experimental import pallas as pl
from jax.experimental.pallas import tpu as pltpu

F32 = jnp.float32
BF16 = jnp.bfloat16
MESH = pl.DeviceIdType.MESH

D = 1024
N_DEV = 8
EPS = 1e-5
HEAD_PAIRS = 4
ATT_BLK = 128
CONV_K = 31
HALO = 32
D_FF = 2816
FF_TILE_FWD = D_FF // 2
FF_TILE_BWD = D_FF // 2
IN0 = 1792
IN1 = 1536
POOL_WINDOWS = (2, 4, 8, 16)
SGU_CHUNK = 128
GELU_C = 0.7978845608028654
GELU_A = 0.044715
ADAM_LR, ADAM_B1, ADAM_B2, ADAM_EPS, ADAM_WD, ADAM_STEP = 0.001, 0.9, 0.999, 1e-08, 0.01, 10
VMEM_LIMIT = 56 << 20

SMALL_SHARD = 31 * 64 + 3 * 64
W_MISC_ROWS = 16
G_MISC_ROWS = 32
G_SMALL_ROWS = 8
REP_ROWS = G_MISC_ROWS - G_SMALL_ROWS
REP_2D = (("c_w_pool", (64, 1024)), ("c_w_s", (64, 1024)), ("mix_norm", (2, 1024)), ("a_b_in", (1, 1792)), ("a_sinks", (1, 8)),
          ("a_conv_b", (1, 512)), ("a_cln_g", (1, 512)), ("a_cln_b", (1, 512)), ("c_b_s", (4, 128)), ("ffn_norm", (2, 1024)),
          ("final_norm", (1, 1024)))
Q_PERM = (0, 4, 1, 5, 2, 6, 3, 7)


def _params(*sem):
    return pltpu.CompilerParams(dimension_semantics=sem, vmem_limit_bytes=VMEM_LIMIT)


def _nn(a, b):
    return jnp.dot(a, b, preferred_element_type=F32)


def _nt(a, b):
    return lax.dot_general(a, b, (((1,), (1,)), ((), ())), preferred_element_type=F32)


def _tn(a, b):
    return lax.dot_general(a, b, (((0,), (0,)), ((), ())), preferred_element_type=F32)


def _tile(n, want=512):
    t = min(want, n)
    assert n % t == 0, (n, t)
    return t


def _seq_tile(s):
    return 512 if s >= 1024 else s // 2


def _rms(x, g):
    r = lax.rsqrt(jnp.mean(x * x, axis=-1, keepdims=True) + EPS)
    return x * r * g, r


def _rms_bwd(x, g, d_y):
    r = lax.rsqrt(jnp.mean(x * x, axis=-1, keepdims=True) + EPS)
    xr = x * r
    u = d_y * g
    d_x = r * (u - xr * jnp.mean(u * xr, axis=-1, keepdims=True))
    return d_x, jnp.sum(d_y * xr, axis=0, keepdims=True)


def _ln(y, g, b):
    mu = jnp.mean(y, axis=-1, keepdims=True)
    yc = y - mu
    rstd = lax.rsqrt(jnp.mean(yc * yc, axis=-1, keepdims=True) + EPS)
    xhat = yc * rstd
    return xhat * g + b, xhat, rstd


def _ln_bwd(d_o, xhat, rstd, g):
    dxh = d_o * g
    return rstd * (dxh - jnp.mean(dxh, axis=-1, keepdims=True) - xhat * jnp.mean(dxh * xhat, axis=-1, keepdims=True))


def _gelu(x):
    th = jnp.tanh(GELU_C * (x + GELU_A * x * x * x))
    return 0.5 * x * (1.0 + th), th


def _gelu_grad(x, th):
    return 0.5 * (1.0 + th) + 0.5 * x * (1.0 - th * th) * GELU_C * (1.0 + 3.0 * GELU_A * x * x)


def _row(c):
    return pl.BlockSpec((1, c), lambda *_: (0, 0))


def _full(shape):
    return pl.BlockSpec(shape, lambda *_: (0,) * len(shape))


def _norm_proj(h, g, wt, bias, name):
    t, n = h.shape[0], wt.shape[0]
    tm = _tile(t)
    has_bias = bias is not None

    def body(*refs):
        h_ref, g_ref, wt_ref = refs[:3]
        z_ref, hn_ref = refs[-2:]
        hn = _rms(h_ref[...].astype(F32), g_ref[...])[0].astype(BF16)
        hn_ref[...] = hn
        z = _nt(hn, wt_ref[...])
        if has_bias:
            z = z + refs[3][...]
        z_ref[...] = z.astype(BF16)

    in_specs = [pl.BlockSpec((tm, D), lambda i: (i, 0)), _row(D), _full((n, D))]
    args = [h, g, wt]
    if has_bias:
        in_specs.append(_row(n))
        args.append(bias)
    return pl.pallas_call(
        body, name=name, grid=(t // tm,), in_specs=in_specs,
        out_specs=[pl.BlockSpec((tm, n), lambda i: (i, 0)), pl.BlockSpec((tm, D), lambda i: (i, 0))],
        out_shape=[jax.ShapeDtypeStruct((t, n), BF16), jax.ShapeDtypeStruct((t, D), BF16)],
        compiler_params=_params("parallel"))(*args)


def _ff_pieces(tf, width=256):
    return [(c0, min(width, tf - c0)) for c0 in range(0, tf, width)]


def _ffn_fwd(h_prev, a, b, w_out, g, wtg, wtu, wd, name, head=None):
    t = h_prev.shape[0]
    tm, tf = _tile(t), FF_TILE_FWD
    nf = D_FF // tf
    n_head = 0 if head is None else 2

    def body(*refs):
        hp_ref, a_ref, b_ref, wa_ref, wb_ref, g_ref, wtg_ref, wtu_ref, wd_ref = refs[:9]
        hmid_ref, o_ref, hn_ref, gate_ref, up_ref = refs[9 + n_head:14 + n_head]
        acc, act = refs[-2:]
        i, f = pl.program_id(0), pl.program_id(1)

        if head is not None:
            fg_ref, t_ref = refs[9:11]
            dfg_ref, loss_ref = refs[16:18]

            @pl.when((i == 0) & (f == 0))
            def _():
                dfg_ref[...] = jnp.zeros_like(dfg_ref)
                loss_ref[...] = jnp.zeros_like(loss_ref)

        @pl.when(f == 0)
        def _():
            x = hp_ref[...].astype(F32) + _nn(a_ref[...], wa_ref[...]) + _nn(b_ref[...], wb_ref[...])
            hmid_ref[...] = x.astype(BF16)
            hn_ref[...] = _rms(x, g_ref[...])[0].astype(BF16)
            acc[...] = x

        hn = hn_ref[...]
        for c0, cw in _ff_pieces(tf):
            rows, cols = slice(c0, c0 + cw), slice(c0, c0 + cw)
            gate = _nt(hn, wtg_ref[rows, :])
            up = _nt(hn, wtu_ref[rows, :])
            gate_ref[:, cols] = gate.astype(BF16)
            up_ref[:, cols] = up.astype(BF16)
            act[:, cols] = (gate * jax.nn.sigmoid(gate) * up).astype(BF16)
        acc[...] += _nn(act[...], wd_ref[...])

        @pl.when(f == nf - 1)
        def _():
            if head is None:
                o_ref[...] = acc[...].astype(BF16)
            else:
                x, gv = acc[...], fg_ref[...]
                err = _rms(x, gv)[0] - t_ref[...]
                loss_ref[...] += 0.5 * jnp.sum(jnp.mean(err * err, axis=-1, keepdims=True), axis=0, keepdims=True)
                d_x, d_g = _rms_bwd(x, gv, err * (1.0 / D))
                o_ref[...] = d_x.astype(BF16)
                dfg_ref[...] += d_g

    tok = pl.BlockSpec((tm, D), lambda i, f: (i, 0))
    wsp = pl.BlockSpec((tf, D), lambda i, f: (f, 0))
    mid = pl.BlockSpec((tm, tf), lambda i, f: (i, f))
    half = pl.BlockSpec((tm, 512), lambda i, f: (i, 0))
    res = jax.ShapeDtypeStruct((t, D), BF16)
    in_specs = [tok, half, half, pl.BlockSpec((512, D), lambda i, f: (0, 0)), pl.BlockSpec((512, D), lambda i, f: (1, 0)),
                _row(D), wsp, wsp, wsp]
    out_specs = [tok, tok, tok, mid, mid]
    out_shape = [res, res, res, jax.ShapeDtypeStruct((t, D_FF), BF16), jax.ShapeDtypeStruct((t, D_FF), BF16)]
    if head is not None:
        in_specs += [_row(D), tok]
        out_specs += [_row(D), _row(1)]
        out_shape += [jax.ShapeDtypeStruct((1, D), F32), jax.ShapeDtypeStruct((1, 1), F32)]
    return pl.pallas_call(
        body, name=name, grid=(t // tm, nf), in_specs=in_specs, out_specs=out_specs, out_shape=out_shape,
        scratch_shapes=[pltpu.VMEM((tm, D), F32), pltpu.VMEM((tm, tf), BF16)],
        compiler_params=_params("parallel" if head is None else "arbitrary", "arbitrary"),
    )(h_prev, a, b, w_out, w_out, g, wtg, wtu, wd, *(head or ()))


def _ffn_bwd(dh, h, g, gate, up, wtg, wtu, wd, w_out, name):
    t = h.shape[0]
    tm, tf = _tile(t), FF_TILE_BWD
    nf = D_FF // tf

    def body(dh_ref, h_ref, g_ref, gate_ref, up_ref, wtg_ref, wtu_ref, wd_ref, wout_ref,
             dhin_ref, dmix_ref, dgate_ref, dup_ref, act_ref, dg_ref, d_hn):
        i, f = pl.program_id(0), pl.program_id(1)

        @pl.when(f == 0)
        def _():
            d_hn[...] = jnp.zeros_like(d_hn)

        @pl.when((i == 0) & (f == 0))
        def _():
            dg_ref[...] = jnp.zeros_like(dg_ref)

        dh = dh_ref[...]
        for c0, cw in _ff_pieces(tf):
            cols = slice(c0, c0 + cw)
            da = _nt(dh, wd_ref[c0:c0 + cw, :])
            gt = gate_ref[:, cols].astype(F32)
            u = up_ref[:, cols].astype(F32)
            sg = jax.nn.sigmoid(gt)
            sil = gt * sg
            act_ref[:, cols] = (sil * u).astype(BF16)
            dup_ref[:, cols] = (da * sil).astype(BF16)
            dgate_ref[:, cols] = (da * u * sg * (1.0 + gt * (1.0 - sg))).astype(BF16)
        d_hn[...] += _nn(dgate_ref[...], wtg_ref[...]) + _nn(dup_ref[...], wtu_ref[...])

        @pl.when(f == nf - 1)
        def _():
            d_x, d_g = _rms_bwd(h_ref[...].astype(F32), g_ref[...], d_hn[...])
            dhin = (dh_ref[...].astype(F32) + d_x).astype(BF16)
            dhin_ref[...] = dhin
            dmix_ref[...] = _nt(dhin, wout_ref[...]).astype(BF16)
            dg_ref[...] += d_g

    tok = pl.BlockSpec((tm, D), lambda i, f: (i, 0))
    wsp = pl.BlockSpec((tf, D), lambda i, f: (f, 0))
    mid = pl.BlockSpec((tm, tf), lambda i, f: (i, f))
    mid_shape = jax.ShapeDtypeStruct((t, D_FF), BF16)
    res = jax.ShapeDtypeStruct((t, D), BF16)
    return pl.pallas_call(
        body, name=name, grid=(t // tm, nf), in_specs=[tok, tok, _row(D), mid, mid, wsp, wsp, wsp, _full((D, D))],
        out_specs=[tok, tok, mid, mid, mid, _row(D)],
        out_shape=[res, res, mid_shape, mid_shape, mid_shape, jax.ShapeDtypeStruct((1, D), F32)],
        scratch_shapes=[pltpu.VMEM((tm, D), F32)],
        compiler_params=_params("arbitrary", "arbitrary"))(dh, h, g, gate, up, wtg, wtu, wd, w_out)


def _proj_bwd_norm(pieces, wt, h, dh, g, dtype, name):
    t = h.shape[0]
    tm = _tile(t)
    n_p = len(pieces)

    def body(*refs):
        p_refs, w_refs = refs[:n_p], refs[n_p:2 * n_p]
        h_ref, dh_ref, g_ref, o_ref, dg_ref = refs[2 * n_p:]

        @pl.when(pl.program_id(0) == 0)
        def _():
            dg_ref[...] = jnp.zeros_like(dg_ref)

        d_hn = _nn(p_refs[0][...], w_refs[0][...])
        for p_ref, w_ref in zip(p_refs[1:], w_refs[1:]):
            d_hn = d_hn + _nn(p_ref[...], w_ref[...])
        d_x, d_g = _rms_bwd(h_ref[...].astype(F32), g_ref[...], d_hn)
        o_ref[...] = (dh_ref[...].astype(F32) + d_x).astype(dtype)
        dg_ref[...] += d_g

    tok = pl.BlockSpec((tm, D), lambda i: (i, 0))
    in_specs = [pl.BlockSpec((tm, a.shape[1]), lambda i: (i, 0)) for a, _ in pieces]
    for a, off in pieces:
        w = a.shape[1]
        assert off % w == 0
        in_specs.append(pl.BlockSpec((w, D), functools.partial(lambda i, blk: (blk, 0), blk=off // w)))
    in_specs += [tok, tok, _row(D)]
    return pl.pallas_call(
        body, name=name, grid=(t // tm,), in_specs=in_specs, out_specs=[tok, _row(D)],
        out_shape=[jax.ShapeDtypeStruct((t, D), dtype), jax.ShapeDtypeStruct((1, D), F32)],
        compiler_params=_params("arbitrary"))(*[a for a, _ in pieces], *([wt] * n_p), h, dh, g)


def _mm_tn(a, b, name):
    t, n = a.shape
    k = b.shape[1]
    tn = n if n <= 1024 else n // 2
    tt = _tile(t, 1024)
    nt = t // tt

    def body(a_ref, b_ref, o_ref, acc):
        s = pl.program_id(1)

        @pl.when(s == 0)
        def _():
            acc[...] = jnp.zeros_like(acc)

        acc[...] += _tn(a_ref[...], b_ref[...].astype(BF16))

        @pl.when(s == nt - 1)
        def _():
            o_ref[...] = acc[...].astype(BF16)

    return pl.pallas_call(
        body, name=name, grid=(n // tn, nt),
        in_specs=[pl.BlockSpec((tt, tn), lambda j, s: (s, j)), pl.BlockSpec((tt, k), lambda j, s: (s, 0))],
        out_specs=pl.BlockSpec((tn, k), lambda j, s: (j, 0)), out_shape=jax.ShapeDtypeStruct((n, k), BF16),
        scratch_shapes=[pltpu.VMEM((tn, k), F32)],
        compiler_params=_params("parallel", "arbitrary"))(a, b)


def _mm_tn_pieces(pieces, b, name):
    t, k = b.shape
    widths = [p.shape[1] for p in pieces]
    n, n_p = sum(widths), len(pieces)
    tt = _tile(t, 1024)
    nt = t // tt

    def body(*refs):
        b_ref, o_ref, acc = refs[n_p:]
        s = pl.program_id(0)

        @pl.when(s == 0)
        def _():
            acc[...] = jnp.zeros_like(acc)

        bb = b_ref[...].astype(BF16)
        off = 0
        for p_ref, w in zip(refs[:n_p], widths):
            acc[off:off + w, :] += _tn(p_ref[...], bb)
            off += w

        @pl.when(s == nt - 1)
        def _():
            o_ref[...] = acc[...].astype(BF16)

    return pl.pallas_call(
        body, name=name, grid=(nt,),
        in_specs=[pl.BlockSpec((tt, w), lambda s: (s, 0)) for w in widths] + [pl.BlockSpec((tt, k), lambda s: (s, 0))],
        out_specs=_full((n, k)), out_shape=jax.ShapeDtypeStruct((n, k), BF16),
        scratch_shapes=[pltpu.VMEM((n, k), F32)], compiler_params=_params("arbitrary"))(*pieces, b)


STACK = HEAD_PAIRS * ATT_BLK


def _attn_valid(first, rows):
    qi = lax.broadcasted_iota(jnp.int32, (rows, 2 * ATT_BLK), 0) % ATT_BLK
    r = lax.broadcasted_iota(jnp.int32, (rows, 2 * ATT_BLK), 1)
    dist = qi + ATT_BLK - r
    return (dist >= 0) & (dist < ATT_BLK) & ((r >= ATT_BLK) | jnp.logical_not(first))


def _stacked(ref, kh, scale):
    lo = lax.broadcasted_iota(jnp.int32, (ATT_BLK, 128), 1) < 64
    keep = lo if kh == 0 else ~lo
    parts = [jnp.where(keep, ref[:, g * 128:(g + 1) * 128] * scale, 0.0).astype(BF16) for g in range(HEAD_PAIRS)]
    return jnp.concatenate(parts, axis=0)


def _unstacked(a0, a1, g):
    lo = lax.broadcasted_iota(jnp.int32, (ATT_BLK, 128), 1) < 64
    rows = slice(g * ATT_BLK, (g + 1) * ATT_BLK)
    return jnp.where(lo, a0[rows], a1[rows])


def _sink_rows(s_ref, kh):
    return jnp.concatenate([jnp.full((ATT_BLK, 128), s_ref[0, kh * 4 + g], F32) for g in range(HEAD_PAIRS)], axis=0)


def _row_sums(a, split):
    hi = a.astype(BF16)
    ones = jnp.ones((2 * ATT_BLK, 128), BF16)
    if not split:
        return _nn(hi, ones)
    lo = (a - hi.astype(F32)).astype(BF16)
    return _nn(hi, ones) + _nn(lo, ones)


def _both(a):
    return jnp.concatenate([a, a], axis=1)


def _attn_probs(qs, kpair, sink, valid):
    s = jnp.where(valid, _nt(qs, kpair), -1e30)
    m = jnp.maximum(jnp.broadcast_to(jnp.max(s, axis=-1, keepdims=True), (s.shape[0], 128)), sink)
    p = jnp.exp(s - _both(m))
    es = jnp.exp(sink - m)
    inv = 1.0 / (_row_sums(p, split=True) + es)
    return p * _both(inv), es * inv


def _attn_probs_head(qm, kpair, sink, valid):
    s = jnp.where(valid, _nt(qm, kpair), -1e30)
    m = jnp.maximum(jnp.max(s, axis=-1, keepdims=True), sink)
    p = jnp.exp(s - m)
    return p * (1.0 / (jnp.sum(p, axis=-1, keepdims=True) + jnp.exp(sink - m)))


def _attn_specs(bsz, order):
    q = pl.BlockSpec((bsz, ATT_BLK, 512), lambda j: (0, order(j), 0))
    kvc = pl.BlockSpec((bsz, ATT_BLK, 256), lambda j: (0, order(j), 6))
    kvp = pl.BlockSpec((bsz, ATT_BLK, 256), lambda j: (0, jnp.maximum(order(j) - 1, 0), 6))
    return q, kvc, kvp


def _window_kv(kvc_ref, kvp_ref):
    kvc, kvp = kvc_ref[...], kvp_ref[...]
    kpair = jnp.concatenate([kvp[:, :128], kvc[:, :128]], axis=0)
    vpair = jnp.concatenate([kvp[:, 128:], kvc[:, 128:]], axis=0)
    return kpair, vpair


def _attn_fwd(z0, sinks, bsz, name):
    t = z0.shape[0]
    seq = t // bsz
    nb = seq // ATT_BLK

    def body(s_ref, q_ref, kvc_ref, kvp_ref, o_ref, token):
        token[...] = jnp.zeros_like(token)
        valid = _attn_valid(pl.program_id(0) == 0, ATT_BLK)
        lo = lax.broadcasted_iota(jnp.int32, (ATT_BLK, 128), 1) < 64
        for b in range(bsz):
            kpair, vpair = _window_kv(kvc_ref.at[b], kvp_ref.at[b])
            for g in range(HEAD_PAIRS):
                qs = q_ref[b, :, g * 128:(g + 1) * 128] * 0.125
                outs = []
                for kh in range(2):
                    qm = jnp.where(lo if kh == 0 else ~lo, qs, 0.0).astype(BF16)
                    p = _attn_probs_head(qm, kpair, s_ref[0, kh * 4 + g], valid)
                    outs.append(_nn(p.astype(BF16), vpair))
                o_ref[b, :, g * 128:(g + 1) * 128] = jnp.where(lo, outs[0], outs[1]).astype(BF16)

    q, kvc, kvp = _attn_specs(bsz, lambda j: j)
    z3 = z0.reshape(bsz, seq, z0.shape[1])
    out, token = pl.pallas_call(
        body, name=name, grid=(nb,),
        in_specs=[pl.BlockSpec(memory_space=pltpu.SMEM), q, kvc, kvp],
        out_specs=[pl.BlockSpec((bsz, ATT_BLK, 512), lambda j: (0, j, 0)), _full((8, 128))],
        out_shape=[jax.ShapeDtypeStruct((bsz, seq, 512), BF16), jax.ShapeDtypeStruct((8, 128), F32)],
        compiler_params=_params("arbitrary"))(sinks, z3, z3, z3)
    return out.reshape(t, 512), token


def _attn_bwd(z0, dmix, sinks, bsz, name):
    t = z0.shape[0]
    seq = t // bsz
    nb = seq // ATT_BLK

    def body(s_ref, q_ref, kvc_ref, kvp_ref, do_ref, dq_ref, dkv_ref, dsink_ref, dbq_ref, dbkv_ref, carry):
        j = pl.program_id(0)

        @pl.when(j == 0)
        def _():
            carry[...] = jnp.zeros_like(carry)
            dsink_ref[...] = jnp.zeros_like(dsink_ref)
            dbq_ref[...] = jnp.zeros_like(dbq_ref)
            dbkv_ref[...] = jnp.zeros_like(dbkv_ref)

        valid = _attn_valid(j == nb - 1, STACK)
        lane = lax.broadcasted_iota(jnp.int32, (1, 128), 1)
        dsink = jnp.zeros((1, 128), F32)
        dbq = [jnp.zeros((1, 128), F32)] * HEAD_PAIRS
        dbkv = jnp.zeros((1, 256), F32)
        for b in range(bsz):
            kpair, vpair = _window_kv(kvc_ref.at[b], kvp_ref.at[b])
            dk = jnp.zeros((2 * ATT_BLK, 128), F32)
            dv = jnp.zeros((2 * ATT_BLK, 128), F32)
            dqs = []
            for kh in range(2):
                qs = _stacked(q_ref.at[b], kh, 0.125)
                dos = _stacked(do_ref.at[b], kh, 1.0)
                p, ps = _attn_probs(qs, kpair, _sink_rows(s_ref, kh), valid)
                dp = _nt(dos, vpair)
                delta = _row_sums(p * dp, split=False)
                ds = (p * (dp - _both(delta))).astype(BF16)
                dqs.append(_nn(ds, kpair))
                dk = dk + _tn(ds, qs)
                dv = dv + _tn(p.astype(BF16), dos)
                psd = ps * delta
                for g in range(HEAD_PAIRS):
                    part = jnp.sum(psd[g * ATT_BLK:(g + 1) * ATT_BLK], axis=0, keepdims=True)
                    dsink = dsink - jnp.where(lane == kh * 4 + g, part, 0.0)
            for g in range(HEAD_PAIRS):
                dq = _unstacked(dqs[0], dqs[1], g) * 0.125
                dq_ref[b, :, g * 128:(g + 1) * 128] = dq.astype(BF16)
                dbq[g] = dbq[g] + jnp.sum(dq, axis=0, keepdims=True)
            dkv = jnp.concatenate([dk[ATT_BLK:], dv[ATT_BLK:]], axis=1) + carry[b]
            dkv_ref[b] = dkv.astype(BF16)
            dbkv = dbkv + jnp.sum(dkv, axis=0, keepdims=True)
            carry[b] = jnp.concatenate([dk[:ATT_BLK], dv[:ATT_BLK]], axis=1)
        dsink_ref[...] += dsink
        dbq_ref[...] += jnp.concatenate(dbq, axis=1)
        dbkv_ref[...] += dbkv

    q, kvc, kvp = _attn_specs(bsz, lambda j: nb - 1 - j)
    z3 = z0.reshape(bsz, seq, z0.shape[1])
    d3 = dmix.reshape(bsz, seq, dmix.shape[1])
    dq, dkv, dsink, dbq, dbkv = pl.pallas_call(
        body, name=name, grid=(nb,),
        in_specs=[pl.BlockSpec(memory_space=pltpu.SMEM), q, kvc, kvp,
                  pl.BlockSpec((bsz, ATT_BLK, 512), lambda j: (0, nb - 1 - j, 0))],
        out_specs=[pl.BlockSpec((bsz, ATT_BLK, 512), lambda j: (0, nb - 1 - j, 0)),
                   pl.BlockSpec((bsz, ATT_BLK, 256), lambda j: (0, nb - 1 - j, 0)), _row(128), _row(512), _row(256)],
        out_shape=[jax.ShapeDtypeStruct((bsz, seq, 512), BF16), jax.ShapeDtypeStruct((bsz, seq, 256), BF16),
                   jax.ShapeDtypeStruct((1, 128), F32), jax.ShapeDtypeStruct((1, 512), F32),
                   jax.ShapeDtypeStruct((1, 256), F32)],
        scratch_shapes=[pltpu.VMEM((bsz, ATT_BLK, 256), F32)],
        compiler_params=_params("arbitrary"))(sinks, z3, z3, z3, d3)
    return dq.reshape(t, 512), dkv.reshape(t, 256), dsink, dbq, dbkv


def _seq_specs(ts, nt, t, width, col):
    per = ts // HALO
    cur = pl.BlockSpec((ts, width), lambda b, i: (b * nt + i, col))
    prev = pl.BlockSpec((HALO, width), lambda b, i: (jnp.maximum((b * nt + i) * per - 1, 0), col))
    nxt = pl.BlockSpec((HALO, width), lambda b, i: (jnp.minimum((b * nt + i + 1) * per, t // HALO - 1), col))
    return prev, cur, nxt


SUB = 8
CONV_ROWS = 64


def _shifted_copies(src, sh, rows_first, rows_rest):
    for r in range(SUB):
        rows = rows_first if r == 0 else rows_rest
        sh[r, pl.ds(0, rows), :] = src[pl.ds(r, rows), :]


def _tap_sum(sh, w, offset, c0, rows):
    acc = None
    for k in range(CONV_K):
        o = offset(k)
        term = sh[o % SUB, pl.ds(c0 + o - o % SUB, rows), :] * w[k:k + 1, :]
        acc = term if acc is None else acc + term
    return acc


def _glu_rows(a_ref, g_ref, rows=slice(None)):
    return a_ref[rows, :].astype(F32) * jax.nn.sigmoid(g_ref[rows, :].astype(F32))


def _conv_fwd(z0, conv_w, conv_b, ln_g, ln_b, bsz, name):
    t = z0.shape[0]
    s = t // bsz
    ts = _seq_tile(s)
    nt = s // ts
    first = HALO - (CONV_K - 1)

    def body(ap_ref, ac_ref, gp_ref, gc_ref, w_ref, cb_ref, lg_ref, lb_ref, o_ref, y_ref, hbuf, sh):
        hbuf[0:HALO, :] = jnp.where(pl.program_id(1) > 0, _glu_rows(ap_ref, gp_ref), 0.0)
        hbuf[HALO:HALO + ts, :] = _glu_rows(ac_ref, gc_ref)
        _shifted_copies(hbuf, sh, ts + HALO, ts + HALO - SUB)
        w, cb, lg, lb = w_ref[...], cb_ref[...], lg_ref[...], lb_ref[...]
        for c0 in range(0, ts, CONV_ROWS):
            y = _tap_sum(sh, w, lambda k: first + k, c0, CONV_ROWS) + cb
            y_ref[c0:c0 + CONV_ROWS, :] = y
            o = _ln(y, lg, lb)[0]
            o_ref[c0:c0 + CONV_ROWS, :] = (o * jax.nn.sigmoid(o)).astype(BF16)

    ap, ac, _ = _seq_specs(ts, nt, t, 512, 1)
    gp, gc, _ = _seq_specs(ts, nt, t, 512, 2)
    tile = pl.BlockSpec((ts, 512), lambda b, i: (b * nt + i, 0))
    return pl.pallas_call(
        body, name=name, grid=(bsz, nt),
        in_specs=[ap, ac, gp, gc, _full((HALO, 512)), _row(512), _row(512), _row(512)],
        out_specs=[tile, tile],
        out_shape=[jax.ShapeDtypeStruct((t, 512), BF16), jax.ShapeDtypeStruct((t, 512), F32)],
        scratch_shapes=[pltpu.VMEM((HALO + ts, 512), F32), pltpu.VMEM((SUB, HALO + ts, 512), F32)],
        compiler_params=_params("parallel", "parallel"))(z0, z0, z0, z0, conv_w, conv_b, ln_g, ln_b)


def _conv_bwd(z0, y, dmix, conv_w, ln_g, ln_b, bsz, name):
    t = z0.shape[0]
    s = t // bsz
    ts = _seq_tile(s)
    nt = s // ts

    def body(ac_ref, gc_ref, yc_ref, yn_ref, dc_ref, dn_ref, w_ref, lg_ref, lb_ref,
             da_ref, dg_ref, dw_ref, dcb_ref, dlg_ref, dlb_ref, dba_ref, dbg_ref, hcur, dybuf, sh_dy):
        b, i = pl.program_id(0), pl.program_id(1)

        @pl.when((b == 0) & (i == 0))
        def _():
            for ref in (dw_ref, dcb_ref, dlg_ref, dlb_ref, dba_ref, dbg_ref):
                ref[...] = jnp.zeros_like(ref)

        w, lg, lb = w_ref[...], lg_ref[...], lb_ref[...]
        hcur[...] = _glu_rows(ac_ref, gc_ref)

        def d_conv_out(yv, dout):
            o, xhat, rstd = _ln(yv, lg, lb)
            sg_o = jax.nn.sigmoid(o)
            d_o = dout * sg_o * (1.0 + o * (1.0 - sg_o))
            return _ln_bwd(d_o, xhat, rstd, lg), d_o * xhat, d_o

        dlg = jnp.zeros((1, 512), F32)
        dlb = jnp.zeros((1, 512), F32)
        dcb = jnp.zeros((1, 512), F32)
        for c0 in range(0, ts, CONV_ROWS):
            rows = slice(c0, c0 + CONV_ROWS)
            dy, g_part, b_part = d_conv_out(yc_ref[rows, :], dc_ref[rows, :].astype(F32))
            dybuf[rows, :] = dy
            dlg = dlg + jnp.sum(g_part, axis=0, keepdims=True)
            dlb = dlb + jnp.sum(b_part, axis=0, keepdims=True)
            dcb = dcb + jnp.sum(dy, axis=0, keepdims=True)
        dn = jnp.where(i < nt - 1, dn_ref[...].astype(F32), 0.0)
        dybuf[ts:ts + HALO, :] = d_conv_out(yn_ref[...], dn)[0]
        dlg_ref[...] += dlg
        dlb_ref[...] += dlb
        dcb_ref[...] += dcb
        _shifted_copies(dybuf, sh_dy, ts + HALO - SUB, ts + HALO - SUB)

        for k in range(CONV_K):
            o = CONV_K - 1 - k
            prod = hcur[...] * sh_dy[o % SUB, pl.ds(o - o % SUB, ts), :]
            dw_ref[pl.ds(k, 1), :] += jnp.sum(prod, axis=0, keepdims=True)
        dba = jnp.zeros((1, 512), F32)
        dbg = jnp.zeros((1, 512), F32)
        for c0 in range(0, ts, CONV_ROWS):
            rows = slice(c0, c0 + CONV_ROWS)
            dh = _tap_sum(sh_dy, w, lambda k: CONV_K - 1 - k, c0, CONV_ROWS)
            a_c = ac_ref[rows, :].astype(F32)
            sg_c = jax.nn.sigmoid(gc_ref[rows, :].astype(F32))
            d_a = dh * sg_c
            d_g = dh * a_c * sg_c * (1.0 - sg_c)
            da_ref[rows, :] = d_a.astype(BF16)
            dg_ref[rows, :] = d_g.astype(BF16)
            dba = dba + jnp.sum(d_a, axis=0, keepdims=True)
            dbg = dbg + jnp.sum(d_g, axis=0, keepdims=True)
        dba_ref[...] += dba
        dbg_ref[...] += dbg

    _, ac, _ = _seq_specs(ts, nt, t, 512, 1)
    _, gc, _ = _seq_specs(ts, nt, t, 512, 2)
    _, yc, yn = _seq_specs(ts, nt, t, 512, 0)
    _, dc, dn = _seq_specs(ts, nt, t, 512, 1)
    tile = pl.BlockSpec((ts, 512), lambda b, i: (b * nt + i, 0))
    vec = jax.ShapeDtypeStruct((1, 512), F32)
    return pl.pallas_call(
        body, name=name, grid=(bsz, nt),
        in_specs=[ac, gc, yc, yn, dc, dn, _full((HALO, 512)), _row(512), _row(512)],
        out_specs=[tile, tile, _full((HALO, 512)), _row(512), _row(512), _row(512), _row(512), _row(512)],
        out_shape=[jax.ShapeDtypeStruct((t, 512), BF16), jax.ShapeDtypeStruct((t, 512), BF16),
                   jax.ShapeDtypeStruct((HALO, 512), F32), vec, vec, vec, vec, vec],
        scratch_shapes=[pltpu.VMEM((ts, 512), F32), pltpu.VMEM((ts + HALO, 512), F32),
                        pltpu.VMEM((SUB, HALO + ts, 512), F32)],
        compiler_params=_params("arbitrary", "arbitrary"))(z0, z0, y, y, dmix, dmix, conv_w, ln_g, ln_b)


def _pooled(pbuf, g, ts, tok):
    w = 2 << g
    cols = slice(128 * g, 128 * (g + 1))
    sm = pbuf[pl.ds(HALO, ts), cols]
    for d in range(1, w):
        sm = sm + pbuf[pl.ds(HALO - d, ts), cols]
    cnt = jnp.minimum(tok + 1, w).astype(F32)
    return sm / cnt - pbuf[pl.ds(HALO, ts), cols]


def _pool_fwd(z1, w_pool, scale, bsz, name):
    t = z1.shape[0]
    s = t // bsz
    ts = _seq_tile(s)
    nt = s // ts

    def body(zp_ref, zc_ref, wp_ref, sc_ref, o_ref, token, pbuf):
        token[...] = jnp.zeros_like(token)
        i = pl.program_id(1)
        pbuf[0:HALO, :] = jnp.where(i > 0, zp_ref[...].astype(F32), 0.0)
        pbuf[HALO:HALO + ts, :] = zc_ref[...].astype(F32)
        tok = i * ts + lax.broadcasted_iota(jnp.int32, (ts, 1), 0)
        for g in range(4):
            cols = slice(128 * g, 128 * (g + 1))
            pooled = _pooled(pbuf, g, ts, tok).astype(BF16)
            o_ref[:, cols] = (_nn(pooled, wp_ref[g].astype(BF16)) * sc_ref[:, cols]).astype(BF16)

    zp, zc, _ = _seq_specs(ts, nt, t, 512, 0)
    return pl.pallas_call(
        body, name=name, grid=(bsz, nt), in_specs=[zp, zc, _full((4, 128, 128)), _row(512)],
        out_specs=[pl.BlockSpec((ts, 512), lambda b, i: (b * nt + i, 0)), _full((8, 128))],
        out_shape=[jax.ShapeDtypeStruct((t, 512), BF16), jax.ShapeDtypeStruct((8, 128), F32)],
        scratch_shapes=[pltpu.VMEM((HALO + ts, 512), F32)],
        compiler_params=_params("arbitrary", "arbitrary"))(z1, z1, w_pool, scale)


def _pool_bwd(z1, dmix, w_pool, scale, bsz, name):
    t = z1.shape[0]
    s = t // bsz
    ts = _seq_tile(s)
    nt = s // ts
    rr = ts + HALO

    def body(zp_ref, zc_ref, dc_ref, dn_ref, wp_ref, sc_ref, dz_ref, dwp_ref, dsc_ref, pbuf, ebuf):
        b, i = pl.program_id(0), pl.program_id(1)

        @pl.when((b == 0) & (i == 0))
        def _():
            dwp_ref[...] = jnp.zeros_like(dwp_ref)
            dsc_ref[...] = jnp.zeros_like(dsc_ref)

        pbuf[0:HALO, :] = jnp.where(i > 0, zp_ref[...].astype(F32), 0.0)
        pbuf[HALO:HALO + ts, :] = zc_ref[...].astype(F32)
        dn = jnp.where(i < nt - 1, dn_ref[...].astype(F32), 0.0)
        dout = jnp.concatenate([dc_ref[...].astype(F32), dn], axis=0)
        tok = i * ts + lax.broadcasted_iota(jnp.int32, (ts, 1), 0)
        tok_r = i * ts + lax.broadcasted_iota(jnp.int32, (rr, 1), 0)
        for g in range(4):
            w = 2 << g
            cols = slice(128 * g, 128 * (g + 1))
            wg = wp_ref[g].astype(BF16)
            pooled = _pooled(pbuf, g, ts, tok).astype(BF16)
            dsc_ref[:, cols] += jnp.sum(dout[:ts, cols] * _nn(pooled, wg), axis=0, keepdims=True)
            dy = (dout[:, cols] * sc_ref[:, cols]).astype(BF16)
            dwp_ref[g] += _tn(pooled, dy[:ts])
            dpl = _nt(dy, wg)
            ebuf[...] = dpl / jnp.minimum(tok_r + 1, w).astype(F32)
            dz = ebuf[pl.ds(0, ts), :] - dpl[:ts]
            for d in range(1, w):
                dz = dz + ebuf[pl.ds(d, ts), :]
            dz_ref[:, cols] = dz.astype(BF16)

    zp, zc, _ = _seq_specs(ts, nt, t, 512, 0)
    _, dc, dn = _seq_specs(ts, nt, t, 512, 0)
    return pl.pallas_call(
        body, name=name, grid=(bsz, nt), in_specs=[zp, zc, dc, dn, _full((4, 128, 128)), _row(512)],
        out_specs=[pl.BlockSpec((ts, 512), lambda b, i: (b * nt + i, 0)), _full((4, 128, 128)), _row(512)],
        out_shape=[jax.ShapeDtypeStruct((t, 512), BF16), jax.ShapeDtypeStruct((4, 128, 128), F32),
                   jax.ShapeDtypeStruct((1, 512), F32)],
        scratch_shapes=[pltpu.VMEM((HALO + ts, 512), F32), pltpu.VMEM((rr, 128), F32)],
        compiler_params=_params("arbitrary", "arbitrary"))(z1, z1, dmix, dmix, w_pool, scale)


def _tril():
    r = lax.broadcasted_iota(jnp.int32, (SGU_CHUNK, SGU_CHUNK), 0)
    c = lax.broadcasted_iota(jnp.int32, (SGU_CHUNK, SGU_CHUNK), 1)
    return r >= c


def _sgu_fwd(z1, ln_g, ln_b, w_s, b_rows, name):
    t = z1.shape[0]
    ts = _tile(t)

    def body(zu_ref, zv_ref, lg_ref, lb_ref, ws_ref, bs_ref, o_ref):
        v = _gelu(zv_ref[...].astype(F32))[0]
        vb = _ln(v, lg_ref[...], lb_ref[...])[0].astype(BF16)
        tril = _tril()
        for g in range(4):
            cols = slice(128 * g, 128 * (g + 1))
            wg = jnp.where(tril, ws_ref[g], 0.0).astype(BF16)
            for c in range(ts // SGU_CHUNK):
                rows = slice(SGU_CHUNK * c, SGU_CHUNK * (c + 1))
                mixed = _nn(wg, vb[rows, cols]) + bs_ref[g]
                o_ref[rows, cols] = (_gelu(zu_ref[rows, cols].astype(F32))[0] * mixed).astype(BF16)

    return pl.pallas_call(
        body, name=name, grid=(t // ts,),
        in_specs=[pl.BlockSpec((ts, 512), lambda i: (i, 1)), pl.BlockSpec((ts, 512), lambda i: (i, 2)),
                  _row(512), _row(512), _full((4, 128, 128)), _full((4, 128, 128))],
        out_specs=pl.BlockSpec((ts, 512), lambda i: (i, 0)), out_shape=jax.ShapeDtypeStruct((t, 512), BF16),
        compiler_params=_params("parallel"))(z1, z1, ln_g, ln_b, w_s, b_rows)


def _sgu_bwd(z1, dmix, ln_g, ln_b, w_s, b_rows, name):
    t = z1.shape[0]
    ts = _tile(t)

    def body(zu_ref, zv_ref, d_ref, lg_ref, lb_ref, ws_ref, bs_ref,
             dzu_ref, dzv_ref, dws_ref, dbs_ref, dlg_ref, dlb_ref, dvbuf):
        @pl.when(pl.program_id(0) == 0)
        def _():
            for ref in (dws_ref, dbs_ref, dlg_ref, dlb_ref):
                ref[...] = jnp.zeros_like(ref)

        zv = zv_ref[...].astype(F32)
        v, thv = _gelu(zv)
        lg = lg_ref[...]
        vln, xhat, rstd = _ln(v, lg, lb_ref[...])
        vb = vln.astype(BF16)
        tril = _tril()
        for g in range(4):
            cols = slice(128 * g, 128 * (g + 1))
            wg = jnp.where(tril, ws_ref[g], 0.0).astype(BF16)
            dws = jnp.zeros((SGU_CHUNK, SGU_CHUNK), F32)
            dbs = jnp.zeros((1, SGU_CHUNK), F32)
            for c in range(ts // SGU_CHUNK):
                rows = slice(SGU_CHUNK * c, SGU_CHUNK * (c + 1))
                vbc = vb[rows, cols]
                mixed = _nn(wg, vbc) + bs_ref[g]
                zu = zu_ref[rows, cols].astype(F32)
                u, thu = _gelu(zu)
                dout = d_ref[rows, cols].astype(F32)
                dzu_ref[rows, cols] = (dout * mixed * _gelu_grad(zu, thu)).astype(BF16)
                dm = dout * u
                dmb = dm.astype(BF16)
                dws = dws + _nt(dmb, vbc)
                dbs = dbs + jnp.sum(dm.T, axis=0, keepdims=True)
                dvbuf[rows, cols] = _tn(wg, dmb)
            dws_ref[g] += jnp.where(tril, dws, 0.0)
            dbs_ref[pl.ds(g, 1), :] += dbs
        dvln = dvbuf[...]
        dlg_ref[...] += jnp.sum(dvln * xhat, axis=0, keepdims=True)
        dlb_ref[...] += jnp.sum(dvln, axis=0, keepdims=True)
        dzv_ref[...] = (_ln_bwd(dvln, xhat, rstd, lg) * _gelu_grad(zv, thv)).astype(BF16)

    tile = pl.BlockSpec((ts, 512), lambda i: (i, 0))
    vec = jax.ShapeDtypeStruct((1, 512), F32)
    return pl.pallas_call(
        body, name=name, grid=(t // ts,),
        in_specs=[pl.BlockSpec((ts, 512), lambda i: (i, 1)), pl.BlockSpec((ts, 512), lambda i: (i, 2)),
                  pl.BlockSpec((ts, 512), lambda i: (i, 1)), _row(512), _row(512), _full((4, 128, 128)),
                  _full((4, 128, 128))],
        out_specs=[tile, tile, _full((4, 128, 128)), _full((4, 128)), _row(512), _row(512)],
        out_shape=[jax.ShapeDtypeStruct((t, 512), BF16), jax.ShapeDtypeStruct((t, 512), BF16),
                   jax.ShapeDtypeStruct((4, 128, 128), F32), jax.ShapeDtypeStruct((4, 128), F32), vec, vec],
        scratch_shapes=[pltpu.VMEM((ts, 512), F32)],
        compiler_params=_params("arbitrary"))(z1, z1, dmix, ln_g, ln_b, w_s, b_rows)


def _row_tile(r):
    for cand in (512, 352, 256, 192, 128, 64, 32, 16, 8):
        if r % cand == 0:
            return cand
    return r


def _sum_slabs(a, name):
    k, r, c = a.shape
    tr = _row_tile(r)

    def body(*refs):
        acc = refs[0][...].astype(F32)
        for ref in refs[1:-1]:
            acc = acc + ref[...].astype(F32)
        refs[-1][...] = acc

    in_specs = [pl.BlockSpec((None, tr, c), functools.partial(lambda i, s: (s, i, 0), s=s)) for s in range(k)]
    return pl.pallas_call(
        body, name=name, grid=(r // tr,), in_specs=in_specs, out_specs=pl.BlockSpec((tr, c), lambda i: (i, 0)),
        out_shape=jax.ShapeDtypeStruct((r, c), F32), compiler_params=_params("parallel"))(*([a] * k))


def _adamw_math(w, g, m, v):
    mn = ADAM_B1 * m + (1.0 - ADAM_B1) * g
    vn = ADAM_B2 * v + (1.0 - ADAM_B2) * (g * g)
    m_hat = mn / (1.0 - ADAM_B1 ** ADAM_STEP)
    v_hat = vn / (1.0 - ADAM_B2 ** ADAM_STEP)
    return -ADAM_LR * (m_hat / (jnp.sqrt(v_hat) + ADAM_EPS) + ADAM_WD * w), mn, vn


def _reduce_adamw(landing, w, m, v, name, layer=None, into=None):
    k, r, c = landing.shape
    tr = _row_tile(r)
    n_into = 0 if into is None else 4

    def body(*refs):
        slabs, (w_ref, m_ref, v_ref) = refs[:k], refs[k:k + 3]
        g_ref, d_ref, mo_ref, vo_ref = refs[k + 3 + n_into:]
        g = slabs[0][...].astype(F32)
        for ref in slabs[1:]:
            g = g + ref[...].astype(F32)
        g_ref[...] = g
        d_ref[...], mo_ref[...], vo_ref[...] = _adamw_math(w_ref[...], g, m_ref[...], v_ref[...])

    if layer is None:
        spec = pl.BlockSpec((tr, c), lambda i: (i, 0))
    else:
        spec = pl.BlockSpec((None, tr, c), lambda i: (layer, i, 0))
    in_specs = [pl.BlockSpec((None, tr, c), functools.partial(lambda i, s: (s, i, 0), s=s)) for s in range(k)]
    in_specs += [spec] * 3 + [ANY] * n_into
    shape = jax.ShapeDtypeStruct(w.shape, F32)
    return pl.pallas_call(
        body, name=name, grid=(r // tr,), in_specs=in_specs, out_specs=[spec] * 4, out_shape=[shape] * 4,
        input_output_aliases={k + 3 + j: j for j in range(n_into)},
        compiler_params=_params("parallel"))(*([landing] * k), w, m, v, *(into or ()))


def _rows_needed(shape):
    return shape[0] * -(-shape[1] // D)


def _as_rows(a):
    r, c = a.shape
    n = -(-c // D)
    assert r == 1 or n == 1
    return jnp.pad(a, ((0, 0), (0, n * D - c))).reshape(r * n, D)


def _adamw_replicated(g_rows, w, m, v, name):
    n = len(REP_2D)

    def body(*refs):
        g_ref, w_refs, m_refs, v_refs, outs = refs[0], refs[1:1 + n], refs[1 + n:1 + 2 * n], refs[1 + 2 * n:1 + 3 * n], refs[1 + 3 * n:]
        r0 = 0
        for k, (_, (r, c)) in enumerate(REP_2D):
            pieces = [g_ref[r0 + j * r:r0 + j * r + r, 0:min(D, c - j * D)] for j in range(-(-c // D))]
            g = pieces[0] if len(pieces) == 1 else jnp.concatenate(pieces, axis=1)
            outs[4 * k][...] = g
            outs[4 * k + 1][...], outs[4 * k + 2][...], outs[4 * k + 3][...] = _adamw_math(
                w_refs[k][...], g, m_refs[k][...], v_refs[k][...])
            r0 += _rows_needed((r, c))

    shapes = [s for _, s in REP_2D]
    return pl.pallas_call(
        body, name=name, in_specs=[_full(g_rows.shape)] + [_full(s) for s in shapes] * 3,
        out_specs=[_full(s) for s in shapes for _ in range(4)],
        out_shape=[jax.ShapeDtypeStruct(s, F32) for s in shapes for _ in range(4)],
        grid=(1,), compiler_params=_params("arbitrary"))(g_rows, *w, *m, *v)


def _adamw(w, g, m, v, name):
    r, c = w.shape
    tr = _row_tile(r)

    def body(w_ref, g_ref, m_ref, v_ref, d_ref, mo_ref, vo_ref):
        d_ref[...], mo_ref[...], vo_ref[...] = _adamw_math(w_ref[...], g_ref[...], m_ref[...], v_ref[...])

    spec = pl.BlockSpec((tr, c), lambda i: (i, 0))
    shape = jax.ShapeDtypeStruct((r, c), F32)
    return pl.pallas_call(
        body, name=name, grid=(r // tr,), in_specs=[spec] * 4, out_specs=[spec] * 3, out_shape=[shape] * 3,
        compiler_params=_params("parallel"))(w, g, m, v)


ANY = pl.BlockSpec(memory_space=pl.ANY)


def _all_gather(block, name):
    r, c_dim = block.shape

    def body(x_ref, out_ref, token, send_sems, recv_sems, local_sem):
        token[...] = jnp.zeros_like(token)
        x, y, c = lax.axis_index("x"), lax.axis_index("y"), lax.axis_index("c")
        me, sibling = (x, y, c), (x, y, 1 - c)
        chips = [(1 - x, y), (x, 1 - y), (1 - x, 1 - y)]

        def rows(px, py, pc):
            return out_ref.at[4 * px + 2 * py + pc]

        def copy(k, blk, to, src=None):
            return pltpu.make_async_remote_copy(
                src_ref=rows(*blk) if src is None else src, dst_ref=rows(*blk), send_sem=send_sems.at[k],
                recv_sem=recv_sems.at[k], device_id=to, device_id_type=MESH)

        mine = pltpu.make_async_copy(x_ref, rows(*me), local_sem)
        mine.start()
        first = [copy(0, me, sibling, src=x_ref)]
        first += [copy(1 + j, me, (*chip, c), src=x_ref) for j, chip in enumerate(chips)]
        for cp in first:
            cp.start()
        passed = [copy(4 + j, (*chip, c), sibling) for j, chip in enumerate(chips)]
        for j, chip in enumerate(chips):
            copy(1 + j, (*chip, c), me).wait_recv()
            passed[j].start()
        copy(0, sibling, me).wait_recv()
        for j, chip in enumerate(chips):
            copy(4 + j, (*chip, 1 - c), me).wait_recv()
        for cp in first + passed:
            cp.wait_send()
        mine.wait()

    return pl.pallas_call(
        body, name=name, in_specs=[ANY], out_specs=[ANY, pl.BlockSpec(memory_space=pltpu.VMEM)],
        out_shape=[jax.ShapeDtypeStruct((N_DEV, r, c_dim), block.dtype), jax.ShapeDtypeStruct((8, 128), F32)],
        scratch_shapes=[pltpu.SemaphoreType.DMA((7,)), pltpu.SemaphoreType.DMA((7,)), pltpu.SemaphoreType.DMA],
    )(block)


HBM = pl.BlockSpec(memory_space=pltpu.HBM)
SEM = pl.BlockSpec(memory_space=pltpu.SEMAPHORE)
EFFECT = pltpu.SideEffectType.DATAFLOW_SIDE_EFFECTING


def _exchange_copies(scatter, src_refs, land_refs, send_sems, recv_sems, local_sems):
    x, y, c = lax.axis_index("x"), lax.axis_index("y"), lax.axis_index("c")
    me = 4 * x + 2 * y + c
    sends, arrivals, locals_ = [], [], []
    for a, (src, land) in enumerate(zip(src_refs, land_refs)):
        def pick(idx, src=src):
            return src.at[idx] if scatter else src

        locals_.append(pltpu.make_async_copy(pick(me), land.at[me], local_sems.at[a]))
        for r in range(1, N_DEV):
            px = 1 - x if r & 4 else x
            py = 1 - y if r & 2 else y
            pc = 1 - c if r & 1 else c
            peer, s = 4 * px + 2 * py + pc, 7 * a + r - 1
            sends.append(pltpu.make_async_remote_copy(
                src_ref=pick(peer), dst_ref=land.at[me], send_sem=send_sems.at[s], recv_sem=recv_sems.at[s],
                device_id=(px, py, pc), device_id_type=MESH))
            arrivals.append(pltpu.make_async_remote_copy(
                src_ref=pick(peer), dst_ref=land.at[peer], send_sem=send_sems.at[s], recv_sem=recv_sems.at[s],
                device_id=(px, py, pc), device_id_type=MESH))
    return sends, arrivals, locals_


def _exchange_start(srcs, scatter, name):
    n = len(srcs)
    lands = [lax.empty((N_DEV,) + s.shape[-2:], s.dtype) for s in srcs]

    def body(*refs):
        src_refs, land_refs = refs[:n], refs[n:2 * n]
        send_sems, recv_sems, local_sems = refs[2 * n:2 * n + 3]
        token = refs[-1]
        sends, _, locals_ = _exchange_copies(scatter, src_refs, land_refs, send_sems, recv_sems, local_sems)
        for cp in locals_ + sends:
            cp.start()
        token[...] = jnp.zeros_like(token)

    res = pl.pallas_call(
        body, name=name,
        out_shape=[pltpu.SemaphoreType.DMA((7 * n,)), pltpu.SemaphoreType.DMA((7 * n,)), pltpu.SemaphoreType.DMA((n,))]
        + [pltpu.HBM(a.shape, a.dtype) for a in list(srcs) + lands] + [jax.ShapeDtypeStruct((8, 128), F32)],
        in_specs=[HBM] * (2 * n), out_specs=[SEM] * 3 + [HBM] * (2 * n) + [pl.BlockSpec(memory_space=pltpu.VMEM)],
        input_output_aliases={i: 3 + i for i in range(2 * n)},
        compiler_params=pltpu.CompilerParams(has_side_effects=EFFECT),
    )(*[pltpu.with_memory_space_constraint(a, pltpu.HBM) for a in list(srcs) + lands])
    return (n, scatter, res[:3], res[3:3 + 2 * n]), res[-1]


def _exchange_wait(handle, after, name):
    n, scatter, sems, thru = handle

    def body(*refs):
        src_refs, land_refs = refs[:n], refs[n:2 * n]
        send_sems, recv_sems, local_sems = refs[2 * n:2 * n + 3]
        sends, arrivals, locals_ = _exchange_copies(scatter, src_refs, land_refs, send_sems, recv_sems, local_sems)
        for cp in arrivals:
            cp.wait_recv()
        for cp in sends:
            cp.wait_send()
        for cp in locals_:
            cp.wait()

    res = pl.pallas_call(
        body, name=name, out_shape=[pltpu.HBM(a.shape, a.dtype) for a in thru],
        in_specs=[HBM] * (2 * n) + [SEM] * 3 + [ANY], out_specs=[HBM] * (2 * n),
        input_output_aliases={i: i for i in range(2 * n)},
        compiler_params=pltpu.CompilerParams(has_side_effects=EFFECT),
    )(*thru, *sems, after)
    return res[n:]


def _behind(tokens, a):
    zero = sum(tok[0, 0] for tok in tokens)
    return jax.tree.map(lambda v: v + zero.astype(v.dtype), a)


def _perm_heads(a, perm, axis):
    idx = [slice(None)] * a.ndim
    parts = []
    for h in perm:
        idx[axis] = slice(64 * h, 64 * (h + 1))
        parts.append(a[tuple(idx)])
    idx[axis] = slice(512, None)
    if a.shape[axis] > 512:
        parts.append(a[tuple(idx)])
    return jnp.concatenate(parts, axis=axis)


Q_INV = tuple(int(i) for i in np.argsort(Q_PERM))


def _in0_to_kernel(a, axis):
    a = _perm_heads(a, Q_PERM, axis)
    idx = [slice(None)] * a.ndim

    def cut(lo, hi):
        idx[axis] = slice(lo, hi)
        return a[tuple(idx)]

    return jnp.concatenate([cut(0, 512), cut(768, 1792), cut(512, 768)], axis=axis)


def _in0_from_kernel(a, axis):
    idx = [slice(None)] * a.ndim

    def cut(lo, hi):
        idx[axis] = slice(lo, hi)
        return a[tuple(idx)]

    a = jnp.concatenate([cut(0, 512), cut(1536, 1792), cut(512, 1536)], axis=axis)
    return _perm_heads(a, Q_INV, axis)


def _f32_as_u16_rows(v, rows):
    bits = lax.bitcast_convert_type(v, jnp.uint16).reshape(-1)
    return jnp.pad(bits, (0, rows * D - bits.shape[0])).reshape(rows, D)


def _pad_rows(v, rows):
    v = v.reshape(-1)
    return jnp.pad(v, (0, rows * D - v.shape[0])).reshape(rows, D)


def kernel(x, mix_norm, a_w_in, a_b_in, a_sinks, a_conv_w, a_conv_b, a_cln_g, a_cln_b, a_w_out, c_w_in, c_w_pool, c_pool_scale, c_sln_g, c_sln_b, c_w_s, c_b_s, c_w_out, ffn_norm, ffn_w_gate, ffn_w_up, ffn_w_down, final_norm, loss_target, m_mix_norm, m_a_w_in, m_a_b_in, m_a_sinks, m_a_conv_w, m_a_conv_b, m_a_cln_g, m_a_cln_b, m_a_w_out, m_c_w_in, m_c_w_pool, m_c_pool_scale, m_c_sln_g, m_c_sln_b, m_c_w_s, m_c_b_s, m_c_w_out, m_ffn_norm, m_ffn_w_gate, m_ffn_w_up, m_ffn_w_down, m_final_norm, v_mix_norm, v_a_w_in, v_a_b_in, v_a_sinks, v_a_conv_w, v_a_conv_b, v_a_cln_g, v_a_cln_b, v_a_w_out, v_c_w_in, v_c_w_pool, v_c_pool_scale, v_c_sln_g, v_c_sln_b, v_c_w_s, v_c_b_s, v_c_w_out, v_ffn_norm, v_ffn_w_gate, v_ffn_w_up, v_ffn_w_down, v_final_norm):
    bsz, seq, _ = x.shape
    t = bsz * seq
    w_in = dict(mix_norm=mix_norm, a_w_in=a_w_in, a_b_in=a_b_in, a_sinks=a_sinks, a_conv_w=a_conv_w, a_conv_b=a_conv_b,
                a_cln_g=a_cln_g, a_cln_b=a_cln_b, a_w_out=a_w_out, c_w_in=c_w_in, c_w_pool=c_w_pool,
                c_pool_scale=c_pool_scale, c_sln_g=c_sln_g, c_sln_b=c_sln_b, c_w_s=c_w_s, c_b_s=c_b_s, c_w_out=c_w_out,
                ffn_norm=ffn_norm, ffn_w_gate=ffn_w_gate, ffn_w_up=ffn_w_up, ffn_w_down=ffn_w_down, final_norm=final_norm)
    m_in = dict(mix_norm=m_mix_norm, a_w_in=m_a_w_in, a_b_in=m_a_b_in, a_sinks=m_a_sinks, a_conv_w=m_a_conv_w,
                a_conv_b=m_a_conv_b, a_cln_g=m_a_cln_g, a_cln_b=m_a_cln_b, a_w_out=m_a_w_out, c_w_in=m_c_w_in,
                c_w_pool=m_c_w_pool, c_pool_scale=m_c_pool_scale, c_sln_g=m_c_sln_g, c_sln_b=m_c_sln_b, c_w_s=m_c_w_s,
                c_b_s=m_c_b_s, c_w_out=m_c_w_out, ffn_norm=m_ffn_norm, ffn_w_gate=m_ffn_w_gate, ffn_w_up=m_ffn_w_up,
                ffn_w_down=m_ffn_w_down, final_norm=m_final_norm)
    v_in = dict(mix_norm=v_mix_norm, a_w_in=v_a_w_in, a_b_in=v_a_b_in, a_sinks=v_a_sinks, a_conv_w=v_a_conv_w,
                a_conv_b=v_a_conv_b, a_cln_g=v_a_cln_g, a_cln_b=v_a_cln_b, a_w_out=v_a_w_out, c_w_in=v_c_w_in,
                c_w_pool=v_c_w_pool, c_pool_scale=v_c_pool_scale, c_sln_g=v_c_sln_g, c_sln_b=v_c_sln_b, c_w_s=v_c_w_s,
                c_b_s=v_c_b_s, c_w_out=v_c_w_out, ffn_norm=v_ffn_norm, ffn_w_gate=v_ffn_w_gate, ffn_w_up=v_ffn_w_up,
                ffn_w_down=v_ffn_w_down, final_norm=v_final_norm)

    small = jnp.concatenate([a_conv_w[0].reshape(-1), c_pool_scale[0], c_sln_g[0], c_sln_b[0]])
    first_bits = lax.bitcast_convert_type(a_w_in[0].T.astype(BF16), jnp.uint16)
    gathered, tok = _all_gather(jnp.concatenate([first_bits, _f32_as_u16_rows(small, W_MISC_ROWS)], axis=0), "gather_mixer0")

    def ffn_shards(l):
        return [ffn_w_gate[l].T.astype(BF16), ffn_w_up[l].T.astype(BF16), ffn_w_down[l].astype(BF16)]

    ffn0_h, tok = _exchange_start(_behind([tok], ffn_shards(0) + [a_w_out[0].astype(BF16)]), False, "gather_ffn0_start")
    mix1_h, tok = _exchange_start(_behind([tok], [c_w_in[0].T.astype(BF16), c_w_out[0].astype(BF16)]), False,
                                  "gather_mixer1_start")
    ffn1_h, tok = _exchange_start(_behind([tok], ffn_shards(1)), False, "gather_ffn1_start")

    a_in_full = lax.bitcast_convert_type(gathered[:, :224].reshape(IN0, D), BF16)
    small_all = lax.bitcast_convert_type(
        gathered[:, 224:].reshape(N_DEV, -1)[:, :2 * SMALL_SHARD].reshape(N_DEV, SMALL_SHARD, 2), F32)
    conv_w = small_all[:, :31 * 64].reshape(N_DEV, 31, 64).transpose(1, 0, 2).reshape(31, 512)
    conv_w = jnp.pad(conv_w, ((0, HALO - CONV_K), (0, 0)))
    pool_scale = small_all[:, 31 * 64:31 * 64 + 64].reshape(1, 512)
    sln_g = small_all[:, 31 * 64 + 64:31 * 64 + 128].reshape(1, 512)
    sln_b = small_all[:, 31 * 64 + 128:].reshape(1, 512)

    wt_in0 = _in0_to_kernel(a_in_full, 0)
    b_in0 = _in0_to_kernel(a_b_in, 1)
    b_rows = jnp.broadcast_to(c_b_s[0][:, :, None], (4, 128, 128))
    conv_b, cln_g, cln_b = a_conv_b, a_cln_g, a_cln_b

    h0 = x.reshape(t, D)
    target = loss_target.reshape(t, D)
    z0, hn0 = _norm_proj(h0, _behind([tok], mix_norm[0:1]), wt_in0, b_in0, "in_proj0")
    attn, tok = _attn_fwd(z0, a_sinks, bsz, "attn_fwd")
    conv, conv_y = _conv_fwd(z0, conv_w, conv_b, cln_g, _behind([tok], cln_b), bsz, "conv_fwd")
    wtg0, wtu0, wd0, a_out_full = (w.reshape(-1, D) for w in _exchange_wait(ffn0_h, conv, "gather_ffn0_wait"))
    w_out0 = _perm_heads(a_out_full, Q_PERM, 0)
    h1, h2, hnf0, gate0, up0 = _ffn_fwd(h0, attn, conv, w_out0, ffn_norm[0:1], wtg0, wtu0, wd0, "ffn_fwd0")
    wt_in1, w_out1 = (w.reshape(-1, D) for w in _exchange_wait(mix1_h, h2, "gather_mixer1_wait"))
    z1, hn1 = _norm_proj(h2, mix_norm[1:2], wt_in1, None, "in_proj1")
    pool, tok = _pool_fwd(z1, c_w_pool[0], pool_scale, bsz, "pool_fwd")
    sgu = _sgu_fwd(z1, sln_g, _behind([tok], sln_b), c_w_s[0], b_rows, "sgu_fwd")
    wtg1, wtu1, wd1 = (w.reshape(D_FF, D) for w in _exchange_wait(ffn1_h, sgu, "gather_ffn1_wait"))
    h3, dh4, hnf1, gate1, up1, d_final_norm, loss_part = _ffn_fwd(
        h2, pool, sgu, w_out1, ffn_norm[1:2], wtg1, wtu1, wd1, "ffn_fwd1", head=(final_norm.reshape(1, D), target))

    def blocks(g):
        return g.reshape(N_DEV, g.shape[0] // N_DEV, D)

    dh3, dmix1, dgate1, dup1, act1, d_fn1 = _ffn_bwd(dh4, h3, ffn_norm[1:2], gate1, up1, wtg1, wtu1, wd1, w_out1, "ffn_bwd1")
    gw_ffn1 = [_mm_tn(dgate1, hnf1, "dw_gate1"), _mm_tn(dup1, hnf1, "dw_up1"), _mm_tn(act1, dh4, "dw_down1")]
    ffn1_g, tok = _exchange_start([blocks(g) for g in gw_ffn1], True, "scatter_ffn1_start")
    gw_c_out = _mm_tn_pieces([pool, sgu], dh3, "dw_out1")
    dzp, d_w_pool, d_pool_scale = _pool_bwd(z1, dmix1, c_w_pool[0], _behind([tok], pool_scale), bsz, "pool_bwd")
    dzu, dzv, d_w_s, d_b_s, d_sln_g, d_sln_b = _sgu_bwd(z1, dmix1, sln_g, sln_b, c_w_s[0], b_rows, "sgu_bwd")
    dh2, d_mn1 = _proj_bwd_norm([(dzp, 0), (dzu, 512), (dzv, 1024)], wt_in1, h2, dh3, mix_norm[1:2], BF16, "in_proj1_bwd")
    gw_c_in = _mm_tn_pieces([dzp, dzu, dzv], hn1, "dw_in1")
    mix1_g, tok = _exchange_start([blocks(gw_c_in), blocks(gw_c_out)], True, "scatter_mixer1_start")
    dh1, dmix0, dgate0, dup0, act0, d_fn0 = _ffn_bwd(dh2, h1, _behind([tok], ffn_norm[0:1]), gate0, up0, wtg0, wtu0, wd0,
                                                      w_out0, "ffn_bwd0")
    gw_ffn0 = [_mm_tn(dgate0, hnf0, "dw_gate0"), _mm_tn(dup0, hnf0, "dw_up0"), _mm_tn(act0, dh2, "dw_down0")]
    ffn0_g, tok = _exchange_start([blocks(g) for g in gw_ffn0], True, "scatter_ffn0_start")
    gw_a_out = _perm_heads(_mm_tn_pieces([attn, conv], dh1, "dw_out0"), Q_INV, 0)
    dq, dkv, d_sink_row, d_bq, d_bkv = _attn_bwd(z0, dmix0, _behind([tok], a_sinks), bsz, "attn_bwd")
    dca, dcg, d_conv_w, d_conv_b, d_cln_g, d_cln_b, d_ba, d_bg = _conv_bwd(z0, conv_y, dmix0, conv_w, cln_g, cln_b, bsz, "conv_bwd")
    gw_a_in = _in0_from_kernel(_mm_tn_pieces([dq, dca, dcg, dkv], hn0, "dw_in0"), 0)
    mix0_g, tok = _exchange_start([blocks(gw_a_in), blocks(gw_a_out)], True, "scatter_mixer0_start")
    dx, d_mn0 = _proj_bwd_norm([(dq, 0), (dca, 512), (dcg, 1024), (dkv, 1536)], wt_in0, h0, dh1,
                               _behind([tok], mix_norm[0:1]), F32, "in_proj0_bwd")
    d_b_in = _in0_from_kernel(jnp.concatenate([d_bq, d_ba, d_bg, d_bkv], axis=1), 1)

    rep = dict(mix_norm=jnp.concatenate([d_mn0, d_mn1], axis=0), a_b_in=d_b_in, a_sinks=d_sink_row[:, :8],
               a_conv_b=d_conv_b, a_cln_g=d_cln_g, a_cln_b=d_cln_b, c_w_pool=d_w_pool.reshape(64, D),
               c_w_s=d_w_s.reshape(64, D), c_b_s=d_b_s, ffn_norm=jnp.concatenate([d_fn0, d_fn1], axis=0),
               final_norm=d_final_norm)
    rep_rows = jnp.concatenate([_as_rows(rep[nm]) for nm, _ in REP_2D] + [_as_rows(loss_part)], axis=0)
    rep_flat = jnp.pad(rep_rows, ((0, N_DEV * REP_ROWS - rep_rows.shape[0]), (0, 0))).reshape(N_DEV, REP_ROWS, D)
    small_g = jnp.concatenate([
        d_conv_w[:CONV_K].reshape(31, N_DEV, 64).transpose(1, 0, 2).reshape(N_DEV, 31 * 64),
        d_pool_scale.reshape(N_DEV, 64), d_sln_g.reshape(N_DEV, 64), d_sln_b.reshape(N_DEV, 64)], axis=1)
    small_g = jnp.pad(small_g, ((0, 0), (0, G_SMALL_ROWS * D - SMALL_SHARD))).reshape(N_DEV, G_SMALL_ROWS, D)
    tail_g, tok = _exchange_start([jnp.concatenate([small_g, rep_flat], axis=1)], True, "scatter_tail_start")

    names = list(w_in)
    g_out, delta, new_m, new_v = {}, {}, {}, {}
    column_sharded = ("a_w_in", "c_w_in", "ffn_w_gate", "ffn_w_up")

    def rows_of(a, nm):
        return jnp.swapaxes(a, 1, 2) if nm in column_sharded else a

    def reduce_adamw(nm, landing, layer, into=None):
        args = [rows_of(d[nm], nm) for d in (w_in, m_in, v_in)]
        if args[0].shape[0] == 1:
            args, layer = [a[0] for a in args], None
        return _reduce_adamw(landing, *args, "adamw_%s_%s" % (nm, layer), layer=layer, into=into)

    def keep(nm, res):
        res = [r if r.ndim == 3 else r[None] for r in res]
        g_out[nm], delta[nm], new_m[nm], new_v[nm] = (rows_of(r, nm) for r in res)

    ffn_names = ("ffn_w_gate", "ffn_w_up", "ffn_w_down")
    landed = _exchange_wait(ffn1_g, tok, "scatter_ffn1_wait")
    ffn_res = [reduce_adamw(nm, a, 1) for nm, a in zip(ffn_names, landed)]
    landed = _exchange_wait(mix1_g, ffn_res[-1][0], "scatter_mixer1_wait")
    for nm, a in zip(("c_w_in", "c_w_out"), landed):
        keep(nm, reduce_adamw(nm, a, 0))
    landed = _exchange_wait(ffn0_g, g_out["c_w_out"], "scatter_ffn0_wait")
    for nm, a, res in zip(ffn_names, landed, ffn_res):
        keep(nm, reduce_adamw(nm, a, 0, into=res))
    landed = _exchange_wait(mix0_g, g_out["ffn_w_down"], "scatter_mixer0_wait")
    for nm, a in zip(("a_w_in", "a_w_out"), landed):
        keep(nm, reduce_adamw(nm, a, 0))
    g_tail = _sum_slabs(_exchange_wait(tail_g, g_out["a_w_out"], "scatter_tail_wait")[0], "sum_tail")
    rep_all = _all_gather(g_tail[G_SMALL_ROWS:], "gather_replicated_grads")[0].reshape(N_DEV * REP_ROWS, D)
    loss = rep_all[sum(_rows_needed(s) for _, s in REP_2D), 0]
    res = _adamw_replicated(rep_all, *[[d[nm].reshape(s) for nm, s in REP_2D] for d in (w_in, m_in, v_in)], "adamw_replicated")
    for k, (nm, _) in enumerate(REP_2D):
        g_out[nm], delta[nm], new_m[nm], new_v[nm] = (r.reshape(w_in[nm].shape) for r in res[4 * k:4 * k + 4])
    small_r = g_tail[:G_SMALL_ROWS].reshape(-1)[:SMALL_SHARD]
    g_out.update(
        a_conv_w=small_r[:31 * 64].reshape(1, 31, 64), c_pool_scale=small_r[31 * 64:31 * 64 + 64].reshape(1, 64),
        c_sln_g=small_r[31 * 64 + 64:31 * 64 + 128].reshape(1, 64), c_sln_b=small_r[31 * 64 + 128:].reshape(1, 64))
    group = ("a_conv_w", "c_pool_scale", "c_sln_g", "c_sln_b")
    flat = [_pad_rows(jnp.concatenate([d[nm].reshape(-1) for nm in group]), G_SMALL_ROWS) for d in (w_in, g_out, m_in, v_in)]
    res = [r.reshape(-1) for r in _adamw(*flat, "adamw_small_sharded")]
    off = 0
    for nm in group:
        n = int(np.prod(w_in[nm].shape))
        delta[nm], new_m[nm], new_v[nm] = (r[off:off + n].reshape(w_in[nm].shape) for r in res)
        off += n

    grad_x = dx.reshape(bsz, seq, D)
    return (loss, grad_x, *[g_out[nm] for nm in names], *[delta[nm] for nm in names],
            *[new_m[nm] for nm in names], *[new_v[nm] for nm in names])
```

```python
import functools

import jax
import jax.numpy as jnp
import numpy as np
from jax import lax
from jax.experimental import pallas as pl
from jax.experimental.pallas import tpu as pltpu

F32 = jnp.float32
BF16 = jnp.bfloat16
MESH = pl.DeviceIdType.MESH

D = 1024
N_DEV = 8
EPS = 1e-5
HEAD_PAIRS = 4
ATT_BLK = 128
CONV_K = 31
HALO = 32
D_FF = 2816
FF_TILE_FWD = D_FF // 2
FF_TILE_BWD = D_FF // 2
IN0 = 1792
IN1 = 1536
POOL_WINDOWS = (2, 4, 8, 16)
SGU_CHUNK = 128
GELU_C = 0.7978845608028654
GELU_A = 0.044715
ADAM_LR, ADAM_B1, ADAM_B2, ADAM_EPS, ADAM_WD, ADAM_STEP = 0.001, 0.9, 0.999, 1e-08, 0.01, 10
VMEM_LIMIT = 56 << 20

SMALL_SHARD = 31 * 64 + 3 * 64
W_MISC_ROWS = 16
G_MISC_ROWS = 32
G_SMALL_ROWS = 8
REP_ROWS = G_MISC_ROWS - G_SMALL_ROWS
REP_2D = (("c_w_pool", (64, 1024)), ("c_w_s", (64, 1024)), ("mix_norm", (2, 1024)), ("a_b_in", (1, 1792)), ("a_sinks", (1, 8)),
          ("a_conv_b", (1, 512)), ("a_cln_g", (1, 512)), ("a_cln_b", (1, 512)), ("c_b_s", (4, 128)), ("ffn_norm", (2, 1024)),
          ("final_norm", (1, 1024)))
Q_PERM = (0, 4, 1, 5, 2, 6, 3, 7)


def _params(*sem):
    return pltpu.CompilerParams(dimension_semantics=sem, vmem_limit_bytes=VMEM_LIMIT)


def _nn(a, b):
    return jnp.dot(a, b, preferred_element_type=F32)


def _nt(a, b):
    return lax.dot_general(a, b, (((1,), (1,)), ((), ())), preferred_element_type=F32)


def _tn(a, b):
    return lax.dot_general(a, b, (((0,), (0,)), ((), ())), preferred_element_type=F32)


def _tile(n, want=512):
    t = min(want, n)
    assert n % t == 0, (n, t)
    return t


def _seq_tile(s):
    return 512 if s >= 1024 else s // 2


def _rms(x, g):
    r = lax.rsqrt(jnp.mean(x * x, axis=-1, keepdims=True) + EPS)
    return x * r * g, r


def _rms_bwd(x, g, d_y):
    r = lax.rsqrt(jnp.mean(x * x, axis=-1, keepdims=True) + EPS)
    xr = x * r
    u = d_y * g
    d_x = r * (u - xr * jnp.mean(u * xr, axis=-1, keepdims=True))
    return d_x, jnp.sum(d_y * xr, axis=0, keepdims=True)


def _ln(y, g, b):
    mu = jnp.mean(y, axis=-1, keepdims=True)
    yc = y - mu
    rstd = lax.rsqrt(jnp.mean(yc * yc, axis=-1, keepdims=True) + EPS)
    xhat = yc * rstd
    return xhat * g + b, xhat, rstd


def _ln_bwd(d_o, xhat, rstd, g):
    dxh = d_o * g
    return rstd * (dxh - jnp.mean(dxh, axis=-1, keepdims=True) - xhat * jnp.mean(dxh * xhat, axis=-1, keepdims=True))


def _gelu(x):
    th = jnp.tanh(GELU_C * (x + GELU_A * x * x * x))
    return 0.5 * x * (1.0 + th), th


def _gelu_grad(x, th):
    return 0.5 * (1.0 + th) + 0.5 * x * (1.0 - th * th) * GELU_C * (1.0 + 3.0 * GELU_A * x * x)


def _row(c):
    return pl.BlockSpec((1, c), lambda *_: (0, 0))


def _full(shape):
    return pl.BlockSpec(shape, lambda *_: (0,) * len(shape))


def _norm_proj(h, g, wt, bias, name):
    t, n = h.shape[0], wt.shape[0]
    tm = _tile(t)
    has_bias = bias is not None

    def body(*refs):
        h_ref, g_ref, wt_ref = refs[:3]
        z_ref, hn_ref = refs[-2:]
        hn = _rms(h_ref[...].astype(F32), g_ref[...])[0].astype(BF16)
        hn_ref[...] = hn
        z = _nt(hn, wt_ref[...])
        if has_bias:
            z = z + refs[3][...]
        z_ref[...] = z.astype(BF16)

    in_specs = [pl.BlockSpec((tm, D), lambda i: (i, 0)), _row(D), _full((n, D))]
    args = [h, g, wt]
    if has_bias:
        in_specs.append(_row(n))
        args.append(bias)
    return pl.pallas_call(
        body, name=name, grid=(t // tm,), in_specs=in_specs,
        out_specs=[pl.BlockSpec((tm, n), lambda i: (i, 0)), pl.BlockSpec((tm, D), lambda i: (i, 0))],
        out_shape=[jax.ShapeDtypeStruct((t, n), BF16), jax.ShapeDtypeStruct((t, D), BF16)],
        compiler_params=_params("parallel"))(*args)


def _ff_pieces(tf, width=256):
    return [(c0, min(width, tf - c0)) for c0 in range(0, tf, width)]


def _ffn_fwd(h_prev, a, b, w_out, g, wtg, wtu, wd, name, head=None):
    t = h_prev.shape[0]
    tm, tf = _tile(t), FF_TILE_FWD
    nf = D_FF // tf
    n_head = 0 if head is None else 2

    def body(*refs):
        hp_ref, a_ref, b_ref, wa_ref, wb_ref, g_ref, wtg_ref, wtu_ref, wd_ref = refs[:9]
        hmid_ref, o_ref, hn_ref, gate_ref, up_ref = refs[9 + n_head:14 + n_head]
        acc, act = refs[-2:]
        i, f = pl.program_id(0), pl.program_id(1)

        if head is not None:
            fg_ref, t_ref = refs[9:11]
            dfg_ref, loss_ref = refs[16:18]

            @pl.when((i == 0) & (f == 0))
            def _():
                dfg_ref[...] = jnp.zeros_like(dfg_ref)
                loss_ref[...] = jnp.zeros_like(loss_ref)

        @pl.when(f == 0)
        def _():
            x = hp_ref[...].astype(F32) + _nn(a_ref[...], wa_ref[...]) + _nn(b_ref[...], wb_ref[...])
            hmid_ref[...] = x.astype(BF16)
            hn_ref[...] = _rms(x, g_ref[...])[0].astype(BF16)
            acc[...] = x

        hn = hn_ref[...]
        for c0, cw in _ff_pieces(tf):
            rows, cols = slice(c0, c0 + cw), slice(c0, c0 + cw)
            gate = _nt(hn, wtg_ref[rows, :])
            up = _nt(hn, wtu_ref[rows, :])
            gate_ref[:, cols] = gate.astype(BF16)
            up_ref[:, cols] = up.astype(BF16)
            act[:, cols] = (gate * jax.nn.sigmoid(gate) * up).astype(BF16)
        acc[...] += _nn(act[...], wd_ref[...])

        @pl.when(f == nf - 1)
        def _():
            if head is None:
                o_ref[...] = acc[...].astype(BF16)
            else:
                x, gv = acc[...], fg_ref[...]
                err = _rms(x, gv)[0] - t_ref[...]
                loss_ref[...] += 0.5 * jnp.sum(jnp.mean(err * err, axis=-1, keepdims=True), axis=0, keepdims=True)
                d_x, d_g = _rms_bwd(x, gv, err * (1.0 / D))
                o_ref[...] = d_x.astype(BF16)
                dfg_ref[...] += d_g

    tok = pl.BlockSpec((tm, D), lambda i, f: (i, 0))
    wsp = pl.BlockSpec((tf, D), lambda i, f: (f, 0))
    mid = pl.BlockSpec((tm, tf), lambda i, f: (i, f))
    half = pl.BlockSpec((tm, 512), lambda i, f: (i, 0))
    res = jax.ShapeDtypeStruct((t, D), BF16)
    in_specs = [tok, half, half, pl.BlockSpec((512, D), lambda i, f: (0, 0)), pl.BlockSpec((512, D), lambda i, f: (1, 0)),
                _row(D), wsp, wsp, wsp]
    out_specs = [tok, tok, tok, mid, mid]
    out_shape = [res, res, res, jax.ShapeDtypeStruct((t, D_FF), BF16), jax.ShapeDtypeStruct((t, D_FF), BF16)]
    if head is not None:
        in_specs += [_row(D), tok]
        out_specs += [_row(D), _row(1)]
        out_shape += [jax.ShapeDtypeStruct((1, D), F32), jax.ShapeDtypeStruct((1, 1), F32)]
    return pl.pallas_call(
        body, name=name, grid=(t // tm, nf), in_specs=in_specs, out_specs=out_specs, out_shape=out_shape,
        scratch_shapes=[pltpu.VMEM((tm, D), F32), pltpu.VMEM((tm, tf), BF16)],
        compiler_params=_params("parallel" if head is None else "arbitrary", "arbitrary"),
    )(h_prev, a, b, w_out, w_out, g, wtg, wtu, wd, *(head or ()))


def _ffn_bwd(dh, h, g, gate, up, wtg, wtu, wd, w_out, name):
    t = h.shape[0]
    tm, tf = _tile(t), FF_TILE_BWD
    nf = D_FF // tf

    def body(dh_ref, h_ref, g_ref, gate_ref, up_ref, wtg_ref, wtu_ref, wd_ref, wout_ref,
             dhin_ref, dmix_ref, dgate_ref, dup_ref, act_ref, dg_ref, d_hn):
        i, f = pl.program_id(0), pl.program_id(1)

        @pl.when(f == 0)
        def _():
            d_hn[...] = jnp.zeros_like(d_hn)

        @pl.when((i == 0) & (f == 0))
        def _():
            dg_ref[...] = jnp.zeros_like(dg_ref)

        dh = dh_ref[...]
        for c0, cw in _ff_pieces(tf):
            cols = slice(c0, c0 + cw)
            da = _nt(dh, wd_ref[c0:c0 + cw, :])
            gt = gate_ref[:, cols].astype(F32)
            u = up_ref[:, cols].astype(F32)
            sg = jax.nn.sigmoid(gt)
            sil = gt * sg
            act_ref[:, cols] = (sil * u).astype(BF16)
            dup_ref[:, cols] = (da * sil).astype(BF16)
            dgate_ref[:, cols] = (da * u * sg * (1.0 + gt * (1.0 - sg))).astype(BF16)
        d_hn[...] += _nn(dgate_ref[...], wtg_ref[...]) + _nn(dup_ref[...], wtu_ref[...])

        @pl.when(f == nf - 1)
        def _():
            d_x, d_g = _rms_bwd(h_ref[...].astype(F32), g_ref[...], d_hn[...])
            dhin = (dh_ref[...].astype(F32) + d_x).astype(BF16)
            dhin_ref[...] = dhin
            dmix_ref[...] = _nt(dhin, wout_ref[...]).astype(BF16)
            dg_ref[...] += d_g

    tok = pl.BlockSpec((tm, D), lambda i, f: (i, 0))
    wsp = pl.BlockSpec((tf, D), lambda i, f: (f, 0))
    mid = pl.BlockSpec((tm, tf), lambda i, f: (i, f))
    mid_shape = jax.ShapeDtypeStruct((t, D_FF), BF16)
    res = jax.ShapeDtypeStruct((t, D), BF16)
    return pl.pallas_call(
        body, name=name, grid=(t // tm, nf), in_specs=[tok, tok, _row(D), mid, mid, wsp, wsp, wsp, _full((D, D))],
        out_specs=[tok, tok, mid, mid, mid, _row(D)],
        out_shape=[res, res, mid_shape, mid_shape, mid_shape, jax.ShapeDtypeStruct((1, D), F32)],
        scratch_shapes=[pltpu.VMEM((tm, D), F32)],
        compiler_params=_params("arbitrary", "arbitrary"))(dh, h, g, gate, up, wtg, wtu, wd, w_out)


FF_HALF = D_FF // 2


def _half_spec(part):
    return pl.BlockSpec((FF_HALF, D), lambda i: (part, 0))


def _ffn_half_fwd(hn, wtg_ref, wtu_ref, gate_ref, up_ref, act):
    for c0, cw in _ff_pieces(FF_HALF):
        cols = slice(c0, c0 + cw)
        gate = _nt(hn, wtg_ref[cols, :])
        up = _nt(hn, wtu_ref[cols, :])
        gate_ref[:, cols] = gate.astype(BF16)
        up_ref[:, cols] = up.astype(BF16)
        act[:, cols] = (gate * jax.nn.sigmoid(gate) * up).astype(BF16)


def _ffn_fwd_a(h_prev, a, b, w_out, g, wtg, wtu, wd, name):
    t = h_prev.shape[0]
    tm = _tile(t)

    def body(hp_ref, a_ref, b_ref, wa_ref, wb_ref, g_ref, wtg_ref, wtu_ref, wd_ref,
             h_ref, hn_ref, gate_ref, up_ref, part_ref, act):
        x = hp_ref[...].astype(F32) + _nn(a_ref[...], wa_ref[...]) + _nn(b_ref[...], wb_ref[...])
        h_ref[...] = x.astype(BF16)
        hn = _rms(x, g_ref[...])[0].astype(BF16)
        hn_ref[...] = hn
        _ffn_half_fwd(hn, wtg_ref, wtu_ref, gate_ref, up_ref, act)
        part_ref[...] = x + _nn(act[...], wd_ref[...])

    tok = pl.BlockSpec((tm, D), lambda i: (i, 0))
    half = pl.BlockSpec((tm, 512), lambda i: (i, 0))
    mid = pl.BlockSpec((tm, FF_HALF), lambda i: (i, 0))
    res = jax.ShapeDtypeStruct((t, D), BF16)
    mid_shape = jax.ShapeDtypeStruct((t, FF_HALF), BF16)
    return pl.pallas_call(
        body, name=name, grid=(t // tm,),
        in_specs=[tok, half, half, pl.BlockSpec((512, D), lambda i: (0, 0)), pl.BlockSpec((512, D), lambda i: (1, 0)),
                  _row(D), _half_spec(0), _half_spec(0), _half_spec(0)],
        out_specs=[tok, tok, mid, mid, tok], out_shape=[res, res, mid_shape, mid_shape, jax.ShapeDtypeStruct((t, D), F32)],
        scratch_shapes=[pltpu.VMEM((tm, FF_HALF), BF16)],
        compiler_params=_params("parallel"))(h_prev, a, b, w_out, w_out, g, wtg, wtu, wd)


def _ffn_fwd_b(hn, part, wtg, wtu, wd, name, head=None):
    t = hn.shape[0]
    tm = _tile(t)
    n_head = 0 if head is None else 2

    def body(*refs):
        hn_ref, part_ref, wtg_ref, wtu_ref, wd_ref = refs[:5]
        gate_ref, up_ref, o_ref = refs[5 + n_head:8 + n_head]
        act = refs[-1]
        _ffn_half_fwd(hn_ref[...], wtg_ref, wtu_ref, gate_ref, up_ref, act)
        x = part_ref[...] + _nn(act[...], wd_ref[...])
        if head is None:
            o_ref[...] = x.astype(BF16)
        else:
            fg_ref, t_ref = refs[5:7]
            dfg_ref, loss_ref = refs[10:12]

            @pl.when(pl.program_id(0) == 0)
            def _():
                dfg_ref[...] = jnp.zeros_like(dfg_ref)
                loss_ref[...] = jnp.zeros_like(loss_ref)

            gv = fg_ref[...]
            err = _rms(x, gv)[0] - t_ref[...]
            loss_ref[...] += 0.5 * jnp.sum(jnp.mean(err * err, axis=-1, keepdims=True), axis=0, keepdims=True)
            d_x, d_g = _rms_bwd(x, gv, err * (1.0 / D))
            o_ref[...] = d_x.astype(BF16)
            dfg_ref[...] += d_g

    tok = pl.BlockSpec((tm, D), lambda i: (i, 0))
    mid = pl.BlockSpec((tm, FF_HALF), lambda i: (i, 0))
    mid_shape = jax.ShapeDtypeStruct((t, FF_HALF), BF16)
    in_specs = [tok, tok, _half_spec(1), _half_spec(1), _half_spec(1)]
    out_specs = [mid, mid, tok]
    out_shape = [mid_shape, mid_shape, jax.ShapeDtypeStruct((t, D), BF16)]
    if head is not None:
        in_specs += [_row(D), tok]
        out_specs += [_row(D), _row(1)]
        out_shape += [jax.ShapeDtypeStruct((1, D), F32), jax.ShapeDtypeStruct((1, 1), F32)]
    return pl.pallas_call(
        body, name=name, grid=(t // tm,), in_specs=in_specs, out_specs=out_specs, out_shape=out_shape,
        scratch_shapes=[pltpu.VMEM((tm, FF_HALF), BF16)],
        compiler_params=_params("parallel" if head is None else "arbitrary"))(hn, part, wtg, wtu, wd, *(head or ()))


def _ffn_half_bwd(dh, gate_ref, up_ref, wd_ref, dgate_ref, dup_ref, act_ref):
    for c0, cw in _ff_pieces(FF_HALF):
        cols = slice(c0, c0 + cw)
        da = _nt(dh, wd_ref[cols, :])
        gt = gate_ref[:, cols].astype(F32)
        u = up_ref[:, cols].astype(F32)
        sg = jax.nn.sigmoid(gt)
        sil = gt * sg
        act_ref[:, cols] = (sil * u).astype(BF16)
        dup_ref[:, cols] = (da * sil).astype(BF16)
        dgate_ref[:, cols] = (da * u * sg * (1.0 + gt * (1.0 - sg))).astype(BF16)


def _ffn_bwd_a(dh, gate, up, wtg, wtu, wd, name):
    t = dh.shape[0]
    tm = _tile(t)

    def body(dh_ref, gate_ref, up_ref, wtg_ref, wtu_ref, wd_ref, dgate_ref, dup_ref, act_ref, part_ref):
        _ffn_half_bwd(dh_ref[...], gate_ref, up_ref, wd_ref, dgate_ref, dup_ref, act_ref)
        part_ref[...] = _nn(dgate_ref[...], wtg_ref[...]) + _nn(dup_ref[...], wtu_ref[...])

    tok = pl.BlockSpec((tm, D), lambda i: (i, 0))
    mid = pl.BlockSpec((tm, FF_HALF), lambda i: (i, 0))
    mid_shape = jax.ShapeDtypeStruct((t, FF_HALF), BF16)
    return pl.pallas_call(
        body, name=name, grid=(t // tm,), in_specs=[tok, mid, mid, _half_spec(0), _half_spec(0), _half_spec(0)],
        out_specs=[mid, mid, mid, tok], out_shape=[mid_shape, mid_shape, mid_shape, jax.ShapeDtypeStruct((t, D), F32)],
        compiler_params=_params("parallel"))(dh, gate, up, wtg, wtu, wd)


def _ffn_bwd_b(dh, h, g, gate, up, wtg, wtu, wd, w_out, part, name):
    t = dh.shape[0]
    tm = _tile(t)

    def body(dh_ref, h_ref, g_ref, gate_ref, up_ref, wtg_ref, wtu_ref, wd_ref, wout_ref, part_ref,
             dhin_ref, dmix_ref, dgate_ref, dup_ref, act_ref, dg_ref):
        @pl.when(pl.program_id(0) == 0)
        def _():
            dg_ref[...] = jnp.zeros_like(dg_ref)

        _ffn_half_bwd(dh_ref[...], gate_ref, up_ref, wd_ref, dgate_ref, dup_ref, act_ref)
        d_hn = part_ref[...] + _nn(dgate_ref[...], wtg_ref[...]) + _nn(dup_ref[...], wtu_ref[...])
        d_x, d_g = _rms_bwd(h_ref[...].astype(F32), g_ref[...], d_hn)
        dhin = (dh_ref[...].astype(F32) + d_x).astype(BF16)
        dhin_ref[...] = dhin
        dmix_ref[...] = _nt(dhin, wout_ref[...]).astype(BF16)
        dg_ref[...] += d_g

    tok = pl.BlockSpec((tm, D), lambda i: (i, 0))
    mid = pl.BlockSpec((tm, FF_HALF), lambda i: (i, 0))
    mid_shape = jax.ShapeDtypeStruct((t, FF_HALF), BF16)
    res = jax.ShapeDtypeStruct((t, D), BF16)
    return pl.pallas_call(
        body, name=name, grid=(t // tm,),
        in_specs=[tok, tok, _row(D), mid, mid, _half_spec(1), _half_spec(1), _half_spec(1), _full((D, D)), tok],
        out_specs=[tok, tok, mid, mid, mid, _row(D)],
        out_shape=[res, res, mid_shape, mid_shape, mid_shape, jax.ShapeDtypeStruct((1, D), F32)],
        compiler_params=_params("arbitrary"))(dh, h, g, gate, up, wtg, wtu, wd, w_out, part)


def _mm_tn_halves(a0, a1, b, name):
    t, n = a0.shape
    k = b.shape[1]
    tt = _tile(t, 1024)
    nt = t // tt

    def body(a0_ref, a1_ref, b_ref, o_ref, acc):
        j, s = pl.program_id(0), pl.program_id(1)

        @pl.when(s == 0)
        def _():
            acc[...] = jnp.zeros_like(acc)

        bb = b_ref[...].astype(BF16)

        @pl.when(j == 0)
        def _():
            acc[...] += _tn(a0_ref[...], bb)

        @pl.when(j == 1)
        def _():
            acc[...] += _tn(a1_ref[...], bb)

        @pl.when(s == nt - 1)
        def _():
            o_ref[...] = acc[...].astype(BF16)

    return pl.pallas_call(
        body, name=name, grid=(2, nt),
        in_specs=[pl.BlockSpec((tt, n), lambda j, s: (jnp.where(j == 0, s, nt - 1), 0)),
                  pl.BlockSpec((tt, n), lambda j, s: (jnp.where(j == 1, s, 0), 0)),
                  pl.BlockSpec((tt, k), lambda j, s: (s, 0))],
        out_specs=pl.BlockSpec((n, k), lambda j, s: (j, 0)), out_shape=jax.ShapeDtypeStruct((2 * n, k), BF16),
        scratch_shapes=[pltpu.VMEM((n, k), F32)],
        compiler_params=_params("parallel", "arbitrary"))(a0, a1, b)


def _proj_bwd_norm(pieces, wt, h, dh, g, dtype, name):
    t = h.shape[0]
    tm = _tile(t)
    n_p = len(pieces)

    def body(*refs):
        p_refs, w_refs = refs[:n_p], refs[n_p:2 * n_p]
        h_ref, dh_ref, g_ref, o_ref, dg_ref = refs[2 * n_p:]

        @pl.when(pl.program_id(0) == 0)
        def _():
            dg_ref[...] = jnp.zeros_like(dg_ref)

        d_hn = _nn(p_refs[0][...], w_refs[0][...])
        for p_ref, w_ref in zip(p_refs[1:], w_refs[1:]):
            d_hn = d_hn + _nn(p_ref[...], w_ref[...])
        d_x, d_g = _rms_bwd(h_ref[...].astype(F32), g_ref[...], d_hn)
        o_ref[...] = (dh_ref[...].astype(F32) + d_x).astype(dtype)
        dg_ref[...] += d_g

    tok = pl.BlockSpec((tm, D), lambda i: (i, 0))
    in_specs = [pl.BlockSpec((tm, a.shape[1]), lambda i: (i, 0)) for a, _ in pieces]
    for a, off in pieces:
        w = a.shape[1]
        assert off % w == 0
        in_specs.append(pl.BlockSpec((w, D), functools.partial(lambda i, blk: (blk, 0), blk=off // w)))
    in_specs += [tok, tok, _row(D)]
    return pl.pallas_call(
        body, name=name, grid=(t // tm,), in_specs=in_specs, out_specs=[tok, _row(D)],
        out_shape=[jax.ShapeDtypeStruct((t, D), dtype), jax.ShapeDtypeStruct((1, D), F32)],
        compiler_params=_params("arbitrary"))(*[a for a, _ in pieces], *([wt] * n_p), h, dh, g)


def _mm_tn(a, b, name):
    t, n = a.shape
    k = b.shape[1]
    tn = n if n <= 1024 else n // 2
    tt = _tile(t, 1024)
    nt = t // tt

    def body(a_ref, b_ref, o_ref, acc):
        s = pl.program_id(1)

        @pl.when(s == 0)
        def _():
            acc[...] = jnp.zeros_like(acc)

        acc[...] += _tn(a_ref[...], b_ref[...].astype(BF16))

        @pl.when(s == nt - 1)
        def _():
            o_ref[...] = acc[...].astype(BF16)

    return pl.pallas_call(
        body, name=name, grid=(n // tn, nt),
        in_specs=[pl.BlockSpec((tt, tn), lambda j, s: (s, j)), pl.BlockSpec((tt, k), lambda j, s: (s, 0))],
        out_specs=pl.BlockSpec((tn, k), lambda j, s: (j, 0)), out_shape=jax.ShapeDtypeStruct((n, k), BF16),
        scratch_shapes=[pltpu.VMEM((tn, k), F32)],
        compiler_params=_params("parallel", "arbitrary"))(a, b)


def _mm_tn_pieces(pieces, b, name):
    t, k = b.shape
    widths = [p.shape[1] for p in pieces]
    n, n_p = sum(widths), len(pieces)
    tt = _tile(t, 1024)
    nt = t // tt

    def body(*refs):
        b_ref, o_ref, acc = refs[n_p:]
        s = pl.program_id(0)

        @pl.when(s == 0)
        def _():
            acc[...] = jnp.zeros_like(acc)

        bb = b_ref[...].astype(BF16)
        off = 0
        for p_ref, w in zip(refs[:n_p], widths):
            acc[off:off + w, :] += _tn(p_ref[...], bb)
            off += w

        @pl.when(s == nt - 1)
        def _():
            o_ref[...] = acc[...].astype(BF16)

    return pl.pallas_call(
        body, name=name, grid=(nt,),
        in_specs=[pl.BlockSpec((tt, w), lambda s: (s, 0)) for w in widths] + [pl.BlockSpec((tt, k), lambda s: (s, 0))],
        out_specs=_full((n, k)), out_shape=jax.ShapeDtypeStruct((n, k), BF16),
        scratch_shapes=[pltpu.VMEM((n, k), F32)], compiler_params=_params("arbitrary"))(*pieces, b)


STACK = HEAD_PAIRS * ATT_BLK


def _attn_valid(first, rows):
    qi = lax.broadcasted_iota(jnp.int32, (rows, 2 * ATT_BLK), 0) % ATT_BLK
    r = lax.broadcasted_iota(jnp.int32, (rows, 2 * ATT_BLK), 1)
    dist = qi + ATT_BLK - r
    return (dist >= 0) & (dist < ATT_BLK) & ((r >= ATT_BLK) | jnp.logical_not(first))


def _stacked(ref, kh, scale):
    lo = lax.broadcasted_iota(jnp.int32, (ATT_BLK, 128), 1) < 64
    keep = lo if kh == 0 else ~lo
    parts = [jnp.where(keep, ref[:, g * 128:(g + 1) * 128] * scale, 0.0).astype(BF16) for g in range(HEAD_PAIRS)]
    return jnp.concatenate(parts, axis=0)


def _unstacked(a0, a1, g):
    lo = lax.broadcasted_iota(jnp.int32, (ATT_BLK, 128), 1) < 64
    rows = slice(g * ATT_BLK, (g + 1) * ATT_BLK)
    return jnp.where(lo, a0[rows], a1[rows])


def _sink_rows(s_ref, kh):
    return jnp.concatenate([jnp.full((ATT_BLK, 128), s_ref[0, kh * 4 + g], F32) for g in range(HEAD_PAIRS)], axis=0)


def _row_sums(a, split):
    hi = a.astype(BF16)
    ones = jnp.ones((2 * ATT_BLK, 128), BF16)
    if not split:
        return _nn(hi, ones)
    lo = (a - hi.astype(F32)).astype(BF16)
    return _nn(hi, ones) + _nn(lo, ones)


def _both(a):
    return jnp.concatenate([a, a], axis=1)


def _attn_probs(qs, kpair, sink, valid):
    s = jnp.where(valid, _nt(qs, kpair), -1e30)
    m = jnp.maximum(jnp.broadcast_to(jnp.max(s, axis=-1, keepdims=True), (s.shape[0], 128)), sink)
    p = jnp.exp(s - _both(m))
    es = jnp.exp(sink - m)
    inv = 1.0 / (_row_sums(p, split=True) + es)
    return p * _both(inv), es * inv


def _attn_probs_head(qm, kpair, sink, valid):
    s = jnp.where(valid, _nt(qm, kpair), -1e30)
    m = jnp.maximum(jnp.max(s, axis=-1, keepdims=True), sink)
    p = jnp.exp(s - m)
    return p * (1.0 / (jnp.sum(p, axis=-1, keepdims=True) + jnp.exp(sink - m)))


def _attn_specs(bsz, order):
    q = pl.BlockSpec((bsz, ATT_BLK, 512), lambda j: (0, order(j), 0))
    kvc = pl.BlockSpec((bsz, ATT_BLK, 256), lambda j: (0, order(j), 6))
    kvp = pl.BlockSpec((bsz, ATT_BLK, 256), lambda j: (0, jnp.maximum(order(j) - 1, 0), 6))
    return q, kvc, kvp


def _window_kv(kvc_ref, kvp_ref):
    kvc, kvp = kvc_ref[...], kvp_ref[...]
    kpair = jnp.concatenate([kvp[:, :128], kvc[:, :128]], axis=0)
    vpair = jnp.concatenate([kvp[:, 128:], kvc[:, 128:]], axis=0)
    return kpair, vpair


def _attn_fwd(z0, sinks, bsz, name):
    t = z0.shape[0]
    seq = t // bsz
    nb = seq // ATT_BLK

    def body(s_ref, q_ref, kvc_ref, kvp_ref, o_ref, token):
        token[...] = jnp.zeros_like(token)
        valid = _attn_valid(pl.program_id(0) == 0, ATT_BLK)
        lo = lax.broadcasted_iota(jnp.int32, (ATT_BLK, 128), 1) < 64
        for b in range(bsz):
            kpair, vpair = _window_kv(kvc_ref.at[b], kvp_ref.at[b])
            for g in range(HEAD_PAIRS):
                qs = q_ref[b, :, g * 128:(g + 1) * 128] * 0.125
                outs = []
                for kh in range(2):
                    qm = jnp.where(lo if kh == 0 else ~lo, qs, 0.0).astype(BF16)
                    p = _attn_probs_head(qm, kpair, s_ref[0, kh * 4 + g], valid)
                    outs.append(_nn(p.astype(BF16), vpair))
                o_ref[b, :, g * 128:(g + 1) * 128] = jnp.where(lo, outs[0], outs[1]).astype(BF16)

    q, kvc, kvp = _attn_specs(bsz, lambda j: j)
    z3 = z0.reshape(bsz, seq, z0.shape[1])
    out, token = pl.pallas_call(
        body, name=name, grid=(nb,),
        in_specs=[pl.BlockSpec(memory_space=pltpu.SMEM), q, kvc, kvp],
        out_specs=[pl.BlockSpec((bsz, ATT_BLK, 512), lambda j: (0, j, 0)), _full((8, 128))],
        out_shape=[jax.ShapeDtypeStruct((bsz, seq, 512), BF16), jax.ShapeDtypeStruct((8, 128), F32)],
        compiler_params=_params("arbitrary"))(sinks, z3, z3, z3)
    return out.reshape(t, 512), token


def _attn_bwd(z0, dmix, sinks, bsz, name):
    t = z0.shape[0]
    seq = t // bsz
    nb = seq // ATT_BLK

    def body(s_ref, q_ref, kvc_ref, kvp_ref, do_ref, dq_ref, dkv_ref, dsink_ref, dbq_ref, dbkv_ref, carry):
        j = pl.program_id(0)

        @pl.when(j == 0)
        def _():
            carry[...] = jnp.zeros_like(carry)
            dsink_ref[...] = jnp.zeros_like(dsink_ref)
            dbq_ref[...] = jnp.zeros_like(dbq_ref)
            dbkv_ref[...] = jnp.zeros_like(dbkv_ref)

        valid = _attn_valid(j == nb - 1, STACK)
        lane = lax.broadcasted_iota(jnp.int32, (1, 128), 1)
        dsink = jnp.zeros((1, 128), F32)
        dbq = [jnp.zeros((1, 128), F32)] * HEAD_PAIRS
        dbkv = jnp.zeros((1, 256), F32)
        for b in range(bsz):
            kpair, vpair = _window_kv(kvc_ref.at[b], kvp_ref.at[b])
            dk = jnp.zeros((2 * ATT_BLK, 128), F32)
            dv = jnp.zeros((2 * ATT_BLK, 128), F32)
            dqs = []
            for kh in range(2):
                qs = _stacked(q_ref.at[b], kh, 0.125)
                dos = _stacked(do_ref.at[b], kh, 1.0)
                p, ps = _attn_probs(qs, kpair, _sink_rows(s_ref, kh), valid)
                dp = _nt(dos, vpair)
                delta = _row_sums(p * dp, split=False)
                ds = (p * (dp - _both(delta))).astype(BF16)
                dqs.append(_nn(ds, kpair))
                dk = dk + _tn(ds, qs)
                dv = dv + _tn(p.astype(BF16), dos)
                psd = ps * delta
                for g in range(HEAD_PAIRS):
                    part = jnp.sum(psd[g * ATT_BLK:(g + 1) * ATT_BLK], axis=0, keepdims=True)
                    dsink = dsink - jnp.where(lane == kh * 4 + g, part, 0.0)
            for g in range(HEAD_PAIRS):
                dq = _unstacked(dqs[0], dqs[1], g) * 0.125
                dq_ref[b, :, g * 128:(g + 1) * 128] = dq.astype(BF16)
                dbq[g] = dbq[g] + jnp.sum(dq, axis=0, keepdims=True)
            dkv = jnp.concatenate([dk[ATT_BLK:], dv[ATT_BLK:]], axis=1) + carry[b]
            dkv_ref[b] = dkv.astype(BF16)
            dbkv = dbkv + jnp.sum(dkv, axis=0, keepdims=True)
            carry[b] = jnp.concatenate([dk[:ATT_BLK], dv[:ATT_BLK]], axis=1)
        dsink_ref[...] += dsink
        dbq_ref[...] += jnp.concatenate(dbq, axis=1)
        dbkv_ref[...] += dbkv

    q, kvc, kvp = _attn_specs(bsz, lambda j: nb - 1 - j)
    z3 = z0.reshape(bsz, seq, z0.shape[1])
    d3 = dmix.reshape(bsz, seq, dmix.shape[1])
    dq, dkv, dsink, dbq, dbkv = pl.pallas_call(
        body, name=name, grid=(nb,),
        in_specs=[pl.BlockSpec(memory_space=pltpu.SMEM), q, kvc, kvp,
                  pl.BlockSpec((bsz, ATT_BLK, 512), lambda j: (0, nb - 1 - j, 0))],
        out_specs=[pl.BlockSpec((bsz, ATT_BLK, 512), lambda j: (0, nb - 1 - j, 0)),
                   pl.BlockSpec((bsz, ATT_BLK, 256), lambda j: (0, nb - 1 - j, 0)), _row(128), _row(512), _row(256)],
        out_shape=[jax.ShapeDtypeStruct((bsz, seq, 512), BF16), jax.ShapeDtypeStruct((bsz, seq, 256), BF16),
                   jax.ShapeDtypeStruct((1, 128), F32), jax.ShapeDtypeStruct((1, 512), F32),
                   jax.ShapeDtypeStruct((1, 256), F32)],
        scratch_shapes=[pltpu.VMEM((bsz, ATT_BLK, 256), F32)],
        compiler_params=_params("arbitrary"))(sinks, z3, z3, z3, d3)
    return dq.reshape(t, 512), dkv.reshape(t, 256), dsink, dbq, dbkv


def _seq_specs(ts, nt, t, width, col):
    per = ts // HALO
    cur = pl.BlockSpec((ts, width), lambda b, i: (b * nt + i, col))
    prev = pl.BlockSpec((HALO, width), lambda b, i: (jnp.maximum((b * nt + i) * per - 1, 0), col))
    nxt = pl.BlockSpec((HALO, width), lambda b, i: (jnp.minimum((b * nt + i + 1) * per, t // HALO - 1), col))
    return prev, cur, nxt


SUB = 8
CONV_ROWS = 64


def _shifted_copies(src, sh, rows_first, rows_rest):
    for r in range(SUB):
        rows = rows_first if r == 0 else rows_rest
        sh[r, pl.ds(0, rows), :] = src[pl.ds(r, rows), :]


def _tap_sum(sh, w, offset, c0, rows):
    acc = None
    for k in range(CONV_K):
        o = offset(k)
        term = sh[o % SUB, pl.ds(c0 + o - o % SUB, rows), :] * w[k:k + 1, :]
        acc = term if acc is None else acc + term
    return acc


def _glu_rows(a_ref, g_ref, rows=slice(None)):
    return a_ref[rows, :].astype(F32) * jax.nn.sigmoid(g_ref[rows, :].astype(F32))


def _conv_fwd(z0, conv_w, conv_b, ln_g, ln_b, bsz, name):
    t = z0.shape[0]
    s = t // bsz
    ts = _seq_tile(s)
    nt = s // ts
    first = HALO - (CONV_K - 1)

    def body(ap_ref, ac_ref, gp_ref, gc_ref, w_ref, cb_ref, lg_ref, lb_ref, o_ref, y_ref, hbuf, sh):
        hbuf[0:HALO, :] = jnp.where(pl.program_id(1) > 0, _glu_rows(ap_ref, gp_ref), 0.0)
        hbuf[HALO:HALO + ts, :] = _glu_rows(ac_ref, gc_ref)
        _shifted_copies(hbuf, sh, ts + HALO, ts + HALO - SUB)
        w, cb, lg, lb = w_ref[...], cb_ref[...], lg_ref[...], lb_ref[...]
        for c0 in range(0, ts, CONV_ROWS):
            y = _tap_sum(sh, w, lambda k: first + k, c0, CONV_ROWS) + cb
            y_ref[c0:c0 + CONV_ROWS, :] = y
            o = _ln(y, lg, lb)[0]
            o_ref[c0:c0 + CONV_ROWS, :] = (o * jax.nn.sigmoid(o)).astype(BF16)

    ap, ac, _ = _seq_specs(ts, nt, t, 512, 1)
    gp, gc, _ = _seq_specs(ts, nt, t, 512, 2)
    tile = pl.BlockSpec((ts, 512), lambda b, i: (b * nt + i, 0))
    return pl.pallas_call(
        body, name=name, grid=(bsz, nt),
        in_specs=[ap, ac, gp, gc, _full((HALO, 512)), _row(512), _row(512), _row(512)],
        out_specs=[tile, tile],
        out_shape=[jax.ShapeDtypeStruct((t, 512), BF16), jax.ShapeDtypeStruct((t, 512), F32)],
        scratch_shapes=[pltpu.VMEM((HALO + ts, 512), F32), pltpu.VMEM((SUB, HALO + ts, 512), F32)],
        compiler_params=_params("parallel", "parallel"))(z0, z0, z0, z0, conv_w, conv_b, ln_g, ln_b)


def _conv_bwd(z0, y, dmix, conv_w, ln_g, ln_b, bsz, name):
    t = z0.shape[0]
    s = t // bsz
    ts = _seq_tile(s)
    nt = s // ts

    def body(ac_ref, gc_ref, yc_ref, yn_ref, dc_ref, dn_ref, w_ref, lg_ref, lb_ref,
             da_ref, dg_ref, dw_ref, dcb_ref, dlg_ref, dlb_ref, dba_ref, dbg_ref, hcur, dybuf, sh_dy):
        b, i = pl.program_id(0), pl.program_id(1)

        @pl.when((b == 0) & (i == 0))
        def _():
            for ref in (dw_ref, dcb_ref, dlg_ref, dlb_ref, dba_ref, dbg_ref):
                ref[...] = jnp.zeros_like(ref)

        w, lg, lb = w_ref[...], lg_ref[...], lb_ref[...]
        hcur[...] = _glu_rows(ac_ref, gc_ref)

        def d_conv_out(yv, dout):
            o, xhat, rstd = _ln(yv, lg, lb)
            sg_o = jax.nn.sigmoid(o)
            d_o = dout * sg_o * (1.0 + o * (1.0 - sg_o))
            return _ln_bwd(d_o, xhat, rstd, lg), d_o * xhat, d_o

        dlg = jnp.zeros((1, 512), F32)
        dlb = jnp.zeros((1, 512), F32)
        dcb = jnp.zeros((1, 512), F32)
        for c0 in range(0, ts, CONV_ROWS):
            rows = slice(c0, c0 + CONV_ROWS)
            dy, g_part, b_part = d_conv_out(yc_ref[rows, :], dc_ref[rows, :].astype(F32))
            dybuf[rows, :] = dy
            dlg = dlg + jnp.sum(g_part, axis=0, keepdims=True)
            dlb = dlb + jnp.sum(b_part, axis=0, keepdims=True)
            dcb = dcb + jnp.sum(dy, axis=0, keepdims=True)
        dn = jnp.where(i < nt - 1, dn_ref[...].astype(F32), 0.0)
        dybuf[ts:ts + HALO, :] = d_conv_out(yn_ref[...], dn)[0]
        dlg_ref[...] += dlg
        dlb_ref[...] += dlb
        dcb_ref[...] += dcb
        _shifted_copies(dybuf, sh_dy, ts + HALO - SUB, ts + HALO - SUB)

        for k in range(CONV_K):
            o = CONV_K - 1 - k
            prod = hcur[...] * sh_dy[o % SUB, pl.ds(o - o % SUB, ts), :]
            dw_ref[pl.ds(k, 1), :] += jnp.sum(prod, axis=0, keepdims=True)
        dba = jnp.zeros((1, 512), F32)
        dbg = jnp.zeros((1, 512), F32)
        for c0 in range(0, ts, CONV_ROWS):
            rows = slice(c0, c0 + CONV_ROWS)
            dh = _tap_sum(sh_dy, w, lambda k: CONV_K - 1 - k, c0, CONV_ROWS)
            a_c = ac_ref[rows, :].astype(F32)
            sg_c = jax.nn.sigmoid(gc_ref[rows, :].astype(F32))
            d_a = dh * sg_c
            d_g = dh * a_c * sg_c * (1.0 - sg_c)
            da_ref[rows, :] = d_a.astype(BF16)
            dg_ref[rows, :] = d_g.astype(BF16)
            dba = dba + jnp.sum(d_a, axis=0, keepdims=True)
            dbg = dbg + jnp.sum(d_g, axis=0, keepdims=True)
        dba_ref[...] += dba
        dbg_ref[...] += dbg

    _, ac, _ = _seq_specs(ts, nt, t, 512, 1)
    _, gc, _ = _seq_specs(ts, nt, t, 512, 2)
    _, yc, yn = _seq_specs(ts, nt, t, 512, 0)
    _, dc, dn = _seq_specs(ts, nt, t, 512, 1)
    tile = pl.BlockSpec((ts, 512), lambda b, i: (b * nt + i, 0))
    vec = jax.ShapeDtypeStruct((1, 512), F32)
    return pl.pallas_call(
        body, name=name, grid=(bsz, nt),
        in_specs=[ac, gc, yc, yn, dc, dn, _full((HALO, 512)), _row(512), _row(512)],
        out_specs=[tile, tile, _full((HALO, 512)), _row(512), _row(512), _row(512), _row(512), _row(512)],
        out_shape=[jax.ShapeDtypeStruct((t, 512), BF16), jax.ShapeDtypeStruct((t, 512), BF16),
                   jax.ShapeDtypeStruct((HALO, 512), F32), vec, vec, vec, vec, vec],
        scratch_shapes=[pltpu.VMEM((ts, 512), F32), pltpu.VMEM((ts + HALO, 512), F32),
                        pltpu.VMEM((SUB, HALO + ts, 512), F32)],
        compiler_params=_params("arbitrary", "arbitrary"))(z0, z0, y, y, dmix, dmix, conv_w, ln_g, ln_b)


def _pooled(pbuf, g, ts, tok):
    w = 2 << g
    cols = slice(128 * g, 128 * (g + 1))
    sm = pbuf[pl.ds(HALO, ts), cols]
    for d in range(1, w):
        sm = sm + pbuf[pl.ds(HALO - d, ts), cols]
    cnt = jnp.minimum(tok + 1, w).astype(F32)
    return sm / cnt - pbuf[pl.ds(HALO, ts), cols]


def _pool_fwd(z1, w_pool, scale, bsz, name):
    t = z1.shape[0]
    s = t // bsz
    ts = _seq_tile(s)
    nt = s // ts

    def body(zp_ref, zc_ref, wp_ref, sc_ref, o_ref, token, pbuf):
        token[...] = jnp.zeros_like(token)
        i = pl.program_id(1)
        pbuf[0:HALO, :] = jnp.where(i > 0, zp_ref[...].astype(F32), 0.0)
        pbuf[HALO:HALO + ts, :] = zc_ref[...].astype(F32)
        tok = i * ts + lax.broadcasted_iota(jnp.int32, (ts, 1), 0)
        for g in range(4):
            cols = slice(128 * g, 128 * (g + 1))
            pooled = _pooled(pbuf, g, ts, tok).astype(BF16)
            o_ref[:, cols] = (_nn(pooled, wp_ref[g].astype(BF16)) * sc_ref[:, cols]).astype(BF16)

    zp, zc, _ = _seq_specs(ts, nt, t, 512, 0)
    return pl.pallas_call(
        body, name=name, grid=(bsz, nt), in_specs=[zp, zc, _full((4, 128, 128)), _row(512)],
        out_specs=[pl.BlockSpec((ts, 512), lambda b, i: (b * nt + i, 0)), _full((8, 128))],
        out_shape=[jax.ShapeDtypeStruct((t, 512), BF16), jax.ShapeDtypeStruct((8, 128), F32)],
        scratch_shapes=[pltpu.VMEM((HALO + ts, 512), F32)],
        compiler_params=_params("arbitrary", "arbitrary"))(z1, z1, w_pool, scale)


def _pool_bwd(z1, dmix, w_pool, scale, bsz, name):
    t = z1.shape[0]
    s = t // bsz
    ts = _seq_tile(s)
    nt = s // ts
    rr = ts + HALO

    def body(zp_ref, zc_ref, dc_ref, dn_ref, wp_ref, sc_ref, dz_ref, dwp_ref, dsc_ref, pbuf, ebuf):
        b, i = pl.program_id(0), pl.program_id(1)

        @pl.when((b == 0) & (i == 0))
        def _():
            dwp_ref[...] = jnp.zeros_like(dwp_ref)
            dsc_ref[...] = jnp.zeros_like(dsc_ref)

        pbuf[0:HALO, :] = jnp.where(i > 0, zp_ref[...].astype(F32), 0.0)
        pbuf[HALO:HALO + ts, :] = zc_ref[...].astype(F32)
        dn = jnp.where(i < nt - 1, dn_ref[...].astype(F32), 0.0)
        dout = jnp.concatenate([dc_ref[...].astype(F32), dn], axis=0)
        tok = i * ts + lax.broadcasted_iota(jnp.int32, (ts, 1), 0)
        tok_r = i * ts + lax.broadcasted_iota(jnp.int32, (rr, 1), 0)
        for g in range(4):
            w = 2 << g
            cols = slice(128 * g, 128 * (g + 1))
            wg = wp_ref[g].astype(BF16)
            pooled = _pooled(pbuf, g, ts, tok).astype(BF16)
            dsc_ref[:, cols] += jnp.sum(dout[:ts, cols] * _nn(pooled, wg), axis=0, keepdims=True)
            dy = (dout[:, cols] * sc_ref[:, cols]).astype(BF16)
            dwp_ref[g] += _tn(pooled, dy[:ts])
            dpl = _nt(dy, wg)
            ebuf[...] = dpl / jnp.minimum(tok_r + 1, w).astype(F32)
            dz = ebuf[pl.ds(0, ts), :] - dpl[:ts]
            for d in range(1, w):
                dz = dz + ebuf[pl.ds(d, ts), :]
            dz_ref[:, cols] = dz.astype(BF16)

    zp, zc, _ = _seq_specs(ts, nt, t, 512, 0)
    _, dc, dn = _seq_specs(ts, nt, t, 512, 0)
    return pl.pallas_call(
        body, name=name, grid=(bsz, nt), in_specs=[zp, zc, dc, dn, _full((4, 128, 128)), _row(512)],
        out_specs=[pl.BlockSpec((ts, 512), lambda b, i: (b * nt + i, 0)), _full((4, 128, 128)), _row(512)],
        out_shape=[jax.ShapeDtypeStruct((t, 512), BF16), jax.ShapeDtypeStruct((4, 128, 128), F32),
                   jax.ShapeDtypeStruct((1, 512), F32)],
        scratch_shapes=[pltpu.VMEM((HALO + ts, 512), F32), pltpu.VMEM((rr, 128), F32)],
        compiler_params=_params("arbitrary", "arbitrary"))(z1, z1, dmix, dmix, w_pool, scale)


def _tril():
    r = lax.broadcasted_iota(jnp.int32, (SGU_CHUNK, SGU_CHUNK), 0)
    c = lax.broadcasted_iota(jnp.int32, (SGU_CHUNK, SGU_CHUNK), 1)
    return r >= c


def _sgu_fwd(z1, ln_g, ln_b, w_s, b_rows, name):
    t = z1.shape[0]
    ts = _tile(t)

    def body(zu_ref, zv_ref, lg_ref, lb_ref, ws_ref, bs_ref, o_ref):
        v = _gelu(zv_ref[...].astype(F32))[0]
        vb = _ln(v, lg_ref[...], lb_ref[...])[0].astype(BF16)
        tril = _tril()
        for g in range(4):
            cols = slice(128 * g, 128 * (g + 1))
            wg = jnp.where(tril, ws_ref[g], 0.0).astype(BF16)
            for c in range(ts // SGU_CHUNK):
                rows = slice(SGU_CHUNK * c, SGU_CHUNK * (c + 1))
                mixed = _nn(wg, vb[rows, cols]) + bs_ref[g]
                o_ref[rows, cols] = (_gelu(zu_ref[rows, cols].astype(F32))[0] * mixed).astype(BF16)

    return pl.pallas_call(
        body, name=name, grid=(t // ts,),
        in_specs=[pl.BlockSpec((ts, 512), lambda i: (i, 1)), pl.BlockSpec((ts, 512), lambda i: (i, 2)),
                  _row(512), _row(512), _full((4, 128, 128)), _full((4, 128, 128))],
        out_specs=pl.BlockSpec((ts, 512), lambda i: (i, 0)), out_shape=jax.ShapeDtypeStruct((t, 512), BF16),
        compiler_params=_params("parallel"))(z1, z1, ln_g, ln_b, w_s, b_rows)


def _sgu_bwd(z1, dmix, ln_g, ln_b, w_s, b_rows, name):
    t = z1.shape[0]
    ts = _tile(t)

    def body(zu_ref, zv_ref, d_ref, lg_ref, lb_ref, ws_ref, bs_ref,
             dzu_ref, dzv_ref, dws_ref, dbs_ref, dlg_ref, dlb_ref, dvbuf):
        @pl.when(pl.program_id(0) == 0)
        def _():
            for ref in (dws_ref, dbs_ref, dlg_ref, dlb_ref):
                ref[...] = jnp.zeros_like(ref)

        zv = zv_ref[...].astype(F32)
        v, thv = _gelu(zv)
        lg = lg_ref[...]
        vln, xhat, rstd = _ln(v, lg, lb_ref[...])
        vb = vln.astype(BF16)
        tril = _tril()
        for g in range(4):
            cols = slice(128 * g, 128 * (g + 1))
            wg = jnp.where(tril, ws_ref[g], 0.0).astype(BF16)
            dws = jnp.zeros((SGU_CHUNK, SGU_CHUNK), F32)
            dbs = jnp.zeros((1, SGU_CHUNK), F32)
            for c in range(ts // SGU_CHUNK):
                rows = slice(SGU_CHUNK * c, SGU_CHUNK * (c + 1))
                vbc = vb[rows, cols]
                mixed = _nn(wg, vbc) + bs_ref[g]
                zu = zu_ref[rows, cols].astype(F32)
                u, thu = _gelu(zu)
                dout = d_ref[rows, cols].astype(F32)
                dzu_ref[rows, cols] = (dout * mixed * _gelu_grad(zu, thu)).astype(BF16)
                dm = dout * u
                dmb = dm.astype(BF16)
                dws = dws + _nt(dmb, vbc)
                dbs = dbs + jnp.sum(dm.T, axis=0, keepdims=True)
                dvbuf[rows, cols] = _tn(wg, dmb)
            dws_ref[g] += jnp.where(tril, dws, 0.0)
            dbs_ref[pl.ds(g, 1), :] += dbs
        dvln = dvbuf[...]
        dlg_ref[...] += jnp.sum(dvln * xhat, axis=0, keepdims=True)
        dlb_ref[...] += jnp.sum(dvln, axis=0, keepdims=True)
        dzv_ref[...] = (_ln_bwd(dvln, xhat, rstd, lg) * _gelu_grad(zv, thv)).astype(BF16)

    tile = pl.BlockSpec((ts, 512), lambda i: (i, 0))
    vec = jax.ShapeDtypeStruct((1, 512), F32)
    return pl.pallas_call(
        body, name=name, grid=(t // ts,),
        in_specs=[pl.BlockSpec((ts, 512), lambda i: (i, 1)), pl.BlockSpec((ts, 512), lambda i: (i, 2)),
                  pl.BlockSpec((ts, 512), lambda i: (i, 1)), _row(512), _row(512), _full((4, 128, 128)),
                  _full((4, 128, 128))],
        out_specs=[tile, tile, _full((4, 128, 128)), _full((4, 128)), _row(512), _row(512)],
        out_shape=[jax.ShapeDtypeStruct((t, 512), BF16), jax.ShapeDtypeStruct((t, 512), BF16),
                   jax.ShapeDtypeStruct((4, 128, 128), F32), jax.ShapeDtypeStruct((4, 128), F32), vec, vec],
        scratch_shapes=[pltpu.VMEM((ts, 512), F32)],
        compiler_params=_params("arbitrary"))(z1, z1, dmix, ln_g, ln_b, w_s, b_rows)


def _row_tile(r):
    for cand in (512, 352, 256, 192, 128, 64, 32, 16, 8):
        if r % cand == 0:
            return cand
    return r


def _sum_slabs(a, name):
    k, r, c = a.shape
    tr = _row_tile(r)

    def body(*refs):
        acc = refs[0][...].astype(F32)
        for ref in refs[1:-1]:
            acc = acc + ref[...].astype(F32)
        refs[-1][...] = acc

    in_specs = [pl.BlockSpec((None, tr, c), functools.partial(lambda i, s: (s, i, 0), s=s)) for s in range(k)]
    return pl.pallas_call(
        body, name=name, grid=(r // tr,), in_specs=in_specs, out_specs=pl.BlockSpec((tr, c), lambda i: (i, 0)),
        out_shape=jax.ShapeDtypeStruct((r, c), F32), compiler_params=_params("parallel"))(*([a] * k))


def _adamw_math(w, g, m, v):
    mn = ADAM_B1 * m + (1.0 - ADAM_B1) * g
    vn = ADAM_B2 * v + (1.0 - ADAM_B2) * (g * g)
    m_hat = mn / (1.0 - ADAM_B1 ** ADAM_STEP)
    v_hat = vn / (1.0 - ADAM_B2 ** ADAM_STEP)
    return -ADAM_LR * (m_hat / (jnp.sqrt(v_hat) + ADAM_EPS) + ADAM_WD * w), mn, vn


def _reduce_adamw(landing, w, m, v, name, layer=None, into=None):
    k, r, c = landing.shape
    tr = _row_tile(r)
    n_into = 0 if into is None else 4

    def body(*refs):
        slabs, (w_ref, m_ref, v_ref) = refs[:k], refs[k:k + 3]
        g_ref, d_ref, mo_ref, vo_ref = refs[k + 3 + n_into:]
        g = slabs[0][...].astype(F32)
        for ref in slabs[1:]:
            g = g + ref[...].astype(F32)
        g_ref[...] = g
        d_ref[...], mo_ref[...], vo_ref[...] = _adamw_math(w_ref[...], g, m_ref[...], v_ref[...])

    if layer is None:
        spec = pl.BlockSpec((tr, c), lambda i: (i, 0))
    else:
        spec = pl.BlockSpec((None, tr, c), lambda i: (layer, i, 0))
    in_specs = [pl.BlockSpec((None, tr, c), functools.partial(lambda i, s: (s, i, 0), s=s)) for s in range(k)]
    in_specs += [spec] * 3 + [ANY] * n_into
    shape = jax.ShapeDtypeStruct(w.shape, F32)
    return pl.pallas_call(
        body, name=name, grid=(r // tr,), in_specs=in_specs, out_specs=[spec] * 4, out_shape=[shape] * 4,
        input_output_aliases={k + 3 + j: j for j in range(n_into)},
        compiler_params=_params("parallel"))(*([landing] * k), w, m, v, *(into or ()))


def _rows_needed(shape):
    return shape[0] * -(-shape[1] // D)


def _as_rows(a):
    r, c = a.shape
    n = -(-c // D)
    assert r == 1 or n == 1
    return jnp.pad(a, ((0, 0), (0, n * D - c))).reshape(r * n, D)


def _adamw_replicated(g_rows, w, m, v, name):
    n = len(REP_2D)

    def body(*refs):
        g_ref, w_refs, m_refs, v_refs, outs = refs[0], refs[1:1 + n], refs[1 + n:1 + 2 * n], refs[1 + 2 * n:1 + 3 * n], refs[1 + 3 * n:]
        r0 = 0
        for k, (_, (r, c)) in enumerate(REP_2D):
            pieces = [g_ref[r0 + j * r:r0 + j * r + r, 0:min(D, c - j * D)] for j in range(-(-c // D))]
            g = pieces[0] if len(pieces) == 1 else jnp.concatenate(pieces, axis=1)
            outs[4 * k][...] = g
            outs[4 * k + 1][...], outs[4 * k + 2][...], outs[4 * k + 3][...] = _adamw_math(
                w_refs[k][...], g, m_refs[k][...], v_refs[k][...])
            r0 += _rows_needed((r, c))

    shapes = [s for _, s in REP_2D]
    return pl.pallas_call(
        body, name=name, in_specs=[_full(g_rows.shape)] + [_full(s) for s in shapes] * 3,
        out_specs=[_full(s) for s in shapes for _ in range(4)],
        out_shape=[jax.ShapeDtypeStruct(s, F32) for s in shapes for _ in range(4)],
        grid=(1,), compiler_params=_params("arbitrary"))(g_rows, *w, *m, *v)


def _adamw(w, g, m, v, name):
    r, c = w.shape
    tr = _row_tile(r)

    def body(w_ref, g_ref, m_ref, v_ref, d_ref, mo_ref, vo_ref):
        d_ref[...], mo_ref[...], vo_ref[...] = _adamw_math(w_ref[...], g_ref[...], m_ref[...], v_ref[...])

    spec = pl.BlockSpec((tr, c), lambda i: (i, 0))
    shape = jax.ShapeDtypeStruct((r, c), F32)
    return pl.pallas_call(
        body, name=name, grid=(r // tr,), in_specs=[spec] * 4, out_specs=[spec] * 3, out_shape=[shape] * 3,
        compiler_params=_params("parallel"))(w, g, m, v)


ANY = pl.BlockSpec(memory_space=pl.ANY)


def _all_gather(block, name):
    r, c_dim = block.shape

    def body(x_ref, out_ref, token, send_sems, recv_sems, local_sem):
        token[...] = jnp.zeros_like(token)
        x, y, c = lax.axis_index("x"), lax.axis_index("y"), lax.axis_index("c")
        me, sibling = (x, y, c), (x, y, 1 - c)
        chips = [(1 - x, y), (x, 1 - y), (1 - x, 1 - y)]

        def rows(px, py, pc):
            return out_ref.at[4 * px + 2 * py + pc]

        def copy(k, blk, to, src=None):
            return pltpu.make_async_remote_copy(
                src_ref=rows(*blk) if src is None else src, dst_ref=rows(*blk), send_sem=send_sems.at[k],
                recv_sem=recv_sems.at[k], device_id=to, device_id_type=MESH)

        mine = pltpu.make_async_copy(x_ref, rows(*me), local_sem)
        mine.start()
        first = [copy(0, me, sibling, src=x_ref)]
        first += [copy(1 + j, me, (*chip, c), src=x_ref) for j, chip in enumerate(chips)]
        for cp in first:
            cp.start()
        passed = [copy(4 + j, (*chip, c), sibling) for j, chip in enumerate(chips)]
        for j, chip in enumerate(chips):
            copy(1 + j, (*chip, c), me).wait_recv()
            passed[j].start()
        copy(0, sibling, me).wait_recv()
        for j, chip in enumerate(chips):
            copy(4 + j, (*chip, 1 - c), me).wait_recv()
        for cp in first + passed:
            cp.wait_send()
        mine.wait()

    return pl.pallas_call(
        body, name=name, in_specs=[ANY], out_specs=[ANY, pl.BlockSpec(memory_space=pltpu.VMEM)],
        out_shape=[jax.ShapeDtypeStruct((N_DEV, r, c_dim), block.dtype), jax.ShapeDtypeStruct((8, 128), F32)],
        scratch_shapes=[pltpu.SemaphoreType.DMA((7,)), pltpu.SemaphoreType.DMA((7,)), pltpu.SemaphoreType.DMA],
    )(block)


HBM = pl.BlockSpec(memory_space=pltpu.HBM)
SEM = pl.BlockSpec(memory_space=pltpu.SEMAPHORE)
EFFECT = pltpu.SideEffectType.DATAFLOW_SIDE_EFFECTING


def _exchange_copies(scatter, src_refs, land_refs, send_sems, recv_sems, local_sems):
    x, y, c = lax.axis_index("x"), lax.axis_index("y"), lax.axis_index("c")
    me = 4 * x + 2 * y + c
    sends, arrivals, locals_ = [], [], []
    for a, (src, land) in enumerate(zip(src_refs, land_refs)):
        def pick(idx, src=src):
            return src.at[idx] if scatter else src

        locals_.append(pltpu.make_async_copy(pick(me), land.at[me], local_sems.at[a]))
        for r in range(1, N_DEV):
            px = 1 - x if r & 4 else x
            py = 1 - y if r & 2 else y
            pc = 1 - c if r & 1 else c
            peer, s = 4 * px + 2 * py + pc, 7 * a + r - 1
            sends.append(pltpu.make_async_remote_copy(
                src_ref=pick(peer), dst_ref=land.at[me], send_sem=send_sems.at[s], recv_sem=recv_sems.at[s],
                device_id=(px, py, pc), device_id_type=MESH))
            arrivals.append(pltpu.make_async_remote_copy(
                src_ref=pick(peer), dst_ref=land.at[peer], send_sem=send_sems.at[s], recv_sem=recv_sems.at[s],
                device_id=(px, py, pc), device_id_type=MESH))
    return sends, arrivals, locals_


def _exchange_start(srcs, scatter, name):
    n = len(srcs)
    lands = [lax.empty((N_DEV,) + s.shape[-2:], s.dtype) for s in srcs]

    def body(*refs):
        src_refs, land_refs = refs[:n], refs[n:2 * n]
        send_sems, recv_sems, local_sems = refs[2 * n:2 * n + 3]
        token = refs[-1]
        sends, _, locals_ = _exchange_copies(scatter, src_refs, land_refs, send_sems, recv_sems, local_sems)
        for cp in locals_ + sends:
            cp.start()
        token[...] = jnp.zeros_like(token)

    res = pl.pallas_call(
        body, name=name,
        out_shape=[pltpu.SemaphoreType.DMA((7 * n,)), pltpu.SemaphoreType.DMA((7 * n,)), pltpu.SemaphoreType.DMA((n,))]
        + [pltpu.HBM(a.shape, a.dtype) for a in list(srcs) + lands] + [jax.ShapeDtypeStruct((8, 128), F32)],
        in_specs=[HBM] * (2 * n), out_specs=[SEM] * 3 + [HBM] * (2 * n) + [pl.BlockSpec(memory_space=pltpu.VMEM)],
        input_output_aliases={i: 3 + i for i in range(2 * n)},
        compiler_params=pltpu.CompilerParams(has_side_effects=EFFECT),
    )(*[pltpu.with_memory_space_constraint(a, pltpu.HBM) for a in list(srcs) + lands])
    return (n, scatter, res[:3], res[3:3 + 2 * n]), res[-1]


def _exchange_wait(handle, after, name):
    n, scatter, sems, thru = handle

    def body(*refs):
        src_refs, land_refs = refs[:n], refs[n:2 * n]
        send_sems, recv_sems, local_sems = refs[2 * n:2 * n + 3]
        sends, arrivals, locals_ = _exchange_copies(scatter, src_refs, land_refs, send_sems, recv_sems, local_sems)
        for cp in arrivals:
            cp.wait_recv()
        for cp in sends:
            cp.wait_send()
        for cp in locals_:
            cp.wait()

    res = pl.pallas_call(
        body, name=name, out_shape=[pltpu.HBM(a.shape, a.dtype) for a in thru],
        in_specs=[HBM] * (2 * n) + [SEM] * 3 + [ANY], out_specs=[HBM] * (2 * n),
        input_output_aliases={i: i for i in range(2 * n)},
        compiler_params=pltpu.CompilerParams(has_side_effects=EFFECT),
    )(*thru, *sems, after)
    return res[n:]


def _behind(tokens, a):
    zero = sum(tok[0, 0] for tok in tokens)
    return jax.tree.map(lambda v: v + zero.astype(v.dtype), a)


def _perm_heads(a, perm, axis):
    idx = [slice(None)] * a.ndim
    parts = []
    for h in perm:
        idx[axis] = slice(64 * h, 64 * (h + 1))
        parts.append(a[tuple(idx)])
    idx[axis] = slice(512, None)
    if a.shape[axis] > 512:
        parts.append(a[tuple(idx)])
    return jnp.concatenate(parts, axis=axis)


Q_INV = tuple(int(i) for i in np.argsort(Q_PERM))


def _in0_to_kernel(a, axis):
    a = _perm_heads(a, Q_PERM, axis)
    idx = [slice(None)] * a.ndim

    def cut(lo, hi):
        idx[axis] = slice(lo, hi)
        return a[tuple(idx)]

    return jnp.concatenate([cut(0, 512), cut(768, 1792), cut(512, 768)], axis=axis)


def _in0_from_kernel(a, axis):
    idx = [slice(None)] * a.ndim

    def cut(lo, hi):
        idx[axis] = slice(lo, hi)
        return a[tuple(idx)]

    a = jnp.concatenate([cut(0, 512), cut(1536, 1792), cut(512, 1536)], axis=axis)
    return _perm_heads(a, Q_INV, axis)


def _f32_as_u16_rows(v, rows):
    bits = lax.bitcast_convert_type(v, jnp.uint16).reshape(-1)
    return jnp.pad(bits, (0, rows * D - bits.shape[0])).reshape(rows, D)


def _pad_rows(v, rows):
    v = v.reshape(-1)
    return jnp.pad(v, (0, rows * D - v.shape[0])).reshape(rows, D)


def kernel(x, mix_norm, a_w_in, a_b_in, a_sinks, a_conv_w, a_conv_b, a_cln_g, a_cln_b, a_w_out, c_w_in, c_w_pool, c_pool_scale, c_sln_g, c_sln_b, c_w_s, c_b_s, c_w_out, ffn_norm, ffn_w_gate, ffn_w_up, ffn_w_down, final_norm, loss_target, m_mix_norm, m_a_w_in, m_a_b_in, m_a_sinks, m_a_conv_w, m_a_conv_b, m_a_cln_g, m_a_cln_b, m_a_w_out, m_c_w_in, m_c_w_pool, m_c_pool_scale, m_c_sln_g, m_c_sln_b, m_c_w_s, m_c_b_s, m_c_w_out, m_ffn_norm, m_ffn_w_gate, m_ffn_w_up, m_ffn_w_down, m_final_norm, v_mix_norm, v_a_w_in, v_a_b_in, v_a_sinks, v_a_conv_w, v_a_conv_b, v_a_cln_g, v_a_cln_b, v_a_w_out, v_c_w_in, v_c_w_pool, v_c_pool_scale, v_c_sln_g, v_c_sln_b, v_c_w_s, v_c_b_s, v_c_w_out, v_ffn_norm, v_ffn_w_gate, v_ffn_w_up, v_ffn_w_down, v_final_norm):
    bsz, seq, _ = x.shape
    t = bsz * seq
    w_in = dict(mix_norm=mix_norm, a_w_in=a_w_in, a_b_in=a_b_in, a_sinks=a_sinks, a_conv_w=a_conv_w, a_conv_b=a_conv_b,
                a_cln_g=a_cln_g, a_cln_b=a_cln_b, a_w_out=a_w_out, c_w_in=c_w_in, c_w_pool=c_w_pool,
                c_pool_scale=c_pool_scale, c_sln_g=c_sln_g, c_sln_b=c_sln_b, c_w_s=c_w_s, c_b_s=c_b_s, c_w_out=c_w_out,
                ffn_norm=ffn_norm, ffn_w_gate=ffn_w_gate, ffn_w_up=ffn_w_up, ffn_w_down=ffn_w_down, final_norm=final_norm)
    m_in = dict(mix_norm=m_mix_norm, a_w_in=m_a_w_in, a_b_in=m_a_b_in, a_sinks=m_a_sinks, a_conv_w=m_a_conv_w,
                a_conv_b=m_a_conv_b, a_cln_g=m_a_cln_g, a_cln_b=m_a_cln_b, a_w_out=m_a_w_out, c_w_in=m_c_w_in,
                c_w_pool=m_c_w_pool, c_pool_scale=m_c_pool_scale, c_sln_g=m_c_sln_g, c_sln_b=m_c_sln_b, c_w_s=m_c_w_s,
                c_b_s=m_c_b_s, c_w_out=m_c_w_out, ffn_norm=m_ffn_norm, ffn_w_gate=m_ffn_w_gate, ffn_w_up=m_ffn_w_up,
                ffn_w_down=m_ffn_w_down, final_norm=m_final_norm)
    v_in = dict(mix_norm=v_mix_norm, a_w_in=v_a_w_in, a_b_in=v_a_b_in, a_sinks=v_a_sinks, a_conv_w=v_a_conv_w,
                a_conv_b=v_a_conv_b, a_cln_g=v_a_cln_g, a_cln_b=v_a_cln_b, a_w_out=v_a_w_out, c_w_in=v_c_w_in,
                c_w_pool=v_c_w_pool, c_pool_scale=v_c_pool_scale, c_sln_g=v_c_sln_g, c_sln_b=v_c_sln_b, c_w_s=v_c_w_s,
                c_b_s=v_c_b_s, c_w_out=v_c_w_out, ffn_norm=v_ffn_norm, ffn_w_gate=v_ffn_w_gate, ffn_w_up=v_ffn_w_up,
                ffn_w_down=v_ffn_w_down, final_norm=v_final_norm)

    small = jnp.concatenate([a_conv_w[0].reshape(-1), c_pool_scale[0], c_sln_g[0], c_sln_b[0]])
    first_bits = lax.bitcast_convert_type(a_w_in[0].T.astype(BF16), jnp.uint16)
    gathered, tok = _all_gather(jnp.concatenate([first_bits, _f32_as_u16_rows(small, W_MISC_ROWS)], axis=0), "gather_mixer0")

    def ffn_shards(l):
        return [ffn_w_gate[l].T.astype(BF16), ffn_w_up[l].T.astype(BF16), ffn_w_down[l].astype(BF16)]

    ffn0_h, tok = _exchange_start(_behind([tok], ffn_shards(0) + [a_w_out[0].astype(BF16)]), False, "gather_ffn0_start")
    mix1_h, tok = _exchange_start(_behind([tok], [c_w_in[0].T.astype(BF16), c_w_out[0].astype(BF16)]), False,
                                  "gather_mixer1_start")
    ffn1_h, tok = _exchange_start(_behind([tok], ffn_shards(1)), False, "gather_ffn1_start")

    a_in_full = lax.bitcast_convert_type(gathered[:, :224].reshape(IN0, D), BF16)
    small_all = lax.bitcast_convert_type(
        gathered[:, 224:].reshape(N_DEV, -1)[:, :2 * SMALL_SHARD].reshape(N_DEV, SMALL_SHARD, 2), F32)
    conv_w = small_all[:, :31 * 64].reshape(N_DEV, 31, 64).transpose(1, 0, 2).reshape(31, 512)
    conv_w = jnp.pad(conv_w, ((0, HALO - CONV_K), (0, 0)))
    pool_scale = small_all[:, 31 * 64:31 * 64 + 64].reshape(1, 512)
    sln_g = small_all[:, 31 * 64 + 64:31 * 64 + 128].reshape(1, 512)
    sln_b = small_all[:, 31 * 64 + 128:].reshape(1, 512)

    wt_in0 = _in0_to_kernel(a_in_full, 0)
    b_in0 = _in0_to_kernel(a_b_in, 1)
    b_rows = jnp.broadcast_to(c_b_s[0][:, :, None], (4, 128, 128))
    conv_b, cln_g, cln_b = a_conv_b, a_cln_g, a_cln_b

    h0 = x.reshape(t, D)
    target = loss_target.reshape(t, D)
    z0, hn0 = _norm_proj(h0, _behind([tok], mix_norm[0:1]), wt_in0, b_in0, "in_proj0")
    attn, tok = _attn_fwd(z0, a_sinks, bsz, "attn_fwd")
    conv, conv_y = _conv_fwd(z0, conv_w, conv_b, cln_g, _behind([tok], cln_b), bsz, "conv_fwd")
    wtg0, wtu0, wd0, a_out_full = (w.reshape(-1, D) for w in _exchange_wait(ffn0_h, conv, "gather_ffn0_wait"))
    w_out0 = _perm_heads(a_out_full, Q_PERM, 0)
    h1, hnf0, gate0a, up0a, part = _ffn_fwd_a(h0, attn, conv, w_out0, ffn_norm[0:1], wtg0, wtu0, wd0, "ffn_fwd0a")
    gate0b, up0b, h2 = _ffn_fwd_b(hnf0, part, wtg0, wtu0, wd0, "ffn_fwd0b")
    wt_in1, w_out1 = (w.reshape(-1, D) for w in _exchange_wait(mix1_h, h2, "gather_mixer1_wait"))
    z1, hn1 = _norm_proj(h2, mix_norm[1:2], wt_in1, None, "in_proj1")
    pool, tok = _pool_fwd(z1, c_w_pool[0], pool_scale, bsz, "pool_fwd")
    sgu = _sgu_fwd(z1, sln_g, _behind([tok], sln_b), c_w_s[0], b_rows, "sgu_fwd")
    wtg1, wtu1, wd1 = (w.reshape(D_FF, D) for w in _exchange_wait(ffn1_h, sgu, "gather_ffn1_wait"))
    h3, hnf1, gate1a, up1a, part = _ffn_fwd_a(h2, pool, sgu, w_out1, ffn_norm[1:2], wtg1, wtu1, wd1, "ffn_fwd1a")
    gate1b, up1b, dh4, d_final_norm, loss_part = _ffn_fwd_b(hnf1, part, wtg1, wtu1, wd1, "ffn_fwd1b",
                                                            head=(final_norm.reshape(1, D), target))

    def blocks(g):
        return g.reshape(N_DEV, g.shape[0] // N_DEV, D)

    dgate1a, dup1a, act1a, part = _ffn_bwd_a(dh4, gate1a, up1a, wtg1, wtu1, wd1, "ffn_bwd1a")
    dh3, dmix1, dgate1b, dup1b, act1b, d_fn1 = _ffn_bwd_b(dh4, h3, ffn_norm[1:2], gate1b, up1b, wtg1, wtu1, wd1, w_out1, part,
                                                          "ffn_bwd1b")
    gw_ffn1 = [_mm_tn_halves(dgate1a, dgate1b, hnf1, "dw_gate1"), _mm_tn_halves(dup1a, dup1b, hnf1, "dw_up1"),
               _mm_tn_halves(act1a, act1b, dh4, "dw_down1")]
    ffn1_g, tok = _exchange_start([blocks(g) for g in gw_ffn1], True, "scatter_ffn1_start")
    gw_c_out = _mm_tn_pieces([pool, sgu], dh3, "dw_out1")
    dzp, d_w_pool, d_pool_scale = _pool_bwd(z1, dmix1, c_w_pool[0], _behind([tok], pool_scale), bsz, "pool_bwd")
    dzu, dzv, d_w_s, d_b_s, d_sln_g, d_sln_b = _sgu_bwd(z1, dmix1, sln_g, sln_b, c_w_s[0], b_rows, "sgu_bwd")
    dh2, d_mn1 = _proj_bwd_norm([(dzp, 0), (dzu, 512), (dzv, 1024)], wt_in1, h2, dh3, mix_norm[1:2], BF16, "in_proj1_bwd")
    gw_c_in = _mm_tn_pieces([dzp, dzu, dzv], hn1, "dw_in1")
    mix1_g, tok = _exchange_start([blocks(gw_c_in), blocks(gw_c_out)], True, "scatter_mixer1_start")
    dgate0a, dup0a, act0a, part = _ffn_bwd_a(dh2, gate0a, up0a, wtg0, wtu0, wd0, "ffn_bwd0a")
    dh1, dmix0, dgate0b, dup0b, act0b, d_fn0 = _ffn_bwd_b(dh2, h1, _behind([tok], ffn_norm[0:1]), gate0b, up0b, wtg0, wtu0, wd0,
                                                          w_out0, part, "ffn_bwd0b")
    gw_ffn0 = [_mm_tn_halves(dgate0a, dgate0b, hnf0, "dw_gate0"), _mm_tn_halves(dup0a, dup0b, hnf0, "dw_up0"),
               _mm_tn_halves(act0a, act0b, dh2, "dw_down0")]
    ffn0_g, tok = _exchange_start([blocks(g) for g in gw_ffn0], True, "scatter_ffn0_start")
    gw_a_out = _perm_heads(_mm_tn_pieces([attn, conv], dh1, "dw_out0"), Q_INV, 0)
    dq, dkv, d_sink_row, d_bq, d_bkv = _attn_bwd(z0, dmix0, _behind([tok], a_sinks), bsz, "attn_bwd")
    dca, dcg, d_conv_w, d_conv_b, d_cln_g, d_cln_b, d_ba, d_bg = _conv_bwd(z0, conv_y, dmix0, conv_w, cln_g, cln_b, bsz, "conv_bwd")
    gw_a_in = _in0_from_kernel(_mm_tn_pieces([dq, dca, dcg, dkv], hn0, "dw_in0"), 0)
    mix0_g, tok = _exchange_start([blocks(gw_a_in), blocks(gw_a_out)], True, "scatter_mixer0_start")
    dx, d_mn0 = _proj_bwd_norm([(dq, 0), (dca, 512), (dcg, 1024), (dkv, 1536)], wt_in0, h0, dh1,
                               _behind([tok], mix_norm[0:1]), F32, "in_proj0_bwd")
    d_b_in = _in0_from_kernel(jnp.concatenate([d_bq, d_ba, d_bg, d_bkv], axis=1), 1)

    rep = dict(mix_norm=jnp.concatenate([d_mn0, d_mn1], axis=0), a_b_in=d_b_in, a_sinks=d_sink_row[:, :8],
               a_conv_b=d_conv_b, a_cln_g=d_cln_g, a_cln_b=d_cln_b, c_w_pool=d_w_pool.reshape(64, D),
               c_w_s=d_w_s.reshape(64, D), c_b_s=d_b_s, ffn_norm=jnp.concatenate([d_fn0, d_fn1], axis=0),
               final_norm=d_final_norm)
    rep_rows = jnp.concatenate([_as_rows(rep[nm]) for nm, _ in REP_2D] + [_as_rows(loss_part)], axis=0)
    rep_flat = jnp.pad(rep_rows, ((0, N_DEV * REP_ROWS - rep_rows.shape[0]), (0, 0))).reshape(N_DEV, REP_ROWS, D)
    small_g = jnp.concatenate([
        d_conv_w[:CONV_K].reshape(31, N_DEV, 64).transpose(1, 0, 2).reshape(N_DEV, 31 * 64),
        d_pool_scale.reshape(N_DEV, 64), d_sln_g.reshape(N_DEV, 64), d_sln_b.reshape(N_DEV, 64)], axis=1)
    small_g = jnp.pad(small_g, ((0, 0), (0, G_SMALL_ROWS * D - SMALL_SHARD))).reshape(N_DEV, G_SMALL_ROWS, D)
    tail_g, tok = _exchange_start([jnp.concatenate([small_g, rep_flat], axis=1)], True, "scatter_tail_start")

    names = list(w_in)
    g_out, delta, new_m, new_v = {}, {}, {}, {}
    column_sharded = ("a_w_in", "c_w_in", "ffn_w_gate", "ffn_w_up")

    def rows_of(a, nm):
        return jnp.swapaxes(a, 1, 2) if nm in column_sharded else a

    def reduce_adamw(nm, landing, layer, into=None):
        args = [rows_of(d[nm], nm) for d in (w_in, m_in, v_in)]
        if args[0].shape[0] == 1:
            args, layer = [a[0] for a in args], None
        return _reduce_adamw(landing, *args, "adamw_%s_%s" % (nm, layer), layer=layer, into=into)

    def keep(nm, res):
        res = [r if r.ndim == 3 else r[None] for r in res]
        g_out[nm], delta[nm], new_m[nm], new_v[nm] = (rows_of(r, nm) for r in res)

    ffn_names = ("ffn_w_gate", "ffn_w_up", "ffn_w_down")
    landed = _exchange_wait(ffn1_g, tok, "scatter_ffn1_wait")
    ffn_res = [reduce_adamw(nm, a, 1) for nm, a in zip(ffn_names, landed)]
    landed = _exchange_wait(mix1_g, ffn_res[-1][0], "scatter_mixer1_wait")
    for nm, a in zip(("c_w_in", "c_w_out"), landed):
        keep(nm, reduce_adamw(nm, a, 0))
    landed = _exchange_wait(ffn0_g, g_out["c_w_out"], "scatter_ffn0_wait")
    for nm, a, res in zip(ffn_names, landed, ffn_res):
        keep(nm, reduce_adamw(nm, a, 0, into=res))
    landed = _exchange_wait(mix0_g, g_out["ffn_w_down"], "scatter_mixer0_wait")
    for nm, a in zip(("a_w_in", "a_w_out"), landed):
        keep(nm, reduce_adamw(nm, a, 0))
    g_tail = _sum_slabs(_exchange_wait(tail_g, g_out["a_w_out"], "scatter_tail_wait")[0], "sum_tail")
    rep_all = _all_gather(g_tail[G_SMALL_ROWS:], "gather_replicated_grads")[0].reshape(N_DEV * REP_ROWS, D)
    loss = rep_all[sum(_rows_needed(s) for _, s in REP_2D), 0]
    res = _adamw_replicated(rep_all, *[[d[nm].reshape(s) for nm, s in REP_2D] for d in (w_in, m_in, v_in)], "adamw_replicated")
    for k, (nm, _) in enumerate(REP_2D):
        g_out[nm], delta[nm], new_m[nm], new_v[nm] = (r.reshape(w_in[nm].shape) for r in res[4 * k:4 * k + 4])
    small_r = g_tail[:G_SMALL_ROWS].reshape(-1)[:SMALL_SHARD]
    g_out.update(
        a_conv_w=small_r[:31 * 64].reshape(1, 31, 64), c_pool_scale=small_r[31 * 64:31 * 64 + 64].reshape(1, 64),
        c_sln_g=small_r[31 * 64 + 64:31 * 64 + 128].reshape(1, 64), c_sln_b=small_r[31 * 64 + 128:].reshape(1, 64))
    group = ("a_conv_w", "c_pool_scale", "c_sln_g", "c_sln_b")
    flat = [_pad_rows(jnp.concatenate([d[nm].reshape(-1) for nm in group]), G_SMALL_ROWS) for d in (w_in, g_out, m_in, v_in)]
    res = [r.reshape(-1) for r in _adamw(*flat, "adamw_small_sharded")]
    off = 0
    for nm in group:
        n = int(np.prod(w_in[nm].shape))
        delta[nm], new_m[nm], new_v[nm] = (r[off:off + n].reshape(w_in[nm].shape) for r in res)
        off += n

    grad_x = dx.reshape(bsz, seq, D)
    return (loss, grad_x, *[g_out[nm] for nm in names], *[delta[nm] for nm in names],
            *[new_m[nm] for nm in names], *[new_v[nm] for nm in names])
```

```python
import functools

import jax
import jax.numpy as jnp
import numpy as np
from jax import lax
from jax.experimental import pallas as pl
from jax.experimental.pallas import tpu as pltpu

F32 = jnp.float32
BF16 = jnp.bfloat16
MESH = pl.DeviceIdType.MESH

D = 1024
N_DEV = 8
EPS = 1e-5
HEAD_PAIRS = 4
ATT_BLK = 128
CONV_K = 31
HALO = 32
D_FF = 2816
IN0 = 1792
IN1 = 1536
POOL_WINDOWS = (2, 4, 8, 16)
SGU_CHUNK = 128
GELU_C = 0.7978845608028654
GELU_A = 0.044715
ADAM_LR, ADAM_B1, ADAM_B2, ADAM_EPS, ADAM_WD, ADAM_STEP = 0.001, 0.9, 0.999, 1e-08, 0.01, 10
VMEM_LIMIT = 56 << 20

SMALL_SHARD = 31 * 64 + 3 * 64
W_MISC_ROWS = 16
G_MISC_ROWS = 32
G_SMALL_ROWS = 8
REP_ROWS = G_MISC_ROWS - G_SMALL_ROWS
REP_2D = (("c_w_pool", (64, 1024)), ("c_w_s", (64, 1024)), ("mix_norm", (2, 1024)), ("a_b_in", (1, 1792)), ("a_sinks", (1, 8)),
          ("a_conv_b", (1, 512)), ("a_cln_g", (1, 512)), ("a_cln_b", (1, 512)), ("c_b_s", (4, 128)), ("ffn_norm", (2, 1024)),
          ("final_norm", (1, 1024)))
Q_PERM = (0, 4, 1, 5, 2, 6, 3, 7)


def _params(*sem):
    return pltpu.CompilerParams(dimension_semantics=sem, vmem_limit_bytes=VMEM_LIMIT)


def _nn(a, b):
    return jnp.dot(a, b, preferred_element_type=F32)


def _nt(a, b):
    return lax.dot_general(a, b, (((1,), (1,)), ((), ())), preferred_element_type=F32)


def _tn(a, b):
    return lax.dot_general(a, b, (((0,), (0,)), ((), ())), preferred_element_type=F32)


def _tile(n, want=512):
    t = min(want, n)
    assert n % t == 0, (n, t)
    return t


def _seq_tile(s):
    return 512 if s >= 1024 else s // 2


def _rms(x, g):
    r = lax.rsqrt(jnp.mean(x * x, axis=-1, keepdims=True) + EPS)
    return x * r * g, r


def _rms_bwd(x, g, d_y):
    r = lax.rsqrt(jnp.mean(x * x, axis=-1, keepdims=True) + EPS)
    xr = x * r
    u = d_y * g
    d_x = r * (u - xr * jnp.mean(u * xr, axis=-1, keepdims=True))
    return d_x, jnp.sum(d_y * xr, axis=0, keepdims=True)


def _ln(y, g, b):
    mu = jnp.mean(y, axis=-1, keepdims=True)
    yc = y - mu
    rstd = lax.rsqrt(jnp.mean(yc * yc, axis=-1, keepdims=True) + EPS)
    xhat = yc * rstd
    return xhat * g + b, xhat, rstd


def _ln_bwd(d_o, xhat, rstd, g):
    dxh = d_o * g
    return rstd * (dxh - jnp.mean(dxh, axis=-1, keepdims=True) - xhat * jnp.mean(dxh * xhat, axis=-1, keepdims=True))


def _gelu(x):
    th = jnp.tanh(GELU_C * (x + GELU_A * x * x * x))
    return 0.5 * x * (1.0 + th), th


def _gelu_grad(x, th):
    return 0.5 * (1.0 + th) + 0.5 * x * (1.0 - th * th) * GELU_C * (1.0 + 3.0 * GELU_A * x * x)


def _row(c):
    return pl.BlockSpec((1, c), lambda *_: (0, 0))


def _full(shape):
    return pl.BlockSpec(shape, lambda *_: (0,) * len(shape))


def _norm_proj(h, g, wt, bias, name):
    t, n = h.shape[0], wt.shape[0]
    tm = _tile(t)
    has_bias = bias is not None

    def body(*refs):
        h_ref, g_ref, wt_ref = refs[:3]
        z_ref, hn_ref = refs[-2:]
        hn = _rms(h_ref[...].astype(F32), g_ref[...])[0].astype(BF16)
        hn_ref[...] = hn
        z = _nt(hn, wt_ref[...])
        if has_bias:
            z = z + refs[3][...]
        z_ref[...] = z.astype(BF16)

    in_specs = [pl.BlockSpec((tm, D), lambda i: (i, 0)), _row(D), _full((n, D))]
    args = [h, g, wt]
    if has_bias:
        in_specs.append(_row(n))
        args.append(bias)
    return pl.pallas_call(
        body, name=name, grid=(t // tm,), in_specs=in_specs,
        out_specs=[pl.BlockSpec((tm, n), lambda i: (i, 0)), pl.BlockSpec((tm, D), lambda i: (i, 0))],
        out_shape=[jax.ShapeDtypeStruct((t, n), BF16), jax.ShapeDtypeStruct((t, D), BF16)],
        compiler_params=_params("parallel"))(*args)


def _ff_pieces(tf, width=256):
    return [(c0, min(width, tf - c0)) for c0 in range(0, tf, width)]


FF_HALF = D_FF // 2


def _half_spec(part):
    return pl.BlockSpec((FF_HALF, D), lambda i: (part, 0))


def _ffn_half_fwd(hn, wtg_ref, wtu_ref, gate_ref, up_ref, act):
    for c0, cw in _ff_pieces(FF_HALF):
        cols = slice(c0, c0 + cw)
        gate = _nt(hn, wtg_ref[cols, :])
        up = _nt(hn, wtu_ref[cols, :])
        gate_ref[:, cols] = gate.astype(BF16)
        up_ref[:, cols] = up.astype(BF16)
        act[:, cols] = (gate * jax.nn.sigmoid(gate) * up).astype(BF16)


def _ffn_fwd_a(h_prev, a, b, w_out, g, wtg, wtu, wd, name):
    t = h_prev.shape[0]
    tm = _tile(t)

    def body(hp_ref, a_ref, b_ref, wa_ref, wb_ref, g_ref, wtg_ref, wtu_ref, wd_ref,
             h_ref, hn_ref, gate_ref, up_ref, part_ref, act):
        x = hp_ref[...].astype(F32) + _nn(a_ref[...], wa_ref[...]) + _nn(b_ref[...], wb_ref[...])
        h_ref[...] = x.astype(BF16)
        hn = _rms(x, g_ref[...])[0].astype(BF16)
        hn_ref[...] = hn
        _ffn_half_fwd(hn, wtg_ref, wtu_ref, gate_ref, up_ref, act)
        part_ref[...] = x + _nn(act[...], wd_ref[...])

    tok = pl.BlockSpec((tm, D), lambda i: (i, 0))
    half = pl.BlockSpec((tm, 512), lambda i: (i, 0))
    mid = pl.BlockSpec((tm, FF_HALF), lambda i: (i, 0))
    res = jax.ShapeDtypeStruct((t, D), BF16)
    mid_shape = jax.ShapeDtypeStruct((t, FF_HALF), BF16)
    return pl.pallas_call(
        body, name=name, grid=(t // tm,),
        in_specs=[tok, half, half, pl.BlockSpec((512, D), lambda i: (0, 0)), pl.BlockSpec((512, D), lambda i: (1, 0)),
                  _row(D), _half_spec(0), _half_spec(0), _half_spec(0)],
        out_specs=[tok, tok, mid, mid, tok], out_shape=[res, res, mid_shape, mid_shape, jax.ShapeDtypeStruct((t, D), F32)],
        scratch_shapes=[pltpu.VMEM((tm, FF_HALF), BF16)],
        compiler_params=_params("parallel"))(h_prev, a, b, w_out, w_out, g, wtg, wtu, wd)


def _ffn_fwd_b(hn, part, wtg, wtu, wd, name, head=None):
    t = hn.shape[0]
    tm = _tile(t)
    n_head = 0 if head is None else 2

    def body(*refs):
        hn_ref, part_ref, wtg_ref, wtu_ref, wd_ref = refs[:5]
        gate_ref, up_ref, o_ref = refs[5 + n_head:8 + n_head]
        act = refs[-1]
        _ffn_half_fwd(hn_ref[...], wtg_ref, wtu_ref, gate_ref, up_ref, act)
        x = part_ref[...] + _nn(act[...], wd_ref[...])
        if head is None:
            o_ref[...] = x.astype(BF16)
        else:
            fg_ref, t_ref = refs[5:7]
            dfg_ref, loss_ref = refs[10:12]

            @pl.when(pl.program_id(0) == 0)
            def _():
                dfg_ref[...] = jnp.zeros_like(dfg_ref)
                loss_ref[...] = jnp.zeros_like(loss_ref)

            gv = fg_ref[...]
            err = _rms(x, gv)[0] - t_ref[...]
            loss_ref[...] += 0.5 * jnp.sum(jnp.mean(err * err, axis=-1, keepdims=True), axis=0, keepdims=True)
            d_x, d_g = _rms_bwd(x, gv, err * (1.0 / D))
            o_ref[...] = d_x.astype(BF16)
            dfg_ref[...] += d_g

    tok = pl.BlockSpec((tm, D), lambda i: (i, 0))
    mid = pl.BlockSpec((tm, FF_HALF), lambda i: (i, 0))
    mid_shape = jax.ShapeDtypeStruct((t, FF_HALF), BF16)
    in_specs = [tok, tok, _half_spec(1), _half_spec(1), _half_spec(1)]
    out_specs = [mid, mid, tok]
    out_shape = [mid_shape, mid_shape, jax.ShapeDtypeStruct((t, D), BF16)]
    if head is not None:
        in_specs += [_row(D), tok]
        out_specs += [_row(D), _row(1)]
        out_shape += [jax.ShapeDtypeStruct((1, D), F32), jax.ShapeDtypeStruct((1, 1), F32)]
    return pl.pallas_call(
        body, name=name, grid=(t // tm,), in_specs=in_specs, out_specs=out_specs, out_shape=out_shape,
        scratch_shapes=[pltpu.VMEM((tm, FF_HALF), BF16)],
        compiler_params=_params("parallel" if head is None else "arbitrary"))(hn, part, wtg, wtu, wd, *(head or ()))


def _ffn_half_bwd(dh, gate_ref, up_ref, wd_ref, dgate_ref, dup_ref, act_ref):
    for c0, cw in _ff_pieces(FF_HALF):
        cols = slice(c0, c0 + cw)
        da = _nt(dh, wd_ref[cols, :])
        gt = gate_ref[:, cols].astype(F32)
        u = up_ref[:, cols].astype(F32)
        sg = jax.nn.sigmoid(gt)
        sil = gt * sg
        act_ref[:, cols] = (sil * u).astype(BF16)
        dup_ref[:, cols] = (da * sil).astype(BF16)
        dgate_ref[:, cols] = (da * u * sg * (1.0 + gt * (1.0 - sg))).astype(BF16)


def _ffn_bwd_a(dh, gate, up, wtg, wtu, wd, name):
    t = dh.shape[0]
    tm = _tile(t)

    def body(dh_ref, gate_ref, up_ref, wtg_ref, wtu_ref, wd_ref, dgate_ref, dup_ref, act_ref, part_ref):
        _ffn_half_bwd(dh_ref[...], gate_ref, up_ref, wd_ref, dgate_ref, dup_ref, act_ref)
        part_ref[...] = _nn(dgate_ref[...], wtg_ref[...]) + _nn(dup_ref[...], wtu_ref[...])

    tok = pl.BlockSpec((tm, D), lambda i: (i, 0))
    mid = pl.BlockSpec((tm, FF_HALF), lambda i: (i, 0))
    wide = jax.ShapeDtypeStruct((t, D_FF), BF16)
    return pl.pallas_call(
        body, name=name, grid=(t // tm,), in_specs=[tok, mid, mid, _half_spec(0), _half_spec(0), _half_spec(0)],
        out_specs=[mid, mid, mid, tok], out_shape=[wide, wide, wide, jax.ShapeDtypeStruct((t, D), F32)],
        compiler_params=_params("parallel"))(dh, gate, up, wtg, wtu, wd)


def _ffn_bwd_b(dh, h, g, gate, up, wtg, wtu, wd, w_out, part, wide, name):
    t = dh.shape[0]
    tm = _tile(t)

    def body(dh_ref, h_ref, g_ref, gate_ref, up_ref, wtg_ref, wtu_ref, wd_ref, wout_ref, part_ref, _a, _b, _c,
             dhin_ref, dmix_ref, dgate_ref, dup_ref, act_ref, dg_ref):
        @pl.when(pl.program_id(0) == 0)
        def _():
            dg_ref[...] = jnp.zeros_like(dg_ref)

        _ffn_half_bwd(dh_ref[...], gate_ref, up_ref, wd_ref, dgate_ref, dup_ref, act_ref)
        d_hn = part_ref[...] + _nn(dgate_ref[...], wtg_ref[...]) + _nn(dup_ref[...], wtu_ref[...])
        d_x, d_g = _rms_bwd(h_ref[...].astype(F32), g_ref[...], d_hn)
        dhin = (dh_ref[...].astype(F32) + d_x).astype(BF16)
        dhin_ref[...] = dhin
        dmix_ref[...] = _nt(dhin, wout_ref[...]).astype(BF16)
        dg_ref[...] += d_g

    tok = pl.BlockSpec((tm, D), lambda i: (i, 0))
    mid = pl.BlockSpec((tm, FF_HALF), lambda i: (i, 0))
    second = pl.BlockSpec((tm, FF_HALF), lambda i: (i, 1))
    wide_shape = jax.ShapeDtypeStruct((t, D_FF), BF16)
    res = jax.ShapeDtypeStruct((t, D), BF16)
    return pl.pallas_call(
        body, name=name, grid=(t // tm,),
        in_specs=[tok, tok, _row(D), mid, mid, _half_spec(1), _half_spec(1), _half_spec(1), _full((D, D)), tok, ANY, ANY, ANY],
        out_specs=[tok, tok, second, second, second, _row(D)],
        out_shape=[res, res, wide_shape, wide_shape, wide_shape, jax.ShapeDtypeStruct((1, D), F32)],
        input_output_aliases={10: 2, 11: 3, 12: 4},
        compiler_params=_params("arbitrary"))(dh, h, g, gate, up, wtg, wtu, wd, w_out, part, *wide)


def _proj_bwd_norm(pieces, wt, h, dh, g, dtype, name):
    t = h.shape[0]
    tm = _tile(t)
    n_p = len(pieces)

    def body(*refs):
        p_refs, w_refs = refs[:n_p], refs[n_p:2 * n_p]
        h_ref, dh_ref, g_ref, o_ref, dg_ref = refs[2 * n_p:]

        @pl.when(pl.program_id(0) == 0)
        def _():
            dg_ref[...] = jnp.zeros_like(dg_ref)

        d_hn = _nn(p_refs[0][...], w_refs[0][...])
        for p_ref, w_ref in zip(p_refs[1:], w_refs[1:]):
            d_hn = d_hn + _nn(p_ref[...], w_ref[...])
        d_x, d_g = _rms_bwd(h_ref[...].astype(F32), g_ref[...], d_hn)
        o_ref[...] = (dh_ref[...].astype(F32) + d_x).astype(dtype)
        dg_ref[...] += d_g

    tok = pl.BlockSpec((tm, D), lambda i: (i, 0))
    in_specs = [pl.BlockSpec((tm, a.shape[1]), lambda i: (i, 0)) for a, _ in pieces]
    for a, off in pieces:
        w = a.shape[1]
        assert off % w == 0
        in_specs.append(pl.BlockSpec((w, D), functools.partial(lambda i, blk: (blk, 0), blk=off // w)))
    in_specs += [tok, tok, _row(D)]
    return pl.pallas_call(
        body, name=name, grid=(t // tm,), in_specs=in_specs, out_specs=[tok, _row(D)],
        out_shape=[jax.ShapeDtypeStruct((t, D), dtype), jax.ShapeDtypeStruct((1, D), F32)],
        compiler_params=_params("arbitrary"))(*[a for a, _ in pieces], *([wt] * n_p), h, dh, g)


def _mm_tn(a, b, name):
    t, n = a.shape
    k = b.shape[1]
    tn = n if n <= 1024 else n // 2
    tt = _tile(t, 1024)
    nt = t // tt

    def body(a_ref, b_ref, o_ref, acc):
        s = pl.program_id(1)

        @pl.when(s == 0)
        def _():
            acc[...] = jnp.zeros_like(acc)

        acc[...] += _tn(a_ref[...], b_ref[...].astype(BF16))

        @pl.when(s == nt - 1)
        def _():
            o_ref[...] = acc[...].astype(BF16)

    return pl.pallas_call(
        body, name=name, grid=(n // tn, nt),
        in_specs=[pl.BlockSpec((tt, tn), lambda j, s: (s, j)), pl.BlockSpec((tt, k), lambda j, s: (s, 0))],
        out_specs=pl.BlockSpec((tn, k), lambda j, s: (j, 0)), out_shape=jax.ShapeDtypeStruct((n, k), BF16),
        scratch_shapes=[pltpu.VMEM((tn, k), F32)],
        compiler_params=_params("parallel", "arbitrary"))(a, b)


def _mm_tn_pieces(pieces, b, name):
    t, k = b.shape
    widths = [p.shape[1] for p in pieces]
    n, n_p = sum(widths), len(pieces)
    tt = _tile(t, 1024)
    nt = t // tt

    def body(*refs):
        b_ref, o_ref, acc = refs[n_p:]
        s = pl.program_id(0)

        @pl.when(s == 0)
        def _():
            acc[...] = jnp.zeros_like(acc)

        bb = b_ref[...].astype(BF16)
        off = 0
        for p_ref, w in zip(refs[:n_p], widths):
            acc[off:off + w, :] += _tn(p_ref[...], bb)
            off += w

        @pl.when(s == nt - 1)
        def _():
            o_ref[...] = acc[...].astype(BF16)

    return pl.pallas_call(
        body, name=name, grid=(nt,),
        in_specs=[pl.BlockSpec((tt, w), lambda s: (s, 0)) for w in widths] + [pl.BlockSpec((tt, k), lambda s: (s, 0))],
        out_specs=_full((n, k)), out_shape=jax.ShapeDtypeStruct((n, k), BF16),
        scratch_shapes=[pltpu.VMEM((n, k), F32)], compiler_params=_params("arbitrary"))(*pieces, b)


STACK = HEAD_PAIRS * ATT_BLK


def _attn_valid(first, rows):
    qi = lax.broadcasted_iota(jnp.int32, (rows, 2 * ATT_BLK), 0) % ATT_BLK
    r = lax.broadcasted_iota(jnp.int32, (rows, 2 * ATT_BLK), 1)
    dist = qi + ATT_BLK - r
    return (dist >= 0) & (dist < ATT_BLK) & ((r >= ATT_BLK) | jnp.logical_not(first))


def _stacked(ref, kh, scale):
    lo = lax.broadcasted_iota(jnp.int32, (ATT_BLK, 128), 1) < 64
    keep = lo if kh == 0 else ~lo
    parts = [jnp.where(keep, ref[:, g * 128:(g + 1) * 128] * scale, 0.0).astype(BF16) for g in range(HEAD_PAIRS)]
    return jnp.concatenate(parts, axis=0)


def _unstacked(a0, a1, g):
    lo = lax.broadcasted_iota(jnp.int32, (ATT_BLK, 128), 1) < 64
    rows = slice(g * ATT_BLK, (g + 1) * ATT_BLK)
    return jnp.where(lo, a0[rows], a1[rows])


def _sink_rows(s_ref, kh):
    return jnp.concatenate([jnp.full((ATT_BLK, 128), s_ref[0, kh * 4 + g], F32) for g in range(HEAD_PAIRS)], axis=0)


def _row_sums(a, split):
    hi = a.astype(BF16)
    ones = jnp.ones((2 * ATT_BLK, 128), BF16)
    if not split:
        return _nn(hi, ones)
    lo = (a - hi.astype(F32)).astype(BF16)
    return _nn(hi, ones) + _nn(lo, ones)


def _both(a):
    return jnp.concatenate([a, a], axis=1)


def _attn_probs(qs, kpair, sink, valid):
    s = jnp.where(valid, _nt(qs, kpair), -1e30)
    m = jnp.maximum(jnp.broadcast_to(jnp.max(s, axis=-1, keepdims=True), (s.shape[0], 128)), sink)
    p = jnp.exp(s - _both(m))
    es = jnp.exp(sink - m)
    inv = 1.0 / (_row_sums(p, split=True) + es)
    return p * _both(inv), es * inv


def _attn_probs_head(qm, kpair, sink, valid):
    s = jnp.where(valid, _nt(qm, kpair), -1e30)
    m = jnp.maximum(jnp.max(s, axis=-1, keepdims=True), sink)
    p = jnp.exp(s - m)
    return p * (1.0 / (jnp.sum(p, axis=-1, keepdims=True) + jnp.exp(sink - m)))


def _attn_specs(bsz, order):
    q = pl.BlockSpec((bsz, ATT_BLK, 512), lambda j: (0, order(j), 0))
    kvc = pl.BlockSpec((bsz, ATT_BLK, 256), lambda j: (0, order(j), 6))
    kvp = pl.BlockSpec((bsz, ATT_BLK, 256), lambda j: (0, jnp.maximum(order(j) - 1, 0), 6))
    return q, kvc, kvp


def _window_kv(kvc_ref, kvp_ref):
    kvc, kvp = kvc_ref[...], kvp_ref[...]
    kpair = jnp.concatenate([kvp[:, :128], kvc[:, :128]], axis=0)
    vpair = jnp.concatenate([kvp[:, 128:], kvc[:, 128:]], axis=0)
    return kpair, vpair


def _attn_fwd(z0, sinks, bsz, name):
    t = z0.shape[0]
    seq = t // bsz
    nb = seq // ATT_BLK

    def body(s_ref, q_ref, kvc_ref, kvp_ref, o_ref, token):
        token[...] = jnp.zeros_like(token)
        valid = _attn_valid(pl.program_id(0) == 0, ATT_BLK)
        lo = lax.broadcasted_iota(jnp.int32, (ATT_BLK, 128), 1) < 64
        for b in range(bsz):
            kpair, vpair = _window_kv(kvc_ref.at[b], kvp_ref.at[b])
            for g in range(HEAD_PAIRS):
                qs = q_ref[b, :, g * 128:(g + 1) * 128] * 0.125
                outs = []
                for kh in range(2):
                    qm = jnp.where(lo if kh == 0 else ~lo, qs, 0.0).astype(BF16)
                    p = _attn_probs_head(qm, kpair, s_ref[0, kh * 4 + g], valid)
                    outs.append(_nn(p.astype(BF16), vpair))
                o_ref[b, :, g * 128:(g + 1) * 128] = jnp.where(lo, outs[0], outs[1]).astype(BF16)

    q, kvc, kvp = _attn_specs(bsz, lambda j: j)
    z3 = z0.reshape(bsz, seq, z0.shape[1])
    out, token = pl.pallas_call(
        body, name=name, grid=(nb,),
        in_specs=[pl.BlockSpec(memory_space=pltpu.SMEM), q, kvc, kvp],
        out_specs=[pl.BlockSpec((bsz, ATT_BLK, 512), lambda j: (0, j, 0)), _full((8, 128))],
        out_shape=[jax.ShapeDtypeStruct((bsz, seq, 512), BF16), jax.ShapeDtypeStruct((8, 128), F32)],
        compiler_params=_params("arbitrary"))(sinks, z3, z3, z3)
    return out.reshape(t, 512), token


def _attn_bwd(z0, dmix, sinks, bsz, name):
    t = z0.shape[0]
    seq = t // bsz
    nb = seq // ATT_BLK

    def body(s_ref, q_ref, kvc_ref, kvp_ref, do_ref, dq_ref, dkv_ref, dsink_ref, dbq_ref, dbkv_ref, carry):
        j = pl.program_id(0)

        @pl.when(j == 0)
        def _():
            carry[...] = jnp.zeros_like(carry)
            dsink_ref[...] = jnp.zeros_like(dsink_ref)
            dbq_ref[...] = jnp.zeros_like(dbq_ref)
            dbkv_ref[...] = jnp.zeros_like(dbkv_ref)

        valid = _attn_valid(j == nb - 1, STACK)
        lane = lax.broadcasted_iota(jnp.int32, (1, 128), 1)
        dsink = jnp.zeros((1, 128), F32)
        dbq = [jnp.zeros((1, 128), F32)] * HEAD_PAIRS
        dbkv = jnp.zeros((1, 256), F32)
        for b in range(bsz):
            kpair, vpair = _window_kv(kvc_ref.at[b], kvp_ref.at[b])
            dk = jnp.zeros((2 * ATT_BLK, 128), F32)
            dv = jnp.zeros((2 * ATT_BLK, 128), F32)
            dqs = []
            for kh in range(2):
                qs = _stacked(q_ref.at[b], kh, 0.125)
                dos = _stacked(do_ref.at[b], kh, 1.0)
                p, ps = _attn_probs(qs, kpair, _sink_rows(s_ref, kh), valid)
                dp = _nt(dos, vpair)
                delta = _row_sums(p * dp, split=False)
                ds = (p * (dp - _both(delta))).astype(BF16)
                dqs.append(_nn(ds, kpair))
                dk = dk + _tn(ds, qs)
                dv = dv + _tn(p.astype(BF16), dos)
                psd = ps * delta
                for g in range(HEAD_PAIRS):
                    part = jnp.sum(psd[g * ATT_BLK:(g + 1) * ATT_BLK], axis=0, keepdims=True)
                    dsink = dsink - jnp.where(lane == kh * 4 + g, part, 0.0)
            for g in range(HEAD_PAIRS):
                dq = _unstacked(dqs[0], dqs[1], g) * 0.125
                dq_ref[b, :, g * 128:(g + 1) * 128] = dq.astype(BF16)
                dbq[g] = dbq[g] + jnp.sum(dq, axis=0, keepdims=True)
            dkv = jnp.concatenate([dk[ATT_BLK:], dv[ATT_BLK:]], axis=1) + carry[b]
            dkv_ref[b] = dkv.astype(BF16)
            dbkv = dbkv + jnp.sum(dkv, axis=0, keepdims=True)
            carry[b] = jnp.concatenate([dk[:ATT_BLK], dv[:ATT_BLK]], axis=1)
        dsink_ref[...] += dsink
        dbq_ref[...] += jnp.concatenate(dbq, axis=1)
        dbkv_ref[...] += dbkv

    q, kvc, kvp = _attn_specs(bsz, lambda j: nb - 1 - j)
    z3 = z0.reshape(bsz, seq, z0.shape[1])
    d3 = dmix.reshape(bsz, seq, dmix.shape[1])
    dq, dkv, dsink, dbq, dbkv = pl.pallas_call(
        body, name=name, grid=(nb,),
        in_specs=[pl.BlockSpec(memory_space=pltpu.SMEM), q, kvc, kvp,
                  pl.BlockSpec((bsz, ATT_BLK, 512), lambda j: (0, nb - 1 - j, 0))],
        out_specs=[pl.BlockSpec((bsz, ATT_BLK, 512), lambda j: (0, nb - 1 - j, 0)),
                   pl.BlockSpec((bsz, ATT_BLK, 256), lambda j: (0, nb - 1 - j, 0)), _row(128), _row(512), _row(256)],
        out_shape=[jax.ShapeDtypeStruct((bsz, seq, 512), BF16), jax.ShapeDtypeStruct((bsz, seq, 256), BF16),
                   jax.ShapeDtypeStruct((1, 128), F32), jax.ShapeDtypeStruct((1, 512), F32),
                   jax.ShapeDtypeStruct((1, 256), F32)],
        scratch_shapes=[pltpu.VMEM((bsz, ATT_BLK, 256), F32)],
        compiler_params=_params("arbitrary"))(sinks, z3, z3, z3, d3)
    return dq.reshape(t, 512), dkv.reshape(t, 256), dsink, dbq, dbkv


def _seq_specs(ts, nt, t, width, col):
    per = ts // HALO
    cur = pl.BlockSpec((ts, width), lambda b, i: (b * nt + i, col))
    prev = pl.BlockSpec((HALO, width), lambda b, i: (jnp.maximum((b * nt + i) * per - 1, 0), col))
    nxt = pl.BlockSpec((HALO, width), lambda b, i: (jnp.minimum((b * nt + i + 1) * per, t // HALO - 1), col))
    return prev, cur, nxt


SUB = 8
CONV_ROWS = 64


def _shifted_copies(src, sh, rows_first, rows_rest):
    for r in range(SUB):
        rows = rows_first if r == 0 else rows_rest
        sh[r, pl.ds(0, rows), :] = src[pl.ds(r, rows), :]


def _tap_sum(sh, w, offset, c0, rows):
    acc = None
    for k in range(CONV_K):
        o = offset(k)
        term = sh[o % SUB, pl.ds(c0 + o - o % SUB, rows), :] * w[k:k + 1, :]
        acc = term if acc is None else acc + term
    return acc


def _glu_rows(a_ref, g_ref, rows=slice(None)):
    return a_ref[rows, :].astype(F32) * jax.nn.sigmoid(g_ref[rows, :].astype(F32))


def _conv_fwd(z0, conv_w, conv_b, ln_g, ln_b, bsz, name):
    t = z0.shape[0]
    s = t // bsz
    ts = _seq_tile(s)
    nt = s // ts
    first = HALO - (CONV_K - 1)

    def body(ap_ref, ac_ref, gp_ref, gc_ref, w_ref, cb_ref, lg_ref, lb_ref, o_ref, y_ref, hbuf, sh):
        hbuf[0:HALO, :] = jnp.where(pl.program_id(1) > 0, _glu_rows(ap_ref, gp_ref), 0.0)
        hbuf[HALO:HALO + ts, :] = _glu_rows(ac_ref, gc_ref)
        _shifted_copies(hbuf, sh, ts + HALO, ts + HALO - SUB)
        w, cb, lg, lb = w_ref[...], cb_ref[...], lg_ref[...], lb_ref[...]
        for c0 in range(0, ts, CONV_ROWS):
            y = _tap_sum(sh, w, lambda k: first + k, c0, CONV_ROWS) + cb
            y_ref[c0:c0 + CONV_ROWS, :] = y
            o = _ln(y, lg, lb)[0]
            o_ref[c0:c0 + CONV_ROWS, :] = (o * jax.nn.sigmoid(o)).astype(BF16)

    ap, ac, _ = _seq_specs(ts, nt, t, 512, 1)
    gp, gc, _ = _seq_specs(ts, nt, t, 512, 2)
    tile = pl.BlockSpec((ts, 512), lambda b, i: (b * nt + i, 0))
    return pl.pallas_call(
        body, name=name, grid=(bsz, nt),
        in_specs=[ap, ac, gp, gc, _full((HALO, 512)), _row(512), _row(512), _row(512)],
        out_specs=[tile, tile],
        out_shape=[jax.ShapeDtypeStruct((t, 512), BF16), jax.ShapeDtypeStruct((t, 512), F32)],
        scratch_shapes=[pltpu.VMEM((HALO + ts, 512), F32), pltpu.VMEM((SUB, HALO + ts, 512), F32)],
        compiler_params=_params("parallel", "parallel"))(z0, z0, z0, z0, conv_w, conv_b, ln_g, ln_b)


def _conv_bwd(z0, y, dmix, conv_w, ln_g, ln_b, bsz, name):
    t = z0.shape[0]
    s = t // bsz
    ts = _seq_tile(s)
    nt = s // ts

    def body(ac_ref, gc_ref, yc_ref, yn_ref, dc_ref, dn_ref, w_ref, lg_ref, lb_ref,
             da_ref, dg_ref, dw_ref, dcb_ref, dlg_ref, dlb_ref, dba_ref, dbg_ref, hcur, dybuf, sh_dy):
        b, i = pl.program_id(0), pl.program_id(1)

        @pl.when((b == 0) & (i == 0))
        def _():
            for ref in (dw_ref, dcb_ref, dlg_ref, dlb_ref, dba_ref, dbg_ref):
                ref[...] = jnp.zeros_like(ref)

        w, lg, lb = w_ref[...], lg_ref[...], lb_ref[...]
        hcur[...] = _glu_rows(ac_ref, gc_ref)

        def d_conv_out(yv, dout):
            o, xhat, rstd = _ln(yv, lg, lb)
            sg_o = jax.nn.sigmoid(o)
            d_o = dout * sg_o * (1.0 + o * (1.0 - sg_o))
            return _ln_bwd(d_o, xhat, rstd, lg), d_o * xhat, d_o

        dlg = jnp.zeros((1, 512), F32)
        dlb = jnp.zeros((1, 512), F32)
        dcb = jnp.zeros((1, 512), F32)
        for c0 in range(0, ts, CONV_ROWS):
            rows = slice(c0, c0 + CONV_ROWS)
            dy, g_part, b_part = d_conv_out(yc_ref[rows, :], dc_ref[rows, :].astype(F32))
            dybuf[rows, :] = dy
            dlg = dlg + jnp.sum(g_part, axis=0, keepdims=True)
            dlb = dlb + jnp.sum(b_part, axis=0, keepdims=True)
            dcb = dcb + jnp.sum(dy, axis=0, keepdims=True)
        dn = jnp.where(i < nt - 1, dn_ref[...].astype(F32), 0.0)
        dybuf[ts:ts + HALO, :] = d_conv_out(yn_ref[...], dn)[0]
        dlg_ref[...] += dlg
        dlb_ref[...] += dlb
        dcb_ref[...] += dcb
        _shifted_copies(dybuf, sh_dy, ts + HALO - SUB, ts + HALO - SUB)

        for k in range(CONV_K):
            o = CONV_K - 1 - k
            prod = hcur[...] * sh_dy[o % SUB, pl.ds(o - o % SUB, ts), :]
            dw_ref[pl.ds(k, 1), :] += jnp.sum(prod, axis=0, keepdims=True)
        dba = jnp.zeros((1, 512), F32)
        dbg = jnp.zeros((1, 512), F32)
        for c0 in range(0, ts, CONV_ROWS):
            rows = slice(c0, c0 + CONV_ROWS)
            dh = _tap_sum(sh_dy, w, lambda k: CONV_K - 1 - k, c0, CONV_ROWS)
            a_c = ac_ref[rows, :].astype(F32)
            sg_c = jax.nn.sigmoid(gc_ref[rows, :].astype(F32))
            d_a = dh * sg_c
            d_g = dh * a_c * sg_c * (1.0 - sg_c)
            da_ref[rows, :] = d_a.astype(BF16)
            dg_ref[rows, :] = d_g.astype(BF16)
            dba = dba + jnp.sum(d_a, axis=0, keepdims=True)
            dbg = dbg + jnp.sum(d_g, axis=0, keepdims=True)
        dba_ref[...] += dba
        dbg_ref[...] += dbg

    _, ac, _ = _seq_specs(ts, nt, t, 512, 1)
    _, gc, _ = _seq_specs(ts, nt, t, 512, 2)
    _, yc, yn = _seq_specs(ts, nt, t, 512, 0)
    _, dc, dn = _seq_specs(ts, nt, t, 512, 1)
    tile = pl.BlockSpec((ts, 512), lambda b, i: (b * nt + i, 0))
    vec = jax.ShapeDtypeStruct((1, 512), F32)
    return pl.pallas_call(
        body, name=name, grid=(bsz, nt),
        in_specs=[ac, gc, yc, yn, dc, dn, _full((HALO, 512)), _row(512), _row(512)],
        out_specs=[tile, tile, _full((HALO, 512)), _row(512), _row(512), _row(512), _row(512), _row(512)],
        out_shape=[jax.ShapeDtypeStruct((t, 512), BF16), jax.ShapeDtypeStruct((t, 512), BF16),
                   jax.ShapeDtypeStruct((HALO, 512), F32), vec, vec, vec, vec, vec],
        scratch_shapes=[pltpu.VMEM((ts, 512), F32), pltpu.VMEM((ts + HALO, 512), F32),
                        pltpu.VMEM((SUB, HALO + ts, 512), F32)],
        compiler_params=_params("arbitrary", "arbitrary"))(z0, z0, y, y, dmix, dmix, conv_w, ln_g, ln_b)


def _pooled(pbuf, g, ts, tok):
    w = 2 << g
    cols = slice(128 * g, 128 * (g + 1))
    sm = pbuf[pl.ds(HALO, ts), cols]
    for d in range(1, w):
        sm = sm + pbuf[pl.ds(HALO - d, ts), cols]
    cnt = jnp.minimum(tok + 1, w).astype(F32)
    return sm / cnt - pbuf[pl.ds(HALO, ts), cols]


def _pool_fwd(z1, w_pool, scale, bsz, name):
    t = z1.shape[0]
    s = t // bsz
    ts = _seq_tile(s)
    nt = s // ts

    def body(zp_ref, zc_ref, wp_ref, sc_ref, o_ref, token, pbuf):
        token[...] = jnp.zeros_like(token)
        i = pl.program_id(1)
        pbuf[0:HALO, :] = jnp.where(i > 0, zp_ref[...].astype(F32), 0.0)
        pbuf[HALO:HALO + ts, :] = zc_ref[...].astype(F32)
        tok = i * ts + lax.broadcasted_iota(jnp.int32, (ts, 1), 0)
        for g in range(4):
            cols = slice(128 * g, 128 * (g + 1))
            pooled = _pooled(pbuf, g, ts, tok).astype(BF16)
            o_ref[:, cols] = (_nn(pooled, wp_ref[g].astype(BF16)) * sc_ref[:, cols]).astype(BF16)

    zp, zc, _ = _seq_specs(ts, nt, t, 512, 0)
    return pl.pallas_call(
        body, name=name, grid=(bsz, nt), in_specs=[zp, zc, _full((4, 128, 128)), _row(512)],
        out_specs=[pl.BlockSpec((ts, 512), lambda b, i: (b * nt + i, 0)), _full((8, 128))],
        out_shape=[jax.ShapeDtypeStruct((t, 512), BF16), jax.ShapeDtypeStruct((8, 128), F32)],
        scratch_shapes=[pltpu.VMEM((HALO + ts, 512), F32)],
        compiler_params=_params("arbitrary", "arbitrary"))(z1, z1, w_pool, scale)


def _pool_bwd(z1, dmix, w_pool, scale, bsz, name):
    t = z1.shape[0]
    s = t // bsz
    ts = _seq_tile(s)
    nt = s // ts
    rr = ts + HALO

    def body(zp_ref, zc_ref, dc_ref, dn_ref, wp_ref, sc_ref, dz_ref, dwp_ref, dsc_ref, pbuf, ebuf):
        b, i = pl.program_id(0), pl.program_id(1)

        @pl.when((b == 0) & (i == 0))
        def _():
            dwp_ref[...] = jnp.zeros_like(dwp_ref)
            dsc_ref[...] = jnp.zeros_like(dsc_ref)

        pbuf[0:HALO, :] = jnp.where(i > 0, zp_ref[...].astype(F32), 0.0)
        pbuf[HALO:HALO + ts, :] = zc_ref[...].astype(F32)
        dn = jnp.where(i < nt - 1, dn_ref[...].astype(F32), 0.0)
        dout = jnp.concatenate([dc_ref[...].astype(F32), dn], axis=0)
        tok = i * ts + lax.broadcasted_iota(jnp.int32, (ts, 1), 0)
        tok_r = i * ts + lax.broadcasted_iota(jnp.int32, (rr, 1), 0)
        for g in range(4):
            w = 2 << g
            cols = slice(128 * g, 128 * (g + 1))
            wg = wp_ref[g].astype(BF16)
            pooled = _pooled(pbuf, g, ts, tok).astype(BF16)
            dsc_ref[:, cols] += jnp.sum(dout[:ts, cols] * _nn(pooled, wg), axis=0, keepdims=True)
            dy = (dout[:, cols] * sc_ref[:, cols]).astype(BF16)
            dwp_ref[g] += _tn(pooled, dy[:ts])
            dpl = _nt(dy, wg)
            ebuf[...] = dpl / jnp.minimum(tok_r + 1, w).astype(F32)
            dz = ebuf[pl.ds(0, ts), :] - dpl[:ts]
            for d in range(1, w):
                dz = dz + ebuf[pl.ds(d, ts), :]
            dz_ref[:, cols] = dz.astype(BF16)

    zp, zc, _ = _seq_specs(ts, nt, t, 512, 0)
    _, dc, dn = _seq_specs(ts, nt, t, 512, 0)
    return pl.pallas_call(
        body, name=name, grid=(bsz, nt), in_specs=[zp, zc, dc, dn, _full((4, 128, 128)), _row(512)],
        out_specs=[pl.BlockSpec((ts, 512), lambda b, i: (b * nt + i, 0)), _full((4, 128, 128)), _row(512)],
        out_shape=[jax.ShapeDtypeStruct((t, 512), BF16), jax.ShapeDtypeStruct((4, 128, 128), F32),
                   jax.ShapeDtypeStruct((1, 512), F32)],
        scratch_shapes=[pltpu.VMEM((HALO + ts, 512), F32), pltpu.VMEM((rr, 128), F32)],
        compiler_params=_params("arbitrary", "arbitrary"))(z1, z1, dmix, dmix, w_pool, scale)


def _tril():
    r = lax.broadcasted_iota(jnp.int32, (SGU_CHUNK, SGU_CHUNK), 0)
    c = lax.broadcasted_iota(jnp.int32, (SGU_CHUNK, SGU_CHUNK), 1)
    return r >= c


def _sgu_fwd(z1, ln_g, ln_b, w_s, b_rows, name):
    t = z1.shape[0]
    ts = _tile(t)

    def body(zu_ref, zv_ref, lg_ref, lb_ref, ws_ref, bs_ref, o_ref):
        v = _gelu(zv_ref[...].astype(F32))[0]
        vb = _ln(v, lg_ref[...], lb_ref[...])[0].astype(BF16)
        tril = _tril()
        for g in range(4):
            cols = slice(128 * g, 128 * (g + 1))
            wg = jnp.where(tril, ws_ref[g], 0.0).astype(BF16)
            for c in range(ts // SGU_CHUNK):
                rows = slice(SGU_CHUNK * c, SGU_CHUNK * (c + 1))
                mixed = _nn(wg, vb[rows, cols]) + bs_ref[g]
                o_ref[rows, cols] = (_gelu(zu_ref[rows, cols].astype(F32))[0] * mixed).astype(BF16)

    return pl.pallas_call(
        body, name=name, grid=(t // ts,),
        in_specs=[pl.BlockSpec((ts, 512), lambda i: (i, 1)), pl.BlockSpec((ts, 512), lambda i: (i, 2)),
                  _row(512), _row(512), _full((4, 128, 128)), _full((4, 128, 128))],
        out_specs=pl.BlockSpec((ts, 512), lambda i: (i, 0)), out_shape=jax.ShapeDtypeStruct((t, 512), BF16),
        compiler_params=_params("parallel"))(z1, z1, ln_g, ln_b, w_s, b_rows)


def _sgu_bwd(z1, dmix, ln_g, ln_b, w_s, b_rows, name):
    t = z1.shape[0]
    ts = _tile(t)

    def body(zu_ref, zv_ref, d_ref, lg_ref, lb_ref, ws_ref, bs_ref,
             dzu_ref, dzv_ref, dws_ref, dbs_ref, dlg_ref, dlb_ref, dvbuf):
        @pl.when(pl.program_id(0) == 0)
        def _():
            for ref in (dws_ref, dbs_ref, dlg_ref, dlb_ref):
                ref[...] = jnp.zeros_like(ref)

        zv = zv_ref[...].astype(F32)
        v, thv = _gelu(zv)
        lg = lg_ref[...]
        vln, xhat, rstd = _ln(v, lg, lb_ref[...])
        vb = vln.astype(BF16)
        tril = _tril()
        for g in range(4):
            cols = slice(128 * g, 128 * (g + 1))
            wg = jnp.where(tril, ws_ref[g], 0.0).astype(BF16)
            dws = jnp.zeros((SGU_CHUNK, SGU_CHUNK), F32)
            dbs = jnp.zeros((1, SGU_CHUNK), F32)
            for c in range(ts // SGU_CHUNK):
                rows = slice(SGU_CHUNK * c, SGU_CHUNK * (c + 1))
                vbc = vb[rows, cols]
                mixed = _nn(wg, vbc) + bs_ref[g]
                zu = zu_ref[rows, cols].astype(F32)
                u, thu = _gelu(zu)
                dout = d_ref[rows, cols].astype(F32)
                dzu_ref[rows, cols] = (dout * mixed * _gelu_grad(zu, thu)).astype(BF16)
                dm = dout * u
                dmb = dm.astype(BF16)
                dws = dws + _nt(dmb, vbc)
                dbs = dbs + jnp.sum(dm.T, axis=0, keepdims=True)
                dvbuf[rows, cols] = _tn(wg, dmb)
            dws_ref[g] += jnp.where(tril, dws, 0.0)
            dbs_ref[pl.ds(g, 1), :] += dbs
        dvln = dvbuf[...]
        dlg_ref[...] += jnp.sum(dvln * xhat, axis=0, keepdims=True)
        dlb_ref[...] += jnp.sum(dvln, axis=0, keepdims=True)
        dzv_ref[...] = (_ln_bwd(dvln, xhat, rstd, lg) * _gelu_grad(zv, thv)).astype(BF16)

    tile = pl.BlockSpec((ts, 512), lambda i: (i, 0))
    vec = jax.ShapeDtypeStruct((1, 512), F32)
    return pl.pallas_call(
        body, name=name, grid=(t // ts,),
        in_specs=[pl.BlockSpec((ts, 512), lambda i: (i, 1)), pl.BlockSpec((ts, 512), lambda i: (i, 2)),
                  pl.BlockSpec((ts, 512), lambda i: (i, 1)), _row(512), _row(512), _full((4, 128, 128)),
                  _full((4, 128, 128))],
        out_specs=[tile, tile, _full((4, 128, 128)), _full((4, 128)), _row(512), _row(512)],
        out_shape=[jax.ShapeDtypeStruct((t, 512), BF16), jax.ShapeDtypeStruct((t, 512), BF16),
                   jax.ShapeDtypeStruct((4, 128, 128), F32), jax.ShapeDtypeStruct((4, 128), F32), vec, vec],
        scratch_shapes=[pltpu.VMEM((ts, 512), F32)],
        compiler_params=_params("arbitrary"))(z1, z1, dmix, ln_g, ln_b, w_s, b_rows)


def _row_tile(r):
    for cand in (512, 352, 256, 192, 128, 64, 32, 16, 8):
        if r % cand == 0:
            return cand
    return r


def _sum_slabs(a, name):
    k, r, c = a.shape
    tr = _row_tile(r)

    def body(*refs):
        acc = refs[0][...].astype(F32)
        for ref in refs[1:-1]:
            acc = acc + ref[...].astype(F32)
        refs[-1][...] = acc

    in_specs = [pl.BlockSpec((None, tr, c), functools.partial(lambda i, s: (s, i, 0), s=s)) for s in range(k)]
    return pl.pallas_call(
        body, name=name, grid=(r // tr,), in_specs=in_specs, out_specs=pl.BlockSpec((tr, c), lambda i: (i, 0)),
        out_shape=jax.ShapeDtypeStruct((r, c), F32), compiler_params=_params("parallel"))(*([a] * k))


def _adamw_math(w, g, m, v):
    mn = ADAM_B1 * m + (1.0 - ADAM_B1) * g
    vn = ADAM_B2 * v + (1.0 - ADAM_B2) * (g * g)
    m_hat = mn / (1.0 - ADAM_B1 ** ADAM_STEP)
    v_hat = vn / (1.0 - ADAM_B2 ** ADAM_STEP)
    return -ADAM_LR * (m_hat / (jnp.sqrt(v_hat) + ADAM_EPS) + ADAM_WD * w), mn, vn


def _reduce_adamw(landing, w, m, v, name, layer=None, into=None):
    k, r, c = landing.shape
    tr = _row_tile(r)
    n_into = 0 if into is None else 4

    def body(*refs):
        slabs, (w_ref, m_ref, v_ref) = refs[:k], refs[k:k + 3]
        g_ref, d_ref, mo_ref, vo_ref = refs[k + 3 + n_into:]
        g = slabs[0][...].astype(F32)
        for ref in slabs[1:]:
            g = g + ref[...].astype(F32)
        g_ref[...] = g
        d_ref[...], mo_ref[...], vo_ref[...] = _adamw_math(w_ref[...], g, m_ref[...], v_ref[...])

    if layer is None:
        spec = pl.BlockSpec((tr, c), lambda i: (i, 0))
    else:
        spec = pl.BlockSpec((None, tr, c), lambda i: (layer, i, 0))
    in_specs = [pl.BlockSpec((None, tr, c), functools.partial(lambda i, s: (s, i, 0), s=s)) for s in range(k)]
    in_specs += [spec] * 3 + [ANY] * n_into
    shape = jax.ShapeDtypeStruct(w.shape, F32)
    return pl.pallas_call(
        body, name=name, grid=(r // tr,), in_specs=in_specs, out_specs=[spec] * 4, out_shape=[shape] * 4,
        input_output_aliases={k + 3 + j: j for j in range(n_into)},
        compiler_params=_params("parallel"))(*([landing] * k), w, m, v, *(into or ()))


def _rows_needed(shape):
    return shape[0] * -(-shape[1] // D)


def _as_rows(a):
    r, c = a.shape
    n = -(-c // D)
    assert r == 1 or n == 1
    return jnp.pad(a, ((0, 0), (0, n * D - c))).reshape(r * n, D)


def _adamw_replicated(g_rows, w, m, v, name):
    n = len(REP_2D)

    def body(*refs):
        g_ref, w_refs, m_refs, v_refs, outs = refs[0], refs[1:1 + n], refs[1 + n:1 + 2 * n], refs[1 + 2 * n:1 + 3 * n], refs[1 + 3 * n:]
        r0 = 0
        for k, (_, (r, c)) in enumerate(REP_2D):
            pieces = [g_ref[r0 + j * r:r0 + j * r + r, 0:min(D, c - j * D)] for j in range(-(-c // D))]
            g = pieces[0] if len(pieces) == 1 else jnp.concatenate(pieces, axis=1)
            outs[4 * k][...] = g
            outs[4 * k + 1][...], outs[4 * k + 2][...], outs[4 * k + 3][...] = _adamw_math(
                w_refs[k][...], g, m_refs[k][...], v_refs[k][...])
            r0 += _rows_needed((r, c))

    shapes = [s for _, s in REP_2D]
    return pl.pallas_call(
        body, name=name, in_specs=[_full(g_rows.shape)] + [_full(s) for s in shapes] * 3,
        out_specs=[_full(s) for s in shapes for _ in range(4)],
        out_shape=[jax.ShapeDtypeStruct(s, F32) for s in shapes for _ in range(4)],
        grid=(1,), compiler_params=_params("arbitrary"))(g_rows, *w, *m, *v)


def _adamw(w, g, m, v, name):
    r, c = w.shape
    tr = _row_tile(r)

    def body(w_ref, g_ref, m_ref, v_ref, d_ref, mo_ref, vo_ref):
        d_ref[...], mo_ref[...], vo_ref[...] = _adamw_math(w_ref[...], g_ref[...], m_ref[...], v_ref[...])

    spec = pl.BlockSpec((tr, c), lambda i: (i, 0))
    shape = jax.ShapeDtypeStruct((r, c), F32)
    return pl.pallas_call(
        body, name=name, grid=(r // tr,), in_specs=[spec] * 4, out_specs=[spec] * 3, out_shape=[shape] * 3,
        compiler_params=_params("parallel"))(w, g, m, v)


ANY = pl.BlockSpec(memory_space=pl.ANY)


def _all_gather(block, name):
    r, c_dim = block.shape

    def body(x_ref, out_ref, token, send_sems, recv_sems, local_sem):
        token[...] = jnp.zeros_like(token)
        x, y, c = lax.axis_index("x"), lax.axis_index("y"), lax.axis_index("c")
        me, sibling = (x, y, c), (x, y, 1 - c)
        chips = [(1 - x, y), (x, 1 - y), (1 - x, 1 - y)]

        def rows(px, py, pc):
            return out_ref.at[4 * px + 2 * py + pc]

        def copy(k, blk, to, src=None):
            return pltpu.make_async_remote_copy(
                src_ref=rows(*blk) if src is None else src, dst_ref=rows(*blk), send_sem=send_sems.at[k],
                recv_sem=recv_sems.at[k], device_id=to, device_id_type=MESH)

        mine = pltpu.make_async_copy(x_ref, rows(*me), local_sem)
        mine.start()
        first = [copy(0, me, sibling, src=x_ref)]
        first += [copy(1 + j, me, (*chip, c), src=x_ref) for j, chip in enumerate(chips)]
        for cp in first:
            cp.start()
        passed = [copy(4 + j, (*chip, c), sibling) for j, chip in enumerate(chips)]
        for j, chip in enumerate(chips):
            copy(1 + j, (*chip, c), me).wait_recv()
            passed[j].start()
        copy(0, sibling, me).wait_recv()
        for j, chip in enumerate(chips):
            copy(4 + j, (*chip, 1 - c), me).wait_recv()
        for cp in first + passed:
            cp.wait_send()
        mine.wait()

    return pl.pallas_call(
        body, name=name, in_specs=[ANY], out_specs=[ANY, pl.BlockSpec(memory_space=pltpu.VMEM)],
        out_shape=[jax.ShapeDtypeStruct((N_DEV, r, c_dim), block.dtype), jax.ShapeDtypeStruct((8, 128), F32)],
        scratch_shapes=[pltpu.SemaphoreType.DMA((7,)), pltpu.SemaphoreType.DMA((7,)), pltpu.SemaphoreType.DMA],
    )(block)


HBM = pl.BlockSpec(memory_space=pltpu.HBM)
SEM = pl.BlockSpec(memory_space=pltpu.SEMAPHORE)
EFFECT = pltpu.SideEffectType.DATAFLOW_SIDE_EFFECTING


def _exchange_copies(scatter, src_refs, land_refs, send_sems, recv_sems, local_sems):
    x, y, c = lax.axis_index("x"), lax.axis_index("y"), lax.axis_index("c")
    me = 4 * x + 2 * y + c
    sends, arrivals, locals_ = [], [], []
    for a, (src, land) in enumerate(zip(src_refs, land_refs)):
        def pick(idx, src=src):
            return src.at[idx] if scatter else src

        locals_.append(pltpu.make_async_copy(pick(me), land.at[me], local_sems.at[a]))
        for r in range(1, N_DEV):
            px = 1 - x if r & 4 else x
            py = 1 - y if r & 2 else y
            pc = 1 - c if r & 1 else c
            peer, s = 4 * px + 2 * py + pc, 7 * a + r - 1
            sends.append(pltpu.make_async_remote_copy(
                src_ref=pick(peer), dst_ref=land.at[me], send_sem=send_sems.at[s], recv_sem=recv_sems.at[s],
                device_id=(px, py, pc), device_id_type=MESH))
            arrivals.append(pltpu.make_async_remote_copy(
                src_ref=pick(peer), dst_ref=land.at[peer], send_sem=send_sems.at[s], recv_sem=recv_sems.at[s],
                device_id=(px, py, pc), device_id_type=MESH))
    return sends, arrivals, locals_


def _exchange_start(srcs, scatter, name):
    n = len(srcs)
    lands = [lax.empty((N_DEV,) + s.shape[-2:], s.dtype) for s in srcs]

    def body(*refs):
        src_refs, land_refs = refs[:n], refs[n:2 * n]
        send_sems, recv_sems, local_sems = refs[2 * n:2 * n + 3]
        token = refs[-1]
        sends, _, locals_ = _exchange_copies(scatter, src_refs, land_refs, send_sems, recv_sems, local_sems)
        for cp in locals_ + sends:
            cp.start()
        token[...] = jnp.zeros_like(token)

    res = pl.pallas_call(
        body, name=name,
        out_shape=[pltpu.SemaphoreType.DMA((7 * n,)), pltpu.SemaphoreType.DMA((7 * n,)), pltpu.SemaphoreType.DMA((n,))]
        + [pltpu.HBM(a.shape, a.dtype) for a in list(srcs) + lands] + [jax.ShapeDtypeStruct((8, 128), F32)],
        in_specs=[HBM] * (2 * n), out_specs=[SEM] * 3 + [HBM] * (2 * n) + [pl.BlockSpec(memory_space=pltpu.VMEM)],
        input_output_aliases={i: 3 + i for i in range(2 * n)},
        compiler_params=pltpu.CompilerParams(has_side_effects=EFFECT),
    )(*[pltpu.with_memory_space_constraint(a, pltpu.HBM) for a in list(srcs) + lands])
    return (n, scatter, res[:3], res[3:3 + 2 * n]), res[-1]


def _exchange_wait(handle, after, name):
    n, scatter, sems, thru = handle

    def body(*refs):
        src_refs, land_refs = refs[:n], refs[n:2 * n]
        send_sems, recv_sems, local_sems = refs[2 * n:2 * n + 3]
        sends, arrivals, locals_ = _exchange_copies(scatter, src_refs, land_refs, send_sems, recv_sems, local_sems)
        for cp in arrivals:
            cp.wait_recv()
        for cp in sends:
            cp.wait_send()
        for cp in locals_:
            cp.wait()

    res = pl.pallas_call(
        body, name=name, out_shape=[pltpu.HBM(a.shape, a.dtype) for a in thru],
        in_specs=[HBM] * (2 * n) + [SEM] * 3 + [ANY], out_specs=[HBM] * (2 * n),
        input_output_aliases={i: i for i in range(2 * n)},
        compiler_params=pltpu.CompilerParams(has_side_effects=EFFECT),
    )(*thru, *sems, after)
    return res[n:]


def _behind(tokens, a):
    zero = sum(tok[0, 0] for tok in tokens)
    return jax.tree.map(lambda v: v + zero.astype(v.dtype), a)


def _perm_heads(a, perm, axis):
    idx = [slice(None)] * a.ndim
    parts = []
    for h in perm:
        idx[axis] = slice(64 * h, 64 * (h + 1))
        parts.append(a[tuple(idx)])
    idx[axis] = slice(512, None)
    if a.shape[axis] > 512:
        parts.append(a[tuple(idx)])
    return jnp.concatenate(parts, axis=axis)


Q_INV = tuple(int(i) for i in np.argsort(Q_PERM))


def _in0_to_kernel(a, axis):
    a = _perm_heads(a, Q_PERM, axis)
    idx = [slice(None)] * a.ndim

    def cut(lo, hi):
        idx[axis] = slice(lo, hi)
        return a[tuple(idx)]

    return jnp.concatenate([cut(0, 512), cut(768, 1792), cut(512, 768)], axis=axis)


def _in0_from_kernel(a, axis):
    idx = [slice(None)] * a.ndim

    def cut(lo, hi):
        idx[axis] = slice(lo, hi)
        return a[tuple(idx)]

    a = jnp.concatenate([cut(0, 512), cut(1536, 1792), cut(512, 1536)], axis=axis)
    return _perm_heads(a, Q_INV, axis)


def _f32_as_u16_rows(v, rows):
    bits = lax.bitcast_convert_type(v, jnp.uint16).reshape(-1)
    return jnp.pad(bits, (0, rows * D - bits.shape[0])).reshape(rows, D)


def _pad_rows(v, rows):
    v = v.reshape(-1)
    return jnp.pad(v, (0, rows * D - v.shape[0])).reshape(rows, D)


def kernel(x, mix_norm, a_w_in, a_b_in, a_sinks, a_conv_w, a_conv_b, a_cln_g, a_cln_b, a_w_out, c_w_in, c_w_pool, c_pool_scale, c_sln_g, c_sln_b, c_w_s, c_b_s, c_w_out, ffn_norm, ffn_w_gate, ffn_w_up, ffn_w_down, final_norm, loss_target, m_mix_norm, m_a_w_in, m_a_b_in, m_a_sinks, m_a_conv_w, m_a_conv_b, m_a_cln_g, m_a_cln_b, m_a_w_out, m_c_w_in, m_c_w_pool, m_c_pool_scale, m_c_sln_g, m_c_sln_b, m_c_w_s, m_c_b_s, m_c_w_out, m_ffn_norm, m_ffn_w_gate, m_ffn_w_up, m_ffn_w_down, m_final_norm, v_mix_norm, v_a_w_in, v_a_b_in, v_a_sinks, v_a_conv_w, v_a_conv_b, v_a_cln_g, v_a_cln_b, v_a_w_out, v_c_w_in, v_c_w_pool, v_c_pool_scale, v_c_sln_g, v_c_sln_b, v_c_w_s, v_c_b_s, v_c_w_out, v_ffn_norm, v_ffn_w_gate, v_ffn_w_up, v_ffn_w_down, v_final_norm):
    bsz, seq, _ = x.shape
    t = bsz * seq
    w_in = dict(mix_norm=mix_norm, a_w_in=a_w_in, a_b_in=a_b_in, a_sinks=a_sinks, a_conv_w=a_conv_w, a_conv_b=a_conv_b,
                a_cln_g=a_cln_g, a_cln_b=a_cln_b, a_w_out=a_w_out, c_w_in=c_w_in, c_w_pool=c_w_pool,
                c_pool_scale=c_pool_scale, c_sln_g=c_sln_g, c_sln_b=c_sln_b, c_w_s=c_w_s, c_b_s=c_b_s, c_w_out=c_w_out,
                ffn_norm=ffn_norm, ffn_w_gate=ffn_w_gate, ffn_w_up=ffn_w_up, ffn_w_down=ffn_w_down, final_norm=final_norm)
    m_in = dict(mix_norm=m_mix_norm, a_w_in=m_a_w_in, a_b_in=m_a_b_in, a_sinks=m_a_sinks, a_conv_w=m_a_conv_w,
                a_conv_b=m_a_conv_b, a_cln_g=m_a_cln_g, a_cln_b=m_a_cln_b, a_w_out=m_a_w_out, c_w_in=m_c_w_in,
                c_w_pool=m_c_w_pool, c_pool_scale=m_c_pool_scale, c_sln_g=m_c_sln_g, c_sln_b=m_c_sln_b, c_w_s=m_c_w_s,
                c_b_s=m_c_b_s, c_w_out=m_c_w_out, ffn_norm=m_ffn_norm, ffn_w_gate=m_ffn_w_gate, ffn_w_up=m_ffn_w_up,
                ffn_w_down=m_ffn_w_down, final_norm=m_final_norm)
    v_in = dict(mix_norm=v_mix_norm, a_w_in=v_a_w_in, a_b_in=v_a_b_in, a_sinks=v_a_sinks, a_conv_w=v_a_conv_w,
                a_conv_b=v_a_conv_b, a_cln_g=v_a_cln_g, a_cln_b=v_a_cln_b, a_w_out=v_a_w_out, c_w_in=v_c_w_in,
                c_w_pool=v_c_w_pool, c_pool_scale=v_c_pool_scale, c_sln_g=v_c_sln_g, c_sln_b=v_c_sln_b, c_w_s=v_c_w_s,
                c_b_s=v_c_b_s, c_w_out=v_c_w_out, ffn_norm=v_ffn_norm, ffn_w_gate=v_ffn_w_gate, ffn_w_up=v_ffn_w_up,
                ffn_w_down=v_ffn_w_down, final_norm=v_final_norm)

    small = jnp.concatenate([a_conv_w[0].reshape(-1), c_pool_scale[0], c_sln_g[0], c_sln_b[0]])
    first_bits = lax.bitcast_convert_type(a_w_in[0].T.astype(BF16), jnp.uint16)
    gathered, tok = _all_gather(jnp.concatenate([first_bits, _f32_as_u16_rows(small, W_MISC_ROWS)], axis=0), "gather_mixer0")

    def ffn_shards(l):
        return [ffn_w_gate[l].T.astype(BF16), ffn_w_up[l].T.astype(BF16), ffn_w_down[l].astype(BF16)]

    ffn0_h, tok = _exchange_start(_behind([tok], ffn_shards(0) + [a_w_out[0].astype(BF16)]), False, "gather_ffn0_start")
    mix1_h, tok = _exchange_start(_behind([tok], [c_w_in[0].T.astype(BF16), c_w_out[0].astype(BF16)]), False,
                                  "gather_mixer1_start")
    ffn1_h, tok = _exchange_start(_behind([tok], ffn_shards(1)), False, "gather_ffn1_start")

    a_in_full = lax.bitcast_convert_type(gathered[:, :224].reshape(IN0, D), BF16)
    small_all = lax.bitcast_convert_type(
        gathered[:, 224:].reshape(N_DEV, -1)[:, :2 * SMALL_SHARD].reshape(N_DEV, SMALL_SHARD, 2), F32)
    conv_w = small_all[:, :31 * 64].reshape(N_DEV, 31, 64).transpose(1, 0, 2).reshape(31, 512)
    conv_w = jnp.pad(conv_w, ((0, HALO - CONV_K), (0, 0)))
    pool_scale = small_all[:, 31 * 64:31 * 64 + 64].reshape(1, 512)
    sln_g = small_all[:, 31 * 64 + 64:31 * 64 + 128].reshape(1, 512)
    sln_b = small_all[:, 31 * 64 + 128:].reshape(1, 512)

    wt_in0 = _in0_to_kernel(a_in_full, 0)
    b_in0 = _in0_to_kernel(a_b_in, 1)
    b_rows = jnp.broadcast_to(c_b_s[0][:, :, None], (4, 128, 128))
    conv_b, cln_g, cln_b = a_conv_b, a_cln_g, a_cln_b

    h0 = x.reshape(t, D)
    target = loss_target.reshape(t, D)
    z0, hn0 = _norm_proj(h0, _behind([tok], mix_norm[0:1]), wt_in0, b_in0, "in_proj0")
    attn, tok = _attn_fwd(z0, a_sinks, bsz, "attn_fwd")
    conv, conv_y = _conv_fwd(z0, conv_w, conv_b, cln_g, _behind([tok], cln_b), bsz, "conv_fwd")
    wtg0, wtu0, wd0, a_out_full = (w.reshape(-1, D) for w in _exchange_wait(ffn0_h, conv, "gather_ffn0_wait"))
    w_out0 = _perm_heads(a_out_full, Q_PERM, 0)
    h1, hnf0, gate0a, up0a, part = _ffn_fwd_a(h0, attn, conv, w_out0, ffn_norm[0:1], wtg0, wtu0, wd0, "ffn_fwd0a")
    gate0b, up0b, h2 = _ffn_fwd_b(hnf0, part, wtg0, wtu0, wd0, "ffn_fwd0b")
    wt_in1, w_out1 = (w.reshape(-1, D) for w in _exchange_wait(mix1_h, h2, "gather_mixer1_wait"))
    z1, hn1 = _norm_proj(h2, mix_norm[1:2], wt_in1, None, "in_proj1")
    pool, tok = _pool_fwd(z1, c_w_pool[0], pool_scale, bsz, "pool_fwd")
    sgu = _sgu_fwd(z1, sln_g, _behind([tok], sln_b), c_w_s[0], b_rows, "sgu_fwd")
    wtg1, wtu1, wd1 = (w.reshape(D_FF, D) for w in _exchange_wait(ffn1_h, sgu, "gather_ffn1_wait"))
    h3, hnf1, gate1a, up1a, part = _ffn_fwd_a(h2, pool, sgu, w_out1, ffn_norm[1:2], wtg1, wtu1, wd1, "ffn_fwd1a")
    gate1b, up1b, dh4, d_final_norm, loss_part = _ffn_fwd_b(hnf1, part, wtg1, wtu1, wd1, "ffn_fwd1b",
                                                            head=(final_norm.reshape(1, D), target))

    def blocks(g):
        return g.reshape(N_DEV, g.shape[0] // N_DEV, D)

    *wide, part = _ffn_bwd_a(dh4, gate1a, up1a, wtg1, wtu1, wd1, "ffn_bwd1a")
    dh3, dmix1, dgate1, dup1, act1, d_fn1 = _ffn_bwd_b(dh4, h3, ffn_norm[1:2], gate1b, up1b, wtg1, wtu1, wd1, w_out1, part,
                                                       wide, "ffn_bwd1b")
    gw_ffn1 = [_mm_tn(dgate1, hnf1, "dw_gate1"), _mm_tn(dup1, hnf1, "dw_up1"), _mm_tn(act1, dh4, "dw_down1")]
    ffn1_g, tok = _exchange_start([blocks(g) for g in gw_ffn1], True, "scatter_ffn1_start")
    gw_c_out = _mm_tn_pieces([pool, sgu], dh3, "dw_out1")
    dzp, d_w_pool, d_pool_scale = _pool_bwd(z1, dmix1, c_w_pool[0], _behind([tok], pool_scale), bsz, "pool_bwd")
    dzu, dzv, d_w_s, d_b_s, d_sln_g, d_sln_b = _sgu_bwd(z1, dmix1, sln_g, sln_b, c_w_s[0], b_rows, "sgu_bwd")
    dh2, d_mn1 = _proj_bwd_norm([(dzp, 0), (dzu, 512), (dzv, 1024)], wt_in1, h2, dh3, mix_norm[1:2], BF16, "in_proj1_bwd")
    gw_c_in = _mm_tn_pieces([dzp, dzu, dzv], hn1, "dw_in1")
    mix1_g, tok = _exchange_start([blocks(gw_c_in), blocks(gw_c_out)], True, "scatter_mixer1_start")
    *wide, part = _ffn_bwd_a(dh2, gate0a, up0a, wtg0, wtu0, wd0, "ffn_bwd0a")
    dh1, dmix0, dgate0, dup0, act0, d_fn0 = _ffn_bwd_b(dh2, h1, _behind([tok], ffn_norm[0:1]), gate0b, up0b, wtg0, wtu0, wd0,
                                                       w_out0, part, wide, "ffn_bwd0b")
    gw_ffn0 = [_mm_tn(dgate0, hnf0, "dw_gate0"), _mm_tn(dup0, hnf0, "dw_up0"), _mm_tn(act0, dh2, "dw_down0")]
    ffn0_g, tok = _exchange_start([blocks(g) for g in gw_ffn0], True, "scatter_ffn0_start")
    gw_a_out = _perm_heads(_mm_tn_pieces([attn, conv], dh1, "dw_out0"), Q_INV, 0)
    dq, dkv, d_sink_row, d_bq, d_bkv = _attn_bwd(z0, dmix0, _behind([tok], a_sinks), bsz, "attn_bwd")
    dca, dcg, d_conv_w, d_conv_b, d_cln_g, d_cln_b, d_ba, d_bg = _conv_bwd(z0, conv_y, dmix0, conv_w, cln_g, cln_b, bsz, "conv_bwd")
    gw_a_in = _in0_from_kernel(_mm_tn_pieces([dq, dca, dcg, dkv], hn0, "dw_in0"), 0)
    mix0_g, tok = _exchange_start([blocks(gw_a_in), blocks(gw_a_out)], True, "scatter_mixer0_start")
    dx, d_mn0 = _proj_bwd_norm([(dq, 0), (dca, 512), (dcg, 1024), (dkv, 1536)], wt_in0, h0, dh1,
                               _behind([tok], mix_norm[0:1]), F32, "in_proj0_bwd")
    d_b_in = _in0_from_kernel(jnp.concatenate([d_bq, d_ba, d_bg, d_bkv], axis=1), 1)

    rep = dict(mix_norm=jnp.concatenate([d_mn0, d_mn1], axis=0), a_b_in=d_b_in, a_sinks=d_sink_row[:, :8],
               a_conv_b=d_conv_b, a_cln_g=d_cln_g, a_cln_b=d_cln_b, c_w_pool=d_w_pool.reshape(64, D),
               c_w_s=d_w_s.reshape(64, D), c_b_s=d_b_s, ffn_norm=jnp.concatenate([d_fn0, d_fn1], axis=0),
               final_norm=d_final_norm)
    rep_rows = jnp.concatenate([_as_rows(rep[nm]) for nm, _ in REP_2D] + [_as_rows(loss_part)], axis=0)
    rep_flat = jnp.pad(rep_rows, ((0, N_DEV * REP_ROWS - rep_rows.shape[0]), (0, 0))).reshape(N_DEV, REP_ROWS, D)
    small_g = jnp.concatenate([
        d_conv_w[:CONV_K].reshape(31, N_DEV, 64).transpose(1, 0, 2).reshape(N_DEV, 31 * 64),
        d_pool_scale.reshape(N_DEV, 64), d_sln_g.reshape(N_DEV, 64), d_sln_b.reshape(N_DEV, 64)], axis=1)
    small_g = jnp.pad(small_g, ((0, 0), (0, G_SMALL_ROWS * D - SMALL_SHARD))).reshape(N_DEV, G_SMALL_ROWS, D)
    tail_g, tok = _exchange_start([jnp.concatenate([small_g, rep_flat], axis=1)], True, "scatter_tail_start")

    names = list(w_in)
    g_out, delta, new_m, new_v = {}, {}, {}, {}
    column_sharded = ("a_w_in", "c_w_in", "ffn_w_gate", "ffn_w_up")

    def rows_of(a, nm):
        return jnp.swapaxes(a, 1, 2) if nm in column_sharded else a

    def reduce_adamw(nm, landing, layer, into=None):
        args = [rows_of(d[nm], nm) for d in (w_in, m_in, v_in)]
        if args[0].shape[0] == 1:
            args, layer = [a[0] for a in args], None
        return _reduce_adamw(landing, *args, "adamw_%s_%s" % (nm, layer), layer=layer, into=into)

    def keep(nm, res):
        res = [r if r.ndim == 3 else r[None] for r in res]
        g_out[nm], delta[nm], new_m[nm], new_v[nm] = (rows_of(r, nm) for r in res)

    ffn_names = ("ffn_w_gate", "ffn_w_up", "ffn_w_down")
    landed = _exchange_wait(ffn1_g, tok, "scatter_ffn1_wait")
    ffn_res = [reduce_adamw(nm, a, 1) for nm, a in zip(ffn_names, landed)]
    landed = _exchange_wait(mix1_g, ffn_res[-1][0], "scatter_mixer1_wait")
    for nm, a in zip(("c_w_in", "c_w_out"), landed):
        keep(nm, reduce_adamw(nm, a, 0))
    landed = _exchange_wait(ffn0_g, g_out["c_w_out"], "scatter_ffn0_wait")
    for nm, a, res in zip(ffn_names, landed, ffn_res):
        keep(nm, reduce_adamw(nm, a, 0, into=res))
    landed = _exchange_wait(mix0_g, g_out["ffn_w_down"], "scatter_mixer0_wait")
    for nm, a in zip(("a_w_in", "a_w_out"), landed):
        keep(nm, reduce_adamw(nm, a, 0))
    g_tail = _sum_slabs(_exchange_wait(tail_g, g_out["a_w_out"], "scatter_tail_wait")[0], "sum_tail")
    rep_all = _all_gather(g_tail[G_SMALL_ROWS:], "gather_replicated_grads")[0].reshape(N_DEV * REP_ROWS, D)
    loss = rep_all[sum(_rows_needed(s) for _, s in REP_2D), 0]
    res = _adamw_replicated(rep_all, *[[d[nm].reshape(s) for nm, s in REP_2D] for d in (w_in, m_in, v_in)], "adamw_replicated")
    for k, (nm, _) in enumerate(REP_2D):
        g_out[nm], delta[nm], new_m[nm], new_v[nm] = (r.reshape(w_in[nm].shape) for r in res[4 * k:4 * k + 4])
    small_r = g_tail[:G_SMALL_ROWS].reshape(-1)[:SMALL_SHARD]
    g_out.update(
        a_conv_w=small_r[:31 * 64].reshape(1, 31, 64), c_pool_scale=small_r[31 * 64:31 * 64 + 64].reshape(1, 64),
        c_sln_g=small_r[31 * 64 + 64:31 * 64 + 128].reshape(1, 64), c_sln_b=small_r[31 * 64 + 128:].reshape(1, 64))
    group = ("a_conv_w", "c_pool_scale", "c_sln_g", "c_sln_b")
    flat = [_pad_rows(jnp.concatenate([d[nm].reshape(-1) for nm in group]), G_SMALL_ROWS) for d in (w_in, g_out, m_in, v_in)]
    res = [r.reshape(-1) for r in _adamw(*flat, "adamw_small_sharded")]
    off = 0
    for nm in group:
        n = int(np.prod(w_in[nm].shape))
        delta[nm], new_m[nm], new_v[nm] = (r[off:off + n].reshape(w_in[nm].shape) for r in res)
        off += n

    grad_x = dx.reshape(bsz, seq, D)
    return (loss, grad_x, *[g_out[nm] for nm in names], *[delta[nm] for nm in names],
            *[new_m[nm] for nm in names], *[new_v[nm] for nm in names])
```

```python
import functools

import jax
import jax.numpy as jnp
import numpy as np
from jax import lax
from jax.experimental import pallas as pl
from jax.experimental.pallas import tpu as pltpu

F32 = jnp.float32
BF16 = jnp.bfloat16
MESH = pl.DeviceIdType.MESH

D = 1024
N_DEV = 8
EPS = 1e-5
HEAD_PAIRS = 4
ATT_BLK = 128
CONV_K = 31
HALO = 32
D_FF = 2816
IN0 = 1792
SGU_CHUNK = 128
GELU_C = 0.7978845608028654
GELU_A = 0.044715
ADAM_LR, ADAM_B1, ADAM_B2, ADAM_EPS, ADAM_WD, ADAM_STEP = 0.001, 0.9, 0.999, 1e-08, 0.01, 10
VMEM_LIMIT = 56 << 20

SMALL_SHARD = 31 * 64 + 3 * 64
W_MISC_ROWS = 16
G_MISC_ROWS = 32
G_SMALL_ROWS = 8
REP_ROWS = G_MISC_ROWS - G_SMALL_ROWS
REP_2D = (("c_w_pool", (64, 1024)), ("c_w_s", (64, 1024)), ("mix_norm", (2, 1024)), ("a_b_in", (1, 1792)), ("a_sinks", (1, 8)),
          ("a_conv_b", (1, 512)), ("a_cln_g", (1, 512)), ("a_cln_b", (1, 512)), ("c_b_s", (4, 128)), ("ffn_norm", (2, 1024)),
          ("final_norm", (1, 1024)))
Q_PERM = (0, 4, 1, 5, 2, 6, 3, 7)


def _params(*sem):
    return pltpu.CompilerParams(dimension_semantics=sem, vmem_limit_bytes=VMEM_LIMIT)


def _nn(a, b):
    return jnp.dot(a, b, preferred_element_type=F32)


def _nt(a, b):
    return lax.dot_general(a, b, (((1,), (1,)), ((), ())), preferred_element_type=F32)


def _tn(a, b):
    return lax.dot_general(a, b, (((0,), (0,)), ((), ())), preferred_element_type=F32)


def _tile(n, want=512):
    t = min(want, n)
    assert n % t == 0, (n, t)
    return t


def _seq_tile(s):
    return 512 if s >= 1024 else s // 2


def _rms(x, g):
    r = lax.rsqrt(jnp.mean(x * x, axis=-1, keepdims=True) + EPS)
    return x * r * g, r


def _rms_bwd(x, g, d_y):
    r = lax.rsqrt(jnp.mean(x * x, axis=-1, keepdims=True) + EPS)
    xr = x * r
    u = d_y * g
    d_x = r * (u - xr * jnp.mean(u * xr, axis=-1, keepdims=True))
    return d_x, jnp.sum(d_y * xr, axis=0, keepdims=True)


def _ln(y, g, b):
    mu = jnp.mean(y, axis=-1, keepdims=True)
    yc = y - mu
    rstd = lax.rsqrt(jnp.mean(yc * yc, axis=-1, keepdims=True) + EPS)
    xhat = yc * rstd
    return xhat * g + b, xhat, rstd


def _ln_bwd(d_o, xhat, rstd, g):
    dxh = d_o * g
    return rstd * (dxh - jnp.mean(dxh, axis=-1, keepdims=True) - xhat * jnp.mean(dxh * xhat, axis=-1, keepdims=True))


def _gelu(x):
    th = jnp.tanh(GELU_C * (x + GELU_A * x * x * x))
    return 0.5 * x * (1.0 + th), th


def _gelu_grad(x, th):
    return 0.5 * (1.0 + th) + 0.5 * x * (1.0 - th * th) * GELU_C * (1.0 + 3.0 * GELU_A * x * x)


def _row(c):
    return pl.BlockSpec((1, c), lambda *_: (0, 0))


def _full(shape):
    return pl.BlockSpec(shape, lambda *_: (0,) * len(shape))


def _norm_proj(h, g, wt, bias, name):
    t, n = h.shape[0], wt.shape[0]
    tm = _tile(t)
    has_bias = bias is not None

    def body(*refs):
        h_ref, g_ref, wt_ref = refs[:3]
        z_ref, hn_ref = refs[-2:]
        hn = _rms(h_ref[...].astype(F32), g_ref[...])[0].astype(BF16)
        hn_ref[...] = hn
        z = _nt(hn, wt_ref[...])
        if has_bias:
            z = z + refs[3][...]
        z_ref[...] = z.astype(BF16)

    in_specs = [pl.BlockSpec((tm, D), lambda i: (i, 0)), _row(D), _full((n, D))]
    args = [h, g, wt]
    if has_bias:
        in_specs.append(_row(n))
        args.append(bias)
    return pl.pallas_call(
        body, name=name, grid=(t // tm,), in_specs=in_specs,
        out_specs=[pl.BlockSpec((tm, n), lambda i: (i, 0)), pl.BlockSpec((tm, D), lambda i: (i, 0))],
        out_shape=[jax.ShapeDtypeStruct((t, n), BF16), jax.ShapeDtypeStruct((t, D), BF16)],
        compiler_params=_params("parallel"))(*args)


def _ff_pieces(tf, width=256):
    return [(c0, min(width, tf - c0)) for c0 in range(0, tf, width)]


FF_HALF = D_FF // 2


def _half_spec(part):
    return pl.BlockSpec((FF_HALF, D), lambda i: (part, 0))


def _ffn_half_fwd(hn, wtg_ref, wtu_ref, gate_ref, up_ref, act):
    for c0, cw in _ff_pieces(FF_HALF):
        cols = slice(c0, c0 + cw)
        gate = _nt(hn, wtg_ref[cols, :])
        up = _nt(hn, wtu_ref[cols, :])
        gate_ref[:, cols] = gate.astype(BF16)
        up_ref[:, cols] = up.astype(BF16)
        act[:, cols] = (gate * jax.nn.sigmoid(gate) * up).astype(BF16)


def _ffn_fwd_a(h_prev, a, b, w_out, g, wtg, wtu, wd, name):
    t = h_prev.shape[0]
    tm = _tile(t)

    def body(hp_ref, a_ref, b_ref, wa_ref, wb_ref, g_ref, wtg_ref, wtu_ref, wd_ref,
             h_ref, hn_ref, gate_ref, up_ref, part_ref, act):
        x = hp_ref[...].astype(F32) + _nn(a_ref[...], wa_ref[...]) + _nn(b_ref[...], wb_ref[...])
        h_ref[...] = x.astype(BF16)
        hn = _rms(x, g_ref[...])[0].astype(BF16)
        hn_ref[...] = hn
        _ffn_half_fwd(hn, wtg_ref, wtu_ref, gate_ref, up_ref, act)
        part_ref[...] = x + _nn(act[...], wd_ref[...])

    tok = pl.BlockSpec((tm, D), lambda i: (i, 0))
    half = pl.BlockSpec((tm, 512), lambda i: (i, 0))
    mid = pl.BlockSpec((tm, FF_HALF), lambda i: (i, 0))
    res = jax.ShapeDtypeStruct((t, D), BF16)
    mid_shape = jax.ShapeDtypeStruct((t, FF_HALF), BF16)
    return pl.pallas_call(
        body, name=name, grid=(t // tm,),
        in_specs=[tok, half, half, pl.BlockSpec((512, D), lambda i: (0, 0)), pl.BlockSpec((512, D), lambda i: (1, 0)),
                  _row(D), _half_spec(0), _half_spec(0), _half_spec(0)],
        out_specs=[tok, tok, mid, mid, tok], out_shape=[res, res, mid_shape, mid_shape, jax.ShapeDtypeStruct((t, D), F32)],
        scratch_shapes=[pltpu.VMEM((tm, FF_HALF), BF16)],
        compiler_params=_params("parallel"))(h_prev, a, b, w_out, w_out, g, wtg, wtu, wd)


def _ffn_fwd_b(hn, part, wtg, wtu, wd, name, head=None):
    t = hn.shape[0]
    tm = _tile(t)
    n_head = 0 if head is None else 2

    def body(*refs):
        hn_ref, part_ref, wtg_ref, wtu_ref, wd_ref = refs[:5]
        gate_ref, up_ref, o_ref = refs[5 + n_head:8 + n_head]
        act = refs[-1]
        _ffn_half_fwd(hn_ref[...], wtg_ref, wtu_ref, gate_ref, up_ref, act)
        x = part_ref[...] + _nn(act[...], wd_ref[...])
        if head is None:
            o_ref[...] = x.astype(BF16)
        else:
            fg_ref, t_ref = refs[5:7]
            dfg_ref, loss_ref = refs[10:12]

            @pl.when(pl.program_id(0) == 0)
            def _():
                dfg_ref[...] = jnp.zeros_like(dfg_ref)
                loss_ref[...] = jnp.zeros_like(loss_ref)

            gv = fg_ref[...]
            err = _rms(x, gv)[0] - t_ref[...]
            loss_ref[...] += 0.5 * jnp.sum(jnp.mean(err * err, axis=-1, keepdims=True), axis=0, keepdims=True)
            d_x, d_g = _rms_bwd(x, gv, err * (1.0 / D))
            o_ref[...] = d_x.astype(BF16)
            dfg_ref[...] += d_g

    tok = pl.BlockSpec((tm, D), lambda i: (i, 0))
    mid = pl.BlockSpec((tm, FF_HALF), lambda i: (i, 0))
    mid_shape = jax.ShapeDtypeStruct((t, FF_HALF), BF16)
    in_specs = [tok, tok, _half_spec(1), _half_spec(1), _half_spec(1)]
    out_specs = [mid, mid, tok]
    out_shape = [mid_shape, mid_shape, jax.ShapeDtypeStruct((t, D), BF16)]
    if head is not None:
        in_specs += [_row(D), tok]
        out_specs += [_row(D), _row(1)]
        out_shape += [jax.ShapeDtypeStruct((1, D), F32), jax.ShapeDtypeStruct((1, 1), F32)]
    return pl.pallas_call(
        body, name=name, grid=(t // tm,), in_specs=in_specs, out_specs=out_specs, out_shape=out_shape,
        scratch_shapes=[pltpu.VMEM((tm, FF_HALF), BF16)],
        compiler_params=_params("parallel" if head is None else "arbitrary"))(hn, part, wtg, wtu, wd, *(head or ()))


def _ffn_half_bwd(dh, gate_ref, up_ref, wd_ref, dgate_ref, dup_ref, act_ref):
    for c0, cw in _ff_pieces(FF_HALF):
        cols = slice(c0, c0 + cw)
        da = _nt(dh, wd_ref[cols, :])
        gt = gate_ref[:, cols].astype(F32)
        u = up_ref[:, cols].astype(F32)
        sg = jax.nn.sigmoid(gt)
        sil = gt * sg
        act_ref[:, cols] = (sil * u).astype(BF16)
        dup_ref[:, cols] = (da * sil).astype(BF16)
        dgate_ref[:, cols] = (da * u * sg * (1.0 + gt * (1.0 - sg))).astype(BF16)


def _ffn_bwd_a(dh, gate, up, wtg, wtu, wd, name):
    t = dh.shape[0]
    tm = _tile(t)

    def body(dh_ref, gate_ref, up_ref, wtg_ref, wtu_ref, wd_ref, dgate_ref, dup_ref, act_ref, part_ref):
        _ffn_half_bwd(dh_ref[...], gate_ref, up_ref, wd_ref, dgate_ref, dup_ref, act_ref)
        part_ref[...] = _nn(dgate_ref[...], wtg_ref[...]) + _nn(dup_ref[...], wtu_ref[...])

    tok = pl.BlockSpec((tm, D), lambda i: (i, 0))
    mid = pl.BlockSpec((tm, FF_HALF), lambda i: (i, 0))
    wide = jax.ShapeDtypeStruct((t, D_FF), BF16)
    return pl.pallas_call(
        body, name=name, grid=(t // tm,), in_specs=[tok, mid, mid, _half_spec(0), _half_spec(0), _half_spec(0)],
        out_specs=[mid, mid, mid, tok], out_shape=[wide, wide, wide, jax.ShapeDtypeStruct((t, D), F32)],
        compiler_params=_params("parallel"))(dh, gate, up, wtg, wtu, wd)


def _ffn_bwd_b(dh, h, g, gate, up, wtg, wtu, wd, w_out, part, wide, name):
    t = dh.shape[0]
    tm = _tile(t)

    def body(dh_ref, h_ref, g_ref, gate_ref, up_ref, wtg_ref, wtu_ref, wd_ref, wout_ref, part_ref, _a, _b, _c,
             dhin_ref, dmix_ref, dgate_ref, dup_ref, act_ref, dg_ref):
        @pl.when(pl.program_id(0) == 0)
        def _():
            dg_ref[...] = jnp.zeros_like(dg_ref)

        _ffn_half_bwd(dh_ref[...], gate_ref, up_ref, wd_ref, dgate_ref, dup_ref, act_ref)
        d_hn = part_ref[...] + _nn(dgate_ref[...], wtg_ref[...]) + _nn(dup_ref[...], wtu_ref[...])
        d_x, d_g = _rms_bwd(h_ref[...].astype(F32), g_ref[...], d_hn)
        dhin = (dh_ref[...].astype(F32) + d_x).astype(BF16)
        dhin_ref[...] = dhin
        dmix_ref[...] = _nt(dhin, wout_ref[...]).astype(BF16)
        dg_ref[...] += d_g

    tok = pl.BlockSpec((tm, D), lambda i: (i, 0))
    mid = pl.BlockSpec((tm, FF_HALF), lambda i: (i, 0))
    second = pl.BlockSpec((tm, FF_HALF), lambda i: (i, 1))
    wide_shape = jax.ShapeDtypeStruct((t, D_FF), BF16)
    res = jax.ShapeDtypeStruct((t, D), BF16)
    return pl.pallas_call(
        body, name=name, grid=(t // tm,),
        in_specs=[tok, tok, _row(D), mid, mid, _half_spec(1), _half_spec(1), _half_spec(1), _full((D, D)), tok, ANY, ANY, ANY],
        out_specs=[tok, tok, second, second, second, _row(D)],
        out_shape=[res, res, wide_shape, wide_shape, wide_shape, jax.ShapeDtypeStruct((1, D), F32)],
        input_output_aliases={10: 2, 11: 3, 12: 4},
        compiler_params=_params("arbitrary"))(dh, h, g, gate, up, wtg, wtu, wd, w_out, part, *wide)


def _proj_bwd_norm(pieces, wt, h, dh, g, dtype, name):
    t = h.shape[0]
    tm = _tile(t)
    n_p = len(pieces)

    def body(*refs):
        p_refs, w_refs = refs[:n_p], refs[n_p:2 * n_p]
        h_ref, dh_ref, g_ref, o_ref, dg_ref = refs[2 * n_p:]

        @pl.when(pl.program_id(0) == 0)
        def _():
            dg_ref[...] = jnp.zeros_like(dg_ref)

        d_hn = _nn(p_refs[0][...], w_refs[0][...])
        for p_ref, w_ref in zip(p_refs[1:], w_refs[1:]):
            d_hn = d_hn + _nn(p_ref[...], w_ref[...])
        d_x, d_g = _rms_bwd(h_ref[...].astype(F32), g_ref[...], d_hn)
        o_ref[...] = (dh_ref[...].astype(F32) + d_x).astype(dtype)
        dg_ref[...] += d_g

    tok = pl.BlockSpec((tm, D), lambda i: (i, 0))
    in_specs = [pl.BlockSpec((tm, a.shape[1]), lambda i: (i, 0)) for a, _ in pieces]
    for a, off in pieces:
        w = a.shape[1]
        assert off % w == 0
        in_specs.append(pl.BlockSpec((w, D), functools.partial(lambda i, blk: (blk, 0), blk=off // w)))
    in_specs += [tok, tok, _row(D)]
    return pl.pallas_call(
        body, name=name, grid=(t // tm,), in_specs=in_specs, out_specs=[tok, _row(D)],
        out_shape=[jax.ShapeDtypeStruct((t, D), dtype), jax.ShapeDtypeStruct((1, D), F32)],
        compiler_params=_params("arbitrary"))(*[a for a, _ in pieces], *([wt] * n_p), h, dh, g)


def _mm_tn(a, b, name):
    t, n = a.shape
    k = b.shape[1]
    tn = n if n <= 1024 else n // 2
    tt = _tile(t, 1024)
    nt = t // tt

    def body(a_ref, b_ref, o_ref, acc):
        s = pl.program_id(1)

        @pl.when(s == 0)
        def _():
            acc[...] = jnp.zeros_like(acc)

        acc[...] += _tn(a_ref[...], b_ref[...].astype(BF16))

        @pl.when(s == nt - 1)
        def _():
            o_ref[...] = acc[...].astype(BF16)

    return pl.pallas_call(
        body, name=name, grid=(n // tn, nt),
        in_specs=[pl.BlockSpec((tt, tn), lambda j, s: (s, j)), pl.BlockSpec((tt, k), lambda j, s: (s, 0))],
        out_specs=pl.BlockSpec((tn, k), lambda j, s: (j, 0)), out_shape=jax.ShapeDtypeStruct((n, k), BF16),
        scratch_shapes=[pltpu.VMEM((tn, k), F32)],
        compiler_params=_params("parallel", "arbitrary"))(a, b)


def _mm_tn_pieces(pieces, b, name):
    t, k = b.shape
    widths = [p.shape[1] for p in pieces]
    n, n_p = sum(widths), len(pieces)
    tt = _tile(t, 1024)
    nt = t // tt

    def body(*refs):
        b_ref, o_ref, acc = refs[n_p:]
        s = pl.program_id(0)

        @pl.when(s == 0)
        def _():
            acc[...] = jnp.zeros_like(acc)

        bb = b_ref[...].astype(BF16)
        off = 0
        for p_ref, w in zip(refs[:n_p], widths):
            acc[off:off + w, :] += _tn(p_ref[...], bb)
            off += w

        @pl.when(s == nt - 1)
        def _():
            o_ref[...] = acc[...].astype(BF16)

    return pl.pallas_call(
        body, name=name, grid=(nt,),
        in_specs=[pl.BlockSpec((tt, w), lambda s: (s, 0)) for w in widths] + [pl.BlockSpec((tt, k), lambda s: (s, 0))],
        out_specs=_full((n, k)), out_shape=jax.ShapeDtypeStruct((n, k), BF16),
        scratch_shapes=[pltpu.VMEM((n, k), F32)], compiler_params=_params("arbitrary"))(*pieces, b)


STACK = HEAD_PAIRS * ATT_BLK


def _attn_valid(first, rows):
    qi = lax.broadcasted_iota(jnp.int32, (rows, 2 * ATT_BLK), 0) % ATT_BLK
    r = lax.broadcasted_iota(jnp.int32, (rows, 2 * ATT_BLK), 1)
    dist = qi + ATT_BLK - r
    return (dist >= 0) & (dist < ATT_BLK) & ((r >= ATT_BLK) | jnp.logical_not(first))


def _stacked(ref, kh, scale):
    lo = lax.broadcasted_iota(jnp.int32, (ATT_BLK, 128), 1) < 64
    keep = lo if kh == 0 else ~lo
    parts = [jnp.where(keep, ref[:, g * 128:(g + 1) * 128] * scale, 0.0).astype(BF16) for g in range(HEAD_PAIRS)]
    return jnp.concatenate(parts, axis=0)


def _unstacked(a0, a1, g):
    lo = lax.broadcasted_iota(jnp.int32, (ATT_BLK, 128), 1) < 64
    rows = slice(g * ATT_BLK, (g + 1) * ATT_BLK)
    return jnp.where(lo, a0[rows], a1[rows])


def _sink_rows(s_ref, kh):
    return jnp.concatenate([jnp.full((ATT_BLK, 128), s_ref[0, kh * 4 + g], F32) for g in range(HEAD_PAIRS)], axis=0)


def _row_sums(a, split):
    hi = a.astype(BF16)
    ones = jnp.ones((2 * ATT_BLK, 128), BF16)
    if not split:
        return _nn(hi, ones)
    lo = (a - hi.astype(F32)).astype(BF16)
    return _nn(hi, ones) + _nn(lo, ones)


def _both(a):
    return jnp.concatenate([a, a], axis=1)


def _attn_probs(qs, kpair, sink, valid):
    s = jnp.where(valid, _nt(qs, kpair), -1e30)
    m = jnp.maximum(jnp.broadcast_to(jnp.max(s, axis=-1, keepdims=True), (s.shape[0], 128)), sink)
    p = jnp.exp(s - _both(m))
    es = jnp.exp(sink - m)
    inv = 1.0 / (_row_sums(p, split=True) + es)
    return p * _both(inv), es * inv


def _attn_probs_head(qm, kpair, sink, valid):
    s = jnp.where(valid, _nt(qm, kpair), -1e30)
    m = jnp.maximum(jnp.max(s, axis=-1, keepdims=True), sink)
    p = jnp.exp(s - m)
    return p * (1.0 / (jnp.sum(p, axis=-1, keepdims=True) + jnp.exp(sink - m)))


def _attn_specs(bsz, order):
    q = pl.BlockSpec((bsz, ATT_BLK, 512), lambda j: (0, order(j), 0))
    kvc = pl.BlockSpec((bsz, ATT_BLK, 256), lambda j: (0, order(j), 6))
    kvp = pl.BlockSpec((bsz, ATT_BLK, 256), lambda j: (0, jnp.maximum(order(j) - 1, 0), 6))
    return q, kvc, kvp


def _window_kv(kvc_ref, kvp_ref):
    kvc, kvp = kvc_ref[...], kvp_ref[...]
    kpair = jnp.concatenate([kvp[:, :128], kvc[:, :128]], axis=0)
    vpair = jnp.concatenate([kvp[:, 128:], kvc[:, 128:]], axis=0)
    return kpair, vpair


def _attn_fwd(z0, sinks, bsz, name):
    t = z0.shape[0]
    seq = t // bsz
    nb = seq // ATT_BLK

    def body(s_ref, q_ref, kvc_ref, kvp_ref, o_ref, token):
        token[...] = jnp.zeros_like(token)
        valid = _attn_valid(pl.program_id(0) == 0, ATT_BLK)
        lo = lax.broadcasted_iota(jnp.int32, (ATT_BLK, 128), 1) < 64
        for b in range(bsz):
            kpair, vpair = _window_kv(kvc_ref.at[b], kvp_ref.at[b])
            for g in range(HEAD_PAIRS):
                qs = q_ref[b, :, g * 128:(g + 1) * 128] * 0.125
                outs = []
                for kh in range(2):
                    qm = jnp.where(lo if kh == 0 else ~lo, qs, 0.0).astype(BF16)
                    p = _attn_probs_head(qm, kpair, s_ref[0, kh * 4 + g], valid)
                    outs.append(_nn(p.astype(BF16), vpair))
                o_ref[b, :, g * 128:(g + 1) * 128] = jnp.where(lo, outs[0], outs[1]).astype(BF16)

    q, kvc, kvp = _attn_specs(bsz, lambda j: j)
    z3 = z0.reshape(bsz, seq, z0.shape[1])
    out, token = pl.pallas_call(
        body, name=name, grid=(nb,),
        in_specs=[pl.BlockSpec(memory_space=pltpu.SMEM), q, kvc, kvp],
        out_specs=[pl.BlockSpec((bsz, ATT_BLK, 512), lambda j: (0, j, 0)), _full((8, 128))],
        out_shape=[jax.ShapeDtypeStruct((bsz, seq, 512), BF16), jax.ShapeDtypeStruct((8, 128), F32)],
        compiler_params=_params("arbitrary"))(sinks, z3, z3, z3)
    return out.reshape(t, 512), token


def _attn_bwd(z0, dmix, sinks, bsz, name):
    t = z0.shape[0]
    seq = t // bsz
    nb = seq // ATT_BLK

    def body(s_ref, q_ref, kvc_ref, kvp_ref, do_ref, dq_ref, dkv_ref, dsink_ref, dbq_ref, dbkv_ref, carry):
        j = pl.program_id(0)

        @pl.when(j == 0)
        def _():
            carry[...] = jnp.zeros_like(carry)
            dsink_ref[...] = jnp.zeros_like(dsink_ref)
            dbq_ref[...] = jnp.zeros_like(dbq_ref)
            dbkv_ref[...] = jnp.zeros_like(dbkv_ref)

        valid = _attn_valid(j == nb - 1, STACK)
        lane = lax.broadcasted_iota(jnp.int32, (1, 128), 1)
        dsink = jnp.zeros((1, 128), F32)
        dbq = [jnp.zeros((1, 128), F32)] * HEAD_PAIRS
        dbkv = jnp.zeros((1, 256), F32)
        for b in range(bsz):
            kpair, vpair = _window_kv(kvc_ref.at[b], kvp_ref.at[b])
            dk = jnp.zeros((2 * ATT_BLK, 128), F32)
            dv = jnp.zeros((2 * ATT_BLK, 128), F32)
            dqs = []
            for kh in range(2):
                qs = _stacked(q_ref.at[b], kh, 0.125)
                dos = _stacked(do_ref.at[b], kh, 1.0)
                p, ps = _attn_probs(qs, kpair, _sink_rows(s_ref, kh), valid)
                dp = _nt(dos, vpair)
                delta = _row_sums(p * dp, split=False)
                ds = (p * (dp - _both(delta))).astype(BF16)
                dqs.append(_nn(ds, kpair))
                dk = dk + _tn(ds, qs)
                dv = dv + _tn(p.astype(BF16), dos)
                psd = ps * delta
                for g in range(HEAD_PAIRS):
                    part = jnp.sum(psd[g * ATT_BLK:(g + 1) * ATT_BLK], axis=0, keepdims=True)
                    dsink = dsink - jnp.where(lane == kh * 4 + g, part, 0.0)
            for g in range(HEAD_PAIRS):
                dq = _unstacked(dqs[0], dqs[1], g) * 0.125
                dq_ref[b, :, g * 128:(g + 1) * 128] = dq.astype(BF16)
                dbq[g] = dbq[g] + jnp.sum(dq, axis=0, keepdims=True)
            dkv = jnp.concatenate([dk[ATT_BLK:], dv[ATT_BLK:]], axis=1) + carry[b]
            dkv_ref[b] = dkv.astype(BF16)
            dbkv = dbkv + jnp.sum(dkv, axis=0, keepdims=True)
            carry[b] = jnp.concatenate([dk[:ATT_BLK], dv[:ATT_BLK]], axis=1)
        dsink_ref[...] += dsink
        dbq_ref[...] += jnp.concatenate(dbq, axis=1)
        dbkv_ref[...] += dbkv

    q, kvc, kvp = _attn_specs(bsz, lambda j: nb - 1 - j)
    z3 = z0.reshape(bsz, seq, z0.shape[1])
    d3 = dmix.reshape(bsz, seq, dmix.shape[1])
    dq, dkv, dsink, dbq, dbkv = pl.pallas_call(
        body, name=name, grid=(nb,),
        in_specs=[pl.BlockSpec(memory_space=pltpu.SMEM), q, kvc, kvp,
                  pl.BlockSpec((bsz, ATT_BLK, 512), lambda j: (0, nb - 1 - j, 0))],
        out_specs=[pl.BlockSpec((bsz, ATT_BLK, 512), lambda j: (0, nb - 1 - j, 0)),
                   pl.BlockSpec((bsz, ATT_BLK, 256), lambda j: (0, nb - 1 - j, 0)), _row(128), _row(512), _row(256)],
        out_shape=[jax.ShapeDtypeStruct((bsz, seq, 512), BF16), jax.ShapeDtypeStruct((bsz, seq, 256), BF16),
                   jax.ShapeDtypeStruct((1, 128), F32), jax.ShapeDtypeStruct((1, 512), F32),
                   jax.ShapeDtypeStruct((1, 256), F32)],
        scratch_shapes=[pltpu.VMEM((bsz, ATT_BLK, 256), F32)],
        compiler_params=_params("arbitrary"))(sinks, z3, z3, z3, d3)
    return dq.reshape(t, 512), dkv.reshape(t, 256), dsink, dbq, dbkv


def _seq_specs(ts, nt, t, width, col):
    per = ts // HALO
    cur = pl.BlockSpec((ts, width), lambda b, i: (b * nt + i, col))
    prev = pl.BlockSpec((HALO, width), lambda b, i: (jnp.maximum((b * nt + i) * per - 1, 0), col))
    nxt = pl.BlockSpec((HALO, width), lambda b, i: (jnp.minimum((b * nt + i + 1) * per, t // HALO - 1), col))
    return prev, cur, nxt


SUB = 8
CONV_ROWS = 32


def _shifted_copies(src, sh, rows_first, rows_rest):
    for r in range(SUB):
        rows = rows_first if r == 0 else rows_rest
        sh[r, pl.ds(0, rows), :] = src[pl.ds(r, rows), :]


def _tap_sum(sh, w, offset, c0, rows):
    acc = None
    for k in range(CONV_K):
        o = offset(k)
        term = sh[o % SUB, pl.ds(c0 + o - o % SUB, rows), :] * w[k:k + 1, :]
        acc = term if acc is None else acc + term
    return acc


def _glu_rows(a_ref, g_ref, rows=slice(None)):
    return a_ref[rows, :].astype(F32) * jax.nn.sigmoid(g_ref[rows, :].astype(F32))


def _conv_fwd(z0, conv_w, conv_b, ln_g, ln_b, bsz, name):
    t = z0.shape[0]
    s = t // bsz
    ts = _seq_tile(s)
    nt = s // ts
    first = HALO - (CONV_K - 1)

    def body(ap_ref, ac_ref, gp_ref, gc_ref, w_ref, cb_ref, lg_ref, lb_ref, o_ref, y_ref, hbuf, sh):
        hbuf[0:HALO, :] = jnp.where(pl.program_id(1) > 0, _glu_rows(ap_ref, gp_ref), 0.0)
        hbuf[HALO:HALO + ts, :] = _glu_rows(ac_ref, gc_ref)
        _shifted_copies(hbuf, sh, ts + HALO, ts + HALO - SUB)
        w, cb, lg, lb = w_ref[...], cb_ref[...], lg_ref[...], lb_ref[...]
        for c0 in range(0, ts, CONV_ROWS):
            y = _tap_sum(sh, w, lambda k: first + k, c0, CONV_ROWS) + cb
            y_ref[c0:c0 + CONV_ROWS, :] = y
            o = _ln(y, lg, lb)[0]
            o_ref[c0:c0 + CONV_ROWS, :] = (o * jax.nn.sigmoid(o)).astype(BF16)

    ap, ac, _ = _seq_specs(ts, nt, t, 512, 1)
    gp, gc, _ = _seq_specs(ts, nt, t, 512, 2)
    tile = pl.BlockSpec((ts, 512), lambda b, i: (b * nt + i, 0))
    return pl.pallas_call(
        body, name=name, grid=(bsz, nt),
        in_specs=[ap, ac, gp, gc, _full((HALO, 512)), _row(512), _row(512), _row(512)],
        out_specs=[tile, tile],
        out_shape=[jax.ShapeDtypeStruct((t, 512), BF16), jax.ShapeDtypeStruct((t, 512), F32)],
        scratch_shapes=[pltpu.VMEM((HALO + ts, 512), F32), pltpu.VMEM((SUB, HALO + ts, 512), F32)],
        compiler_params=_params("parallel", "parallel"))(z0, z0, z0, z0, conv_w, conv_b, ln_g, ln_b)


def _conv_bwd(z0, y, dmix, conv_w, ln_g, ln_b, bsz, name):
    t = z0.shape[0]
    s = t // bsz
    ts = _seq_tile(s)
    nt = s // ts

    def body(ac_ref, gc_ref, yc_ref, yn_ref, dc_ref, dn_ref, w_ref, lg_ref, lb_ref,
             da_ref, dg_ref, dw_ref, dcb_ref, dlg_ref, dlb_ref, dba_ref, dbg_ref, hcur, dybuf, sh_dy):
        b, i = pl.program_id(0), pl.program_id(1)

        @pl.when((b == 0) & (i == 0))
        def _():
            for ref in (dw_ref, dcb_ref, dlg_ref, dlb_ref, dba_ref, dbg_ref):
                ref[...] = jnp.zeros_like(ref)

        w, lg, lb = w_ref[...], lg_ref[...], lb_ref[...]
        hcur[...] = _glu_rows(ac_ref, gc_ref)

        def d_conv_out(yv, dout):
            o, xhat, rstd = _ln(yv, lg, lb)
            sg_o = jax.nn.sigmoid(o)
            d_o = dout * sg_o * (1.0 + o * (1.0 - sg_o))
            return _ln_bwd(d_o, xhat, rstd, lg), d_o * xhat, d_o

        dlg = jnp.zeros((1, 512), F32)
        dlb = jnp.zeros((1, 512), F32)
        dcb = jnp.zeros((1, 512), F32)
        for c0 in range(0, ts, CONV_ROWS):
            rows = slice(c0, c0 + CONV_ROWS)
            dy, g_part, b_part = d_conv_out(yc_ref[rows, :], dc_ref[rows, :].astype(F32))
            dybuf[rows, :] = dy
            dlg = dlg + jnp.sum(g_part, axis=0, keepdims=True)
            dlb = dlb + jnp.sum(b_part, axis=0, keepdims=True)
            dcb = dcb + jnp.sum(dy, axis=0, keepdims=True)
        dn = jnp.where(i < nt - 1, dn_ref[...].astype(F32), 0.0)
        dybuf[ts:ts + HALO, :] = d_conv_out(yn_ref[...], dn)[0]
        dlg_ref[...] += dlg
        dlb_ref[...] += dlb
        dcb_ref[...] += dcb
        _shifted_copies(dybuf, sh_dy, ts + HALO - SUB, ts + HALO - SUB)

        for k in range(CONV_K):
            o = CONV_K - 1 - k
            prod = hcur[...] * sh_dy[o % SUB, pl.ds(o - o % SUB, ts), :]
            dw_ref[pl.ds(k, 1), :] += jnp.sum(prod, axis=0, keepdims=True)
        dba = jnp.zeros((1, 512), F32)
        dbg = jnp.zeros((1, 512), F32)
        for c0 in range(0, ts, CONV_ROWS):
            rows = slice(c0, c0 + CONV_ROWS)
            dh = _tap_sum(sh_dy, w, lambda k: CONV_K - 1 - k, c0, CONV_ROWS)
            a_c = ac_ref[rows, :].astype(F32)
            sg_c = jax.nn.sigmoid(gc_ref[rows, :].astype(F32))
            d_a = dh * sg_c
            d_g = dh * a_c * sg_c * (1.0 - sg_c)
            da_ref[rows, :] = d_a.astype(BF16)
            dg_ref[rows, :] = d_g.astype(BF16)
            dba = dba + jnp.sum(d_a, axis=0, keepdims=True)
            dbg = dbg + jnp.sum(d_g, axis=0, keepdims=True)
        dba_ref[...] += dba
        dbg_ref[...] += dbg

    _, ac, _ = _seq_specs(ts, nt, t, 512, 1)
    _, gc, _ = _seq_specs(ts, nt, t, 512, 2)
    _, yc, yn = _seq_specs(ts, nt, t, 512, 0)
    _, dc, dn = _seq_specs(ts, nt, t, 512, 1)
    tile = pl.BlockSpec((ts, 512), lambda b, i: (b * nt + i, 0))
    vec = jax.ShapeDtypeStruct((1, 512), F32)
    return pl.pallas_call(
        body, name=name, grid=(bsz, nt),
        in_specs=[ac, gc, yc, yn, dc, dn, _full((HALO, 512)), _row(512), _row(512)],
        out_specs=[tile, tile, _full((HALO, 512)), _row(512), _row(512), _row(512), _row(512), _row(512)],
        out_shape=[jax.ShapeDtypeStruct((t, 512), BF16), jax.ShapeDtypeStruct((t, 512), BF16),
                   jax.ShapeDtypeStruct((HALO, 512), F32), vec, vec, vec, vec, vec],
        scratch_shapes=[pltpu.VMEM((ts, 512), F32), pltpu.VMEM((ts + HALO, 512), F32),
                        pltpu.VMEM((SUB, HALO + ts, 512), F32)],
        compiler_params=_params("arbitrary", "arbitrary"))(z0, z0, y, y, dmix, dmix, conv_w, ln_g, ln_b)


def _pooled(pbuf, g, ts, tok):
    w = 2 << g
    cols = slice(128 * g, 128 * (g + 1))
    sm = pbuf[pl.ds(HALO, ts), cols]
    for d in range(1, w):
        sm = sm + pbuf[pl.ds(HALO - d, ts), cols]
    cnt = jnp.minimum(tok + 1, w).astype(F32)
    return sm / cnt - pbuf[pl.ds(HALO, ts), cols]


def _pool_fwd(z1, w_pool, scale, bsz, name):
    t = z1.shape[0]
    s = t // bsz
    ts = _seq_tile(s)
    nt = s // ts

    def body(zp_ref, zc_ref, wp_ref, sc_ref, o_ref, token, pbuf):
        token[...] = jnp.zeros_like(token)
        i = pl.program_id(1)
        pbuf[0:HALO, :] = jnp.where(i > 0, zp_ref[...].astype(F32), 0.0)
        pbuf[HALO:HALO + ts, :] = zc_ref[...].astype(F32)
        tok = i * ts + lax.broadcasted_iota(jnp.int32, (ts, 1), 0)
        for g in range(4):
            cols = slice(128 * g, 128 * (g + 1))
            pooled = _pooled(pbuf, g, ts, tok).astype(BF16)
            o_ref[:, cols] = (_nn(pooled, wp_ref[g].astype(BF16)) * sc_ref[:, cols]).astype(BF16)

    zp, zc, _ = _seq_specs(ts, nt, t, 512, 0)
    return pl.pallas_call(
        body, name=name, grid=(bsz, nt), in_specs=[zp, zc, _full((4, 128, 128)), _row(512)],
        out_specs=[pl.BlockSpec((ts, 512), lambda b, i: (b * nt + i, 0)), _full((8, 128))],
        out_shape=[jax.ShapeDtypeStruct((t, 512), BF16), jax.ShapeDtypeStruct((8, 128), F32)],
        scratch_shapes=[pltpu.VMEM((HALO + ts, 512), F32)],
        compiler_params=_params("arbitrary", "arbitrary"))(z1, z1, w_pool, scale)


def _pool_bwd(z1, dmix, w_pool, scale, bsz, name):
    t = z1.shape[0]
    s = t // bsz
    ts = _seq_tile(s)
    nt = s // ts
    rr = ts + HALO

    def body(zp_ref, zc_ref, dc_ref, dn_ref, wp_ref, sc_ref, dz_ref, dwp_ref, dsc_ref, pbuf, ebuf):
        b, i = pl.program_id(0), pl.program_id(1)

        @pl.when((b == 0) & (i == 0))
        def _():
            dwp_ref[...] = jnp.zeros_like(dwp_ref)
            dsc_ref[...] = jnp.zeros_like(dsc_ref)

        pbuf[0:HALO, :] = jnp.where(i > 0, zp_ref[...].astype(F32), 0.0)
        pbuf[HALO:HALO + ts, :] = zc_ref[...].astype(F32)
        dn = jnp.where(i < nt - 1, dn_ref[...].astype(F32), 0.0)
        dout = jnp.concatenate([dc_ref[...].astype(F32), dn], axis=0)
        tok = i * ts + lax.broadcasted_iota(jnp.int32, (ts, 1), 0)
        tok_r = i * ts + lax.broadcasted_iota(jnp.int32, (rr, 1), 0)
        for g in range(4):
            w = 2 << g
            cols = slice(128 * g, 128 * (g + 1))
            wg = wp_ref[g].astype(BF16)
            pooled = _pooled(pbuf, g, ts, tok).astype(BF16)
            dsc_ref[:, cols] += jnp.sum(dout[:ts, cols] * _nn(pooled, wg), axis=0, keepdims=True)
            dy = (dout[:, cols] * sc_ref[:, cols]).astype(BF16)
            dwp_ref[g] += _tn(pooled, dy[:ts])
            dpl = _nt(dy, wg)
            ebuf[...] = dpl / jnp.minimum(tok_r + 1, w).astype(F32)
            dz = ebuf[pl.ds(0, ts), :] - dpl[:ts]
            for d in range(1, w):
                dz = dz + ebuf[pl.ds(d, ts), :]
            dz_ref[:, cols] = dz.astype(BF16)

    zp, zc, _ = _seq_specs(ts, nt, t, 512, 0)
    _, dc, dn = _seq_specs(ts, nt, t, 512, 0)
    return pl.pallas_call(
        body, name=name, grid=(bsz, nt), in_specs=[zp, zc, dc, dn, _full((4, 128, 128)), _row(512)],
        out_specs=[pl.BlockSpec((ts, 512), lambda b, i: (b * nt + i, 0)), _full((4, 128, 128)), _row(512)],
        out_shape=[jax.ShapeDtypeStruct((t, 512), BF16), jax.ShapeDtypeStruct((4, 128, 128), F32),
                   jax.ShapeDtypeStruct((1, 512), F32)],
        scratch_shapes=[pltpu.VMEM((HALO + ts, 512), F32), pltpu.VMEM((rr, 128), F32)],
        compiler_params=_params("arbitrary", "arbitrary"))(z1, z1, dmix, dmix, w_pool, scale)


def _tril():
    r = lax.broadcasted_iota(jnp.int32, (SGU_CHUNK, SGU_CHUNK), 0)
    c = lax.broadcasted_iota(jnp.int32, (SGU_CHUNK, SGU_CHUNK), 1)
    return r >= c


def _sgu_fwd(z1, ln_g, ln_b, w_s, b_rows, name):
    t = z1.shape[0]
    ts = _tile(t)

    def body(zu_ref, zv_ref, lg_ref, lb_ref, ws_ref, bs_ref, o_ref):
        v = _gelu(zv_ref[...].astype(F32))[0]
        vb = _ln(v, lg_ref[...], lb_ref[...])[0].astype(BF16)
        tril = _tril()
        for g in range(4):
            cols = slice(128 * g, 128 * (g + 1))
            wg = jnp.where(tril, ws_ref[g], 0.0).astype(BF16)
            for c in range(ts // SGU_CHUNK):
                rows = slice(SGU_CHUNK * c, SGU_CHUNK * (c + 1))
                mixed = _nn(wg, vb[rows, cols]) + bs_ref[g]
                o_ref[rows, cols] = (_gelu(zu_ref[rows, cols].astype(F32))[0] * mixed).astype(BF16)

    return pl.pallas_call(
        body, name=name, grid=(t // ts,),
        in_specs=[pl.BlockSpec((ts, 512), lambda i: (i, 1)), pl.BlockSpec((ts, 512), lambda i: (i, 2)),
                  _row(512), _row(512), _full((4, 128, 128)), _full((4, 128, 128))],
        out_specs=pl.BlockSpec((ts, 512), lambda i: (i, 0)), out_shape=jax.ShapeDtypeStruct((t, 512), BF16),
        compiler_params=_params("parallel"))(z1, z1, ln_g, ln_b, w_s, b_rows)


def _sgu_bwd(z1, dmix, ln_g, ln_b, w_s, b_rows, name):
    t = z1.shape[0]
    ts = _tile(t)

    def body(zu_ref, zv_ref, d_ref, lg_ref, lb_ref, ws_ref, bs_ref,
             dzu_ref, dzv_ref, dws_ref, dbs_ref, dlg_ref, dlb_ref, dvbuf):
        @pl.when(pl.program_id(0) == 0)
        def _():
            for ref in (dws_ref, dbs_ref, dlg_ref, dlb_ref):
                ref[...] = jnp.zeros_like(ref)

        zv = zv_ref[...].astype(F32)
        v, thv = _gelu(zv)
        lg = lg_ref[...]
        vln, xhat, rstd = _ln(v, lg, lb_ref[...])
        vb = vln.astype(BF16)
        tril = _tril()
        for g in range(4):
            cols = slice(128 * g, 128 * (g + 1))
            wg = jnp.where(tril, ws_ref[g], 0.0).astype(BF16)
            dws = jnp.zeros((SGU_CHUNK, SGU_CHUNK), F32)
            dbs = jnp.zeros((1, SGU_CHUNK), F32)
            for c in range(ts // SGU_CHUNK):
                rows = slice(SGU_CHUNK * c, SGU_CHUNK * (c + 1))
                vbc = vb[rows, cols]
                mixed = _nn(wg, vbc) + bs_ref[g]
                zu = zu_ref[rows, cols].astype(F32)
                u, thu = _gelu(zu)
                dout = d_ref[rows, cols].astype(F32)
                dzu_ref[rows, cols] = (dout * mixed * _gelu_grad(zu, thu)).astype(BF16)
                dm = dout * u
                dmb = dm.astype(BF16)
                dws = dws + _nt(dmb, vbc)
                dbs = dbs + jnp.sum(dm.T, axis=0, keepdims=True)
                dvbuf[rows, cols] = _tn(wg, dmb)
            dws_ref[g] += jnp.where(tril, dws, 0.0)
            dbs_ref[pl.ds(g, 1), :] += dbs
        dvln = dvbuf[...]
        dlg_ref[...] += jnp.sum(dvln * xhat, axis=0, keepdims=True)
        dlb_ref[...] += jnp.sum(dvln, axis=0, keepdims=True)
        dzv_ref[...] = (_ln_bwd(dvln, xhat, rstd, lg) * _gelu_grad(zv, thv)).astype(BF16)

    tile = pl.BlockSpec((ts, 512), lambda i: (i, 0))
    vec = jax.ShapeDtypeStruct((1, 512), F32)
    return pl.pallas_call(
        body, name=name, grid=(t // ts,),
        in_specs=[pl.BlockSpec((ts, 512), lambda i: (i, 1)), pl.BlockSpec((ts, 512), lambda i: (i, 2)),
                  pl.BlockSpec((ts, 512), lambda i: (i, 1)), _row(512), _row(512), _full((4, 128, 128)),
                  _full((4, 128, 128))],
        out_specs=[tile, tile, _full((4, 128, 128)), _full((4, 128)), _row(512), _row(512)],
        out_shape=[jax.ShapeDtypeStruct((t, 512), BF16), jax.ShapeDtypeStruct((t, 512), BF16),
                   jax.ShapeDtypeStruct((4, 128, 128), F32), jax.ShapeDtypeStruct((4, 128), F32), vec, vec],
        scratch_shapes=[pltpu.VMEM((ts, 512), F32)],
        compiler_params=_params("arbitrary"))(z1, z1, dmix, ln_g, ln_b, w_s, b_rows)


def _row_tile(r):
    for cand in (512, 352, 256, 192, 128, 64, 32, 16, 8):
        if r % cand == 0:
            return cand
    return r


def _sum_slabs(a, name):
    k, r, c = a.shape
    tr = _row_tile(r)

    def body(*refs):
        acc = refs[0][...].astype(F32)
        for ref in refs[1:-1]:
            acc = acc + ref[...].astype(F32)
        refs[-1][...] = acc

    in_specs = [pl.BlockSpec((None, tr, c), functools.partial(lambda i, s: (s, i, 0), s=s)) for s in range(k)]
    return pl.pallas_call(
        body, name=name, grid=(r // tr,), in_specs=in_specs, out_specs=pl.BlockSpec((tr, c), lambda i: (i, 0)),
        out_shape=jax.ShapeDtypeStruct((r, c), F32), compiler_params=_params("parallel"))(*([a] * k))


def _adamw_math(w, g, m, v):
    mn = ADAM_B1 * m + (1.0 - ADAM_B1) * g
    vn = ADAM_B2 * v + (1.0 - ADAM_B2) * (g * g)
    m_hat = mn / (1.0 - ADAM_B1 ** ADAM_STEP)
    v_hat = vn / (1.0 - ADAM_B2 ** ADAM_STEP)
    return -ADAM_LR * (m_hat / (jnp.sqrt(v_hat) + ADAM_EPS) + ADAM_WD * w), mn, vn


def _reduce_adamw(landing, w, m, v, name, layer=None, into=None):
    k, r, c = landing.shape
    tr = _row_tile(r)
    n_into = 0 if into is None else 4

    def body(*refs):
        slabs, (w_ref, m_ref, v_ref) = refs[:k], refs[k:k + 3]
        g_ref, d_ref, mo_ref, vo_ref = refs[k + 3 + n_into:]
        g = slabs[0][...].astype(F32)
        for ref in slabs[1:]:
            g = g + ref[...].astype(F32)
        g_ref[...] = g
        d_ref[...], mo_ref[...], vo_ref[...] = _adamw_math(w_ref[...], g, m_ref[...], v_ref[...])

    if layer is None:
        spec = pl.BlockSpec((tr, c), lambda i: (i, 0))
    else:
        spec = pl.BlockSpec((None, tr, c), lambda i: (layer, i, 0))
    in_specs = [pl.BlockSpec((None, tr, c), functools.partial(lambda i, s: (s, i, 0), s=s)) for s in range(k)]
    in_specs += [spec] * 3 + [ANY] * n_into
    shape = jax.ShapeDtypeStruct(w.shape, F32)
    return pl.pallas_call(
        body, name=name, grid=(r // tr,), in_specs=in_specs, out_specs=[spec] * 4, out_shape=[shape] * 4,
        input_output_aliases={k + 3 + j: j for j in range(n_into)},
        compiler_params=_params("parallel"))(*([landing] * k), w, m, v, *(into or ()))


def _rows_needed(shape):
    return shape[0] * -(-shape[1] // D)


def _as_rows(a):
    r, c = a.shape
    n = -(-c // D)
    assert r == 1 or n == 1
    return jnp.pad(a, ((0, 0), (0, n * D - c))).reshape(r * n, D)


def _adamw_replicated(g_rows, w, m, v, name):
    n = len(REP_2D)

    def body(*refs):
        g_ref, w_refs, m_refs, v_refs, outs = refs[0], refs[1:1 + n], refs[1 + n:1 + 2 * n], refs[1 + 2 * n:1 + 3 * n], refs[1 + 3 * n:]
        r0 = 0
        for k, (_, (r, c)) in enumerate(REP_2D):
            pieces = [g_ref[r0 + j * r:r0 + j * r + r, 0:min(D, c - j * D)] for j in range(-(-c // D))]
            g = pieces[0] if len(pieces) == 1 else jnp.concatenate(pieces, axis=1)
            outs[4 * k][...] = g
            outs[4 * k + 1][...], outs[4 * k + 2][...], outs[4 * k + 3][...] = _adamw_math(
                w_refs[k][...], g, m_refs[k][...], v_refs[k][...])
            r0 += _rows_needed((r, c))

    shapes = [s for _, s in REP_2D]
    return pl.pallas_call(
        body, name=name, in_specs=[_full(g_rows.shape)] + [_full(s) for s in shapes] * 3,
        out_specs=[_full(s) for s in shapes for _ in range(4)],
        out_shape=[jax.ShapeDtypeStruct(s, F32) for s in shapes for _ in range(4)],
        grid=(1,), compiler_params=_params("arbitrary"))(g_rows, *w, *m, *v)


def _adamw(w, g, m, v, name):
    r, c = w.shape
    tr = _row_tile(r)

    def body(w_ref, g_ref, m_ref, v_ref, d_ref, mo_ref, vo_ref):
        d_ref[...], mo_ref[...], vo_ref[...] = _adamw_math(w_ref[...], g_ref[...], m_ref[...], v_ref[...])

    spec = pl.BlockSpec((tr, c), lambda i: (i, 0))
    shape = jax.ShapeDtypeStruct((r, c), F32)
    return pl.pallas_call(
        body, name=name, grid=(r // tr,), in_specs=[spec] * 4, out_specs=[spec] * 3, out_shape=[shape] * 3,
        compiler_params=_params("parallel"))(w, g, m, v)


ANY = pl.BlockSpec(memory_space=pl.ANY)


def _all_gather(block, name):
    r, c_dim = block.shape

    def body(x_ref, out_ref, token, send_sems, recv_sems, local_sem):
        token[...] = jnp.zeros_like(token)
        x, y, c = lax.axis_index("x"), lax.axis_index("y"), lax.axis_index("c")
        me, sibling = (x, y, c), (x, y, 1 - c)
        chips = [(1 - x, y), (x, 1 - y), (1 - x, 1 - y)]

        def rows(px, py, pc):
            return out_ref.at[4 * px + 2 * py + pc]

        def copy(k, blk, to, src=None):
            return pltpu.make_async_remote_copy(
                src_ref=rows(*blk) if src is None else src, dst_ref=rows(*blk), send_sem=send_sems.at[k],
                recv_sem=recv_sems.at[k], device_id=to, device_id_type=MESH)

        mine = pltpu.make_async_copy(x_ref, rows(*me), local_sem)
        mine.start()
        first = [copy(0, me, sibling, src=x_ref)]
        first += [copy(1 + j, me, (*chip, c), src=x_ref) for j, chip in enumerate(chips)]
        for cp in first:
            cp.start()
        passed = [copy(4 + j, (*chip, c), sibling) for j, chip in enumerate(chips)]
        for j, chip in enumerate(chips):
            copy(1 + j, (*chip, c), me).wait_recv()
            passed[j].start()
        copy(0, sibling, me).wait_recv()
        for j, chip in enumerate(chips):
            copy(4 + j, (*chip, 1 - c), me).wait_recv()
        for cp in first + passed:
            cp.wait_send()
        mine.wait()

    return pl.pallas_call(
        body, name=name, in_specs=[ANY], out_specs=[ANY, pl.BlockSpec(memory_space=pltpu.VMEM)],
        out_shape=[jax.ShapeDtypeStruct((N_DEV, r, c_dim), block.dtype), jax.ShapeDtypeStruct((8, 128), F32)],
        scratch_shapes=[pltpu.SemaphoreType.DMA((7,)), pltpu.SemaphoreType.DMA((7,)), pltpu.SemaphoreType.DMA],
    )(block)


HBM = pl.BlockSpec(memory_space=pltpu.HBM)
SEM = pl.BlockSpec(memory_space=pltpu.SEMAPHORE)
EFFECT = pltpu.SideEffectType.DATAFLOW_SIDE_EFFECTING


def _exchange_copies(scatter, src_refs, land_refs, send_sems, recv_sems, local_sems):
    x, y, c = lax.axis_index("x"), lax.axis_index("y"), lax.axis_index("c")
    me = 4 * x + 2 * y + c
    sends, arrivals, locals_ = [], [], []
    for a, (src, land) in enumerate(zip(src_refs, land_refs)):
        def pick(idx, src=src):
            return src.at[idx] if scatter else src

        locals_.append(pltpu.make_async_copy(pick(me), land.at[me], local_sems.at[a]))
        for r in range(1, N_DEV):
            px = 1 - x if r & 4 else x
            py = 1 - y if r & 2 else y
            pc = 1 - c if r & 1 else c
            peer, s = 4 * px + 2 * py + pc, 7 * a + r - 1
            sends.append(pltpu.make_async_remote_copy(
                src_ref=pick(peer), dst_ref=land.at[me], send_sem=send_sems.at[s], recv_sem=recv_sems.at[s],
                device_id=(px, py, pc), device_id_type=MESH))
            arrivals.append(pltpu.make_async_remote_copy(
                src_ref=pick(peer), dst_ref=land.at[peer], send_sem=send_sems.at[s], recv_sem=recv_sems.at[s],
                device_id=(px, py, pc), device_id_type=MESH))
    return sends, arrivals, locals_


def _exchange_start(srcs, scatter, name):
    n = len(srcs)
    lands = [lax.empty((N_DEV,) + s.shape[-2:], s.dtype) for s in srcs]

    def body(*refs):
        src_refs, land_refs = refs[:n], refs[n:2 * n]
        send_sems, recv_sems, local_sems = refs[2 * n:2 * n + 3]
        token = refs[-1]
        sends, _, locals_ = _exchange_copies(scatter, src_refs, land_refs, send_sems, recv_sems, local_sems)
        for cp in locals_ + sends:
            cp.start()
        token[...] = jnp.zeros_like(token)

    res = pl.pallas_call(
        body, name=name,
        out_shape=[pltpu.SemaphoreType.DMA((7 * n,)), pltpu.SemaphoreType.DMA((7 * n,)), pltpu.SemaphoreType.DMA((n,))]
        + [pltpu.HBM(a.shape, a.dtype) for a in list(srcs) + lands] + [jax.ShapeDtypeStruct((8, 128), F32)],
        in_specs=[HBM] * (2 * n), out_specs=[SEM] * 3 + [HBM] * (2 * n) + [pl.BlockSpec(memory_space=pltpu.VMEM)],
        input_output_aliases={i: 3 + i for i in range(2 * n)},
        compiler_params=pltpu.CompilerParams(has_side_effects=EFFECT),
    )(*[pltpu.with_memory_space_constraint(a, pltpu.HBM) for a in list(srcs) + lands])
    return (n, scatter, res[:3], res[3:3 + 2 * n]), res[-1]


def _exchange_wait(handle, after, name):
    n, scatter, sems, thru = handle

    def body(*refs):
        src_refs, land_refs = refs[:n], refs[n:2 * n]
        send_sems, recv_sems, local_sems = refs[2 * n:2 * n + 3]
        sends, arrivals, locals_ = _exchange_copies(scatter, src_refs, land_refs, send_sems, recv_sems, local_sems)
        for cp in arrivals:
            cp.wait_recv()
        for cp in sends:
            cp.wait_send()
        for cp in locals_:
            cp.wait()

    res = pl.pallas_call(
        body, name=name, out_shape=[pltpu.HBM(a.shape, a.dtype) for a in thru],
        in_specs=[HBM] * (2 * n) + [SEM] * 3 + [ANY], out_specs=[HBM] * (2 * n),
        input_output_aliases={i: i for i in range(2 * n)},
        compiler_params=pltpu.CompilerParams(has_side_effects=EFFECT),
    )(*thru, *sems, after)
    return res[n:]


def _behind(tokens, a):
    zero = sum(tok[0, 0] for tok in tokens)
    return jax.tree.map(lambda v: v + zero.astype(v.dtype), a)


def _perm_heads(a, perm, axis):
    idx = [slice(None)] * a.ndim
    parts = []
    for h in perm:
        idx[axis] = slice(64 * h, 64 * (h + 1))
        parts.append(a[tuple(idx)])
    idx[axis] = slice(512, None)
    if a.shape[axis] > 512:
        parts.append(a[tuple(idx)])
    return jnp.concatenate(parts, axis=axis)


Q_INV = tuple(int(i) for i in np.argsort(Q_PERM))


def _in0_to_kernel(a, axis):
    a = _perm_heads(a, Q_PERM, axis)
    idx = [slice(None)] * a.ndim

    def cut(lo, hi):
        idx[axis] = slice(lo, hi)
        return a[tuple(idx)]

    return jnp.concatenate([cut(0, 512), cut(768, 1792), cut(512, 768)], axis=axis)


def _in0_from_kernel(a, axis):
    idx = [slice(None)] * a.ndim

    def cut(lo, hi):
        idx[axis] = slice(lo, hi)
        return a[tuple(idx)]

    a = jnp.concatenate([cut(0, 512), cut(1536, 1792), cut(512, 1536)], axis=axis)
    return _perm_heads(a, Q_INV, axis)


def _f32_as_u16_rows(v, rows):
    bits = lax.bitcast_convert_type(v, jnp.uint16).reshape(-1)
    return jnp.pad(bits, (0, rows * D - bits.shape[0])).reshape(rows, D)


def _pad_rows(v, rows):
    v = v.reshape(-1)
    return jnp.pad(v, (0, rows * D - v.shape[0])).reshape(rows, D)


def kernel(x, mix_norm, a_w_in, a_b_in, a_sinks, a_conv_w, a_conv_b, a_cln_g, a_cln_b, a_w_out, c_w_in, c_w_pool, c_pool_scale, c_sln_g, c_sln_b, c_w_s, c_b_s, c_w_out, ffn_norm, ffn_w_gate, ffn_w_up, ffn_w_down, final_norm, loss_target, m_mix_norm, m_a_w_in, m_a_b_in, m_a_sinks, m_a_conv_w, m_a_conv_b, m_a_cln_g, m_a_cln_b, m_a_w_out, m_c_w_in, m_c_w_pool, m_c_pool_scale, m_c_sln_g, m_c_sln_b, m_c_w_s, m_c_b_s, m_c_w_out, m_ffn_norm, m_ffn_w_gate, m_ffn_w_up, m_ffn_w_down, m_final_norm, v_mix_norm, v_a_w_in, v_a_b_in, v_a_sinks, v_a_conv_w, v_a_conv_b, v_a_cln_g, v_a_cln_b, v_a_w_out, v_c_w_in, v_c_w_pool, v_c_pool_scale, v_c_sln_g, v_c_sln_b, v_c_w_s, v_c_b_s, v_c_w_out, v_ffn_norm, v_ffn_w_gate, v_ffn_w_up, v_ffn_w_down, v_final_norm):
    bsz, seq, _ = x.shape
    t = bsz * seq
    w_in = dict(mix_norm=mix_norm, a_w_in=a_w_in, a_b_in=a_b_in, a_sinks=a_sinks, a_conv_w=a_conv_w, a_conv_b=a_conv_b,
                a_cln_g=a_cln_g, a_cln_b=a_cln_b, a_w_out=a_w_out, c_w_in=c_w_in, c_w_pool=c_w_pool,
                c_pool_scale=c_pool_scale, c_sln_g=c_sln_g, c_sln_b=c_sln_b, c_w_s=c_w_s, c_b_s=c_b_s, c_w_out=c_w_out,
                ffn_norm=ffn_norm, ffn_w_gate=ffn_w_gate, ffn_w_up=ffn_w_up, ffn_w_down=ffn_w_down, final_norm=final_norm)
    m_in = dict(mix_norm=m_mix_norm, a_w_in=m_a_w_in, a_b_in=m_a_b_in, a_sinks=m_a_sinks, a_conv_w=m_a_conv_w,
                a_conv_b=m_a_conv_b, a_cln_g=m_a_cln_g, a_cln_b=m_a_cln_b, a_w_out=m_a_w_out, c_w_in=m_c_w_in,
                c_w_pool=m_c_w_pool, c_pool_scale=m_c_pool_scale, c_sln_g=m_c_sln_g, c_sln_b=m_c_sln_b, c_w_s=m_c_w_s,
                c_b_s=m_c_b_s, c_w_out=m_c_w_out, ffn_norm=m_ffn_norm, ffn_w_gate=m_ffn_w_gate, ffn_w_up=m_ffn_w_up,
                ffn_w_down=m_ffn_w_down, final_norm=m_final_norm)
    v_in = dict(mix_norm=v_mix_norm, a_w_in=v_a_w_in, a_b_in=v_a_b_in, a_sinks=v_a_sinks, a_conv_w=v_a_conv_w,
                a_conv_b=v_a_conv_b, a_cln_g=v_a_cln_g, a_cln_b=v_a_cln_b, a_w_out=v_a_w_out, c_w_in=v_c_w_in,
                c_w_pool=v_c_w_pool, c_pool_scale=v_c_pool_scale, c_sln_g=v_c_sln_g, c_sln_b=v_c_sln_b, c_w_s=v_c_w_s,
                c_b_s=v_c_b_s, c_w_out=v_c_w_out, ffn_norm=v_ffn_norm, ffn_w_gate=v_ffn_w_gate, ffn_w_up=v_ffn_w_up,
                ffn_w_down=v_ffn_w_down, final_norm=v_final_norm)

    small = jnp.concatenate([a_conv_w[0].reshape(-1), c_pool_scale[0], c_sln_g[0], c_sln_b[0]])
    first_bits = lax.bitcast_convert_type(a_w_in[0].T.astype(BF16), jnp.uint16)
    gathered, tok = _all_gather(jnp.concatenate([first_bits, _f32_as_u16_rows(small, W_MISC_ROWS)], axis=0), "gather_mixer0")

    def ffn_shards(l):
        return [ffn_w_gate[l].T.astype(BF16), ffn_w_up[l].T.astype(BF16), ffn_w_down[l].astype(BF16)]

    ffn0_h, tok = _exchange_start(_behind([tok], ffn_shards(0) + [a_w_out[0].astype(BF16)]), False, "gather_ffn0_start")
    mix1_h, tok = _exchange_start(_behind([tok], [c_w_in[0].T.astype(BF16), c_w_out[0].astype(BF16)]), False,
                                  "gather_mixer1_start")
    ffn1_h, tok = _exchange_start(_behind([tok], ffn_shards(1)), False, "gather_ffn1_start")

    a_in_full = lax.bitcast_convert_type(gathered[:, :224].reshape(IN0, D), BF16)
    small_all = lax.bitcast_convert_type(
        gathered[:, 224:].reshape(N_DEV, -1)[:, :2 * SMALL_SHARD].reshape(N_DEV, SMALL_SHARD, 2), F32)
    conv_w = small_all[:, :31 * 64].reshape(N_DEV, 31, 64).transpose(1, 0, 2).reshape(31, 512)
    conv_w = jnp.pad(conv_w, ((0, HALO - CONV_K), (0, 0)))
    pool_scale = small_all[:, 31 * 64:31 * 64 + 64].reshape(1, 512)
    sln_g = small_all[:, 31 * 64 + 64:31 * 64 + 128].reshape(1, 512)
    sln_b = small_all[:, 31 * 64 + 128:].reshape(1, 512)

    wt_in0 = _in0_to_kernel(a_in_full, 0)
    b_in0 = _in0_to_kernel(a_b_in, 1)
    b_rows = jnp.broadcast_to(c_b_s[0][:, :, None], (4, 128, 128))
    conv_b, cln_g, cln_b = a_conv_b, a_cln_g, a_cln_b

    h0 = x.reshape(t, D)
    target = loss_target.reshape(t, D)
    z0, hn0 = _norm_proj(h0, _behind([tok], mix_norm[0:1]), wt_in0, b_in0, "in_proj0")
    attn, tok = _attn_fwd(z0, a_sinks, bsz, "attn_fwd")
    conv, conv_y = _conv_fwd(z0, conv_w, conv_b, cln_g, _behind([tok], cln_b), bsz, "conv_fwd")
    wtg0, wtu0, wd0, a_out_full = (w.reshape(-1, D) for w in _exchange_wait(ffn0_h, conv, "gather_ffn0_wait"))
    w_out0 = _perm_heads(a_out_full, Q_PERM, 0)
    h1, hnf0, gate0a, up0a, part = _ffn_fwd_a(h0, attn, conv, w_out0, ffn_norm[0:1], wtg0, wtu0, wd0, "ffn_fwd0a")
    gate0b, up0b, h2 = _ffn_fwd_b(hnf0, part, wtg0, wtu0, wd0, "ffn_fwd0b")
    wt_in1, w_out1 = (w.reshape(-1, D) for w in _exchange_wait(mix1_h, h2, "gather_mixer1_wait"))
    z1, hn1 = _norm_proj(h2, mix_norm[1:2], wt_in1, None, "in_proj1")
    pool, tok = _pool_fwd(z1, c_w_pool[0], pool_scale, bsz, "pool_fwd")
    sgu = _sgu_fwd(z1, sln_g, _behind([tok], sln_b), c_w_s[0], b_rows, "sgu_fwd")
    wtg1, wtu1, wd1 = (w.reshape(D_FF, D) for w in _exchange_wait(ffn1_h, sgu, "gather_ffn1_wait"))
    h3, hnf1, gate1a, up1a, part = _ffn_fwd_a(h2, pool, sgu, w_out1, ffn_norm[1:2], wtg1, wtu1, wd1, "ffn_fwd1a")
    gate1b, up1b, dh4, d_final_norm, loss_part = _ffn_fwd_b(hnf1, part, wtg1, wtu1, wd1, "ffn_fwd1b",
                                                            head=(final_norm.reshape(1, D), target))

    def blocks(g):
        return g.reshape(N_DEV, g.shape[0] // N_DEV, D)

    *wide, part = _ffn_bwd_a(dh4, gate1a, up1a, wtg1, wtu1, wd1, "ffn_bwd1a")
    dh3, dmix1, dgate1, dup1, act1, d_fn1 = _ffn_bwd_b(dh4, h3, ffn_norm[1:2], gate1b, up1b, wtg1, wtu1, wd1, w_out1, part,
                                                       wide, "ffn_bwd1b")
    gw_ffn1 = [_mm_tn(dgate1, hnf1, "dw_gate1"), _mm_tn(dup1, hnf1, "dw_up1"), _mm_tn(act1, dh4, "dw_down1")]
    ffn1_g, tok = _exchange_start([blocks(g) for g in gw_ffn1], True, "scatter_ffn1_start")
    gw_c_out = _mm_tn_pieces([pool, sgu], dh3, "dw_out1")
    dzp, d_w_pool, d_pool_scale = _pool_bwd(z1, dmix1, c_w_pool[0], _behind([tok], pool_scale), bsz, "pool_bwd")
    dzu, dzv, d_w_s, d_b_s, d_sln_g, d_sln_b = _sgu_bwd(z1, dmix1, sln_g, sln_b, c_w_s[0], b_rows, "sgu_bwd")
    dh2, d_mn1 = _proj_bwd_norm([(dzp, 0), (dzu, 512), (dzv, 1024)], wt_in1, h2, dh3, mix_norm[1:2], BF16, "in_proj1_bwd")
    gw_c_in = _mm_tn_pieces([dzp, dzu, dzv], hn1, "dw_in1")
    mix1_g, tok = _exchange_start([blocks(gw_c_in), blocks(gw_c_out)], True, "scatter_mixer1_start")
    *wide, part = _ffn_bwd_a(dh2, gate0a, up0a, wtg0, wtu0, wd0, "ffn_bwd0a")
    dh1, dmix0, dgate0, dup0, act0, d_fn0 = _ffn_bwd_b(dh2, h1, _behind([tok], ffn_norm[0:1]), gate0b, up0b, wtg0, wtu0, wd0,
                                                       w_out0, part, wide, "ffn_bwd0b")
    gw_ffn0 = [_mm_tn(dgate0, hnf0, "dw_gate0"), _mm_tn(dup0, hnf0, "dw_up0"), _mm_tn(act0, dh2, "dw_down0")]
    ffn0_g, tok = _exchange_start([blocks(g) for g in gw_ffn0], True, "scatter_ffn0_start")
    gw_a_out = _perm_heads(_mm_tn_pieces([attn, conv], dh1, "dw_out0"), Q_INV, 0)
    dq, dkv, d_sink_row, d_bq, d_bkv = _attn_bwd(z0, dmix0, _behind([tok], a_sinks), bsz, "attn_bwd")
    dca, dcg, d_conv_w, d_conv_b, d_cln_g, d_cln_b, d_ba, d_bg = _conv_bwd(z0, conv_y, dmix0, conv_w, cln_g, cln_b, bsz, "conv_bwd")
    gw_a_in = _in0_from_kernel(_mm_tn_pieces([dq, dca, dcg, dkv], hn0, "dw_in0"), 0)
    mix0_g, tok = _exchange_start([blocks(gw_a_in), blocks(gw_a_out)], True, "scatter_mixer0_start")
    dx, d_mn0 = _proj_bwd_norm([(dq, 0), (dca, 512), (dcg, 1024), (dkv, 1536)], wt_in0, h0, dh1,
                               _behind([tok], mix_norm[0:1]), F32, "in_proj0_bwd")
    d_b_in = _in0_from_kernel(jnp.concatenate([d_bq, d_ba, d_bg, d_bkv], axis=1), 1)

    rep = dict(mix_norm=jnp.concatenate([d_mn0, d_mn1], axis=0), a_b_in=d_b_in, a_sinks=d_sink_row[:, :8],
               a_conv_b=d_conv_b, a_cln_g=d_cln_g, a_cln_b=d_cln_b, c_w_pool=d_w_pool.reshape(64, D),
               c_w_s=d_w_s.reshape(64, D), c_b_s=d_b_s, ffn_norm=jnp.concatenate([d_fn0, d_fn1], axis=0),
               final_norm=d_final_norm)
    rep_rows = jnp.concatenate([_as_rows(rep[nm]) for nm, _ in REP_2D] + [_as_rows(loss_part)], axis=0)
    rep_flat = jnp.pad(rep_rows, ((0, N_DEV * REP_ROWS - rep_rows.shape[0]), (0, 0))).reshape(N_DEV, REP_ROWS, D)
    small_g = jnp.concatenate([
        d_conv_w[:CONV_K].reshape(31, N_DEV, 64).transpose(1, 0, 2).reshape(N_DEV, 31 * 64),
        d_pool_scale.reshape(N_DEV, 64), d_sln_g.reshape(N_DEV, 64), d_sln_b.reshape(N_DEV, 64)], axis=1)
    small_g = jnp.pad(small_g, ((0, 0), (0, G_SMALL_ROWS * D - SMALL_SHARD))).reshape(N_DEV, G_SMALL_ROWS, D)
    tail_g, tok = _exchange_start([jnp.concatenate([small_g, rep_flat], axis=1)], True, "scatter_tail_start")

    names = list(w_in)
    g_out, delta, new_m, new_v = {}, {}, {}, {}
    column_sharded = ("a_w_in", "c_w_in", "ffn_w_gate", "ffn_w_up")

    def rows_of(a, nm):
        return jnp.swapaxes(a, 1, 2) if nm in column_sharded else a

    def reduce_adamw(nm, landing, layer, into=None):
        args = [rows_of(d[nm], nm) for d in (w_in, m_in, v_in)]
        if args[0].shape[0] == 1:
            args, layer = [a[0] for a in args], None
        return _reduce_adamw(landing, *args, "adamw_%s_%s" % (nm, layer), layer=layer, into=into)

    def keep(nm, res):
        res = [r if r.ndim == 3 else r[None] for r in res]
        g_out[nm], delta[nm], new_m[nm], new_v[nm] = (rows_of(r, nm) for r in res)

    ffn_names = ("ffn_w_gate", "ffn_w_up", "ffn_w_down")
    landed = _exchange_wait(ffn1_g, tok, "scatter_ffn1_wait")
    ffn_res = [reduce_adamw(nm, a, 1) for nm, a in zip(ffn_names, landed)]
    landed = _exchange_wait(mix1_g, ffn_res[-1][0], "scatter_mixer1_wait")
    for nm, a in zip(("c_w_in", "c_w_out"), landed):
        keep(nm, reduce_adamw(nm, a, 0))
    landed = _exchange_wait(ffn0_g, g_out["c_w_out"], "scatter_ffn0_wait")
    for nm, a, res in zip(ffn_names, landed, ffn_res):
        keep(nm, reduce_adamw(nm, a, 0, into=res))
    landed = _exchange_wait(mix0_g, g_out["ffn_w_down"], "scatter_mixer0_wait")
    for nm, a in zip(("a_w_in", "a_w_out"), landed):
        keep(nm, reduce_adamw(nm, a, 0))
    g_tail = _sum_slabs(_exchange_wait(tail_g, g_out["a_w_out"], "scatter_tail_wait")[0], "sum_tail")
    rep_all = _all_gather(g_tail[G_SMALL_ROWS:], "gather_replicated_grads")[0].reshape(N_DEV * REP_ROWS, D)
    loss = rep_all[sum(_rows_needed(s) for _, s in REP_2D), 0]
    res = _adamw_replicated(rep_all, *[[d[nm].reshape(s) for nm, s in REP_2D] for d in (w_in, m_in, v_in)], "adamw_replicated")
    for k, (nm, _) in enumerate(REP_2D):
        g_out[nm], delta[nm], new_m[nm], new_v[nm] = (r.reshape(w_in[nm].shape) for r in res[4 * k:4 * k + 4])
    small_r = g_tail[:G_SMALL_ROWS].reshape(-1)[:SMALL_SHARD]
    g_out.update(
        a_conv_w=small_r[:31 * 64].reshape(1, 31, 64), c_pool_scale=small_r[31 * 64:31 * 64 + 64].reshape(1, 64),
        c_sln_g=small_r[31 * 64 + 64:31 * 64 + 128].reshape(1, 64), c_sln_b=small_r[31 * 64 + 128:].reshape(1, 64))
    group = ("a_conv_w", "c_pool_scale", "c_sln_g", "c_sln_b")
    flat = [_pad_rows(jnp.concatenate([d[nm].reshape(-1) for nm in group]), G_SMALL_ROWS) for d in (w_in, g_out, m_in, v_in)]
    res = [r.reshape(-1) for r in _adamw(*flat, "adamw_small_sharded")]
    off = 0
    for nm in group:
        n = int(np.prod(w_in[nm].shape))
        delta[nm], new_m[nm], new_v[nm] = (r[off:off + n].reshape(w_in[nm].shape) for r in res)
        off += n

    grad_x = dx.reshape(bsz, seq, D)
    return (loss, grad_x, *[g_out[nm] for nm in names], *[delta[nm] for nm in names],
            *[new_m[nm] for nm in names], *[new_v[nm] for nm in names])
```

```python
import functools

import jax
import jax.numpy as jnp
import numpy as np
from jax import lax
from jax.experimental import pallas as pl
from jax.experimental.pallas import tpu as pltpu

F32 = jnp.float32
BF16 = jnp.bfloat16
MESH = pl.DeviceIdType.MESH

D = 1024
N_DEV = 8
EPS = 1e-5
HEAD_PAIRS = 4
ATT_BLK = 128
CONV_K = 31
HALO = 32
D_FF = 2816
IN0 = 1792
SGU_CHUNK = 128
GELU_C = 0.7978845608028654
GELU_A = 0.044715
ADAM_LR, ADAM_B1, ADAM_B2, ADAM_EPS, ADAM_WD, ADAM_STEP = 0.001, 0.9, 0.999, 1e-08, 0.01, 10
VMEM_LIMIT = 56 << 20

SMALL_SHARD = 31 * 64 + 3 * 64
W_MISC_ROWS = 16
G_MISC_ROWS = 32
G_SMALL_ROWS = 8
REP_ROWS = G_MISC_ROWS - G_SMALL_ROWS
REP_2D = (("c_w_pool", (64, 1024)), ("c_w_s", (64, 1024)), ("mix_norm", (2, 1024)), ("a_b_in", (1, 1792)), ("a_sinks", (1, 8)),
          ("a_conv_b", (1, 512)), ("a_cln_g", (1, 512)), ("a_cln_b", (1, 512)), ("c_b_s", (4, 128)), ("ffn_norm", (2, 1024)),
          ("final_norm", (1, 1024)))
Q_PERM = (0, 4, 1, 5, 2, 6, 3, 7)


def _params(*sem):
    return pltpu.CompilerParams(dimension_semantics=sem, vmem_limit_bytes=VMEM_LIMIT)


def _nn(a, b):
    return jnp.dot(a, b, preferred_element_type=F32)


def _nt(a, b):
    return lax.dot_general(a, b, (((1,), (1,)), ((), ())), preferred_element_type=F32)


def _tn(a, b):
    return lax.dot_general(a, b, (((0,), (0,)), ((), ())), preferred_element_type=F32)


def _tile(n, want=512):
    t = min(want, n)
    assert n % t == 0, (n, t)
    return t


def _seq_tile(s):
    return 512 if s >= 1024 else s // 2


def _rms(x, g):
    r = lax.rsqrt(jnp.mean(x * x, axis=-1, keepdims=True) + EPS)
    return x * r * g, r


def _rms_bwd(x, g, d_y):
    r = lax.rsqrt(jnp.mean(x * x, axis=-1, keepdims=True) + EPS)
    xr = x * r
    u = d_y * g
    d_x = r * (u - xr * jnp.mean(u * xr, axis=-1, keepdims=True))
    return d_x, jnp.sum(d_y * xr, axis=0, keepdims=True)


def _ln(y, g, b):
    mu = jnp.mean(y, axis=-1, keepdims=True)
    yc = y - mu
    rstd = lax.rsqrt(jnp.mean(yc * yc, axis=-1, keepdims=True) + EPS)
    xhat = yc * rstd
    return xhat * g + b, xhat, rstd


def _ln_bwd(d_o, xhat, rstd, g):
    dxh = d_o * g
    return rstd * (dxh - jnp.mean(dxh, axis=-1, keepdims=True) - xhat * jnp.mean(dxh * xhat, axis=-1, keepdims=True))


def _gelu(x):
    th = jnp.tanh(GELU_C * (x + GELU_A * x * x * x))
    return 0.5 * x * (1.0 + th), th


def _gelu_grad(x, th):
    return 0.5 * (1.0 + th) + 0.5 * x * (1.0 - th * th) * GELU_C * (1.0 + 3.0 * GELU_A * x * x)


def _row(c):
    return pl.BlockSpec((1, c), lambda *_: (0, 0))


def _full(shape):
    return pl.BlockSpec(shape, lambda *_: (0,) * len(shape))


def _norm_proj(h, g, wt, bias, name):
    t, n = h.shape[0], wt.shape[0]
    tm = _tile(t, 1024)
    has_bias = bias is not None

    def body(*refs):
        h_ref, g_ref, wt_ref = refs[:3]
        z_ref, hn_ref = refs[-2:]
        hn = _rms(h_ref[...].astype(F32), g_ref[...])[0].astype(BF16)
        hn_ref[...] = hn
        z = _nt(hn, wt_ref[...])
        if has_bias:
            z = z + refs[3][...]
        z_ref[...] = z.astype(BF16)

    in_specs = [pl.BlockSpec((tm, D), lambda i: (i, 0)), _row(D), _full((n, D))]
    args = [h, g, wt]
    if has_bias:
        in_specs.append(_row(n))
        args.append(bias)
    return pl.pallas_call(
        body, name=name, grid=(t // tm,), in_specs=in_specs,
        out_specs=[pl.BlockSpec((tm, n), lambda i: (i, 0)), pl.BlockSpec((tm, D), lambda i: (i, 0))],
        out_shape=[jax.ShapeDtypeStruct((t, n), BF16), jax.ShapeDtypeStruct((t, D), BF16)],
        compiler_params=_params("parallel"))(*args)


def _ff_pieces(tf, width=256):
    return [(c0, min(width, tf - c0)) for c0 in range(0, tf, width)]


FF_HALF = D_FF // 2


def _half_spec(part):
    return pl.BlockSpec((FF_HALF, D), lambda i: (part, 0))


def _ffn_half_fwd(hn, wtg_ref, wtu_ref, gate_ref, up_ref, act):
    for c0, cw in _ff_pieces(FF_HALF):
        cols = slice(c0, c0 + cw)
        gate = _nt(hn, wtg_ref[cols, :])
        up = _nt(hn, wtu_ref[cols, :])
        gate_ref[:, cols] = gate.astype(BF16)
        up_ref[:, cols] = up.astype(BF16)
        act[:, cols] = (gate * jax.nn.sigmoid(gate) * up).astype(BF16)


def _ffn_fwd_a(h_prev, a, b, w_out, g, wtg, wtu, wd, name):
    t = h_prev.shape[0]
    tm = _tile(t)

    def body(hp_ref, a_ref, b_ref, wa_ref, wb_ref, g_ref, wtg_ref, wtu_ref, wd_ref,
             h_ref, hn_ref, gate_ref, up_ref, part_ref, act):
        x = hp_ref[...].astype(F32) + _nn(a_ref[...], wa_ref[...]) + _nn(b_ref[...], wb_ref[...])
        h_ref[...] = x.astype(BF16)
        hn = _rms(x, g_ref[...])[0].astype(BF16)
        hn_ref[...] = hn
        _ffn_half_fwd(hn, wtg_ref, wtu_ref, gate_ref, up_ref, act)
        part_ref[...] = x + _nn(act[...], wd_ref[...])

    tok = pl.BlockSpec((tm, D), lambda i: (i, 0))
    half = pl.BlockSpec((tm, 512), lambda i: (i, 0))
    mid = pl.BlockSpec((tm, FF_HALF), lambda i: (i, 0))
    res = jax.ShapeDtypeStruct((t, D), BF16)
    mid_shape = jax.ShapeDtypeStruct((t, FF_HALF), BF16)
    return pl.pallas_call(
        body, name=name, grid=(t // tm,),
        in_specs=[tok, half, half, pl.BlockSpec((512, D), lambda i: (0, 0)), pl.BlockSpec((512, D), lambda i: (1, 0)),
                  _row(D), _half_spec(0), _half_spec(0), _half_spec(0)],
        out_specs=[tok, tok, mid, mid, tok], out_shape=[res, res, mid_shape, mid_shape, jax.ShapeDtypeStruct((t, D), F32)],
        scratch_shapes=[pltpu.VMEM((tm, FF_HALF), BF16)],
        compiler_params=_params("parallel"))(h_prev, a, b, w_out, w_out, g, wtg, wtu, wd)


def _ffn_fwd_b(hn, part, wtg, wtu, wd, name, head=None):
    t = hn.shape[0]
    tm = _tile(t)
    n_head = 0 if head is None else 2

    def body(*refs):
        hn_ref, part_ref, wtg_ref, wtu_ref, wd_ref = refs[:5]
        gate_ref, up_ref, o_ref = refs[5 + n_head:8 + n_head]
        act = refs[-1]
        _ffn_half_fwd(hn_ref[...], wtg_ref, wtu_ref, gate_ref, up_ref, act)
        x = part_ref[...] + _nn(act[...], wd_ref[...])
        if head is None:
            o_ref[...] = x.astype(BF16)
        else:
            fg_ref, t_ref = refs[5:7]
            dfg_ref, loss_ref = refs[10:12]

            @pl.when(pl.program_id(0) == 0)
            def _():
                dfg_ref[...] = jnp.zeros_like(dfg_ref)
                loss_ref[...] = jnp.zeros_like(loss_ref)

            gv = fg_ref[...]
            err = _rms(x, gv)[0] - t_ref[...]
            loss_ref[...] += 0.5 * jnp.sum(jnp.mean(err * err, axis=-1, keepdims=True), axis=0, keepdims=True)
            d_x, d_g = _rms_bwd(x, gv, err * (1.0 / D))
            o_ref[...] = d_x.astype(BF16)
            dfg_ref[...] += d_g

    tok = pl.BlockSpec((tm, D), lambda i: (i, 0))
    mid = pl.BlockSpec((tm, FF_HALF), lambda i: (i, 0))
    mid_shape = jax.ShapeDtypeStruct((t, FF_HALF), BF16)
    in_specs = [tok, tok, _half_spec(1), _half_spec(1), _half_spec(1)]
    out_specs = [mid, mid, tok]
    out_shape = [mid_shape, mid_shape, jax.ShapeDtypeStruct((t, D), BF16)]
    if head is not None:
        in_specs += [_row(D), tok]
        out_specs += [_row(D), _row(1)]
        out_shape += [jax.ShapeDtypeStruct((1, D), F32), jax.ShapeDtypeStruct((1, 1), F32)]
    return pl.pallas_call(
        body, name=name, grid=(t // tm,), in_specs=in_specs, out_specs=out_specs, out_shape=out_shape,
        scratch_shapes=[pltpu.VMEM((tm, FF_HALF), BF16)],
        compiler_params=_params("parallel" if head is None else "arbitrary"))(hn, part, wtg, wtu, wd, *(head or ()))


def _ffn_half_bwd(dh, gate_ref, up_ref, wd_ref, dgate_ref, dup_ref, act_ref):
    for c0, cw in _ff_pieces(FF_HALF):
        cols = slice(c0, c0 + cw)
        da = _nt(dh, wd_ref[cols, :])
        gt = gate_ref[:, cols].astype(F32)
        u = up_ref[:, cols].astype(F32)
        sg = jax.nn.sigmoid(gt)
        sil = gt * sg
        act_ref[:, cols] = (sil * u).astype(BF16)
        dup_ref[:, cols] = (da * sil).astype(BF16)
        dgate_ref[:, cols] = (da * u * sg * (1.0 + gt * (1.0 - sg))).astype(BF16)


def _ffn_bwd_a(dh, gate, up, wtg, wtu, wd, name):
    t = dh.shape[0]
    tm = _tile(t)

    def body(dh_ref, gate_ref, up_ref, wtg_ref, wtu_ref, wd_ref, dgate_ref, dup_ref, act_ref, part_ref):
        _ffn_half_bwd(dh_ref[...], gate_ref, up_ref, wd_ref, dgate_ref, dup_ref, act_ref)
        part_ref[...] = _nn(dgate_ref[...], wtg_ref[...]) + _nn(dup_ref[...], wtu_ref[...])

    tok = pl.BlockSpec((tm, D), lambda i: (i, 0))
    mid = pl.BlockSpec((tm, FF_HALF), lambda i: (i, 0))
    wide = jax.ShapeDtypeStruct((t, D_FF), BF16)
    return pl.pallas_call(
        body, name=name, grid=(t // tm,), in_specs=[tok, mid, mid, _half_spec(0), _half_spec(0), _half_spec(0)],
        out_specs=[mid, mid, mid, tok], out_shape=[wide, wide, wide, jax.ShapeDtypeStruct((t, D), F32)],
        compiler_params=_params("parallel"))(dh, gate, up, wtg, wtu, wd)


def _ffn_bwd_b(dh, h, g, gate, up, wtg, wtu, wd, w_out, part, wide, name):
    t = dh.shape[0]
    tm = _tile(t)

    def body(dh_ref, h_ref, g_ref, gate_ref, up_ref, wtg_ref, wtu_ref, wd_ref, wout_ref, part_ref, _a, _b, _c,
             dhin_ref, dmix_ref, dgate_ref, dup_ref, act_ref, dg_ref):
        @pl.when(pl.program_id(0) == 0)
        def _():
            dg_ref[...] = jnp.zeros_like(dg_ref)

        _ffn_half_bwd(dh_ref[...], gate_ref, up_ref, wd_ref, dgate_ref, dup_ref, act_ref)
        d_hn = part_ref[...] + _nn(dgate_ref[...], wtg_ref[...]) + _nn(dup_ref[...], wtu_ref[...])
        d_x, d_g = _rms_bwd(h_ref[...].astype(F32), g_ref[...], d_hn)
        dhin = (dh_ref[...].astype(F32) + d_x).astype(BF16)
        dhin_ref[...] = dhin
        dmix_ref[...] = _nt(dhin, wout_ref[...]).astype(BF16)
        dg_ref[...] += d_g

    tok = pl.BlockSpec((tm, D), lambda i: (i, 0))
    mid = pl.BlockSpec((tm, FF_HALF), lambda i: (i, 0))
    second = pl.BlockSpec((tm, FF_HALF), lambda i: (i, 1))
    wide_shape = jax.ShapeDtypeStruct((t, D_FF), BF16)
    res = jax.ShapeDtypeStruct((t, D), BF16)
    return pl.pallas_call(
        body, name=name, grid=(t // tm,),
        in_specs=[tok, tok, _row(D), mid, mid, _half_spec(1), _half_spec(1), _half_spec(1), _full((D, D)), tok, ANY, ANY, ANY],
        out_specs=[tok, tok, second, second, second, _row(D)],
        out_shape=[res, res, wide_shape, wide_shape, wide_shape, jax.ShapeDtypeStruct((1, D), F32)],
        input_output_aliases={10: 2, 11: 3, 12: 4},
        compiler_params=_params("arbitrary"))(dh, h, g, gate, up, wtg, wtu, wd, w_out, part, *wide)


def _proj_bwd_norm(pieces, wt, h, dh, g, dtype, name):
    t = h.shape[0]
    tm = _tile(t)
    n_p = len(pieces)

    def body(*refs):
        p_refs, w_refs = refs[:n_p], refs[n_p:2 * n_p]
        h_ref, dh_ref, g_ref, o_ref, dg_ref = refs[2 * n_p:]

        @pl.when(pl.program_id(0) == 0)
        def _():
            dg_ref[...] = jnp.zeros_like(dg_ref)

        d_hn = _nn(p_refs[0][...], w_refs[0][...])
        for p_ref, w_ref in zip(p_refs[1:], w_refs[1:]):
            d_hn = d_hn + _nn(p_ref[...], w_ref[...])
        d_x, d_g = _rms_bwd(h_ref[...].astype(F32), g_ref[...], d_hn)
        o_ref[...] = (dh_ref[...].astype(F32) + d_x).astype(dtype)
        dg_ref[...] += d_g

    tok = pl.BlockSpec((tm, D), lambda i: (i, 0))
    in_specs = [pl.BlockSpec((tm, a.shape[1]), lambda i: (i, 0)) for a, _ in pieces]
    for a, off in pieces:
        w = a.shape[1]
        assert off % w == 0
        in_specs.append(pl.BlockSpec((w, D), functools.partial(lambda i, blk: (blk, 0), blk=off // w)))
    in_specs += [tok, tok, _row(D)]
    return pl.pallas_call(
        body, name=name, grid=(t // tm,), in_specs=in_specs, out_specs=[tok, _row(D)],
        out_shape=[jax.ShapeDtypeStruct((t, D), dtype), jax.ShapeDtypeStruct((1, D), F32)],
        compiler_params=_params("arbitrary"))(*[a for a, _ in pieces], *([wt] * n_p), h, dh, g)


def _mm_tn(a, b, name):
    t, n = a.shape
    k = b.shape[1]
    tn = n if n <= 1024 else n // 2
    tt = _tile(t, 2048)
    nt = t // tt

    def body(a_ref, b_ref, o_ref, acc):
        s = pl.program_id(1)

        @pl.when(s == 0)
        def _():
            acc[...] = jnp.zeros_like(acc)

        acc[...] += _tn(a_ref[...], b_ref[...].astype(BF16))

        @pl.when(s == nt - 1)
        def _():
            o_ref[...] = acc[...].astype(BF16)

    return pl.pallas_call(
        body, name=name, grid=(n // tn, nt),
        in_specs=[pl.BlockSpec((tt, tn), lambda j, s: (s, j)), pl.BlockSpec((tt, k), lambda j, s: (s, 0))],
        out_specs=pl.BlockSpec((tn, k), lambda j, s: (j, 0)), out_shape=jax.ShapeDtypeStruct((n, k), BF16),
        scratch_shapes=[pltpu.VMEM((tn, k), F32)],
        compiler_params=_params("parallel", "arbitrary"))(a, b)


def _mm_tn_pieces(pieces, b, name):
    t, k = b.shape
    widths = [p.shape[1] for p in pieces]
    n, n_p = sum(widths), len(pieces)
    tt = _tile(t, 1024)
    nt = t // tt

    def body(*refs):
        b_ref, o_ref, acc = refs[n_p:]
        s = pl.program_id(0)

        @pl.when(s == 0)
        def _():
            acc[...] = jnp.zeros_like(acc)

        bb = b_ref[...].astype(BF16)
        off = 0
        for p_ref, w in zip(refs[:n_p], widths):
            acc[off:off + w, :] += _tn(p_ref[...], bb)
            off += w

        @pl.when(s == nt - 1)
        def _():
            o_ref[...] = acc[...].astype(BF16)

    return pl.pallas_call(
        body, name=name, grid=(nt,),
        in_specs=[pl.BlockSpec((tt, w), lambda s: (s, 0)) for w in widths] + [pl.BlockSpec((tt, k), lambda s: (s, 0))],
        out_specs=_full((n, k)), out_shape=jax.ShapeDtypeStruct((n, k), BF16),
        scratch_shapes=[pltpu.VMEM((n, k), F32)], compiler_params=_params("arbitrary"))(*pieces, b)


STACK = HEAD_PAIRS * ATT_BLK


def _attn_valid(first, rows):
    qi = lax.broadcasted_iota(jnp.int32, (rows, 2 * ATT_BLK), 0) % ATT_BLK
    r = lax.broadcasted_iota(jnp.int32, (rows, 2 * ATT_BLK), 1)
    dist = qi + ATT_BLK - r
    return (dist >= 0) & (dist < ATT_BLK) & ((r >= ATT_BLK) | jnp.logical_not(first))


def _stacked(ref, kh, scale):
    lo = lax.broadcasted_iota(jnp.int32, (ATT_BLK, 128), 1) < 64
    keep = lo if kh == 0 else ~lo
    parts = [jnp.where(keep, ref[:, g * 128:(g + 1) * 128] * scale, 0.0).astype(BF16) for g in range(HEAD_PAIRS)]
    return jnp.concatenate(parts, axis=0)


def _unstacked(a0, a1, g):
    lo = lax.broadcasted_iota(jnp.int32, (ATT_BLK, 128), 1) < 64
    rows = slice(g * ATT_BLK, (g + 1) * ATT_BLK)
    return jnp.where(lo, a0[rows], a1[rows])


def _sink_rows(s_ref, kh):
    return jnp.concatenate([jnp.full((ATT_BLK, 128), s_ref[0, kh * 4 + g], F32) for g in range(HEAD_PAIRS)], axis=0)


def _row_sums(a, split):
    hi = a.astype(BF16)
    ones = jnp.ones((2 * ATT_BLK, 128), BF16)
    if not split:
        return _nn(hi, ones)
    lo = (a - hi.astype(F32)).astype(BF16)
    return _nn(hi, ones) + _nn(lo, ones)


def _both(a):
    return jnp.concatenate([a, a], axis=1)


def _attn_probs(qs, kpair, sink, valid):
    s = jnp.where(valid, _nt(qs, kpair), -1e30)
    m = jnp.maximum(jnp.broadcast_to(jnp.max(s, axis=-1, keepdims=True), (s.shape[0], 128)), sink)
    p = jnp.exp(s - _both(m))
    es = jnp.exp(sink - m)
    inv = 1.0 / (_row_sums(p, split=True) + es)
    return p * _both(inv), es * inv


def _attn_probs_head(qm, kpair, sink, valid):
    s = jnp.where(valid, _nt(qm, kpair), -1e30)
    m = jnp.maximum(jnp.max(s, axis=-1, keepdims=True), sink)
    p = jnp.exp(s - m)
    return p * (1.0 / (jnp.sum(p, axis=-1, keepdims=True) + jnp.exp(sink - m)))


def _attn_specs(bsz, order):
    q = pl.BlockSpec((bsz, ATT_BLK, 512), lambda j: (0, order(j), 0))
    kvc = pl.BlockSpec((bsz, ATT_BLK, 256), lambda j: (0, order(j), 6))
    kvp = pl.BlockSpec((bsz, ATT_BLK, 256), lambda j: (0, jnp.maximum(order(j) - 1, 0), 6))
    return q, kvc, kvp


def _window_kv(kvc_ref, kvp_ref):
    kvc, kvp = kvc_ref[...], kvp_ref[...]
    kpair = jnp.concatenate([kvp[:, :128], kvc[:, :128]], axis=0)
    vpair = jnp.concatenate([kvp[:, 128:], kvc[:, 128:]], axis=0)
    return kpair, vpair


def _attn_fwd(z0, sinks, bsz, name):
    t = z0.shape[0]
    seq = t // bsz
    nb = seq // ATT_BLK

    def body(s_ref, q_ref, kvc_ref, kvp_ref, o_ref, token):
        token[...] = jnp.zeros_like(token)
        valid = _attn_valid(pl.program_id(0) == 0, ATT_BLK)
        lo = lax.broadcasted_iota(jnp.int32, (ATT_BLK, 128), 1) < 64
        for b in range(bsz):
            kpair, vpair = _window_kv(kvc_ref.at[b], kvp_ref.at[b])
            for g in range(HEAD_PAIRS):
                qs = q_ref[b, :, g * 128:(g + 1) * 128] * 0.125
                outs = []
                for kh in range(2):
                    qm = jnp.where(lo if kh == 0 else ~lo, qs, 0.0).astype(BF16)
                    p = _attn_probs_head(qm, kpair, s_ref[0, kh * 4 + g], valid)
                    outs.append(_nn(p.astype(BF16), vpair))
                o_ref[b, :, g * 128:(g + 1) * 128] = jnp.where(lo, outs[0], outs[1]).astype(BF16)

    q, kvc, kvp = _attn_specs(bsz, lambda j: j)
    z3 = z0.reshape(bsz, seq, z0.shape[1])
    out, token = pl.pallas_call(
        body, name=name, grid=(nb,),
        in_specs=[pl.BlockSpec(memory_space=pltpu.SMEM), q, kvc, kvp],
        out_specs=[pl.BlockSpec((bsz, ATT_BLK, 512), lambda j: (0, j, 0)), _full((8, 128))],
        out_shape=[jax.ShapeDtypeStruct((bsz, seq, 512), BF16), jax.ShapeDtypeStruct((8, 128), F32)],
        compiler_params=_params("arbitrary"))(sinks, z3, z3, z3)
    return out.reshape(t, 512), token


def _attn_bwd(z0, dmix, sinks, bsz, name):
    t = z0.shape[0]
    seq = t // bsz
    nb = seq // ATT_BLK

    def body(s_ref, q_ref, kvc_ref, kvp_ref, do_ref, dq_ref, dkv_ref, dsink_ref, dbq_ref, dbkv_ref, carry):
        j = pl.program_id(0)

        @pl.when(j == 0)
        def _():
            carry[...] = jnp.zeros_like(carry)
            dsink_ref[...] = jnp.zeros_like(dsink_ref)
            dbq_ref[...] = jnp.zeros_like(dbq_ref)
            dbkv_ref[...] = jnp.zeros_like(dbkv_ref)

        valid = _attn_valid(j == nb - 1, STACK)
        lane = lax.broadcasted_iota(jnp.int32, (1, 128), 1)
        dsink = jnp.zeros((1, 128), F32)
        dbq = [jnp.zeros((1, 128), F32)] * HEAD_PAIRS
        dbkv = jnp.zeros((1, 256), F32)
        for b in range(bsz):
            kpair, vpair = _window_kv(kvc_ref.at[b], kvp_ref.at[b])
            dk = jnp.zeros((2 * ATT_BLK, 128), F32)
            dv = jnp.zeros((2 * ATT_BLK, 128), F32)
            dqs = []
            for kh in range(2):
                qs = _stacked(q_ref.at[b], kh, 0.125)
                dos = _stacked(do_ref.at[b], kh, 1.0)
                p, ps = _attn_probs(qs, kpair, _sink_rows(s_ref, kh), valid)
                dp = _nt(dos, vpair)
                delta = _row_sums(p * dp, split=False)
                ds = (p * (dp - _both(delta))).astype(BF16)
                dqs.append(_nn(ds, kpair))
                dk = dk + _tn(ds, qs)
                dv = dv + _tn(p.astype(BF16), dos)
                psd = ps * delta
                for g in range(HEAD_PAIRS):
                    part = jnp.sum(psd[g * ATT_BLK:(g + 1) * ATT_BLK], axis=0, keepdims=True)
                    dsink = dsink - jnp.where(lane == kh * 4 + g, part, 0.0)
            for g in range(HEAD_PAIRS):
                dq = _unstacked(dqs[0], dqs[1], g) * 0.125
                dq_ref[b, :, g * 128:(g + 1) * 128] = dq.astype(BF16)
                dbq[g] = dbq[g] + jnp.sum(dq, axis=0, keepdims=True)
            dkv = jnp.concatenate([dk[ATT_BLK:], dv[ATT_BLK:]], axis=1) + carry[b]
            dkv_ref[b] = dkv.astype(BF16)
            dbkv = dbkv + jnp.sum(dkv, axis=0, keepdims=True)
            carry[b] = jnp.concatenate([dk[:ATT_BLK], dv[:ATT_BLK]], axis=1)
        dsink_ref[...] += dsink
        dbq_ref[...] += jnp.concatenate(dbq, axis=1)
        dbkv_ref[...] += dbkv

    q, kvc, kvp = _attn_specs(bsz, lambda j: nb - 1 - j)
    z3 = z0.reshape(bsz, seq, z0.shape[1])
    d3 = dmix.reshape(bsz, seq, dmix.shape[1])
    dq, dkv, dsink, dbq, dbkv = pl.pallas_call(
        body, name=name, grid=(nb,),
        in_specs=[pl.BlockSpec(memory_space=pltpu.SMEM), q, kvc, kvp,
                  pl.BlockSpec((bsz, ATT_BLK, 512), lambda j: (0, nb - 1 - j, 0))],
        out_specs=[pl.BlockSpec((bsz, ATT_BLK, 512), lambda j: (0, nb - 1 - j, 0)),
                   pl.BlockSpec((bsz, ATT_BLK, 256), lambda j: (0, nb - 1 - j, 0)), _row(128), _row(512), _row(256)],
        out_shape=[jax.ShapeDtypeStruct((bsz, seq, 512), BF16), jax.ShapeDtypeStruct((bsz, seq, 256), BF16),
                   jax.ShapeDtypeStruct((1, 128), F32), jax.ShapeDtypeStruct((1, 512), F32),
                   jax.ShapeDtypeStruct((1, 256), F32)],
        scratch_shapes=[pltpu.VMEM((bsz, ATT_BLK, 256), F32)],
        compiler_params=_params("arbitrary"))(sinks, z3, z3, z3, d3)
    return dq.reshape(t, 512), dkv.reshape(t, 256), dsink, dbq, dbkv


def _seq_specs(ts, nt, t, width, col):
    per = ts // HALO
    cur = pl.BlockSpec((ts, width), lambda b, i: (b * nt + i, col))
    prev = pl.BlockSpec((HALO, width), lambda b, i: (jnp.maximum((b * nt + i) * per - 1, 0), col))
    nxt = pl.BlockSpec((HALO, width), lambda b, i: (jnp.minimum((b * nt + i + 1) * per, t // HALO - 1), col))
    return prev, cur, nxt


SUB = 8
CONV_ROWS = 32


def _shifted_copies(src, sh, rows_first, rows_rest):
    for r in range(SUB):
        rows = rows_first if r == 0 else rows_rest
        sh[r, pl.ds(0, rows), :] = src[pl.ds(r, rows), :]


def _tap_sum(sh, w, offset, c0, rows):
    acc = None
    for k in range(CONV_K):
        o = offset(k)
        term = sh[o % SUB, pl.ds(c0 + o - o % SUB, rows), :] * w[k:k + 1, :]
        acc = term if acc is None else acc + term
    return acc


def _glu_rows(a_ref, g_ref, rows=slice(None)):
    return a_ref[rows, :].astype(F32) * jax.nn.sigmoid(g_ref[rows, :].astype(F32))


def _conv_fwd(z0, conv_w, conv_b, ln_g, ln_b, bsz, name):
    t = z0.shape[0]
    s = t // bsz
    ts = _seq_tile(s)
    nt = s // ts
    first = HALO - (CONV_K - 1)

    def body(ap_ref, ac_ref, gp_ref, gc_ref, w_ref, cb_ref, lg_ref, lb_ref, o_ref, y_ref, hbuf, sh):
        hbuf[0:HALO, :] = jnp.where(pl.program_id(1) > 0, _glu_rows(ap_ref, gp_ref), 0.0)
        hbuf[HALO:HALO + ts, :] = _glu_rows(ac_ref, gc_ref)
        _shifted_copies(hbuf, sh, ts + HALO, ts + HALO - SUB)
        w, cb, lg, lb = w_ref[...], cb_ref[...], lg_ref[...], lb_ref[...]
        for c0 in range(0, ts, CONV_ROWS):
            y = _tap_sum(sh, w, lambda k: first + k, c0, CONV_ROWS) + cb
            y_ref[c0:c0 + CONV_ROWS, :] = y
            o = _ln(y, lg, lb)[0]
            o_ref[c0:c0 + CONV_ROWS, :] = (o * jax.nn.sigmoid(o)).astype(BF16)

    ap, ac, _ = _seq_specs(ts, nt, t, 512, 1)
    gp, gc, _ = _seq_specs(ts, nt, t, 512, 2)
    tile = pl.BlockSpec((ts, 512), lambda b, i: (b * nt + i, 0))
    return pl.pallas_call(
        body, name=name, grid=(bsz, nt),
        in_specs=[ap, ac, gp, gc, _full((HALO, 512)), _row(512), _row(512), _row(512)],
        out_specs=[tile, tile],
        out_shape=[jax.ShapeDtypeStruct((t, 512), BF16), jax.ShapeDtypeStruct((t, 512), F32)],
        scratch_shapes=[pltpu.VMEM((HALO + ts, 512), F32), pltpu.VMEM((SUB, HALO + ts, 512), F32)],
        compiler_params=_params("parallel", "parallel"))(z0, z0, z0, z0, conv_w, conv_b, ln_g, ln_b)


def _conv_bwd(z0, y, dmix, conv_w, ln_g, ln_b, bsz, name):
    t = z0.shape[0]
    s = t // bsz
    ts = _seq_tile(s)
    nt = s // ts

    def body(ac_ref, gc_ref, yc_ref, yn_ref, dc_ref, dn_ref, w_ref, lg_ref, lb_ref,
             da_ref, dg_ref, dw_ref, dcb_ref, dlg_ref, dlb_ref, dba_ref, dbg_ref, hcur, dybuf, sh_dy):
        b, i = pl.program_id(0), pl.program_id(1)

        @pl.when((b == 0) & (i == 0))
        def _():
            for ref in (dw_ref, dcb_ref, dlg_ref, dlb_ref, dba_ref, dbg_ref):
                ref[...] = jnp.zeros_like(ref)

        w, lg, lb = w_ref[...], lg_ref[...], lb_ref[...]
        hcur[...] = _glu_rows(ac_ref, gc_ref)

        def d_conv_out(yv, dout):
            o, xhat, rstd = _ln(yv, lg, lb)
            sg_o = jax.nn.sigmoid(o)
            d_o = dout * sg_o * (1.0 + o * (1.0 - sg_o))
            return _ln_bwd(d_o, xhat, rstd, lg), d_o * xhat, d_o

        dlg = jnp.zeros((1, 512), F32)
        dlb = jnp.zeros((1, 512), F32)
        dcb = jnp.zeros((1, 512), F32)
        for c0 in range(0, ts, CONV_ROWS):
            rows = slice(c0, c0 + CONV_ROWS)
            dy, g_part, b_part = d_conv_out(yc_ref[rows, :], dc_ref[rows, :].astype(F32))
            dybuf[rows, :] = dy
            dlg = dlg + jnp.sum(g_part, axis=0, keepdims=True)
            dlb = dlb + jnp.sum(b_part, axis=0, keepdims=True)
            dcb = dcb + jnp.sum(dy, axis=0, keepdims=True)
        dn = jnp.where(i < nt - 1, dn_ref[...].astype(F32), 0.0)
        dybuf[ts:ts + HALO, :] = d_conv_out(yn_ref[...], dn)[0]
        dlg_ref[...] += dlg
        dlb_ref[...] += dlb
        dcb_ref[...] += dcb
        _shifted_copies(dybuf, sh_dy, ts + HALO - SUB, ts + HALO - SUB)

        for k in range(CONV_K):
            o = CONV_K - 1 - k
            prod = hcur[...] * sh_dy[o % SUB, pl.ds(o - o % SUB, ts), :]
            dw_ref[pl.ds(k, 1), :] += jnp.sum(prod, axis=0, keepdims=True)
        dba = jnp.zeros((1, 512), F32)
        dbg = jnp.zeros((1, 512), F32)
        for c0 in range(0, ts, CONV_ROWS):
            rows = slice(c0, c0 + CONV_ROWS)
            dh = _tap_sum(sh_dy, w, lambda k: CONV_K - 1 - k, c0, CONV_ROWS)
            a_c = ac_ref[rows, :].astype(F32)
            sg_c = jax.nn.sigmoid(gc_ref[rows, :].astype(F32))
            d_a = dh * sg_c
            d_g = dh * a_c * sg_c * (1.0 - sg_c)
            da_ref[rows, :] = d_a.astype(BF16)
            dg_ref[rows, :] = d_g.astype(BF16)
            dba = dba + jnp.sum(d_a, axis=0, keepdims=True)
            dbg = dbg + jnp.sum(d_g, axis=0, keepdims=True)
        dba_ref[...] += dba
        dbg_ref[...] += dbg

    _, ac, _ = _seq_specs(ts, nt, t, 512, 1)
    _, gc, _ = _seq_specs(ts, nt, t, 512, 2)
    _, yc, yn = _seq_specs(ts, nt, t, 512, 0)
    _, dc, dn = _seq_specs(ts, nt, t, 512, 1)
    tile = pl.BlockSpec((ts, 512), lambda b, i: (b * nt + i, 0))
    vec = jax.ShapeDtypeStruct((1, 512), F32)
    return pl.pallas_call(
        body, name=name, grid=(bsz, nt),
        in_specs=[ac, gc, yc, yn, dc, dn, _full((HALO, 512)), _row(512), _row(512)],
        out_specs=[tile, tile, _full((HALO, 512)), _row(512), _row(512), _row(512), _row(512), _row(512)],
        out_shape=[jax.ShapeDtypeStruct((t, 512), BF16), jax.ShapeDtypeStruct((t, 512), BF16),
                   jax.ShapeDtypeStruct((HALO, 512), F32), vec, vec, vec, vec, vec],
        scratch_shapes=[pltpu.VMEM((ts, 512), F32), pltpu.VMEM((ts + HALO, 512), F32),
                        pltpu.VMEM((SUB, HALO + ts, 512), F32)],
        compiler_params=_params("arbitrary", "arbitrary"))(z0, z0, y, y, dmix, dmix, conv_w, ln_g, ln_b)


def _pooled(pbuf, g, ts, tok):
    w = 2 << g
    cols = slice(128 * g, 128 * (g + 1))
    sm = pbuf[pl.ds(HALO, ts), cols]
    for d in range(1, w):
        sm = sm + pbuf[pl.ds(HALO - d, ts), cols]
    cnt = jnp.minimum(tok + 1, w).astype(F32)
    return sm / cnt - pbuf[pl.ds(HALO, ts), cols]


def _pool_fwd(z1, w_pool, scale, bsz, name):
    t = z1.shape[0]
    s = t // bsz
    ts = _seq_tile(s)
    nt = s // ts

    def body(zp_ref, zc_ref, wp_ref, sc_ref, o_ref, token, pbuf):
        token[...] = jnp.zeros_like(token)
        i = pl.program_id(1)
        pbuf[0:HALO, :] = jnp.where(i > 0, zp_ref[...].astype(F32), 0.0)
        pbuf[HALO:HALO + ts, :] = zc_ref[...].astype(F32)
        tok = i * ts + lax.broadcasted_iota(jnp.int32, (ts, 1), 0)
        for g in range(4):
            cols = slice(128 * g, 128 * (g + 1))
            pooled = _pooled(pbuf, g, ts, tok).astype(BF16)
            o_ref[:, cols] = (_nn(pooled, wp_ref[g].astype(BF16)) * sc_ref[:, cols]).astype(BF16)

    zp, zc, _ = _seq_specs(ts, nt, t, 512, 0)
    return pl.pallas_call(
        body, name=name, grid=(bsz, nt), in_specs=[zp, zc, _full((4, 128, 128)), _row(512)],
        out_specs=[pl.BlockSpec((ts, 512), lambda b, i: (b * nt + i, 0)), _full((8, 128))],
        out_shape=[jax.ShapeDtypeStruct((t, 512), BF16), jax.ShapeDtypeStruct((8, 128), F32)],
        scratch_shapes=[pltpu.VMEM((HALO + ts, 512), F32)],
        compiler_params=_params("arbitrary", "arbitrary"))(z1, z1, w_pool, scale)


def _pool_bwd(z1, dmix, w_pool, scale, bsz, name):
    t = z1.shape[0]
    s = t // bsz
    ts = _seq_tile(s)
    nt = s // ts
    rr = ts + HALO

    def body(zp_ref, zc_ref, dc_ref, dn_ref, wp_ref, sc_ref, dz_ref, dwp_ref, dsc_ref, pbuf, ebuf):
        b, i = pl.program_id(0), pl.program_id(1)

        @pl.when((b == 0) & (i == 0))
        def _():
            dwp_ref[...] = jnp.zeros_like(dwp_ref)
            dsc_ref[...] = jnp.zeros_like(dsc_ref)

        pbuf[0:HALO, :] = jnp.where(i > 0, zp_ref[...].astype(F32), 0.0)
        pbuf[HALO:HALO + ts, :] = zc_ref[...].astype(F32)
        dn = jnp.where(i < nt - 1, dn_ref[...].astype(F32), 0.0)
        dout = jnp.concatenate([dc_ref[...].astype(F32), dn], axis=0)
        tok = i * ts + lax.broadcasted_iota(jnp.int32, (ts, 1), 0)
        tok_r = i * ts + lax.broadcasted_iota(jnp.int32, (rr, 1), 0)
        for g in range(4):
            w = 2 << g
            cols = slice(128 * g, 128 * (g + 1))
            wg = wp_ref[g].astype(BF16)
            pooled = _pooled(pbuf, g, ts, tok).astype(BF16)
            dsc_ref[:, cols] += jnp.sum(dout[:ts, cols] * _nn(pooled, wg), axis=0, keepdims=True)
            dy = (dout[:, cols] * sc_ref[:, cols]).astype(BF16)
            dwp_ref[g] += _tn(pooled, dy[:ts])
            dpl = _nt(dy, wg)
            ebuf[...] = dpl / jnp.minimum(tok_r + 1, w).astype(F32)
            dz = ebuf[pl.ds(0, ts), :] - dpl[:ts]
            for d in range(1, w):
                dz = dz + ebuf[pl.ds(d, ts), :]
            dz_ref[:, cols] = dz.astype(BF16)

    zp, zc, _ = _seq_specs(ts, nt, t, 512, 0)
    _, dc, dn = _seq_specs(ts, nt, t, 512, 0)
    return pl.pallas_call(
        body, name=name, grid=(bsz, nt), in_specs=[zp, zc, dc, dn, _full((4, 128, 128)), _row(512)],
        out_specs=[pl.BlockSpec((ts, 512), lambda b, i: (b * nt + i, 0)), _full((4, 128, 128)), _row(512)],
        out_shape=[jax.ShapeDtypeStruct((t, 512), BF16), jax.ShapeDtypeStruct((4, 128, 128), F32),
                   jax.ShapeDtypeStruct((1, 512), F32)],
        scratch_shapes=[pltpu.VMEM((HALO + ts, 512), F32), pltpu.VMEM((rr, 128), F32)],
        compiler_params=_params("arbitrary", "arbitrary"))(z1, z1, dmix, dmix, w_pool, scale)


def _tril():
    r = lax.broadcasted_iota(jnp.int32, (SGU_CHUNK, SGU_CHUNK), 0)
    c = lax.broadcasted_iota(jnp.int32, (SGU_CHUNK, SGU_CHUNK), 1)
    return r >= c


def _sgu_fwd(z1, ln_g, ln_b, w_s, b_rows, name):
    t = z1.shape[0]
    ts = _tile(t)

    def body(zu_ref, zv_ref, lg_ref, lb_ref, ws_ref, bs_ref, o_ref):
        v = _gelu(zv_ref[...].astype(F32))[0]
        vb = _ln(v, lg_ref[...], lb_ref[...])[0].astype(BF16)
        tril = _tril()
        for g in range(4):
            cols = slice(128 * g, 128 * (g + 1))
            wg = jnp.where(tril, ws_ref[g], 0.0).astype(BF16)
            for c in range(ts // SGU_CHUNK):
                rows = slice(SGU_CHUNK * c, SGU_CHUNK * (c + 1))
                mixed = _nn(wg, vb[rows, cols]) + bs_ref[g]
                o_ref[rows, cols] = (_gelu(zu_ref[rows, cols].astype(F32))[0] * mixed).astype(BF16)

    return pl.pallas_call(
        body, name=name, grid=(t // ts,),
        in_specs=[pl.BlockSpec((ts, 512), lambda i: (i, 1)), pl.BlockSpec((ts, 512), lambda i: (i, 2)),
                  _row(512), _row(512), _full((4, 128, 128)), _full((4, 128, 128))],
        out_specs=pl.BlockSpec((ts, 512), lambda i: (i, 0)), out_shape=jax.ShapeDtypeStruct((t, 512), BF16),
        compiler_params=_params("parallel"))(z1, z1, ln_g, ln_b, w_s, b_rows)


def _sgu_bwd(z1, dmix, ln_g, ln_b, w_s, b_rows, name):
    t = z1.shape[0]
    ts = _tile(t)

    def body(zu_ref, zv_ref, d_ref, lg_ref, lb_ref, ws_ref, bs_ref,
             dzu_ref, dzv_ref, dws_ref, dbs_ref, dlg_ref, dlb_ref, dvbuf):
        @pl.when(pl.program_id(0) == 0)
        def _():
            for ref in (dws_ref, dbs_ref, dlg_ref, dlb_ref):
                ref[...] = jnp.zeros_like(ref)

        zv = zv_ref[...].astype(F32)
        v, thv = _gelu(zv)
        lg = lg_ref[...]
        vln, xhat, rstd = _ln(v, lg, lb_ref[...])
        vb = vln.astype(BF16)
        tril = _tril()
        for g in range(4):
            cols = slice(128 * g, 128 * (g + 1))
            wg = jnp.where(tril, ws_ref[g], 0.0).astype(BF16)
            dws = jnp.zeros((SGU_CHUNK, SGU_CHUNK), F32)
            dbs = jnp.zeros((1, SGU_CHUNK), F32)
            for c in range(ts // SGU_CHUNK):
                rows = slice(SGU_CHUNK * c, SGU_CHUNK * (c + 1))
                vbc = vb[rows, cols]
                mixed = _nn(wg, vbc) + bs_ref[g]
                zu = zu_ref[rows, cols].astype(F32)
                u, thu = _gelu(zu)
                dout = d_ref[rows, cols].astype(F32)
                dzu_ref[rows, cols] = (dout * mixed * _gelu_grad(zu, thu)).astype(BF16)
                dm = dout * u
                dmb = dm.astype(BF16)
                dws = dws + _nt(dmb, vbc)
                dbs = dbs + jnp.sum(dm.T, axis=0, keepdims=True)
                dvbuf[rows, cols] = _tn(wg, dmb)
            dws_ref[g] += jnp.where(tril, dws, 0.0)
            dbs_ref[pl.ds(g, 1), :] += dbs
        dvln = dvbuf[...]
        dlg_ref[...] += jnp.sum(dvln * xhat, axis=0, keepdims=True)
        dlb_ref[...] += jnp.sum(dvln, axis=0, keepdims=True)
        dzv_ref[...] = (_ln_bwd(dvln, xhat, rstd, lg) * _gelu_grad(zv, thv)).astype(BF16)

    tile = pl.BlockSpec((ts, 512), lambda i: (i, 0))
    vec = jax.ShapeDtypeStruct((1, 512), F32)
    return pl.pallas_call(
        body, name=name, grid=(t // ts,),
        in_specs=[pl.BlockSpec((ts, 512), lambda i: (i, 1)), pl.BlockSpec((ts, 512), lambda i: (i, 2)),
                  pl.BlockSpec((ts, 512), lambda i: (i, 1)), _row(512), _row(512), _full((4, 128, 128)),
                  _full((4, 128, 128))],
        out_specs=[tile, tile, _full((4, 128, 128)), _full((4, 128)), _row(512), _row(512)],
        out_shape=[jax.ShapeDtypeStruct((t, 512), BF16), jax.ShapeDtypeStruct((t, 512), BF16),
                   jax.ShapeDtypeStruct((4, 128, 128), F32), jax.ShapeDtypeStruct((4, 128), F32), vec, vec],
        scratch_shapes=[pltpu.VMEM((ts, 512), F32)],
        compiler_params=_params("arbitrary"))(z1, z1, dmix, ln_g, ln_b, w_s, b_rows)


def _row_tile(r):
    for cand in (512, 352, 256, 192, 128, 64, 32, 16, 8):
        if r % cand == 0:
            return cand
    return r


def _sum_slabs(a, name):
    k, r, c = a.shape
    tr = _row_tile(r)

    def body(*refs):
        acc = refs[0][...].astype(F32)
        for ref in refs[1:-1]:
            acc = acc + ref[...].astype(F32)
        refs[-1][...] = acc

    in_specs = [pl.BlockSpec((None, tr, c), functools.partial(lambda i, s: (s, i, 0), s=s)) for s in range(k)]
    return pl.pallas_call(
        body, name=name, grid=(r // tr,), in_specs=in_specs, out_specs=pl.BlockSpec((tr, c), lambda i: (i, 0)),
        out_shape=jax.ShapeDtypeStruct((r, c), F32), compiler_params=_params("parallel"))(*([a] * k))


def _adamw_math(w, g, m, v):
    mn = ADAM_B1 * m + (1.0 - ADAM_B1) * g
    vn = ADAM_B2 * v + (1.0 - ADAM_B2) * (g * g)
    m_hat = mn / (1.0 - ADAM_B1 ** ADAM_STEP)
    v_hat = vn / (1.0 - ADAM_B2 ** ADAM_STEP)
    return -ADAM_LR * (m_hat / (jnp.sqrt(v_hat) + ADAM_EPS) + ADAM_WD * w), mn, vn


def _reduce_adamw(landing, w, m, v, name, layer=None, into=None):
    k, r, c = landing.shape
    tr = _row_tile(r)
    n_into = 0 if into is None else 4

    def body(*refs):
        slabs, (w_ref, m_ref, v_ref) = refs[:k], refs[k:k + 3]
        g_ref, d_ref, mo_ref, vo_ref = refs[k + 3 + n_into:]
        g = slabs[0][...].astype(F32)
        for ref in slabs[1:]:
            g = g + ref[...].astype(F32)
        g_ref[...] = g
        d_ref[...], mo_ref[...], vo_ref[...] = _adamw_math(w_ref[...], g, m_ref[...], v_ref[...])

    if layer is None:
        spec = pl.BlockSpec((tr, c), lambda i: (i, 0))
    else:
        spec = pl.BlockSpec((None, tr, c), lambda i: (layer, i, 0))
    in_specs = [pl.BlockSpec((None, tr, c), functools.partial(lambda i, s: (s, i, 0), s=s)) for s in range(k)]
    in_specs += [spec] * 3 + [ANY] * n_into
    shape = jax.ShapeDtypeStruct(w.shape, F32)
    return pl.pallas_call(
        body, name=name, grid=(r // tr,), in_specs=in_specs, out_specs=[spec] * 4, out_shape=[shape] * 4,
        input_output_aliases={k + 3 + j: j for j in range(n_into)},
        compiler_params=_params("parallel"))(*([landing] * k), w, m, v, *(into or ()))


def _rows_needed(shape):
    return shape[0] * -(-shape[1] // D)


def _as_rows(a):
    r, c = a.shape
    n = -(-c // D)
    assert r == 1 or n == 1
    return jnp.pad(a, ((0, 0), (0, n * D - c))).reshape(r * n, D)


def _adamw_replicated(g_rows, w, m, v, name):
    n = len(REP_2D)

    def body(*refs):
        g_ref, w_refs, m_refs, v_refs, outs = refs[0], refs[1:1 + n], refs[1 + n:1 + 2 * n], refs[1 + 2 * n:1 + 3 * n], refs[1 + 3 * n:]
        r0 = 0
        for k, (_, (r, c)) in enumerate(REP_2D):
            pieces = [g_ref[r0 + j * r:r0 + j * r + r, 0:min(D, c - j * D)] for j in range(-(-c // D))]
            g = pieces[0] if len(pieces) == 1 else jnp.concatenate(pieces, axis=1)
            outs[4 * k][...] = g
            outs[4 * k + 1][...], outs[4 * k + 2][...], outs[4 * k + 3][...] = _adamw_math(
                w_refs[k][...], g, m_refs[k][...], v_refs[k][...])
            r0 += _rows_needed((r, c))

    shapes = [s for _, s in REP_2D]
    return pl.pallas_call(
        body, name=name, in_specs=[_full(g_rows.shape)] + [_full(s) for s in shapes] * 3,
        out_specs=[_full(s) for s in shapes for _ in range(4)],
        out_shape=[jax.ShapeDtypeStruct(s, F32) for s in shapes for _ in range(4)],
        grid=(1,), compiler_params=_params("arbitrary"))(g_rows, *w, *m, *v)


def _adamw(w, g, m, v, name):
    r, c = w.shape
    tr = _row_tile(r)

    def body(w_ref, g_ref, m_ref, v_ref, d_ref, mo_ref, vo_ref):
        d_ref[...], mo_ref[...], vo_ref[...] = _adamw_math(w_ref[...], g_ref[...], m_ref[...], v_ref[...])

    spec = pl.BlockSpec((tr, c), lambda i: (i, 0))
    shape = jax.ShapeDtypeStruct((r, c), F32)
    return pl.pallas_call(
        body, name=name, grid=(r // tr,), in_specs=[spec] * 4, out_specs=[spec] * 3, out_shape=[shape] * 3,
        compiler_params=_params("parallel"))(w, g, m, v)


ANY = pl.BlockSpec(memory_space=pl.ANY)


def _all_gather(block, name):
    r, c_dim = block.shape

    def body(x_ref, out_ref, token, send_sems, recv_sems, local_sem):
        token[...] = jnp.zeros_like(token)
        x, y, c = lax.axis_index("x"), lax.axis_index("y"), lax.axis_index("c")
        me, sibling = (x, y, c), (x, y, 1 - c)
        chips = [(1 - x, y), (x, 1 - y), (1 - x, 1 - y)]

        def rows(px, py, pc):
            return out_ref.at[4 * px + 2 * py + pc]

        def copy(k, blk, to, src=None):
            return pltpu.make_async_remote_copy(
                src_ref=rows(*blk) if src is None else src, dst_ref=rows(*blk), send_sem=send_sems.at[k],
                recv_sem=recv_sems.at[k], device_id=to, device_id_type=MESH)

        mine = pltpu.make_async_copy(x_ref, rows(*me), local_sem)
        mine.start()
        first = [copy(0, me, sibling, src=x_ref)]
        first += [copy(1 + j, me, (*chip, c), src=x_ref) for j, chip in enumerate(chips)]
        for cp in first:
            cp.start()
        passed = [copy(4 + j, (*chip, c), sibling) for j, chip in enumerate(chips)]
        for j, chip in enumerate(chips):
            copy(1 + j, (*chip, c), me).wait_recv()
            passed[j].start()
        copy(0, sibling, me).wait_recv()
        for j, chip in enumerate(chips):
            copy(4 + j, (*chip, 1 - c), me).wait_recv()
        for cp in first + passed:
            cp.wait_send()
        mine.wait()

    return pl.pallas_call(
        body, name=name, in_specs=[ANY], out_specs=[ANY, pl.BlockSpec(memory_space=pltpu.VMEM)],
        out_shape=[jax.ShapeDtypeStruct((N_DEV, r, c_dim), block.dtype), jax.ShapeDtypeStruct((8, 128), F32)],
        scratch_shapes=[pltpu.SemaphoreType.DMA((7,)), pltpu.SemaphoreType.DMA((7,)), pltpu.SemaphoreType.DMA],
    )(block)


HBM = pl.BlockSpec(memory_space=pltpu.HBM)
SEM = pl.BlockSpec(memory_space=pltpu.SEMAPHORE)
EFFECT = pltpu.SideEffectType.DATAFLOW_SIDE_EFFECTING


def _exchange_copies(scatter, src_refs, land_refs, send_sems, recv_sems, local_sems):
    x, y, c = lax.axis_index("x"), lax.axis_index("y"), lax.axis_index("c")
    me = 4 * x + 2 * y + c
    sends, arrivals, locals_ = [], [], []
    for a, (src, land) in enumerate(zip(src_refs, land_refs)):
        def pick(idx, src=src):
            return src.at[idx] if scatter else src

        locals_.append(pltpu.make_async_copy(pick(me), land.at[me], local_sems.at[a]))
        for r in range(1, N_DEV):
            px = 1 - x if r & 4 else x
            py = 1 - y if r & 2 else y
            pc = 1 - c if r & 1 else c
            peer, s = 4 * px + 2 * py + pc, 7 * a + r - 1
            sends.append(pltpu.make_async_remote_copy(
                src_ref=pick(peer), dst_ref=land.at[me], send_sem=send_sems.at[s], recv_sem=recv_sems.at[s],
                device_id=(px, py, pc), device_id_type=MESH))
            arrivals.append(pltpu.make_async_remote_copy(
                src_ref=pick(peer), dst_ref=land.at[peer], send_sem=send_sems.at[s], recv_sem=recv_sems.at[s],
                device_id=(px, py, pc), device_id_type=MESH))
    return sends, arrivals, locals_


def _exchange_start(srcs, scatter, name):
    n = len(srcs)
    lands = [lax.empty((N_DEV,) + s.shape[-2:], s.dtype) for s in srcs]

    def body(*refs):
        src_refs, land_refs = refs[:n], refs[n:2 * n]
        send_sems, recv_sems, local_sems = refs[2 * n:2 * n + 3]
        token = refs[-1]
        sends, _, locals_ = _exchange_copies(scatter, src_refs, land_refs, send_sems, recv_sems, local_sems)
        for cp in locals_ + sends:
            cp.start()
        token[...] = jnp.zeros_like(token)

    res = pl.pallas_call(
        body, name=name,
        out_shape=[pltpu.SemaphoreType.DMA((7 * n,)), pltpu.SemaphoreType.DMA((7 * n,)), pltpu.SemaphoreType.DMA((n,))]
        + [pltpu.HBM(a.shape, a.dtype) for a in list(srcs) + lands] + [jax.ShapeDtypeStruct((8, 128), F32)],
        in_specs=[HBM] * (2 * n), out_specs=[SEM] * 3 + [HBM] * (2 * n) + [pl.BlockSpec(memory_space=pltpu.VMEM)],
        input_output_aliases={i: 3 + i for i in range(2 * n)},
        compiler_params=pltpu.CompilerParams(has_side_effects=EFFECT),
    )(*[pltpu.with_memory_space_constraint(a, pltpu.HBM) for a in list(srcs) + lands])
    return (n, scatter, res[:3], res[3:3 + 2 * n]), res[-1]


def _exchange_wait(handle, after, name):
    n, scatter, sems, thru = handle

    def body(*refs):
        src_refs, land_refs = refs[:n], refs[n:2 * n]
        send_sems, recv_sems, local_sems = refs[2 * n:2 * n + 3]
        sends, arrivals, locals_ = _exchange_copies(scatter, src_refs, land_refs, send_sems, recv_sems, local_sems)
        for cp in arrivals:
            cp.wait_recv()
        for cp in sends:
            cp.wait_send()
        for cp in locals_:
            cp.wait()

    res = pl.pallas_call(
        body, name=name, out_shape=[pltpu.HBM(a.shape, a.dtype) for a in thru],
        in_specs=[HBM] * (2 * n) + [SEM] * 3 + [ANY], out_specs=[HBM] * (2 * n),
        input_output_aliases={i: i for i in range(2 * n)},
        compiler_params=pltpu.CompilerParams(has_side_effects=EFFECT),
    )(*thru, *sems, after)
    return res[n:]


def _behind(tokens, a):
    zero = sum(tok[0, 0] for tok in tokens)
    return jax.tree.map(lambda v: v + zero.astype(v.dtype), a)


def _perm_heads(a, perm, axis):
    idx = [slice(None)] * a.ndim
    parts = []
    for h in perm:
        idx[axis] = slice(64 * h, 64 * (h + 1))
        parts.append(a[tuple(idx)])
    idx[axis] = slice(512, None)
    if a.shape[axis] > 512:
        parts.append(a[tuple(idx)])
    return jnp.concatenate(parts, axis=axis)


Q_INV = tuple(int(i) for i in np.argsort(Q_PERM))


def _in0_to_kernel(a, axis):
    a = _perm_heads(a, Q_PERM, axis)
    idx = [slice(None)] * a.ndim

    def cut(lo, hi):
        idx[axis] = slice(lo, hi)
        return a[tuple(idx)]

    return jnp.concatenate([cut(0, 512), cut(768, 1792), cut(512, 768)], axis=axis)


def _in0_from_kernel(a, axis):
    idx = [slice(None)] * a.ndim

    def cut(lo, hi):
        idx[axis] = slice(lo, hi)
        return a[tuple(idx)]

    a = jnp.concatenate([cut(0, 512), cut(1536, 1792), cut(512, 1536)], axis=axis)
    return _perm_heads(a, Q_INV, axis)


def _f32_as_u16_rows(v, rows):
    bits = lax.bitcast_convert_type(v, jnp.uint16).reshape(-1)
    return jnp.pad(bits, (0, rows * D - bits.shape[0])).reshape(rows, D)


def _pad_rows(v, rows):
    v = v.reshape(-1)
    return jnp.pad(v, (0, rows * D - v.shape[0])).reshape(rows, D)


def kernel(x, mix_norm, a_w_in, a_b_in, a_sinks, a_conv_w, a_conv_b, a_cln_g, a_cln_b, a_w_out, c_w_in, c_w_pool, c_pool_scale, c_sln_g, c_sln_b, c_w_s, c_b_s, c_w_out, ffn_norm, ffn_w_gate, ffn_w_up, ffn_w_down, final_norm, loss_target, m_mix_norm, m_a_w_in, m_a_b_in, m_a_sinks, m_a_conv_w, m_a_conv_b, m_a_cln_g, m_a_cln_b, m_a_w_out, m_c_w_in, m_c_w_pool, m_c_pool_scale, m_c_sln_g, m_c_sln_b, m_c_w_s, m_c_b_s, m_c_w_out, m_ffn_norm, m_ffn_w_gate, m_ffn_w_up, m_ffn_w_down, m_final_norm, v_mix_norm, v_a_w_in, v_a_b_in, v_a_sinks, v_a_conv_w, v_a_conv_b, v_a_cln_g, v_a_cln_b, v_a_w_out, v_c_w_in, v_c_w_pool, v_c_pool_scale, v_c_sln_g, v_c_sln_b, v_c_w_s, v_c_b_s, v_c_w_out, v_ffn_norm, v_ffn_w_gate, v_ffn_w_up, v_ffn_w_down, v_final_norm):
    bsz, seq, _ = x.shape
    t = bsz * seq
    w_in = dict(mix_norm=mix_norm, a_w_in=a_w_in, a_b_in=a_b_in, a_sinks=a_sinks, a_conv_w=a_conv_w, a_conv_b=a_conv_b,
                a_cln_g=a_cln_g, a_cln_b=a_cln_b, a_w_out=a_w_out, c_w_in=c_w_in, c_w_pool=c_w_pool,
                c_pool_scale=c_pool_scale, c_sln_g=c_sln_g, c_sln_b=c_sln_b, c_w_s=c_w_s, c_b_s=c_b_s, c_w_out=c_w_out,
                ffn_norm=ffn_norm, ffn_w_gate=ffn_w_gate, ffn_w_up=ffn_w_up, ffn_w_down=ffn_w_down, final_norm=final_norm)
    m_in = dict(mix_norm=m_mix_norm, a_w_in=m_a_w_in, a_b_in=m_a_b_in, a_sinks=m_a_sinks, a_conv_w=m_a_conv_w,
                a_conv_b=m_a_conv_b, a_cln_g=m_a_cln_g, a_cln_b=m_a_cln_b, a_w_out=m_a_w_out, c_w_in=m_c_w_in,
                c_w_pool=m_c_w_pool, c_pool_scale=m_c_pool_scale, c_sln_g=m_c_sln_g, c_sln_b=m_c_sln_b, c_w_s=m_c_w_s,
                c_b_s=m_c_b_s, c_w_out=m_c_w_out, ffn_norm=m_ffn_norm, ffn_w_gate=m_ffn_w_gate, ffn_w_up=m_ffn_w_up,
                ffn_w_down=m_ffn_w_down, final_norm=m_final_norm)
    v_in = dict(mix_norm=v_mix_norm, a_w_in=v_a_w_in, a_b_in=v_a_b_in, a_sinks=v_a_sinks, a_conv_w=v_a_conv_w,
                a_conv_b=v_a_conv_b, a_cln_g=v_a_cln_g, a_cln_b=v_a_cln_b, a_w_out=v_a_w_out, c_w_in=v_c_w_in,
                c_w_pool=v_c_w_pool, c_pool_scale=v_c_pool_scale, c_sln_g=v_c_sln_g, c_sln_b=v_c_sln_b, c_w_s=v_c_w_s,
                c_b_s=v_c_b_s, c_w_out=v_c_w_out, ffn_norm=v_ffn_norm, ffn_w_gate=v_ffn_w_gate, ffn_w_up=v_ffn_w_up,
                ffn_w_down=v_ffn_w_down, final_norm=v_final_norm)

    small = jnp.concatenate([a_conv_w[0].reshape(-1), c_pool_scale[0], c_sln_g[0], c_sln_b[0]])
    first_bits = lax.bitcast_convert_type(a_w_in[0].T.astype(BF16), jnp.uint16)
    gathered, tok = _all_gather(jnp.concatenate([first_bits, _f32_as_u16_rows(small, W_MISC_ROWS)], axis=0), "gather_mixer0")

    def ffn_shards(l):
        return [ffn_w_gate[l].T.astype(BF16), ffn_w_up[l].T.astype(BF16), ffn_w_down[l].astype(BF16)]

    ffn0_h, tok = _exchange_start(_behind([tok], ffn_shards(0) + [a_w_out[0].astype(BF16)]), False, "gather_ffn0_start")
    mix1_h, tok = _exchange_start(_behind([tok], [c_w_in[0].T.astype(BF16), c_w_out[0].astype(BF16)]), False,
                                  "gather_mixer1_start")
    ffn1_h, tok = _exchange_start(_behind([tok], ffn_shards(1)), False, "gather_ffn1_start")

    a_in_full = lax.bitcast_convert_type(gathered[:, :224].reshape(IN0, D), BF16)
    small_all = lax.bitcast_convert_type(
        gathered[:, 224:].reshape(N_DEV, -1)[:, :2 * SMALL_SHARD].reshape(N_DEV, SMALL_SHARD, 2), F32)
    conv_w = small_all[:, :31 * 64].reshape(N_DEV, 31, 64).transpose(1, 0, 2).reshape(31, 512)
    conv_w = jnp.pad(conv_w, ((0, HALO - CONV_K), (0, 0)))
    pool_scale = small_all[:, 31 * 64:31 * 64 + 64].reshape(1, 512)
    sln_g = small_all[:, 31 * 64 + 64:31 * 64 + 128].reshape(1, 512)
    sln_b = small_all[:, 31 * 64 + 128:].reshape(1, 512)

    wt_in0 = _in0_to_kernel(a_in_full, 0)
    b_in0 = _in0_to_kernel(a_b_in, 1)
    b_rows = jnp.broadcast_to(c_b_s[0][:, :, None], (4, 128, 128))
    conv_b, cln_g, cln_b = a_conv_b, a_cln_g, a_cln_b

    h0 = x.reshape(t, D)
    target = loss_target.reshape(t, D)
    z0, hn0 = _norm_proj(h0, _behind([tok], mix_norm[0:1]), wt_in0, b_in0, "in_proj0")
    attn, tok = _attn_fwd(z0, a_sinks, bsz, "attn_fwd")
    conv, conv_y = _conv_fwd(z0, conv_w, conv_b, cln_g, _behind([tok], cln_b), bsz, "conv_fwd")
    wtg0, wtu0, wd0, a_out_full = (w.reshape(-1, D) for w in _exchange_wait(ffn0_h, conv, "gather_ffn0_wait"))
    w_out0 = _perm_heads(a_out_full, Q_PERM, 0)
    h1, hnf0, gate0a, up0a, part = _ffn_fwd_a(h0, attn, conv, w_out0, ffn_norm[0:1], wtg0, wtu0, wd0, "ffn_fwd0a")
    gate0b, up0b, h2 = _ffn_fwd_b(hnf0, part, wtg0, wtu0, wd0, "ffn_fwd0b")
    wt_in1, w_out1 = (w.reshape(-1, D) for w in _exchange_wait(mix1_h, h2, "gather_mixer1_wait"))
    z1, hn1 = _norm_proj(h2, mix_norm[1:2], wt_in1, None, "in_proj1")
    pool, tok = _pool_fwd(z1, c_w_pool[0], pool_scale, bsz, "pool_fwd")
    sgu = _sgu_fwd(z1, sln_g, _behind([tok], sln_b), c_w_s[0], b_rows, "sgu_fwd")
    wtg1, wtu1, wd1 = (w.reshape(D_FF, D) for w in _exchange_wait(ffn1_h, sgu, "gather_ffn1_wait"))
    h3, hnf1, gate1a, up1a, part = _ffn_fwd_a(h2, pool, sgu, w_out1, ffn_norm[1:2], wtg1, wtu1, wd1, "ffn_fwd1a")
    gate1b, up1b, dh4, d_final_norm, loss_part = _ffn_fwd_b(hnf1, part, wtg1, wtu1, wd1, "ffn_fwd1b",
                                                            head=(final_norm.reshape(1, D), target))

    def blocks(g):
        return g.reshape(N_DEV, g.shape[0] // N_DEV, D)

    *wide, part = _ffn_bwd_a(dh4, gate1a, up1a, wtg1, wtu1, wd1, "ffn_bwd1a")
    dh3, dmix1, dgate1, dup1, act1, d_fn1 = _ffn_bwd_b(dh4, h3, ffn_norm[1:2], gate1b, up1b, wtg1, wtu1, wd1, w_out1, part,
                                                       wide, "ffn_bwd1b")
    gw_ffn1 = [_mm_tn(dgate1, hnf1, "dw_gate1"), _mm_tn(dup1, hnf1, "dw_up1"), _mm_tn(act1, dh4, "dw_down1")]
    ffn1_g, tok = _exchange_start([blocks(g) for g in gw_ffn1], True, "scatter_ffn1_start")
    gw_c_out = _mm_tn_pieces([pool, sgu], dh3, "dw_out1")
    dzp, d_w_pool, d_pool_scale = _pool_bwd(z1, dmix1, c_w_pool[0], _behind([tok], pool_scale), bsz, "pool_bwd")
    dzu, dzv, d_w_s, d_b_s, d_sln_g, d_sln_b = _sgu_bwd(z1, dmix1, sln_g, sln_b, c_w_s[0], b_rows, "sgu_bwd")
    dh2, d_mn1 = _proj_bwd_norm([(dzp, 0), (dzu, 512), (dzv, 1024)], wt_in1, h2, dh3, mix_norm[1:2], BF16, "in_proj1_bwd")
    gw_c_in = _mm_tn_pieces([dzp, dzu, dzv], hn1, "dw_in1")
    mix1_g, tok = _exchange_start([blocks(gw_c_in), blocks(gw_c_out)], True, "scatter_mixer1_start")
    *wide, part = _ffn_bwd_a(dh2, gate0a, up0a, wtg0, wtu0, wd0, "ffn_bwd0a")
    dh1, dmix0, dgate0, dup0, act0, d_fn0 = _ffn_bwd_b(dh2, h1, _behind([tok], ffn_norm[0:1]), gate0b, up0b, wtg0, wtu0, wd0,
                                                       w_out0, part, wide, "ffn_bwd0b")
    gw_ffn0 = [_mm_tn(dgate0, hnf0, "dw_gate0"), _mm_tn(dup0, hnf0, "dw_up0"), _mm_tn(act0, dh2, "dw_down0")]
    ffn0_g, tok = _exchange_start([blocks(g) for g in gw_ffn0], True, "scatter_ffn0_start")
    gw_a_out = _perm_heads(_mm_tn_pieces([attn, conv], dh1, "dw_out0"), Q_INV, 0)
    dq, dkv, d_sink_row, d_bq, d_bkv = _attn_bwd(z0, dmix0, _behind([tok], a_sinks), bsz, "attn_bwd")
    dca, dcg, d_conv_w, d_conv_b, d_cln_g, d_cln_b, d_ba, d_bg = _conv_bwd(z0, conv_y, dmix0, conv_w, cln_g, cln_b, bsz, "conv_bwd")
    gw_a_in = _in0_from_kernel(_mm_tn_pieces([dq, dca, dcg, dkv], hn0, "dw_in0"), 0)
    mix0_g, tok = _exchange_start([blocks(gw_a_in), blocks(gw_a_out)], True, "scatter_mixer0_start")
    dx, d_mn0 = _proj_bwd_norm([(dq, 0), (dca, 512), (dcg, 1024), (dkv, 1536)], wt_in0, h0, dh1,
                               _behind([tok], mix_norm[0:1]), F32, "in_proj0_bwd")
    d_b_in = _in0_from_kernel(jnp.concatenate([d_bq, d_ba, d_bg, d_bkv], axis=1), 1)

    rep = dict(mix_norm=jnp.concatenate([d_mn0, d_mn1], axis=0), a_b_in=d_b_in, a_sinks=d_sink_row[:, :8],
               a_conv_b=d_conv_b, a_cln_g=d_cln_g, a_cln_b=d_cln_b, c_w_pool=d_w_pool.reshape(64, D),
               c_w_s=d_w_s.reshape(64, D), c_b_s=d_b_s, ffn_norm=jnp.concatenate([d_fn0, d_fn1], axis=0),
               final_norm=d_final_norm)
    rep_rows = jnp.concatenate([_as_rows(rep[nm]) for nm, _ in REP_2D] + [_as_rows(loss_part)], axis=0)
    rep_flat = jnp.pad(rep_rows, ((0, N_DEV * REP_ROWS - rep_rows.shape[0]), (0, 0))).reshape(N_DEV, REP_ROWS, D)
    small_g = jnp.concatenate([
        d_conv_w[:CONV_K].reshape(31, N_DEV, 64).transpose(1, 0, 2).reshape(N_DEV, 31 * 64),
        d_pool_scale.reshape(N_DEV, 64), d_sln_g.reshape(N_DEV, 64), d_sln_b.reshape(N_DEV, 64)], axis=1)
    small_g = jnp.pad(small_g, ((0, 0), (0, G_SMALL_ROWS * D - SMALL_SHARD))).reshape(N_DEV, G_SMALL_ROWS, D)
    tail_g, tok = _exchange_start([jnp.concatenate([small_g, rep_flat], axis=1)], True, "scatter_tail_start")

    names = list(w_in)
    g_out, delta, new_m, new_v = {}, {}, {}, {}
    column_sharded = ("a_w_in", "c_w_in", "ffn_w_gate", "ffn_w_up")

    def rows_of(a, nm):
        return jnp.swapaxes(a, 1, 2) if nm in column_sharded else a

    def reduce_adamw(nm, landing, layer, into=None):
        args = [rows_of(d[nm], nm) for d in (w_in, m_in, v_in)]
        if args[0].shape[0] == 1:
            args, layer = [a[0] for a in args], None
        return _reduce_adamw(landing, *args, "adamw_%s_%s" % (nm, layer), layer=layer, into=into)

    def keep(nm, res):
        res = [r if r.ndim == 3 else r[None] for r in res]
        g_out[nm], delta[nm], new_m[nm], new_v[nm] = (rows_of(r, nm) for r in res)

    ffn_names = ("ffn_w_gate", "ffn_w_up", "ffn_w_down")
    landed = _exchange_wait(ffn1_g, tok, "scatter_ffn1_wait")
    ffn_res = [reduce_adamw(nm, a, 1) for nm, a in zip(ffn_names, landed)]
    landed = _exchange_wait(mix1_g, ffn_res[-1][0], "scatter_mixer1_wait")
    for nm, a in zip(("c_w_in", "c_w_out"), landed):
        keep(nm, reduce_adamw(nm, a, 0))
    landed = _exchange_wait(ffn0_g, g_out["c_w_out"], "scatter_ffn0_wait")
    for nm, a, res in zip(ffn_names, landed, ffn_res):
        keep(nm, reduce_adamw(nm, a, 0, into=res))
    landed = _exchange_wait(mix0_g, g_out["ffn_w_down"], "scatter_mixer0_wait")
    for nm, a in zip(("a_w_in", "a_w_out"), landed):
        keep(nm, reduce_adamw(nm, a, 0))
    g_tail = _sum_slabs(_exchange_wait(tail_g, g_out["a_w_out"], "scatter_tail_wait")[0], "sum_tail")
    rep_all = _all_gather(g_tail[G_SMALL_ROWS:], "gather_replicated_grads")[0].reshape(N_DEV * REP_ROWS, D)
    loss = rep_all[sum(_rows_needed(s) for _, s in REP_2D), 0]
    res = _adamw_replicated(rep_all, *[[d[nm].reshape(s) for nm, s in REP_2D] for d in (w_in, m_in, v_in)], "adamw_replicated")
    for k, (nm, _) in enumerate(REP_2D):
        g_out[nm], delta[nm], new_m[nm], new_v[nm] = (r.reshape(w_in[nm].shape) for r in res[4 * k:4 * k + 4])
    small_r = g_tail[:G_SMALL_ROWS].reshape(-1)[:SMALL_SHARD]
    g_out.update(
        a_conv_w=small_r[:31 * 64].reshape(1, 31, 64), c_pool_scale=small_r[31 * 64:31 * 64 + 64].reshape(1, 64),
        c_sln_g=small_r[31 * 64 + 64:31 * 64 + 128].reshape(1, 64), c_sln_b=small_r[31 * 64 + 128:].reshape(1, 64))
    group = ("a_conv_w", "c_pool_scale", "c_sln_g", "c_sln_b")
    flat = [_pad_rows(jnp.concatenate([d[nm].reshape(-1) for nm in group]), G_SMALL_ROWS) for d in (w_in, g_out, m_in, v_in)]
    res = [r.reshape(-1) for r in _adamw(*flat, "adamw_small_sharded")]
    off = 0
    for nm in group:
        n = int(np.prod(w_in[nm].shape))
        delta[nm], new_m[nm], new_v[nm] = (r[off:off + n].reshape(w_in[nm].shape) for r in res)
        off += n

    grad_x = dx.reshape(bsz, seq, D)
    return (loss, grad_x, *[g_out[nm] for nm in names], *[delta[nm] for nm in names],
            *[new_m[nm] for nm in names], *[new_v[nm] for nm in names])
```

```python
import functools

import jax
import jax.numpy as jnp
import numpy as np
from jax import lax
from jax.experimental import pallas as pl
from jax.experimental.pallas import tpu as pltpu

F32 = jnp.float32
BF16 = jnp.bfloat16
MESH = pl.DeviceIdType.MESH

D = 1024
N_DEV = 8
EPS = 1e-5
HEAD_PAIRS = 4
ATT_BLK = 128
CONV_K = 31
HALO = 32
D_FF = 2816
IN0 = 1792
SGU_CHUNK = 128
GELU_C = 0.7978845608028654
GELU_A = 0.044715
ADAM_LR, ADAM_B1, ADAM_B2, ADAM_EPS, ADAM_WD, ADAM_STEP = 0.001, 0.9, 0.999, 1e-08, 0.01, 10
VMEM_LIMIT = 56 << 20

SMALL_SHARD = 31 * 64 + 3 * 64
W_MISC_ROWS = 16
G_MISC_ROWS = 32
G_SMALL_ROWS = 8
REP_ROWS = G_MISC_ROWS - G_SMALL_ROWS
REP_2D = (("c_w_pool", (64, 1024)), ("c_w_s", (64, 1024)), ("mix_norm", (2, 1024)), ("a_b_in", (1, 1792)), ("a_sinks", (1, 8)),
          ("a_conv_b", (1, 512)), ("a_cln_g", (1, 512)), ("a_cln_b", (1, 512)), ("c_b_s", (4, 128)), ("ffn_norm", (2, 1024)),
          ("final_norm", (1, 1024)))
Q_PERM = (0, 4, 1, 5, 2, 6, 3, 7)


def _params(*sem):
    return pltpu.CompilerParams(dimension_semantics=sem, vmem_limit_bytes=VMEM_LIMIT)


def _nn(a, b):
    return jnp.dot(a, b, preferred_element_type=F32)


def _nt(a, b):
    return lax.dot_general(a, b, (((1,), (1,)), ((), ())), preferred_element_type=F32)


def _tn(a, b):
    return lax.dot_general(a, b, (((0,), (0,)), ((), ())), preferred_element_type=F32)


def _tile(n, want=512):
    t = min(want, n)
    assert n % t == 0, (n, t)
    return t


def _seq_tile(s):
    return 512 if s >= 1024 else s // 2


def _rms(x, g):
    r = lax.rsqrt(jnp.mean(x * x, axis=-1, keepdims=True) + EPS)
    return x * r * g, r


def _rms_bwd(x, g, d_y):
    r = lax.rsqrt(jnp.mean(x * x, axis=-1, keepdims=True) + EPS)
    xr = x * r
    u = d_y * g
    d_x = r * (u - xr * jnp.mean(u * xr, axis=-1, keepdims=True))
    return d_x, jnp.sum(d_y * xr, axis=0, keepdims=True)


def _ln(y, g, b):
    mu = jnp.mean(y, axis=-1, keepdims=True)
    yc = y - mu
    rstd = lax.rsqrt(jnp.mean(yc * yc, axis=-1, keepdims=True) + EPS)
    xhat = yc * rstd
    return xhat * g + b, xhat, rstd


def _ln_bwd(d_o, xhat, rstd, g):
    dxh = d_o * g
    return rstd * (dxh - jnp.mean(dxh, axis=-1, keepdims=True) - xhat * jnp.mean(dxh * xhat, axis=-1, keepdims=True))


def _gelu(x):
    th = jnp.tanh(GELU_C * (x + GELU_A * x * x * x))
    return 0.5 * x * (1.0 + th), th


def _gelu_grad(x, th):
    return 0.5 * (1.0 + th) + 0.5 * x * (1.0 - th * th) * GELU_C * (1.0 + 3.0 * GELU_A * x * x)


def _row(c):
    return pl.BlockSpec((1, c), lambda *_: (0, 0))


def _full(shape):
    return pl.BlockSpec(shape, lambda *_: (0,) * len(shape))


def _norm_proj(h, g, wt, bias, name):
    t, n = h.shape[0], wt.shape[0]
    tm = _tile(t, 1024)
    has_bias = bias is not None

    def body(*refs):
        h_ref, g_ref, wt_ref = refs[:3]
        z_ref, hn_ref = refs[-2:]
        hn = _rms(h_ref[...].astype(F32), g_ref[...])[0].astype(BF16)
        hn_ref[...] = hn
        z = _nt(hn, wt_ref[...])
        if has_bias:
            z = z + refs[3][...]
        z_ref[...] = z.astype(BF16)

    in_specs = [pl.BlockSpec((tm, D), lambda i: (i, 0)), _row(D), _full((n, D))]
    args = [h, g, wt]
    if has_bias:
        in_specs.append(_row(n))
        args.append(bias)
    return pl.pallas_call(
        body, name=name, grid=(t // tm,), in_specs=in_specs,
        out_specs=[pl.BlockSpec((tm, n), lambda i: (i, 0)), pl.BlockSpec((tm, D), lambda i: (i, 0))],
        out_shape=[jax.ShapeDtypeStruct((t, n), BF16), jax.ShapeDtypeStruct((t, D), BF16)],
        compiler_params=_params("parallel"))(*args)


def _ff_pieces(tf, width=256):
    return [(c0, min(width, tf - c0)) for c0 in range(0, tf, width)]


FF_HALF = D_FF // 2


def _half_spec(part):
    return pl.BlockSpec((FF_HALF, D), lambda i: (part, 0))


def _ffn_half_fwd(hn, wtg_ref, wtu_ref, gate_ref, up_ref, act):
    for c0, cw in _ff_pieces(FF_HALF):
        cols = slice(c0, c0 + cw)
        gate = _nt(hn, wtg_ref[cols, :])
        up = _nt(hn, wtu_ref[cols, :])
        gate_ref[:, cols] = gate.astype(BF16)
        up_ref[:, cols] = up.astype(BF16)
        act[:, cols] = (gate * jax.nn.sigmoid(gate) * up).astype(BF16)


def _ffn_fwd_a(h_prev, a, b, w_out, g, wtg, wtu, wd, name):
    t = h_prev.shape[0]
    tm = _tile(t)

    def body(hp_ref, a_ref, b_ref, wa_ref, wb_ref, g_ref, wtg_ref, wtu_ref, wd_ref,
             h_ref, hn_ref, gate_ref, up_ref, part_ref, act):
        x = hp_ref[...].astype(F32) + _nn(a_ref[...], wa_ref[...]) + _nn(b_ref[...], wb_ref[...])
        h_ref[...] = x.astype(BF16)
        hn = _rms(x, g_ref[...])[0].astype(BF16)
        hn_ref[...] = hn
        _ffn_half_fwd(hn, wtg_ref, wtu_ref, gate_ref, up_ref, act)
        part_ref[...] = x + _nn(act[...], wd_ref[...])

    tok = pl.BlockSpec((tm, D), lambda i: (i, 0))
    half = pl.BlockSpec((tm, 512), lambda i: (i, 0))
    mid = pl.BlockSpec((tm, FF_HALF), lambda i: (i, 0))
    res = jax.ShapeDtypeStruct((t, D), BF16)
    mid_shape = jax.ShapeDtypeStruct((t, FF_HALF), BF16)
    return pl.pallas_call(
        body, name=name, grid=(t // tm,),
        in_specs=[tok, half, half, pl.BlockSpec((512, D), lambda i: (0, 0)), pl.BlockSpec((512, D), lambda i: (1, 0)),
                  _row(D), _half_spec(0), _half_spec(0), _half_spec(0)],
        out_specs=[tok, tok, mid, mid, tok], out_shape=[res, res, mid_shape, mid_shape, jax.ShapeDtypeStruct((t, D), F32)],
        scratch_shapes=[pltpu.VMEM((tm, FF_HALF), BF16)],
        compiler_params=_params("parallel"))(h_prev, a, b, w_out, w_out, g, wtg, wtu, wd)


def _ffn_fwd_b(hn, part, wtg, wtu, wd, name, head=None):
    t = hn.shape[0]
    tm = _tile(t, 1024 if head is None else 512)
    n_head = 0 if head is None else 2

    def body(*refs):
        hn_ref, part_ref, wtg_ref, wtu_ref, wd_ref = refs[:5]
        gate_ref, up_ref, o_ref = refs[5 + n_head:8 + n_head]
        act = refs[-1]
        _ffn_half_fwd(hn_ref[...], wtg_ref, wtu_ref, gate_ref, up_ref, act)
        x = part_ref[...] + _nn(act[...], wd_ref[...])
        if head is None:
            o_ref[...] = x.astype(BF16)
        else:
            fg_ref, t_ref = refs[5:7]
            dfg_ref, loss_ref = refs[10:12]

            @pl.when(pl.program_id(0) == 0)
            def _():
                dfg_ref[...] = jnp.zeros_like(dfg_ref)
                loss_ref[...] = jnp.zeros_like(loss_ref)

            gv = fg_ref[...]
            err = _rms(x, gv)[0] - t_ref[...]
            loss_ref[...] += 0.5 * jnp.sum(jnp.mean(err * err, axis=-1, keepdims=True), axis=0, keepdims=True)
            d_x, d_g = _rms_bwd(x, gv, err * (1.0 / D))
            o_ref[...] = d_x.astype(BF16)
            dfg_ref[...] += d_g

    tok = pl.BlockSpec((tm, D), lambda i: (i, 0))
    mid = pl.BlockSpec((tm, FF_HALF), lambda i: (i, 0))
    mid_shape = jax.ShapeDtypeStruct((t, FF_HALF), BF16)
    in_specs = [tok, tok, _half_spec(1), _half_spec(1), _half_spec(1)]
    out_specs = [mid, mid, tok]
    out_shape = [mid_shape, mid_shape, jax.ShapeDtypeStruct((t, D), BF16)]
    if head is not None:
        in_specs += [_row(D), tok]
        out_specs += [_row(D), _row(1)]
        out_shape += [jax.ShapeDtypeStruct((1, D), F32), jax.ShapeDtypeStruct((1, 1), F32)]
    return pl.pallas_call(
        body, name=name, grid=(t // tm,), in_specs=in_specs, out_specs=out_specs, out_shape=out_shape,
        scratch_shapes=[pltpu.VMEM((tm, FF_HALF), BF16)],
        compiler_params=_params("parallel" if head is None else "arbitrary"))(hn, part, wtg, wtu, wd, *(head or ()))


def _ffn_half_bwd(dh, gate_ref, up_ref, wd_ref, dgate_ref, dup_ref, act_ref):
    for c0, cw in _ff_pieces(FF_HALF):
        cols = slice(c0, c0 + cw)
        da = _nt(dh, wd_ref[cols, :])
        gt = gate_ref[:, cols].astype(F32)
        u = up_ref[:, cols].astype(F32)
        sg = jax.nn.sigmoid(gt)
        sil = gt * sg
        act_ref[:, cols] = (sil * u).astype(BF16)
        dup_ref[:, cols] = (da * sil).astype(BF16)
        dgate_ref[:, cols] = (da * u * sg * (1.0 + gt * (1.0 - sg))).astype(BF16)


def _ffn_bwd_a(dh, gate, up, wtg, wtu, wd, name):
    t = dh.shape[0]
    tm = _tile(t)

    def body(dh_ref, gate_ref, up_ref, wtg_ref, wtu_ref, wd_ref, dgate_ref, dup_ref, act_ref, part_ref):
        _ffn_half_bwd(dh_ref[...], gate_ref, up_ref, wd_ref, dgate_ref, dup_ref, act_ref)
        part_ref[...] = _nn(dgate_ref[...], wtg_ref[...]) + _nn(dup_ref[...], wtu_ref[...])

    tok = pl.BlockSpec((tm, D), lambda i: (i, 0))
    mid = pl.BlockSpec((tm, FF_HALF), lambda i: (i, 0))
    wide = jax.ShapeDtypeStruct((t, D_FF), BF16)
    return pl.pallas_call(
        body, name=name, grid=(t // tm,), in_specs=[tok, mid, mid, _half_spec(0), _half_spec(0), _half_spec(0)],
        out_specs=[mid, mid, mid, tok], out_shape=[wide, wide, wide, jax.ShapeDtypeStruct((t, D), F32)],
        compiler_params=_params("parallel"))(dh, gate, up, wtg, wtu, wd)


def _ffn_bwd_b(dh, h, g, gate, up, wtg, wtu, wd, w_out, part, wide, name):
    t = dh.shape[0]
    tm = _tile(t)

    def body(dh_ref, h_ref, g_ref, gate_ref, up_ref, wtg_ref, wtu_ref, wd_ref, wout_ref, part_ref, _a, _b, _c,
             dhin_ref, dmix_ref, dgate_ref, dup_ref, act_ref, dg_ref):
        @pl.when(pl.program_id(0) == 0)
        def _():
            dg_ref[...] = jnp.zeros_like(dg_ref)

        _ffn_half_bwd(dh_ref[...], gate_ref, up_ref, wd_ref, dgate_ref, dup_ref, act_ref)
        d_hn = part_ref[...] + _nn(dgate_ref[...], wtg_ref[...]) + _nn(dup_ref[...], wtu_ref[...])
        d_x, d_g = _rms_bwd(h_ref[...].astype(F32), g_ref[...], d_hn)
        dhin = (dh_ref[...].astype(F32) + d_x).astype(BF16)
        dhin_ref[...] = dhin
        dmix_ref[...] = _nt(dhin, wout_ref[...]).astype(BF16)
        dg_ref[...] += d_g

    tok = pl.BlockSpec((tm, D), lambda i: (i, 0))
    mid = pl.BlockSpec((tm, FF_HALF), lambda i: (i, 0))
    second = pl.BlockSpec((tm, FF_HALF), lambda i: (i, 1))
    wide_shape = jax.ShapeDtypeStruct((t, D_FF), BF16)
    res = jax.ShapeDtypeStruct((t, D), BF16)
    return pl.pallas_call(
        body, name=name, grid=(t // tm,),
        in_specs=[tok, tok, _row(D), mid, mid, _half_spec(1), _half_spec(1), _half_spec(1), _full((D, D)), tok, ANY, ANY, ANY],
        out_specs=[tok, tok, second, second, second, _row(D)],
        out_shape=[res, res, wide_shape, wide_shape, wide_shape, jax.ShapeDtypeStruct((1, D), F32)],
        input_output_aliases={10: 2, 11: 3, 12: 4},
        compiler_params=_params("arbitrary"))(dh, h, g, gate, up, wtg, wtu, wd, w_out, part, *wide)


def _proj_bwd_norm(pieces, wt, h, dh, g, dtype, name):
    t = h.shape[0]
    tm = _tile(t)
    n_p = len(pieces)

    def body(*refs):
        p_refs, w_refs = refs[:n_p], refs[n_p:2 * n_p]
        h_ref, dh_ref, g_ref, o_ref, dg_ref = refs[2 * n_p:]

        @pl.when(pl.program_id(0) == 0)
        def _():
            dg_ref[...] = jnp.zeros_like(dg_ref)

        d_hn = _nn(p_refs[0][...], w_refs[0][...])
        for p_ref, w_ref in zip(p_refs[1:], w_refs[1:]):
            d_hn = d_hn + _nn(p_ref[...], w_ref[...])
        d_x, d_g = _rms_bwd(h_ref[...].astype(F32), g_ref[...], d_hn)
        o_ref[...] = (dh_ref[...].astype(F32) + d_x).astype(dtype)
        dg_ref[...] += d_g

    tok = pl.BlockSpec((tm, D), lambda i: (i, 0))
    in_specs = [pl.BlockSpec((tm, a.shape[1]), lambda i: (i, 0)) for a, _ in pieces]
    for a, off in pieces:
        w = a.shape[1]
        assert off % w == 0
        in_specs.append(pl.BlockSpec((w, D), functools.partial(lambda i, blk: (blk, 0), blk=off // w)))
    in_specs += [tok, tok, _row(D)]
    return pl.pallas_call(
        body, name=name, grid=(t // tm,), in_specs=in_specs, out_specs=[tok, _row(D)],
        out_shape=[jax.ShapeDtypeStruct((t, D), dtype), jax.ShapeDtypeStruct((1, D), F32)],
        compiler_params=_params("arbitrary"))(*[a for a, _ in pieces], *([wt] * n_p), h, dh, g)


def _mm_tn(a, b, name):
    t, n = a.shape
    k = b.shape[1]
    tn = n if n <= 1024 else n // 2
    tt = _tile(t, 2048)
    nt = t // tt

    def body(a_ref, b_ref, o_ref, acc):
        s = pl.program_id(1)

        @pl.when(s == 0)
        def _():
            acc[...] = jnp.zeros_like(acc)

        acc[...] += _tn(a_ref[...], b_ref[...].astype(BF16))

        @pl.when(s == nt - 1)
        def _():
            o_ref[...] = acc[...].astype(BF16)

    return pl.pallas_call(
        body, name=name, grid=(n // tn, nt),
        in_specs=[pl.BlockSpec((tt, tn), lambda j, s: (s, j)), pl.BlockSpec((tt, k), lambda j, s: (s, 0))],
        out_specs=pl.BlockSpec((tn, k), lambda j, s: (j, 0)), out_shape=jax.ShapeDtypeStruct((n, k), BF16),
        scratch_shapes=[pltpu.VMEM((tn, k), F32)],
        compiler_params=_params("parallel", "arbitrary"))(a, b)


def _mm_tn_pieces(pieces, b, name):
    t, k = b.shape
    widths = [p.shape[1] for p in pieces]
    n, n_p = sum(widths), len(pieces)
    tt = _tile(t, 1024)
    nt = t // tt

    def body(*refs):
        b_ref, o_ref, acc = refs[n_p:]
        s = pl.program_id(0)

        @pl.when(s == 0)
        def _():
            acc[...] = jnp.zeros_like(acc)

        bb = b_ref[...].astype(BF16)
        off = 0
        for p_ref, w in zip(refs[:n_p], widths):
            acc[off:off + w, :] += _tn(p_ref[...], bb)
            off += w

        @pl.when(s == nt - 1)
        def _():
            o_ref[...] = acc[...].astype(BF16)

    return pl.pallas_call(
        body, name=name, grid=(nt,),
        in_specs=[pl.BlockSpec((tt, w), lambda s: (s, 0)) for w in widths] + [pl.BlockSpec((tt, k), lambda s: (s, 0))],
        out_specs=_full((n, k)), out_shape=jax.ShapeDtypeStruct((n, k), BF16),
        scratch_shapes=[pltpu.VMEM((n, k), F32)], compiler_params=_params("arbitrary"))(*pieces, b)


STACK = HEAD_PAIRS * ATT_BLK


def _attn_valid(first, rows):
    qi = lax.broadcasted_iota(jnp.int32, (rows, 2 * ATT_BLK), 0) % ATT_BLK
    r = lax.broadcasted_iota(jnp.int32, (rows, 2 * ATT_BLK), 1)
    dist = qi + ATT_BLK - r
    return (dist >= 0) & (dist < ATT_BLK) & ((r >= ATT_BLK) | jnp.logical_not(first))


def _stacked(ref, kh, scale):
    lo = lax.broadcasted_iota(jnp.int32, (ATT_BLK, 128), 1) < 64
    keep = lo if kh == 0 else ~lo
    parts = [jnp.where(keep, ref[:, g * 128:(g + 1) * 128] * scale, 0.0).astype(BF16) for g in range(HEAD_PAIRS)]
    return jnp.concatenate(parts, axis=0)


def _unstacked(a0, a1, g):
    lo = lax.broadcasted_iota(jnp.int32, (ATT_BLK, 128), 1) < 64
    rows = slice(g * ATT_BLK, (g + 1) * ATT_BLK)
    return jnp.where(lo, a0[rows], a1[rows])


def _sink_rows(s_ref, kh):
    return jnp.concatenate([jnp.full((ATT_BLK, 128), s_ref[0, kh * 4 + g], F32) for g in range(HEAD_PAIRS)], axis=0)


def _row_sums(a, split):
    hi = a.astype(BF16)
    ones = jnp.ones((2 * ATT_BLK, 128), BF16)
    if not split:
        return _nn(hi, ones)
    lo = (a - hi.astype(F32)).astype(BF16)
    return _nn(hi, ones) + _nn(lo, ones)


def _both(a):
    return jnp.concatenate([a, a], axis=1)


def _attn_probs(qs, kpair, sink, valid):
    s = jnp.where(valid, _nt(qs, kpair), -1e30)
    m = jnp.maximum(jnp.broadcast_to(jnp.max(s, axis=-1, keepdims=True), (s.shape[0], 128)), sink)
    p = jnp.exp(s - _both(m))
    es = jnp.exp(sink - m)
    inv = 1.0 / (_row_sums(p, split=True) + es)
    return p * _both(inv), es * inv


def _attn_probs_head(qm, kpair, sink, valid):
    s = jnp.where(valid, _nt(qm, kpair), -1e30)
    m = jnp.maximum(jnp.max(s, axis=-1, keepdims=True), sink)
    p = jnp.exp(s - m)
    return p * (1.0 / (jnp.sum(p, axis=-1, keepdims=True) + jnp.exp(sink - m)))


def _attn_specs(bsz, order):
    q = pl.BlockSpec((bsz, ATT_BLK, 512), lambda j: (0, order(j), 0))
    kvc = pl.BlockSpec((bsz, ATT_BLK, 256), lambda j: (0, order(j), 6))
    kvp = pl.BlockSpec((bsz, ATT_BLK, 256), lambda j: (0, jnp.maximum(order(j) - 1, 0), 6))
    return q, kvc, kvp


def _window_kv(kvc_ref, kvp_ref):
    kvc, kvp = kvc_ref[...], kvp_ref[...]
    kpair = jnp.concatenate([kvp[:, :128], kvc[:, :128]], axis=0)
    vpair = jnp.concatenate([kvp[:, 128:], kvc[:, 128:]], axis=0)
    return kpair, vpair


def _attn_fwd(z0, sinks, bsz, name):
    t = z0.shape[0]
    seq = t // bsz
    nb = seq // ATT_BLK

    def body(s_ref, q_ref, kvc_ref, kvp_ref, o_ref, token):
        token[...] = jnp.zeros_like(token)
        valid = _attn_valid(pl.program_id(0) == 0, ATT_BLK)
        lo = lax.broadcasted_iota(jnp.int32, (ATT_BLK, 128), 1) < 64
        for b in range(bsz):
            kpair, vpair = _window_kv(kvc_ref.at[b], kvp_ref.at[b])
            for g in range(HEAD_PAIRS):
                qs = q_ref[b, :, g * 128:(g + 1) * 128] * 0.125
                outs = []
                for kh in range(2):
                    qm = jnp.where(lo if kh == 0 else ~lo, qs, 0.0).astype(BF16)
                    p = _attn_probs_head(qm, kpair, s_ref[0, kh * 4 + g], valid)
                    outs.append(_nn(p.astype(BF16), vpair))
                o_ref[b, :, g * 128:(g + 1) * 128] = jnp.where(lo, outs[0], outs[1]).astype(BF16)

    q, kvc, kvp = _attn_specs(bsz, lambda j: j)
    z3 = z0.reshape(bsz, seq, z0.shape[1])
    out, token = pl.pallas_call(
        body, name=name, grid=(nb,),
        in_specs=[pl.BlockSpec(memory_space=pltpu.SMEM), q, kvc, kvp],
        out_specs=[pl.BlockSpec((bsz, ATT_BLK, 512), lambda j: (0, j, 0)), _full((8, 128))],
        out_shape=[jax.ShapeDtypeStruct((bsz, seq, 512), BF16), jax.ShapeDtypeStruct((8, 128), F32)],
        compiler_params=_params("arbitrary"))(sinks, z3, z3, z3)
    return out.reshape(t, 512), token


def _attn_bwd(z0, dmix, sinks, bsz, name):
    t = z0.shape[0]
    seq = t // bsz
    nb = seq // ATT_BLK

    def body(s_ref, q_ref, kvc_ref, kvp_ref, do_ref, dq_ref, dkv_ref, dsink_ref, dbq_ref, dbkv_ref, carry):
        j = pl.program_id(0)

        @pl.when(j == 0)
        def _():
            carry[...] = jnp.zeros_like(carry)
            dsink_ref[...] = jnp.zeros_like(dsink_ref)
            dbq_ref[...] = jnp.zeros_like(dbq_ref)
            dbkv_ref[...] = jnp.zeros_like(dbkv_ref)

        valid = _attn_valid(j == nb - 1, STACK)
        lane = lax.broadcasted_iota(jnp.int32, (1, 128), 1)
        dsink = jnp.zeros((1, 128), F32)
        dbq = [jnp.zeros((1, 128), F32)] * HEAD_PAIRS
        dbkv = jnp.zeros((1, 256), F32)
        for b in range(bsz):
            kpair, vpair = _window_kv(kvc_ref.at[b], kvp_ref.at[b])
            dk = jnp.zeros((2 * ATT_BLK, 128), F32)
            dv = jnp.zeros((2 * ATT_BLK, 128), F32)
            dqs = []
            for kh in range(2):
                qs = _stacked(q_ref.at[b], kh, 0.125)
                dos = _stacked(do_ref.at[b], kh, 1.0)
                p, ps = _attn_probs(qs, kpair, _sink_rows(s_ref, kh), valid)
                dp = _nt(dos, vpair)
                delta = _row_sums(p * dp, split=False)
                ds = (p * (dp - _both(delta))).astype(BF16)
                dqs.append(_nn(ds, kpair))
                dk = dk + _tn(ds, qs)
                dv = dv + _tn(p.astype(BF16), dos)
                psd = ps * delta
                for g in range(HEAD_PAIRS):
                    part = jnp.sum(psd[g * ATT_BLK:(g + 1) * ATT_BLK], axis=0, keepdims=True)
                    dsink = dsink - jnp.where(lane == kh * 4 + g, part, 0.0)
            for g in range(HEAD_PAIRS):
                dq = _unstacked(dqs[0], dqs[1], g) * 0.125
                dq_ref[b, :, g * 128:(g + 1) * 128] = dq.astype(BF16)
                dbq[g] = dbq[g] + jnp.sum(dq, axis=0, keepdims=True)
            dkv = jnp.concatenate([dk[ATT_BLK:], dv[ATT_BLK:]], axis=1) + carry[b]
            dkv_ref[b] = dkv.astype(BF16)
            dbkv = dbkv + jnp.sum(dkv, axis=0, keepdims=True)
            carry[b] = jnp.concatenate([dk[:ATT_BLK], dv[:ATT_BLK]], axis=1)
        dsink_ref[...] += dsink
        dbq_ref[...] += jnp.concatenate(dbq, axis=1)
        dbkv_ref[...] += dbkv

    q, kvc, kvp = _attn_specs(bsz, lambda j: nb - 1 - j)
    z3 = z0.reshape(bsz, seq, z0.shape[1])
    d3 = dmix.reshape(bsz, seq, dmix.shape[1])
    dq, dkv, dsink, dbq, dbkv = pl.pallas_call(
        body, name=name, grid=(nb,),
        in_specs=[pl.BlockSpec(memory_space=pltpu.SMEM), q, kvc, kvp,
                  pl.BlockSpec((bsz, ATT_BLK, 512), lambda j: (0, nb - 1 - j, 0))],
        out_specs=[pl.BlockSpec((bsz, ATT_BLK, 512), lambda j: (0, nb - 1 - j, 0)),
                   pl.BlockSpec((bsz, ATT_BLK, 256), lambda j: (0, nb - 1 - j, 0)), _row(128), _row(512), _row(256)],
        out_shape=[jax.ShapeDtypeStruct((bsz, seq, 512), BF16), jax.ShapeDtypeStruct((bsz, seq, 256), BF16),
                   jax.ShapeDtypeStruct((1, 128), F32), jax.ShapeDtypeStruct((1, 512), F32),
                   jax.ShapeDtypeStruct((1, 256), F32)],
        scratch_shapes=[pltpu.VMEM((bsz, ATT_BLK, 256), F32)],
        compiler_params=_params("arbitrary"))(sinks, z3, z3, z3, d3)
    return dq.reshape(t, 512), dkv.reshape(t, 256), dsink, dbq, dbkv


def _seq_specs(ts, nt, t, width, col):
    per = ts // HALO
    cur = pl.BlockSpec((ts, width), lambda b, i: (b * nt + i, col))
    prev = pl.BlockSpec((HALO, width), lambda b, i: (jnp.maximum((b * nt + i) * per - 1, 0), col))
    nxt = pl.BlockSpec((HALO, width), lambda b, i: (jnp.minimum((b * nt + i + 1) * per, t // HALO - 1), col))
    return prev, cur, nxt


SUB = 8
CONV_ROWS = 32


def _shifted_copies(src, sh, rows_first, rows_rest):
    for r in range(SUB):
        rows = rows_first if r == 0 else rows_rest
        sh[r, pl.ds(0, rows), :] = src[pl.ds(r, rows), :]


def _tap_sum(sh, w, offset, c0, rows):
    acc = None
    for k in range(CONV_K):
        o = offset(k)
        term = sh[o % SUB, pl.ds(c0 + o - o % SUB, rows), :] * w[k:k + 1, :]
        acc = term if acc is None else acc + term
    return acc


def _glu_rows(a_ref, g_ref, rows=slice(None)):
    return a_ref[rows, :].astype(F32) * jax.nn.sigmoid(g_ref[rows, :].astype(F32))


def _conv_fwd(z0, conv_w, conv_b, ln_g, ln_b, bsz, name):
    t = z0.shape[0]
    s = t // bsz
    ts = _seq_tile(s)
    nt = s // ts
    first = HALO - (CONV_K - 1)

    def body(ap_ref, ac_ref, gp_ref, gc_ref, w_ref, cb_ref, lg_ref, lb_ref, o_ref, y_ref, hbuf, sh):
        hbuf[0:HALO, :] = jnp.where(pl.program_id(1) > 0, _glu_rows(ap_ref, gp_ref), 0.0)
        hbuf[HALO:HALO + ts, :] = _glu_rows(ac_ref, gc_ref)
        _shifted_copies(hbuf, sh, ts + HALO, ts + HALO - SUB)
        w, cb, lg, lb = w_ref[...], cb_ref[...], lg_ref[...], lb_ref[...]
        for c0 in range(0, ts, CONV_ROWS):
            y = _tap_sum(sh, w, lambda k: first + k, c0, CONV_ROWS) + cb
            y_ref[c0:c0 + CONV_ROWS, :] = y
            o = _ln(y, lg, lb)[0]
            o_ref[c0:c0 + CONV_ROWS, :] = (o * jax.nn.sigmoid(o)).astype(BF16)

    ap, ac, _ = _seq_specs(ts, nt, t, 512, 1)
    gp, gc, _ = _seq_specs(ts, nt, t, 512, 2)
    tile = pl.BlockSpec((ts, 512), lambda b, i: (b * nt + i, 0))
    return pl.pallas_call(
        body, name=name, grid=(bsz, nt),
        in_specs=[ap, ac, gp, gc, _full((HALO, 512)), _row(512), _row(512), _row(512)],
        out_specs=[tile, tile],
        out_shape=[jax.ShapeDtypeStruct((t, 512), BF16), jax.ShapeDtypeStruct((t, 512), F32)],
        scratch_shapes=[pltpu.VMEM((HALO + ts, 512), F32), pltpu.VMEM((SUB, HALO + ts, 512), F32)],
        compiler_params=_params("parallel", "parallel"))(z0, z0, z0, z0, conv_w, conv_b, ln_g, ln_b)


def _conv_bwd(z0, y, dmix, conv_w, ln_g, ln_b, bsz, name):
    t = z0.shape[0]
    s = t // bsz
    ts = _seq_tile(s)
    nt = s // ts

    def body(ac_ref, gc_ref, yc_ref, yn_ref, dc_ref, dn_ref, w_ref, lg_ref, lb_ref,
             da_ref, dg_ref, dw_ref, dcb_ref, dlg_ref, dlb_ref, dba_ref, dbg_ref, hcur, dybuf, sh_dy):
        b, i = pl.program_id(0), pl.program_id(1)

        @pl.when((b == 0) & (i == 0))
        def _():
            for ref in (dw_ref, dcb_ref, dlg_ref, dlb_ref, dba_ref, dbg_ref):
                ref[...] = jnp.zeros_like(ref)

        w, lg, lb = w_ref[...], lg_ref[...], lb_ref[...]
        hcur[...] = _glu_rows(ac_ref, gc_ref)

        def d_conv_out(yv, dout):
            o, xhat, rstd = _ln(yv, lg, lb)
            sg_o = jax.nn.sigmoid(o)
            d_o = dout * sg_o * (1.0 + o * (1.0 - sg_o))
            return _ln_bwd(d_o, xhat, rstd, lg), d_o * xhat, d_o

        dlg = jnp.zeros((1, 512), F32)
        dlb = jnp.zeros((1, 512), F32)
        dcb = jnp.zeros((1, 512), F32)
        for c0 in range(0, ts, CONV_ROWS):
            rows = slice(c0, c0 + CONV_ROWS)
            dy, g_part, b_part = d_conv_out(yc_ref[rows, :], dc_ref[rows, :].astype(F32))
            dybuf[rows, :] = dy
            dlg = dlg + jnp.sum(g_part, axis=0, keepdims=True)
            dlb = dlb + jnp.sum(b_part, axis=0, keepdims=True)
            dcb = dcb + jnp.sum(dy, axis=0, keepdims=True)
        dn = jnp.where(i < nt - 1, dn_ref[...].astype(F32), 0.0)
        dybuf[ts:ts + HALO, :] = d_conv_out(yn_ref[...], dn)[0]
        dlg_ref[...] += dlg
        dlb_ref[...] += dlb
        dcb_ref[...] += dcb
        _shifted_copies(dybuf, sh_dy, ts + HALO - SUB, ts + HALO - SUB)

        for k in range(CONV_K):
            o = CONV_K - 1 - k
            prod = hcur[...] * sh_dy[o % SUB, pl.ds(o - o % SUB, ts), :]
            dw_ref[pl.ds(k, 1), :] += jnp.sum(prod, axis=0, keepdims=True)
        dba = jnp.zeros((1, 512), F32)
        dbg = jnp.zeros((1, 512), F32)
        for c0 in range(0, ts, CONV_ROWS):
            rows = slice(c0, c0 + CONV_ROWS)
            dh = _tap_sum(sh_dy, w, lambda k: CONV_K - 1 - k, c0, CONV_ROWS)
            a_c = ac_ref[rows, :].astype(F32)
            sg_c = jax.nn.sigmoid(gc_ref[rows, :].astype(F32))
            d_a = dh * sg_c
            d_g = dh * a_c * sg_c * (1.0 - sg_c)
            da_ref[rows, :] = d_a.astype(BF16)
            dg_ref[rows, :] = d_g.astype(BF16)
            dba = dba + jnp.sum(d_a, axis=0, keepdims=True)
            dbg = dbg + jnp.sum(d_g, axis=0, keepdims=True)
        dba_ref[...] += dba
        dbg_ref[...] += dbg

    _, ac, _ = _seq_specs(ts, nt, t, 512, 1)
    _, gc, _ = _seq_specs(ts, nt, t, 512, 2)
    _, yc, yn = _seq_specs(ts, nt, t, 512, 0)
    _, dc, dn = _seq_specs(ts, nt, t, 512, 1)
    tile = pl.BlockSpec((ts, 512), lambda b, i: (b * nt + i, 0))
    vec = jax.ShapeDtypeStruct((1, 512), F32)
    return pl.pallas_call(
        body, name=name, grid=(bsz, nt),
        in_specs=[ac, gc, yc, yn, dc, dn, _full((HALO, 512)), _row(512), _row(512)],
        out_specs=[tile, tile, _full((HALO, 512)), _row(512), _row(512), _row(512), _row(512), _row(512)],
        out_shape=[jax.ShapeDtypeStruct((t, 512), BF16), jax.ShapeDtypeStruct((t, 512), BF16),
                   jax.ShapeDtypeStruct((HALO, 512), F32), vec, vec, vec, vec, vec],
        scratch_shapes=[pltpu.VMEM((ts, 512), F32), pltpu.VMEM((ts + HALO, 512), F32),
                        pltpu.VMEM((SUB, HALO + ts, 512), F32)],
        compiler_params=_params("arbitrary", "arbitrary"))(z0, z0, y, y, dmix, dmix, conv_w, ln_g, ln_b)


def _pooled(pbuf, g, ts, tok):
    w = 2 << g
    cols = slice(128 * g, 128 * (g + 1))
    sm = pbuf[pl.ds(HALO, ts), cols]
    for d in range(1, w):
        sm = sm + pbuf[pl.ds(HALO - d, ts), cols]
    cnt = jnp.minimum(tok + 1, w).astype(F32)
    return sm / cnt - pbuf[pl.ds(HALO, ts), cols]


def _pool_fwd(z1, w_pool, scale, bsz, name):
    t = z1.shape[0]
    s = t // bsz
    ts = _seq_tile(s)
    nt = s // ts

    def body(zp_ref, zc_ref, wp_ref, sc_ref, o_ref, token, pbuf):
        token[...] = jnp.zeros_like(token)
        i = pl.program_id(1)
        pbuf[0:HALO, :] = jnp.where(i > 0, zp_ref[...].astype(F32), 0.0)
        pbuf[HALO:HALO + ts, :] = zc_ref[...].astype(F32)
        tok = i * ts + lax.broadcasted_iota(jnp.int32, (ts, 1), 0)
        for g in range(4):
            cols = slice(128 * g, 128 * (g + 1))
            pooled = _pooled(pbuf, g, ts, tok).astype(BF16)
            o_ref[:, cols] = (_nn(pooled, wp_ref[g].astype(BF16)) * sc_ref[:, cols]).astype(BF16)

    zp, zc, _ = _seq_specs(ts, nt, t, 512, 0)
    return pl.pallas_call(
        body, name=name, grid=(bsz, nt), in_specs=[zp, zc, _full((4, 128, 128)), _row(512)],
        out_specs=[pl.BlockSpec((ts, 512), lambda b, i: (b * nt + i, 0)), _full((8, 128))],
        out_shape=[jax.ShapeDtypeStruct((t, 512), BF16), jax.ShapeDtypeStruct((8, 128), F32)],
        scratch_shapes=[pltpu.VMEM((HALO + ts, 512), F32)],
        compiler_params=_params("arbitrary", "arbitrary"))(z1, z1, w_pool, scale)


def _pool_bwd(z1, dmix, w_pool, scale, bsz, name):
    t = z1.shape[0]
    s = t // bsz
    ts = _seq_tile(s)
    nt = s // ts
    rr = ts + HALO

    def body(zp_ref, zc_ref, dc_ref, dn_ref, wp_ref, sc_ref, dz_ref, dwp_ref, dsc_ref, pbuf, ebuf):
        b, i = pl.program_id(0), pl.program_id(1)

        @pl.when((b == 0) & (i == 0))
        def _():
            dwp_ref[...] = jnp.zeros_like(dwp_ref)
            dsc_ref[...] = jnp.zeros_like(dsc_ref)

        pbuf[0:HALO, :] = jnp.where(i > 0, zp_ref[...].astype(F32), 0.0)
        pbuf[HALO:HALO + ts, :] = zc_ref[...].astype(F32)
        dn = jnp.where(i < nt - 1, dn_ref[...].astype(F32), 0.0)
        dout = jnp.concatenate([dc_ref[...].astype(F32), dn], axis=0)
        tok = i * ts + lax.broadcasted_iota(jnp.int32, (ts, 1), 0)
        tok_r = i * ts + lax.broadcasted_iota(jnp.int32, (rr, 1), 0)
        for g in range(4):
            w = 2 << g
            cols = slice(128 * g, 128 * (g + 1))
            wg = wp_ref[g].astype(BF16)
            pooled = _pooled(pbuf, g, ts, tok).astype(BF16)
            dsc_ref[:, cols] += jnp.sum(dout[:ts, cols] * _nn(pooled, wg), axis=0, keepdims=True)
            dy = (dout[:, cols] * sc_ref[:, cols]).astype(BF16)
            dwp_ref[g] += _tn(pooled, dy[:ts])
            dpl = _nt(dy, wg)
            ebuf[...] = dpl / jnp.minimum(tok_r + 1, w).astype(F32)
            dz = ebuf[pl.ds(0, ts), :] - dpl[:ts]
            for d in range(1, w):
                dz = dz + ebuf[pl.ds(d, ts), :]
            dz_ref[:, cols] = dz.astype(BF16)

    zp, zc, _ = _seq_specs(ts, nt, t, 512, 0)
    _, dc, dn = _seq_specs(ts, nt, t, 512, 0)
    return pl.pallas_call(
        body, name=name, grid=(bsz, nt), in_specs=[zp, zc, dc, dn, _full((4, 128, 128)), _row(512)],
        out_specs=[pl.BlockSpec((ts, 512), lambda b, i: (b * nt + i, 0)), _full((4, 128, 128)), _row(512)],
        out_shape=[jax.ShapeDtypeStruct((t, 512), BF16), jax.ShapeDtypeStruct((4, 128, 128), F32),
                   jax.ShapeDtypeStruct((1, 512), F32)],
        scratch_shapes=[pltpu.VMEM((HALO + ts, 512), F32), pltpu.VMEM((rr, 128), F32)],
        compiler_params=_params("arbitrary", "arbitrary"))(z1, z1, dmix, dmix, w_pool, scale)


def _tril():
    r = lax.broadcasted_iota(jnp.int32, (SGU_CHUNK, SGU_CHUNK), 0)
    c = lax.broadcasted_iota(jnp.int32, (SGU_CHUNK, SGU_CHUNK), 1)
    return r >= c


def _sgu_fwd(z1, ln_g, ln_b, w_s, b_rows, name):
    t = z1.shape[0]
    ts = _tile(t)

    def body(zu_ref, zv_ref, lg_ref, lb_ref, ws_ref, bs_ref, o_ref):
        v = _gelu(zv_ref[...].astype(F32))[0]
        vb = _ln(v, lg_ref[...], lb_ref[...])[0].astype(BF16)
        tril = _tril()
        for g in range(4):
            cols = slice(128 * g, 128 * (g + 1))
            wg = jnp.where(tril, ws_ref[g], 0.0).astype(BF16)
            for c in range(ts // SGU_CHUNK):
                rows = slice(SGU_CHUNK * c, SGU_CHUNK * (c + 1))
                mixed = _nn(wg, vb[rows, cols]) + bs_ref[g]
                o_ref[rows, cols] = (_gelu(zu_ref[rows, cols].astype(F32))[0] * mixed).astype(BF16)

    return pl.pallas_call(
        body, name=name, grid=(t // ts,),
        in_specs=[pl.BlockSpec((ts, 512), lambda i: (i, 1)), pl.BlockSpec((ts, 512), lambda i: (i, 2)),
                  _row(512), _row(512), _full((4, 128, 128)), _full((4, 128, 128))],
        out_specs=pl.BlockSpec((ts, 512), lambda i: (i, 0)), out_shape=jax.ShapeDtypeStruct((t, 512), BF16),
        compiler_params=_params("parallel"))(z1, z1, ln_g, ln_b, w_s, b_rows)


def _sgu_bwd(z1, dmix, ln_g, ln_b, w_s, b_rows, name):
    t = z1.shape[0]
    ts = _tile(t)

    def body(zu_ref, zv_ref, d_ref, lg_ref, lb_ref, ws_ref, bs_ref,
             dzu_ref, dzv_ref, dws_ref, dbs_ref, dlg_ref, dlb_ref, dvbuf):
        @pl.when(pl.program_id(0) == 0)
        def _():
            for ref in (dws_ref, dbs_ref, dlg_ref, dlb_ref):
                ref[...] = jnp.zeros_like(ref)

        zv = zv_ref[...].astype(F32)
        v, thv = _gelu(zv)
        lg = lg_ref[...]
        vln, xhat, rstd = _ln(v, lg, lb_ref[...])
        vb = vln.astype(BF16)
        tril = _tril()
        for g in range(4):
            cols = slice(128 * g, 128 * (g + 1))
            wg = jnp.where(tril, ws_ref[g], 0.0).astype(BF16)
            dws = jnp.zeros((SGU_CHUNK, SGU_CHUNK), F32)
            dbs = jnp.zeros((1, SGU_CHUNK), F32)
            for c in range(ts // SGU_CHUNK):
                rows = slice(SGU_CHUNK * c, SGU_CHUNK * (c + 1))
                vbc = vb[rows, cols]
                mixed = _nn(wg, vbc) + bs_ref[g]
                zu = zu_ref[rows, cols].astype(F32)
                u, thu = _gelu(zu)
                dout = d_ref[rows, cols].astype(F32)
                dzu_ref[rows, cols] = (dout * mixed * _gelu_grad(zu, thu)).astype(BF16)
                dm = dout * u
                dmb = dm.astype(BF16)
                dws = dws + _nt(dmb, vbc)
                dbs = dbs + jnp.sum(dm.T, axis=0, keepdims=True)
                dvbuf[rows, cols] = _tn(wg, dmb)
            dws_ref[g] += jnp.where(tril, dws, 0.0)
            dbs_ref[pl.ds(g, 1), :] += dbs
        dvln = dvbuf[...]
        dlg_ref[...] += jnp.sum(dvln * xhat, axis=0, keepdims=True)
        dlb_ref[...] += jnp.sum(dvln, axis=0, keepdims=True)
        dzv_ref[...] = (_ln_bwd(dvln, xhat, rstd, lg) * _gelu_grad(zv, thv)).astype(BF16)

    tile = pl.BlockSpec((ts, 512), lambda i: (i, 0))
    vec = jax.ShapeDtypeStruct((1, 512), F32)
    return pl.pallas_call(
        body, name=name, grid=(t // ts,),
        in_specs=[pl.BlockSpec((ts, 512), lambda i: (i, 1)), pl.BlockSpec((ts, 512), lambda i: (i, 2)),
                  pl.BlockSpec((ts, 512), lambda i: (i, 1)), _row(512), _row(512), _full((4, 128, 128)),
                  _full((4, 128, 128))],
        out_specs=[tile, tile, _full((4, 128, 128)), _full((4, 128)), _row(512), _row(512)],
        out_shape=[jax.ShapeDtypeStruct((t, 512), BF16), jax.ShapeDtypeStruct((t, 512), BF16),
                   jax.ShapeDtypeStruct((4, 128, 128), F32), jax.ShapeDtypeStruct((4, 128), F32), vec, vec],
        scratch_shapes=[pltpu.VMEM((ts, 512), F32)],
        compiler_params=_params("arbitrary"))(z1, z1, dmix, ln_g, ln_b, w_s, b_rows)


def _row_tile(r):
    for cand in (512, 352, 256, 192, 128, 64, 32, 16, 8):
        if r % cand == 0:
            return cand
    return r


def _sum_slabs(a, name):
    k, r, c = a.shape
    tr = _row_tile(r)

    def body(*refs):
        acc = refs[0][...].astype(F32)
        for ref in refs[1:-1]:
            acc = acc + ref[...].astype(F32)
        refs[-1][...] = acc

    in_specs = [pl.BlockSpec((None, tr, c), functools.partial(lambda i, s: (s, i, 0), s=s)) for s in range(k)]
    return pl.pallas_call(
        body, name=name, grid=(r // tr,), in_specs=in_specs, out_specs=pl.BlockSpec((tr, c), lambda i: (i, 0)),
        out_shape=jax.ShapeDtypeStruct((r, c), F32), compiler_params=_params("parallel"))(*([a] * k))


def _adamw_math(w, g, m, v):
    mn = ADAM_B1 * m + (1.0 - ADAM_B1) * g
    vn = ADAM_B2 * v + (1.0 - ADAM_B2) * (g * g)
    m_hat = mn / (1.0 - ADAM_B1 ** ADAM_STEP)
    v_hat = vn / (1.0 - ADAM_B2 ** ADAM_STEP)
    return -ADAM_LR * (m_hat / (jnp.sqrt(v_hat) + ADAM_EPS) + ADAM_WD * w), mn, vn


def _reduce_adamw(landing, w, m, v, name, layer=None, into=None):
    k, r, c = landing.shape
    tr = _row_tile(r)
    n_into = 0 if into is None else 4

    def body(*refs):
        slabs, (w_ref, m_ref, v_ref) = refs[:k], refs[k:k + 3]
        g_ref, d_ref, mo_ref, vo_ref = refs[k + 3 + n_into:]
        g = slabs[0][...].astype(F32)
        for ref in slabs[1:]:
            g = g + ref[...].astype(F32)
        g_ref[...] = g
        d_ref[...], mo_ref[...], vo_ref[...] = _adamw_math(w_ref[...], g, m_ref[...], v_ref[...])

    if layer is None:
        spec = pl.BlockSpec((tr, c), lambda i: (i, 0))
    else:
        spec = pl.BlockSpec((None, tr, c), lambda i: (layer, i, 0))
    in_specs = [pl.BlockSpec((None, tr, c), functools.partial(lambda i, s: (s, i, 0), s=s)) for s in range(k)]
    in_specs += [spec] * 3 + [ANY] * n_into
    shape = jax.ShapeDtypeStruct(w.shape, F32)
    return pl.pallas_call(
        body, name=name, grid=(r // tr,), in_specs=in_specs, out_specs=[spec] * 4, out_shape=[shape] * 4,
        input_output_aliases={k + 3 + j: j for j in range(n_into)},
        compiler_params=_params("parallel"))(*([landing] * k), w, m, v, *(into or ()))


def _rows_needed(shape):
    return shape[0] * -(-shape[1] // D)


def _as_rows(a):
    r, c = a.shape
    n = -(-c // D)
    assert r == 1 or n == 1
    return jnp.pad(a, ((0, 0), (0, n * D - c))).reshape(r * n, D)


def _adamw_replicated(g_rows, w, m, v, name):
    n = len(REP_2D)

    def body(*refs):
        g_ref, w_refs, m_refs, v_refs, outs = refs[0], refs[1:1 + n], refs[1 + n:1 + 2 * n], refs[1 + 2 * n:1 + 3 * n], refs[1 + 3 * n:]
        r0 = 0
        for k, (_, (r, c)) in enumerate(REP_2D):
            pieces = [g_ref[r0 + j * r:r0 + j * r + r, 0:min(D, c - j * D)] for j in range(-(-c // D))]
            g = pieces[0] if len(pieces) == 1 else jnp.concatenate(pieces, axis=1)
            outs[4 * k][...] = g
            outs[4 * k + 1][...], outs[4 * k + 2][...], outs[4 * k + 3][...] = _adamw_math(
                w_refs[k][...], g, m_refs[k][...], v_refs[k][...])
            r0 += _rows_needed((r, c))

    shapes = [s for _, s in REP_2D]
    return pl.pallas_call(
        body, name=name, in_specs=[_full(g_rows.shape)] + [_full(s) for s in shapes] * 3,
        out_specs=[_full(s) for s in shapes for _ in range(4)],
        out_shape=[jax.ShapeDtypeStruct(s, F32) for s in shapes for _ in range(4)],
        grid=(1,), compiler_params=_params("arbitrary"))(g_rows, *w, *m, *v)


def _adamw(w, g, m, v, name):
    r, c = w.shape
    tr = _row_tile(r)

    def body(w_ref, g_ref, m_ref, v_ref, d_ref, mo_ref, vo_ref):
        d_ref[...], mo_ref[...], vo_ref[...] = _adamw_math(w_ref[...], g_ref[...], m_ref[...], v_ref[...])

    spec = pl.BlockSpec((tr, c), lambda i: (i, 0))
    shape = jax.ShapeDtypeStruct((r, c), F32)
    return pl.pallas_call(
        body, name=name, grid=(r // tr,), in_specs=[spec] * 4, out_specs=[spec] * 3, out_shape=[shape] * 3,
        compiler_params=_params("parallel"))(w, g, m, v)


ANY = pl.BlockSpec(memory_space=pl.ANY)


def _all_gather(block, name):
    r, c_dim = block.shape

    def body(x_ref, out_ref, token, send_sems, recv_sems, local_sem):
        token[...] = jnp.zeros_like(token)
        x, y, c = lax.axis_index("x"), lax.axis_index("y"), lax.axis_index("c")
        me, sibling = (x, y, c), (x, y, 1 - c)
        chips = [(1 - x, y), (x, 1 - y), (1 - x, 1 - y)]

        def rows(px, py, pc):
            return out_ref.at[4 * px + 2 * py + pc]

        def copy(k, blk, to, src=None):
            return pltpu.make_async_remote_copy(
                src_ref=rows(*blk) if src is None else src, dst_ref=rows(*blk), send_sem=send_sems.at[k],
                recv_sem=recv_sems.at[k], device_id=to, device_id_type=MESH)

        mine = pltpu.make_async_copy(x_ref, rows(*me), local_sem)
        mine.start()
        first = [copy(0, me, sibling, src=x_ref)]
        first += [copy(1 + j, me, (*chip, c), src=x_ref) for j, chip in enumerate(chips)]
        for cp in first:
            cp.start()
        passed = [copy(4 + j, (*chip, c), sibling) for j, chip in enumerate(chips)]
        for j, chip in enumerate(chips):
            copy(1 + j, (*chip, c), me).wait_recv()
            passed[j].start()
        copy(0, sibling, me).wait_recv()
        for j, chip in enumerate(chips):
            copy(4 + j, (*chip, 1 - c), me).wait_recv()
        for cp in first + passed:
            cp.wait_send()
        mine.wait()

    return pl.pallas_call(
        body, name=name, in_specs=[ANY], out_specs=[ANY, pl.BlockSpec(memory_space=pltpu.VMEM)],
        out_shape=[jax.ShapeDtypeStruct((N_DEV, r, c_dim), block.dtype), jax.ShapeDtypeStruct((8, 128), F32)],
        scratch_shapes=[pltpu.SemaphoreType.DMA((7,)), pltpu.SemaphoreType.DMA((7,)), pltpu.SemaphoreType.DMA],
    )(block)


HBM = pl.BlockSpec(memory_space=pltpu.HBM)
SEM = pl.BlockSpec(memory_space=pltpu.SEMAPHORE)
EFFECT = pltpu.SideEffectType.DATAFLOW_SIDE_EFFECTING


def _exchange_copies(scatter, src_refs, land_refs, send_sems, recv_sems, local_sems):
    x, y, c = lax.axis_index("x"), lax.axis_index("y"), lax.axis_index("c")
    me = 4 * x + 2 * y + c
    sends, arrivals, locals_ = [], [], []
    for a, (src, land) in enumerate(zip(src_refs, land_refs)):
        def pick(idx, src=src):
            return src.at[idx] if scatter else src

        locals_.append(pltpu.make_async_copy(pick(me), land.at[me], local_sems.at[a]))
        for r in range(1, N_DEV):
            px = 1 - x if r & 4 else x
            py = 1 - y if r & 2 else y
            pc = 1 - c if r & 1 else c
            peer, s = 4 * px + 2 * py + pc, 7 * a + r - 1
            sends.append(pltpu.make_async_remote_copy(
                src_ref=pick(peer), dst_ref=land.at[me], send_sem=send_sems.at[s], recv_sem=recv_sems.at[s],
                device_id=(px, py, pc), device_id_type=MESH))
            arrivals.append(pltpu.make_async_remote_copy(
                src_ref=pick(peer), dst_ref=land.at[peer], send_sem=send_sems.at[s], recv_sem=recv_sems.at[s],
                device_id=(px, py, pc), device_id_type=MESH))
    return sends, arrivals, locals_


def _exchange_start(srcs, scatter, name):
    n = len(srcs)
    lands = [lax.empty((N_DEV,) + s.shape[-2:], s.dtype) for s in srcs]

    def body(*refs):
        src_refs, land_refs = refs[:n], refs[n:2 * n]
        send_sems, recv_sems, local_sems = refs[2 * n:2 * n + 3]
        token = refs[-1]
        sends, _, locals_ = _exchange_copies(scatter, src_refs, land_refs, send_sems, recv_sems, local_sems)
        for cp in locals_ + sends:
            cp.start()
        token[...] = jnp.zeros_like(token)

    res = pl.pallas_call(
        body, name=name,
        out_shape=[pltpu.SemaphoreType.DMA((7 * n,)), pltpu.SemaphoreType.DMA((7 * n,)), pltpu.SemaphoreType.DMA((n,))]
        + [pltpu.HBM(a.shape, a.dtype) for a in list(srcs) + lands] + [jax.ShapeDtypeStruct((8, 128), F32)],
        in_specs=[HBM] * (2 * n), out_specs=[SEM] * 3 + [HBM] * (2 * n) + [pl.BlockSpec(memory_space=pltpu.VMEM)],
        input_output_aliases={i: 3 + i for i in range(2 * n)},
        compiler_params=pltpu.CompilerParams(has_side_effects=EFFECT),
    )(*[pltpu.with_memory_space_constraint(a, pltpu.HBM) for a in list(srcs) + lands])
    return (n, scatter, res[:3], res[3:3 + 2 * n]), res[-1]


def _exchange_wait(handle, after, name):
    n, scatter, sems, thru = handle

    def body(*refs):
        src_refs, land_refs = refs[:n], refs[n:2 * n]
        send_sems, recv_sems, local_sems = refs[2 * n:2 * n + 3]
        sends, arrivals, locals_ = _exchange_copies(scatter, src_refs, land_refs, send_sems, recv_sems, local_sems)
        for cp in arrivals:
            cp.wait_recv()
        for cp in sends:
            cp.wait_send()
        for cp in locals_:
            cp.wait()

    res = pl.pallas_call(
        body, name=name, out_shape=[pltpu.HBM(a.shape, a.dtype) for a in thru],
        in_specs=[HBM] * (2 * n) + [SEM] * 3 + [ANY], out_specs=[HBM] * (2 * n),
        input_output_aliases={i: i for i in range(2 * n)},
        compiler_params=pltpu.CompilerParams(has_side_effects=EFFECT),
    )(*thru, *sems, after)
    return res[n:]


def _behind(tokens, a):
    zero = sum(tok[0, 0] for tok in tokens)
    return jax.tree.map(lambda v: v + zero.astype(v.dtype), a)


def _perm_heads(a, perm, axis):
    idx = [slice(None)] * a.ndim
    parts = []
    for h in perm:
        idx[axis] = slice(64 * h, 64 * (h + 1))
        parts.append(a[tuple(idx)])
    idx[axis] = slice(512, None)
    if a.shape[axis] > 512:
        parts.append(a[tuple(idx)])
    return jnp.concatenate(parts, axis=axis)


Q_INV = tuple(int(i) for i in np.argsort(Q_PERM))


def _in0_to_kernel(a, axis):
    a = _perm_heads(a, Q_PERM, axis)
    idx = [slice(None)] * a.ndim

    def cut(lo, hi):
        idx[axis] = slice(lo, hi)
        return a[tuple(idx)]

    return jnp.concatenate([cut(0, 512), cut(768, 1792), cut(512, 768)], axis=axis)


def _in0_from_kernel(a, axis):
    idx = [slice(None)] * a.ndim

    def cut(lo, hi):
        idx[axis] = slice(lo, hi)
        return a[tuple(idx)]

    a = jnp.concatenate([cut(0, 512), cut(1536, 1792), cut(512, 1536)], axis=axis)
    return _perm_heads(a, Q_INV, axis)


def _f32_as_u16_rows(v, rows):
    bits = lax.bitcast_convert_type(v, jnp.uint16).reshape(-1)
    return jnp.pad(bits, (0, rows * D - bits.shape[0])).reshape(rows, D)


def _pad_rows(v, rows):
    v = v.reshape(-1)
    return jnp.pad(v, (0, rows * D - v.shape[0])).reshape(rows, D)


def kernel(x, mix_norm, a_w_in, a_b_in, a_sinks, a_conv_w, a_conv_b, a_cln_g, a_cln_b, a_w_out, c_w_in, c_w_pool, c_pool_scale, c_sln_g, c_sln_b, c_w_s, c_b_s, c_w_out, ffn_norm, ffn_w_gate, ffn_w_up, ffn_w_down, final_norm, loss_target, m_mix_norm, m_a_w_in, m_a_b_in, m_a_sinks, m_a_conv_w, m_a_conv_b, m_a_cln_g, m_a_cln_b, m_a_w_out, m_c_w_in, m_c_w_pool, m_c_pool_scale, m_c_sln_g, m_c_sln_b, m_c_w_s, m_c_b_s, m_c_w_out, m_ffn_norm, m_ffn_w_gate, m_ffn_w_up, m_ffn_w_down, m_final_norm, v_mix_norm, v_a_w_in, v_a_b_in, v_a_sinks, v_a_conv_w, v_a_conv_b, v_a_cln_g, v_a_cln_b, v_a_w_out, v_c_w_in, v_c_w_pool, v_c_pool_scale, v_c_sln_g, v_c_sln_b, v_c_w_s, v_c_b_s, v_c_w_out, v_ffn_norm, v_ffn_w_gate, v_ffn_w_up, v_ffn_w_down, v_final_norm):
    bsz, seq, _ = x.shape
    t = bsz * seq
    w_in = dict(mix_norm=mix_norm, a_w_in=a_w_in, a_b_in=a_b_in, a_sinks=a_sinks, a_conv_w=a_conv_w, a_conv_b=a_conv_b,
                a_cln_g=a_cln_g, a_cln_b=a_cln_b, a_w_out=a_w_out, c_w_in=c_w_in, c_w_pool=c_w_pool,
                c_pool_scale=c_pool_scale, c_sln_g=c_sln_g, c_sln_b=c_sln_b, c_w_s=c_w_s, c_b_s=c_b_s, c_w_out=c_w_out,
                ffn_norm=ffn_norm, ffn_w_gate=ffn_w_gate, ffn_w_up=ffn_w_up, ffn_w_down=ffn_w_down, final_norm=final_norm)
    m_in = dict(mix_norm=m_mix_norm, a_w_in=m_a_w_in, a_b_in=m_a_b_in, a_sinks=m_a_sinks, a_conv_w=m_a_conv_w,
                a_conv_b=m_a_conv_b, a_cln_g=m_a_cln_g, a_cln_b=m_a_cln_b, a_w_out=m_a_w_out, c_w_in=m_c_w_in,
                c_w_pool=m_c_w_pool, c_pool_scale=m_c_pool_scale, c_sln_g=m_c_sln_g, c_sln_b=m_c_sln_b, c_w_s=m_c_w_s,
                c_b_s=m_c_b_s, c_w_out=m_c_w_out, ffn_norm=m_ffn_norm, ffn_w_gate=m_ffn_w_gate, ffn_w_up=m_ffn_w_up,
                ffn_w_down=m_ffn_w_down, final_norm=m_final_norm)
    v_in = dict(mix_norm=v_mix_norm, a_w_in=v_a_w_in, a_b_in=v_a_b_in, a_sinks=v_a_sinks, a_conv_w=v_a_conv_w,
                a_conv_b=v_a_conv_b, a_cln_g=v_a_cln_g, a_cln_b=v_a_cln_b, a_w_out=v_a_w_out, c_w_in=v_c_w_in,
                c_w_pool=v_c_w_pool, c_pool_scale=v_c_pool_scale, c_sln_g=v_c_sln_g, c_sln_b=v_c_sln_b, c_w_s=v_c_w_s,
                c_b_s=v_c_b_s, c_w_out=v_c_w_out, ffn_norm=v_ffn_norm, ffn_w_gate=v_ffn_w_gate, ffn_w_up=v_ffn_w_up,
                ffn_w_down=v_ffn_w_down, final_norm=v_final_norm)

    small = jnp.concatenate([a_conv_w[0].reshape(-1), c_pool_scale[0], c_sln_g[0], c_sln_b[0]])
    first_bits = lax.bitcast_convert_type(a_w_in[0].T.astype(BF16), jnp.uint16)
    gathered, tok = _all_gather(jnp.concatenate([first_bits, _f32_as_u16_rows(small, W_MISC_ROWS)], axis=0), "gather_mixer0")

    def ffn_shards(l):
        return [ffn_w_gate[l].T.astype(BF16), ffn_w_up[l].T.astype(BF16), ffn_w_down[l].astype(BF16)]

    ffn0_h, tok = _exchange_start(_behind([tok], ffn_shards(0) + [a_w_out[0].astype(BF16)]), False, "gather_ffn0_start")
    mix1_h, tok = _exchange_start(_behind([tok], [c_w_in[0].T.astype(BF16), c_w_out[0].astype(BF16)]), False,
                                  "gather_mixer1_start")
    ffn1_h, tok = _exchange_start(_behind([tok], ffn_shards(1)), False, "gather_ffn1_start")

    a_in_full = lax.bitcast_convert_type(gathered[:, :224].reshape(IN0, D), BF16)
    small_all = lax.bitcast_convert_type(
        gathered[:, 224:].reshape(N_DEV, -1)[:, :2 * SMALL_SHARD].reshape(N_DEV, SMALL_SHARD, 2), F32)
    conv_w = small_all[:, :31 * 64].reshape(N_DEV, 31, 64).transpose(1, 0, 2).reshape(31, 512)
    conv_w = jnp.pad(conv_w, ((0, HALO - CONV_K), (0, 0)))
    pool_scale = small_all[:, 31 * 64:31 * 64 + 64].reshape(1, 512)
    sln_g = small_all[:, 31 * 64 + 64:31 * 64 + 128].reshape(1, 512)
    sln_b = small_all[:, 31 * 64 + 128:].reshape(1, 512)

    wt_in0 = _in0_to_kernel(a_in_full, 0)
    b_in0 = _in0_to_kernel(a_b_in, 1)
    b_rows = jnp.broadcast_to(c_b_s[0][:, :, None], (4, 128, 128))
    conv_b, cln_g, cln_b = a_conv_b, a_cln_g, a_cln_b

    h0 = x.reshape(t, D)
    target = loss_target.reshape(t, D)
    z0, hn0 = _norm_proj(h0, _behind([tok], mix_norm[0:1]), wt_in0, b_in0, "in_proj0")
    attn, tok = _attn_fwd(z0, a_sinks, bsz, "attn_fwd")
    conv, conv_y = _conv_fwd(z0, conv_w, conv_b, cln_g, _behind([tok], cln_b), bsz, "conv_fwd")
    wtg0, wtu0, wd0, a_out_full = (w.reshape(-1, D) for w in _exchange_wait(ffn0_h, conv, "gather_ffn0_wait"))
    w_out0 = _perm_heads(a_out_full, Q_PERM, 0)
    h1, hnf0, gate0a, up0a, part = _ffn_fwd_a(h0, attn, conv, w_out0, ffn_norm[0:1], wtg0, wtu0, wd0, "ffn_fwd0a")
    gate0b, up0b, h2 = _ffn_fwd_b(hnf0, part, wtg0, wtu0, wd0, "ffn_fwd0b")
    wt_in1, w_out1 = (w.reshape(-1, D) for w in _exchange_wait(mix1_h, h2, "gather_mixer1_wait"))
    z1, hn1 = _norm_proj(h2, mix_norm[1:2], wt_in1, None, "in_proj1")
    pool, tok = _pool_fwd(z1, c_w_pool[0], pool_scale, bsz, "pool_fwd")
    sgu = _sgu_fwd(z1, sln_g, _behind([tok], sln_b), c_w_s[0], b_rows, "sgu_fwd")
    wtg1, wtu1, wd1 = (w.reshape(D_FF, D) for w in _exchange_wait(ffn1_h, sgu, "gather_ffn1_wait"))
    h3, hnf1, gate1a, up1a, part = _ffn_fwd_a(h2, pool, sgu, w_out1, ffn_norm[1:2], wtg1, wtu1, wd1, "ffn_fwd1a")
    gate1b, up1b, dh4, d_final_norm, loss_part = _ffn_fwd_b(hnf1, part, wtg1, wtu1, wd1, "ffn_fwd1b",
                                                            head=(final_norm.reshape(1, D), target))

    def blocks(g):
        return g.reshape(N_DEV, g.shape[0] // N_DEV, D)

    *wide, part = _ffn_bwd_a(dh4, gate1a, up1a, wtg1, wtu1, wd1, "ffn_bwd1a")
    dh3, dmix1, dgate1, dup1, act1, d_fn1 = _ffn_bwd_b(dh4, h3, ffn_norm[1:2], gate1b, up1b, wtg1, wtu1, wd1, w_out1, part,
                                                       wide, "ffn_bwd1b")
    gw_ffn1 = [_mm_tn(dgate1, hnf1, "dw_gate1"), _mm_tn(dup1, hnf1, "dw_up1"), _mm_tn(act1, dh4, "dw_down1")]
    ffn1_g, tok = _exchange_start([blocks(g) for g in gw_ffn1], True, "scatter_ffn1_start")
    gw_c_out = _mm_tn_pieces([pool, sgu], dh3, "dw_out1")
    dzp, d_w_pool, d_pool_scale = _pool_bwd(z1, dmix1, c_w_pool[0], _behind([tok], pool_scale), bsz, "pool_bwd")
    dzu, dzv, d_w_s, d_b_s, d_sln_g, d_sln_b = _sgu_bwd(z1, dmix1, sln_g, sln_b, c_w_s[0], b_rows, "sgu_bwd")
    dh2, d_mn1 = _proj_bwd_norm([(dzp, 0), (dzu, 512), (dzv, 1024)], wt_in1, h2, dh3, mix_norm[1:2], BF16, "in_proj1_bwd")
    gw_c_in = _mm_tn_pieces([dzp, dzu, dzv], hn1, "dw_in1")
    mix1_g, tok = _exchange_start([blocks(gw_c_in), blocks(gw_c_out)], True, "scatter_mixer1_start")
    *wide, part = _ffn_bwd_a(dh2, gate0a, up0a, wtg0, wtu0, wd0, "ffn_bwd0a")
    dh1, dmix0, dgate0, dup0, act0, d_fn0 = _ffn_bwd_b(dh2, h1, _behind([tok], ffn_norm[0:1]), gate0b, up0b, wtg0, wtu0, wd0,
                                                       w_out0, part, wide, "ffn_bwd0b")
    gw_ffn0 = [_mm_tn(dgate0, hnf0, "dw_gate0"), _mm_tn(dup0, hnf0, "dw_up0"), _mm_tn(act0, dh2, "dw_down0")]
    ffn0_g, tok = _exchange_start([blocks(g) for g in gw_ffn0], True, "scatter_ffn0_start")
    gw_a_out = _perm_heads(_mm_tn_pieces([attn, conv], dh1, "dw_out0"), Q_INV, 0)
    dq, dkv, d_sink_row, d_bq, d_bkv = _attn_bwd(z0, dmix0, _behind([tok], a_sinks), bsz, "attn_bwd")
    dca, dcg, d_conv_w, d_conv_b, d_cln_g, d_cln_b, d_ba, d_bg = _conv_bwd(z0, conv_y, dmix0, conv_w, cln_g, cln_b, bsz, "conv_bwd")
    gw_a_in = _in0_from_kernel(_mm_tn_pieces([dq, dca, dcg, dkv], hn0, "dw_in0"), 0)
    mix0_g, tok = _exchange_start([blocks(gw_a_in), blocks(gw_a_out)], True, "scatter_mixer0_start")
    dx, d_mn0 = _proj_bwd_norm([(dq, 0), (dca, 512), (dcg, 1024), (dkv, 1536)], wt_in0, h0, dh1,
                               _behind([tok], mix_norm[0:1]), F32, "in_proj0_bwd")
    d_b_in = _in0_from_kernel(jnp.concatenate([d_bq, d_ba, d_bg, d_bkv], axis=1), 1)

    rep = dict(mix_norm=jnp.concatenate([d_mn0, d_mn1], axis=0), a_b_in=d_b_in, a_sinks=d_sink_row[:, :8],
               a_conv_b=d_conv_b, a_cln_g=d_cln_g, a_cln_b=d_cln_b, c_w_pool=d_w_pool.reshape(64, D),
               c_w_s=d_w_s.reshape(64, D), c_b_s=d_b_s, ffn_norm=jnp.concatenate([d_fn0, d_fn1], axis=0),
               final_norm=d_final_norm)
    rep_rows = jnp.concatenate([_as_rows(rep[nm]) for nm, _ in REP_2D] + [_as_rows(loss_part)], axis=0)
    rep_flat = jnp.pad(rep_rows, ((0, N_DEV * REP_ROWS - rep_rows.shape[0]), (0, 0))).reshape(N_DEV, REP_ROWS, D)
    small_g = jnp.concatenate([
        d_conv_w[:CONV_K].reshape(31, N_DEV, 64).transpose(1, 0, 2).reshape(N_DEV, 31 * 64),
        d_pool_scale.reshape(N_DEV, 64), d_sln_g.reshape(N_DEV, 64), d_sln_b.reshape(N_DEV, 64)], axis=1)
    small_g = jnp.pad(small_g, ((0, 0), (0, G_SMALL_ROWS * D - SMALL_SHARD))).reshape(N_DEV, G_SMALL_ROWS, D)
    tail_g, tok = _exchange_start([jnp.concatenate([small_g, rep_flat], axis=1)], True, "scatter_tail_start")

    names = list(w_in)
    g_out, delta, new_m, new_v = {}, {}, {}, {}
    column_sharded = ("a_w_in", "c_w_in", "ffn_w_gate", "ffn_w_up")

    def rows_of(a, nm):
        return jnp.swapaxes(a, 1, 2) if nm in column_sharded else a

    def reduce_adamw(nm, landing, layer, into=None):
        args = [rows_of(d[nm], nm) for d in (w_in, m_in, v_in)]
        if args[0].shape[0] == 1:
            args, layer = [a[0] for a in args], None
        return _reduce_adamw(landing, *args, "adamw_%s_%s" % (nm, layer), layer=layer, into=into)

    def keep(nm, res):
        res = [r if r.ndim == 3 else r[None] for r in res]
        g_out[nm], delta[nm], new_m[nm], new_v[nm] = (rows_of(r, nm) for r in res)

    ffn_names = ("ffn_w_gate", "ffn_w_up", "ffn_w_down")
    landed = _exchange_wait(ffn1_g, tok, "scatter_ffn1_wait")
    ffn_res = [reduce_adamw(nm, a, 1) for nm, a in zip(ffn_names, landed)]
    landed = _exchange_wait(mix1_g, ffn_res[-1][0], "scatter_mixer1_wait")
    for nm, a in zip(("c_w_in", "c_w_out"), landed):
        keep(nm, reduce_adamw(nm, a, 0))
    landed = _exchange_wait(ffn0_g, g_out["c_w_out"], "scatter_ffn0_wait")
    for nm, a, res in zip(ffn_names, landed, ffn_res):
        keep(nm, reduce_adamw(nm, a, 0, into=res))
    landed = _exchange_wait(mix0_g, g_out["ffn_w_down"], "scatter_mixer0_wait")
    for nm, a in zip(("a_w_in", "a_w_out"), landed):
        keep(nm, reduce_adamw(nm, a, 0))
    g_tail = _sum_slabs(_exchange_wait(tail_g, g_out["a_w_out"], "scatter_tail_wait")[0], "sum_tail")
    rep_all = _all_gather(g_tail[G_SMALL_ROWS:], "gather_replicated_grads")[0].reshape(N_DEV * REP_ROWS, D)
    loss = rep_all[sum(_rows_needed(s) for _, s in REP_2D), 0]
    res = _adamw_replicated(rep_all, *[[d[nm].reshape(s) for nm, s in REP_2D] for d in (w_in, m_in, v_in)], "adamw_replicated")
    for k, (nm, _) in enumerate(REP_2D):
        g_out[nm], delta[nm], new_m[nm], new_v[nm] = (r.reshape(w_in[nm].shape) for r in res[4 * k:4 * k + 4])
    small_r = g_tail[:G_SMALL_ROWS].reshape(-1)[:SMALL_SHARD]
    g_out.update(
        a_conv_w=small_r[:31 * 64].reshape(1, 31, 64), c_pool_scale=small_r[31 * 64:31 * 64 + 64].reshape(1, 64),
        c_sln_g=small_r[31 * 64 + 64:31 * 64 + 128].reshape(1, 64), c_sln_b=small_r[31 * 64 + 128:].reshape(1, 64))
    group = ("a_conv_w", "c_pool_scale", "c_sln_g", "c_sln_b")
    flat = [_pad_rows(jnp.concatenate([d[nm].reshape(-1) for nm in group]), G_SMALL_ROWS) for d in (w_in, g_out, m_in, v_in)]
    res = [r.reshape(-1) for r in _adamw(*flat, "adamw_small_sharded")]
    off = 0
    for nm in group:
        n = int(np.prod(w_in[nm].shape))
        delta[nm], new_m[nm], new_v[nm] = (r[off:off + n].reshape(w_in[nm].shape) for r in res)
        off += n

    grad_x = dx.reshape(bsz, seq, D)
    return (loss, grad_x, *[g_out[nm] for nm in names], *[delta[nm] for nm in names],
            *[new_m[nm] for nm in names], *[new_v[nm] for nm in names])
```

```python
import functools

import jax
import jax.numpy as jnp
import numpy as np
from jax import lax
from jax.experimental import pallas as pl
from jax.experimental.pallas import tpu as pltpu

F32 = jnp.float32
BF16 = jnp.bfloat16
MESH = pl.DeviceIdType.MESH

D = 1024
N_DEV = 8
EPS = 1e-5
HEAD_PAIRS = 4
ATT_BLK = 128
CONV_K = 31
HALO = 32
D_FF = 2816
IN0 = 1792
SGU_CHUNK = 128
GELU_C = 0.7978845608028654
GELU_A = 0.044715
ADAM_LR, ADAM_B1, ADAM_B2, ADAM_EPS, ADAM_WD, ADAM_STEP = 0.001, 0.9, 0.999, 1e-08, 0.01, 10
VMEM_LIMIT = 56 << 20

SMALL_SHARD = 31 * 64 + 3 * 64
W_MISC_ROWS = 16
G_MISC_ROWS = 32
G_SMALL_ROWS = 8
REP_ROWS = G_MISC_ROWS - G_SMALL_ROWS
REP_2D = (("c_w_pool", (64, 1024)), ("c_w_s", (64, 1024)), ("mix_norm", (2, 1024)), ("a_b_in", (1, 1792)), ("a_sinks", (1, 8)),
          ("a_conv_b", (1, 512)), ("a_cln_g", (1, 512)), ("a_cln_b", (1, 512)), ("c_b_s", (4, 128)), ("ffn_norm", (2, 1024)),
          ("final_norm", (1, 1024)))
Q_PERM = (0, 4, 1, 5, 2, 6, 3, 7)


def _params(*sem):
    return pltpu.CompilerParams(dimension_semantics=sem, vmem_limit_bytes=VMEM_LIMIT)


def _nn(a, b):
    return jnp.dot(a, b, preferred_element_type=F32)


def _nt(a, b):
    return lax.dot_general(a, b, (((1,), (1,)), ((), ())), preferred_element_type=F32)


def _tn(a, b):
    return lax.dot_general(a, b, (((0,), (0,)), ((), ())), preferred_element_type=F32)


def _tile(n, want=512):
    t = min(want, n)
    assert n % t == 0, (n, t)
    return t


def _seq_tile(s):
    return 512 if s >= 1024 else s // 2


def _rms(x, g):
    r = lax.rsqrt(jnp.mean(x * x, axis=-1, keepdims=True) + EPS)
    return x * r * g, r


def _rms_bwd(x, g, d_y):
    r = lax.rsqrt(jnp.mean(x * x, axis=-1, keepdims=True) + EPS)
    xr = x * r
    u = d_y * g
    d_x = r * (u - xr * jnp.mean(u * xr, axis=-1, keepdims=True))
    return d_x, jnp.sum(d_y * xr, axis=0, keepdims=True)


def _ln(y, g, b):
    mu = jnp.mean(y, axis=-1, keepdims=True)
    yc = y - mu
    rstd = lax.rsqrt(jnp.mean(yc * yc, axis=-1, keepdims=True) + EPS)
    xhat = yc * rstd
    return xhat * g + b, xhat, rstd


def _ln_bwd(d_o, xhat, rstd, g):
    dxh = d_o * g
    return rstd * (dxh - jnp.mean(dxh, axis=-1, keepdims=True) - xhat * jnp.mean(dxh * xhat, axis=-1, keepdims=True))


def _gelu(x):
    th = jnp.tanh(GELU_C * (x + GELU_A * x * x * x))
    return 0.5 * x * (1.0 + th), th


def _gelu_grad(x, th):
    return 0.5 * (1.0 + th) + 0.5 * x * (1.0 - th * th) * GELU_C * (1.0 + 3.0 * GELU_A * x * x)


def _row(c):
    return pl.BlockSpec((1, c), lambda *_: (0, 0))


def _full(shape):
    return pl.BlockSpec(shape, lambda *_: (0,) * len(shape))


def _norm_proj(h, g, wt, bias, name):
    t, n = h.shape[0], wt.shape[0]
    tm = _tile(t, 1024)
    has_bias = bias is not None

    def body(*refs):
        h_ref, g_ref, wt_ref = refs[:3]
        z_ref, hn_ref = refs[-2:]
        hn = _rms(h_ref[...].astype(F32), g_ref[...])[0].astype(BF16)
        hn_ref[...] = hn
        z = _nt(hn, wt_ref[...])
        if has_bias:
            z = z + refs[3][...]
        z_ref[...] = z.astype(BF16)

    in_specs = [pl.BlockSpec((tm, D), lambda i: (i, 0)), _row(D), _full((n, D))]
    args = [h, g, wt]
    if has_bias:
        in_specs.append(_row(n))
        args.append(bias)
    return pl.pallas_call(
        body, name=name, grid=(t // tm,), in_specs=in_specs,
        out_specs=[pl.BlockSpec((tm, n), lambda i: (i, 0)), pl.BlockSpec((tm, D), lambda i: (i, 0))],
        out_shape=[jax.ShapeDtypeStruct((t, n), BF16), jax.ShapeDtypeStruct((t, D), BF16)],
        compiler_params=_params("parallel"))(*args)


def _ff_pieces(tf, width=256):
    return [(c0, min(width, tf - c0)) for c0 in range(0, tf, width)]


FF_HALF = D_FF // 2


def _half_spec(part):
    return pl.BlockSpec((FF_HALF, D), lambda i: (part, 0))


def _ffn_half_fwd(hn, wtg_ref, wtu_ref, gate_ref, up_ref, act):
    for c0, cw in _ff_pieces(FF_HALF):
        cols = slice(c0, c0 + cw)
        gate = _nt(hn, wtg_ref[cols, :])
        up = _nt(hn, wtu_ref[cols, :])
        gate_ref[:, cols] = gate.astype(BF16)
        up_ref[:, cols] = up.astype(BF16)
        act[:, cols] = (gate * jax.nn.sigmoid(gate) * up).astype(BF16)


def _ffn_fwd_a(h_prev, a, b, w_out, g, wtg, wtu, wd, name):
    t = h_prev.shape[0]
    tm = _tile(t)

    def body(hp_ref, a_ref, b_ref, wa_ref, wb_ref, g_ref, wtg_ref, wtu_ref, wd_ref,
             h_ref, hn_ref, gate_ref, up_ref, part_ref, act):
        x = hp_ref[...].astype(F32) + _nn(a_ref[...], wa_ref[...]) + _nn(b_ref[...], wb_ref[...])
        h_ref[...] = x.astype(BF16)
        hn = _rms(x, g_ref[...])[0].astype(BF16)
        hn_ref[...] = hn
        _ffn_half_fwd(hn, wtg_ref, wtu_ref, gate_ref, up_ref, act)
        part_ref[...] = x + _nn(act[...], wd_ref[...])

    tok = pl.BlockSpec((tm, D), lambda i: (i, 0))
    half = pl.BlockSpec((tm, 512), lambda i: (i, 0))
    mid = pl.BlockSpec((tm, FF_HALF), lambda i: (i, 0))
    res = jax.ShapeDtypeStruct((t, D), BF16)
    mid_shape = jax.ShapeDtypeStruct((t, FF_HALF), BF16)
    return pl.pallas_call(
        body, name=name, grid=(t // tm,),
        in_specs=[tok, half, half, pl.BlockSpec((512, D), lambda i: (0, 0)), pl.BlockSpec((512, D), lambda i: (1, 0)),
                  _row(D), _half_spec(0), _half_spec(0), _half_spec(0)],
        out_specs=[tok, tok, mid, mid, tok], out_shape=[res, res, mid_shape, mid_shape, jax.ShapeDtypeStruct((t, D), F32)],
        scratch_shapes=[pltpu.VMEM((tm, FF_HALF), BF16)],
        compiler_params=_params("parallel"))(h_prev, a, b, w_out, w_out, g, wtg, wtu, wd)


def _ffn_fwd_b(hn, part, wtg, wtu, wd, name, head=None):
    t = hn.shape[0]
    tm = _tile(t, 1024 if head is None else 512)
    n_head = 0 if head is None else 2

    def body(*refs):
        hn_ref, part_ref, wtg_ref, wtu_ref, wd_ref = refs[:5]
        gate_ref, up_ref, o_ref = refs[5 + n_head:8 + n_head]
        act = refs[-1]
        _ffn_half_fwd(hn_ref[...], wtg_ref, wtu_ref, gate_ref, up_ref, act)
        x = part_ref[...] + _nn(act[...], wd_ref[...])
        if head is None:
            o_ref[...] = x.astype(BF16)
        else:
            fg_ref, t_ref = refs[5:7]
            dfg_ref, loss_ref = refs[10:12]

            @pl.when(pl.program_id(0) == 0)
            def _():
                dfg_ref[...] = jnp.zeros_like(dfg_ref)
                loss_ref[...] = jnp.zeros_like(loss_ref)

            gv = fg_ref[...]
            err = _rms(x, gv)[0] - t_ref[...]
            loss_ref[...] += 0.5 * jnp.sum(jnp.mean(err * err, axis=-1, keepdims=True), axis=0, keepdims=True)
            d_x, d_g = _rms_bwd(x, gv, err * (1.0 / D))
            o_ref[...] = d_x.astype(BF16)
            dfg_ref[...] += d_g

    tok = pl.BlockSpec((tm, D), lambda i: (i, 0))
    mid = pl.BlockSpec((tm, FF_HALF), lambda i: (i, 0))
    mid_shape = jax.ShapeDtypeStruct((t, FF_HALF), BF16)
    in_specs = [tok, tok, _half_spec(1), _half_spec(1), _half_spec(1)]
    out_specs = [mid, mid, tok]
    out_shape = [mid_shape, mid_shape, jax.ShapeDtypeStruct((t, D), BF16)]
    if head is not None:
        in_specs += [_row(D), tok]
        out_specs += [_row(D), _row(1)]
        out_shape += [jax.ShapeDtypeStruct((1, D), F32), jax.ShapeDtypeStruct((1, 1), F32)]
    return pl.pallas_call(
        body, name=name, grid=(t // tm,), in_specs=in_specs, out_specs=out_specs, out_shape=out_shape,
        scratch_shapes=[pltpu.VMEM((tm, FF_HALF), BF16)],
        compiler_params=_params("parallel" if head is None else "arbitrary"))(hn, part, wtg, wtu, wd, *(head or ()))


def _ffn_half_bwd(dh, gate_ref, up_ref, wd_ref, dgate_ref, dup_ref, act_ref):
    for c0, cw in _ff_pieces(FF_HALF):
        cols = slice(c0, c0 + cw)
        da = _nt(dh, wd_ref[cols, :])
        gt = gate_ref[:, cols].astype(F32)
        u = up_ref[:, cols].astype(F32)
        sg = jax.nn.sigmoid(gt)
        sil = gt * sg
        act_ref[:, cols] = (sil * u).astype(BF16)
        dup_ref[:, cols] = (da * sil).astype(BF16)
        dgate_ref[:, cols] = (da * u * sg * (1.0 + gt * (1.0 - sg))).astype(BF16)


def _ffn_bwd_a(dh, gate, up, wtg, wtu, wd, name):
    t = dh.shape[0]
    tm = _tile(t)

    def body(dh_ref, gate_ref, up_ref, wtg_ref, wtu_ref, wd_ref, dgate_ref, dup_ref, act_ref, part_ref):
        _ffn_half_bwd(dh_ref[...], gate_ref, up_ref, wd_ref, dgate_ref, dup_ref, act_ref)
        part_ref[...] = _nn(dgate_ref[...], wtg_ref[...]) + _nn(dup_ref[...], wtu_ref[...])

    tok = pl.BlockSpec((tm, D), lambda i: (i, 0))
    mid = pl.BlockSpec((tm, FF_HALF), lambda i: (i, 0))
    wide = jax.ShapeDtypeStruct((t, D_FF), BF16)
    return pl.pallas_call(
        body, name=name, grid=(t // tm,), in_specs=[tok, mid, mid, _half_spec(0), _half_spec(0), _half_spec(0)],
        out_specs=[mid, mid, mid, tok], out_shape=[wide, wide, wide, jax.ShapeDtypeStruct((t, D), F32)],
        compiler_params=_params("parallel"))(dh, gate, up, wtg, wtu, wd)


def _ffn_bwd_b(dh, h, g, gate, up, wtg, wtu, wd, w_out, part, wide, name):
    t = dh.shape[0]
    tm = _tile(t)

    def body(dh_ref, h_ref, g_ref, gate_ref, up_ref, wtg_ref, wtu_ref, wd_ref, wout_ref, part_ref, _a, _b, _c,
             dhin_ref, dmix_ref, dgate_ref, dup_ref, act_ref, dg_ref):
        @pl.when(pl.program_id(0) == 0)
        def _():
            dg_ref[...] = jnp.zeros_like(dg_ref)

        _ffn_half_bwd(dh_ref[...], gate_ref, up_ref, wd_ref, dgate_ref, dup_ref, act_ref)
        d_hn = part_ref[...] + _nn(dgate_ref[...], wtg_ref[...]) + _nn(dup_ref[...], wtu_ref[...])
        d_x, d_g = _rms_bwd(h_ref[...].astype(F32), g_ref[...], d_hn)
        dhin = (dh_ref[...].astype(F32) + d_x).astype(BF16)
        dhin_ref[...] = dhin
        dmix_ref[...] = _nt(dhin, wout_ref[...]).astype(BF16)
        dg_ref[...] += d_g

    tok = pl.BlockSpec((tm, D), lambda i: (i, 0))
    mid = pl.BlockSpec((tm, FF_HALF), lambda i: (i, 0))
    second = pl.BlockSpec((tm, FF_HALF), lambda i: (i, 1))
    wide_shape = jax.ShapeDtypeStruct((t, D_FF), BF16)
    res = jax.ShapeDtypeStruct((t, D), BF16)
    return pl.pallas_call(
        body, name=name, grid=(t // tm,),
        in_specs=[tok, tok, _row(D), mid, mid, _half_spec(1), _half_spec(1), _half_spec(1), _full((D, D)), tok, ANY, ANY, ANY],
        out_specs=[tok, tok, second, second, second, _row(D)],
        out_shape=[res, res, wide_shape, wide_shape, wide_shape, jax.ShapeDtypeStruct((1, D), F32)],
        input_output_aliases={10: 2, 11: 3, 12: 4},
        compiler_params=_params("arbitrary"))(dh, h, g, gate, up, wtg, wtu, wd, w_out, part, *wide)


def _proj_bwd_norm(pieces, wt, h, dh, g, dtype, name):
    t = h.shape[0]
    tm = _tile(t)
    n_p = len(pieces)

    def body(*refs):
        p_refs, w_refs = refs[:n_p], refs[n_p:2 * n_p]
        h_ref, dh_ref, g_ref, o_ref, dg_ref = refs[2 * n_p:]

        @pl.when(pl.program_id(0) == 0)
        def _():
            dg_ref[...] = jnp.zeros_like(dg_ref)

        d_hn = _nn(p_refs[0][...], w_refs[0][...])
        for p_ref, w_ref in zip(p_refs[1:], w_refs[1:]):
            d_hn = d_hn + _nn(p_ref[...], w_ref[...])
        d_x, d_g = _rms_bwd(h_ref[...].astype(F32), g_ref[...], d_hn)
        o_ref[...] = (dh_ref[...].astype(F32) + d_x).astype(dtype)
        dg_ref[...] += d_g

    tok = pl.BlockSpec((tm, D), lambda i: (i, 0))
    in_specs = [pl.BlockSpec((tm, a.shape[1]), lambda i: (i, 0)) for a, _ in pieces]
    for a, off in pieces:
        w = a.shape[1]
        assert off % w == 0
        in_specs.append(pl.BlockSpec((w, D), functools.partial(lambda i, blk: (blk, 0), blk=off // w)))
    in_specs += [tok, tok, _row(D)]
    return pl.pallas_call(
        body, name=name, grid=(t // tm,), in_specs=in_specs, out_specs=[tok, _row(D)],
        out_shape=[jax.ShapeDtypeStruct((t, D), dtype), jax.ShapeDtypeStruct((1, D), F32)],
        compiler_params=_params("arbitrary"))(*[a for a, _ in pieces], *([wt] * n_p), h, dh, g)


def _mm_tn(a, b, name):
    t, n = a.shape
    k = b.shape[1]
    tn = n if n <= 1024 else n // 2
    tt = _tile(t, 2048)
    nt = t // tt

    def body(a_ref, b_ref, o_ref, acc):
        s = pl.program_id(1)

        @pl.when(s == 0)
        def _():
            acc[...] = jnp.zeros_like(acc)

        acc[...] += _tn(a_ref[...], b_ref[...].astype(BF16))

        @pl.when(s == nt - 1)
        def _():
            o_ref[...] = acc[...].astype(BF16)

    return pl.pallas_call(
        body, name=name, grid=(n // tn, nt),
        in_specs=[pl.BlockSpec((tt, tn), lambda j, s: (s, j)), pl.BlockSpec((tt, k), lambda j, s: (s, 0))],
        out_specs=pl.BlockSpec((tn, k), lambda j, s: (j, 0)), out_shape=jax.ShapeDtypeStruct((n, k), BF16),
        scratch_shapes=[pltpu.VMEM((tn, k), F32)],
        compiler_params=_params("parallel", "arbitrary"))(a, b)


def _mm_tn_pieces(pieces, b, name):
    t, k = b.shape
    widths = [p.shape[1] for p in pieces]
    n, n_p = sum(widths), len(pieces)
    tt = _tile(t, 2048)
    nt = t // tt

    def body(*refs):
        b_ref, o_ref, acc = refs[n_p:]
        s = pl.program_id(0)

        @pl.when(s == 0)
        def _():
            acc[...] = jnp.zeros_like(acc)

        bb = b_ref[...].astype(BF16)
        off = 0
        for p_ref, w in zip(refs[:n_p], widths):
            acc[off:off + w, :] += _tn(p_ref[...], bb)
            off += w

        @pl.when(s == nt - 1)
        def _():
            o_ref[...] = acc[...].astype(BF16)

    return pl.pallas_call(
        body, name=name, grid=(nt,),
        in_specs=[pl.BlockSpec((tt, w), lambda s: (s, 0)) for w in widths] + [pl.BlockSpec((tt, k), lambda s: (s, 0))],
        out_specs=_full((n, k)), out_shape=jax.ShapeDtypeStruct((n, k), BF16),
        scratch_shapes=[pltpu.VMEM((n, k), F32)], compiler_params=_params("arbitrary"))(*pieces, b)


STACK = HEAD_PAIRS * ATT_BLK


def _attn_valid(first, rows):
    qi = lax.broadcasted_iota(jnp.int32, (rows, 2 * ATT_BLK), 0) % ATT_BLK
    r = lax.broadcasted_iota(jnp.int32, (rows, 2 * ATT_BLK), 1)
    dist = qi + ATT_BLK - r
    return (dist >= 0) & (dist < ATT_BLK) & ((r >= ATT_BLK) | jnp.logical_not(first))


def _stacked(ref, kh, scale):
    lo = lax.broadcasted_iota(jnp.int32, (ATT_BLK, 128), 1) < 64
    keep = lo if kh == 0 else ~lo
    parts = [jnp.where(keep, ref[:, g * 128:(g + 1) * 128] * scale, 0.0).astype(BF16) for g in range(HEAD_PAIRS)]
    return jnp.concatenate(parts, axis=0)


def _unstacked(a0, a1, g):
    lo = lax.broadcasted_iota(jnp.int32, (ATT_BLK, 128), 1) < 64
    rows = slice(g * ATT_BLK, (g + 1) * ATT_BLK)
    return jnp.where(lo, a0[rows], a1[rows])


def _sink_rows(s_ref, kh):
    return jnp.concatenate([jnp.full((ATT_BLK, 128), s_ref[0, kh * 4 + g], F32) for g in range(HEAD_PAIRS)], axis=0)


def _row_sums(a, split):
    hi = a.astype(BF16)
    ones = jnp.ones((2 * ATT_BLK, 128), BF16)
    if not split:
        return _nn(hi, ones)
    lo = (a - hi.astype(F32)).astype(BF16)
    return _nn(hi, ones) + _nn(lo, ones)


def _both(a):
    return jnp.concatenate([a, a], axis=1)


def _attn_probs(qs, kpair, sink, valid):
    s = jnp.where(valid, _nt(qs, kpair), -1e30)
    m = jnp.maximum(jnp.broadcast_to(jnp.max(s, axis=-1, keepdims=True), (s.shape[0], 128)), sink)
    p = jnp.exp(s - _both(m))
    es = jnp.exp(sink - m)
    inv = 1.0 / (_row_sums(p, split=True) + es)
    return p * _both(inv), es * inv


def _attn_probs_head(qm, kpair, sink, valid):
    s = jnp.where(valid, _nt(qm, kpair), -1e30)
    m = jnp.maximum(jnp.max(s, axis=-1, keepdims=True), sink)
    p = jnp.exp(s - m)
    return p * (1.0 / (jnp.sum(p, axis=-1, keepdims=True) + jnp.exp(sink - m)))


def _attn_specs(bsz, order):
    q = pl.BlockSpec((bsz, ATT_BLK, 512), lambda j: (0, order(j), 0))
    kvc = pl.BlockSpec((bsz, ATT_BLK, 256), lambda j: (0, order(j), 6))
    kvp = pl.BlockSpec((bsz, ATT_BLK, 256), lambda j: (0, jnp.maximum(order(j) - 1, 0), 6))
    return q, kvc, kvp


def _window_kv(kvc_ref, kvp_ref):
    kvc, kvp = kvc_ref[...], kvp_ref[...]
    kpair = jnp.concatenate([kvp[:, :128], kvc[:, :128]], axis=0)
    vpair = jnp.concatenate([kvp[:, 128:], kvc[:, 128:]], axis=0)
    return kpair, vpair


def _attn_fwd(z0, sinks, bsz, name):
    t = z0.shape[0]
    seq = t // bsz
    nb = seq // ATT_BLK

    def body(s_ref, q_ref, kvc_ref, kvp_ref, o_ref, token):
        token[...] = jnp.zeros_like(token)
        valid = _attn_valid(pl.program_id(0) == 0, ATT_BLK)
        lo = lax.broadcasted_iota(jnp.int32, (ATT_BLK, 128), 1) < 64
        for b in range(bsz):
            kpair, vpair = _window_kv(kvc_ref.at[b], kvp_ref.at[b])
            for g in range(HEAD_PAIRS):
                qs = q_ref[b, :, g * 128:(g + 1) * 128] * 0.125
                outs = []
                for kh in range(2):
                    qm = jnp.where(lo if kh == 0 else ~lo, qs, 0.0).astype(BF16)
                    p = _attn_probs_head(qm, kpair, s_ref[0, kh * 4 + g], valid)
                    outs.append(_nn(p.astype(BF16), vpair))
                o_ref[b, :, g * 128:(g + 1) * 128] = jnp.where(lo, outs[0], outs[1]).astype(BF16)

    q, kvc, kvp = _attn_specs(bsz, lambda j: j)
    z3 = z0.reshape(bsz, seq, z0.shape[1])
    out, token = pl.pallas_call(
        body, name=name, grid=(nb,),
        in_specs=[pl.BlockSpec(memory_space=pltpu.SMEM), q, kvc, kvp],
        out_specs=[pl.BlockSpec((bsz, ATT_BLK, 512), lambda j: (0, j, 0)), _full((8, 128))],
        out_shape=[jax.ShapeDtypeStruct((bsz, seq, 512), BF16), jax.ShapeDtypeStruct((8, 128), F32)],
        compiler_params=_params("arbitrary"))(sinks, z3, z3, z3)
    return out.reshape(t, 512), token


def _attn_bwd(z0, dmix, sinks, bsz, name):
    t = z0.shape[0]
    seq = t // bsz
    nb = seq // ATT_BLK

    def body(s_ref, q_ref, kvc_ref, kvp_ref, do_ref, dq_ref, dkv_ref, dsink_ref, dbq_ref, dbkv_ref, carry):
        j = pl.program_id(0)

        @pl.when(j == 0)
        def _():
            carry[...] = jnp.zeros_like(carry)
            dsink_ref[...] = jnp.zeros_like(dsink_ref)
            dbq_ref[...] = jnp.zeros_like(dbq_ref)
            dbkv_ref[...] = jnp.zeros_like(dbkv_ref)

        valid = _attn_valid(j == nb - 1, STACK)
        lane = lax.broadcasted_iota(jnp.int32, (1, 128), 1)
        dsink = jnp.zeros((1, 128), F32)
        dbq = [jnp.zeros((1, 128), F32)] * HEAD_PAIRS
        dbkv = jnp.zeros((1, 256), F32)
        for b in range(bsz):
            kpair, vpair = _window_kv(kvc_ref.at[b], kvp_ref.at[b])
            dk = jnp.zeros((2 * ATT_BLK, 128), F32)
            dv = jnp.zeros((2 * ATT_BLK, 128), F32)
            dqs = []
            for kh in range(2):
                qs = _stacked(q_ref.at[b], kh, 0.125)
                dos = _stacked(do_ref.at[b], kh, 1.0)
                p, ps = _attn_probs(qs, kpair, _sink_rows(s_ref, kh), valid)
                dp = _nt(dos, vpair)
                delta = _row_sums(p * dp, split=False)
                ds = (p * (dp - _both(delta))).astype(BF16)
                dqs.append(_nn(ds, kpair))
                dk = dk + _tn(ds, qs)
                dv = dv + _tn(p.astype(BF16), dos)
                psd = ps * delta
                for g in range(HEAD_PAIRS):
                    part = jnp.sum(psd[g * ATT_BLK:(g + 1) * ATT_BLK], axis=0, keepdims=True)
                    dsink = dsink - jnp.where(lane == kh * 4 + g, part, 0.0)
            for g in range(HEAD_PAIRS):
                dq = _unstacked(dqs[0], dqs[1], g) * 0.125
                dq_ref[b, :, g * 128:(g + 1) * 128] = dq.astype(BF16)
                dbq[g] = dbq[g] + jnp.sum(dq, axis=0, keepdims=True)
            dkv = jnp.concatenate([dk[ATT_BLK:], dv[ATT_BLK:]], axis=1) + carry[b]
            dkv_ref[b] = dkv.astype(BF16)
            dbkv = dbkv + jnp.sum(dkv, axis=0, keepdims=True)
            carry[b] = jnp.concatenate([dk[:ATT_BLK], dv[:ATT_BLK]], axis=1)
        dsink_ref[...] += dsink
        dbq_ref[...] += jnp.concatenate(dbq, axis=1)
        dbkv_ref[...] += dbkv

    q, kvc, kvp = _attn_specs(bsz, lambda j: nb - 1 - j)
    z3 = z0.reshape(bsz, seq, z0.shape[1])
    d3 = dmix.reshape(bsz, seq, dmix.shape[1])
    dq, dkv, dsink, dbq, dbkv = pl.pallas_call(
        body, name=name, grid=(nb,),
        in_specs=[pl.BlockSpec(memory_space=pltpu.SMEM), q, kvc, kvp,
                  pl.BlockSpec((bsz, ATT_BLK, 512), lambda j: (0, nb - 1 - j, 0))],
        out_specs=[pl.BlockSpec((bsz, ATT_BLK, 512), lambda j: (0, nb - 1 - j, 0)),
                   pl.BlockSpec((bsz, ATT_BLK, 256), lambda j: (0, nb - 1 - j, 0)), _row(128), _row(512), _row(256)],
        out_shape=[jax.ShapeDtypeStruct((bsz, seq, 512), BF16), jax.ShapeDtypeStruct((bsz, seq, 256), BF16),
                   jax.ShapeDtypeStruct((1, 128), F32), jax.ShapeDtypeStruct((1, 512), F32),
                   jax.ShapeDtypeStruct((1, 256), F32)],
        scratch_shapes=[pltpu.VMEM((bsz, ATT_BLK, 256), F32)],
        compiler_params=_params("arbitrary"))(sinks, z3, z3, z3, d3)
    return dq.reshape(t, 512), dkv.reshape(t, 256), dsink, dbq, dbkv


def _seq_specs(ts, nt, t, width, col):
    per = ts // HALO
    cur = pl.BlockSpec((ts, width), lambda b, i: (b * nt + i, col))
    prev = pl.BlockSpec((HALO, width), lambda b, i: (jnp.maximum((b * nt + i) * per - 1, 0), col))
    nxt = pl.BlockSpec((HALO, width), lambda b, i: (jnp.minimum((b * nt + i + 1) * per, t // HALO - 1), col))
    return prev, cur, nxt


SUB = 8
CONV_ROWS = 32


def _shifted_copies(src, sh, rows_first, rows_rest):
    for r in range(SUB):
        rows = rows_first if r == 0 else rows_rest
        sh[r, pl.ds(0, rows), :] = src[pl.ds(r, rows), :]


def _tap_sum(sh, w, offset, c0, rows):
    acc = None
    for k in range(CONV_K):
        o = offset(k)
        term = sh[o % SUB, pl.ds(c0 + o - o % SUB, rows), :] * w[k:k + 1, :]
        acc = term if acc is None else acc + term
    return acc


def _glu_rows(a_ref, g_ref, rows=slice(None)):
    return a_ref[rows, :].astype(F32) * jax.nn.sigmoid(g_ref[rows, :].astype(F32))


def _conv_fwd(z0, conv_w, conv_b, ln_g, ln_b, bsz, name):
    t = z0.shape[0]
    s = t // bsz
    ts = _seq_tile(s)
    nt = s // ts
    first = HALO - (CONV_K - 1)

    def body(ap_ref, ac_ref, gp_ref, gc_ref, w_ref, cb_ref, lg_ref, lb_ref, o_ref, y_ref, hbuf, sh):
        hbuf[0:HALO, :] = jnp.where(pl.program_id(1) > 0, _glu_rows(ap_ref, gp_ref), 0.0)
        hbuf[HALO:HALO + ts, :] = _glu_rows(ac_ref, gc_ref)
        _shifted_copies(hbuf, sh, ts + HALO, ts + HALO - SUB)
        w, cb, lg, lb = w_ref[...], cb_ref[...], lg_ref[...], lb_ref[...]
        for c0 in range(0, ts, CONV_ROWS):
            y = _tap_sum(sh, w, lambda k: first + k, c0, CONV_ROWS) + cb
            y_ref[c0:c0 + CONV_ROWS, :] = y
            o = _ln(y, lg, lb)[0]
            o_ref[c0:c0 + CONV_ROWS, :] = (o * jax.nn.sigmoid(o)).astype(BF16)

    ap, ac, _ = _seq_specs(ts, nt, t, 512, 1)
    gp, gc, _ = _seq_specs(ts, nt, t, 512, 2)
    tile = pl.BlockSpec((ts, 512), lambda b, i: (b * nt + i, 0))
    return pl.pallas_call(
        body, name=name, grid=(bsz, nt),
        in_specs=[ap, ac, gp, gc, _full((HALO, 512)), _row(512), _row(512), _row(512)],
        out_specs=[tile, tile],
        out_shape=[jax.ShapeDtypeStruct((t, 512), BF16), jax.ShapeDtypeStruct((t, 512), F32)],
        scratch_shapes=[pltpu.VMEM((HALO + ts, 512), F32), pltpu.VMEM((SUB, HALO + ts, 512), F32)],
        compiler_params=_params("parallel", "parallel"))(z0, z0, z0, z0, conv_w, conv_b, ln_g, ln_b)


def _conv_bwd(z0, y, dmix, conv_w, ln_g, ln_b, bsz, name):
    t = z0.shape[0]
    s = t // bsz
    ts = _seq_tile(s)
    nt = s // ts

    def body(ac_ref, gc_ref, yc_ref, yn_ref, dc_ref, dn_ref, w_ref, lg_ref, lb_ref,
             da_ref, dg_ref, dw_ref, dcb_ref, dlg_ref, dlb_ref, dba_ref, dbg_ref, hcur, dybuf, sh_dy):
        b, i = pl.program_id(0), pl.program_id(1)

        @pl.when((b == 0) & (i == 0))
        def _():
            for ref in (dw_ref, dcb_ref, dlg_ref, dlb_ref, dba_ref, dbg_ref):
                ref[...] = jnp.zeros_like(ref)

        w, lg, lb = w_ref[...], lg_ref[...], lb_ref[...]
        hcur[...] = _glu_rows(ac_ref, gc_ref)

        def d_conv_out(yv, dout):
            o, xhat, rstd = _ln(yv, lg, lb)
            sg_o = jax.nn.sigmoid(o)
            d_o = dout * sg_o * (1.0 + o * (1.0 - sg_o))
            return _ln_bwd(d_o, xhat, rstd, lg), d_o * xhat, d_o

        dlg = jnp.zeros((1, 512), F32)
        dlb = jnp.zeros((1, 512), F32)
        dcb = jnp.zeros((1, 512), F32)
        for c0 in range(0, ts, CONV_ROWS):
            rows = slice(c0, c0 + CONV_ROWS)
            dy, g_part, b_part = d_conv_out(yc_ref[rows, :], dc_ref[rows, :].astype(F32))
            dybuf[rows, :] = dy
            dlg = dlg + jnp.sum(g_part, axis=0, keepdims=True)
            dlb = dlb + jnp.sum(b_part, axis=0, keepdims=True)
            dcb = dcb + jnp.sum(dy, axis=0, keepdims=True)
        dn = jnp.where(i < nt - 1, dn_ref[...].astype(F32), 0.0)
        dybuf[ts:ts + HALO, :] = d_conv_out(yn_ref[...], dn)[0]
        dlg_ref[...] += dlg
        dlb_ref[...] += dlb
        dcb_ref[...] += dcb
        _shifted_copies(dybuf, sh_dy, ts + HALO - SUB, ts + HALO - SUB)

        for k in range(CONV_K):
            o = CONV_K - 1 - k
            prod = hcur[...] * sh_dy[o % SUB, pl.ds(o - o % SUB, ts), :]
            dw_ref[pl.ds(k, 1), :] += jnp.sum(prod, axis=0, keepdims=True)
        dba = jnp.zeros((1, 512), F32)
        dbg = jnp.zeros((1, 512), F32)
        for c0 in range(0, ts, CONV_ROWS):
            rows = slice(c0, c0 + CONV_ROWS)
            dh = _tap_sum(sh_dy, w, lambda k: CONV_K - 1 - k, c0, CONV_ROWS)
            a_c = ac_ref[rows, :].astype(F32)
            sg_c = jax.nn.sigmoid(gc_ref[rows, :].astype(F32))
            d_a = dh * sg_c
            d_g = dh * a_c * sg_c * (1.0 - sg_c)
            da_ref[rows, :] = d_a.astype(BF16)
            dg_ref[rows, :] = d_g.astype(BF16)
            dba = dba + jnp.sum(d_a, axis=0, keepdims=True)
            dbg = dbg + jnp.sum(d_g, axis=0, keepdims=True)
        dba_ref[...] += dba
        dbg_ref[...] += dbg

    _, ac, _ = _seq_specs(ts, nt, t, 512, 1)
    _, gc, _ = _seq_specs(ts, nt, t, 512, 2)
    _, yc, yn = _seq_specs(ts, nt, t, 512, 0)
    _, dc, dn = _seq_specs(ts, nt, t, 512, 1)
    tile = pl.BlockSpec((ts, 512), lambda b, i: (b * nt + i, 0))
    vec = jax.ShapeDtypeStruct((1, 512), F32)
    return pl.pallas_call(
        body, name=name, grid=(bsz, nt),
        in_specs=[ac, gc, yc, yn, dc, dn, _full((HALO, 512)), _row(512), _row(512)],
        out_specs=[tile, tile, _full((HALO, 512)), _row(512), _row(512), _row(512), _row(512), _row(512)],
        out_shape=[jax.ShapeDtypeStruct((t, 512), BF16), jax.ShapeDtypeStruct((t, 512), BF16),
                   jax.ShapeDtypeStruct((HALO, 512), F32), vec, vec, vec, vec, vec],
        scratch_shapes=[pltpu.VMEM((ts, 512), F32), pltpu.VMEM((ts + HALO, 512), F32),
                        pltpu.VMEM((SUB, HALO + ts, 512), F32)],
        compiler_params=_params("arbitrary", "arbitrary"))(z0, z0, y, y, dmix, dmix, conv_w, ln_g, ln_b)


def _pooled(pbuf, g, ts, tok):
    w = 2 << g
    cols = slice(128 * g, 128 * (g + 1))
    sm = pbuf[pl.ds(HALO, ts), cols]
    for d in range(1, w):
        sm = sm + pbuf[pl.ds(HALO - d, ts), cols]
    cnt = jnp.minimum(tok + 1, w).astype(F32)
    return sm / cnt - pbuf[pl.ds(HALO, ts), cols]


def _pool_fwd(z1, w_pool, scale, bsz, name):
    t = z1.shape[0]
    s = t // bsz
    ts = _seq_tile(s)
    nt = s // ts

    def body(zp_ref, zc_ref, wp_ref, sc_ref, o_ref, token, pbuf):
        token[...] = jnp.zeros_like(token)
        i = pl.program_id(1)
        pbuf[0:HALO, :] = jnp.where(i > 0, zp_ref[...].astype(F32), 0.0)
        pbuf[HALO:HALO + ts, :] = zc_ref[...].astype(F32)
        tok = i * ts + lax.broadcasted_iota(jnp.int32, (ts, 1), 0)
        for g in range(4):
            cols = slice(128 * g, 128 * (g + 1))
            pooled = _pooled(pbuf, g, ts, tok).astype(BF16)
            o_ref[:, cols] = (_nn(pooled, wp_ref[g].astype(BF16)) * sc_ref[:, cols]).astype(BF16)

    zp, zc, _ = _seq_specs(ts, nt, t, 512, 0)
    return pl.pallas_call(
        body, name=name, grid=(bsz, nt), in_specs=[zp, zc, _full((4, 128, 128)), _row(512)],
        out_specs=[pl.BlockSpec((ts, 512), lambda b, i: (b * nt + i, 0)), _full((8, 128))],
        out_shape=[jax.ShapeDtypeStruct((t, 512), BF16), jax.ShapeDtypeStruct((8, 128), F32)],
        scratch_shapes=[pltpu.VMEM((HALO + ts, 512), F32)],
        compiler_params=_params("arbitrary", "arbitrary"))(z1, z1, w_pool, scale)


def _pool_bwd(z1, dmix, w_pool, scale, bsz, name):
    t = z1.shape[0]
    s = t // bsz
    ts = _seq_tile(s)
    nt = s // ts
    rr = ts + HALO

    def body(zp_ref, zc_ref, dc_ref, dn_ref, wp_ref, sc_ref, dz_ref, dwp_ref, dsc_ref, pbuf, ebuf):
        b, i = pl.program_id(0), pl.program_id(1)

        @pl.when((b == 0) & (i == 0))
        def _():
            dwp_ref[...] = jnp.zeros_like(dwp_ref)
            dsc_ref[...] = jnp.zeros_like(dsc_ref)

        pbuf[0:HALO, :] = jnp.where(i > 0, zp_ref[...].astype(F32), 0.0)
        pbuf[HALO:HALO + ts, :] = zc_ref[...].astype(F32)
        dn = jnp.where(i < nt - 1, dn_ref[...].astype(F32), 0.0)
        dout = jnp.concatenate([dc_ref[...].astype(F32), dn], axis=0)
        tok = i * ts + lax.broadcasted_iota(jnp.int32, (ts, 1), 0)
        tok_r = i * ts + lax.broadcasted_iota(jnp.int32, (rr, 1), 0)
        for g in range(4):
            w = 2 << g
            cols = slice(128 * g, 128 * (g + 1))
            wg = wp_ref[g].astype(BF16)
            pooled = _pooled(pbuf, g, ts, tok).astype(BF16)
            dsc_ref[:, cols] += jnp.sum(dout[:ts, cols] * _nn(pooled, wg), axis=0, keepdims=True)
            dy = (dout[:, cols] * sc_ref[:, cols]).astype(BF16)
            dwp_ref[g] += _tn(pooled, dy[:ts])
            dpl = _nt(dy, wg)
            ebuf[...] = dpl / jnp.minimum(tok_r + 1, w).astype(F32)
            dz = ebuf[pl.ds(0, ts), :] - dpl[:ts]
            for d in range(1, w):
                dz = dz + ebuf[pl.ds(d, ts), :]
            dz_ref[:, cols] = dz.astype(BF16)

    zp, zc, _ = _seq_specs(ts, nt, t, 512, 0)
    _, dc, dn = _seq_specs(ts, nt, t, 512, 0)
    return pl.pallas_call(
        body, name=name, grid=(bsz, nt), in_specs=[zp, zc, dc, dn, _full((4, 128, 128)), _row(512)],
        out_specs=[pl.BlockSpec((ts, 512), lambda b, i: (b * nt + i, 0)), _full((4, 128, 128)), _row(512)],
        out_shape=[jax.ShapeDtypeStruct((t, 512), BF16), jax.ShapeDtypeStruct((4, 128, 128), F32),
                   jax.ShapeDtypeStruct((1, 512), F32)],
        scratch_shapes=[pltpu.VMEM((HALO + ts, 512), F32), pltpu.VMEM((rr, 128), F32)],
        compiler_params=_params("arbitrary", "arbitrary"))(z1, z1, dmix, dmix, w_pool, scale)


def _tril():
    r = lax.broadcasted_iota(jnp.int32, (SGU_CHUNK, SGU_CHUNK), 0)
    c = lax.broadcasted_iota(jnp.int32, (SGU_CHUNK, SGU_CHUNK), 1)
    return r >= c


def _sgu_fwd(z1, ln_g, ln_b, w_s, b_rows, name):
    t = z1.shape[0]
    ts = _tile(t)

    def body(zu_ref, zv_ref, lg_ref, lb_ref, ws_ref, bs_ref, o_ref):
        v = _gelu(zv_ref[...].astype(F32))[0]
        vb = _ln(v, lg_ref[...], lb_ref[...])[0].astype(BF16)
        tril = _tril()
        for g in range(4):
            cols = slice(128 * g, 128 * (g + 1))
            wg = jnp.where(tril, ws_ref[g], 0.0).astype(BF16)
            for c in range(ts // SGU_CHUNK):
                rows = slice(SGU_CHUNK * c, SGU_CHUNK * (c + 1))
                mixed = _nn(wg, vb[rows, cols]) + bs_ref[g]
                o_ref[rows, cols] = (_gelu(zu_ref[rows, cols].astype(F32))[0] * mixed).astype(BF16)

    return pl.pallas_call(
        body, name=name, grid=(t // ts,),
        in_specs=[pl.BlockSpec((ts, 512), lambda i: (i, 1)), pl.BlockSpec((ts, 512), lambda i: (i, 2)),
                  _row(512), _row(512), _full((4, 128, 128)), _full((4, 128, 128))],
        out_specs=pl.BlockSpec((ts, 512), lambda i: (i, 0)), out_shape=jax.ShapeDtypeStruct((t, 512), BF16),
        compiler_params=_params("parallel"))(z1, z1, ln_g, ln_b, w_s, b_rows)


def _sgu_bwd(z1, dmix, ln_g, ln_b, w_s, b_rows, name):
    t = z1.shape[0]
    ts = _tile(t)

    def body(zu_ref, zv_ref, d_ref, lg_ref, lb_ref, ws_ref, bs_ref,
             dzu_ref, dzv_ref, dws_ref, dbs_ref, dlg_ref, dlb_ref, dvbuf):
        @pl.when(pl.program_id(0) == 0)
        def _():
            for ref in (dws_ref, dbs_ref, dlg_ref, dlb_ref):
                ref[...] = jnp.zeros_like(ref)

        zv = zv_ref[...].astype(F32)
        v, thv = _gelu(zv)
        lg = lg_ref[...]
        vln, xhat, rstd = _ln(v, lg, lb_ref[...])
        vb = vln.astype(BF16)
        tril = _tril()
        for g in range(4):
            cols = slice(128 * g, 128 * (g + 1))
            wg = jnp.where(tril, ws_ref[g], 0.0).astype(BF16)
            dws = jnp.zeros((SGU_CHUNK, SGU_CHUNK), F32)
            dbs = jnp.zeros((1, SGU_CHUNK), F32)
            for c in range(ts // SGU_CHUNK):
                rows = slice(SGU_CHUNK * c, SGU_CHUNK * (c + 1))
                vbc = vb[rows, cols]
                mixed = _nn(wg, vbc) + bs_ref[g]
                zu = zu_ref[rows, cols].astype(F32)
                u, thu = _gelu(zu)
                dout = d_ref[rows, cols].astype(F32)
                dzu_ref[rows, cols] = (dout * mixed * _gelu_grad(zu, thu)).astype(BF16)
                dm = dout * u
                dmb = dm.astype(BF16)
                dws = dws + _nt(dmb, vbc)
                dbs = dbs + jnp.sum(dm.T, axis=0, keepdims=True)
                dvbuf[rows, cols] = _tn(wg, dmb)
            dws_ref[g] += jnp.where(tril, dws, 0.0)
            dbs_ref[pl.ds(g, 1), :] += dbs
        dvln = dvbuf[...]
        dlg_ref[...] += jnp.sum(dvln * xhat, axis=0, keepdims=True)
        dlb_ref[...] += jnp.sum(dvln, axis=0, keepdims=True)
        dzv_ref[...] = (_ln_bwd(dvln, xhat, rstd, lg) * _gelu_grad(zv, thv)).astype(BF16)

    tile = pl.BlockSpec((ts, 512), lambda i: (i, 0))
    vec = jax.ShapeDtypeStruct((1, 512), F32)
    return pl.pallas_call(
        body, name=name, grid=(t // ts,),
        in_specs=[pl.BlockSpec((ts, 512), lambda i: (i, 1)), pl.BlockSpec((ts, 512), lambda i: (i, 2)),
                  pl.BlockSpec((ts, 512), lambda i: (i, 1)), _row(512), _row(512), _full((4, 128, 128)),
                  _full((4, 128, 128))],
        out_specs=[tile, tile, _full((4, 128, 128)), _full((4, 128)), _row(512), _row(512)],
        out_shape=[jax.ShapeDtypeStruct((t, 512), BF16), jax.ShapeDtypeStruct((t, 512), BF16),
                   jax.ShapeDtypeStruct((4, 128, 128), F32), jax.ShapeDtypeStruct((4, 128), F32), vec, vec],
        scratch_shapes=[pltpu.VMEM((ts, 512), F32)],
        compiler_params=_params("arbitrary"))(z1, z1, dmix, ln_g, ln_b, w_s, b_rows)


def _row_tile(r):
    for cand in (512, 352, 256, 192, 128, 64, 32, 16, 8):
        if r % cand == 0:
            return cand
    return r


def _sum_slabs(a, name):
    k, r, c = a.shape
    tr = _row_tile(r)

    def body(*refs):
        acc = refs[0][...].astype(F32)
        for ref in refs[1:-1]:
            acc = acc + ref[...].astype(F32)
        refs[-1][...] = acc

    in_specs = [pl.BlockSpec((None, tr, c), functools.partial(lambda i, s: (s, i, 0), s=s)) for s in range(k)]
    return pl.pallas_call(
        body, name=name, grid=(r // tr,), in_specs=in_specs, out_specs=pl.BlockSpec((tr, c), lambda i: (i, 0)),
        out_shape=jax.ShapeDtypeStruct((r, c), F32), compiler_params=_params("parallel"))(*([a] * k))


def _adamw_math(w, g, m, v):
    mn = ADAM_B1 * m + (1.0 - ADAM_B1) * g
    vn = ADAM_B2 * v + (1.0 - ADAM_B2) * (g * g)
    m_hat = mn / (1.0 - ADAM_B1 ** ADAM_STEP)
    v_hat = vn / (1.0 - ADAM_B2 ** ADAM_STEP)
    return -ADAM_LR * (m_hat / (jnp.sqrt(v_hat) + ADAM_EPS) + ADAM_WD * w), mn, vn


def _reduce_adamw(landing, w, m, v, name, layer=None, into=None):
    k, r, c = landing.shape
    tr = _row_tile(r)
    n_into = 0 if into is None else 4

    def body(*refs):
        slabs, (w_ref, m_ref, v_ref) = refs[:k], refs[k:k + 3]
        g_ref, d_ref, mo_ref, vo_ref = refs[k + 3 + n_into:]
        g = slabs[0][...].astype(F32)
        for ref in slabs[1:]:
            g = g + ref[...].astype(F32)
        g_ref[...] = g
        d_ref[...], mo_ref[...], vo_ref[...] = _adamw_math(w_ref[...], g, m_ref[...], v_ref[...])

    if layer is None:
        spec = pl.BlockSpec((tr, c), lambda i: (i, 0))
    else:
        spec = pl.BlockSpec((None, tr, c), lambda i: (layer, i, 0))
    in_specs = [pl.BlockSpec((None, tr, c), functools.partial(lambda i, s: (s, i, 0), s=s)) for s in range(k)]
    in_specs += [spec] * 3 + [ANY] * n_into
    shape = jax.ShapeDtypeStruct(w.shape, F32)
    return pl.pallas_call(
        body, name=name, grid=(r // tr,), in_specs=in_specs, out_specs=[spec] * 4, out_shape=[shape] * 4,
        input_output_aliases={k + 3 + j: j for j in range(n_into)},
        compiler_params=_params("parallel"))(*([landing] * k), w, m, v, *(into or ()))


def _rows_needed(shape):
    return shape[0] * -(-shape[1] // D)


def _as_rows(a):
    r, c = a.shape
    n = -(-c // D)
    assert r == 1 or n == 1
    return jnp.pad(a, ((0, 0), (0, n * D - c))).reshape(r * n, D)


def _adamw_replicated(g_rows, w, m, v, name):
    n = len(REP_2D)

    def body(*refs):
        g_ref, w_refs, m_refs, v_refs, outs = refs[0], refs[1:1 + n], refs[1 + n:1 + 2 * n], refs[1 + 2 * n:1 + 3 * n], refs[1 + 3 * n:]
        r0 = 0
        for k, (_, (r, c)) in enumerate(REP_2D):
            pieces = [g_ref[r0 + j * r:r0 + j * r + r, 0:min(D, c - j * D)] for j in range(-(-c // D))]
            g = pieces[0] if len(pieces) == 1 else jnp.concatenate(pieces, axis=1)
            outs[4 * k][...] = g
            outs[4 * k + 1][...], outs[4 * k + 2][...], outs[4 * k + 3][...] = _adamw_math(
                w_refs[k][...], g, m_refs[k][...], v_refs[k][...])
            r0 += _rows_needed((r, c))

    shapes = [s for _, s in REP_2D]
    return pl.pallas_call(
        body, name=name, in_specs=[_full(g_rows.shape)] + [_full(s) for s in shapes] * 3,
        out_specs=[_full(s) for s in shapes for _ in range(4)],
        out_shape=[jax.ShapeDtypeStruct(s, F32) for s in shapes for _ in range(4)],
        grid=(1,), compiler_params=_params("arbitrary"))(g_rows, *w, *m, *v)


def _adamw(w, g, m, v, name):
    r, c = w.shape
    tr = _row_tile(r)

    def body(w_ref, g_ref, m_ref, v_ref, d_ref, mo_ref, vo_ref):
        d_ref[...], mo_ref[...], vo_ref[...] = _adamw_math(w_ref[...], g_ref[...], m_ref[...], v_ref[...])

    spec = pl.BlockSpec((tr, c), lambda i: (i, 0))
    shape = jax.ShapeDtypeStruct((r, c), F32)
    return pl.pallas_call(
        body, name=name, grid=(r // tr,), in_specs=[spec] * 4, out_specs=[spec] * 3, out_shape=[shape] * 3,
        compiler_params=_params("parallel"))(w, g, m, v)


ANY = pl.BlockSpec(memory_space=pl.ANY)


def _all_gather(block, name):
    r, c_dim = block.shape

    def body(x_ref, out_ref, token, send_sems, recv_sems, local_sem):
        token[...] = jnp.zeros_like(token)
        x, y, c = lax.axis_index("x"), lax.axis_index("y"), lax.axis_index("c")
        me, sibling = (x, y, c), (x, y, 1 - c)
        chips = [(1 - x, y), (x, 1 - y), (1 - x, 1 - y)]

        def rows(px, py, pc):
            return out_ref.at[4 * px + 2 * py + pc]

        def copy(k, blk, to, src=None):
            return pltpu.make_async_remote_copy(
                src_ref=rows(*blk) if src is None else src, dst_ref=rows(*blk), send_sem=send_sems.at[k],
                recv_sem=recv_sems.at[k], device_id=to, device_id_type=MESH)

        mine = pltpu.make_async_copy(x_ref, rows(*me), local_sem)
        mine.start()
        first = [copy(0, me, sibling, src=x_ref)]
        first += [copy(1 + j, me, (*chip, c), src=x_ref) for j, chip in enumerate(chips)]
        for cp in first:
            cp.start()
        passed = [copy(4 + j, (*chip, c), sibling) for j, chip in enumerate(chips)]
        for j, chip in enumerate(chips):
            copy(1 + j, (*chip, c), me).wait_recv()
            passed[j].start()
        copy(0, sibling, me).wait_recv()
        for j, chip in enumerate(chips):
            copy(4 + j, (*chip, 1 - c), me).wait_recv()
        for cp in first + passed:
            cp.wait_send()
        mine.wait()

    return pl.pallas_call(
        body, name=name, in_specs=[ANY], out_specs=[ANY, pl.BlockSpec(memory_space=pltpu.VMEM)],
        out_shape=[jax.ShapeDtypeStruct((N_DEV, r, c_dim), block.dtype), jax.ShapeDtypeStruct((8, 128), F32)],
        scratch_shapes=[pltpu.SemaphoreType.DMA((7,)), pltpu.SemaphoreType.DMA((7,)), pltpu.SemaphoreType.DMA],
    )(block)


HBM = pl.BlockSpec(memory_space=pltpu.HBM)
SEM = pl.BlockSpec(memory_space=pltpu.SEMAPHORE)
EFFECT = pltpu.SideEffectType.DATAFLOW_SIDE_EFFECTING


def _exchange_copies(scatter, src_refs, land_refs, send_sems, recv_sems, local_sems):
    x, y, c = lax.axis_index("x"), lax.axis_index("y"), lax.axis_index("c")
    me = 4 * x + 2 * y + c
    sends, arrivals, locals_ = [], [], []
    for a, (src, land) in enumerate(zip(src_refs, land_refs)):
        def pick(idx, src=src):
            return src.at[idx] if scatter else src

        locals_.append(pltpu.make_async_copy(pick(me), land.at[me], local_sems.at[a]))
        for r in range(1, N_DEV):
            px = 1 - x if r & 4 else x
            py = 1 - y if r & 2 else y
            pc = 1 - c if r & 1 else c
            peer, s = 4 * px + 2 * py + pc, 7 * a + r - 1
            sends.append(pltpu.make_async_remote_copy(
                src_ref=pick(peer), dst_ref=land.at[me], send_sem=send_sems.at[s], recv_sem=recv_sems.at[s],
                device_id=(px, py, pc), device_id_type=MESH))
            arrivals.append(pltpu.make_async_remote_copy(
                src_ref=pick(peer), dst_ref=land.at[peer], send_sem=send_sems.at[s], recv_sem=recv_sems.at[s],
                device_id=(px, py, pc), device_id_type=MESH))
    return sends, arrivals, locals_


def _exchange_start(srcs, scatter, name):
    n = len(srcs)
    lands = [lax.empty((N_DEV,) + s.shape[-2:], s.dtype) for s in srcs]

    def body(*refs):
        src_refs, land_refs = refs[:n], refs[n:2 * n]
        send_sems, recv_sems, local_sems = refs[2 * n:2 * n + 3]
        token = refs[-1]
        sends, _, locals_ = _exchange_copies(scatter, src_refs, land_refs, send_sems, recv_sems, local_sems)
        for cp in locals_ + sends:
            cp.start()
        token[...] = jnp.zeros_like(token)

    res = pl.pallas_call(
        body, name=name,
        out_shape=[pltpu.SemaphoreType.DMA((7 * n,)), pltpu.SemaphoreType.DMA((7 * n,)), pltpu.SemaphoreType.DMA((n,))]
        + [pltpu.HBM(a.shape, a.dtype) for a in list(srcs) + lands] + [jax.ShapeDtypeStruct((8, 128), F32)],
        in_specs=[HBM] * (2 * n), out_specs=[SEM] * 3 + [HBM] * (2 * n) + [pl.BlockSpec(memory_space=pltpu.VMEM)],
        input_output_aliases={i: 3 + i for i in range(2 * n)},
        compiler_params=pltpu.CompilerParams(has_side_effects=EFFECT),
    )(*[pltpu.with_memory_space_constraint(a, pltpu.HBM) for a in list(srcs) + lands])
    return (n, scatter, res[:3], res[3:3 + 2 * n]), res[-1]


def _exchange_wait(handle, after, name):
    n, scatter, sems, thru = handle

    def body(*refs):
        src_refs, land_refs = refs[:n], refs[n:2 * n]
        send_sems, recv_sems, local_sems = refs[2 * n:2 * n + 3]
        sends, arrivals, locals_ = _exchange_copies(scatter, src_refs, land_refs, send_sems, recv_sems, local_sems)
        for cp in arrivals:
            cp.wait_recv()
        for cp in sends:
            cp.wait_send()
        for cp in locals_:
            cp.wait()

    res = pl.pallas_call(
        body, name=name, out_shape=[pltpu.HBM(a.shape, a.dtype) for a in thru],
        in_specs=[HBM] * (2 * n) + [SEM] * 3 + [ANY], out_specs=[HBM] * (2 * n),
        input_output_aliases={i: i for i in range(2 * n)},
        compiler_params=pltpu.CompilerParams(has_side_effects=EFFECT),
    )(*thru, *sems, after)
    return res[n:]


def _behind(tokens, a):
    zero = sum(tok[0, 0] for tok in tokens)
    return jax.tree.map(lambda v: v + zero.astype(v.dtype), a)


def _perm_heads(a, perm, axis):
    idx = [slice(None)] * a.ndim
    parts = []
    for h in perm:
        idx[axis] = slice(64 * h, 64 * (h + 1))
        parts.append(a[tuple(idx)])
    idx[axis] = slice(512, None)
    if a.shape[axis] > 512:
        parts.append(a[tuple(idx)])
    return jnp.concatenate(parts, axis=axis)


Q_INV = tuple(int(i) for i in np.argsort(Q_PERM))


def _in0_to_kernel(a, axis):
    a = _perm_heads(a, Q_PERM, axis)
    idx = [slice(None)] * a.ndim

    def cut(lo, hi):
        idx[axis] = slice(lo, hi)
        return a[tuple(idx)]

    return jnp.concatenate([cut(0, 512), cut(768, 1792), cut(512, 768)], axis=axis)


def _in0_from_kernel(a, axis):
    idx = [slice(None)] * a.ndim

    def cut(lo, hi):
        idx[axis] = slice(lo, hi)
        return a[tuple(idx)]

    a = jnp.concatenate([cut(0, 512), cut(1536, 1792), cut(512, 1536)], axis=axis)
    return _perm_heads(a, Q_INV, axis)


def _f32_as_u16_rows(v, rows):
    bits = lax.bitcast_convert_type(v, jnp.uint16).reshape(-1)
    return jnp.pad(bits, (0, rows * D - bits.shape[0])).reshape(rows, D)


def _pad_rows(v, rows):
    v = v.reshape(-1)
    return jnp.pad(v, (0, rows * D - v.shape[0])).reshape(rows, D)


def kernel(x, mix_norm, a_w_in, a_b_in, a_sinks, a_conv_w, a_conv_b, a_cln_g, a_cln_b, a_w_out, c_w_in, c_w_pool, c_pool_scale, c_sln_g, c_sln_b, c_w_s, c_b_s, c_w_out, ffn_norm, ffn_w_gate, ffn_w_up, ffn_w_down, final_norm, loss_target, m_mix_norm, m_a_w_in, m_a_b_in, m_a_sinks, m_a_conv_w, m_a_conv_b, m_a_cln_g, m_a_cln_b, m_a_w_out, m_c_w_in, m_c_w_pool, m_c_pool_scale, m_c_sln_g, m_c_sln_b, m_c_w_s, m_c_b_s, m_c_w_out, m_ffn_norm, m_ffn_w_gate, m_ffn_w_up, m_ffn_w_down, m_final_norm, v_mix_norm, v_a_w_in, v_a_b_in, v_a_sinks, v_a_conv_w, v_a_conv_b, v_a_cln_g, v_a_cln_b, v_a_w_out, v_c_w_in, v_c_w_pool, v_c_pool_scale, v_c_sln_g, v_c_sln_b, v_c_w_s, v_c_b_s, v_c_w_out, v_ffn_norm, v_ffn_w_gate, v_ffn_w_up, v_ffn_w_down, v_final_norm):
    bsz, seq, _ = x.shape
    t = bsz * seq
    w_in = dict(mix_norm=mix_norm, a_w_in=a_w_in, a_b_in=a_b_in, a_sinks=a_sinks, a_conv_w=a_conv_w, a_conv_b=a_conv_b,
                a_cln_g=a_cln_g, a_cln_b=a_cln_b, a_w_out=a_w_out, c_w_in=c_w_in, c_w_pool=c_w_pool,
                c_pool_scale=c_pool_scale, c_sln_g=c_sln_g, c_sln_b=c_sln_b, c_w_s=c_w_s, c_b_s=c_b_s, c_w_out=c_w_out,
                ffn_norm=ffn_norm, ffn_w_gate=ffn_w_gate, ffn_w_up=ffn_w_up, ffn_w_down=ffn_w_down, final_norm=final_norm)
    m_in = dict(mix_norm=m_mix_norm, a_w_in=m_a_w_in, a_b_in=m_a_b_in, a_sinks=m_a_sinks, a_conv_w=m_a_conv_w,
                a_conv_b=m_a_conv_b, a_cln_g=m_a_cln_g, a_cln_b=m_a_cln_b, a_w_out=m_a_w_out, c_w_in=m_c_w_in,
                c_w_pool=m_c_w_pool, c_pool_scale=m_c_pool_scale, c_sln_g=m_c_sln_g, c_sln_b=m_c_sln_b, c_w_s=m_c_w_s,
                c_b_s=m_c_b_s, c_w_out=m_c_w_out, ffn_norm=m_ffn_norm, ffn_w_gate=m_ffn_w_gate, ffn_w_up=m_ffn_w_up,
                ffn_w_down=m_ffn_w_down, final_norm=m_final_norm)
    v_in = dict(mix_norm=v_mix_norm, a_w_in=v_a_w_in, a_b_in=v_a_b_in, a_sinks=v_a_sinks, a_conv_w=v_a_conv_w,
                a_conv_b=v_a_conv_b, a_cln_g=v_a_cln_g, a_cln_b=v_a_cln_b, a_w_out=v_a_w_out, c_w_in=v_c_w_in,
                c_w_pool=v_c_w_pool, c_pool_scale=v_c_pool_scale, c_sln_g=v_c_sln_g, c_sln_b=v_c_sln_b, c_w_s=v_c_w_s,
                c_b_s=v_c_b_s, c_w_out=v_c_w_out, ffn_norm=v_ffn_norm, ffn_w_gate=v_ffn_w_gate, ffn_w_up=v_ffn_w_up,
                ffn_w_down=v_ffn_w_down, final_norm=v_final_norm)

    small = jnp.concatenate([a_conv_w[0].reshape(-1), c_pool_scale[0], c_sln_g[0], c_sln_b[0]])
    first_bits = lax.bitcast_convert_type(a_w_in[0].T.astype(BF16), jnp.uint16)
    gathered, tok = _all_gather(jnp.concatenate([first_bits, _f32_as_u16_rows(small, W_MISC_ROWS)], axis=0), "gather_mixer0")

    def ffn_shards(l):
        return [ffn_w_gate[l].T.astype(BF16), ffn_w_up[l].T.astype(BF16), ffn_w_down[l].astype(BF16)]

    ffn0_h, tok = _exchange_start(_behind([tok], ffn_shards(0) + [a_w_out[0].astype(BF16)]), False, "gather_ffn0_start")
    mix1_h, tok = _exchange_start(_behind([tok], [c_w_in[0].T.astype(BF16), c_w_out[0].astype(BF16)]), False,
                                  "gather_mixer1_start")
    ffn1_h, tok = _exchange_start(_behind([tok], ffn_shards(1)), False, "gather_ffn1_start")

    a_in_full = lax.bitcast_convert_type(gathered[:, :224].reshape(IN0, D), BF16)
    small_all = lax.bitcast_convert_type(
        gathered[:, 224:].reshape(N_DEV, -1)[:, :2 * SMALL_SHARD].reshape(N_DEV, SMALL_SHARD, 2), F32)
    conv_w = small_all[:, :31 * 64].reshape(N_DEV, 31, 64).transpose(1, 0, 2).reshape(31, 512)
    conv_w = jnp.pad(conv_w, ((0, HALO - CONV_K), (0, 0)))
    pool_scale = small_all[:, 31 * 64:31 * 64 + 64].reshape(1, 512)
    sln_g = small_all[:, 31 * 64 + 64:31 * 64 + 128].reshape(1, 512)
    sln_b = small_all[:, 31 * 64 + 128:].reshape(1, 512)

    wt_in0 = _in0_to_kernel(a_in_full, 0)
    b_in0 = _in0_to_kernel(a_b_in, 1)
    b_rows = jnp.broadcast_to(c_b_s[0][:, :, None], (4, 128, 128))
    conv_b, cln_g, cln_b = a_conv_b, a_cln_g, a_cln_b

    h0 = x.reshape(t, D)
    target = loss_target.reshape(t, D)
    z0, hn0 = _norm_proj(h0, _behind([tok], mix_norm[0:1]), wt_in0, b_in0, "in_proj0")
    attn, tok = _attn_fwd(z0, a_sinks, bsz, "attn_fwd")
    conv, conv_y = _conv_fwd(z0, conv_w, conv_b, cln_g, _behind([tok], cln_b), bsz, "conv_fwd")
    wtg0, wtu0, wd0, a_out_full = (w.reshape(-1, D) for w in _exchange_wait(ffn0_h, conv, "gather_ffn0_wait"))
    w_out0 = _perm_heads(a_out_full, Q_PERM, 0)
    h1, hnf0, gate0a, up0a, part = _ffn_fwd_a(h0, attn, conv, w_out0, ffn_norm[0:1], wtg0, wtu0, wd0, "ffn_fwd0a")
    gate0b, up0b, h2 = _ffn_fwd_b(hnf0, part, wtg0, wtu0, wd0, "ffn_fwd0b")
    wt_in1, w_out1 = (w.reshape(-1, D) for w in _exchange_wait(mix1_h, h2, "gather_mixer1_wait"))
    z1, hn1 = _norm_proj(h2, mix_norm[1:2], wt_in1, None, "in_proj1")
    pool, tok = _pool_fwd(z1, c_w_pool[0], pool_scale, bsz, "pool_fwd")
    sgu = _sgu_fwd(z1, sln_g, _behind([tok], sln_b), c_w_s[0], b_rows, "sgu_fwd")
    wtg1, wtu1, wd1 = (w.reshape(D_FF, D) for w in _exchange_wait(ffn1_h, sgu, "gather_ffn1_wait"))
    h3, hnf1, gate1a, up1a, part = _ffn_fwd_a(h2, pool, sgu, w_out1, ffn_norm[1:2], wtg1, wtu1, wd1, "ffn_fwd1a")
    gate1b, up1b, dh4, d_final_norm, loss_part = _ffn_fwd_b(hnf1, part, wtg1, wtu1, wd1, "ffn_fwd1b",
                                                            head=(final_norm.reshape(1, D), target))

    def blocks(g):
        return g.reshape(N_DEV, g.shape[0] // N_DEV, D)

    *wide, part = _ffn_bwd_a(dh4, gate1a, up1a, wtg1, wtu1, wd1, "ffn_bwd1a")
    dh3, dmix1, dgate1, dup1, act1, d_fn1 = _ffn_bwd_b(dh4, h3, ffn_norm[1:2], gate1b, up1b, wtg1, wtu1, wd1, w_out1, part,
                                                       wide, "ffn_bwd1b")
    gw_ffn1 = [_mm_tn(dgate1, hnf1, "dw_gate1"), _mm_tn(dup1, hnf1, "dw_up1"), _mm_tn(act1, dh4, "dw_down1")]
    ffn1_g, tok = _exchange_start([blocks(g) for g in gw_ffn1], True, "scatter_ffn1_start")
    gw_c_out = _mm_tn_pieces([pool, sgu], dh3, "dw_out1")
    dzp, d_w_pool, d_pool_scale = _pool_bwd(z1, dmix1, c_w_pool[0], _behind([tok], pool_scale), bsz, "pool_bwd")
    dzu, dzv, d_w_s, d_b_s, d_sln_g, d_sln_b = _sgu_bwd(z1, dmix1, sln_g, sln_b, c_w_s[0], b_rows, "sgu_bwd")
    dh2, d_mn1 = _proj_bwd_norm([(dzp, 0), (dzu, 512), (dzv, 1024)], wt_in1, h2, dh3, mix_norm[1:2], BF16, "in_proj1_bwd")
    gw_c_in = _mm_tn_pieces([dzp, dzu, dzv], hn1, "dw_in1")
    mix1_g, tok = _exchange_start([blocks(gw_c_in), blocks(gw_c_out)], True, "scatter_mixer1_start")
    *wide, part = _ffn_bwd_a(dh2, gate0a, up0a, wtg0, wtu0, wd0, "ffn_bwd0a")
    dh1, dmix0, dgate0, dup0, act0, d_fn0 = _ffn_bwd_b(dh2, h1, _behind([tok], ffn_norm[0:1]), gate0b, up0b, wtg0, wtu0, wd0,
                                                       w_out0, part, wide, "ffn_bwd0b")
    gw_ffn0 = [_mm_tn(dgate0, hnf0, "dw_gate0"), _mm_tn(dup0, hnf0, "dw_up0"), _mm_tn(act0, dh2, "dw_down0")]
    ffn0_g, tok = _exchange_start([blocks(g) for g in gw_ffn0], True, "scatter_ffn0_start")
    gw_a_out = _perm_heads(_mm_tn_pieces([attn, conv], dh1, "dw_out0"), Q_INV, 0)
    dq, dkv, d_sink_row, d_bq, d_bkv = _attn_bwd(z0, dmix0, _behind([tok], a_sinks), bsz, "attn_bwd")
    dca, dcg, d_conv_w, d_conv_b, d_cln_g, d_cln_b, d_ba, d_bg = _conv_bwd(z0, conv_y, dmix0, conv_w, cln_g, cln_b, bsz, "conv_bwd")
    gw_a_in = _in0_from_kernel(_mm_tn_pieces([dq, dca, dcg, dkv], hn0, "dw_in0"), 0)
    mix0_g, tok = _exchange_start([blocks(gw_a_in), blocks(gw_a_out)], True, "scatter_mixer0_start")
    dx, d_mn0 = _proj_bwd_norm([(dq, 0), (dca, 512), (dcg, 1024), (dkv, 1536)], wt_in0, h0, dh1,
                               _behind([tok], mix_norm[0:1]), F32, "in_proj0_bwd")
    d_b_in = _in0_from_kernel(jnp.concatenate([d_bq, d_ba, d_bg, d_bkv], axis=1), 1)

    rep = dict(mix_norm=jnp.concatenate([d_mn0, d_mn1], axis=0), a_b_in=d_b_in, a_sinks=d_sink_row[:, :8],
               a_conv_b=d_conv_b, a_cln_g=d_cln_g, a_cln_b=d_cln_b, c_w_pool=d_w_pool.reshape(64, D),
               c_w_s=d_w_s.reshape(64, D), c_b_s=d_b_s, ffn_norm=jnp.concatenate([d_fn0, d_fn1], axis=0),
               final_norm=d_final_norm)
    rep_rows = jnp.concatenate([_as_rows(rep[nm]) for nm, _ in REP_2D] + [_as_rows(loss_part)], axis=0)
    rep_flat = jnp.pad(rep_rows, ((0, N_DEV * REP_ROWS - rep_rows.shape[0]), (0, 0))).reshape(N_DEV, REP_ROWS, D)
    small_g = jnp.concatenate([
        d_conv_w[:CONV_K].reshape(31, N_DEV, 64).transpose(1, 0, 2).reshape(N_DEV, 31 * 64),
        d_pool_scale.reshape(N_DEV, 64), d_sln_g.reshape(N_DEV, 64), d_sln_b.reshape(N_DEV, 64)], axis=1)
    small_g = jnp.pad(small_g, ((0, 0), (0, G_SMALL_ROWS * D - SMALL_SHARD))).reshape(N_DEV, G_SMALL_ROWS, D)
    tail_g, tok = _exchange_start([jnp.concatenate([small_g, rep_flat], axis=1)], True, "scatter_tail_start")

    names = list(w_in)
    g_out, delta, new_m, new_v = {}, {}, {}, {}
    column_sharded = ("a_w_in", "c_w_in", "ffn_w_gate", "ffn_w_up")

    def rows_of(a, nm):
        return jnp.swapaxes(a, 1, 2) if nm in column_sharded else a

    def reduce_adamw(nm, landing, layer, into=None):
        args = [rows_of(d[nm], nm) for d in (w_in, m_in, v_in)]
        if args[0].shape[0] == 1:
            args, layer = [a[0] for a in args], None
        return _reduce_adamw(landing, *args, "adamw_%s_%s" % (nm, layer), layer=layer, into=into)

    def keep(nm, res):
        res = [r if r.ndim == 3 else r[None] for r in res]
        g_out[nm], delta[nm], new_m[nm], new_v[nm] = (rows_of(r, nm) for r in res)

    ffn_names = ("ffn_w_gate", "ffn_w_up", "ffn_w_down")
    landed = _exchange_wait(ffn1_g, tok, "scatter_ffn1_wait")
    ffn_res = [reduce_adamw(nm, a, 1) for nm, a in zip(ffn_names, landed)]
    landed = _exchange_wait(mix1_g, ffn_res[-1][0], "scatter_mixer1_wait")
    for nm, a in zip(("c_w_in", "c_w_out"), landed):
        keep(nm, reduce_adamw(nm, a, 0))
    landed = _exchange_wait(ffn0_g, g_out["c_w_out"], "scatter_ffn0_wait")
    for nm, a, res in zip(ffn_names, landed, ffn_res):
        keep(nm, reduce_adamw(nm, a, 0, into=res))
    landed = _exchange_wait(mix0_g, g_out["ffn_w_down"], "scatter_mixer0_wait")
    for nm, a in zip(("a_w_in", "a_w_out"), landed):
        keep(nm, reduce_adamw(nm, a, 0))
    g_tail = _sum_slabs(_exchange_wait(tail_g, g_out["a_w_out"], "scatter_tail_wait")[0], "sum_tail")
    rep_all = _all_gather(g_tail[G_SMALL_ROWS:], "gather_replicated_grads")[0].reshape(N_DEV * REP_ROWS, D)
    loss = rep_all[sum(_rows_needed(s) for _, s in REP_2D), 0]
    res = _adamw_replicated(rep_all, *[[d[nm].reshape(s) for nm, s in REP_2D] for d in (w_in, m_in, v_in)], "adamw_replicated")
    for k, (nm, _) in enumerate(REP_2D):
        g_out[nm], delta[nm], new_m[nm], new_v[nm] = (r.reshape(w_in[nm].shape) for r in res[4 * k:4 * k + 4])
    small_r = g_tail[:G_SMALL_ROWS].reshape(-1)[:SMALL_SHARD]
    g_out.update(
        a_conv_w=small_r[:31 * 64].reshape(1, 31, 64), c_pool_scale=small_r[31 * 64:31 * 64 + 64].reshape(1, 64),
        c_sln_g=small_r[31 * 64 + 64:31 * 64 + 128].reshape(1, 64), c_sln_b=small_r[31 * 64 + 128:].reshape(1, 64))
    group = ("a_conv_w", "c_pool_scale", "c_sln_g", "c_sln_b")
    flat = [_pad_rows(jnp.concatenate([d[nm].reshape(-1) for nm in group]), G_SMALL_ROWS) for d in (w_in, g_out, m_in, v_in)]
    res = [r.reshape(-1) for r in _adamw(*flat, "adamw_small_sharded")]
    off = 0
    for nm in group:
        n = int(np.prod(w_in[nm].shape))
        delta[nm], new_m[nm], new_v[nm] = (r[off:off + n].reshape(w_in[nm].shape) for r in res)
        off += n

    grad_x = dx.reshape(bsz, seq, D)
    return (loss, grad_x, *[g_out[nm] for nm in names], *[delta[nm] for nm in names],
            *[new_m[nm] for nm in names], *[new_v[nm] for nm in names])
```

```python
import functools

import jax
import jax.numpy as jnp
import numpy as np
from jax import lax
from jax.experimental import pallas as pl
from jax.experimental.pallas import tpu as pltpu

F32 = jnp.float32
BF16 = jnp.bfloat16
MESH = pl.DeviceIdType.MESH

D = 1024
N_DEV = 8
EPS = 1e-5
HEAD_PAIRS = 4
ATT_BLK = 128
CONV_K = 31
HALO = 32
D_FF = 2816
IN0 = 1792
SGU_CHUNK = 128
GELU_C = 0.7978845608028654
GELU_A = 0.044715
ADAM_LR, ADAM_B1, ADAM_B2, ADAM_EPS, ADAM_WD, ADAM_STEP = 0.001, 0.9, 0.999, 1e-08, 0.01, 10
VMEM_LIMIT = 56 << 20

SMALL_SHARD = 31 * 64 + 3 * 64
W_MISC_ROWS = 16
G_MISC_ROWS = 32
G_SMALL_ROWS = 8
REP_ROWS = G_MISC_ROWS - G_SMALL_ROWS
REP_2D = (("c_w_pool", (64, 1024)), ("c_w_s", (64, 1024)), ("mix_norm", (2, 1024)), ("a_b_in", (1, 1792)), ("a_sinks", (1, 8)),
          ("a_conv_b", (1, 512)), ("a_cln_g", (1, 512)), ("a_cln_b", (1, 512)), ("c_b_s", (4, 128)), ("ffn_norm", (2, 1024)),
          ("final_norm", (1, 1024)))
Q_PERM = (0, 4, 1, 5, 2, 6, 3, 7)


def _params(*sem):
    return pltpu.CompilerParams(dimension_semantics=sem, vmem_limit_bytes=VMEM_LIMIT)


def _nn(a, b):
    return jnp.dot(a, b, preferred_element_type=F32)


def _nt(a, b):
    return lax.dot_general(a, b, (((1,), (1,)), ((), ())), preferred_element_type=F32)


def _tn(a, b):
    return lax.dot_general(a, b, (((0,), (0,)), ((), ())), preferred_element_type=F32)


def _tile(n, want=512):
    t = min(want, n)
    assert n % t == 0, (n, t)
    return t


def _seq_tile(s):
    return 512 if s >= 1024 else s // 2


def _rms(x, g):
    r = lax.rsqrt(jnp.mean(x * x, axis=-1, keepdims=True) + EPS)
    return x * r * g, r


def _rms_bwd(x, g, d_y):
    r = lax.rsqrt(jnp.mean(x * x, axis=-1, keepdims=True) + EPS)
    xr = x * r
    u = d_y * g
    d_x = r * (u - xr * jnp.mean(u * xr, axis=-1, keepdims=True))
    return d_x, jnp.sum(d_y * xr, axis=0, keepdims=True)


def _ln(y, g, b):
    mu = jnp.mean(y, axis=-1, keepdims=True)
    yc = y - mu
    rstd = lax.rsqrt(jnp.mean(yc * yc, axis=-1, keepdims=True) + EPS)
    xhat = yc * rstd
    return xhat * g + b, xhat, rstd


def _ln_bwd(d_o, xhat, rstd, g):
    dxh = d_o * g
    return rstd * (dxh - jnp.mean(dxh, axis=-1, keepdims=True) - xhat * jnp.mean(dxh * xhat, axis=-1, keepdims=True))


def _gelu(x):
    th = jnp.tanh(GELU_C * (x + GELU_A * x * x * x))
    return 0.5 * x * (1.0 + th), th


def _gelu_grad(x, th):
    return 0.5 * (1.0 + th) + 0.5 * x * (1.0 - th * th) * GELU_C * (1.0 + 3.0 * GELU_A * x * x)


def _row(c):
    return pl.BlockSpec((1, c), lambda *_: (0, 0))


def _full(shape):
    return pl.BlockSpec(shape, lambda *_: (0,) * len(shape))


def _norm_proj(h, g, wt, bias, name):
    t, n = h.shape[0], wt.shape[0]
    tm = _tile(t, 1024)
    has_bias = bias is not None

    def body(*refs):
        h_ref, g_ref, wt_ref = refs[:3]
        z_ref, hn_ref = refs[-2:]
        hn = _rms(h_ref[...].astype(F32), g_ref[...])[0].astype(BF16)
        hn_ref[...] = hn
        z = _nt(hn, wt_ref[...])
        if has_bias:
            z = z + refs[3][...]
        z_ref[...] = z.astype(BF16)

    in_specs = [pl.BlockSpec((tm, D), lambda i: (i, 0)), _row(D), _full((n, D))]
    args = [h, g, wt]
    if has_bias:
        in_specs.append(_row(n))
        args.append(bias)
    return pl.pallas_call(
        body, name=name, grid=(t // tm,), in_specs=in_specs,
        out_specs=[pl.BlockSpec((tm, n), lambda i: (i, 0)), pl.BlockSpec((tm, D), lambda i: (i, 0))],
        out_shape=[jax.ShapeDtypeStruct((t, n), BF16), jax.ShapeDtypeStruct((t, D), BF16)],
        compiler_params=_params("parallel"))(*args)


def _ff_pieces(tf, width=256):
    return [(c0, min(width, tf - c0)) for c0 in range(0, tf, width)]


FF_HALF = D_FF // 2


def _half_spec(part):
    return pl.BlockSpec((FF_HALF, D), lambda i: (part, 0))


def _ffn_half_fwd(hn, wtg_ref, wtu_ref, gate_ref, up_ref, act):
    for c0, cw in _ff_pieces(FF_HALF):
        cols = slice(c0, c0 + cw)
        gate = _nt(hn, wtg_ref[cols, :])
        up = _nt(hn, wtu_ref[cols, :])
        gate_ref[:, cols] = gate.astype(BF16)
        up_ref[:, cols] = up.astype(BF16)
        act[:, cols] = (gate * jax.nn.sigmoid(gate) * up).astype(BF16)


def _ffn_fwd_a(h_prev, a, b, w_out, g, wtg, wtu, wd, name):
    t = h_prev.shape[0]
    tm = _tile(t)

    def body(hp_ref, a_ref, b_ref, wa_ref, wb_ref, g_ref, wtg_ref, wtu_ref, wd_ref,
             h_ref, hn_ref, gate_ref, up_ref, part_ref, act):
        x = hp_ref[...].astype(F32) + _nn(a_ref[...], wa_ref[...]) + _nn(b_ref[...], wb_ref[...])
        h_ref[...] = x.astype(BF16)
        hn = _rms(x, g_ref[...])[0].astype(BF16)
        hn_ref[...] = hn
        _ffn_half_fwd(hn, wtg_ref, wtu_ref, gate_ref, up_ref, act)
        part_ref[...] = x + _nn(act[...], wd_ref[...])

    tok = pl.BlockSpec((tm, D), lambda i: (i, 0))
    half = pl.BlockSpec((tm, 512), lambda i: (i, 0))
    mid = pl.BlockSpec((tm, FF_HALF), lambda i: (i, 0))
    res = jax.ShapeDtypeStruct((t, D), BF16)
    mid_shape = jax.ShapeDtypeStruct((t, FF_HALF), BF16)
    return pl.pallas_call(
        body, name=name, grid=(t // tm,),
        in_specs=[tok, half, half, pl.BlockSpec((512, D), lambda i: (0, 0)), pl.BlockSpec((512, D), lambda i: (1, 0)),
                  _row(D), _half_spec(0), _half_spec(0), _half_spec(0)],
        out_specs=[tok, tok, mid, mid, tok], out_shape=[res, res, mid_shape, mid_shape, jax.ShapeDtypeStruct((t, D), F32)],
        scratch_shapes=[pltpu.VMEM((tm, FF_HALF), BF16)],
        compiler_params=_params("parallel"))(h_prev, a, b, w_out, w_out, g, wtg, wtu, wd)


def _ffn_fwd_b(hn, part, wtg, wtu, wd, name, head=None):
    t = hn.shape[0]
    tm = _tile(t, 1024 if head is None else 512)
    n_head = 0 if head is None else 2

    def body(*refs):
        hn_ref, part_ref, wtg_ref, wtu_ref, wd_ref = refs[:5]
        gate_ref, up_ref, o_ref = refs[5 + n_head:8 + n_head]
        act = refs[-1]
        _ffn_half_fwd(hn_ref[...], wtg_ref, wtu_ref, gate_ref, up_ref, act)
        x = part_ref[...] + _nn(act[...], wd_ref[...])
        if head is None:
            o_ref[...] = x.astype(BF16)
        else:
            fg_ref, t_ref = refs[5:7]
            dfg_ref, loss_ref = refs[10:12]

            @pl.when(pl.program_id(0) == 0)
            def _():
                dfg_ref[...] = jnp.zeros_like(dfg_ref)
                loss_ref[...] = jnp.zeros_like(loss_ref)

            gv = fg_ref[...]
            err = _rms(x, gv)[0] - t_ref[...]
            loss_ref[...] += 0.5 * jnp.sum(jnp.mean(err * err, axis=-1, keepdims=True), axis=0, keepdims=True)
            d_x, d_g = _rms_bwd(x, gv, err * (1.0 / D))
            o_ref[...] = d_x.astype(BF16)
            dfg_ref[...] += d_g

    tok = pl.BlockSpec((tm, D), lambda i: (i, 0))
    mid = pl.BlockSpec((tm, FF_HALF), lambda i: (i, 0))
    mid_shape = jax.ShapeDtypeStruct((t, FF_HALF), BF16)
    in_specs = [tok, tok, _half_spec(1), _half_spec(1), _half_spec(1)]
    out_specs = [mid, mid, tok]
    out_shape = [mid_shape, mid_shape, jax.ShapeDtypeStruct((t, D), BF16)]
    if head is not None:
        in_specs += [_row(D), tok]
        out_specs += [_row(D), _row(1)]
        out_shape += [jax.ShapeDtypeStruct((1, D), F32), jax.ShapeDtypeStruct((1, 1), F32)]
    return pl.pallas_call(
        body, name=name, grid=(t // tm,), in_specs=in_specs, out_specs=out_specs, out_shape=out_shape,
        scratch_shapes=[pltpu.VMEM((tm, FF_HALF), BF16)],
        compiler_params=_params("parallel" if head is None else "arbitrary"))(hn, part, wtg, wtu, wd, *(head or ()))


def _ffn_half_bwd(dh, gate_ref, up_ref, wd_ref, dgate_ref, dup_ref, act_ref):
    for c0, cw in _ff_pieces(FF_HALF):
        cols = slice(c0, c0 + cw)
        da = _nt(dh, wd_ref[cols, :])
        gt = gate_ref[:, cols].astype(F32)
        u = up_ref[:, cols].astype(F32)
        sg = jax.nn.sigmoid(gt)
        sil = gt * sg
        act_ref[:, cols] = (sil * u).astype(BF16)
        dup_ref[:, cols] = (da * sil).astype(BF16)
        dgate_ref[:, cols] = (da * u * sg * (1.0 + gt * (1.0 - sg))).astype(BF16)


def _ffn_bwd_a(dh, gate, up, wtg, wtu, wd, name):
    t = dh.shape[0]
    tm = _tile(t)

    def body(dh_ref, gate_ref, up_ref, wtg_ref, wtu_ref, wd_ref, dgate_ref, dup_ref, act_ref, part_ref):
        _ffn_half_bwd(dh_ref[...], gate_ref, up_ref, wd_ref, dgate_ref, dup_ref, act_ref)
        part_ref[...] = _nn(dgate_ref[...], wtg_ref[...]) + _nn(dup_ref[...], wtu_ref[...])

    tok = pl.BlockSpec((tm, D), lambda i: (i, 0))
    mid = pl.BlockSpec((tm, FF_HALF), lambda i: (i, 0))
    wide = jax.ShapeDtypeStruct((t, D_FF), BF16)
    return pl.pallas_call(
        body, name=name, grid=(t // tm,), in_specs=[tok, mid, mid, _half_spec(0), _half_spec(0), _half_spec(0)],
        out_specs=[mid, mid, mid, tok], out_shape=[wide, wide, wide, jax.ShapeDtypeStruct((t, D), F32)],
        compiler_params=_params("parallel"))(dh, gate, up, wtg, wtu, wd)


def _ffn_bwd_b(dh, h, g, gate, up, wtg, wtu, wd, w_out, part, wide, name):
    t = dh.shape[0]
    tm = _tile(t)

    def body(dh_ref, h_ref, g_ref, gate_ref, up_ref, wtg_ref, wtu_ref, wd_ref, wout_ref, part_ref, _a, _b, _c,
             dhin_ref, dmix_ref, dgate_ref, dup_ref, act_ref, dg_ref):
        @pl.when(pl.program_id(0) == 0)
        def _():
            dg_ref[...] = jnp.zeros_like(dg_ref)

        _ffn_half_bwd(dh_ref[...], gate_ref, up_ref, wd_ref, dgate_ref, dup_ref, act_ref)
        d_hn = part_ref[...] + _nn(dgate_ref[...], wtg_ref[...]) + _nn(dup_ref[...], wtu_ref[...])
        d_x, d_g = _rms_bwd(h_ref[...].astype(F32), g_ref[...], d_hn)
        dhin = (dh_ref[...].astype(F32) + d_x).astype(BF16)
        dhin_ref[...] = dhin
        dmix_ref[...] = _nt(dhin, wout_ref[...]).astype(BF16)
        dg_ref[...] += d_g

    tok = pl.BlockSpec((tm, D), lambda i: (i, 0))
    mid = pl.BlockSpec((tm, FF_HALF), lambda i: (i, 0))
    second = pl.BlockSpec((tm, FF_HALF), lambda i: (i, 1))
    wide_shape = jax.ShapeDtypeStruct((t, D_FF), BF16)
    res = jax.ShapeDtypeStruct((t, D), BF16)
    return pl.pallas_call(
        body, name=name, grid=(t // tm,),
        in_specs=[tok, tok, _row(D), mid, mid, _half_spec(1), _half_spec(1), _half_spec(1), _full((D, D)), tok, ANY, ANY, ANY],
        out_specs=[tok, tok, second, second, second, _row(D)],
        out_shape=[res, res, wide_shape, wide_shape, wide_shape, jax.ShapeDtypeStruct((1, D), F32)],
        input_output_aliases={10: 2, 11: 3, 12: 4},
        compiler_params=_params("arbitrary"))(dh, h, g, gate, up, wtg, wtu, wd, w_out, part, *wide)


def _proj_bwd_norm(pieces, wt, h, dh, g, dtype, name):
    t = h.shape[0]
    tm = _tile(t)
    n_p = len(pieces)

    def body(*refs):
        p_refs, w_refs = refs[:n_p], refs[n_p:2 * n_p]
        h_ref, dh_ref, g_ref, o_ref, dg_ref = refs[2 * n_p:]

        @pl.when(pl.program_id(0) == 0)
        def _():
            dg_ref[...] = jnp.zeros_like(dg_ref)

        d_hn = _nn(p_refs[0][...], w_refs[0][...])
        for p_ref, w_ref in zip(p_refs[1:], w_refs[1:]):
            d_hn = d_hn + _nn(p_ref[...], w_ref[...])
        d_x, d_g = _rms_bwd(h_ref[...].astype(F32), g_ref[...], d_hn)
        o_ref[...] = (dh_ref[...].astype(F32) + d_x).astype(dtype)
        dg_ref[...] += d_g

    tok = pl.BlockSpec((tm, D), lambda i: (i, 0))
    in_specs = [pl.BlockSpec((tm, a.shape[1]), lambda i: (i, 0)) for a, _ in pieces]
    for a, off in pieces:
        w = a.shape[1]
        assert off % w == 0
        in_specs.append(pl.BlockSpec((w, D), functools.partial(lambda i, blk: (blk, 0), blk=off // w)))
    in_specs += [tok, tok, _row(D)]
    return pl.pallas_call(
        body, name=name, grid=(t // tm,), in_specs=in_specs, out_specs=[tok, _row(D)],
        out_shape=[jax.ShapeDtypeStruct((t, D), dtype), jax.ShapeDtypeStruct((1, D), F32)],
        compiler_params=_params("arbitrary"))(*[a for a, _ in pieces], *([wt] * n_p), h, dh, g)


def _mm_tn(a, b, name):
    t, n = a.shape
    k = b.shape[1]
    tn = n if n <= 1024 else n // 2
    tt = _tile(t, 2048)
    nt = t // tt

    def body(a_ref, b_ref, o_ref, acc):
        s = pl.program_id(1)

        @pl.when(s == 0)
        def _():
            acc[...] = jnp.zeros_like(acc)

        acc[...] += _tn(a_ref[...], b_ref[...].astype(BF16))

        @pl.when(s == nt - 1)
        def _():
            o_ref[...] = acc[...].astype(BF16)

    return pl.pallas_call(
        body, name=name, grid=(n // tn, nt),
        in_specs=[pl.BlockSpec((tt, tn), lambda j, s: (s, j)), pl.BlockSpec((tt, k), lambda j, s: (s, 0))],
        out_specs=pl.BlockSpec((tn, k), lambda j, s: (j, 0)), out_shape=jax.ShapeDtypeStruct((n, k), BF16),
        scratch_shapes=[pltpu.VMEM((tn, k), F32)],
        compiler_params=_params("parallel", "arbitrary"))(a, b)


def _mm_tn_pieces(pieces, b, name):
    t, k = b.shape
    widths = [p.shape[1] for p in pieces]
    n, n_p = sum(widths), len(pieces)
    tt = _tile(t, 1024)
    nt = t // tt

    def body(*refs):
        b_ref, o_ref, acc = refs[n_p:]
        s = pl.program_id(0)

        @pl.when(s == 0)
        def _():
            acc[...] = jnp.zeros_like(acc)

        bb = b_ref[...].astype(BF16)
        off = 0
        for p_ref, w in zip(refs[:n_p], widths):
            acc[off:off + w, :] += _tn(p_ref[...], bb)
            off += w

        @pl.when(s == nt - 1)
        def _():
            o_ref[...] = acc[...].astype(BF16)

    return pl.pallas_call(
        body, name=name, grid=(nt,),
        in_specs=[pl.BlockSpec((tt, w), lambda s: (s, 0)) for w in widths] + [pl.BlockSpec((tt, k), lambda s: (s, 0))],
        out_specs=_full((n, k)), out_shape=jax.ShapeDtypeStruct((n, k), BF16),
        scratch_shapes=[pltpu.VMEM((n, k), F32)], compiler_params=_params("arbitrary"))(*pieces, b)


STACK = HEAD_PAIRS * ATT_BLK


def _attn_valid(first, rows):
    qi = lax.broadcasted_iota(jnp.int32, (rows, 2 * ATT_BLK), 0) % ATT_BLK
    r = lax.broadcasted_iota(jnp.int32, (rows, 2 * ATT_BLK), 1)
    dist = qi + ATT_BLK - r
    return (dist >= 0) & (dist < ATT_BLK) & ((r >= ATT_BLK) | jnp.logical_not(first))


def _stacked(ref, kh, scale):
    lo = lax.broadcasted_iota(jnp.int32, (ATT_BLK, 128), 1) < 64
    keep = lo if kh == 0 else ~lo
    parts = [jnp.where(keep, ref[:, g * 128:(g + 1) * 128] * scale, 0.0).astype(BF16) for g in range(HEAD_PAIRS)]
    return jnp.concatenate(parts, axis=0)


def _unstacked(a0, a1, g):
    lo = lax.broadcasted_iota(jnp.int32, (ATT_BLK, 128), 1) < 64
    rows = slice(g * ATT_BLK, (g + 1) * ATT_BLK)
    return jnp.where(lo, a0[rows], a1[rows])


def _sink_rows(s_ref, kh):
    return jnp.concatenate([jnp.full((ATT_BLK, 128), s_ref[0, kh * 4 + g], F32) for g in range(HEAD_PAIRS)], axis=0)


def _row_sums(a, split):
    hi = a.astype(BF16)
    ones = jnp.ones((2 * ATT_BLK, 128), BF16)
    if not split:
        return _nn(hi, ones)
    lo = (a - hi.astype(F32)).astype(BF16)
    return _nn(hi, ones) + _nn(lo, ones)


def _both(a):
    return jnp.concatenate([a, a], axis=1)


def _attn_probs(qs, kpair, sink, valid):
    s = jnp.where(valid, _nt(qs, kpair), -1e30)
    m = jnp.maximum(jnp.broadcast_to(jnp.max(s, axis=-1, keepdims=True), (s.shape[0], 128)), sink)
    p = jnp.exp(s - _both(m))
    es = jnp.exp(sink - m)
    inv = 1.0 / (_row_sums(p, split=True) + es)
    return p * _both(inv), es * inv


def _attn_probs_head(qm, kpair, sink, valid):
    s = jnp.where(valid, _nt(qm, kpair), -1e30)
    m = jnp.maximum(jnp.max(s, axis=-1, keepdims=True), sink)
    p = jnp.exp(s - m)
    return p * (1.0 / (jnp.sum(p, axis=-1, keepdims=True) + jnp.exp(sink - m)))


def _attn_specs(bsz, order):
    q = pl.BlockSpec((bsz, ATT_BLK, 512), lambda j: (0, order(j), 0))
    kvc = pl.BlockSpec((bsz, ATT_BLK, 256), lambda j: (0, order(j), 6))
    kvp = pl.BlockSpec((bsz, ATT_BLK, 256), lambda j: (0, jnp.maximum(order(j) - 1, 0), 6))
    return q, kvc, kvp


def _window_kv(kvc_ref, kvp_ref):
    kvc, kvp = kvc_ref[...], kvp_ref[...]
    kpair = jnp.concatenate([kvp[:, :128], kvc[:, :128]], axis=0)
    vpair = jnp.concatenate([kvp[:, 128:], kvc[:, 128:]], axis=0)
    return kpair, vpair


def _attn_fwd(z0, sinks, bsz, name):
    t = z0.shape[0]
    seq = t // bsz
    nb = seq // ATT_BLK

    def body(s_ref, q_ref, kvc_ref, kvp_ref, o_ref, token):
        token[...] = jnp.zeros_like(token)
        valid = _attn_valid(pl.program_id(0) == 0, ATT_BLK)
        lo = lax.broadcasted_iota(jnp.int32, (ATT_BLK, 128), 1) < 64
        for b in range(bsz):
            kpair, vpair = _window_kv(kvc_ref.at[b], kvp_ref.at[b])
            for g in range(HEAD_PAIRS):
                qs = q_ref[b, :, g * 128:(g + 1) * 128] * 0.125
                outs = []
                for kh in range(2):
                    qm = jnp.where(lo if kh == 0 else ~lo, qs, 0.0).astype(BF16)
                    p = _attn_probs_head(qm, kpair, s_ref[0, kh * 4 + g], valid)
                    outs.append(_nn(p.astype(BF16), vpair))
                o_ref[b, :, g * 128:(g + 1) * 128] = jnp.where(lo, outs[0], outs[1]).astype(BF16)

    q, kvc, kvp = _attn_specs(bsz, lambda j: j)
    z3 = z0.reshape(bsz, seq, z0.shape[1])
    out, token = pl.pallas_call(
        body, name=name, grid=(nb,),
        in_specs=[pl.BlockSpec(memory_space=pltpu.SMEM), q, kvc, kvp],
        out_specs=[pl.BlockSpec((bsz, ATT_BLK, 512), lambda j: (0, j, 0)), _full((8, 128))],
        out_shape=[jax.ShapeDtypeStruct((bsz, seq, 512), BF16), jax.ShapeDtypeStruct((8, 128), F32)],
        compiler_params=_params("arbitrary"))(sinks, z3, z3, z3)
    return out.reshape(t, 512), token


def _attn_bwd(z0, dmix, sinks, bsz, name):
    t = z0.shape[0]
    seq = t // bsz
    nb = seq // ATT_BLK

    def body(s_ref, q_ref, kvc_ref, kvp_ref, do_ref, dq_ref, dkv_ref, dsink_ref, dbq_ref, dbkv_ref, carry):
        j = pl.program_id(0)

        @pl.when(j == 0)
        def _():
            carry[...] = jnp.zeros_like(carry)
            dsink_ref[...] = jnp.zeros_like(dsink_ref)
            dbq_ref[...] = jnp.zeros_like(dbq_ref)
            dbkv_ref[...] = jnp.zeros_like(dbkv_ref)

        valid = _attn_valid(j == nb - 1, STACK)
        lane = lax.broadcasted_iota(jnp.int32, (1, 128), 1)
        dsink = jnp.zeros((1, 128), F32)
        dbq = [jnp.zeros((1, 128), F32)] * HEAD_PAIRS
        dbkv = jnp.zeros((1, 256), F32)
        for b in range(bsz):
            kpair, vpair = _window_kv(kvc_ref.at[b], kvp_ref.at[b])
            dk = jnp.zeros((2 * ATT_BLK, 128), F32)
            dv = jnp.zeros((2 * ATT_BLK, 128), F32)
            dqs = []
            for kh in range(2):
                qs = _stacked(q_ref.at[b], kh, 0.125)
                dos = _stacked(do_ref.at[b], kh, 1.0)
                p, ps = _attn_probs(qs, kpair, _sink_rows(s_ref, kh), valid)
                dp = _nt(dos, vpair)
                delta = _row_sums(p * dp, split=False)
                ds = (p * (dp - _both(delta))).astype(BF16)
                dqs.append(_nn(ds, kpair))
                dk = dk + _tn(ds, qs)
                dv = dv + _tn(p.astype(BF16), dos)
                psd = ps * delta
                for g in range(HEAD_PAIRS):
                    part = jnp.sum(psd[g * ATT_BLK:(g + 1) * ATT_BLK], axis=0, keepdims=True)
                    dsink = dsink - jnp.where(lane == kh * 4 + g, part, 0.0)
            for g in range(HEAD_PAIRS):
                dq = _unstacked(dqs[0], dqs[1], g) * 0.125
                dq_ref[b, :, g * 128:(g + 1) * 128] = dq.astype(BF16)
                dbq[g] = dbq[g] + jnp.sum(dq, axis=0, keepdims=True)
            dkv = jnp.concatenate([dk[ATT_BLK:], dv[ATT_BLK:]], axis=1) + carry[b]
            dkv_ref[b] = dkv.astype(BF16)
            dbkv = dbkv + jnp.sum(dkv, axis=0, keepdims=True)
            carry[b] = jnp.concatenate([dk[:ATT_BLK], dv[:ATT_BLK]], axis=1)
        dsink_ref[...] += dsink
        dbq_ref[...] += jnp.concatenate(dbq, axis=1)
        dbkv_ref[...] += dbkv

    q, kvc, kvp = _attn_specs(bsz, lambda j: nb - 1 - j)
    z3 = z0.reshape(bsz, seq, z0.shape[1])
    d3 = dmix.reshape(bsz, seq, dmix.shape[1])
    dq, dkv, dsink, dbq, dbkv = pl.pallas_call(
        body, name=name, grid=(nb,),
        in_specs=[pl.BlockSpec(memory_space=pltpu.SMEM), q, kvc, kvp,
                  pl.BlockSpec((bsz, ATT_BLK, 512), lambda j: (0, nb - 1 - j, 0))],
        out_specs=[pl.BlockSpec((bsz, ATT_BLK, 512), lambda j: (0, nb - 1 - j, 0)),
                   pl.BlockSpec((bsz, ATT_BLK, 256), lambda j: (0, nb - 1 - j, 0)), _row(128), _row(512), _row(256)],
        out_shape=[jax.ShapeDtypeStruct((bsz, seq, 512), BF16), jax.ShapeDtypeStruct((bsz, seq, 256), BF16),
                   jax.ShapeDtypeStruct((1, 128), F32), jax.ShapeDtypeStruct((1, 512), F32),
                   jax.ShapeDtypeStruct((1, 256), F32)],
        scratch_shapes=[pltpu.VMEM((bsz, ATT_BLK, 256), F32)],
        compiler_params=_params("arbitrary"))(sinks, z3, z3, z3, d3)
    return dq.reshape(t, 512), dkv.reshape(t, 256), dsink, dbq, dbkv


def _seq_specs(ts, nt, t, width, col):
    per = ts // HALO
    cur = pl.BlockSpec((ts, width), lambda b, i: (b * nt + i, col))
    prev = pl.BlockSpec((HALO, width), lambda b, i: (jnp.maximum((b * nt + i) * per - 1, 0), col))
    nxt = pl.BlockSpec((HALO, width), lambda b, i: (jnp.minimum((b * nt + i + 1) * per, t // HALO - 1), col))
    return prev, cur, nxt


SUB = 8
CONV_ROWS = 32


def _shifted_copies(src, sh, rows_first, rows_rest):
    for r in range(SUB):
        rows = rows_first if r == 0 else rows_rest
        sh[r, pl.ds(0, rows), :] = src[pl.ds(r, rows), :]


def _tap_sum(sh, w, offset, c0, rows):
    acc = None
    for k in range(CONV_K):
        o = offset(k)
        term = sh[o % SUB, pl.ds(c0 + o - o % SUB, rows), :] * w[k:k + 1, :]
        acc = term if acc is None else acc + term
    return acc


def _glu_rows(a_ref, g_ref, rows=slice(None)):
    return a_ref[rows, :].astype(F32) * jax.nn.sigmoid(g_ref[rows, :].astype(F32))


def _conv_fwd(z0, conv_w, conv_b, ln_g, ln_b, bsz, name):
    t = z0.shape[0]
    s = t // bsz
    ts = _seq_tile(s)
    nt = s // ts
    first = HALO - (CONV_K - 1)

    def body(ap_ref, ac_ref, gp_ref, gc_ref, w_ref, cb_ref, lg_ref, lb_ref, o_ref, y_ref, hbuf, sh):
        hbuf[0:HALO, :] = jnp.where(pl.program_id(1) > 0, _glu_rows(ap_ref, gp_ref), 0.0)
        hbuf[HALO:HALO + ts, :] = _glu_rows(ac_ref, gc_ref)
        _shifted_copies(hbuf, sh, ts + HALO, ts + HALO - SUB)
        w, cb, lg, lb = w_ref[...], cb_ref[...], lg_ref[...], lb_ref[...]
        for c0 in range(0, ts, CONV_ROWS):
            y = _tap_sum(sh, w, lambda k: first + k, c0, CONV_ROWS) + cb
            y_ref[c0:c0 + CONV_ROWS, :] = y
            o = _ln(y, lg, lb)[0]
            o_ref[c0:c0 + CONV_ROWS, :] = (o * jax.nn.sigmoid(o)).astype(BF16)

    ap, ac, _ = _seq_specs(ts, nt, t, 512, 1)
    gp, gc, _ = _seq_specs(ts, nt, t, 512, 2)
    tile = pl.BlockSpec((ts, 512), lambda b, i: (b * nt + i, 0))
    return pl.pallas_call(
        body, name=name, grid=(bsz, nt),
        in_specs=[ap, ac, gp, gc, _full((HALO, 512)), _row(512), _row(512), _row(512)],
        out_specs=[tile, tile],
        out_shape=[jax.ShapeDtypeStruct((t, 512), BF16), jax.ShapeDtypeStruct((t, 512), F32)],
        scratch_shapes=[pltpu.VMEM((HALO + ts, 512), F32), pltpu.VMEM((SUB, HALO + ts, 512), F32)],
        compiler_params=_params("parallel", "parallel"))(z0, z0, z0, z0, conv_w, conv_b, ln_g, ln_b)


def _conv_bwd(z0, y, dmix, conv_w, ln_g, ln_b, bsz, name):
    t = z0.shape[0]
    s = t // bsz
    ts = _seq_tile(s)
    nt = s // ts

    def body(ac_ref, gc_ref, yc_ref, yn_ref, dc_ref, dn_ref, w_ref, lg_ref, lb_ref,
             da_ref, dg_ref, dw_ref, dcb_ref, dlg_ref, dlb_ref, dba_ref, dbg_ref, hcur, dybuf, sh_dy):
        b, i = pl.program_id(0), pl.program_id(1)

        @pl.when((b == 0) & (i == 0))
        def _():
            for ref in (dw_ref, dcb_ref, dlg_ref, dlb_ref, dba_ref, dbg_ref):
                ref[...] = jnp.zeros_like(ref)

        w, lg, lb = w_ref[...], lg_ref[...], lb_ref[...]
        hcur[...] = _glu_rows(ac_ref, gc_ref)

        def d_conv_out(yv, dout):
            o, xhat, rstd = _ln(yv, lg, lb)
            sg_o = jax.nn.sigmoid(o)
            d_o = dout * sg_o * (1.0 + o * (1.0 - sg_o))
            return _ln_bwd(d_o, xhat, rstd, lg), d_o * xhat, d_o

        dlg = jnp.zeros((1, 512), F32)
        dlb = jnp.zeros((1, 512), F32)
        dcb = jnp.zeros((1, 512), F32)
        for c0 in range(0, ts, CONV_ROWS):
            rows = slice(c0, c0 + CONV_ROWS)
            dy, g_part, b_part = d_conv_out(yc_ref[rows, :], dc_ref[rows, :].astype(F32))
            dybuf[rows, :] = dy
            dlg = dlg + jnp.sum(g_part, axis=0, keepdims=True)
            dlb = dlb + jnp.sum(b_part, axis=0, keepdims=True)
            dcb = dcb + jnp.sum(dy, axis=0, keepdims=True)
        dn = jnp.where(i < nt - 1, dn_ref[...].astype(F32), 0.0)
        dybuf[ts:ts + HALO, :] = d_conv_out(yn_ref[...], dn)[0]
        dlg_ref[...] += dlg
        dlb_ref[...] += dlb
        dcb_ref[...] += dcb
        _shifted_copies(dybuf, sh_dy, ts + HALO - SUB, ts + HALO - SUB)

        for k in range(CONV_K):
            o = CONV_K - 1 - k
            prod = hcur[...] * sh_dy[o % SUB, pl.ds(o - o % SUB, ts), :]
            dw_ref[pl.ds(k, 1), :] += jnp.sum(prod, axis=0, keepdims=True)
        dba = jnp.zeros((1, 512), F32)
        dbg = jnp.zeros((1, 512), F32)
        for c0 in range(0, ts, CONV_ROWS):
            rows = slice(c0, c0 + CONV_ROWS)
            dh = _tap_sum(sh_dy, w, lambda k: CONV_K - 1 - k, c0, CONV_ROWS)
            a_c = ac_ref[rows, :].astype(F32)
            sg_c = jax.nn.sigmoid(gc_ref[rows, :].astype(F32))
            d_a = dh * sg_c
            d_g = dh * a_c * sg_c * (1.0 - sg_c)
            da_ref[rows, :] = d_a.astype(BF16)
            dg_ref[rows, :] = d_g.astype(BF16)
            dba = dba + jnp.sum(d_a, axis=0, keepdims=True)
            dbg = dbg + jnp.sum(d_g, axis=0, keepdims=True)
        dba_ref[...] += dba
        dbg_ref[...] += dbg

    _, ac, _ = _seq_specs(ts, nt, t, 512, 1)
    _, gc, _ = _seq_specs(ts, nt, t, 512, 2)
    _, yc, yn = _seq_specs(ts, nt, t, 512, 0)
    _, dc, dn = _seq_specs(ts, nt, t, 512, 1)
    tile = pl.BlockSpec((ts, 512), lambda b, i: (b * nt + i, 0))
    vec = jax.ShapeDtypeStruct((1, 512), F32)
    return pl.pallas_call(
        body, name=name, grid=(bsz, nt),
        in_specs=[ac, gc, yc, yn, dc, dn, _full((HALO, 512)), _row(512), _row(512)],
        out_specs=[tile, tile, _full((HALO, 512)), _row(512), _row(512), _row(512), _row(512), _row(512)],
        out_shape=[jax.ShapeDtypeStruct((t, 512), BF16), jax.ShapeDtypeStruct((t, 512), BF16),
                   jax.ShapeDtypeStruct((HALO, 512), F32), vec, vec, vec, vec, vec],
        scratch_shapes=[pltpu.VMEM((ts, 512), F32), pltpu.VMEM((ts + HALO, 512), F32),
                        pltpu.VMEM((SUB, HALO + ts, 512), F32)],
        compiler_params=_params("arbitrary", "arbitrary"))(z0, z0, y, y, dmix, dmix, conv_w, ln_g, ln_b)


def _pooled(pbuf, g, ts, tok):
    w = 2 << g
    cols = slice(128 * g, 128 * (g + 1))
    sm = pbuf[pl.ds(HALO, ts), cols]
    for d in range(1, w):
        sm = sm + pbuf[pl.ds(HALO - d, ts), cols]
    cnt = jnp.minimum(tok + 1, w).astype(F32)
    return sm / cnt - pbuf[pl.ds(HALO, ts), cols]


def _pool_fwd(z1, w_pool, scale, bsz, name):
    t = z1.shape[0]
    s = t // bsz
    ts = _seq_tile(s)
    nt = s // ts

    def body(zp_ref, zc_ref, wp_ref, sc_ref, o_ref, token, pbuf):
        token[...] = jnp.zeros_like(token)
        i = pl.program_id(1)
        pbuf[0:HALO, :] = jnp.where(i > 0, zp_ref[...].astype(F32), 0.0)
        pbuf[HALO:HALO + ts, :] = zc_ref[...].astype(F32)
        tok = i * ts + lax.broadcasted_iota(jnp.int32, (ts, 1), 0)
        for g in range(4):
            cols = slice(128 * g, 128 * (g + 1))
            pooled = _pooled(pbuf, g, ts, tok).astype(BF16)
            o_ref[:, cols] = (_nn(pooled, wp_ref[g].astype(BF16)) * sc_ref[:, cols]).astype(BF16)

    zp, zc, _ = _seq_specs(ts, nt, t, 512, 0)
    return pl.pallas_call(
        body, name=name, grid=(bsz, nt), in_specs=[zp, zc, _full((4, 128, 128)), _row(512)],
        out_specs=[pl.BlockSpec((ts, 512), lambda b, i: (b * nt + i, 0)), _full((8, 128))],
        out_shape=[jax.ShapeDtypeStruct((t, 512), BF16), jax.ShapeDtypeStruct((8, 128), F32)],
        scratch_shapes=[pltpu.VMEM((HALO + ts, 512), F32)],
        compiler_params=_params("arbitrary", "arbitrary"))(z1, z1, w_pool, scale)


def _pool_bwd(z1, dmix, w_pool, scale, bsz, name):
    t = z1.shape[0]
    s = t // bsz
    ts = _seq_tile(s)
    nt = s // ts
    rr = ts + HALO

    def body(zp_ref, zc_ref, dc_ref, dn_ref, wp_ref, sc_ref, dz_ref, dwp_ref, dsc_ref, pbuf, ebuf):
        b, i = pl.program_id(0), pl.program_id(1)

        @pl.when((b == 0) & (i == 0))
        def _():
            dwp_ref[...] = jnp.zeros_like(dwp_ref)
            dsc_ref[...] = jnp.zeros_like(dsc_ref)

        pbuf[0:HALO, :] = jnp.where(i > 0, zp_ref[...].astype(F32), 0.0)
        pbuf[HALO:HALO + ts, :] = zc_ref[...].astype(F32)
        dn = jnp.where(i < nt - 1, dn_ref[...].astype(F32), 0.0)
        dout = jnp.concatenate([dc_ref[...].astype(F32), dn], axis=0)
        tok = i * ts + lax.broadcasted_iota(jnp.int32, (ts, 1), 0)
        tok_r = i * ts + lax.broadcasted_iota(jnp.int32, (rr, 1), 0)
        for g in range(4):
            w = 2 << g
            cols = slice(128 * g, 128 * (g + 1))
            wg = wp_ref[g].astype(BF16)
            pooled = _pooled(pbuf, g, ts, tok).astype(BF16)
            dsc_ref[:, cols] += jnp.sum(dout[:ts, cols] * _nn(pooled, wg), axis=0, keepdims=True)
            dy = (dout[:, cols] * sc_ref[:, cols]).astype(BF16)
            dwp_ref[g] += _tn(pooled, dy[:ts])
            dpl = _nt(dy, wg)
            ebuf[...] = dpl / jnp.minimum(tok_r + 1, w).astype(F32)
            dz = ebuf[pl.ds(0, ts), :] - dpl[:ts]
            for d in range(1, w):
                dz = dz + ebuf[pl.ds(d, ts), :]
            dz_ref[:, cols] = dz.astype(BF16)

    zp, zc, _ = _seq_specs(ts, nt, t, 512, 0)
    _, dc, dn = _seq_specs(ts, nt, t, 512, 0)
    return pl.pallas_call(
        body, name=name, grid=(bsz, nt), in_specs=[zp, zc, dc, dn, _full((4, 128, 128)), _row(512)],
        out_specs=[pl.BlockSpec((ts, 512), lambda b, i: (b * nt + i, 0)), _full((4, 128, 128)), _row(512)],
        out_shape=[jax.ShapeDtypeStruct((t, 512), BF16), jax.ShapeDtypeStruct((4, 128, 128), F32),
                   jax.ShapeDtypeStruct((1, 512), F32)],
        scratch_shapes=[pltpu.VMEM((HALO + ts, 512), F32), pltpu.VMEM((rr, 128), F32)],
        compiler_params=_params("arbitrary", "arbitrary"))(z1, z1, dmix, dmix, w_pool, scale)


def _tril():
    r = lax.broadcasted_iota(jnp.int32, (SGU_CHUNK, SGU_CHUNK), 0)
    c = lax.broadcasted_iota(jnp.int32, (SGU_CHUNK, SGU_CHUNK), 1)
    return r >= c


def _sgu_fwd(z1, ln_g, ln_b, w_s, b_rows, name):
    t = z1.shape[0]
    ts = _tile(t)

    def body(zu_ref, zv_ref, lg_ref, lb_ref, ws_ref, bs_ref, o_ref):
        v = _gelu(zv_ref[...].astype(F32))[0]
        vb = _ln(v, lg_ref[...], lb_ref[...])[0].astype(BF16)
        tril = _tril()
        for g in range(4):
            cols = slice(128 * g, 128 * (g + 1))
            wg = jnp.where(tril, ws_ref[g], 0.0).astype(BF16)
            n_c = ts // SGU_CHUNK
            wide = jnp.concatenate([vb[SGU_CHUNK * c:SGU_CHUNK * (c + 1), cols] for c in range(n_c)], axis=1)
            mixed_all = _nn(wg, wide)
            for c in range(n_c):
                rows = slice(SGU_CHUNK * c, SGU_CHUNK * (c + 1))
                mixed = mixed_all[:, 128 * c:128 * (c + 1)] + bs_ref[g]
                o_ref[rows, cols] = (_gelu(zu_ref[rows, cols].astype(F32))[0] * mixed).astype(BF16)

    return pl.pallas_call(
        body, name=name, grid=(t // ts,),
        in_specs=[pl.BlockSpec((ts, 512), lambda i: (i, 1)), pl.BlockSpec((ts, 512), lambda i: (i, 2)),
                  _row(512), _row(512), _full((4, 128, 128)), _full((4, 128, 128))],
        out_specs=pl.BlockSpec((ts, 512), lambda i: (i, 0)), out_shape=jax.ShapeDtypeStruct((t, 512), BF16),
        compiler_params=_params("parallel"))(z1, z1, ln_g, ln_b, w_s, b_rows)


def _sgu_bwd(z1, dmix, ln_g, ln_b, w_s, b_rows, name):
    t = z1.shape[0]
    ts = _tile(t)

    def body(zu_ref, zv_ref, d_ref, lg_ref, lb_ref, ws_ref, bs_ref,
             dzu_ref, dzv_ref, dws_ref, dbs_ref, dlg_ref, dlb_ref, dvbuf):
        @pl.when(pl.program_id(0) == 0)
        def _():
            for ref in (dws_ref, dbs_ref, dlg_ref, dlb_ref):
                ref[...] = jnp.zeros_like(ref)

        zv = zv_ref[...].astype(F32)
        v, thv = _gelu(zv)
        lg = lg_ref[...]
        vln, xhat, rstd = _ln(v, lg, lb_ref[...])
        vb = vln.astype(BF16)
        tril = _tril()
        for g in range(4):
            cols = slice(128 * g, 128 * (g + 1))
            wg = jnp.where(tril, ws_ref[g], 0.0).astype(BF16)
            dws = jnp.zeros((SGU_CHUNK, SGU_CHUNK), F32)
            dbs = jnp.zeros((1, SGU_CHUNK), F32)
            for c in range(ts // SGU_CHUNK):
                rows = slice(SGU_CHUNK * c, SGU_CHUNK * (c + 1))
                vbc = vb[rows, cols]
                mixed = _nn(wg, vbc) + bs_ref[g]
                zu = zu_ref[rows, cols].astype(F32)
                u, thu = _gelu(zu)
                dout = d_ref[rows, cols].astype(F32)
                dzu_ref[rows, cols] = (dout * mixed * _gelu_grad(zu, thu)).astype(BF16)
                dm = dout * u
                dmb = dm.astype(BF16)
                dws = dws + _nt(dmb, vbc)
                dbs = dbs + jnp.sum(dm.T, axis=0, keepdims=True)
                dvbuf[rows, cols] = _tn(wg, dmb)
            dws_ref[g] += jnp.where(tril, dws, 0.0)
            dbs_ref[pl.ds(g, 1), :] += dbs
        dvln = dvbuf[...]
        dlg_ref[...] += jnp.sum(dvln * xhat, axis=0, keepdims=True)
        dlb_ref[...] += jnp.sum(dvln, axis=0, keepdims=True)
        dzv_ref[...] = (_ln_bwd(dvln, xhat, rstd, lg) * _gelu_grad(zv, thv)).astype(BF16)

    tile = pl.BlockSpec((ts, 512), lambda i: (i, 0))
    vec = jax.ShapeDtypeStruct((1, 512), F32)
    return pl.pallas_call(
        body, name=name, grid=(t // ts,),
        in_specs=[pl.BlockSpec((ts, 512), lambda i: (i, 1)), pl.BlockSpec((ts, 512), lambda i: (i, 2)),
                  pl.BlockSpec((ts, 512), lambda i: (i, 1)), _row(512), _row(512), _full((4, 128, 128)),
                  _full((4, 128, 128))],
        out_specs=[tile, tile, _full((4, 128, 128)), _full((4, 128)), _row(512), _row(512)],
        out_shape=[jax.ShapeDtypeStruct((t, 512), BF16), jax.ShapeDtypeStruct((t, 512), BF16),
                   jax.ShapeDtypeStruct((4, 128, 128), F32), jax.ShapeDtypeStruct((4, 128), F32), vec, vec],
        scratch_shapes=[pltpu.VMEM((ts, 512), F32)],
        compiler_params=_params("arbitrary"))(z1, z1, dmix, ln_g, ln_b, w_s, b_rows)


def _row_tile(r):
    for cand in (512, 352, 256, 192, 128, 64, 32, 16, 8):
        if r % cand == 0:
            return cand
    return r


def _sum_slabs(a, name):
    k, r, c = a.shape
    tr = _row_tile(r)

    def body(*refs):
        acc = refs[0][...].astype(F32)
        for ref in refs[1:-1]:
            acc = acc + ref[...].astype(F32)
        refs[-1][...] = acc

    in_specs = [pl.BlockSpec((None, tr, c), functools.partial(lambda i, s: (s, i, 0), s=s)) for s in range(k)]
    return pl.pallas_call(
        body, name=name, grid=(r // tr,), in_specs=in_specs, out_specs=pl.BlockSpec((tr, c), lambda i: (i, 0)),
        out_shape=jax.ShapeDtypeStruct((r, c), F32), compiler_params=_params("parallel"))(*([a] * k))


def _adamw_math(w, g, m, v):
    mn = ADAM_B1 * m + (1.0 - ADAM_B1) * g
    vn = ADAM_B2 * v + (1.0 - ADAM_B2) * (g * g)
    m_hat = mn / (1.0 - ADAM_B1 ** ADAM_STEP)
    v_hat = vn / (1.0 - ADAM_B2 ** ADAM_STEP)
    return -ADAM_LR * (m_hat / (jnp.sqrt(v_hat) + ADAM_EPS) + ADAM_WD * w), mn, vn


def _reduce_adamw(landing, w, m, v, name, layer=None, into=None):
    k, r, c = landing.shape
    tr = _row_tile(r)
    n_into = 0 if into is None else 4

    def body(*refs):
        slabs, (w_ref, m_ref, v_ref) = refs[:k], refs[k:k + 3]
        g_ref, d_ref, mo_ref, vo_ref = refs[k + 3 + n_into:]
        g = slabs[0][...].astype(F32)
        for ref in slabs[1:]:
            g = g + ref[...].astype(F32)
        g_ref[...] = g
        d_ref[...], mo_ref[...], vo_ref[...] = _adamw_math(w_ref[...], g, m_ref[...], v_ref[...])

    if layer is None:
        spec = pl.BlockSpec((tr, c), lambda i: (i, 0))
    else:
        spec = pl.BlockSpec((None, tr, c), lambda i: (layer, i, 0))
    in_specs = [pl.BlockSpec((None, tr, c), functools.partial(lambda i, s: (s, i, 0), s=s)) for s in range(k)]
    in_specs += [spec] * 3 + [ANY] * n_into
    shape = jax.ShapeDtypeStruct(w.shape, F32)
    return pl.pallas_call(
        body, name=name, grid=(r // tr,), in_specs=in_specs, out_specs=[spec] * 4, out_shape=[shape] * 4,
        input_output_aliases={k + 3 + j: j for j in range(n_into)},
        compiler_params=_params("parallel"))(*([landing] * k), w, m, v, *(into or ()))


def _rows_needed(shape):
    return shape[0] * -(-shape[1] // D)


def _as_rows(a):
    r, c = a.shape
    n = -(-c // D)
    assert r == 1 or n == 1
    return jnp.pad(a, ((0, 0), (0, n * D - c))).reshape(r * n, D)


def _adamw_replicated(g_rows, w, m, v, name):
    n = len(REP_2D)

    def body(*refs):
        g_ref, w_refs, m_refs, v_refs, outs = refs[0], refs[1:1 + n], refs[1 + n:1 + 2 * n], refs[1 + 2 * n:1 + 3 * n], refs[1 + 3 * n:]
        r0 = 0
        for k, (_, (r, c)) in enumerate(REP_2D):
            pieces = [g_ref[r0 + j * r:r0 + j * r + r, 0:min(D, c - j * D)] for j in range(-(-c // D))]
            g = pieces[0] if len(pieces) == 1 else jnp.concatenate(pieces, axis=1)
            outs[4 * k][...] = g
            outs[4 * k + 1][...], outs[4 * k + 2][...], outs[4 * k + 3][...] = _adamw_math(
                w_refs[k][...], g, m_refs[k][...], v_refs[k][...])
            r0 += _rows_needed((r, c))

    shapes = [s for _, s in REP_2D]
    return pl.pallas_call(
        body, name=name, in_specs=[_full(g_rows.shape)] + [_full(s) for s in shapes] * 3,
        out_specs=[_full(s) for s in shapes for _ in range(4)],
        out_shape=[jax.ShapeDtypeStruct(s, F32) for s in shapes for _ in range(4)],
        grid=(1,), compiler_params=_params("arbitrary"))(g_rows, *w, *m, *v)


def _adamw(w, g, m, v, name):
    r, c = w.shape
    tr = _row_tile(r)

    def body(w_ref, g_ref, m_ref, v_ref, d_ref, mo_ref, vo_ref):
        d_ref[...], mo_ref[...], vo_ref[...] = _adamw_math(w_ref[...], g_ref[...], m_ref[...], v_ref[...])

    spec = pl.BlockSpec((tr, c), lambda i: (i, 0))
    shape = jax.ShapeDtypeStruct((r, c), F32)
    return pl.pallas_call(
        body, name=name, grid=(r // tr,), in_specs=[spec] * 4, out_specs=[spec] * 3, out_shape=[shape] * 3,
        compiler_params=_params("parallel"))(w, g, m, v)


ANY = pl.BlockSpec(memory_space=pl.ANY)


def _all_gather(block, name):
    r, c_dim = block.shape

    def body(x_ref, out_ref, token, send_sems, recv_sems, local_sem):
        token[...] = jnp.zeros_like(token)
        x, y, c = lax.axis_index("x"), lax.axis_index("y"), lax.axis_index("c")
        me, sibling = (x, y, c), (x, y, 1 - c)
        chips = [(1 - x, y), (x, 1 - y), (1 - x, 1 - y)]

        def rows(px, py, pc):
            return out_ref.at[4 * px + 2 * py + pc]

        def copy(k, blk, to, src=None):
            return pltpu.make_async_remote_copy(
                src_ref=rows(*blk) if src is None else src, dst_ref=rows(*blk), send_sem=send_sems.at[k],
                recv_sem=recv_sems.at[k], device_id=to, device_id_type=MESH)

        mine = pltpu.make_async_copy(x_ref, rows(*me), local_sem)
        mine.start()
        first = [copy(0, me, sibling, src=x_ref)]
        first += [copy(1 + j, me, (*chip, c), src=x_ref) for j, chip in enumerate(chips)]
        for cp in first:
            cp.start()
        passed = [copy(4 + j, (*chip, c), sibling) for j, chip in enumerate(chips)]
        for j, chip in enumerate(chips):
            copy(1 + j, (*chip, c), me).wait_recv()
            passed[j].start()
        copy(0, sibling, me).wait_recv()
        for j, chip in enumerate(chips):
            copy(4 + j, (*chip, 1 - c), me).wait_recv()
        for cp in first + passed:
            cp.wait_send()
        mine.wait()

    return pl.pallas_call(
        body, name=name, in_specs=[ANY], out_specs=[ANY, pl.BlockSpec(memory_space=pltpu.VMEM)],
        out_shape=[jax.ShapeDtypeStruct((N_DEV, r, c_dim), block.dtype), jax.ShapeDtypeStruct((8, 128), F32)],
        scratch_shapes=[pltpu.SemaphoreType.DMA((7,)), pltpu.SemaphoreType.DMA((7,)), pltpu.SemaphoreType.DMA],
    )(block)


HBM = pl.BlockSpec(memory_space=pltpu.HBM)
SEM = pl.BlockSpec(memory_space=pltpu.SEMAPHORE)
EFFECT = pltpu.SideEffectType.DATAFLOW_SIDE_EFFECTING


def _exchange_copies(scatter, src_refs, land_refs, send_sems, recv_sems, local_sems):
    x, y, c = lax.axis_index("x"), lax.axis_index("y"), lax.axis_index("c")
    me = 4 * x + 2 * y + c
    sends, arrivals, locals_ = [], [], []
    for a, (src, land) in enumerate(zip(src_refs, land_refs)):
        def pick(idx, src=src):
            return src.at[idx] if scatter else src

        locals_.append(pltpu.make_async_copy(pick(me), land.at[me], local_sems.at[a]))
        for r in range(1, N_DEV):
            px = 1 - x if r & 4 else x
            py = 1 - y if r & 2 else y
            pc = 1 - c if r & 1 else c
            peer, s = 4 * px + 2 * py + pc, 7 * a + r - 1
            sends.append(pltpu.make_async_remote_copy(
                src_ref=pick(peer), dst_ref=land.at[me], send_sem=send_sems.at[s], recv_sem=recv_sems.at[s],
                device_id=(px, py, pc), device_id_type=MESH))
            arrivals.append(pltpu.make_async_remote_copy(
                src_ref=pick(peer), dst_ref=land.at[peer], send_sem=send_sems.at[s], recv_sem=recv_sems.at[s],
                device_id=(px, py, pc), device_id_type=MESH))
    return sends, arrivals, locals_


def _exchange_start(srcs, scatter, name):
    n = len(srcs)
    lands = [lax.empty((N_DEV,) + s.shape[-2:], s.dtype) for s in srcs]

    def body(*refs):
        src_refs, land_refs = refs[:n], refs[n:2 * n]
        send_sems, recv_sems, local_sems = refs[2 * n:2 * n + 3]
        token = refs[-1]
        sends, _, locals_ = _exchange_copies(scatter, src_refs, land_refs, send_sems, recv_sems, local_sems)
        for cp in locals_ + sends:
            cp.start()
        token[...] = jnp.zeros_like(token)

    res = pl.pallas_call(
        body, name=name,
        out_shape=[pltpu.SemaphoreType.DMA((7 * n,)), pltpu.SemaphoreType.DMA((7 * n,)), pltpu.SemaphoreType.DMA((n,))]
        + [pltpu.HBM(a.shape, a.dtype) for a in list(srcs) + lands] + [jax.ShapeDtypeStruct((8, 128), F32)],
        in_specs=[HBM] * (2 * n), out_specs=[SEM] * 3 + [HBM] * (2 * n) + [pl.BlockSpec(memory_space=pltpu.VMEM)],
        input_output_aliases={i: 3 + i for i in range(2 * n)},
        compiler_params=pltpu.CompilerParams(has_side_effects=EFFECT),
    )(*[pltpu.with_memory_space_constraint(a, pltpu.HBM) for a in list(srcs) + lands])
    return (n, scatter, res[:3], res[3:3 + 2 * n]), res[-1]


def _exchange_wait(handle, after, name):
    n, scatter, sems, thru = handle

    def body(*refs):
        src_refs, land_refs = refs[:n], refs[n:2 * n]
        send_sems, recv_sems, local_sems = refs[2 * n:2 * n + 3]
        sends, arrivals, locals_ = _exchange_copies(scatter, src_refs, land_refs, send_sems, recv_sems, local_sems)
        for cp in arrivals:
            cp.wait_recv()
        for cp in sends:
            cp.wait_send()
        for cp in locals_:
            cp.wait()

    res = pl.pallas_call(
        body, name=name, out_shape=[pltpu.HBM(a.shape, a.dtype) for a in thru],
        in_specs=[HBM] * (2 * n) + [SEM] * 3 + [ANY], out_specs=[HBM] * (2 * n),
        input_output_aliases={i: i for i in range(2 * n)},
        compiler_params=pltpu.CompilerParams(has_side_effects=EFFECT),
    )(*thru, *sems, after)
    return res[n:]


def _behind(tokens, a):
    zero = sum(tok[0, 0] for tok in tokens)
    return jax.tree.map(lambda v: v + zero.astype(v.dtype), a)


def _perm_heads(a, perm, axis):
    idx = [slice(None)] * a.ndim
    parts = []
    for h in perm:
        idx[axis] = slice(64 * h, 64 * (h + 1))
        parts.append(a[tuple(idx)])
    idx[axis] = slice(512, None)
    if a.shape[axis] > 512:
        parts.append(a[tuple(idx)])
    return jnp.concatenate(parts, axis=axis)


Q_INV = tuple(int(i) for i in np.argsort(Q_PERM))


def _in0_to_kernel(a, axis):
    a = _perm_heads(a, Q_PERM, axis)
    idx = [slice(None)] * a.ndim

    def cut(lo, hi):
        idx[axis] = slice(lo, hi)
        return a[tuple(idx)]

    return jnp.concatenate([cut(0, 512), cut(768, 1792), cut(512, 768)], axis=axis)


def _in0_from_kernel(a, axis):
    idx = [slice(None)] * a.ndim

    def cut(lo, hi):
        idx[axis] = slice(lo, hi)
        return a[tuple(idx)]

    a = jnp.concatenate([cut(0, 512), cut(1536, 1792), cut(512, 1536)], axis=axis)
    return _perm_heads(a, Q_INV, axis)


def _f32_as_u16_rows(v, rows):
    bits = lax.bitcast_convert_type(v, jnp.uint16).reshape(-1)
    return jnp.pad(bits, (0, rows * D - bits.shape[0])).reshape(rows, D)


def _pad_rows(v, rows):
    v = v.reshape(-1)
    return jnp.pad(v, (0, rows * D - v.shape[0])).reshape(rows, D)


def kernel(x, mix_norm, a_w_in, a_b_in, a_sinks, a_conv_w, a_conv_b, a_cln_g, a_cln_b, a_w_out, c_w_in, c_w_pool, c_pool_scale, c_sln_g, c_sln_b, c_w_s, c_b_s, c_w_out, ffn_norm, ffn_w_gate, ffn_w_up, ffn_w_down, final_norm, loss_target, m_mix_norm, m_a_w_in, m_a_b_in, m_a_sinks, m_a_conv_w, m_a_conv_b, m_a_cln_g, m_a_cln_b, m_a_w_out, m_c_w_in, m_c_w_pool, m_c_pool_scale, m_c_sln_g, m_c_sln_b, m_c_w_s, m_c_b_s, m_c_w_out, m_ffn_norm, m_ffn_w_gate, m_ffn_w_up, m_ffn_w_down, m_final_norm, v_mix_norm, v_a_w_in, v_a_b_in, v_a_sinks, v_a_conv_w, v_a_conv_b, v_a_cln_g, v_a_cln_b, v_a_w_out, v_c_w_in, v_c_w_pool, v_c_pool_scale, v_c_sln_g, v_c_sln_b, v_c_w_s, v_c_b_s, v_c_w_out, v_ffn_norm, v_ffn_w_gate, v_ffn_w_up, v_ffn_w_down, v_final_norm):
    bsz, seq, _ = x.shape
    t = bsz * seq
    w_in = dict(mix_norm=mix_norm, a_w_in=a_w_in, a_b_in=a_b_in, a_sinks=a_sinks, a_conv_w=a_conv_w, a_conv_b=a_conv_b,
                a_cln_g=a_cln_g, a_cln_b=a_cln_b, a_w_out=a_w_out, c_w_in=c_w_in, c_w_pool=c_w_pool,
                c_pool_scale=c_pool_scale, c_sln_g=c_sln_g, c_sln_b=c_sln_b, c_w_s=c_w_s, c_b_s=c_b_s, c_w_out=c_w_out,
                ffn_norm=ffn_norm, ffn_w_gate=ffn_w_gate, ffn_w_up=ffn_w_up, ffn_w_down=ffn_w_down, final_norm=final_norm)
    m_in = dict(mix_norm=m_mix_norm, a_w_in=m_a_w_in, a_b_in=m_a_b_in, a_sinks=m_a_sinks, a_conv_w=m_a_conv_w,
                a_conv_b=m_a_conv_b, a_cln_g=m_a_cln_g, a_cln_b=m_a_cln_b, a_w_out=m_a_w_out, c_w_in=m_c_w_in,
                c_w_pool=m_c_w_pool, c_pool_scale=m_c_pool_scale, c_sln_g=m_c_sln_g, c_sln_b=m_c_sln_b, c_w_s=m_c_w_s,
                c_b_s=m_c_b_s, c_w_out=m_c_w_out, ffn_norm=m_ffn_norm, ffn_w_gate=m_ffn_w_gate, ffn_w_up=m_ffn_w_up,
                ffn_w_down=m_ffn_w_down, final_norm=m_final_norm)
    v_in = dict(mix_norm=v_mix_norm, a_w_in=v_a_w_in, a_b_in=v_a_b_in, a_sinks=v_a_sinks, a_conv_w=v_a_conv_w,
                a_conv_b=v_a_conv_b, a_cln_g=v_a_cln_g, a_cln_b=v_a_cln_b, a_w_out=v_a_w_out, c_w_in=v_c_w_in,
                c_w_pool=v_c_w_pool, c_pool_scale=v_c_pool_scale, c_sln_g=v_c_sln_g, c_sln_b=v_c_sln_b, c_w_s=v_c_w_s,
                c_b_s=v_c_b_s, c_w_out=v_c_w_out, ffn_norm=v_ffn_norm, ffn_w_gate=v_ffn_w_gate, ffn_w_up=v_ffn_w_up,
                ffn_w_down=v_ffn_w_down, final_norm=v_final_norm)

    small = jnp.concatenate([a_conv_w[0].reshape(-1), c_pool_scale[0], c_sln_g[0], c_sln_b[0]])
    first_bits = lax.bitcast_convert_type(a_w_in[0].T.astype(BF16), jnp.uint16)
    gathered, tok = _all_gather(jnp.concatenate([first_bits, _f32_as_u16_rows(small, W_MISC_ROWS)], axis=0), "gather_mixer0")

    def ffn_shards(l):
        return [ffn_w_gate[l].T.astype(BF16), ffn_w_up[l].T.astype(BF16), ffn_w_down[l].astype(BF16)]

    ffn0_h, tok = _exchange_start(_behind([tok], ffn_shards(0) + [a_w_out[0].astype(BF16)]), False, "gather_ffn0_start")
    mix1_h, tok = _exchange_start(_behind([tok], [c_w_in[0].T.astype(BF16), c_w_out[0].astype(BF16)]), False,
                                  "gather_mixer1_start")
    ffn1_h, tok = _exchange_start(_behind([tok], ffn_shards(1)), False, "gather_ffn1_start")

    a_in_full = lax.bitcast_convert_type(gathered[:, :224].reshape(IN0, D), BF16)
    small_all = lax.bitcast_convert_type(
        gathered[:, 224:].reshape(N_DEV, -1)[:, :2 * SMALL_SHARD].reshape(N_DEV, SMALL_SHARD, 2), F32)
    conv_w = small_all[:, :31 * 64].reshape(N_DEV, 31, 64).transpose(1, 0, 2).reshape(31, 512)
    conv_w = jnp.pad(conv_w, ((0, HALO - CONV_K), (0, 0)))
    pool_scale = small_all[:, 31 * 64:31 * 64 + 64].reshape(1, 512)
    sln_g = small_all[:, 31 * 64 + 64:31 * 64 + 128].reshape(1, 512)
    sln_b = small_all[:, 31 * 64 + 128:].reshape(1, 512)

    wt_in0 = _in0_to_kernel(a_in_full, 0)
    b_in0 = _in0_to_kernel(a_b_in, 1)
    b_rows = jnp.broadcast_to(c_b_s[0][:, :, None], (4, 128, 128))
    conv_b, cln_g, cln_b = a_conv_b, a_cln_g, a_cln_b

    h0 = x.reshape(t, D)
    target = loss_target.reshape(t, D)
    z0, hn0 = _norm_proj(h0, _behind([tok], mix_norm[0:1]), wt_in0, b_in0, "in_proj0")
    attn, tok = _attn_fwd(z0, a_sinks, bsz, "attn_fwd")
    conv, conv_y = _conv_fwd(z0, conv_w, conv_b, cln_g, _behind([tok], cln_b), bsz, "conv_fwd")
    wtg0, wtu0, wd0, a_out_full = (w.reshape(-1, D) for w in _exchange_wait(ffn0_h, conv, "gather_ffn0_wait"))
    w_out0 = _perm_heads(a_out_full, Q_PERM, 0)
    h1, hnf0, gate0a, up0a, part = _ffn_fwd_a(h0, attn, conv, w_out0, ffn_norm[0:1], wtg0, wtu0, wd0, "ffn_fwd0a")
    gate0b, up0b, h2 = _ffn_fwd_b(hnf0, part, wtg0, wtu0, wd0, "ffn_fwd0b")
    wt_in1, w_out1 = (w.reshape(-1, D) for w in _exchange_wait(mix1_h, h2, "gather_mixer1_wait"))
    z1, hn1 = _norm_proj(h2, mix_norm[1:2], wt_in1, None, "in_proj1")
    pool, tok = _pool_fwd(z1, c_w_pool[0], pool_scale, bsz, "pool_fwd")
    sgu = _sgu_fwd(z1, sln_g, _behind([tok], sln_b), c_w_s[0], b_rows, "sgu_fwd")
    wtg1, wtu1, wd1 = (w.reshape(D_FF, D) for w in _exchange_wait(ffn1_h, sgu, "gather_ffn1_wait"))
    h3, hnf1, gate1a, up1a, part = _ffn_fwd_a(h2, pool, sgu, w_out1, ffn_norm[1:2], wtg1, wtu1, wd1, "ffn_fwd1a")
    gate1b, up1b, dh4, d_final_norm, loss_part = _ffn_fwd_b(hnf1, part, wtg1, wtu1, wd1, "ffn_fwd1b",
                                                            head=(final_norm.reshape(1, D), target))

    def blocks(g):
        return g.reshape(N_DEV, g.shape[0] // N_DEV, D)

    *wide, part = _ffn_bwd_a(dh4, gate1a, up1a, wtg1, wtu1, wd1, "ffn_bwd1a")
    dh3, dmix1, dgate1, dup1, act1, d_fn1 = _ffn_bwd_b(dh4, h3, ffn_norm[1:2], gate1b, up1b, wtg1, wtu1, wd1, w_out1, part,
                                                       wide, "ffn_bwd1b")
    gw_ffn1 = [_mm_tn(dgate1, hnf1, "dw_gate1"), _mm_tn(dup1, hnf1, "dw_up1"), _mm_tn(act1, dh4, "dw_down1")]
    ffn1_g, tok = _exchange_start([blocks(g) for g in gw_ffn1], True, "scatter_ffn1_start")
    gw_c_out = _mm_tn_pieces([pool, sgu], dh3, "dw_out1")
    dzp, d_w_pool, d_pool_scale = _pool_bwd(z1, dmix1, c_w_pool[0], _behind([tok], pool_scale), bsz, "pool_bwd")
    dzu, dzv, d_w_s, d_b_s, d_sln_g, d_sln_b = _sgu_bwd(z1, dmix1, sln_g, sln_b, c_w_s[0], b_rows, "sgu_bwd")
    dh2, d_mn1 = _proj_bwd_norm([(dzp, 0), (dzu, 512), (dzv, 1024)], wt_in1, h2, dh3, mix_norm[1:2], BF16, "in_proj1_bwd")
    gw_c_in = _mm_tn_pieces([dzp, dzu, dzv], hn1, "dw_in1")
    mix1_g, tok = _exchange_start([blocks(gw_c_in), blocks(gw_c_out)], True, "scatter_mixer1_start")
    *wide, part = _ffn_bwd_a(dh2, gate0a, up0a, wtg0, wtu0, wd0, "ffn_bwd0a")
    dh1, dmix0, dgate0, dup0, act0, d_fn0 = _ffn_bwd_b(dh2, h1, _behind([tok], ffn_norm[0:1]), gate0b, up0b, wtg0, wtu0, wd0,
                                                       w_out0, part, wide, "ffn_bwd0b")
    gw_ffn0 = [_mm_tn(dgate0, hnf0, "dw_gate0"), _mm_tn(dup0, hnf0, "dw_up0"), _mm_tn(act0, dh2, "dw_down0")]
    ffn0_g, tok = _exchange_start([blocks(g) for g in gw_ffn0], True, "scatter_ffn0_start")
    gw_a_out = _perm_heads(_mm_tn_pieces([attn, conv], dh1, "dw_out0"), Q_INV, 0)
    dq, dkv, d_sink_row, d_bq, d_bkv = _attn_bwd(z0, dmix0, _behind([tok], a_sinks), bsz, "attn_bwd")
    dca, dcg, d_conv_w, d_conv_b, d_cln_g, d_cln_b, d_ba, d_bg = _conv_bwd(z0, conv_y, dmix0, conv_w, cln_g, cln_b, bsz, "conv_bwd")
    gw_a_in = _in0_from_kernel(_mm_tn_pieces([dq, dca, dcg, dkv], hn0, "dw_in0"), 0)
    mix0_g, tok = _exchange_start([blocks(gw_a_in), blocks(gw_a_out)], True, "scatter_mixer0_start")
    dx, d_mn0 = _proj_bwd_norm([(dq, 0), (dca, 512), (dcg, 1024), (dkv, 1536)], wt_in0, h0, dh1,
                               _behind([tok], mix_norm[0:1]), F32, "in_proj0_bwd")
    d_b_in = _in0_from_kernel(jnp.concatenate([d_bq, d_ba, d_bg, d_bkv], axis=1), 1)

    rep = dict(mix_norm=jnp.concatenate([d_mn0, d_mn1], axis=0), a_b_in=d_b_in, a_sinks=d_sink_row[:, :8],
               a_conv_b=d_conv_b, a_cln_g=d_cln_g, a_cln_b=d_cln_b, c_w_pool=d_w_pool.reshape(64, D),
               c_w_s=d_w_s.reshape(64, D), c_b_s=d_b_s, ffn_norm=jnp.concatenate([d_fn0, d_fn1], axis=0),
               final_norm=d_final_norm)
    rep_rows = jnp.concatenate([_as_rows(rep[nm]) for nm, _ in REP_2D] + [_as_rows(loss_part)], axis=0)
    rep_flat = jnp.pad(rep_rows, ((0, N_DEV * REP_ROWS - rep_rows.shape[0]), (0, 0))).reshape(N_DEV, REP_ROWS, D)
    small_g = jnp.concatenate([
        d_conv_w[:CONV_K].reshape(31, N_DEV, 64).transpose(1, 0, 2).reshape(N_DEV, 31 * 64),
        d_pool_scale.reshape(N_DEV, 64), d_sln_g.reshape(N_DEV, 64), d_sln_b.reshape(N_DEV, 64)], axis=1)
    small_g = jnp.pad(small_g, ((0, 0), (0, G_SMALL_ROWS * D - SMALL_SHARD))).reshape(N_DEV, G_SMALL_ROWS, D)
    tail_g, tok = _exchange_start([jnp.concatenate([small_g, rep_flat], axis=1)], True, "scatter_tail_start")

    names = list(w_in)
    g_out, delta, new_m, new_v = {}, {}, {}, {}
    column_sharded = ("a_w_in", "c_w_in", "ffn_w_gate", "ffn_w_up")

    def rows_of(a, nm):
        return jnp.swapaxes(a, 1, 2) if nm in column_sharded else a

    def reduce_adamw(nm, landing, layer, into=None):
        args = [rows_of(d[nm], nm) for d in (w_in, m_in, v_in)]
        if args[0].shape[0] == 1:
            args, layer = [a[0] for a in args], None
        return _reduce_adamw(landing, *args, "adamw_%s_%s" % (nm, layer), layer=layer, into=into)

    def keep(nm, res):
        res = [r if r.ndim == 3 else r[None] for r in res]
        g_out[nm], delta[nm], new_m[nm], new_v[nm] = (rows_of(r, nm) for r in res)

    ffn_names = ("ffn_w_gate", "ffn_w_up", "ffn_w_down")
    landed = _exchange_wait(ffn1_g, tok, "scatter_ffn1_wait")
    ffn_res = [reduce_adamw(nm, a, 1) for nm, a in zip(ffn_names, landed)]
    landed = _exchange_wait(mix1_g, ffn_res[-1][0], "scatter_mixer1_wait")
    for nm, a in zip(("c_w_in", "c_w_out"), landed):
        keep(nm, reduce_adamw(nm, a, 0))
    landed = _exchange_wait(ffn0_g, g_out["c_w_out"], "scatter_ffn0_wait")
    for nm, a, res in zip(ffn_names, landed, ffn_res):
        keep(nm, reduce_adamw(nm, a, 0, into=res))
    landed = _exchange_wait(mix0_g, g_out["ffn_w_down"], "scatter_mixer0_wait")
    for nm, a in zip(("a_w_in", "a_w_out"), landed):
        keep(nm, reduce_adamw(nm, a, 0))
    g_tail = _sum_slabs(_exchange_wait(tail_g, g_out["a_w_out"], "scatter_tail_wait")[0], "sum_tail")
    rep_all = _all_gather(g_tail[G_SMALL_ROWS:], "gather_replicated_grads")[0].reshape(N_DEV * REP_ROWS, D)
    loss = rep_all[sum(_rows_needed(s) for _, s in REP_2D), 0]
    res = _adamw_replicated(rep_all, *[[d[nm].reshape(s) for nm, s in REP_2D] for d in (w_in, m_in, v_in)], "adamw_replicated")
    for k, (nm, _) in enumerate(REP_2D):
        g_out[nm], delta[nm], new_m[nm], new_v[nm] = (r.reshape(w_in[nm].shape) for r in res[4 * k:4 * k + 4])
    small_r = g_tail[:G_SMALL_ROWS].reshape(-1)[:SMALL_SHARD]
    g_out.update(
        a_conv_w=small_r[:31 * 64].reshape(1, 31, 64), c_pool_scale=small_r[31 * 64:31 * 64 + 64].reshape(1, 64),
        c_sln_g=small_r[31 * 64 + 64:31 * 64 + 128].reshape(1, 64), c_sln_b=small_r[31 * 64 + 128:].reshape(1, 64))
    group = ("a_conv_w", "c_pool_scale", "c_sln_g", "c_sln_b")
    flat = [_pad_rows(jnp.concatenate([d[nm].reshape(-1) for nm in group]), G_SMALL_ROWS) for d in (w_in, g_out, m_in, v_in)]
    res = [r.reshape(-1) for r in _adamw(*flat, "adamw_small_sharded")]
    off = 0
    for nm in group:
        n = int(np.prod(w_in[nm].shape))
        delta[nm], new_m[nm], new_v[nm] = (r[off:off + n].reshape(w_in[nm].shape) for r in res)
        off += n

    grad_x = dx.reshape(bsz, seq, D)
    return (loss, grad_x, *[g_out[nm] for nm in names], *[delta[nm] for nm in names],
            *[new_m[nm] for nm in names], *[new_v[nm] for nm in names])
```
